```python
import math
import jax, jax.numpy as jnp
from jax import lax
import numpy as np


D_MODEL = 1024
BATCH = 2
SEQ = 8192
DEPTH = 1

PLE_DIM = 256
ATTN_HEADS = 8
HEAD_DIM = 64
ATTN_WIDTH = ATTN_HEADS * HEAD_DIM
SSM_WIDTH = D_MODEL - ATTN_WIDTH
SSM_GROUP_CH = 16
SSM_GROUPS = SSM_WIDTH // SSM_GROUP_CH
SSM_STATE = 64
MIX_WIDTH = ATTN_WIDTH + SSM_WIDTH
IN_COLS = 3 * ATTN_WIDTH + ATTN_HEADS + SSM_WIDTH
D_FF = 2816
Q_BLOCK = 128
EPS = 1e-6

kernel_name = 'hybrid_fox_s5_macaron_ple'


def rmsnorm(x, g):
    xf = x.astype(jnp.float32)
    y = xf * lax.rsqrt(jnp.mean(xf * xf, axis=-1, keepdims=True) + EPS)
    return (y * g.astype(jnp.float32)).astype(x.dtype)


def swiglu(x, w1, w3, w2):
    return (jax.nn.silu(x @ w1) * (x @ w3)) @ w2


def forgetting_attention(q, k, v, log_f):
    B, L, H, hd = q.shape
    scale = 1.0 / math.sqrt(hd)
    q = q.transpose(0, 2, 1, 3)
    k = k.transpose(0, 2, 1, 3)
    v = v.transpose(0, 2, 1, 3)
    c = jnp.cumsum(log_f, axis=1).transpose(0, 2, 1)
    kpos = jnp.arange(L)
    n_blocks = L // Q_BLOCK

    def block(i):
        s0 = i * Q_BLOCK
        qb = lax.dynamic_slice_in_dim(q, s0, Q_BLOCK, axis=2)
        cb = lax.dynamic_slice_in_dim(c, s0, Q_BLOCK, axis=2)
        logits = (jnp.einsum('bhqd,bhkd->bhqk', qb, k).astype(jnp.float32) * scale
                  + cb[..., :, None] - c[..., None, :])
        qpos = s0 + jnp.arange(Q_BLOCK)
        mask = kpos[None, :] <= qpos[:, None]
        w = jax.nn.softmax(jnp.where(mask, logits, -jnp.inf), axis=-1)
        return jnp.einsum('bhqk,bhkd->bhqd', w.astype(v.dtype), v)

    out = lax.map(block, jnp.arange(n_blocks))
    return out.transpose(1, 0, 3, 2, 4).reshape(B, L, H * hd)


def _ssm_combine(e1, e2):
    a1r, a1i, b1r, b1i = e1
    a2r, a2i, b2r, b2i = e2
    ar = a2r * a1r - a2i * a1i
    ai = a2r * a1i + a2i * a1r
    br = a2r * b1r - a2i * b1i + b2r
    bi = a2r * b1i + a2i * b1r + b2i
    return ar, ai, br, bi


def s5_mixer(s, a_re, a_im, log_dt, b_re, b_im, c_re, c_im, d_skip, w_glu, b_glu):
    B, L, _ = s.shape
    f32 = jnp.float32
    u = s.astype(f32).reshape(B, L, SSM_GROUPS, SSM_GROUP_CH)
    ar, ai = a_re.astype(f32), a_im.astype(f32)
    dt = jnp.exp(log_dt.astype(f32))[:, None]
    decay = jnp.exp(dt * ar)
    abar_r = decay * jnp.cos(dt * ai)
    abar_i = decay * jnp.sin(dt * ai)
    nr, ni = abar_r - 1.0, abar_i
    den = ar * ar + ai * ai
    fr = (nr * ar + ni * ai) / den
    fi = (ni * ar - nr * ai) / den
    br, bi = b_re.astype(f32), b_im.astype(f32)
    bbar_r = fr[..., None] * br - fi[..., None] * bi
    bbar_i = fr[..., None] * bi + fi[..., None] * br
    bu_r = jnp.einsum('blgh,gph->blgp', u, bbar_r)
    bu_i = jnp.einsum('blgh,gph->blgp', u, bbar_i)
    a_r_full = jnp.broadcast_to(abar_r, bu_r.shape)
    a_i_full = jnp.broadcast_to(abar_i, bu_i.shape)
    _, _, xr, xi = lax.associative_scan(_ssm_combine, (a_r_full, a_i_full, bu_r, bu_i), axis=1)
    y = (jnp.einsum('blgp,ghp->blgh', xr, c_re.astype(f32))
         - jnp.einsum('blgp,ghp->blgh', xi, c_im.astype(f32))
         + d_skip.astype(f32) * u)
    y = jax.nn.gelu(y.reshape(B, L, SSM_WIDTH)).astype(s.dtype)
    return y * jax.nn.sigmoid(y @ w_glu + b_glu)


def setup_inputs(seed: int = 0) -> dict:
    key = jax.random.key(seed)
    ks = iter(jax.random.split(key, 40))
    nrm = lambda shape, scale: jax.random.normal(next(ks), shape, jnp.float32) * scale
    gain = lambda shape: 1.0 + nrm(shape, 0.05)
    Dp, D, F = DEPTH, D_MODEL, D_FF
    G, P, Hc = SSM_GROUPS, SSM_STATE, SSM_GROUP_CH
    inp = {}
    inp['x'] = nrm((BATCH, SEQ, D), 1.0)
    inp['p'] = nrm((DEPTH, BATCH, SEQ, PLE_DIM), 1.0)
    inp['g_ffn1'] = gain((Dp, D))
    inp['w1_a'] = nrm((Dp, D, F), D ** -0.5)
    inp['w3_a'] = nrm((Dp, D, F), D ** -0.5)
    inp['w2_a'] = nrm((Dp, F, D), F ** -0.5)
    inp['g_mix'] = gain((Dp, D))
    inp['w_in'] = nrm((Dp, D, IN_COLS), D ** -0.5)
    inp['b_f'] = jnp.linspace(1.0, 5.0, ATTN_HEADS)[None, :] + nrm((Dp, ATTN_HEADS), 0.1)
    inp['a_re'] = -0.5 + nrm((Dp, G, P), 0.01)
    inp['a_im'] = jnp.pi * jnp.arange(P, dtype=jnp.float32)[None, None, :] + nrm((Dp, G, P), 0.01)
    inp['log_dt'] = jax.random.uniform(next(ks), (Dp, G), jnp.float32, math.log(1e-3), math.log(1e-1))
    inp['b_re'] = nrm((Dp, G, P, Hc), (2.0 * Hc) ** -0.5)
    inp['b_im'] = nrm((Dp, G, P, Hc), (2.0 * Hc) ** -0.5)
    inp['c_re'] = nrm((Dp, G, Hc, P), (2.0 * P) ** -0.5)
    inp['c_im'] = nrm((Dp, G, Hc, P), (2.0 * P) ** -0.5)
    inp['d_skip'] = nrm((Dp, G, Hc), 1.0)
    inp['w_glu'] = nrm((Dp, SSM_WIDTH, SSM_WIDTH), SSM_WIDTH ** -0.5)
    inp['b_glu'] = nrm((Dp, SSM_WIDTH), 0.02)
    inp['g_attn_out'] = gain((Dp, ATTN_WIDTH))
    inp['g_ssm_out'] = gain((Dp, SSM_WIDTH))
    inp['w_out'] = nrm((Dp, MIX_WIDTH, D), MIX_WIDTH ** -0.5)
    inp['g_ffn2'] = gain((Dp, D))
    inp['w1_b'] = nrm((Dp, D, F), D ** -0.5)
    inp['w3_b'] = nrm((Dp, D, F), D ** -0.5)
    inp['w2_b'] = nrm((Dp, F, D), F ** -0.5)
    inp['g_ple'] = gain((Dp, D))
    inp['w_ple_gate'] = nrm((Dp, D, D), D ** -0.5)
    inp['w_ple_proj'] = nrm((Dp, PLE_DIM, D), PLE_DIM ** -0.5)
    inp['g_final'] = gain((D,))
    return inp


def reference(x, p, g_ffn1, w1_a, w3_a, w2_a, g_mix, w_in, b_f, a_re, a_im, log_dt,
              b_re, b_im, c_re, c_im, d_skip, w_glu, b_glu, g_attn_out, g_ssm_out, w_out,
              g_ffn2, w1_b, w3_b, w2_b, g_ple, w_ple_gate, w_ple_proj, g_final):
    B, L, _ = x.shape
    h = x
    s_q, s_k, s_v, s_f = ATTN_WIDTH, 2 * ATTN_WIDTH, 3 * ATTN_WIDTH, 3 * ATTN_WIDTH + ATTN_HEADS
    for i in range(DEPTH):
        h = h + 0.5 * swiglu(rmsnorm(h, g_ffn1[i]), w1_a[i], w3_a[i], w2_a[i])
        u = rmsnorm(h, g_mix[i])
        z = u @ w_in[i]
        q = z[..., :s_q].reshape(B, L, ATTN_HEADS, HEAD_DIM)
        k = z[..., s_q:s_k].reshape(B, L, ATTN_HEADS, HEAD_DIM)
        v = z[..., s_k:s_v].reshape(B, L, ATTN_HEADS, HEAD_DIM)
        log_f = jax.nn.log_sigmoid(z[..., s_v:s_f].astype(jnp.float32) + b_f[i].astype(jnp.float32))
        s_in = z[..., s_f:]
        attn = forgetting_attention(q, k, v, log_f)
        ssm = s5_mixer(s_in, a_re[i], a_im[i], log_dt[i], b_re[i], b_im[i], c_re[i], c_im[i],
                       d_skip[i], w_glu[i], b_glu[i])
        mixed = jnp.concatenate([rmsnorm(attn, g_attn_out[i]), rmsnorm(ssm, g_ssm_out[i])], axis=-1)
        h = h + mixed @ w_out[i]
        h = h + 0.5 * swiglu(rmsnorm(h, g_ffn2[i]), w1_b[i], w3_b[i], w2_b[i])
        gate = jax.nn.sigmoid(rmsnorm(h, g_ple[i]) @ w_ple_gate[i])
        h = h + gate * (p[i] @ w_ple_proj[i])
    return rmsnorm(h, g_final)
```

```python
import functools
import math

import jax
import jax.numpy as jnp
from jax import lax
from jax.experimental import pallas as pl
from jax.experimental.pallas import tpu as pltpu

D_MODEL = 1024
ATTN_HEADS = 8
HEAD_DIM = 64
ATTN_WIDTH = ATTN_HEADS * HEAD_DIM
SSM_WIDTH = D_MODEL - ATTN_WIDTH
SSM_GROUP_CH = 16
SSM_GROUPS = SSM_WIDTH // SSM_GROUP_CH
SSM_STATE = 64
D_FF = 2816
PLE_DIM = 256
EPS = 1e-6

LANES = 128
HEAD_PAIR = 2 * HEAD_DIM
N_PAIRS = ATTN_HEADS // 2
FF_CHUNK = 256
TOKEN_TILE = 512
ATTN_TILE = 512
SSM_CHUNK = 32
NEG_BIG = -1e30
VMEM_LIMIT = 56 * 1024 * 1024

BF16 = jnp.bfloat16
F32 = jnp.float32


def _rms(x, g):
    ms = jnp.mean(x * x, axis=-1, keepdims=True)
    return x * lax.rsqrt(ms + EPS) * g


def _sigmoid(x):
    return 1.0 / (1.0 + jnp.exp(-x))


def _swiglu(xn, w1_ref, w3_ref, w2_ref):
    acc = None
    for c in range(D_FF // FF_CHUNK):
        sl = slice(c * FF_CHUNK, (c + 1) * FF_CHUNK)
        a = jnp.dot(xn, w1_ref[:, sl], preferred_element_type=F32)
        b = jnp.dot(xn, w3_ref[:, sl], preferred_element_type=F32)
        gated = (a * _sigmoid(a) * b).astype(BF16)
        part = jnp.dot(gated, w2_ref[sl, :], preferred_element_type=F32)
        acc = part if acc is None else acc + part
    return acc


def _const_spec(shape):
    nd = len(shape)
    return pl.BlockSpec(shape, lambda *_: (0,) * nd, pipeline_mode=pl.Buffered(1))


def _head_kernel(x_ref, g1_ref, w1_ref, w3_ref, w2_ref, gm_ref, wqkv_ref, ws_ref,
                 wf_ref, bf_ref, h1_ref, q_ref, k_ref, v_ref, s_ref, ct_ref,
                 carry_ref):
    tm = x_ref.shape[0]
    x = x_ref[...]
    h1 = x + 0.5 * _swiglu(_rms(x, g1_ref[...]).astype(BF16), w1_ref, w3_ref, w2_ref)
    h1_ref[...] = h1
    un = _rms(h1, gm_ref[...]).astype(BF16)
    qkv = jnp.dot(un, wqkv_ref[...], preferred_element_type=F32)
    q_ref[...] = qkv[:, :ATTN_WIDTH].astype(BF16)
    k_ref[...] = qkv[:, ATTN_WIDTH:2 * ATTN_WIDTH].astype(BF16)
    v_ref[...] = qkv[:, 2 * ATTN_WIDTH:].astype(BF16)
    s_ref[...] = jnp.dot(un, ws_ref[...], preferred_element_type=F32).astype(BF16)

    zf = jnp.dot(un, wf_ref[...], preferred_element_type=F32)
    zft = zf.T[:ATTN_HEADS, :] + bf_ref[...]
    logf = jnp.minimum(zft, 0.0) - jnp.log1p(jnp.exp(-jnp.abs(zft)))
    lane = lax.broadcasted_iota(jnp.int32, logf.shape, 1)
    c = logf
    shift = 1
    while shift < tm:
        c = c + jnp.where(lane >= shift, pltpu.roll(c, shift, 1), 0.0)
        shift *= 2

    @pl.when(pl.program_id(1) == 0)
    def _():
        carry_ref[...] = jnp.zeros_like(carry_ref)

    c = c + carry_ref[:, 0:1]
    ct_ref[...] = c
    carry_ref[...] = jnp.broadcast_to(c[:, tm - 1:tm], carry_ref.shape)


def _head_call(x, g1, w1, w3, w2, gm, wqkv, ws, wf, bf):
    B, L, D = x.shape
    tm = TOKEN_TILE
    tile = lambda w: pl.BlockSpec((None, tm, w), lambda b, i: (b, i, 0))
    out_shape = (
        jax.ShapeDtypeStruct((B, L, D), F32),
        jax.ShapeDtypeStruct((B, L, ATTN_WIDTH), BF16),
        jax.ShapeDtypeStruct((B, L, ATTN_WIDTH), BF16),
        jax.ShapeDtypeStruct((B, L, ATTN_WIDTH), BF16),
        jax.ShapeDtypeStruct((B, L, SSM_WIDTH), BF16),
        jax.ShapeDtypeStruct((B, ATTN_HEADS, L), F32),
    )
    return pl.pallas_call(
        _head_kernel,
        grid=(B, L // tm),
        in_specs=[tile(D), _const_spec(g1.shape), _const_spec(w1.shape),
                  _const_spec(w3.shape), _const_spec(w2.shape), _const_spec(gm.shape),
                  _const_spec(wqkv.shape), _const_spec(ws.shape), _const_spec(wf.shape),
                  _const_spec(bf.shape)],
        out_specs=(tile(D), tile(ATTN_WIDTH), tile(ATTN_WIDTH), tile(ATTN_WIDTH),
                   tile(SSM_WIDTH),
                   pl.BlockSpec((None, ATTN_HEADS, tm), lambda b, i: (b, 0, i))),
        out_shape=out_shape,
        scratch_shapes=[pltpu.VMEM((ATTN_HEADS, LANES), F32)],
        compiler_params=pltpu.CompilerParams(
            dimension_semantics=("arbitrary", "arbitrary"),
            vmem_limit_bytes=VMEM_LIMIT),
        name="head",
    )(x, g1, w1, w3, w2, gm, wqkv, ws, wf, bf)


def _attn_kernel(q_ref, k_ref, v_ref, c_ref, o_ref, m_ref, l_ref, acc_ref):
    t = q_ref.shape[0]
    qi = pl.program_id(2)
    q0 = pl.multiple_of(qi * t, t)
    q = q_ref[...]
    lane = lax.broadcasted_iota(jnp.int32, (1, HEAD_PAIR), 1)
    first = lane < HEAD_DIM
    q_heads = (jnp.where(first, q, 0).astype(BF16), jnp.where(first, 0, q).astype(BF16))
    c_base = [c_ref[h:h + 1, pl.ds(q0, LANES)][:, 0:1] for h in range(2)]
    row = lax.broadcasted_iota(jnp.int32, (t, t), 0)
    col = lax.broadcasted_iota(jnp.int32, (t, t), 1)
    causal = col <= row

    def scores(h, k0, masked):
        kt = k_ref[pl.ds(k0, t), :]
        s = lax.dot_general(q_heads[h], kt, (((1,), (1,)), ((), ())),
                            preferred_element_type=F32)
        s = s - (c_ref[h:h + 1, pl.ds(k0, t)] - c_base[h])
        return jnp.where(causal, s, NEG_BIG) if masked else s

    for h in range(2):
        s = scores(h, q0, True)
        m = jnp.max(s, axis=-1, keepdims=True)
        p = jnp.exp(s - m)
        m_ref[h] = m
        l_ref[h] = jnp.sum(p, axis=-1, keepdims=True)
        acc_ref[h] = jnp.dot(p.astype(BF16), v_ref[pl.ds(q0, t), :],
                             preferred_element_type=F32)

    def body(j, carry):
        k0 = pl.multiple_of(j * t, t)
        for h in range(2):
            s = scores(h, k0, False)
            m_old = m_ref[h]
            m_new = jnp.maximum(m_old, jnp.max(s, axis=-1, keepdims=True))
            alpha = jnp.exp(m_old - m_new)
            p = jnp.exp(s - m_new)
            m_ref[h] = m_new
            l_ref[h] = alpha * l_ref[h] + jnp.sum(p, axis=-1, keepdims=True)
            acc_ref[h] = alpha * acc_ref[h] + jnp.dot(
                p.astype(BF16), v_ref[pl.ds(k0, t), :], preferred_element_type=F32)
        return carry

    lax.fori_loop(0, qi, body, 0)
    o_ref[...] = jnp.where(first, acc_ref[0] / l_ref[0], acc_ref[1] / l_ref[1])


def _attn_call(q, k, v, ct):
    B, L, _ = q.shape
    t = ATTN_TILE
    c4 = ct.reshape(B, N_PAIRS, 2, L)
    return pl.pallas_call(
        _attn_kernel,
        grid=(B, N_PAIRS, L // t),
        in_specs=[pl.BlockSpec((None, t, HEAD_PAIR), lambda b, p, i: (b, i, p)),
                  pl.BlockSpec((None, L, HEAD_PAIR), lambda b, p, i: (b, 0, p)),
                  pl.BlockSpec((None, L, HEAD_PAIR), lambda b, p, i: (b, 0, p)),
                  pl.BlockSpec((None, None, 2, L), lambda b, p, i: (b, p, 0, 0))],
        out_specs=pl.BlockSpec((None, t, HEAD_PAIR), lambda b, p, i: (b, i, p)),
        out_shape=jax.ShapeDtypeStruct((B, L, ATTN_WIDTH), F32),
        scratch_shapes=[pltpu.VMEM((2, t, 1), F32), pltpu.VMEM((2, t, 1), F32),
                        pltpu.VMEM((2, t, HEAD_PAIR), F32)],
        compiler_params=pltpu.CompilerParams(
            dimension_semantics=("arbitrary", "arbitrary", "arbitrary"),
            vmem_limit_bytes=VMEM_LIMIT),
        name="attn",
    )(q, k, v, c4)


def _gelu_tanh(x):
    return 0.5 * x * (1.0 + jnp.tanh(math.sqrt(2.0 / math.pi) * (x + 0.044715 * (x * x * x))))


def _ssm_kernel(e_ref, mt_ref, w1t_ref, wot_ref, pw_ref, f_ref, *, chunks_per_seq):
    e = e_ref[...]
    y = jnp.dot(mt_ref[...], e, preferred_element_type=F32)
    s = jnp.dot(w1t_ref[...], e, preferred_element_type=F32)
    sr, si = s[:SSM_STATE], s[SSM_STATE:]
    pos = lax.broadcasted_iota(jnp.int32, sr.shape, 1) % chunks_per_seq

    def shifted(a, shift):
        return jnp.where(pos >= shift, pltpu.roll(a, shift, 1), 0.0)

    shift, step = 1, 0
    while shift < chunks_per_seq:
        ar, ai = pw_ref[step, :SSM_STATE], pw_ref[step, SSM_STATE:]
        srs, sis = shifted(sr, shift), shifted(si, shift)
        sr, si = sr + ar * srs - ai * sis, si + ar * sis + ai * srs
        shift *= 2
        step += 1
    x_prev = jnp.concatenate([shifted(sr, 1), shifted(si, 1)], axis=0).astype(BF16)
    y = y + jnp.dot(wot_ref[...], x_prev, preferred_element_type=F32)
    f_ref[...] = _gelu_tanh(y).reshape(f_ref.shape)


def _ssm_call(e, mt, w1t, wot, pw, chunks_per_seq):
    G, TH, NC = e.shape
    T = TH // SSM_GROUP_CH
    grp = lambda *shape: pl.BlockSpec((None,) + shape, lambda g: (g,) + (0,) * len(shape))
    return pl.pallas_call(
        functools.partial(_ssm_kernel, chunks_per_seq=chunks_per_seq),
        grid=(G,),
        in_specs=[grp(TH, NC), grp(TH, TH), grp(2 * SSM_STATE, TH), grp(TH, 2 * SSM_STATE),
                  grp(pw.shape[1], 2 * SSM_STATE, 1)],
        out_specs=pl.BlockSpec((T, SSM_GROUP_CH, NC), lambda g: (0, g, 0)),
        out_shape=jax.ShapeDtypeStruct((T, SSM_WIDTH, NC), F32),
        compiler_params=pltpu.CompilerParams(
            dimension_semantics=("arbitrary",), vmem_limit_bytes=VMEM_LIMIT),
        name="ssm",
    )(e, mt, w1t, wot, pw)


def _ssm_operators(a_re, a_im, log_dt, b_re, b_im, c_re, c_im, d_skip, n_scan_steps):
    T, P, H = SSM_CHUNK, SSM_STATE, SSM_GROUP_CH
    hi = lax.Precision.HIGHEST
    dt = jnp.exp(log_dt)[:, None]
    lam_r, lam_i = dt * a_re, dt * a_im
    j = jnp.arange(T + 1, dtype=F32)[None, :, None]
    mag = jnp.exp(j * lam_r[:, None, :])
    pr, pi = mag * jnp.cos(j * lam_i[:, None, :]), mag * jnp.sin(j * lam_i[:, None, :])
    abar_r, abar_i = pr[:, 1], pi[:, 1]
    nr, ni = abar_r - 1.0, abar_i
    den = a_re * a_re + a_im * a_im
    fr, fi = (nr * a_re + ni * a_im) / den, (ni * a_re - nr * a_im) / den
    bbar_r = fr[..., None] * b_re - fi[..., None] * b_im
    bbar_i = fr[..., None] * b_im + fi[..., None] * b_re
    cer = c_re[:, None] * pr[:, :T, None, :] - c_im[:, None] * pi[:, :T, None, :]
    cei = c_re[:, None] * pi[:, :T, None, :] + c_im[:, None] * pr[:, :T, None, :]
    kj = (jnp.einsum('gjhp,gpk->gjhk', cer, bbar_r, precision=hi)
          - jnp.einsum('gjhp,gpk->gjhk', cei, bbar_i, precision=hi))
    kj = kj.at[:, 0].add(jax.vmap(jnp.diag)(d_skip))
    lag = jnp.arange(T)[:, None] - jnp.arange(T)[None, :]
    toe = jnp.where((lag >= 0)[None, :, :, None, None],
                    kj[:, jnp.clip(lag, 0, T - 1)], 0.0)
    mt = toe.transpose(0, 1, 3, 2, 4).reshape(-1, T * H, T * H)
    wr = pr[:, T - 1::-1][:, :T]
    wi = pi[:, T - 1::-1][:, :T]
    w1r = wr[..., None] * bbar_r[:, None] - wi[..., None] * bbar_i[:, None]
    w1i = wr[..., None] * bbar_i[:, None] + wi[..., None] * bbar_r[:, None]
    w1t = jnp.concatenate([w1r, w1i], axis=2).transpose(0, 2, 1, 3).reshape(-1, 2 * P, T * H)
    cr1 = c_re[:, None] * pr[:, 1:, None, :] - c_im[:, None] * pi[:, 1:, None, :]
    ci1 = c_re[:, None] * pi[:, 1:, None, :] + c_im[:, None] * pr[:, 1:, None, :]
    wot = jnp.concatenate([cr1, -ci1], axis=-1).reshape(-1, T * H, 2 * P)
    qr, qi = pr[:, T], pi[:, T]
    pows = []
    for _ in range(n_scan_steps):
        pows.append(jnp.concatenate([qr, qi], axis=-1))
        qr, qi = qr * qr - qi * qi, 2.0 * qr * qi
    pw = jnp.stack(pows, axis=1)[..., None]
    return mt.astype(BF16), w1t.astype(BF16), wot.astype(BF16), pw


def _tail_kernel(h1_ref, attn_ref, y_ref, p_ref, wglu_ref, bglu_ref, ga_ref, gs_ref,
                 woa_ref, wos_ref, g2_ref, w1_ref, w3_ref, w2_ref, gp_ref, wpg_ref,
                 wpp_ref, gf_ref, o_ref):
    y = y_ref[...]
    glu = y * _sigmoid(jnp.dot(y.astype(BF16), wglu_ref[...], preferred_element_type=F32)
                       + bglu_ref[...])
    an = _rms(attn_ref[...], ga_ref[...]).astype(BF16)
    sn = _rms(glu, gs_ref[...]).astype(BF16)
    h = (h1_ref[...] + jnp.dot(an, woa_ref[...], preferred_element_type=F32)
         + jnp.dot(sn, wos_ref[...], preferred_element_type=F32))
    h = h + 0.5 * _swiglu(_rms(h, g2_ref[...]).astype(BF16), w1_ref, w3_ref, w2_ref)
    gate = _sigmoid(jnp.dot(_rms(h, gp_ref[...]).astype(BF16), wpg_ref[...],
                            preferred_element_type=F32))
    h = h + gate * jnp.dot(p_ref[...].astype(BF16), wpp_ref[...], preferred_element_type=F32)
    o_ref[...] = _rms(h, gf_ref[...])


def _tail_call(h1, attn, y, p, *consts):
    B, L, D = h1.shape
    tm = TOKEN_TILE
    tile = lambda w: pl.BlockSpec((None, tm, w), lambda b, i: (b, i, 0))
    return pl.pallas_call(
        _tail_kernel,
        grid=(B, L // tm),
        in_specs=[tile(D), tile(ATTN_WIDTH), tile(SSM_WIDTH), tile(PLE_DIM)]
                 + [_const_spec(c.shape) for c in consts],
        out_specs=tile(D),
        out_shape=jax.ShapeDtypeStruct((B, L, D), F32),
        compiler_params=pltpu.CompilerParams(
            dimension_semantics=("arbitrary", "arbitrary"),
            vmem_limit_bytes=VMEM_LIMIT),
        name="tail",
    )(h1, attn, y, p, *consts)


def kernel(x, p, g_ffn1, w1_a, w3_a, w2_a, g_mix, w_in, b_f, a_re, a_im, log_dt, b_re, b_im, c_re, c_im, d_skip, w_glu, b_glu, g_attn_out, g_ssm_out, w_out, g_ffn2, w1_b, w3_b, w2_b, g_ple, w_ple_gate, w_ple_proj, g_final):
    B, L, D = x.shape
    assert D == D_MODEL and L % ATTN_TILE == 0 and L % TOKEN_TILE == 0 and L % SSM_CHUNK == 0
    assert g_ffn1.shape[0] == 1, "single layer"
    row = lambda g: g.reshape(1, -1).astype(F32)
    bf = lambda w: w.astype(BF16)
    s_v, s_f = 3 * ATTN_WIDTH, 3 * ATTN_WIDTH + ATTN_HEADS
    w_in0 = w_in[0]
    scale = 1.0 / math.sqrt(HEAD_DIM)
    wqkv = jnp.concatenate([w_in0[:, :ATTN_WIDTH] * scale, w_in0[:, ATTN_WIDTH:s_v]], axis=1)
    wf = jnp.pad(w_in0[:, s_v:s_f], ((0, 0), (0, LANES - ATTN_HEADS)))

    h1, q, k, v, s_in, ct = _head_call(
        x, row(g_ffn1[0]), bf(w1_a[0]), bf(w3_a[0]), bf(w2_a[0]), row(g_mix[0]),
        bf(wqkv), bf(w_in0[:, s_f:]), bf(wf), b_f[0].reshape(ATTN_HEADS, 1).astype(F32))

    attn = _attn_call(q, k, v, ct)

    T = SSM_CHUNK
    chunks_per_seq = L // T
    n_chunks = B * chunks_per_seq
    n_scan_steps = max(1, (chunks_per_seq - 1).bit_length())
    mt, w1t, wot, pw = _ssm_operators(a_re[0], a_im[0], log_dt[0], b_re[0], b_im[0],
                                      c_re[0], c_im[0], d_skip[0], n_scan_steps)
    e = (s_in.reshape(n_chunks, T, SSM_GROUPS, SSM_GROUP_CH)
         .transpose(2, 1, 3, 0).reshape(SSM_GROUPS, T * SSM_GROUP_CH, n_chunks))
    f = _ssm_call(e, mt, w1t, wot, pw, chunks_per_seq)
    y = f.transpose(2, 0, 1).reshape(B, L, SSM_WIDTH)

    w_out0 = w_out[0]
    return _tail_call(
        h1, attn, y, p[0],
        bf(w_glu[0]), row(b_glu[0]), row(g_attn_out[0]), row(g_ssm_out[0]),
        bf(w_out0[:ATTN_WIDTH]), bf(w_out0[ATTN_WIDTH:]), row(g_ffn2[0]),
        bf(w1_b[0]), bf(w3_b[0]), bf(w2_b[0]), row(g_ple[0]), bf(w_ple_gate[0]),
        bf(w_ple_proj[0]), row(g_final))
```

```python
import functools
import math

import jax
import jax.numpy as jnp
from jax import lax
from jax.experimental import pallas as pl
from jax.experimental.pallas import tpu as pltpu

D_MODEL = 1024
ATTN_HEADS = 8
HEAD_DIM = 64
ATTN_WIDTH = ATTN_HEADS * HEAD_DIM
SSM_WIDTH = D_MODEL - ATTN_WIDTH
SSM_GROUP_CH = 16
SSM_GROUPS = SSM_WIDTH // SSM_GROUP_CH
SSM_STATE = 64
D_FF = 2816
PLE_DIM = 256
EPS = 1e-6

LANES = 128
HEAD_PAIR = 2 * HEAD_DIM
N_PAIRS = ATTN_HEADS // 2
FF_CHUNK = 256
TOKEN_TILE = 512
ATTN_TILE = 512
SSM_CHUNK = 32
NEG_BIG = -1e30
LOG2E = math.log2(math.e)
N_BIAS = 3
BIAS_ROWS = 8
VMEM_LIMIT = 56 * 1024 * 1024

BF16 = jnp.bfloat16
F32 = jnp.float32


def _rms(x, g):
    ms = jnp.mean(x * x, axis=-1, keepdims=True)
    return x * lax.rsqrt(ms + EPS) * g


def _sigmoid(x):
    return 1.0 / (1.0 + jnp.exp(-x))


def _swiglu(xn, w1_ref, w3_ref, w2_ref):
    acc = None
    for c in range(D_FF // FF_CHUNK):
        sl = slice(c * FF_CHUNK, (c + 1) * FF_CHUNK)
        a = jnp.dot(xn, w1_ref[:, sl], preferred_element_type=F32)
        b = jnp.dot(xn, w3_ref[:, sl], preferred_element_type=F32)
        gated = (a * _sigmoid(a) * b).astype(BF16)
        part = jnp.dot(gated, w2_ref[sl, :], preferred_element_type=F32)
        acc = part if acc is None else acc + part
    return acc


def _const_spec(shape):
    nd = len(shape)
    return pl.BlockSpec(shape, lambda *_: (0,) * nd, pipeline_mode=pl.Buffered(1))


def _head_kernel(x_ref, g1_ref, w1_ref, w3_ref, w2_ref, gm_ref, wqkv_ref, ws_ref,
                 wf_ref, bf_ref, h1_ref, q_ref, kt_ref, v_ref, s_ref, ct_ref,
                 carry_ref):
    tm = x_ref.shape[0]
    x = x_ref[...]
    h1 = x + 0.5 * _swiglu(_rms(x, g1_ref[...]).astype(BF16), w1_ref, w3_ref, w2_ref)
    h1_ref[...] = h1
    un = _rms(h1, gm_ref[...]).astype(BF16)
    qkv = jnp.dot(un, wqkv_ref[...], preferred_element_type=F32)
    q_ref[...] = qkv[:, :ATTN_WIDTH].astype(BF16)
    s_ref[...] = jnp.dot(un, ws_ref[...], preferred_element_type=F32).astype(BF16)

    zf = jnp.dot(un, wf_ref[...], preferred_element_type=F32)
    zft = zf.T[:ATTN_HEADS, :] + bf_ref[...]
    logf = jnp.minimum(zft, 0.0) - jnp.log1p(jnp.exp(-jnp.abs(zft)))
    lane = lax.broadcasted_iota(jnp.int32, logf.shape, 1)
    c = logf
    shift = 1
    while shift < tm:
        c = c + jnp.where(lane >= shift, pltpu.roll(c, shift, 1), 0.0)
        shift *= 2

    @pl.when(pl.program_id(1) == 0)
    def _():
        carry_ref[...] = jnp.zeros_like(carry_ref)

    c_abs = c + carry_ref[:, 0:1]
    ct_ref[...] = c_abs * LOG2E
    carry_ref[...] = jnp.broadcast_to(c_abs[:, tm - 1:tm], carry_ref.shape)

    rel = (c - c[:, 0:1]) * LOG2E
    hi = rel.astype(BF16).astype(F32)
    mid = (rel - hi).astype(BF16).astype(F32)
    lo = (rel - hi - mid).astype(BF16).astype(F32)
    kt = qkv[:, ATTN_WIDTH:2 * ATTN_WIDTH].T
    vv = qkv[:, 2 * ATTN_WIDTH:]
    sub = lax.broadcasted_iota(jnp.int32, (BIAS_ROWS, tm), 0)
    zeros = jnp.zeros((HEAD_DIM - BIAS_ROWS, tm), F32)
    vlane = lax.broadcasted_iota(jnp.int32, (tm, HEAD_PAIR), 1)
    for h in range(ATTN_HEADS):
        bias = jnp.where(sub == 0, -hi[h:h + 1],
                         jnp.where(sub == 1, -mid[h:h + 1],
                                   jnp.where(sub == 2, -lo[h:h + 1], 0.0)))
        k_h = kt[h * HEAD_DIM:(h + 1) * HEAD_DIM]
        vp = vv[:, (h // 2) * HEAD_PAIR:(h // 2 + 1) * HEAD_PAIR]
        if h % 2 == 0:
            kt_ref[h] = jnp.concatenate([k_h, bias, zeros], axis=0).astype(BF16)
            v_ref[h] = jnp.where(vlane < HEAD_DIM, vp,
                                 jnp.where(vlane == HEAD_DIM, 1.0, 0.0)).astype(BF16)
        else:
            kt_ref[h] = jnp.concatenate([bias, zeros, k_h], axis=0).astype(BF16)
            v_ref[h] = jnp.where(vlane >= HEAD_DIM, vp,
                                 jnp.where(vlane == 0, 1.0, 0.0)).astype(BF16)


def _head_call(x, g1, w1, w3, w2, gm, wqkv, ws, wf, bf):
    B, L, D = x.shape
    tm = TOKEN_TILE
    tile = lambda w: pl.BlockSpec((None, tm, w), lambda b, i: (b, i, 0))
    out_shape = (
        jax.ShapeDtypeStruct((B, L, D), F32),
        jax.ShapeDtypeStruct((B, L, ATTN_WIDTH), BF16),
        jax.ShapeDtypeStruct((B, ATTN_HEADS, HEAD_PAIR, L), BF16),
        jax.ShapeDtypeStruct((B, ATTN_HEADS, L, HEAD_PAIR), BF16),
        jax.ShapeDtypeStruct((B, L, SSM_WIDTH), BF16),
        jax.ShapeDtypeStruct((B, ATTN_HEADS, L), F32),
    )
    return pl.pallas_call(
        _head_kernel,
        grid=(B, L // tm),
        in_specs=[tile(D), _const_spec(g1.shape), _const_spec(w1.shape),
                  _const_spec(w3.shape), _const_spec(w2.shape), _const_spec(gm.shape),
                  _const_spec(wqkv.shape), _const_spec(ws.shape), _const_spec(wf.shape),
                  _const_spec(bf.shape)],
        out_specs=(tile(D), tile(ATTN_WIDTH),
                   pl.BlockSpec((None, ATTN_HEADS, HEAD_PAIR, tm), lambda b, i: (b, 0, 0, i)),
                   pl.BlockSpec((None, ATTN_HEADS, tm, HEAD_PAIR), lambda b, i: (b, 0, i, 0)),
                   tile(SSM_WIDTH),
                   pl.BlockSpec((None, ATTN_HEADS, tm), lambda b, i: (b, 0, i))),
        out_shape=out_shape,
        scratch_shapes=[pltpu.VMEM((ATTN_HEADS, LANES), F32)],
        compiler_params=pltpu.CompilerParams(
            dimension_semantics=("arbitrary", "arbitrary"),
            vmem_limit_bytes=VMEM_LIMIT),
        name="head",
    )(x, g1, w1, w3, w2, gm, wqkv, ws, wf, bf)


def _attn_kernel(q_ref, kt_ref, v_ref, c_ref, o_ref, sa0, sa1, sb0, sb1, m_ref, acc_ref):
    t = q_ref.shape[0]
    n_full = pl.program_id(2)
    q = q_ref[...]
    lane = lax.broadcasted_iota(jnp.int32, (1, HEAD_PAIR), 1)
    first = lane < HEAD_DIM
    ones_even = jnp.where((lane >= HEAD_DIM) & (lane < HEAD_DIM + N_BIAS), 1.0, 0.0).astype(BF16)
    ones_odd = jnp.where(lane < N_BIAS, 1.0, 0.0).astype(BF16)
    q_heads = (jnp.where(first, q, ones_even), jnp.where(first, ones_odd, q))
    c_q = [c_ref[h:h + 1, pl.ds(pl.multiple_of(n_full * t, t), LANES)][:, 0:1] for h in range(2)]
    buf_a, buf_b = (sa0, sa1), (sb0, sb1)

    def start(k):
        return pl.multiple_of(k * t, t)

    def qk(h, k):
        return jnp.dot(q_heads[h], kt_ref[h, :, pl.ds(start(k), t)], preferred_element_type=F32)

    def consume(h, s, k):
        d = c_ref[h:h + 1, pl.ds(start(k), LANES)][:, 0:1] - c_q[h]
        m_old = m_ref[h]
        m_new = jnp.maximum(m_old, jnp.max(s, axis=-1, keepdims=True) - d)
        p = jnp.exp2(s - (m_new + d)).astype(BF16)
        m_ref[h] = m_new
        acc_ref[h] = jnp.exp2(m_old - m_new) * acc_ref[h] + jnp.dot(
            p, v_ref[h, pl.ds(start(k), t), :], preferred_element_type=F32)

    def step(cur, nxt, k):
        for h in range(2):
            nxt[h][...] = qk(h, k + 1)
        for h in range(2):
            consume(h, cur[h][...], k)

    def consume_diagonal(buf):
        row = lax.broadcasted_iota(jnp.int32, (t, t), 0)
        col = lax.broadcasted_iota(jnp.int32, (t, t), 1)
        for h in range(2):
            consume(h, jnp.where(col <= row, buf[h][...], NEG_BIG), n_full)

    m_ref[...] = jnp.full(m_ref.shape, NEG_BIG, F32)
    acc_ref[...] = jnp.zeros(acc_ref.shape, F32)
    for h in range(2):
        buf_a[h][...] = qk(h, 0)

    def pair(i, carry):
        step(buf_a, buf_b, 2 * i)
        step(buf_b, buf_a, 2 * i + 1)
        return carry

    lax.fori_loop(0, n_full // 2, pair, 0)

    @pl.when(n_full % 2 == 1)
    def _():
        step(buf_a, buf_b, n_full - 1)
        consume_diagonal(buf_b)

    @pl.when(n_full % 2 == 0)
    def _():
        consume_diagonal(buf_a)

    acc0, acc1 = acc_ref[0], acc_ref[1]
    o_ref[...] = jnp.where(first, acc0 / acc0[:, HEAD_DIM:HEAD_DIM + 1], acc1 / acc1[:, 0:1])


def _attn_call(q, kt, v, ct):
    B, L, _ = q.shape
    t = ATTN_TILE
    c4 = ct.reshape(B, N_PAIRS, 2, L)
    pair_block = lambda *shape: pl.BlockSpec((None, 2) + shape, lambda b, p, i: (b, p, 0, 0))
    return pl.pallas_call(
        _attn_kernel,
        grid=(B, N_PAIRS, L // t),
        in_specs=[pl.BlockSpec((None, t, HEAD_PAIR), lambda b, p, i: (b, i, p)),
                  pair_block(HEAD_PAIR, L), pair_block(L, HEAD_PAIR),
                  pl.BlockSpec((None, None, 2, L), lambda b, p, i: (b, p, 0, 0))],
        out_specs=pl.BlockSpec((None, t, HEAD_PAIR), lambda b, p, i: (b, i, p)),
        out_shape=jax.ShapeDtypeStruct((B, L, ATTN_WIDTH), F32),
        scratch_shapes=[pltpu.VMEM((t, t), F32)] * 4
                       + [pltpu.VMEM((2, t, 1), F32), pltpu.VMEM((2, t, HEAD_PAIR), F32)],
        compiler_params=pltpu.CompilerParams(
            dimension_semantics=("arbitrary", "arbitrary", "arbitrary"),
            vmem_limit_bytes=VMEM_LIMIT),
        name="attn",
    )(q, kt, v, c4)


def _gelu_tanh(x):
    return 0.5 * x * (1.0 + jnp.tanh(math.sqrt(2.0 / math.pi) * (x + 0.044715 * (x * x * x))))


def _ssm_kernel(e_ref, mt_ref, w1t_ref, wot_ref, pw_ref, f_ref, *, chunks_per_seq):
    e = e_ref[...]
    y = jnp.dot(mt_ref[...], e, preferred_element_type=F32)
    s = jnp.dot(w1t_ref[...], e, preferred_element_type=F32)
    sr, si = s[:SSM_STATE], s[SSM_STATE:]
    pos = lax.broadcasted_iota(jnp.int32, sr.shape, 1) % chunks_per_seq

    def shifted(a, shift):
        return jnp.where(pos >= shift, pltpu.roll(a, shift, 1), 0.0)

    shift, step = 1, 0
    while shift < chunks_per_seq:
        ar, ai = pw_ref[step, :SSM_STATE], pw_ref[step, SSM_STATE:]
        srs, sis = shifted(sr, shift), shifted(si, shift)
        sr, si = sr + ar * srs - ai * sis, si + ar * sis + ai * srs
        shift *= 2
        step += 1
    x_prev = jnp.concatenate([shifted(sr, 1), shifted(si, 1)], axis=0).astype(BF16)
    y = y + jnp.dot(wot_ref[...], x_prev, preferred_element_type=F32)
    f_ref[...] = _gelu_tanh(y).reshape(f_ref.shape)


def _ssm_call(e, mt, w1t, wot, pw, chunks_per_seq):
    G, TH, NC = e.shape
    T = TH // SSM_GROUP_CH
    grp = lambda *shape: pl.BlockSpec((None,) + shape, lambda g: (g,) + (0,) * len(shape))
    return pl.pallas_call(
        functools.partial(_ssm_kernel, chunks_per_seq=chunks_per_seq),
        grid=(G,),
        in_specs=[grp(TH, NC), grp(TH, TH), grp(2 * SSM_STATE, TH), grp(TH, 2 * SSM_STATE),
                  grp(pw.shape[1], 2 * SSM_STATE, 1)],
        out_specs=pl.BlockSpec((T, SSM_GROUP_CH, NC), lambda g: (0, g, 0)),
        out_shape=jax.ShapeDtypeStruct((T, SSM_WIDTH, NC), F32),
        compiler_params=pltpu.CompilerParams(
            dimension_semantics=("arbitrary",), vmem_limit_bytes=VMEM_LIMIT),
        name="ssm",
    )(e, mt, w1t, wot, pw)


def _ssm_operators(a_re, a_im, log_dt, b_re, b_im, c_re, c_im, d_skip, n_scan_steps):
    T, P, H = SSM_CHUNK, SSM_STATE, SSM_GROUP_CH
    hi = lax.Precision.HIGHEST
    dt = jnp.exp(log_dt)[:, None]
    lam_r, lam_i = dt * a_re, dt * a_im
    j = jnp.arange(T + 1, dtype=F32)[None, :, None]
    mag = jnp.exp(j * lam_r[:, None, :])
    pr, pi = mag * jnp.cos(j * lam_i[:, None, :]), mag * jnp.sin(j * lam_i[:, None, :])
    abar_r, abar_i = pr[:, 1], pi[:, 1]
    nr, ni = abar_r - 1.0, abar_i
    den = a_re * a_re + a_im * a_im
    fr, fi = (nr * a_re + ni * a_im) / den, (ni * a_re - nr * a_im) / den
    bbar_r = fr[..., None] * b_re - fi[..., None] * b_im
    bbar_i = fr[..., None] * b_im + fi[..., None] * b_re
    cer = c_re[:, None] * pr[:, :T, None, :] - c_im[:, None] * pi[:, :T, None, :]
    cei = c_re[:, None] * pi[:, :T, None, :] + c_im[:, None] * pr[:, :T, None, :]
    kj = (jnp.einsum('gjhp,gpk->gjhk', cer, bbar_r, precision=hi)
          - jnp.einsum('gjhp,gpk->gjhk', cei, bbar_i, precision=hi))
    kj = kj.at[:, 0].add(jax.vmap(jnp.diag)(d_skip))
    lag = jnp.arange(T)[:, None] - jnp.arange(T)[None, :]
    toe = jnp.where((lag >= 0)[None, :, :, None, None],
                    kj[:, jnp.clip(lag, 0, T - 1)], 0.0)
    mt = toe.transpose(0, 1, 3, 2, 4).reshape(-1, T * H, T * H)
    wr = pr[:, T - 1::-1][:, :T]
    wi = pi[:, T - 1::-1][:, :T]
    w1r = wr[..., None] * bbar_r[:, None] - wi[..., None] * bbar_i[:, None]
    w1i = wr[..., None] * bbar_i[:, None] + wi[..., None] * bbar_r[:, None]
    w1t = jnp.concatenate([w1r, w1i], axis=2).transpose(0, 2, 1, 3).reshape(-1, 2 * P, T * H)
    cr1 = c_re[:, None] * pr[:, 1:, None, :] - c_im[:, None] * pi[:, 1:, None, :]
    ci1 = c_re[:, None] * pi[:, 1:, None, :] + c_im[:, None] * pr[:, 1:, None, :]
    wot = jnp.concatenate([cr1, -ci1], axis=-1).reshape(-1, T * H, 2 * P)
    qr, qi = pr[:, T], pi[:, T]
    pows = []
    for _ in range(n_scan_steps):
        pows.append(jnp.concatenate([qr, qi], axis=-1))
        qr, qi = qr * qr - qi * qi, 2.0 * qr * qi
    pw = jnp.stack(pows, axis=1)[..., None]
    return mt.astype(BF16), w1t.astype(BF16), wot.astype(BF16), pw


def _tail_kernel(h1_ref, attn_ref, y_ref, p_ref, wglu_ref, bglu_ref, ga_ref, gs_ref,
                 woa_ref, wos_ref, g2_ref, w1_ref, w3_ref, w2_ref, gp_ref, wpg_ref,
                 wpp_ref, gf_ref, o_ref):
    y = y_ref[...]
    glu = y * _sigmoid(jnp.dot(y.astype(BF16), wglu_ref[...], preferred_element_type=F32)
                       + bglu_ref[...])
    an = _rms(attn_ref[...], ga_ref[...]).astype(BF16)
    sn = _rms(glu, gs_ref[...]).astype(BF16)
    h = (h1_ref[...] + jnp.dot(an, woa_ref[...], preferred_element_type=F32)
         + jnp.dot(sn, wos_ref[...], preferred_element_type=F32))
    h = h + 0.5 * _swiglu(_rms(h, g2_ref[...]).astype(BF16), w1_ref, w3_ref, w2_ref)
    gate = _sigmoid(jnp.dot(_rms(h, gp_ref[...]).astype(BF16), wpg_ref[...],
                            preferred_element_type=F32))
    h = h + gate * jnp.dot(p_ref[...].astype(BF16), wpp_ref[...], preferred_element_type=F32)
    o_ref[...] = _rms(h, gf_ref[...])


def _tail_call(h1, attn, y, p, *consts):
    B, L, D = h1.shape
    tm = TOKEN_TILE
    tile = lambda w: pl.BlockSpec((None, tm, w), lambda b, i: (b, i, 0))
    return pl.pallas_call(
        _tail_kernel,
        grid=(B, L // tm),
        in_specs=[tile(D), tile(ATTN_WIDTH), tile(SSM_WIDTH), tile(PLE_DIM)]
                 + [_const_spec(c.shape) for c in consts],
        out_specs=tile(D),
        out_shape=jax.ShapeDtypeStruct((B, L, D), F32),
        compiler_params=pltpu.CompilerParams(
            dimension_semantics=("arbitrary", "arbitrary"),
            vmem_limit_bytes=VMEM_LIMIT),
        name="tail",
    )(h1, attn, y, p, *consts)


def kernel(x, p, g_ffn1, w1_a, w3_a, w2_a, g_mix, w_in, b_f, a_re, a_im, log_dt, b_re, b_im, c_re, c_im, d_skip, w_glu, b_glu, g_attn_out, g_ssm_out, w_out, g_ffn2, w1_b, w3_b, w2_b, g_ple, w_ple_gate, w_ple_proj, g_final):
    B, L, D = x.shape
    assert D == D_MODEL and L % ATTN_TILE == 0 and L % TOKEN_TILE == 0 and L % SSM_CHUNK == 0
    assert g_ffn1.shape[0] == 1, "single layer"
    assert TOKEN_TILE == ATTN_TILE, "decay bias rows are relative to the kv tile start"
    row = lambda g: g.reshape(1, -1).astype(F32)
    bf = lambda w: w.astype(BF16)
    s_v, s_f = 3 * ATTN_WIDTH, 3 * ATTN_WIDTH + ATTN_HEADS
    w_in0 = w_in[0]
    scale = LOG2E / math.sqrt(HEAD_DIM)
    wqkv = jnp.concatenate([w_in0[:, :ATTN_WIDTH] * scale, w_in0[:, ATTN_WIDTH:s_v]], axis=1)
    wf = jnp.pad(w_in0[:, s_v:s_f], ((0, 0), (0, LANES - ATTN_HEADS)))

    h1, q, kt, v, s_in, ct = _head_call(
        x, row(g_ffn1[0]), bf(w1_a[0]), bf(w3_a[0]), bf(w2_a[0]), row(g_mix[0]),
        bf(wqkv), bf(w_in0[:, s_f:]), bf(wf), b_f[0].reshape(ATTN_HEADS, 1).astype(F32))

    attn = _attn_call(q, kt, v, ct)

    T = SSM_CHUNK
    chunks_per_seq = L // T
    n_chunks = B * chunks_per_seq
    n_scan_steps = max(1, (chunks_per_seq - 1).bit_length())
    mt, w1t, wot, pw = _ssm_operators(a_re[0], a_im[0], log_dt[0], b_re[0], b_im[0],
                                      c_re[0], c_im[0], d_skip[0], n_scan_steps)
    e = (s_in.reshape(n_chunks, T, SSM_GROUPS, SSM_GROUP_CH)
         .transpose(2, 1, 3, 0).reshape(SSM_GROUPS, T * SSM_GROUP_CH, n_chunks))
    f = _ssm_call(e, mt, w1t, wot, pw, chunks_per_seq)
    y = f.transpose(2, 0, 1).reshape(B, L, SSM_WIDTH)

    w_out0 = w_out[0]
    return _tail_call(
        h1, attn, y, p[0],
        bf(w_glu[0]), row(b_glu[0]), row(g_attn_out[0]), row(g_ssm_out[0]),
        bf(w_out0[:ATTN_WIDTH]), bf(w_out0[ATTN_WIDTH:]), row(g_ffn2[0]),
        bf(w1_b[0]), bf(w3_b[0]), bf(w2_b[0]), row(g_ple[0]), bf(w_ple_gate[0]),
        bf(w_ple_proj[0]), row(g_final))
```

```python
import functools
import math

import jax
import jax.numpy as jnp
from jax import lax
from jax.experimental import pallas as pl
from jax.experimental.pallas import tpu as pltpu

D_MODEL = 1024
ATTN_HEADS = 8
HEAD_DIM = 64
ATTN_WIDTH = ATTN_HEADS * HEAD_DIM
SSM_WIDTH = D_MODEL - ATTN_WIDTH
SSM_GROUP_CH = 16
SSM_GROUPS = SSM_WIDTH // SSM_GROUP_CH
SSM_STATE = 64
D_FF = 2816
PLE_DIM = 256
EPS = 1e-6

LANES = 128
HEAD_PAIR = 2 * HEAD_DIM
N_PAIRS = ATTN_HEADS // 2
FF_CHUNK = 256
TOKEN_TILE = 512
ATTN_TILE = 512
STEPS_PER_ITER = 4
SSM_CHUNK = 32
NEG_BIG = -1e30
LOG2E = math.log2(math.e)
N_BIAS = 3
BIAS_ROWS = 8
VMEM_LIMIT = 56 * 1024 * 1024

BF16 = jnp.bfloat16
F32 = jnp.float32


def _rms(x, g):
    ms = jnp.mean(x * x, axis=-1, keepdims=True)
    return x * lax.rsqrt(ms + EPS) * g


def _sigmoid(x):
    return 1.0 / (1.0 + jnp.exp(-x))


def _swiglu(xn, w1_ref, w3_ref, w2_ref):
    acc = None
    for c in range(D_FF // FF_CHUNK):
        sl = slice(c * FF_CHUNK, (c + 1) * FF_CHUNK)
        a = jnp.dot(xn, w1_ref[:, sl], preferred_element_type=F32)
        b = jnp.dot(xn, w3_ref[:, sl], preferred_element_type=F32)
        gated = (a * _sigmoid(a) * b).astype(BF16)
        part = jnp.dot(gated, w2_ref[sl, :], preferred_element_type=F32)
        acc = part if acc is None else acc + part
    return acc


def _const_spec(shape):
    nd = len(shape)
    return pl.BlockSpec(shape, lambda *_: (0,) * nd, pipeline_mode=pl.Buffered(1))


def _head_kernel(x_ref, g1_ref, w1_ref, w3_ref, w2_ref, gm_ref, wqkv_ref, ws_ref,
                 wf_ref, bf_ref, h1_ref, q_ref, kt_ref, v_ref, s_ref, ct_ref,
                 carry_ref):
    tm = x_ref.shape[0]
    x = x_ref[...]
    h1 = x + 0.5 * _swiglu(_rms(x, g1_ref[...]).astype(BF16), w1_ref, w3_ref, w2_ref)
    h1_ref[...] = h1
    un = _rms(h1, gm_ref[...]).astype(BF16)
    qkv = jnp.dot(un, wqkv_ref[...], preferred_element_type=F32)
    q_ref[...] = qkv[:, :ATTN_WIDTH].astype(BF16)
    s_ref[...] = jnp.dot(un, ws_ref[...], preferred_element_type=F32).astype(BF16)

    zf = jnp.dot(un, wf_ref[...], preferred_element_type=F32)
    zft = zf.T[:ATTN_HEADS, :] + bf_ref[...]
    logf = jnp.minimum(zft, 0.0) - jnp.log1p(jnp.exp(-jnp.abs(zft)))
    lane = lax.broadcasted_iota(jnp.int32, logf.shape, 1)
    c = logf
    shift = 1
    while shift < tm:
        c = c + jnp.where(lane >= shift, pltpu.roll(c, shift, 1), 0.0)
        shift *= 2

    @pl.when(pl.program_id(1) == 0)
    def _():
        carry_ref[...] = jnp.zeros_like(carry_ref)

    c_abs = c + carry_ref[:, 0:1]
    ct_ref[...] = c_abs * LOG2E
    carry_ref[...] = jnp.broadcast_to(c_abs[:, tm - 1:tm], carry_ref.shape)

    rel = (c - c[:, 0:1]) * LOG2E
    hi = rel.astype(BF16).astype(F32)
    mid = (rel - hi).astype(BF16).astype(F32)
    lo = (rel - hi - mid).astype(BF16).astype(F32)
    kt = qkv[:, ATTN_WIDTH:2 * ATTN_WIDTH].T
    vv = qkv[:, 2 * ATTN_WIDTH:]
    sub = lax.broadcasted_iota(jnp.int32, (BIAS_ROWS, tm), 0)
    zeros = jnp.zeros((HEAD_DIM - BIAS_ROWS, tm), F32)
    vlane = lax.broadcasted_iota(jnp.int32, (tm, HEAD_PAIR), 1)
    for h in range(ATTN_HEADS):
        bias = jnp.where(sub == 0, -hi[h:h + 1],
                         jnp.where(sub == 1, -mid[h:h + 1],
                                   jnp.where(sub == 2, -lo[h:h + 1], 0.0)))
        k_h = kt[h * HEAD_DIM:(h + 1) * HEAD_DIM]
        vp = vv[:, (h // 2) * HEAD_PAIR:(h // 2 + 1) * HEAD_PAIR]
        if h % 2 == 0:
            kt_ref[h] = jnp.concatenate([k_h, bias, zeros], axis=0).astype(BF16)
            v_ref[h] = jnp.where(vlane < HEAD_DIM, vp,
                                 jnp.where(vlane == HEAD_DIM, 1.0, 0.0)).astype(BF16)
        else:
            kt_ref[h] = jnp.concatenate([bias, zeros, k_h], axis=0).astype(BF16)
            v_ref[h] = jnp.where(vlane >= HEAD_DIM, vp,
                                 jnp.where(vlane == 0, 1.0, 0.0)).astype(BF16)


def _head_call(x, g1, w1, w3, w2, gm, wqkv, ws, wf, bf):
    B, L, D = x.shape
    tm = TOKEN_TILE
    tile = lambda w: pl.BlockSpec((None, tm, w), lambda b, i: (b, i, 0))
    out_shape = (
        jax.ShapeDtypeStruct((B, L, D), F32),
        jax.ShapeDtypeStruct((B, L, ATTN_WIDTH), BF16),
        jax.ShapeDtypeStruct((B, ATTN_HEADS, HEAD_PAIR, L), BF16),
        jax.ShapeDtypeStruct((B, ATTN_HEADS, L, HEAD_PAIR), BF16),
        jax.ShapeDtypeStruct((B, L, SSM_WIDTH), BF16),
        jax.ShapeDtypeStruct((B, ATTN_HEADS, L), F32),
    )
    return pl.pallas_call(
        _head_kernel,
        grid=(B, L // tm),
        in_specs=[tile(D), _const_spec(g1.shape), _const_spec(w1.shape),
                  _const_spec(w3.shape), _const_spec(w2.shape), _const_spec(gm.shape),
                  _const_spec(wqkv.shape), _const_spec(ws.shape), _const_spec(wf.shape),
                  _const_spec(bf.shape)],
        out_specs=(tile(D), tile(ATTN_WIDTH),
                   pl.BlockSpec((None, ATTN_HEADS, HEAD_PAIR, tm), lambda b, i: (b, 0, 0, i)),
                   pl.BlockSpec((None, ATTN_HEADS, tm, HEAD_PAIR), lambda b, i: (b, 0, i, 0)),
                   tile(SSM_WIDTH),
                   pl.BlockSpec((None, ATTN_HEADS, tm), lambda b, i: (b, 0, i))),
        out_shape=out_shape,
        scratch_shapes=[pltpu.VMEM((ATTN_HEADS, LANES), F32)],
        compiler_params=pltpu.CompilerParams(
            dimension_semantics=("arbitrary", "arbitrary"),
            vmem_limit_bytes=VMEM_LIMIT),
        name="head",
    )(x, g1, w1, w3, w2, gm, wqkv, ws, wf, bf)


def _attn_kernel(q_ref, kt_ref, v_ref, c_ref, o_ref, s_e0, s_e1, s_o0, s_o1, m_ref, acc_ref):
    t = q_ref.shape[0]
    n = pl.program_id(2)
    q = q_ref[...]
    lane = lax.broadcasted_iota(jnp.int32, (1, HEAD_PAIR), 1)
    first = lane < HEAD_DIM
    ones_even = jnp.where((lane >= HEAD_DIM) & (lane < HEAD_DIM + N_BIAS), 1.0, 0.0).astype(BF16)
    ones_odd = jnp.where(lane < N_BIAS, 1.0, 0.0).astype(BF16)
    q_heads = (jnp.where(first, q, ones_even), jnp.where(first, ones_odd, q))
    s_buf = ((s_e0, s_e1), (s_o0, s_o1))

    def start(kv):
        return pl.multiple_of(kv * t, t)

    c_q = [c_ref[h:h + 1, pl.ds(start(n), LANES)][:, 0:1] for h in range(2)]

    def qk(h, kv):
        return jnp.dot(q_heads[h], kt_ref[h, :, pl.ds(start(kv), t)], preferred_element_type=F32)

    def softmax(h, s, kv):
        d = c_ref[h:h + 1, pl.ds(start(kv), LANES)][:, 0:1] - c_q[h]
        m_old = m_ref[h]
        m_new = jnp.maximum(m_old, jnp.max(s, axis=-1, keepdims=True) - d)
        m_ref[h] = m_new
        p = jnp.exp2(s - jnp.tile(m_new + d, (1, t // LANES))).astype(BF16)
        return p, jnp.exp2(m_old - m_new)

    def consume(s0, s1, kv):
        p0, a0 = softmax(0, s0, kv)
        p1, a1 = softmax(1, s1, kv)
        v0, v1 = v_ref[0, pl.ds(start(kv), t), :], v_ref[1, pl.ds(start(kv), t), :]
        z = jnp.zeros_like(v0)
        v_diag = jnp.concatenate([jnp.concatenate([v0, z], axis=1),
                                  jnp.concatenate([z, v1], axis=1)], axis=0)
        acc_ref[...] = jnp.concatenate([a0, a1], axis=1) * acc_ref[...] + jnp.dot(
            jnp.concatenate([p0, p1], axis=1), v_diag, preferred_element_type=F32)

    def step(kv, par):
        for h in range(2):
            s_buf[1 - par][h][...] = qk(h, kv + 1)
        consume(s_buf[par][0][...], s_buf[par][1][...], kv)

    def consume_diagonal(par):
        row = lax.broadcasted_iota(jnp.int32, (t, t), 0)
        col = lax.broadcasted_iota(jnp.int32, (t, t), 1)
        consume(*(jnp.where(col <= row, s_buf[par][h][...], NEG_BIG) for h in range(2)), n)

    m_ref[...] = jnp.full(m_ref.shape, NEG_BIG, F32)
    acc_ref[...] = jnp.zeros(acc_ref.shape, F32)
    for h in range(2):
        s_buf[0][h][...] = qk(h, 0)

    def unrolled(i, carry):
        for u in range(STEPS_PER_ITER):
            step(STEPS_PER_ITER * i + u, u % 2)
        return carry

    lax.fori_loop(0, n // STEPS_PER_ITER, unrolled, 0)
    rest = n % STEPS_PER_ITER
    for u in range(STEPS_PER_ITER - 1):
        @pl.when(rest > u)
        def _(u=u):
            step(n - rest + u, u % 2)

    for par in range(2):
        @pl.when(rest % 2 == par)
        def _(par=par):
            consume_diagonal(par)

    acc = acc_ref[...]
    acc0, acc1 = acc[:, :HEAD_PAIR], acc[:, HEAD_PAIR:]
    o_ref[...] = jnp.where(first, acc0 / acc0[:, HEAD_DIM:HEAD_DIM + 1], acc1 / acc1[:, 0:1])


def _attn_call(q, kt, v, ct):
    B, L, _ = q.shape
    t = ATTN_TILE
    c4 = ct.reshape(B, N_PAIRS, 2, L)
    pair_block = lambda *shape: pl.BlockSpec((None, 2) + shape, lambda b, p, i: (b, p, 0, 0))
    return pl.pallas_call(
        _attn_kernel,
        grid=(B, N_PAIRS, L // t),
        in_specs=[pl.BlockSpec((None, t, HEAD_PAIR), lambda b, p, i: (b, i, p)),
                  pair_block(HEAD_PAIR, L), pair_block(L, HEAD_PAIR),
                  pl.BlockSpec((None, None, 2, L), lambda b, p, i: (b, p, 0, 0))],
        out_specs=pl.BlockSpec((None, t, HEAD_PAIR), lambda b, p, i: (b, i, p)),
        out_shape=jax.ShapeDtypeStruct((B, L, ATTN_WIDTH), F32),
        scratch_shapes=[pltpu.VMEM((t, t), F32)] * 4
                       + [pltpu.VMEM((2, t, LANES), F32), pltpu.VMEM((t, 2 * HEAD_PAIR), F32)],
        compiler_params=pltpu.CompilerParams(
            dimension_semantics=("arbitrary", "arbitrary", "arbitrary"),
            vmem_limit_bytes=VMEM_LIMIT),
        name="attn",
    )(q, kt, v, c4)


def _gelu_tanh(x):
    return 0.5 * x * (1.0 + jnp.tanh(math.sqrt(2.0 / math.pi) * (x + 0.044715 * (x * x * x))))


def _ssm_kernel(e_ref, mt_ref, w1t_ref, wot_ref, pw_ref, f_ref, *, chunks_per_seq):
    e = e_ref[...]
    y = jnp.dot(mt_ref[...], e, preferred_element_type=F32)
    s = jnp.dot(w1t_ref[...], e, preferred_element_type=F32)
    sr, si = s[:SSM_STATE], s[SSM_STATE:]
    pos = lax.broadcasted_iota(jnp.int32, sr.shape, 1) % chunks_per_seq

    def shifted(a, shift):
        return jnp.where(pos >= shift, pltpu.roll(a, shift, 1), 0.0)

    shift, step = 1, 0
    while shift < chunks_per_seq:
        ar, ai = pw_ref[step, :SSM_STATE], pw_ref[step, SSM_STATE:]
        srs, sis = shifted(sr, shift), shifted(si, shift)
        sr, si = sr + ar * srs - ai * sis, si + ar * sis + ai * srs
        shift *= 2
        step += 1
    x_prev = jnp.concatenate([shifted(sr, 1), shifted(si, 1)], axis=0).astype(BF16)
    y = y + jnp.dot(wot_ref[...], x_prev, preferred_element_type=F32)
    f_ref[...] = _gelu_tanh(y).reshape(f_ref.shape)


def _ssm_call(e, mt, w1t, wot, pw, chunks_per_seq):
    G, TH, NC = e.shape
    T = TH // SSM_GROUP_CH
    grp = lambda *shape: pl.BlockSpec((None,) + shape, lambda g: (g,) + (0,) * len(shape))
    return pl.pallas_call(
        functools.partial(_ssm_kernel, chunks_per_seq=chunks_per_seq),
        grid=(G,),
        in_specs=[grp(TH, NC), grp(TH, TH), grp(2 * SSM_STATE, TH), grp(TH, 2 * SSM_STATE),
                  grp(pw.shape[1], 2 * SSM_STATE, 1)],
        out_specs=pl.BlockSpec((T, SSM_GROUP_CH, NC), lambda g: (0, g, 0)),
        out_shape=jax.ShapeDtypeStruct((T, SSM_WIDTH, NC), F32),
        compiler_params=pltpu.CompilerParams(
            dimension_semantics=("arbitrary",), vmem_limit_bytes=VMEM_LIMIT),
        name="ssm",
    )(e, mt, w1t, wot, pw)


def _ssm_operators(a_re, a_im, log_dt, b_re, b_im, c_re, c_im, d_skip, n_scan_steps):
    T, P, H = SSM_CHUNK, SSM_STATE, SSM_GROUP_CH
    hi = lax.Precision.HIGHEST
    dt = jnp.exp(log_dt)[:, None]
    lam_r, lam_i = dt * a_re, dt * a_im
    j = jnp.arange(T + 1, dtype=F32)[None, :, None]
    mag = jnp.exp(j * lam_r[:, None, :])
    pr, pi = mag * jnp.cos(j * lam_i[:, None, :]), mag * jnp.sin(j * lam_i[:, None, :])
    abar_r, abar_i = pr[:, 1], pi[:, 1]
    nr, ni = abar_r - 1.0, abar_i
    den = a_re * a_re + a_im * a_im
    fr, fi = (nr * a_re + ni * a_im) / den, (ni * a_re - nr * a_im) / den
    bbar_r = fr[..., None] * b_re - fi[..., None] * b_im
    bbar_i = fr[..., None] * b_im + fi[..., None] * b_re
    cer = c_re[:, None] * pr[:, :T, None, :] - c_im[:, None] * pi[:, :T, None, :]
    cei = c_re[:, None] * pi[:, :T, None, :] + c_im[:, None] * pr[:, :T, None, :]
    kj = (jnp.einsum('gjhp,gpk->gjhk', cer, bbar_r, precision=hi)
          - jnp.einsum('gjhp,gpk->gjhk', cei, bbar_i, precision=hi))
    kj = kj.at[:, 0].add(jax.vmap(jnp.diag)(d_skip))
    lag = jnp.arange(T)[:, None] - jnp.arange(T)[None, :]
    toe = jnp.where((lag >= 0)[None, :, :, None, None],
                    kj[:, jnp.clip(lag, 0, T - 1)], 0.0)
    mt = toe.transpose(0, 1, 3, 2, 4).reshape(-1, T * H, T * H)
    wr = pr[:, T - 1::-1][:, :T]
    wi = pi[:, T - 1::-1][:, :T]
    w1r = wr[..., None] * bbar_r[:, None] - wi[..., None] * bbar_i[:, None]
    w1i = wr[..., None] * bbar_i[:, None] + wi[..., None] * bbar_r[:, None]
    w1t = jnp.concatenate([w1r, w1i], axis=2).transpose(0, 2, 1, 3).reshape(-1, 2 * P, T * H)
    cr1 = c_re[:, None] * pr[:, 1:, None, :] - c_im[:, None] * pi[:, 1:, None, :]
    ci1 = c_re[:, None] * pi[:, 1:, None, :] + c_im[:, None] * pr[:, 1:, None, :]
    wot = jnp.concatenate([cr1, -ci1], axis=-1).reshape(-1, T * H, 2 * P)
    qr, qi = pr[:, T], pi[:, T]
    pows = []
    for _ in range(n_scan_steps):
        pows.append(jnp.concatenate([qr, qi], axis=-1))
        qr, qi = qr * qr - qi * qi, 2.0 * qr * qi
    pw = jnp.stack(pows, axis=1)[..., None]
    return mt.astype(BF16), w1t.astype(BF16), wot.astype(BF16), pw


def _tail_kernel(h1_ref, attn_ref, y_ref, p_ref, wglu_ref, bglu_ref, ga_ref, gs_ref,
                 woa_ref, wos_ref, g2_ref, w1_ref, w3_ref, w2_ref, gp_ref, wpg_ref,
                 wpp_ref, gf_ref, o_ref):
    y = y_ref[...]
    glu = y * _sigmoid(jnp.dot(y.astype(BF16), wglu_ref[...], preferred_element_type=F32)
                       + bglu_ref[...])
    an = _rms(attn_ref[...], ga_ref[...]).astype(BF16)
    sn = _rms(glu, gs_ref[...]).astype(BF16)
    h = (h1_ref[...] + jnp.dot(an, woa_ref[...], preferred_element_type=F32)
         + jnp.dot(sn, wos_ref[...], preferred_element_type=F32))
    h = h + 0.5 * _swiglu(_rms(h, g2_ref[...]).astype(BF16), w1_ref, w3_ref, w2_ref)
    gate = _sigmoid(jnp.dot(_rms(h, gp_ref[...]).astype(BF16), wpg_ref[...],
                            preferred_element_type=F32))
    h = h + gate * jnp.dot(p_ref[...].astype(BF16), wpp_ref[...], preferred_element_type=F32)
    o_ref[...] = _rms(h, gf_ref[...])


def _tail_call(h1, attn, y, p, *consts):
    B, L, D = h1.shape
    tm = TOKEN_TILE
    tile = lambda w: pl.BlockSpec((None, tm, w), lambda b, i: (b, i, 0))
    return pl.pallas_call(
        _tail_kernel,
        grid=(B, L // tm),
        in_specs=[tile(D), tile(ATTN_WIDTH), tile(SSM_WIDTH), tile(PLE_DIM)]
                 + [_const_spec(c.shape) for c in consts],
        out_specs=tile(D),
        out_shape=jax.ShapeDtypeStruct((B, L, D), F32),
        compiler_params=pltpu.CompilerParams(
            dimension_semantics=("arbitrary", "arbitrary"),
            vmem_limit_bytes=VMEM_LIMIT),
        name="tail",
    )(h1, attn, y, p, *consts)


def kernel(x, p, g_ffn1, w1_a, w3_a, w2_a, g_mix, w_in, b_f, a_re, a_im, log_dt, b_re, b_im, c_re, c_im, d_skip, w_glu, b_glu, g_attn_out, g_ssm_out, w_out, g_ffn2, w1_b, w3_b, w2_b, g_ple, w_ple_gate, w_ple_proj, g_final):
    B, L, D = x.shape
    assert D == D_MODEL and L % ATTN_TILE == 0 and L % TOKEN_TILE == 0 and L % SSM_CHUNK == 0
    assert g_ffn1.shape[0] == 1, "single layer"
    assert TOKEN_TILE == ATTN_TILE, "decay bias rows are relative to the kv tile start"
    row = lambda g: g.reshape(1, -1).astype(F32)
    bf = lambda w: w.astype(BF16)
    s_v, s_f = 3 * ATTN_WIDTH, 3 * ATTN_WIDTH + ATTN_HEADS
    w_in0 = w_in[0]
    scale = LOG2E / math.sqrt(HEAD_DIM)
    wqkv = jnp.concatenate([w_in0[:, :ATTN_WIDTH] * scale, w_in0[:, ATTN_WIDTH:s_v]], axis=1)
    wf = jnp.pad(w_in0[:, s_v:s_f], ((0, 0), (0, LANES - ATTN_HEADS)))

    h1, q, kt, v, s_in, ct = _head_call(
        x, row(g_ffn1[0]), bf(w1_a[0]), bf(w3_a[0]), bf(w2_a[0]), row(g_mix[0]),
        bf(wqkv), bf(w_in0[:, s_f:]), bf(wf), b_f[0].reshape(ATTN_HEADS, 1).astype(F32))

    attn = _attn_call(q, kt, v, ct)

    T = SSM_CHUNK
    chunks_per_seq = L // T
    n_chunks = B * chunks_per_seq
    n_scan_steps = max(1, (chunks_per_seq - 1).bit_length())
    mt, w1t, wot, pw = _ssm_operators(a_re[0], a_im[0], log_dt[0], b_re[0], b_im[0],
                                      c_re[0], c_im[0], d_skip[0], n_scan_steps)
    e = (s_in.reshape(n_chunks, T, SSM_GROUPS, SSM_GROUP_CH)
         .transpose(2, 1, 3, 0).reshape(SSM_GROUPS, T * SSM_GROUP_CH, n_chunks))
    f = _ssm_call(e, mt, w1t, wot, pw, chunks_per_seq)
    y = f.transpose(2, 0, 1).reshape(B, L, SSM_WIDTH)

    w_out0 = w_out[0]
    return _tail_call(
        h1, attn, y, p[0],
        bf(w_glu[0]), row(b_glu[0]), row(g_attn_out[0]), row(g_ssm_out[0]),
        bf(w_out0[:ATTN_WIDTH]), bf(w_out0[ATTN_WIDTH:]), row(g_ffn2[0]),
        bf(w1_b[0]), bf(w3_b[0]), bf(w2_b[0]), row(g_ple[0]), bf(w_ple_gate[0]),
        bf(w_ple_proj[0]), row(g_final))
```

```python
import functools
import math

import jax
import jax.numpy as jnp
from jax import lax
from jax.experimental import pallas as pl
from jax.experimental.pallas import tpu as pltpu

D_MODEL = 1024
ATTN_HEADS = 8
HEAD_DIM = 64
ATTN_WIDTH = ATTN_HEADS * HEAD_DIM
SSM_WIDTH = D_MODEL - ATTN_WIDTH
SSM_GROUP_CH = 16
SSM_GROUPS = SSM_WIDTH // SSM_GROUP_CH
SSM_STATE = 64
D_FF = 2816
PLE_DIM = 256
EPS = 1e-6

LANES = 128
HEAD_PAIR = 2 * HEAD_DIM
N_PAIRS = ATTN_HEADS // 2
FF_CHUNK = 256
TOKEN_TILE = 512
ATTN_TILE = 512
SSM_CHUNK = 32
NEG_BIG = -1e30
LOG2E = math.log2(math.e)
N_BIAS = 3
BIAS_ROWS = 8
VMEM_LIMIT = 56 * 1024 * 1024

BF16 = jnp.bfloat16
F32 = jnp.float32


def _rms(x, g):
    ms = jnp.mean(x * x, axis=-1, keepdims=True)
    return x * lax.rsqrt(ms + EPS) * g


def _sigmoid(x):
    return 1.0 / (1.0 + jnp.exp(-x))


def _swiglu(xn, w1_ref, w3_ref, w2_ref):
    acc = None
    for c in range(D_FF // FF_CHUNK):
        sl = slice(c * FF_CHUNK, (c + 1) * FF_CHUNK)
        a = jnp.dot(xn, w1_ref[:, sl], preferred_element_type=F32)
        b = jnp.dot(xn, w3_ref[:, sl], preferred_element_type=F32)
        gated = (a * _sigmoid(a) * b).astype(BF16)
        part = jnp.dot(gated, w2_ref[sl, :], preferred_element_type=F32)
        acc = part if acc is None else acc + part
    return acc


def _const_spec(shape):
    nd = len(shape)
    return pl.BlockSpec(shape, lambda *_: (0,) * nd, pipeline_mode=pl.Buffered(1))


def _head_kernel(x_ref, g1_ref, w1_ref, w3_ref, w2_ref, gm_ref, wqkv_ref, ws_ref,
                 wf_ref, bf_ref, h1_ref, q_ref, kt_ref, v_ref, s_ref, ct_ref,
                 carry_ref):
    tm = x_ref.shape[0]
    x = x_ref[...]
    h1 = x + 0.5 * _swiglu(_rms(x, g1_ref[...]).astype(BF16), w1_ref, w3_ref, w2_ref)
    h1_ref[...] = h1
    un = _rms(h1, gm_ref[...]).astype(BF16)
    qkv = jnp.dot(un, wqkv_ref[...], preferred_element_type=F32)
    q_ref[...] = qkv[:, :ATTN_WIDTH].astype(BF16)
    s_ref[...] = jnp.dot(un, ws_ref[...], preferred_element_type=F32).astype(BF16)

    zf = jnp.dot(un, wf_ref[...], preferred_element_type=F32)
    zft = zf.T[:ATTN_HEADS, :] + bf_ref[...]
    logf = jnp.minimum(zft, 0.0) - jnp.log1p(jnp.exp(-jnp.abs(zft)))
    lane = lax.broadcasted_iota(jnp.int32, logf.shape, 1)
    c = logf
    shift = 1
    while shift < tm:
        c = c + jnp.where(lane >= shift, pltpu.roll(c, shift, 1), 0.0)
        shift *= 2

    @pl.when(pl.program_id(1) == 0)
    def _():
        carry_ref[...] = jnp.zeros_like(carry_ref)

    c_abs = c + carry_ref[:, 0:1]
    ct_ref[...] = c_abs * LOG2E
    carry_ref[...] = jnp.broadcast_to(c_abs[:, tm - 1:tm], carry_ref.shape)

    rel = (c - c[:, 0:1]) * LOG2E
    hi = rel.astype(BF16).astype(F32)
    mid = (rel - hi).astype(BF16).astype(F32)
    lo = (rel - hi - mid).astype(BF16).astype(F32)
    kt = qkv[:, ATTN_WIDTH:2 * ATTN_WIDTH].T
    vv = qkv[:, 2 * ATTN_WIDTH:]
    sub = lax.broadcasted_iota(jnp.int32, (BIAS_ROWS, tm), 0)
    zeros = jnp.zeros((HEAD_DIM - BIAS_ROWS, tm), F32)
    vlane = lax.broadcasted_iota(jnp.int32, (tm, HEAD_PAIR), 1)
    for h in range(ATTN_HEADS):
        bias = jnp.where(sub == 0, -hi[h:h + 1],
                         jnp.where(sub == 1, -mid[h:h + 1],
                                   jnp.where(sub == 2, -lo[h:h + 1], 0.0)))
        k_h = kt[h * HEAD_DIM:(h + 1) * HEAD_DIM]
        vp = vv[:, (h // 2) * HEAD_PAIR:(h // 2 + 1) * HEAD_PAIR]
        if h % 2 == 0:
            kt_ref[h] = jnp.concatenate([k_h, bias, zeros], axis=0).astype(BF16)
            v_ref[h] = jnp.where(vlane < HEAD_DIM, vp,
                                 jnp.where(vlane == HEAD_DIM, 1.0, 0.0)).astype(BF16)
        else:
            kt_ref[h] = jnp.concatenate([bias, zeros, k_h], axis=0).astype(BF16)
            v_ref[h] = jnp.where(vlane >= HEAD_DIM, vp,
                                 jnp.where(vlane == 0, 1.0, 0.0)).astype(BF16)


def _head_call(x, g1, w1, w3, w2, gm, wqkv, ws, wf, bf):
    B, L, D = x.shape
    tm = TOKEN_TILE
    tile = lambda w: pl.BlockSpec((None, tm, w), lambda b, i: (b, i, 0))
    out_shape = (
        jax.ShapeDtypeStruct((B, L, D), F32),
        jax.ShapeDtypeStruct((B, L, ATTN_WIDTH), BF16),
        jax.ShapeDtypeStruct((B, ATTN_HEADS, HEAD_PAIR, L), BF16),
        jax.ShapeDtypeStruct((B, ATTN_HEADS, L, HEAD_PAIR), BF16),
        jax.ShapeDtypeStruct((B, L, SSM_WIDTH), BF16),
        jax.ShapeDtypeStruct((B, ATTN_HEADS, L), F32),
    )
    return pl.pallas_call(
        _head_kernel,
        grid=(B, L // tm),
        in_specs=[tile(D), _const_spec(g1.shape), _const_spec(w1.shape),
                  _const_spec(w3.shape), _const_spec(w2.shape), _const_spec(gm.shape),
                  _const_spec(wqkv.shape), _const_spec(ws.shape), _const_spec(wf.shape),
                  _const_spec(bf.shape)],
        out_specs=(tile(D), tile(ATTN_WIDTH),
                   pl.BlockSpec((None, ATTN_HEADS, HEAD_PAIR, tm), lambda b, i: (b, 0, 0, i)),
                   pl.BlockSpec((None, ATTN_HEADS, tm, HEAD_PAIR), lambda b, i: (b, 0, i, 0)),
                   tile(SSM_WIDTH),
                   pl.BlockSpec((None, ATTN_HEADS, tm), lambda b, i: (b, 0, i))),
        out_shape=out_shape,
        scratch_shapes=[pltpu.VMEM((ATTN_HEADS, LANES), F32)],
        compiler_params=pltpu.CompilerParams(
            dimension_semantics=("arbitrary", "arbitrary"),
            vmem_limit_bytes=VMEM_LIMIT),
        name="head",
    )(x, g1, w1, w3, w2, gm, wqkv, ws, wf, bf)


def _attn_kernel(q_ref, kt_ref, v_ref, c_ref, o_ref, s_e0, s_e1, s_o0, s_o1, m_ref, acc_ref):
    t = q_ref.shape[0]
    n = pl.program_id(2)
    q = q_ref[...]
    lane = lax.broadcasted_iota(jnp.int32, (1, HEAD_PAIR), 1)
    first = lane < HEAD_DIM
    ones_even = jnp.where((lane >= HEAD_DIM) & (lane < HEAD_DIM + N_BIAS), 1.0, 0.0).astype(BF16)
    ones_odd = jnp.where(lane < N_BIAS, 1.0, 0.0).astype(BF16)
    q_heads = (jnp.where(first, q, ones_even), jnp.where(first, ones_odd, q))
    s_buf = ((s_e0, s_e1), (s_o0, s_o1))

    def start(kv):
        return pl.multiple_of(kv * t, t)

    c_q = [c_ref[h:h + 1, pl.ds(start(n), LANES)][:, 0:1] for h in range(2)]

    def qk(h, kv):
        return jnp.dot(q_heads[h], kt_ref[h, :, pl.ds(start(kv), t)], preferred_element_type=F32)

    def consume(h, s, kv):
        d = c_ref[h:h + 1, pl.ds(start(kv), LANES)][:, 0:1] - c_q[h]
        m_old = m_ref[h]
        m_new = jnp.maximum(m_old, jnp.max(s, axis=-1, keepdims=True) - d)
        p = jnp.exp2(s - (m_new + d)).astype(BF16)
        m_ref[h] = m_new
        acc_ref[h] = jnp.exp2(m_old - m_new) * acc_ref[h] + jnp.dot(
            p, v_ref[h, pl.ds(start(kv), t), :], preferred_element_type=F32)

    def step(kv, par):
        for h in range(2):
            s_buf[1 - par][h][...] = qk(h, kv + 1)
        for h in range(2):
            consume(h, s_buf[par][h][...], kv)

    def consume_diagonal(par):
        row = lax.broadcasted_iota(jnp.int32, (t, t), 0)
        col = lax.broadcasted_iota(jnp.int32, (t, t), 1)
        for h in range(2):
            consume(h, jnp.where(col <= row, s_buf[par][h][...], NEG_BIG), n)

    m_ref[...] = jnp.full(m_ref.shape, NEG_BIG, F32)
    acc_ref[...] = jnp.zeros(acc_ref.shape, F32)
    for h in range(2):
        s_buf[0][h][...] = qk(h, 0)

    def pair(i, carry):
        step(2 * i, 0)
        step(2 * i + 1, 1)
        return carry

    lax.fori_loop(0, n // 2, pair, 0)

    @pl.when(n % 2 == 1)
    def _():
        step(n - 1, 0)
        consume_diagonal(1)

    @pl.when(n % 2 == 0)
    def _():
        consume_diagonal(0)

    acc0, acc1 = acc_ref[0], acc_ref[1]
    o_ref[...] = jnp.where(first, acc0 / acc0[:, HEAD_DIM:HEAD_DIM + 1], acc1 / acc1[:, 0:1])


def _attn_call(q, kt, v, ct):
    B, L, _ = q.shape
    t = ATTN_TILE
    c4 = ct.reshape(B, N_PAIRS, 2, L)
    pair_block = lambda *shape: pl.BlockSpec((None, 2) + shape, lambda b, p, i: (b, p, 0, 0))
    return pl.pallas_call(
        _attn_kernel,
        grid=(B, N_PAIRS, L // t),
        in_specs=[pl.BlockSpec((None, t, HEAD_PAIR), lambda b, p, i: (b, i, p)),
                  pair_block(HEAD_PAIR, L), pair_block(L, HEAD_PAIR),
                  pl.BlockSpec((None, None, 2, L), lambda b, p, i: (b, p, 0, 0))],
        out_specs=pl.BlockSpec((None, t, HEAD_PAIR), lambda b, p, i: (b, i, p)),
        out_shape=jax.ShapeDtypeStruct((B, L, ATTN_WIDTH), F32),
        scratch_shapes=[pltpu.VMEM((t, t), F32)] * 4
                       + [pltpu.VMEM((2, t, 1), F32), pltpu.VMEM((2, t, HEAD_PAIR), F32)],
        compiler_params=pltpu.CompilerParams(
            dimension_semantics=("arbitrary", "arbitrary", "arbitrary"),
            vmem_limit_bytes=VMEM_LIMIT),
        name="attn",
    )(q, kt, v, c4)


def _gelu_tanh(x):
    return 0.5 * x * (1.0 + jnp.tanh(math.sqrt(2.0 / math.pi) * (x + 0.044715 * (x * x * x))))


def _cis(mag_arg, ang):
    mag = jnp.exp(mag_arg)
    return mag * jnp.cos(ang), mag * jnp.sin(ang)


def _cmul(ar, ai, br, bi):
    return ar * br - ai * bi, ar * bi + ai * br


def _cpow2(zr, zi, n):
    assert n & (n - 1) == 0
    while n > 1:
        zr, zi = zr * zr - zi * zi, 2.0 * zr * zi
        n //= 2
    return zr, zi


def _ssm_kernel(arow_ref, acol_ref, ldt_ref, bt_ref, cab_ref, dcol_ref, e_ref, f_ref, z_ref,
                *, chunks_per_seq):
    T, P, H = SSM_CHUNK, SSM_STATE, SSM_GROUP_CH
    TH = T * H
    hi = lax.Precision.HIGHEST
    dt = jnp.exp(ldt_ref[...])

    lam_r, lam_i = dt * arow_ref[0:1, :], dt * arow_ref[1:2, :]
    j0 = lax.broadcasted_iota(jnp.int32, (T, 2 * P), 0).astype(F32)
    pa0, pb0 = _cis(j0 * lam_r, j0 * lam_i)
    pa1, pb1 = _cmul(pa0, pb0, *_cis(lam_r, lam_i))
    over_h = lambda a: jnp.concatenate(
        [jnp.broadcast_to(a[j:j + 1, :], (H, 2 * P)) for j in range(T)], axis=0)
    ca, cb = jnp.tile(cab_ref[0], (T, 1)), jnp.tile(cab_ref[1], (T, 1))
    c_pow0 = over_h(pa0) * ca + over_h(pb0) * cb
    c_pow1 = over_h(pa1) * ca + over_h(pb1) * cb

    a_r, a_i = acol_ref[:, 0:1], acol_ref[:, 1:2]
    lr, li = dt * a_r, dt * a_i
    abar_r, abar_i = _cis(lr, li)
    nr, ni = abar_r - 1.0, abar_i
    den = a_r * a_r + a_i * a_i
    fr, fi = (nr * a_r + ni * a_i) / den, (ni * a_r - nr * a_i) / den
    b_r, b_i = bt_ref[0], bt_ref[1]
    bb_r, bb_i = fr * b_r - fi * b_i, fr * b_i + fi * b_r

    kcol = jnp.dot(c_pow0, jnp.concatenate([bb_r, bb_i], axis=0), precision=hi,
                   preferred_element_type=F32)
    lane_h = lax.broadcasted_iota(jnp.int32, (H, LANES), 1) % H
    skip = jnp.where(lane_h == lax.broadcasted_iota(jnp.int32, (H, LANES), 0), dcol_ref[...], 0.0)

    @pl.when(pl.program_id(0) == 0)
    def _():
        z_ref[0:TH, :] = jnp.zeros((TH, LANES), F32)

    z_ref[TH:2 * TH, :] = kcol
    z_ref[TH:TH + H, :] = kcol[:H] + skip
    lane_group = lax.broadcasted_iota(jnp.int32, (1, LANES), 1) // H
    groups_per_block = LANES // H
    blocks = []
    for v in range(TH // LANES):
        blk = None
        for u in range(groups_per_block):
            s = v * groups_per_block + u
            piece = z_ref[TH - H * s:2 * TH - H * s, :]
            blk = piece if blk is None else jnp.where(lane_group == u, piece, blk)
        blocks.append(blk.astype(BF16))
    mt = jnp.concatenate(blocks, axis=1)

    expo = (groups_per_block - 1 - lane_group).astype(F32)
    wr, wi = _cis(lr * expo, li * expo)
    hop_r, hop_i = _cpow2(abar_r, abar_i, groups_per_block)
    w1_r, w1_i = [], []
    for v in range(TH // LANES):
        w1_r.insert(0, wr * bb_r - wi * bb_i)
        w1_i.insert(0, wr * bb_i + wi * bb_r)
        wr, wi = _cmul(wr, wi, hop_r, hop_i)
    w1t = jnp.concatenate([jnp.concatenate(w1_r, axis=1),
                           jnp.concatenate(w1_i, axis=1)], axis=0).astype(BF16)

    e = e_ref[...]
    y = jnp.dot(mt, e, preferred_element_type=F32)
    st = jnp.dot(w1t, e, preferred_element_type=F32)
    sr, si = st[:P], st[P:]
    pos = lax.broadcasted_iota(jnp.int32, sr.shape, 1) % chunks_per_seq

    def shifted(a, shift):
        return jnp.where(pos >= shift, pltpu.roll(a, shift, 1), 0.0)

    qr, qi = _cpow2(abar_r, abar_i, T)
    shift = 1
    while shift < chunks_per_seq:
        srs, sis = shifted(sr, shift), shifted(si, shift)
        sr, si = sr + qr * srs - qi * sis, si + qr * sis + qi * srs
        qr, qi = qr * qr - qi * qi, 2.0 * qr * qi
        shift *= 2
    x_prev = jnp.concatenate([shifted(sr, 1), shifted(si, 1)], axis=0).astype(BF16)
    y = y + jnp.dot(c_pow1.astype(BF16), x_prev, preferred_element_type=F32)
    f_ref[...] = _gelu_tanh(y).reshape(f_ref.shape)


def _ssm_call(arow, acol, ldt, bt, cab, dcol, e, chunks_per_seq):
    G, TH, NC = e.shape
    T = TH // SSM_GROUP_CH
    grp = lambda a: pl.BlockSpec((None,) + a.shape[1:], lambda g: (g,) + (0,) * (a.ndim - 1))
    return pl.pallas_call(
        functools.partial(_ssm_kernel, chunks_per_seq=chunks_per_seq),
        grid=(G,),
        in_specs=[grp(a) for a in (arow, acol, ldt, bt, cab, dcol, e)],
        out_specs=pl.BlockSpec((T, SSM_GROUP_CH, NC), lambda g: (0, g, 0)),
        out_shape=jax.ShapeDtypeStruct((T, SSM_WIDTH, NC), F32),
        scratch_shapes=[pltpu.VMEM((2 * TH, LANES), F32)],
        compiler_params=pltpu.CompilerParams(
            dimension_semantics=("arbitrary",), vmem_limit_bytes=VMEM_LIMIT),
        name="ssm",
    )(arow, acol, ldt, bt, cab, dcol, e)


def _ssm_param_layouts(a_re, a_im, log_dt, b_re, b_im, c_re, c_im, d_skip):
    G = a_re.shape[0]
    arow = jnp.stack([jnp.concatenate([a_re, a_re], -1), jnp.concatenate([a_im, a_im], -1)], 1)
    acol = jnp.stack([a_re, a_im], -1)
    reps = LANES // SSM_GROUP_CH
    bt = jnp.stack([jnp.tile(b_re, (1, 1, reps)), jnp.tile(b_im, (1, 1, reps))], 1)
    cab = jnp.stack([jnp.concatenate([c_re, -c_im], -1), jnp.concatenate([-c_im, -c_re], -1)], 1)
    return (arow.astype(F32), acol.astype(F32), log_dt.reshape(G, 1, 1).astype(F32),
            bt.astype(F32), cab.astype(F32), d_skip.reshape(G, SSM_GROUP_CH, 1).astype(F32))


def _tail_kernel(h1_ref, attn_ref, y_ref, p_ref, wglu_ref, bglu_ref, ga_ref, gs_ref,
                 woa_ref, wos_ref, g2_ref, w1_ref, w3_ref, w2_ref, gp_ref, wpg_ref,
                 wpp_ref, gf_ref, o_ref):
    y = y_ref[...]
    glu = y * _sigmoid(jnp.dot(y.astype(BF16), wglu_ref[...], preferred_element_type=F32)
                       + bglu_ref[...])
    an = _rms(attn_ref[...], ga_ref[...]).astype(BF16)
    sn = _rms(glu, gs_ref[...]).astype(BF16)
    h = (h1_ref[...] + jnp.dot(an, woa_ref[...], preferred_element_type=F32)
         + jnp.dot(sn, wos_ref[...], preferred_element_type=F32))
    h = h + 0.5 * _swiglu(_rms(h, g2_ref[...]).astype(BF16), w1_ref, w3_ref, w2_ref)
    gate = _sigmoid(jnp.dot(_rms(h, gp_ref[...]).astype(BF16), wpg_ref[...],
                            preferred_element_type=F32))
    h = h + gate * jnp.dot(p_ref[...].astype(BF16), wpp_ref[...], preferred_element_type=F32)
    o_ref[...] = _rms(h, gf_ref[...])


def _tail_call(h1, attn, y, p, *consts):
    B, L, D = h1.shape
    tm = TOKEN_TILE
    tile = lambda w: pl.BlockSpec((None, tm, w), lambda b, i: (b, i, 0))
    return pl.pallas_call(
        _tail_kernel,
        grid=(B, L // tm),
        in_specs=[tile(D), tile(ATTN_WIDTH), tile(SSM_WIDTH), tile(PLE_DIM)]
                 + [_const_spec(c.shape) for c in consts],
        out_specs=tile(D),
        out_shape=jax.ShapeDtypeStruct((B, L, D), F32),
        compiler_params=pltpu.CompilerParams(
            dimension_semantics=("arbitrary", "arbitrary"),
            vmem_limit_bytes=VMEM_LIMIT),
        name="tail",
    )(h1, attn, y, p, *consts)


def kernel(x, p, g_ffn1, w1_a, w3_a, w2_a, g_mix, w_in, b_f, a_re, a_im, log_dt, b_re, b_im, c_re, c_im, d_skip, w_glu, b_glu, g_attn_out, g_ssm_out, w_out, g_ffn2, w1_b, w3_b, w2_b, g_ple, w_ple_gate, w_ple_proj, g_final):
    B, L, D = x.shape
    assert D == D_MODEL and L % ATTN_TILE == 0 and L % TOKEN_TILE == 0 and L % SSM_CHUNK == 0
    assert g_ffn1.shape[0] == 1, "single layer"
    assert TOKEN_TILE == ATTN_TILE, "decay bias rows are relative to the kv tile start"
    row = lambda g: g.reshape(1, -1).astype(F32)
    bf = lambda w: w.astype(BF16)
    s_v, s_f = 3 * ATTN_WIDTH, 3 * ATTN_WIDTH + ATTN_HEADS
    w_in0 = w_in[0]
    scale = LOG2E / math.sqrt(HEAD_DIM)
    wqkv = jnp.concatenate([w_in0[:, :ATTN_WIDTH] * scale, w_in0[:, ATTN_WIDTH:s_v]], axis=1)
    wf = jnp.pad(w_in0[:, s_v:s_f], ((0, 0), (0, LANES - ATTN_HEADS)))

    h1, q, kt, v, s_in, ct = _head_call(
        x, row(g_ffn1[0]), bf(w1_a[0]), bf(w3_a[0]), bf(w2_a[0]), row(g_mix[0]),
        bf(wqkv), bf(w_in0[:, s_f:]), bf(wf), b_f[0].reshape(ATTN_HEADS, 1).astype(F32))

    attn = _attn_call(q, kt, v, ct)

    T = SSM_CHUNK
    chunks_per_seq = L // T
    n_chunks = B * chunks_per_seq
    e = (s_in.reshape(n_chunks, T, SSM_GROUPS, SSM_GROUP_CH)
         .transpose(2, 1, 3, 0).reshape(SSM_GROUPS, T * SSM_GROUP_CH, n_chunks))
    f = _ssm_call(*_ssm_param_layouts(a_re[0], a_im[0], log_dt[0], b_re[0], b_im[0],
                                      c_re[0], c_im[0], d_skip[0]),
                  e, chunks_per_seq)
    y = f.transpose(2, 0, 1).reshape(B, L, SSM_WIDTH)

    w_out0 = w_out[0]
    return _tail_call(
        h1, attn, y, p[0],
        bf(w_glu[0]), row(b_glu[0]), row(g_attn_out[0]), row(g_ssm_out[0]),
        bf(w_out0[:ATTN_WIDTH]), bf(w_out0[ATTN_WIDTH:]), row(g_ffn2[0]),
        bf(w1_b[0]), bf(w3_b[0]), bf(w2_b[0]), row(g_ple[0]), bf(w_ple_gate[0]),
        bf(w_ple_proj[0]), row(g_final))
```

```python
import functools
import math

import jax
import jax.numpy as jnp
from jax import lax
from jax.experimental import pallas as pl
from jax.experimental.pallas import tpu as pltpu

D_MODEL = 1024
ATTN_HEADS = 8
HEAD_DIM = 64
ATTN_WIDTH = ATTN_HEADS * HEAD_DIM
SSM_WIDTH = D_MODEL - ATTN_WIDTH
SSM_GROUP_CH = 16
SSM_GROUPS = SSM_WIDTH // SSM_GROUP_CH
SSM_STATE = 64
D_FF = 2816
PLE_DIM = 256
EPS = 1e-6

LANES = 128
HEAD_PAIR = 2 * HEAD_DIM
N_PAIRS = ATTN_HEADS // 2
FF_CHUNK = 256
TOKEN_TILE = 512
ATTN_TILE = 512
STEPS_PER_ITER = 4
SSM_CHUNK = 32
NEG_BIG = -1e30
LOG2E = math.log2(math.e)
N_BIAS = 3
BIAS_ROWS = 8
VMEM_LIMIT = 56 * 1024 * 1024

BF16 = jnp.bfloat16
F32 = jnp.float32


def _rms(x, g):
    ms = jnp.mean(x * x, axis=-1, keepdims=True)
    return x * lax.rsqrt(ms + EPS) * g


def _sigmoid(x):
    return 1.0 / (1.0 + jnp.exp(-x))


def _swiglu(xn, w1_ref, w3_ref, w2_ref):
    acc = None
    for c in range(D_FF // FF_CHUNK):
        sl = slice(c * FF_CHUNK, (c + 1) * FF_CHUNK)
        a = jnp.dot(xn, w1_ref[:, sl], preferred_element_type=F32)
        b = jnp.dot(xn, w3_ref[:, sl], preferred_element_type=F32)
        gated = (a * _sigmoid(a) * b).astype(BF16)
        part = jnp.dot(gated, w2_ref[sl, :], preferred_element_type=F32)
        acc = part if acc is None else acc + part
    return acc


def _const_spec(shape):
    nd = len(shape)
    return pl.BlockSpec(shape, lambda *_: (0,) * nd, pipeline_mode=pl.Buffered(1))


def _head_kernel(x_ref, g1_ref, w1_ref, w3_ref, w2_ref, gm_ref, wqkv_ref, ws_ref,
                 wf_ref, bf_ref, h1_ref, q_ref, kt_ref, v_ref, s_ref, ct_ref,
                 carry_ref):
    tm = x_ref.shape[0]
    x = x_ref[...]
    h1 = x + 0.5 * _swiglu(_rms(x, g1_ref[...]).astype(BF16), w1_ref, w3_ref, w2_ref)
    h1_ref[...] = h1
    un = _rms(h1, gm_ref[...]).astype(BF16)
    qkv = jnp.dot(un, wqkv_ref[...], preferred_element_type=F32)
    q_ref[...] = qkv[:, :ATTN_WIDTH].astype(BF16)
    s_ref[...] = jnp.dot(un, ws_ref[...], preferred_element_type=F32).astype(BF16)

    zf = jnp.dot(un, wf_ref[...], preferred_element_type=F32)
    zft = zf.T[:ATTN_HEADS, :] + bf_ref[...]
    logf = jnp.minimum(zft, 0.0) - jnp.log1p(jnp.exp(-jnp.abs(zft)))
    lane = lax.broadcasted_iota(jnp.int32, logf.shape, 1)
    c = logf
    shift = 1
    while shift < tm:
        c = c + jnp.where(lane >= shift, pltpu.roll(c, shift, 1), 0.0)
        shift *= 2

    @pl.when(pl.program_id(1) == 0)
    def _():
        carry_ref[...] = jnp.zeros_like(carry_ref)

    c_abs = c + carry_ref[:, 0:1]
    ct_ref[...] = c_abs * LOG2E
    carry_ref[...] = jnp.broadcast_to(c_abs[:, tm - 1:tm], carry_ref.shape)

    rel = (c - c[:, 0:1]) * LOG2E
    hi = rel.astype(BF16).astype(F32)
    mid = (rel - hi).astype(BF16).astype(F32)
    lo = (rel - hi - mid).astype(BF16).astype(F32)
    kt = qkv[:, ATTN_WIDTH:2 * ATTN_WIDTH].T
    vv = qkv[:, 2 * ATTN_WIDTH:]
    sub = lax.broadcasted_iota(jnp.int32, (BIAS_ROWS, tm), 0)
    zeros = jnp.zeros((HEAD_DIM - BIAS_ROWS, tm), F32)
    vlane = lax.broadcasted_iota(jnp.int32, (tm, HEAD_PAIR), 1)
    for h in range(ATTN_HEADS):
        bias = jnp.where(sub == 0, -hi[h:h + 1],
                         jnp.where(sub == 1, -mid[h:h + 1],
                                   jnp.where(sub == 2, -lo[h:h + 1], 0.0)))
        k_h = kt[h * HEAD_DIM:(h + 1) * HEAD_DIM]
        vp = vv[:, (h // 2) * HEAD_PAIR:(h // 2 + 1) * HEAD_PAIR]
        if h % 2 == 0:
            kt_ref[h] = jnp.concatenate([k_h, bias, zeros], axis=0).astype(BF16)
            v_ref[h] = jnp.where(vlane < HEAD_DIM, vp,
                                 jnp.where(vlane == HEAD_DIM, 1.0, 0.0)).astype(BF16)
        else:
            kt_ref[h] = jnp.concatenate([bias, zeros, k_h], axis=0).astype(BF16)
            v_ref[h] = jnp.where(vlane >= HEAD_DIM, vp,
                                 jnp.where(vlane == 0, 1.0, 0.0)).astype(BF16)


def _head_call(x, g1, w1, w3, w2, gm, wqkv, ws, wf, bf):
    B, L, D = x.shape
    tm = TOKEN_TILE
    tile = lambda w: pl.BlockSpec((None, tm, w), lambda b, i: (b, i, 0))
    out_shape = (
        jax.ShapeDtypeStruct((B, L, D), F32),
        jax.ShapeDtypeStruct((B, L, ATTN_WIDTH), BF16),
        jax.ShapeDtypeStruct((B, ATTN_HEADS, HEAD_PAIR, L), BF16),
        jax.ShapeDtypeStruct((B, ATTN_HEADS, L, HEAD_PAIR), BF16),
        jax.ShapeDtypeStruct((B, L, SSM_WIDTH), BF16),
        jax.ShapeDtypeStruct((B, ATTN_HEADS, L), F32),
    )
    return pl.pallas_call(
        _head_kernel,
        grid=(B, L // tm),
        in_specs=[tile(D), _const_spec(g1.shape), _const_spec(w1.shape),
                  _const_spec(w3.shape), _const_spec(w2.shape), _const_spec(gm.shape),
                  _const_spec(wqkv.shape), _const_spec(ws.shape), _const_spec(wf.shape),
                  _const_spec(bf.shape)],
        out_specs=(tile(D), tile(ATTN_WIDTH),
                   pl.BlockSpec((None, ATTN_HEADS, HEAD_PAIR, tm), lambda b, i: (b, 0, 0, i)),
                   pl.BlockSpec((None, ATTN_HEADS, tm, HEAD_PAIR), lambda b, i: (b, 0, i, 0)),
                   tile(SSM_WIDTH),
                   pl.BlockSpec((None, ATTN_HEADS, tm), lambda b, i: (b, 0, i))),
        out_shape=out_shape,
        scratch_shapes=[pltpu.VMEM((ATTN_HEADS, LANES), F32)],
        compiler_params=pltpu.CompilerParams(
            dimension_semantics=("arbitrary", "arbitrary"),
            vmem_limit_bytes=VMEM_LIMIT),
        name="head",
    )(x, g1, w1, w3, w2, gm, wqkv, ws, wf, bf)


def _attn_kernel(q_ref, kt_ref, v_ref, c_ref, o_ref, s_e0, s_e1, s_o0, s_o1, m_ref, acc_ref):
    t = q_ref.shape[0]
    n = pl.program_id(2)
    q = q_ref[...]
    lane = lax.broadcasted_iota(jnp.int32, (1, HEAD_PAIR), 1)
    first = lane < HEAD_DIM
    ones_even = jnp.where((lane >= HEAD_DIM) & (lane < HEAD_DIM + N_BIAS), 1.0, 0.0).astype(BF16)
    ones_odd = jnp.where(lane < N_BIAS, 1.0, 0.0).astype(BF16)
    q_heads = (jnp.where(first, q, ones_even), jnp.where(first, ones_odd, q))
    s_buf = ((s_e0, s_e1), (s_o0, s_o1))

    def start(kv):
        return pl.multiple_of(kv * t, t)

    c_q = [c_ref[h:h + 1, pl.ds(start(n), LANES)][:, 0:1] for h in range(2)]

    def qk(h, kv):
        return jnp.dot(q_heads[h], kt_ref[h, :, pl.ds(start(kv), t)], preferred_element_type=F32)

    def consume(h, s, kv):
        d = c_ref[h:h + 1, pl.ds(start(kv), LANES)][:, 0:1] - c_q[h]
        m_old = m_ref[h]
        m_new = jnp.maximum(m_old, jnp.max(s, axis=-1, keepdims=True) - d)
        p = jnp.exp2((s - (m_new + d)).astype(BF16))
        m_ref[h] = m_new
        acc_ref[h] = jnp.exp2(m_old - m_new) * acc_ref[h] + jnp.dot(
            p, v_ref[h, pl.ds(start(kv), t), :], preferred_element_type=F32)

    def step(kv, par):
        for h in range(2):
            s_buf[1 - par][h][...] = qk(h, kv + 1)
        for h in range(2):
            consume(h, s_buf[par][h][...], kv)

    def consume_diagonal(par):
        row = lax.broadcasted_iota(jnp.int32, (t, t), 0)
        col = lax.broadcasted_iota(jnp.int32, (t, t), 1)
        for h in range(2):
            consume(h, jnp.where(col <= row, s_buf[par][h][...], NEG_BIG), n)

    m_ref[...] = jnp.full(m_ref.shape, NEG_BIG, F32)
    acc_ref[...] = jnp.zeros(acc_ref.shape, F32)
    for h in range(2):
        s_buf[0][h][...] = qk(h, 0)

    def unrolled(i, carry):
        for u in range(STEPS_PER_ITER):
            step(STEPS_PER_ITER * i + u, u % 2)
        return carry

    lax.fori_loop(0, n // STEPS_PER_ITER, unrolled, 0)
    rest = n % STEPS_PER_ITER
    for u in range(STEPS_PER_ITER - 1):
        @pl.when(rest > u)
        def _(u=u):
            step(n - rest + u, u % 2)

    for par in range(2):
        @pl.when(rest % 2 == par)
        def _(par=par):
            consume_diagonal(par)

    acc0, acc1 = acc_ref[0], acc_ref[1]
    o_ref[...] = jnp.where(first, acc0 / acc0[:, HEAD_DIM:HEAD_DIM + 1], acc1 / acc1[:, 0:1])


def _attn_call(q, kt, v, ct):
    B, L, _ = q.shape
    t = ATTN_TILE
    c4 = ct.reshape(B, N_PAIRS, 2, L)
    pair_block = lambda *shape: pl.BlockSpec((None, 2) + shape, lambda b, p, i: (b, p, 0, 0))
    return pl.pallas_call(
        _attn_kernel,
        grid=(B, N_PAIRS, L // t),
        in_specs=[pl.BlockSpec((None, t, HEAD_PAIR), lambda b, p, i: (b, i, p)),
                  pair_block(HEAD_PAIR, L), pair_block(L, HEAD_PAIR),
                  pl.BlockSpec((None, None, 2, L), lambda b, p, i: (b, p, 0, 0))],
        out_specs=pl.BlockSpec((None, t, HEAD_PAIR), lambda b, p, i: (b, i, p)),
        out_shape=jax.ShapeDtypeStruct((B, L, ATTN_WIDTH), F32),
        scratch_shapes=[pltpu.VMEM((t, t), F32)] * 4
                       + [pltpu.VMEM((2, t, 1), F32), pltpu.VMEM((2, t, HEAD_PAIR), F32)],
        compiler_params=pltpu.CompilerParams(
            dimension_semantics=("arbitrary", "arbitrary", "arbitrary"),
            vmem_limit_bytes=VMEM_LIMIT),
        name="attn",
    )(q, kt, v, c4)


def _gelu_tanh(x):
    return 0.5 * x * (1.0 + jnp.tanh(math.sqrt(2.0 / math.pi) * (x + 0.044715 * (x * x * x))))


def _cis(mag_arg, ang):
    mag = jnp.exp(mag_arg)
    return mag * jnp.cos(ang), mag * jnp.sin(ang)


def _cmul(ar, ai, br, bi):
    return ar * br - ai * bi, ar * bi + ai * br


def _cpow2(zr, zi, n):
    assert n & (n - 1) == 0
    while n > 1:
        zr, zi = zr * zr - zi * zi, 2.0 * zr * zi
        n //= 2
    return zr, zi


def _ssm_kernel(arow_ref, acol_ref, ldt_ref, bt_ref, cab_ref, dcol_ref, e_ref, f_ref, z_ref,
                *, chunks_per_seq):
    T, P, H = SSM_CHUNK, SSM_STATE, SSM_GROUP_CH
    TH = T * H
    hi = lax.Precision.HIGHEST
    dt = jnp.exp(ldt_ref[...])

    lam_r, lam_i = dt * arow_ref[0:1, :], dt * arow_ref[1:2, :]
    j0 = lax.broadcasted_iota(jnp.int32, (T, 2 * P), 0).astype(F32)
    pa0, pb0 = _cis(j0 * lam_r, j0 * lam_i)
    pa1, pb1 = _cmul(pa0, pb0, *_cis(lam_r, lam_i))
    over_h = lambda a: jnp.concatenate(
        [jnp.broadcast_to(a[j:j + 1, :], (H, 2 * P)) for j in range(T)], axis=0)
    ca, cb = jnp.tile(cab_ref[0], (T, 1)), jnp.tile(cab_ref[1], (T, 1))
    c_pow0 = over_h(pa0) * ca + over_h(pb0) * cb
    c_pow1 = over_h(pa1) * ca + over_h(pb1) * cb

    a_r, a_i = acol_ref[:, 0:1], acol_ref[:, 1:2]
    lr, li = dt * a_r, dt * a_i
    abar_r, abar_i = _cis(lr, li)
    nr, ni = abar_r - 1.0, abar_i
    den = a_r * a_r + a_i * a_i
    fr, fi = (nr * a_r + ni * a_i) / den, (ni * a_r - nr * a_i) / den
    b_r, b_i = bt_ref[0], bt_ref[1]
    bb_r, bb_i = fr * b_r - fi * b_i, fr * b_i + fi * b_r

    kcol = jnp.dot(c_pow0, jnp.concatenate([bb_r, bb_i], axis=0), precision=hi,
                   preferred_element_type=F32)
    lane_h = lax.broadcasted_iota(jnp.int32, (H, LANES), 1) % H
    skip = jnp.where(lane_h == lax.broadcasted_iota(jnp.int32, (H, LANES), 0), dcol_ref[...], 0.0)

    @pl.when(pl.program_id(0) == 0)
    def _():
        z_ref[0:TH, :] = jnp.zeros((TH, LANES), F32)

    z_ref[TH:2 * TH, :] = kcol
    z_ref[TH:TH + H, :] = kcol[:H] + skip
    lane_group = lax.broadcasted_iota(jnp.int32, (1, LANES), 1) // H
    groups_per_block = LANES // H
    blocks = []
    for v in range(TH // LANES):
        blk = None
        for u in range(groups_per_block):
            s = v * groups_per_block + u
            piece = z_ref[TH - H * s:2 * TH - H * s, :]
            blk = piece if blk is None else jnp.where(lane_group == u, piece, blk)
        blocks.append(blk.astype(BF16))
    mt = jnp.concatenate(blocks, axis=1)

    expo = (groups_per_block - 1 - lane_group).astype(F32)
    wr, wi = _cis(lr * expo, li * expo)
    hop_r, hop_i = _cpow2(abar_r, abar_i, groups_per_block)
    w1_r, w1_i = [], []
    for v in range(TH // LANES):
        w1_r.insert(0, wr * bb_r - wi * bb_i)
        w1_i.insert(0, wr * bb_i + wi * bb_r)
        wr, wi = _cmul(wr, wi, hop_r, hop_i)
    w1t = jnp.concatenate([jnp.concatenate(w1_r, axis=1),
                           jnp.concatenate(w1_i, axis=1)], axis=0).astype(BF16)

    e = e_ref[...]
    y = jnp.dot(mt, e, preferred_element_type=F32)
    st = jnp.dot(w1t, e, preferred_element_type=F32)
    sr, si = st[:P], st[P:]
    pos = lax.broadcasted_iota(jnp.int32, sr.shape, 1) % chunks_per_seq

    def shifted(a, shift):
        return jnp.where(pos >= shift, pltpu.roll(a, shift, 1), 0.0)

    qr, qi = _cpow2(abar_r, abar_i, T)
    shift = 1
    while shift < chunks_per_seq:
        srs, sis = shifted(sr, shift), shifted(si, shift)
        sr, si = sr + qr * srs - qi * sis, si + qr * sis + qi * srs
        qr, qi = qr * qr - qi * qi, 2.0 * qr * qi
        shift *= 2
    x_prev = jnp.concatenate([shifted(sr, 1), shifted(si, 1)], axis=0).astype(BF16)
    y = y + jnp.dot(c_pow1.astype(BF16), x_prev, preferred_element_type=F32)
    f_ref[...] = _gelu_tanh(y).reshape(f_ref.shape)


def _ssm_call(arow, acol, ldt, bt, cab, dcol, e, chunks_per_seq):
    G, TH, NC = e.shape
    T = TH // SSM_GROUP_CH
    grp = lambda a: pl.BlockSpec((None,) + a.shape[1:], lambda g: (g,) + (0,) * (a.ndim - 1))
    return pl.pallas_call(
        functools.partial(_ssm_kernel, chunks_per_seq=chunks_per_seq),
        grid=(G,),
        in_specs=[grp(a) for a in (arow, acol, ldt, bt, cab, dcol, e)],
        out_specs=pl.BlockSpec((T, SSM_GROUP_CH, NC), lambda g: (0, g, 0)),
        out_shape=jax.ShapeDtypeStruct((T, SSM_WIDTH, NC), F32),
        scratch_shapes=[pltpu.VMEM((2 * TH, LANES), F32)],
        compiler_params=pltpu.CompilerParams(
            dimension_semantics=("arbitrary",), vmem_limit_bytes=VMEM_LIMIT),
        name="ssm",
    )(arow, acol, ldt, bt, cab, dcol, e)


def _ssm_param_layouts(a_re, a_im, log_dt, b_re, b_im, c_re, c_im, d_skip):
    G = a_re.shape[0]
    arow = jnp.stack([jnp.concatenate([a_re, a_re], -1), jnp.concatenate([a_im, a_im], -1)], 1)
    acol = jnp.stack([a_re, a_im], -1)
    reps = LANES // SSM_GROUP_CH
    bt = jnp.stack([jnp.tile(b_re, (1, 1, reps)), jnp.tile(b_im, (1, 1, reps))], 1)
    cab = jnp.stack([jnp.concatenate([c_re, -c_im], -1), jnp.concatenate([-c_im, -c_re], -1)], 1)
    return (arow.astype(F32), acol.astype(F32), log_dt.reshape(G, 1, 1).astype(F32),
            bt.astype(F32), cab.astype(F32), d_skip.reshape(G, SSM_GROUP_CH, 1).astype(F32))


def _tail_kernel(h1_ref, attn_ref, y_ref, p_ref, wglu_ref, bglu_ref, ga_ref, gs_ref,
                 woa_ref, wos_ref, g2_ref, w1_ref, w3_ref, w2_ref, gp_ref, wpg_ref,
                 wpp_ref, gf_ref, o_ref):
    y = y_ref[...]
    glu = y * _sigmoid(jnp.dot(y.astype(BF16), wglu_ref[...], preferred_element_type=F32)
                       + bglu_ref[...])
    an = _rms(attn_ref[...], ga_ref[...]).astype(BF16)
    sn = _rms(glu, gs_ref[...]).astype(BF16)
    h = (h1_ref[...] + jnp.dot(an, woa_ref[...], preferred_element_type=F32)
         + jnp.dot(sn, wos_ref[...], preferred_element_type=F32))
    h = h + 0.5 * _swiglu(_rms(h, g2_ref[...]).astype(BF16), w1_ref, w3_ref, w2_ref)
    gate = _sigmoid(jnp.dot(_rms(h, gp_ref[...]).astype(BF16), wpg_ref[...],
                            preferred_element_type=F32))
    h = h + gate * jnp.dot(p_ref[...].astype(BF16), wpp_ref[...], preferred_element_type=F32)
    o_ref[...] = _rms(h, gf_ref[...])


def _tail_call(h1, attn, y, p, *consts):
    B, L, D = h1.shape
    tm = TOKEN_TILE
    tile = lambda w: pl.BlockSpec((None, tm, w), lambda b, i: (b, i, 0))
    return pl.pallas_call(
        _tail_kernel,
        grid=(B, L // tm),
        in_specs=[tile(D), tile(ATTN_WIDTH), tile(SSM_WIDTH), tile(PLE_DIM)]
                 + [_const_spec(c.shape) for c in consts],
        out_specs=tile(D),
        out_shape=jax.ShapeDtypeStruct((B, L, D), F32),
        compiler_params=pltpu.CompilerParams(
            dimension_semantics=("arbitrary", "arbitrary"),
            vmem_limit_bytes=VMEM_LIMIT),
        name="tail",
    )(h1, attn, y, p, *consts)


def kernel(x, p, g_ffn1, w1_a, w3_a, w2_a, g_mix, w_in, b_f, a_re, a_im, log_dt, b_re, b_im, c_re, c_im, d_skip, w_glu, b_glu, g_attn_out, g_ssm_out, w_out, g_ffn2, w1_b, w3_b, w2_b, g_ple, w_ple_gate, w_ple_proj, g_final):
    B, L, D = x.shape
    assert D == D_MODEL and L % ATTN_TILE == 0 and L % TOKEN_TILE == 0 and L % SSM_CHUNK == 0
    assert g_ffn1.shape[0] == 1, "single layer"
    assert TOKEN_TILE == ATTN_TILE, "decay bias rows are relative to the kv tile start"
    row = lambda g: g.reshape(1, -1).astype(F32)
    bf = lambda w: w.astype(BF16)
    s_v, s_f = 3 * ATTN_WIDTH, 3 * ATTN_WIDTH + ATTN_HEADS
    w_in0 = w_in[0]
    scale = LOG2E / math.sqrt(HEAD_DIM)
    wqkv = jnp.concatenate([w_in0[:, :ATTN_WIDTH] * scale, w_in0[:, ATTN_WIDTH:s_v]], axis=1)
    wf = jnp.pad(w_in0[:, s_v:s_f], ((0, 0), (0, LANES - ATTN_HEADS)))

    h1, q, kt, v, s_in, ct = _head_call(
        x, row(g_ffn1[0]), bf(w1_a[0]), bf(w3_a[0]), bf(w2_a[0]), row(g_mix[0]),
        bf(wqkv), bf(w_in0[:, s_f:]), bf(wf), b_f[0].reshape(ATTN_HEADS, 1).astype(F32))

    attn = _attn_call(q, kt, v, ct)

    T = SSM_CHUNK
    chunks_per_seq = L // T
    n_chunks = B * chunks_per_seq
    e = (s_in.reshape(n_chunks, T, SSM_GROUPS, SSM_GROUP_CH)
         .transpose(2, 1, 3, 0).reshape(SSM_GROUPS, T * SSM_GROUP_CH, n_chunks))
    f = _ssm_call(*_ssm_param_layouts(a_re[0], a_im[0], log_dt[0], b_re[0], b_im[0],
                                      c_re[0], c_im[0], d_skip[0]),
                  e, chunks_per_seq)
    y = f.transpose(2, 0, 1).reshape(B, L, SSM_WIDTH)

    w_out0 = w_out[0]
    return _tail_call(
        h1, attn, y, p[0],
        bf(w_glu[0]), row(b_glu[0]), row(g_attn_out[0]), row(g_ssm_out[0]),
        bf(w_out0[:ATTN_WIDTH]), bf(w_out0[ATTN_WIDTH:]), row(g_ffn2[0]),
        bf(w1_b[0]), bf(w3_b[0]), bf(w2_b[0]), row(g_ple[0]), bf(w_ple_gate[0]),
        bf(w_ple_proj[0]), row(g_final))
```

```python
import functools
import math

import jax
import jax.numpy as jnp
from jax import lax
from jax.experimental import pallas as pl
from jax.experimental.pallas import tpu as pltpu

D_MODEL = 1024
ATTN_HEADS = 8
HEAD_DIM = 64
ATTN_WIDTH = ATTN_HEADS * HEAD_DIM
SSM_WIDTH = D_MODEL - ATTN_WIDTH
SSM_GROUP_CH = 16
SSM_GROUPS = SSM_WIDTH // SSM_GROUP_CH
SSM_STATE = 64
D_FF = 2816
PLE_DIM = 256
EPS = 1e-6

LANES = 128
HEAD_PAIR = 2 * HEAD_DIM
N_PAIRS = ATTN_HEADS // 2
FF_CHUNK = 256
TOKEN_TILE = 512
ATTN_TILE = 512
STEPS_PER_ITER = 4
SSM_CHUNK = 32
NEG_BIG = -1e30
SKIP_LOG2 = 140.0
BOUND_SLACK_MUL = 1.001
BOUND_SLACK_ADD = 1.0
LOG2E = math.log2(math.e)
N_BIAS = 3
BIAS_ROWS = 8
VMEM_LIMIT = 56 * 1024 * 1024

BF16 = jnp.bfloat16
F32 = jnp.float32


def _rms(x, g):
    ms = jnp.mean(x * x, axis=-1, keepdims=True)
    return x * lax.rsqrt(ms + EPS) * g


def _sigmoid(x):
    return 1.0 / (1.0 + jnp.exp(-x))


def _swiglu(xn, w1_ref, w3_ref, w2_ref):
    acc = None
    for c in range(D_FF // FF_CHUNK):
        sl = slice(c * FF_CHUNK, (c + 1) * FF_CHUNK)
        a = jnp.dot(xn, w1_ref[:, sl], preferred_element_type=F32)
        b = jnp.dot(xn, w3_ref[:, sl], preferred_element_type=F32)
        gated = (a * _sigmoid(a) * b).astype(BF16)
        part = jnp.dot(gated, w2_ref[sl, :], preferred_element_type=F32)
        acc = part if acc is None else acc + part
    return acc


def _const_spec(shape):
    nd = len(shape)
    return pl.BlockSpec(shape, lambda *_: (0,) * nd, pipeline_mode=pl.Buffered(1))


def _head_kernel(x_ref, g1_ref, w1_ref, w3_ref, w2_ref, gm_ref, wqkv_ref, ws_ref,
                 wf_ref, bf_ref, h1_ref, q_ref, kt_ref, v_ref, s_ref, ct_ref, kn_ref,
                 carry_ref):
    tm = x_ref.shape[0]
    x = x_ref[...]
    h1 = x + 0.5 * _swiglu(_rms(x, g1_ref[...]).astype(BF16), w1_ref, w3_ref, w2_ref)
    h1_ref[...] = h1
    un = _rms(h1, gm_ref[...]).astype(BF16)
    qkv = jnp.dot(un, wqkv_ref[...], preferred_element_type=F32)
    q_ref[...] = qkv[:, :ATTN_WIDTH].astype(BF16)
    s_ref[...] = jnp.dot(un, ws_ref[...], preferred_element_type=F32).astype(BF16)

    zf = jnp.dot(un, wf_ref[...], preferred_element_type=F32)
    zft = zf.T[:ATTN_HEADS, :] + bf_ref[...]
    logf = jnp.minimum(zft, 0.0) - jnp.log1p(jnp.exp(-jnp.abs(zft)))
    lane = lax.broadcasted_iota(jnp.int32, logf.shape, 1)
    c = logf
    shift = 1
    while shift < tm:
        c = c + jnp.where(lane >= shift, pltpu.roll(c, shift, 1), 0.0)
        shift *= 2

    @pl.when(pl.program_id(1) == 0)
    def _():
        carry_ref[...] = jnp.zeros_like(carry_ref)

    c_abs = c + carry_ref[:, 0:1]
    ct_ref[...] = c_abs * LOG2E
    carry_ref[...] = jnp.broadcast_to(c_abs[:, tm - 1:tm], carry_ref.shape)

    rel = (c - c[:, 0:1]) * LOG2E
    hi = rel.astype(BF16).astype(F32)
    mid = (rel - hi).astype(BF16).astype(F32)
    lo = (rel - hi - mid).astype(BF16).astype(F32)
    kt = qkv[:, ATTN_WIDTH:2 * ATTN_WIDTH].astype(BF16).astype(F32).T
    vv = qkv[:, 2 * ATTN_WIDTH:]
    k_sq = (kt * kt).reshape(ATTN_HEADS, HEAD_DIM, tm)
    kn_ref[...] = jnp.broadcast_to(
        jnp.max(jnp.sum(k_sq, axis=1), axis=-1, keepdims=True), kn_ref.shape)
    sub = lax.broadcasted_iota(jnp.int32, (BIAS_ROWS, tm), 0)
    zeros = jnp.zeros((HEAD_DIM - BIAS_ROWS, tm), F32)
    vlane = lax.broadcasted_iota(jnp.int32, (tm, HEAD_PAIR), 1)
    for h in range(ATTN_HEADS):
        bias = jnp.where(sub == 0, -hi[h:h + 1],
                         jnp.where(sub == 1, -mid[h:h + 1],
                                   jnp.where(sub == 2, -lo[h:h + 1], 0.0)))
        k_h = kt[h * HEAD_DIM:(h + 1) * HEAD_DIM]
        vp = vv[:, (h // 2) * HEAD_PAIR:(h // 2 + 1) * HEAD_PAIR]
        if h % 2 == 0:
            kt_ref[h] = jnp.concatenate([k_h, bias, zeros], axis=0).astype(BF16)
            v_ref[h] = jnp.where(vlane < HEAD_DIM, vp,
                                 jnp.where(vlane == HEAD_DIM, 1.0, 0.0)).astype(BF16)
        else:
            kt_ref[h] = jnp.concatenate([bias, zeros, k_h], axis=0).astype(BF16)
            v_ref[h] = jnp.where(vlane >= HEAD_DIM, vp,
                                 jnp.where(vlane == 0, 1.0, 0.0)).astype(BF16)


def _head_call(x, g1, w1, w3, w2, gm, wqkv, ws, wf, bf):
    B, L, D = x.shape
    tm = TOKEN_TILE
    tile = lambda w: pl.BlockSpec((None, tm, w), lambda b, i: (b, i, 0))
    out_shape = (
        jax.ShapeDtypeStruct((B, L, D), F32),
        jax.ShapeDtypeStruct((B, L, ATTN_WIDTH), BF16),
        jax.ShapeDtypeStruct((B, ATTN_HEADS, HEAD_PAIR, L), BF16),
        jax.ShapeDtypeStruct((B, ATTN_HEADS, L, HEAD_PAIR), BF16),
        jax.ShapeDtypeStruct((B, L, SSM_WIDTH), BF16),
        jax.ShapeDtypeStruct((B, ATTN_HEADS, L), F32),
        jax.ShapeDtypeStruct((B, L // tm, ATTN_HEADS, LANES), F32),
    )
    return pl.pallas_call(
        _head_kernel,
        grid=(B, L // tm),
        in_specs=[tile(D), _const_spec(g1.shape), _const_spec(w1.shape),
                  _const_spec(w3.shape), _const_spec(w2.shape), _const_spec(gm.shape),
                  _const_spec(wqkv.shape), _const_spec(ws.shape), _const_spec(wf.shape),
                  _const_spec(bf.shape)],
        out_specs=(tile(D), tile(ATTN_WIDTH),
                   pl.BlockSpec((None, ATTN_HEADS, HEAD_PAIR, tm), lambda b, i: (b, 0, 0, i)),
                   pl.BlockSpec((None, ATTN_HEADS, tm, HEAD_PAIR), lambda b, i: (b, 0, i, 0)),
                   tile(SSM_WIDTH),
                   pl.BlockSpec((None, ATTN_HEADS, tm), lambda b, i: (b, 0, i)),
                   pl.BlockSpec((None, None, ATTN_HEADS, LANES), lambda b, i: (b, i, 0, 0))),
        out_shape=out_shape,
        scratch_shapes=[pltpu.VMEM((ATTN_HEADS, LANES), F32)],
        compiler_params=pltpu.CompilerParams(
            dimension_semantics=("arbitrary", "arbitrary"),
            vmem_limit_bytes=VMEM_LIMIT),
        name="head",
    )(x, g1, w1, w3, w2, gm, wqkv, ws, wf, bf)


def _attn_kernel(q_ref, kt_ref, v_ref, c_ref, st_ref, o_ref, s_e0, s_e1, s_o0, s_o1, m_ref, acc_ref):
    t = q_ref.shape[0]
    n = pl.program_id(2)
    q = q_ref[...]
    lane = lax.broadcasted_iota(jnp.int32, (1, HEAD_PAIR), 1)
    first = lane < HEAD_DIM
    ones_even = jnp.where((lane >= HEAD_DIM) & (lane < HEAD_DIM + N_BIAS), 1.0, 0.0).astype(BF16)
    ones_odd = jnp.where(lane < N_BIAS, 1.0, 0.0).astype(BF16)
    q_heads = (jnp.where(first, q, ones_even), jnp.where(first, ones_odd, q))
    s_buf = ((s_e0, s_e1), (s_o0, s_o1))

    def start(kv):
        return pl.multiple_of(kv * t, t)

    c_q = [c_ref[h:h + 1, pl.ds(start(n), LANES)][:, 0:1] for h in range(2)]

    def qk(h, kv):
        return jnp.dot(q_heads[h], kt_ref[h, :, pl.ds(start(kv), t)], preferred_element_type=F32)

    def consume(h, s, kv):
        d = c_ref[h:h + 1, pl.ds(start(kv), LANES)][:, 0:1] - c_q[h]
        m_old = m_ref[h]
        m_new = jnp.maximum(m_old, jnp.max(s, axis=-1, keepdims=True) - d)
        p = jnp.exp2((s - (m_new + d)).astype(BF16))
        m_ref[h] = m_new
        acc_ref[h] = jnp.exp2(m_old - m_new) * acc_ref[h] + jnp.dot(
            p, v_ref[h, pl.ds(start(kv), t), :], preferred_element_type=F32)

    def step(kv, par):
        for h in range(2):
            s_buf[1 - par][h][...] = qk(h, kv + 1)
        for h in range(2):
            consume(h, s_buf[par][h][...], kv)

    m_ref[...] = jnp.full(m_ref.shape, NEG_BIG, F32)
    acc_ref[...] = jnp.zeros(acc_ref.shape, F32)
    row = lax.broadcasted_iota(jnp.int32, (t, t), 0)
    col = lax.broadcasted_iota(jnp.int32, (t, t), 1)
    for h in range(2):
        consume(h, jnp.where(col <= row, qk(h, n), NEG_BIG), n)

    tile_id = lax.broadcasted_iota(jnp.int32, (1, LANES), 1)
    q_sq = q.astype(F32) * q.astype(F32)
    first_needed = []
    for h in range(2):
        q_norm2 = jnp.max(jnp.sum(jnp.where(first == (h == 0), q_sq, 0.0), axis=-1, keepdims=True))
        k_norm2, c_next = st_ref[h:h + 1, :], st_ref[2 + h:3 + h, :]
        bound = (jnp.sqrt(q_norm2 * k_norm2) * BOUND_SLACK_MUL + BOUND_SLACK_ADD
                 + c_q[h] - c_next - jnp.min(m_ref[h]))
        needed = (tile_id < n) & (bound >= -SKIP_LOG2)
        first_needed.append(jnp.min(jnp.where(needed, tile_id, n)))
    j0 = jnp.minimum(first_needed[0], first_needed[1])
    count = n - j0

    for h in range(2):
        s_buf[0][h][...] = qk(h, j0)

    def unrolled(i, carry):
        for u in range(STEPS_PER_ITER):
            step(j0 + STEPS_PER_ITER * i + u, u % 2)
        return carry

    lax.fori_loop(0, count // STEPS_PER_ITER, unrolled, 0)
    rest = count % STEPS_PER_ITER
    for u in range(STEPS_PER_ITER - 1):
        @pl.when(rest > u)
        def _(u=u):
            step(n - rest + u, u % 2)

    acc0, acc1 = acc_ref[0], acc_ref[1]
    o_ref[...] = jnp.where(first, acc0 / acc0[:, HEAD_DIM:HEAD_DIM + 1], acc1 / acc1[:, 0:1])


def _attn_call(q, kt, v, ct, k_norm2):
    B, L, _ = q.shape
    t = ATTN_TILE
    n_tiles = L // t
    assert n_tiles <= LANES
    c4 = ct.reshape(B, N_PAIRS, 2, L)
    c_next = jnp.roll(ct[:, :, ::t], -1, axis=-1)
    per_tile = jnp.stack([k_norm2[..., 0].transpose(0, 2, 1), c_next], axis=1)
    per_tile = jnp.pad(per_tile, ((0, 0),) * 3 + ((0, LANES - n_tiles),))
    stats = (per_tile.reshape(B, 2, N_PAIRS, 2, LANES).transpose(0, 2, 1, 3, 4)
             .reshape(B, N_PAIRS, 4, LANES))
    pair_block = lambda *shape: pl.BlockSpec((None, 2) + shape, lambda b, p, i: (b, p, 0, 0))
    return pl.pallas_call(
        _attn_kernel,
        grid=(B, N_PAIRS, L // t),
        in_specs=[pl.BlockSpec((None, t, HEAD_PAIR), lambda b, p, i: (b, i, p)),
                  pair_block(HEAD_PAIR, L), pair_block(L, HEAD_PAIR),
                  pl.BlockSpec((None, None, 2, L), lambda b, p, i: (b, p, 0, 0)),
                  pl.BlockSpec((None, None, 4, LANES), lambda b, p, i: (b, p, 0, 0))],
        out_specs=pl.BlockSpec((None, t, HEAD_PAIR), lambda b, p, i: (b, i, p)),
        out_shape=jax.ShapeDtypeStruct((B, L, ATTN_WIDTH), F32),
        scratch_shapes=[pltpu.VMEM((t, t), F32)] * 4
                       + [pltpu.VMEM((2, t, 1), F32), pltpu.VMEM((2, t, HEAD_PAIR), F32)],
        compiler_params=pltpu.CompilerParams(
            dimension_semantics=("arbitrary", "arbitrary", "arbitrary"),
            vmem_limit_bytes=VMEM_LIMIT),
        name="attn",
    )(q, kt, v, c4, stats)


def _gelu_tanh(x):
    return 0.5 * x * (1.0 + jnp.tanh(math.sqrt(2.0 / math.pi) * (x + 0.044715 * (x * x * x))))


def _cis(mag_arg, ang):
    mag = jnp.exp(mag_arg)
    return mag * jnp.cos(ang), mag * jnp.sin(ang)


def _cmul(ar, ai, br, bi):
    return ar * br - ai * bi, ar * bi + ai * br


def _cpow2(zr, zi, n):
    assert n & (n - 1) == 0
    while n > 1:
        zr, zi = zr * zr - zi * zi, 2.0 * zr * zi
        n //= 2
    return zr, zi


def _ssm_kernel(arow_ref, acol_ref, ldt_ref, bt_ref, cab_ref, dcol_ref, e_ref, f_ref, z_ref,
                *, chunks_per_seq):
    T, P, H = SSM_CHUNK, SSM_STATE, SSM_GROUP_CH
    TH = T * H
    hi = lax.Precision.HIGHEST
    dt = jnp.exp(ldt_ref[...])

    lam_r, lam_i = dt * arow_ref[0:1, :], dt * arow_ref[1:2, :]
    j0 = lax.broadcasted_iota(jnp.int32, (T, 2 * P), 0).astype(F32)
    pa0, pb0 = _cis(j0 * lam_r, j0 * lam_i)
    pa1, pb1 = _cmul(pa0, pb0, *_cis(lam_r, lam_i))
    over_h = lambda a: jnp.concatenate(
        [jnp.broadcast_to(a[j:j + 1, :], (H, 2 * P)) for j in range(T)], axis=0)
    ca, cb = jnp.tile(cab_ref[0], (T, 1)), jnp.tile(cab_ref[1], (T, 1))
    c_pow0 = over_h(pa0) * ca + over_h(pb0) * cb
    c_pow1 = over_h(pa1) * ca + over_h(pb1) * cb

    a_r, a_i = acol_ref[:, 0:1], acol_ref[:, 1:2]
    lr, li = dt * a_r, dt * a_i
    abar_r, abar_i = _cis(lr, li)
    nr, ni = abar_r - 1.0, abar_i
    den = a_r * a_r + a_i * a_i
    fr, fi = (nr * a_r + ni * a_i) / den, (ni * a_r - nr * a_i) / den
    b_r, b_i = bt_ref[0], bt_ref[1]
    bb_r, bb_i = fr * b_r - fi * b_i, fr * b_i + fi * b_r

    kcol = jnp.dot(c_pow0, jnp.concatenate([bb_r, bb_i], axis=0), precision=hi,
                   preferred_element_type=F32)
    lane_h = lax.broadcasted_iota(jnp.int32, (H, LANES), 1) % H
    skip = jnp.where(lane_h == lax.broadcasted_iota(jnp.int32, (H, LANES), 0), dcol_ref[...], 0.0)

    @pl.when(pl.program_id(0) == 0)
    def _():
        z_ref[0:TH, :] = jnp.zeros((TH, LANES), F32)

    z_ref[TH:2 * TH, :] = kcol
    z_ref[TH:TH + H, :] = kcol[:H] + skip
    lane_group = lax.broadcasted_iota(jnp.int32, (1, LANES), 1) // H
    groups_per_block = LANES // H
    blocks = []
    for v in range(TH // LANES):
        blk = None
        for u in range(groups_per_block):
            s = v * groups_per_block + u
            piece = z_ref[TH - H * s:2 * TH - H * s, :]
            blk = piece if blk is None else jnp.where(lane_group == u, piece, blk)
        blocks.append(blk.astype(BF16))
    mt = jnp.concatenate(blocks, axis=1)

    expo = (groups_per_block - 1 - lane_group).astype(F32)
    wr, wi = _cis(lr * expo, li * expo)
    hop_r, hop_i = _cpow2(abar_r, abar_i, groups_per_block)
    w1_r, w1_i = [], []
    for v in range(TH // LANES):
        w1_r.insert(0, wr * bb_r - wi * bb_i)
        w1_i.insert(0, wr * bb_i + wi * bb_r)
        wr, wi = _cmul(wr, wi, hop_r, hop_i)
    w1t = jnp.concatenate([jnp.concatenate(w1_r, axis=1),
                           jnp.concatenate(w1_i, axis=1)], axis=0).astype(BF16)

    e = e_ref[...]
    y = jnp.dot(mt, e, preferred_element_type=F32)
    st = jnp.dot(w1t, e, preferred_element_type=F32)
    sr, si = st[:P], st[P:]
    pos = lax.broadcasted_iota(jnp.int32, sr.shape, 1) % chunks_per_seq

    def shifted(a, shift):
        return jnp.where(pos >= shift, pltpu.roll(a, shift, 1), 0.0)

    qr, qi = _cpow2(abar_r, abar_i, T)
    shift = 1
    while shift < chunks_per_seq:
        srs, sis = shifted(sr, shift), shifted(si, shift)
        sr, si = sr + qr * srs - qi * sis, si + qr * sis + qi * srs
        qr, qi = qr * qr - qi * qi, 2.0 * qr * qi
        shift *= 2
    x_prev = jnp.concatenate([shifted(sr, 1), shifted(si, 1)], axis=0).astype(BF16)
    y = y + jnp.dot(c_pow1.astype(BF16), x_prev, preferred_element_type=F32)
    f_ref[...] = _gelu_tanh(y).reshape(f_ref.shape)


def _ssm_call(arow, acol, ldt, bt, cab, dcol, e, chunks_per_seq):
    G, TH, NC = e.shape
    T = TH // SSM_GROUP_CH
    grp = lambda a: pl.BlockSpec((None,) + a.shape[1:], lambda g: (g,) + (0,) * (a.ndim - 1))
    return pl.pallas_call(
        functools.partial(_ssm_kernel, chunks_per_seq=chunks_per_seq),
        grid=(G,),
        in_specs=[grp(a) for a in (arow, acol, ldt, bt, cab, dcol, e)],
        out_specs=pl.BlockSpec((T, SSM_GROUP_CH, NC), lambda g: (0, g, 0)),
        out_shape=jax.ShapeDtypeStruct((T, SSM_WIDTH, NC), F32),
        scratch_shapes=[pltpu.VMEM((2 * TH, LANES), F32)],
        compiler_params=pltpu.CompilerParams(
            dimension_semantics=("arbitrary",), vmem_limit_bytes=VMEM_LIMIT),
        name="ssm",
    )(arow, acol, ldt, bt, cab, dcol, e)


def _ssm_param_layouts(a_re, a_im, log_dt, b_re, b_im, c_re, c_im, d_skip):
    G = a_re.shape[0]
    arow = jnp.stack([jnp.concatenate([a_re, a_re], -1), jnp.concatenate([a_im, a_im], -1)], 1)
    acol = jnp.stack([a_re, a_im], -1)
    reps = LANES // SSM_GROUP_CH
    bt = jnp.stack([jnp.tile(b_re, (1, 1, reps)), jnp.tile(b_im, (1, 1, reps))], 1)
    cab = jnp.stack([jnp.concatenate([c_re, -c_im], -1), jnp.concatenate([-c_im, -c_re], -1)], 1)
    return (arow.astype(F32), acol.astype(F32), log_dt.reshape(G, 1, 1).astype(F32),
            bt.astype(F32), cab.astype(F32), d_skip.reshape(G, SSM_GROUP_CH, 1).astype(F32))


def _tail_kernel(h1_ref, attn_ref, y_ref, p_ref, wglu_ref, bglu_ref, ga_ref, gs_ref,
                 woa_ref, wos_ref, g2_ref, w1_ref, w3_ref, w2_ref, gp_ref, wpg_ref,
                 wpp_ref, gf_ref, o_ref):
    y = y_ref[...]
    glu = y * _sigmoid(jnp.dot(y.astype(BF16), wglu_ref[...], preferred_element_type=F32)
                       + bglu_ref[...])
    an = _rms(attn_ref[...], ga_ref[...]).astype(BF16)
    sn = _rms(glu, gs_ref[...]).astype(BF16)
    h = (h1_ref[...] + jnp.dot(an, woa_ref[...], preferred_element_type=F32)
         + jnp.dot(sn, wos_ref[...], preferred_element_type=F32))
    h = h + 0.5 * _swiglu(_rms(h, g2_ref[...]).astype(BF16), w1_ref, w3_ref, w2_ref)
    gate = _sigmoid(jnp.dot(_rms(h, gp_ref[...]).astype(BF16), wpg_ref[...],
                            preferred_element_type=F32))
    h = h + gate * jnp.dot(p_ref[...].astype(BF16), wpp_ref[...], preferred_element_type=F32)
    o_ref[...] = _rms(h, gf_ref[...])


def _tail_call(h1, attn, y, p, *consts):
    B, L, D = h1.shape
    tm = TOKEN_TILE
    tile = lambda w: pl.BlockSpec((None, tm, w), lambda b, i: (b, i, 0))
    return pl.pallas_call(
        _tail_kernel,
        grid=(B, L // tm),
        in_specs=[tile(D), tile(ATTN_WIDTH), tile(SSM_WIDTH), tile(PLE_DIM)]
                 + [_const_spec(c.shape) for c in consts],
        out_specs=tile(D),
        out_shape=jax.ShapeDtypeStruct((B, L, D), F32),
        compiler_params=pltpu.CompilerParams(
            dimension_semantics=("arbitrary", "arbitrary"),
            vmem_limit_bytes=VMEM_LIMIT),
        name="tail",
    )(h1, attn, y, p, *consts)


def kernel(x, p, g_ffn1, w1_a, w3_a, w2_a, g_mix, w_in, b_f, a_re, a_im, log_dt, b_re, b_im, c_re, c_im, d_skip, w_glu, b_glu, g_attn_out, g_ssm_out, w_out, g_ffn2, w1_b, w3_b, w2_b, g_ple, w_ple_gate, w_ple_proj, g_final):
    B, L, D = x.shape
    assert D == D_MODEL and L % ATTN_TILE == 0 and L % TOKEN_TILE == 0 and L % SSM_CHUNK == 0
    assert g_ffn1.shape[0] == 1, "single layer"
    assert TOKEN_TILE == ATTN_TILE, "decay bias rows are relative to the kv tile start"
    row = lambda g: g.reshape(1, -1).astype(F32)
    bf = lambda w: w.astype(BF16)
    s_v, s_f = 3 * ATTN_WIDTH, 3 * ATTN_WIDTH + ATTN_HEADS
    w_in0 = w_in[0]
    scale = LOG2E / math.sqrt(HEAD_DIM)
    wqkv = jnp.concatenate([w_in0[:, :ATTN_WIDTH] * scale, w_in0[:, ATTN_WIDTH:s_v]], axis=1)
    wf = jnp.pad(w_in0[:, s_v:s_f], ((0, 0), (0, LANES - ATTN_HEADS)))

    h1, q, kt, v, s_in, ct, k_norm2 = _head_call(
        x, row(g_ffn1[0]), bf(w1_a[0]), bf(w3_a[0]), bf(w2_a[0]), row(g_mix[0]),
        bf(wqkv), bf(w_in0[:, s_f:]), bf(wf), b_f[0].reshape(ATTN_HEADS, 1).astype(F32))

    attn = _attn_call(q, kt, v, ct, k_norm2)

    T = SSM_CHUNK
    chunks_per_seq = L // T
    n_chunks = B * chunks_per_seq
    e = (s_in.reshape(n_chunks, T, SSM_GROUPS, SSM_GROUP_CH)
         .transpose(2, 1, 3, 0).reshape(SSM_GROUPS, T * SSM_GROUP_CH, n_chunks))
    f = _ssm_call(*_ssm_param_layouts(a_re[0], a_im[0], log_dt[0], b_re[0], b_im[0],
                                      c_re[0], c_im[0], d_skip[0]),
                  e, chunks_per_seq)
    y = f.transpose(2, 0, 1).reshape(B, L, SSM_WIDTH)

    w_out0 = w_out[0]
    return _tail_call(
        h1, attn, y, p[0],
        bf(w_glu[0]), row(b_glu[0]), row(g_attn_out[0]), row(g_ssm_out[0]),
        bf(w_out0[:ATTN_WIDTH]), bf(w_out0[ATTN_WIDTH:]), row(g_ffn2[0]),
        bf(w1_b[0]), bf(w3_b[0]), bf(w2_b[0]), row(g_ple[0]), bf(w_ple_gate[0]),
        bf(w_ple_proj[0]), row(g_final))
```

```python
import functools
import math

import jax
import jax.numpy as jnp
from jax import lax
from jax.experimental import pallas as pl
from jax.experimental.pallas import tpu as pltpu

D_MODEL = 1024
ATTN_HEADS = 8
HEAD_DIM = 64
ATTN_WIDTH = ATTN_HEADS * HEAD_DIM
SSM_WIDTH = D_MODEL - ATTN_WIDTH
SSM_GROUP_CH = 16
SSM_GROUPS = SSM_WIDTH // SSM_GROUP_CH
SSM_STATE = 64
D_FF = 2816
PLE_DIM = 256
EPS = 1e-6

LANES = 128
HEAD_PAIR = 2 * HEAD_DIM
N_PAIRS = ATTN_HEADS // 2
FF_CHUNK = 256
TOKEN_TILE = 512
ATTN_TILE = 512
SSM_CHUNK = 32
NEG_BIG = -1e30
SKIP_LOG2 = 140.0
BOUND_SLACK_MUL = 1.001
BOUND_SLACK_ADD = 1.0
LOG2E = math.log2(math.e)
N_BIAS = 3
BIAS_ROWS = 8
VMEM_LIMIT = 56 * 1024 * 1024

BF16 = jnp.bfloat16
F32 = jnp.float32


def _rms(x, g):
    ms = jnp.mean(x * x, axis=-1, keepdims=True)
    return x * lax.rsqrt(ms + EPS) * g


def _sigmoid(x):
    return 1.0 / (1.0 + jnp.exp(-x))


def _swiglu(xn, w1_ref, w3_ref, w2_ref):
    acc = None
    for c in range(D_FF // FF_CHUNK):
        sl = slice(c * FF_CHUNK, (c + 1) * FF_CHUNK)
        a = jnp.dot(xn, w1_ref[:, sl], preferred_element_type=F32)
        b = jnp.dot(xn, w3_ref[:, sl], preferred_element_type=F32)
        gated = (a * _sigmoid(a) * b).astype(BF16)
        part = jnp.dot(gated, w2_ref[sl, :], preferred_element_type=F32)
        acc = part if acc is None else acc + part
    return acc


def _const_spec(shape):
    nd = len(shape)
    return pl.BlockSpec(shape, lambda *_: (0,) * nd, pipeline_mode=pl.Buffered(1))


def _head_kernel(x_ref, g1_ref, w1_ref, w3_ref, w2_ref, gm_ref, wqkv_ref, ws_ref,
                 wf_ref, bf_ref, h1_ref, q_ref, kt_ref, v_ref, s_ref, ct_ref, kn_ref,
                 carry_ref):
    tm = x_ref.shape[0]
    x = x_ref[...]
    h1 = x + 0.5 * _swiglu(_rms(x, g1_ref[...]).astype(BF16), w1_ref, w3_ref, w2_ref)
    h1_ref[...] = h1
    un = _rms(h1, gm_ref[...]).astype(BF16)
    qkv = jnp.dot(un, wqkv_ref[...], preferred_element_type=F32)
    q_ref[...] = qkv[:, :ATTN_WIDTH].astype(BF16)
    s_ref[...] = jnp.dot(un, ws_ref[...], preferred_element_type=F32).astype(BF16)

    zf = jnp.dot(un, wf_ref[...], preferred_element_type=F32)
    zft = zf.T[:ATTN_HEADS, :] + bf_ref[...]
    logf = jnp.minimum(zft, 0.0) - jnp.log1p(jnp.exp(-jnp.abs(zft)))
    lane = lax.broadcasted_iota(jnp.int32, logf.shape, 1)
    c = logf
    shift = 1
    while shift < tm:
        c = c + jnp.where(lane >= shift, pltpu.roll(c, shift, 1), 0.0)
        shift *= 2

    @pl.when(pl.program_id(1) == 0)
    def _():
        carry_ref[...] = jnp.zeros_like(carry_ref)

    c_abs = c + carry_ref[:, 0:1]
    ct_ref[...] = c_abs * LOG2E
    carry_ref[...] = jnp.broadcast_to(c_abs[:, tm - 1:tm], carry_ref.shape)

    rel = (c - c[:, 0:1]) * LOG2E
    hi = rel.astype(BF16).astype(F32)
    mid = (rel - hi).astype(BF16).astype(F32)
    lo = (rel - hi - mid).astype(BF16).astype(F32)
    kt = qkv[:, ATTN_WIDTH:2 * ATTN_WIDTH].astype(BF16).astype(F32).T
    vv = qkv[:, 2 * ATTN_WIDTH:]
    k_sq = (kt * kt).reshape(ATTN_HEADS, HEAD_DIM, tm)
    kn_ref[...] = jnp.broadcast_to(
        jnp.max(jnp.sum(k_sq, axis=1), axis=-1, keepdims=True), kn_ref.shape)
    sub = lax.broadcasted_iota(jnp.int32, (BIAS_ROWS, tm), 0)
    zeros = jnp.zeros((HEAD_DIM - BIAS_ROWS, tm), F32)
    vlane = lax.broadcasted_iota(jnp.int32, (tm, HEAD_PAIR), 1)
    for h in range(ATTN_HEADS):
        bias = jnp.where(sub == 0, -hi[h:h + 1],
                         jnp.where(sub == 1, -mid[h:h + 1],
                                   jnp.where(sub == 2, -lo[h:h + 1], 0.0)))
        k_h = kt[h * HEAD_DIM:(h + 1) * HEAD_DIM]
        vp = vv[:, (h // 2) * HEAD_PAIR:(h // 2 + 1) * HEAD_PAIR]
        if h % 2 == 0:
            kt_ref[h] = jnp.concatenate([k_h, bias, zeros], axis=0).astype(BF16)
            v_ref[h] = jnp.where(vlane < HEAD_DIM, vp,
                                 jnp.where(vlane == HEAD_DIM, 1.0, 0.0)).astype(BF16)
        else:
            kt_ref[h] = jnp.concatenate([bias, zeros, k_h], axis=0).astype(BF16)
            v_ref[h] = jnp.where(vlane >= HEAD_DIM, vp,
                                 jnp.where(vlane == 0, 1.0, 0.0)).astype(BF16)


def _head_call(x, g1, w1, w3, w2, gm, wqkv, ws, wf, bf):
    B, L, D = x.shape
    tm = TOKEN_TILE
    tile = lambda w: pl.BlockSpec((None, tm, w), lambda b, i: (b, i, 0))
    out_shape = (
        jax.ShapeDtypeStruct((B, L, D), F32),
        jax.ShapeDtypeStruct((B, L, ATTN_WIDTH), BF16),
        jax.ShapeDtypeStruct((B, ATTN_HEADS, HEAD_PAIR, L), BF16),
        jax.ShapeDtypeStruct((B, ATTN_HEADS, L, HEAD_PAIR), BF16),
        jax.ShapeDtypeStruct((B, L, SSM_WIDTH), BF16),
        jax.ShapeDtypeStruct((B, ATTN_HEADS, L), F32),
        jax.ShapeDtypeStruct((B, L // tm, ATTN_HEADS, LANES), F32),
    )
    return pl.pallas_call(
        _head_kernel,
        grid=(B, L // tm),
        in_specs=[tile(D), _const_spec(g1.shape), _const_spec(w1.shape),
                  _const_spec(w3.shape), _const_spec(w2.shape), _const_spec(gm.shape),
                  _const_spec(wqkv.shape), _const_spec(ws.shape), _const_spec(wf.shape),
                  _const_spec(bf.shape)],
        out_specs=(tile(D), tile(ATTN_WIDTH),
                   pl.BlockSpec((None, ATTN_HEADS, HEAD_PAIR, tm), lambda b, i: (b, 0, 0, i)),
                   pl.BlockSpec((None, ATTN_HEADS, tm, HEAD_PAIR), lambda b, i: (b, 0, i, 0)),
                   tile(SSM_WIDTH),
                   pl.BlockSpec((None, ATTN_HEADS, tm), lambda b, i: (b, 0, i)),
                   pl.BlockSpec((None, None, ATTN_HEADS, LANES), lambda b, i: (b, i, 0, 0))),
        out_shape=out_shape,
        scratch_shapes=[pltpu.VMEM((ATTN_HEADS, LANES), F32)],
        compiler_params=pltpu.CompilerParams(
            dimension_semantics=("arbitrary", "arbitrary"),
            vmem_limit_bytes=VMEM_LIMIT),
        name="head",
    )(x, g1, w1, w3, w2, gm, wqkv, ws, wf, bf)


def _attn_kernel(q_ref, kt_ref, v_ref, c_ref, st_ref, o_ref, s_e0, s_e1, s_o0, s_o1, m_ref, acc_ref):
    t = q_ref.shape[0]
    n = pl.program_id(2)
    q = q_ref[...]
    lane = lax.broadcasted_iota(jnp.int32, (1, HEAD_PAIR), 1)
    first = lane < HEAD_DIM
    ones_even = jnp.where((lane >= HEAD_DIM) & (lane < HEAD_DIM + N_BIAS), 1.0, 0.0).astype(BF16)
    ones_odd = jnp.where(lane < N_BIAS, 1.0, 0.0).astype(BF16)
    q_heads = (jnp.where(first, q, ones_even), jnp.where(first, ones_odd, q))
    s_buf = ((s_e0, s_e1), (s_o0, s_o1))

    def start(kv):
        return pl.multiple_of(kv * t, t)

    c_q = [c_ref[h:h + 1, pl.ds(start(n), LANES)][:, 0:1] for h in range(2)]

    def qk(h, kv):
        return jnp.dot(q_heads[h], kt_ref[h, :, pl.ds(start(kv), t)], preferred_element_type=F32)

    def consume(h, s, kv):
        d = c_ref[h:h + 1, pl.ds(start(kv), LANES)][:, 0:1] - c_q[h]
        m_old = m_ref[h]
        m_new = jnp.maximum(m_old, jnp.max(s, axis=-1, keepdims=True) - d)
        p = jnp.exp2((s - (m_new + d)).astype(BF16))
        m_ref[h] = m_new
        acc_ref[h] = jnp.exp2(m_old - m_new) * acc_ref[h] + jnp.dot(
            p, v_ref[h, pl.ds(start(kv), t), :], preferred_element_type=F32)

    def step(kv, par):
        for h in range(2):
            s_buf[1 - par][h][...] = qk(h, kv - 1)
        for h in range(2):
            consume(h, s_buf[par][h][...], kv)

    m_ref[...] = jnp.full(m_ref.shape, NEG_BIG, F32)
    acc_ref[...] = jnp.zeros(acc_ref.shape, F32)
    for h in range(2):
        s_buf[1][h][...] = qk(h, n)

    q_sq = q.astype(F32) * q.astype(F32)
    reach = []
    for h in range(2):
        q_norm2 = jnp.max(jnp.sum(jnp.where(first == (h == 0), q_sq, 0.0), axis=-1, keepdims=True),
                          axis=0, keepdims=True)
        k_norm2, c_next = st_ref[h:h + 1, :], st_ref[2 + h:3 + h, :]
        reach.append(jnp.sqrt(q_norm2 * k_norm2) * BOUND_SLACK_MUL + BOUND_SLACK_ADD
                     + c_q[h] - c_next)

    @pl.when(n >= 0)
    def _():
        row = lax.broadcasted_iota(jnp.int32, (t, t), 0)
        col = lax.broadcasted_iota(jnp.int32, (t, t), 1)
        for h in range(2):
            s_buf[0][h][...] = qk(h, jnp.maximum(n - 1, 0))
        for h in range(2):
            consume(h, jnp.where(col <= row, s_buf[1][h][...], NEG_BIG), n)

    tile_id = lax.broadcasted_iota(jnp.int32, (1, LANES), 1)
    needed = tile_id < 0
    for h in range(2):
        m_low = jnp.min(m_ref[h], axis=0, keepdims=True)
        needed = needed | (reach[h] - m_low >= -SKIP_LOG2)
    n_f = n.astype(F32)
    first_needed = jnp.min(jnp.where(needed & (tile_id < n), tile_id.astype(F32), n_f))
    count = n - first_needed.astype(jnp.int32)

    def pair(i, carry):
        step(n - 1 - 2 * i, 0)
        step(n - 2 - 2 * i, 1)
        return carry

    n_pairs = jnp.maximum(count - 1, 0) // 2
    lax.fori_loop(0, n_pairs, pair, 0)
    left = count - 2 * n_pairs
    last = n - count

    @pl.when(left == 2)
    def _():
        step(last + 1, 0)
        for h in range(2):
            consume(h, s_buf[1][h][...], last)

    @pl.when(left == 1)
    def _():
        for h in range(2):
            consume(h, s_buf[0][h][...], last)

    acc0, acc1 = acc_ref[0], acc_ref[1]
    o_ref[...] = jnp.where(first, acc0 / acc0[:, HEAD_DIM:HEAD_DIM + 1], acc1 / acc1[:, 0:1])


def _attn_call(q, kt, v, ct, k_norm2):
    B, L, _ = q.shape
    t = ATTN_TILE
    n_tiles = L // t
    assert n_tiles <= LANES
    c4 = ct.reshape(B, N_PAIRS, 2, L)
    c_next = jnp.roll(ct[:, :, ::t], -1, axis=-1)
    per_tile = jnp.stack([k_norm2[..., 0].transpose(0, 2, 1), c_next], axis=1)
    per_tile = jnp.pad(per_tile, ((0, 0),) * 3 + ((0, LANES - n_tiles),))
    stats = (per_tile.reshape(B, 2, N_PAIRS, 2, LANES).transpose(0, 2, 1, 3, 4)
             .reshape(B, N_PAIRS, 4, LANES))
    pair_block = lambda *shape: pl.BlockSpec((None, 2) + shape, lambda b, p, i: (b, p, 0, 0))
    return pl.pallas_call(
        _attn_kernel,
        grid=(B, N_PAIRS, L // t),
        in_specs=[pl.BlockSpec((None, t, HEAD_PAIR), lambda b, p, i: (b, i, p)),
                  pair_block(HEAD_PAIR, L), pair_block(L, HEAD_PAIR),
                  pl.BlockSpec((None, None, 2, L), lambda b, p, i: (b, p, 0, 0)),
                  pl.BlockSpec((None, None, 4, LANES), lambda b, p, i: (b, p, 0, 0))],
        out_specs=pl.BlockSpec((None, t, HEAD_PAIR), lambda b, p, i: (b, i, p)),
        out_shape=jax.ShapeDtypeStruct((B, L, ATTN_WIDTH), F32),
        scratch_shapes=[pltpu.VMEM((t, t), F32)] * 4
                       + [pltpu.VMEM((2, t, 1), F32), pltpu.VMEM((2, t, HEAD_PAIR), F32)],
        compiler_params=pltpu.CompilerParams(
            dimension_semantics=("arbitrary", "arbitrary", "arbitrary"),
            vmem_limit_bytes=VMEM_LIMIT),
        name="attn",
    )(q, kt, v, c4, stats)


def _gelu_tanh(x):
    return 0.5 * x * (1.0 + jnp.tanh(math.sqrt(2.0 / math.pi) * (x + 0.044715 * (x * x * x))))


def _cis(mag_arg, ang):
    mag = jnp.exp(mag_arg)
    return mag * jnp.cos(ang), mag * jnp.sin(ang)


def _cmul(ar, ai, br, bi):
    return ar * br - ai * bi, ar * bi + ai * br


def _cpow2(zr, zi, n):
    assert n & (n - 1) == 0
    while n > 1:
        zr, zi = zr * zr - zi * zi, 2.0 * zr * zi
        n //= 2
    return zr, zi


def _ssm_kernel(arow_ref, acol_ref, ldt_ref, bt_ref, cab_ref, dcol_ref, e_ref, f_ref, z_ref,
                *, chunks_per_seq):
    T, P, H = SSM_CHUNK, SSM_STATE, SSM_GROUP_CH
    TH = T * H
    hi = lax.Precision.HIGHEST
    dt = jnp.exp(ldt_ref[...])

    lam_r, lam_i = dt * arow_ref[0:1, :], dt * arow_ref[1:2, :]
    j0 = lax.broadcasted_iota(jnp.int32, (T, 2 * P), 0).astype(F32)
    pa0, pb0 = _cis(j0 * lam_r, j0 * lam_i)
    pa1, pb1 = _cmul(pa0, pb0, *_cis(lam_r, lam_i))
    over_h = lambda a: jnp.concatenate(
        [jnp.broadcast_to(a[j:j + 1, :], (H, 2 * P)) for j in range(T)], axis=0)
    ca, cb = jnp.tile(cab_ref[0], (T, 1)), jnp.tile(cab_ref[1], (T, 1))
    c_pow0 = over_h(pa0) * ca + over_h(pb0) * cb
    c_pow1 = over_h(pa1) * ca + over_h(pb1) * cb

    a_r, a_i = acol_ref[:, 0:1], acol_ref[:, 1:2]
    lr, li = dt * a_r, dt * a_i
    abar_r, abar_i = _cis(lr, li)
    nr, ni = abar_r - 1.0, abar_i
    den = a_r * a_r + a_i * a_i
    fr, fi = (nr * a_r + ni * a_i) / den, (ni * a_r - nr * a_i) / den
    b_r, b_i = bt_ref[0], bt_ref[1]
    bb_r, bb_i = fr * b_r - fi * b_i, fr * b_i + fi * b_r

    kcol = jnp.dot(c_pow0, jnp.concatenate([bb_r, bb_i], axis=0), precision=hi,
                   preferred_element_type=F32)
    lane_h = lax.broadcasted_iota(jnp.int32, (H, LANES), 1) % H
    skip = jnp.where(lane_h == lax.broadcasted_iota(jnp.int32, (H, LANES), 0), dcol_ref[...], 0.0)

    @pl.when(pl.program_id(0) == 0)
    def _():
        z_ref[0:TH, :] = jnp.zeros((TH, LANES), F32)

    z_ref[TH:2 * TH, :] = kcol
    z_ref[TH:TH + H, :] = kcol[:H] + skip
    lane_group = lax.broadcasted_iota(jnp.int32, (1, LANES), 1) // H
    groups_per_block = LANES // H
    blocks = []
    for v in range(TH // LANES):
        blk = None
        for u in range(groups_per_block):
            s = v * groups_per_block + u
            piece = z_ref[TH - H * s:2 * TH - H * s, :]
            blk = piece if blk is None else jnp.where(lane_group == u, piece, blk)
        blocks.append(blk.astype(BF16))
    mt = jnp.concatenate(blocks, axis=1)

    expo = (groups_per_block - 1 - lane_group).astype(F32)
    wr, wi = _cis(lr * expo, li * expo)
    hop_r, hop_i = _cpow2(abar_r, abar_i, groups_per_block)
    w1_r, w1_i = [], []
    for v in range(TH // LANES):
        w1_r.insert(0, wr * bb_r - wi * bb_i)
        w1_i.insert(0, wr * bb_i + wi * bb_r)
        wr, wi = _cmul(wr, wi, hop_r, hop_i)
    w1t = jnp.concatenate([jnp.concatenate(w1_r, axis=1),
                           jnp.concatenate(w1_i, axis=1)], axis=0).astype(BF16)

    e = e_ref[...]
    y = jnp.dot(mt, e, preferred_element_type=F32)
    st = jnp.dot(w1t, e, preferred_element_type=F32)
    sr, si = st[:P], st[P:]
    pos = lax.broadcasted_iota(jnp.int32, sr.shape, 1) % chunks_per_seq

    def shifted(a, shift):
        return jnp.where(pos >= shift, pltpu.roll(a, shift, 1), 0.0)

    qr, qi = _cpow2(abar_r, abar_i, T)
    shift = 1
    while shift < chunks_per_seq:
        srs, sis = shifted(sr, shift), shifted(si, shift)
        sr, si = sr + qr * srs - qi * sis, si + qr * sis + qi * srs
        qr, qi = qr * qr - qi * qi, 2.0 * qr * qi
        shift *= 2
    x_prev = jnp.concatenate([shifted(sr, 1), shifted(si, 1)], axis=0).astype(BF16)
    y = y + jnp.dot(c_pow1.astype(BF16), x_prev, preferred_element_type=F32)
    f_ref[...] = _gelu_tanh(y).reshape(f_ref.shape)


def _ssm_call(arow, acol, ldt, bt, cab, dcol, e, chunks_per_seq):
    G, TH, NC = e.shape
    T = TH // SSM_GROUP_CH
    grp = lambda a: pl.BlockSpec((None,) + a.shape[1:], lambda g: (g,) + (0,) * (a.ndim - 1))
    return pl.pallas_call(
        functools.partial(_ssm_kernel, chunks_per_seq=chunks_per_seq),
        grid=(G,),
        in_specs=[grp(a) for a in (arow, acol, ldt, bt, cab, dcol, e)],
        out_specs=pl.BlockSpec((T, SSM_GROUP_CH, NC), lambda g: (0, g, 0)),
        out_shape=jax.ShapeDtypeStruct((T, SSM_WIDTH, NC), F32),
        scratch_shapes=[pltpu.VMEM((2 * TH, LANES), F32)],
        compiler_params=pltpu.CompilerParams(
            dimension_semantics=("arbitrary",), vmem_limit_bytes=VMEM_LIMIT),
        name="ssm",
    )(arow, acol, ldt, bt, cab, dcol, e)


def _ssm_param_layouts(a_re, a_im, log_dt, b_re, b_im, c_re, c_im, d_skip):
    G = a_re.shape[0]
    arow = jnp.stack([jnp.concatenate([a_re, a_re], -1), jnp.concatenate([a_im, a_im], -1)], 1)
    acol = jnp.stack([a_re, a_im], -1)
    reps = LANES // SSM_GROUP_CH
    bt = jnp.stack([jnp.tile(b_re, (1, 1, reps)), jnp.tile(b_im, (1, 1, reps))], 1)
    cab = jnp.stack([jnp.concatenate([c_re, -c_im], -1), jnp.concatenate([-c_im, -c_re], -1)], 1)
    return (arow.astype(F32), acol.astype(F32), log_dt.reshape(G, 1, 1).astype(F32),
            bt.astype(F32), cab.astype(F32), d_skip.reshape(G, SSM_GROUP_CH, 1).astype(F32))


def _tail_kernel(h1_ref, attn_ref, y_ref, p_ref, wglu_ref, bglu_ref, ga_ref, gs_ref,
                 woa_ref, wos_ref, g2_ref, w1_ref, w3_ref, w2_ref, gp_ref, wpg_ref,
                 wpp_ref, gf_ref, o_ref):
    y = y_ref[...]
    glu = y * _sigmoid(jnp.dot(y.astype(BF16), wglu_ref[...], preferred_element_type=F32)
                       + bglu_ref[...])
    an = _rms(attn_ref[...], ga_ref[...]).astype(BF16)
    sn = _rms(glu, gs_ref[...]).astype(BF16)
    h = (h1_ref[...] + jnp.dot(an, woa_ref[...], preferred_element_type=F32)
         + jnp.dot(sn, wos_ref[...], preferred_element_type=F32))
    h = h + 0.5 * _swiglu(_rms(h, g2_ref[...]).astype(BF16), w1_ref, w3_ref, w2_ref)
    gate = _sigmoid(jnp.dot(_rms(h, gp_ref[...]).astype(BF16), wpg_ref[...],
                            preferred_element_type=F32))
    h = h + gate * jnp.dot(p_ref[...].astype(BF16), wpp_ref[...], preferred_element_type=F32)
    o_ref[...] = _rms(h, gf_ref[...])


def _tail_call(h1, attn, y, p, *consts):
    B, L, D = h1.shape
    tm = TOKEN_TILE
    tile = lambda w: pl.BlockSpec((None, tm, w), lambda b, i: (b, i, 0))
    return pl.pallas_call(
        _tail_kernel,
        grid=(B, L // tm),
        in_specs=[tile(D), tile(ATTN_WIDTH), tile(SSM_WIDTH), tile(PLE_DIM)]
                 + [_const_spec(c.shape) for c in consts],
        out_specs=tile(D),
        out_shape=jax.ShapeDtypeStruct((B, L, D), F32),
        compiler_params=pltpu.CompilerParams(
            dimension_semantics=("arbitrary", "arbitrary"),
            vmem_limit_bytes=VMEM_LIMIT),
        name="tail",
    )(h1, attn, y, p, *consts)


def kernel(x, p, g_ffn1, w1_a, w3_a, w2_a, g_mix, w_in, b_f, a_re, a_im, log_dt, b_re, b_im, c_re, c_im, d_skip, w_glu, b_glu, g_attn_out, g_ssm_out, w_out, g_ffn2, w1_b, w3_b, w2_b, g_ple, w_ple_gate, w_ple_proj, g_final):
    B, L, D = x.shape
    assert D == D_MODEL and L % ATTN_TILE == 0 and L % TOKEN_TILE == 0 and L % SSM_CHUNK == 0
    assert g_ffn1.shape[0] == 1, "single layer"
    assert TOKEN_TILE == ATTN_TILE, "decay bias rows are relative to the kv tile start"
    row = lambda g: g.reshape(1, -1).astype(F32)
    bf = lambda w: w.astype(BF16)
    s_v, s_f = 3 * ATTN_WIDTH, 3 * ATTN_WIDTH + ATTN_HEADS
    w_in0 = w_in[0]
    scale = LOG2E / math.sqrt(HEAD_DIM)
    wqkv = jnp.concatenate([w_in0[:, :ATTN_WIDTH] * scale, w_in0[:, ATTN_WIDTH:s_v]], axis=1)
    wf = jnp.pad(w_in0[:, s_v:s_f], ((0, 0), (0, LANES - ATTN_HEADS)))

    h1, q, kt, v, s_in, ct, k_norm2 = _head_call(
        x, row(g_ffn1[0]), bf(w1_a[0]), bf(w3_a[0]), bf(w2_a[0]), row(g_mix[0]),
        bf(wqkv), bf(w_in0[:, s_f:]), bf(wf), b_f[0].reshape(ATTN_HEADS, 1).astype(F32))

    attn = _attn_call(q, kt, v, ct, k_norm2)

    T = SSM_CHUNK
    chunks_per_seq = L // T
    n_chunks = B * chunks_per_seq
    e = (s_in.reshape(n_chunks, T, SSM_GROUPS, SSM_GROUP_CH)
         .transpose(2, 1, 3, 0).reshape(SSM_GROUPS, T * SSM_GROUP_CH, n_chunks))
    f = _ssm_call(*_ssm_param_layouts(a_re[0], a_im[0], log_dt[0], b_re[0], b_im[0],
                                      c_re[0], c_im[0], d_skip[0]),
                  e, chunks_per_seq)
    y = f.transpose(2, 0, 1).reshape(B, L, SSM_WIDTH)

    w_out0 = w_out[0]
    return _tail_call(
        h1, attn, y, p[0],
        bf(w_glu[0]), row(b_glu[0]), row(g_attn_out[0]), row(g_ssm_out[0]),
        bf(w_out0[:ATTN_WIDTH]), bf(w_out0[ATTN_WIDTH:]), row(g_ffn2[0]),
        bf(w1_b[0]), bf(w3_b[0]), bf(w2_b[0]), row(g_ple[0]), bf(w_ple_gate[0]),
        bf(w_ple_proj[0]), row(g_final))
```

```python
import functools
import math

import jax
import jax.numpy as jnp
from jax import lax
from jax.experimental import pallas as pl
from jax.experimental.pallas import tpu as pltpu

D_MODEL = 1024
ATTN_HEADS = 8
HEAD_DIM = 64
ATTN_WIDTH = ATTN_HEADS * HEAD_DIM
SSM_WIDTH = D_MODEL - ATTN_WIDTH
SSM_GROUP_CH = 16
SSM_GROUPS = SSM_WIDTH // SSM_GROUP_CH
SSM_STATE = 64
D_FF = 2816
PLE_DIM = 256
EPS = 1e-6

LANES = 128
HEAD_PAIR = 2 * HEAD_DIM
N_PAIRS = ATTN_HEADS // 2
FF_CHUNK = 256
TOKEN_TILE = 512
ATTN_TILE = 512
SSM_CHUNK = 32
NEG_BIG = -1e30
SKIP_LOG2 = 140.0
BOUND_SLACK_MUL = 1.001
BOUND_SLACK_ADD = 1.0
LOG2E = math.log2(math.e)
N_BIAS = 3
BIAS_ROWS = 8
VMEM_LIMIT = 56 * 1024 * 1024

BF16 = jnp.bfloat16
F32 = jnp.float32


def _rms(x, g):
    ms = jnp.mean(x * x, axis=-1, keepdims=True)
    return x * lax.rsqrt(ms + EPS) * g


def _sigmoid(x):
    return 1.0 / (1.0 + jnp.exp(-x))


def _swiglu(xn, w1_ref, w3_ref, w2_ref):
    acc = None
    for c in range(D_FF // FF_CHUNK):
        sl = slice(c * FF_CHUNK, (c + 1) * FF_CHUNK)
        a = jnp.dot(xn, w1_ref[:, sl], preferred_element_type=F32)
        b = jnp.dot(xn, w3_ref[:, sl], preferred_element_type=F32)
        gated = (a * _sigmoid(a) * b).astype(BF16)
        part = jnp.dot(gated, w2_ref[sl, :], preferred_element_type=F32)
        acc = part if acc is None else acc + part
    return acc


def _const_spec(shape):
    nd = len(shape)
    return pl.BlockSpec(shape, lambda *_: (0,) * nd, pipeline_mode=pl.Buffered(1))


def _head_kernel(x_ref, g1_ref, w1_ref, w3_ref, w2_ref, gm_ref, wqkv_ref, ws_ref,
                 wf_ref, bf_ref, h1_ref, q_ref, kt_ref, v_ref, s_ref, ct_ref, kn_ref,
                 carry_ref):
    tm = x_ref.shape[0]
    x = x_ref[...]
    h1 = x + 0.5 * _swiglu(_rms(x, g1_ref[...]).astype(BF16), w1_ref, w3_ref, w2_ref)
    h1_ref[...] = h1
    un = _rms(h1, gm_ref[...]).astype(BF16)
    qkv = jnp.dot(un, wqkv_ref[...], preferred_element_type=F32)
    q_ref[...] = qkv[:, :ATTN_WIDTH].astype(BF16)
    s_ref[...] = jnp.dot(un, ws_ref[...], preferred_element_type=F32).astype(BF16)

    zf = jnp.dot(un, wf_ref[...], preferred_element_type=F32)
    zft = zf.T[:ATTN_HEADS, :] + bf_ref[...]
    logf = jnp.minimum(zft, 0.0) - jnp.log1p(jnp.exp(-jnp.abs(zft)))
    lane = lax.broadcasted_iota(jnp.int32, logf.shape, 1)
    c = logf
    shift = 1
    while shift < tm:
        c = c + jnp.where(lane >= shift, pltpu.roll(c, shift, 1), 0.0)
        shift *= 2

    @pl.when(pl.program_id(1) == 0)
    def _():
        carry_ref[...] = jnp.zeros_like(carry_ref)

    c_abs = c + carry_ref[:, 0:1]
    ct_ref[...] = c_abs * LOG2E
    carry_ref[...] = jnp.broadcast_to(c_abs[:, tm - 1:tm], carry_ref.shape)

    rel = (c - c[:, 0:1]) * LOG2E
    hi = rel.astype(BF16).astype(F32)
    mid = (rel - hi).astype(BF16).astype(F32)
    lo = (rel - hi - mid).astype(BF16).astype(F32)
    kt = qkv[:, ATTN_WIDTH:2 * ATTN_WIDTH].astype(BF16).astype(F32).T
    vv = qkv[:, 2 * ATTN_WIDTH:]
    k_sq = (kt * kt).reshape(ATTN_HEADS, HEAD_DIM, tm)
    kn_ref[...] = jnp.broadcast_to(
        jnp.max(jnp.sum(k_sq, axis=1), axis=-1, keepdims=True), kn_ref.shape)
    sub = lax.broadcasted_iota(jnp.int32, (BIAS_ROWS, tm), 0)
    zeros = jnp.zeros((HEAD_DIM - BIAS_ROWS, tm), F32)
    vlane = lax.broadcasted_iota(jnp.int32, (tm, HEAD_PAIR), 1)
    for h in range(ATTN_HEADS):
        bias = jnp.where(sub == 0, -hi[h:h + 1],
                         jnp.where(sub == 1, -mid[h:h + 1],
                                   jnp.where(sub == 2, -lo[h:h + 1], 0.0)))
        k_h = kt[h * HEAD_DIM:(h + 1) * HEAD_DIM]
        vp = vv[:, (h // 2) * HEAD_PAIR:(h // 2 + 1) * HEAD_PAIR]
        if h % 2 == 0:
            kt_ref[h] = jnp.concatenate([k_h, bias, zeros], axis=0).astype(BF16)
            v_ref[h] = jnp.where(vlane < HEAD_DIM, vp,
                                 jnp.where(vlane == HEAD_DIM, 1.0, 0.0)).astype(BF16)
        else:
            kt_ref[h] = jnp.concatenate([bias, zeros, k_h], axis=0).astype(BF16)
            v_ref[h] = jnp.where(vlane >= HEAD_DIM, vp,
                                 jnp.where(vlane == 0, 1.0, 0.0)).astype(BF16)


def _head_call(x, g1, w1, w3, w2, gm, wqkv, ws, wf, bf):
    B, L, D = x.shape
    tm = TOKEN_TILE
    tile = lambda w: pl.BlockSpec((None, tm, w), lambda b, i: (b, i, 0))
    out_shape = (
        jax.ShapeDtypeStruct((B, L, D), F32),
        jax.ShapeDtypeStruct((B, L, ATTN_WIDTH), BF16),
        jax.ShapeDtypeStruct((B, ATTN_HEADS, HEAD_PAIR, L), BF16),
        jax.ShapeDtypeStruct((B, ATTN_HEADS, L, HEAD_PAIR), BF16),
        jax.ShapeDtypeStruct((B, L, SSM_WIDTH), BF16),
        jax.ShapeDtypeStruct((B, ATTN_HEADS, L), F32),
        jax.ShapeDtypeStruct((B, L // tm, ATTN_HEADS, LANES), F32),
    )
    return pl.pallas_call(
        _head_kernel,
        grid=(B, L // tm),
        in_specs=[tile(D), _const_spec(g1.shape), _const_spec(w1.shape),
                  _const_spec(w3.shape), _const_spec(w2.shape), _const_spec(gm.shape),
                  _const_spec(wqkv.shape), _const_spec(ws.shape), _const_spec(wf.shape),
                  _const_spec(bf.shape)],
        out_specs=(tile(D), tile(ATTN_WIDTH),
                   pl.BlockSpec((None, ATTN_HEADS, HEAD_PAIR, tm), lambda b, i: (b, 0, 0, i)),
                   pl.BlockSpec((None, ATTN_HEADS, tm, HEAD_PAIR), lambda b, i: (b, 0, i, 0)),
                   tile(SSM_WIDTH),
                   pl.BlockSpec((None, ATTN_HEADS, tm), lambda b, i: (b, 0, i)),
                   pl.BlockSpec((None, None, ATTN_HEADS, LANES), lambda b, i: (b, i, 0, 0))),
        out_shape=out_shape,
        scratch_shapes=[pltpu.VMEM((ATTN_HEADS, LANES), F32)],
        compiler_params=pltpu.CompilerParams(
            dimension_semantics=("arbitrary", "arbitrary"),
            vmem_limit_bytes=VMEM_LIMIT),
        name="head",
    )(x, g1, w1, w3, w2, gm, wqkv, ws, wf, bf)


def _attn_kernel(q_ref, kt_ref, v_ref, c_ref, st_ref, o_ref, s_e0, s_e1, s_o0, s_o1, m_ref, acc_ref):
    t = q_ref.shape[0]
    n = pl.program_id(2)
    q = q_ref[...]
    lane = lax.broadcasted_iota(jnp.int32, (1, HEAD_PAIR), 1)
    first = lane < HEAD_DIM
    ones_even = jnp.where((lane >= HEAD_DIM) & (lane < HEAD_DIM + N_BIAS), 1.0, 0.0).astype(BF16)
    ones_odd = jnp.where(lane < N_BIAS, 1.0, 0.0).astype(BF16)
    q_heads = (jnp.where(first, q, ones_even), jnp.where(first, ones_odd, q))
    s_buf = ((s_e0, s_e1), (s_o0, s_o1))

    def start(kv):
        return pl.multiple_of(kv * t, t)

    c_q = [c_ref[h:h + 1, pl.ds(start(n), LANES)][:, 0:1] for h in range(2)]

    def qk(h, kv):
        return jnp.dot(q_heads[h], kt_ref[h, :, pl.ds(start(kv), t)], preferred_element_type=F32)

    def consume(h, s, kv):
        d = c_ref[h:h + 1, pl.ds(start(kv), LANES)][:, 0:1] - c_q[h]
        m_old = m_ref[h]
        m_new = jnp.maximum(m_old, jnp.max(s, axis=-1, keepdims=True) - d)
        p = jnp.exp2((s - (m_new + d)).astype(BF16))
        m_ref[h] = m_new
        acc_ref[h] = jnp.exp2(m_old - m_new) * acc_ref[h] + jnp.dot(
            p, v_ref[h, pl.ds(start(kv), t), :], preferred_element_type=F32)

    def step(kv, par):
        for h in range(2):
            s_buf[1 - par][h][...] = qk(h, kv - 1)
        for h in range(2):
            consume(h, s_buf[par][h][...], kv)

    m_ref[...] = jnp.full(m_ref.shape, NEG_BIG, F32)
    acc_ref[...] = jnp.zeros(acc_ref.shape, F32)
    for h in range(2):
        s_buf[1][h][...] = qk(h, n)

    q_sq = q.astype(F32) * q.astype(F32)
    reach = []
    for h in range(2):
        q_norm2 = jnp.max(jnp.sum(jnp.where(first == (h == 0), q_sq, 0.0), axis=-1, keepdims=True),
                          axis=0, keepdims=True)
        k_norm2, c_next = st_ref[h:h + 1, :], st_ref[2 + h:3 + h, :]
        reach.append(jnp.sqrt(q_norm2 * k_norm2) * BOUND_SLACK_MUL + BOUND_SLACK_ADD
                     + c_q[h] - c_next)

    @pl.when(n >= 0)
    def _():
        row = lax.broadcasted_iota(jnp.int32, (t, t), 0)
        col = lax.broadcasted_iota(jnp.int32, (t, t), 1)
        for h in range(2):
            s_buf[0][h][...] = qk(h, jnp.maximum(n - 1, 0))
        for h in range(2):
            consume(h, jnp.where(col <= row, s_buf[1][h][...], NEG_BIG), n)

    tile_id = lax.broadcasted_iota(jnp.int32, (1, LANES), 1)
    needed = tile_id < 0
    for h in range(2):
        m_low = jnp.min(m_ref[h], axis=0, keepdims=True)
        needed = needed | (reach[h] - m_low >= -SKIP_LOG2)
    n_f = n.astype(F32)
    first_needed = jnp.min(jnp.where(needed & (tile_id < n), tile_id.astype(F32), n_f))
    count = n - first_needed.astype(jnp.int32)

    def pair(i, carry):
        step(n - 1 - 2 * i, 0)
        step(n - 2 - 2 * i, 1)
        return carry

    n_pairs = jnp.maximum(count - 1, 0) // 2
    lax.fori_loop(0, n_pairs, pair, 0)
    left = count - 2 * n_pairs
    last = n - count

    @pl.when(left == 2)
    def _():
        step(last + 1, 0)
        for h in range(2):
            consume(h, s_buf[1][h][...], last)

    @pl.when(left == 1)
    def _():
        for h in range(2):
            consume(h, s_buf[0][h][...], last)

    acc0, acc1 = acc_ref[0], acc_ref[1]
    o_ref[...] = jnp.where(first, acc0 / acc0[:, HEAD_DIM:HEAD_DIM + 1], acc1 / acc1[:, 0:1])


def _attn_call(q, kt, v, ct, k_norm2):
    B, L, _ = q.shape
    t = ATTN_TILE
    n_tiles = L // t
    assert n_tiles <= LANES
    c4 = ct.reshape(B, N_PAIRS, 2, L)
    c_next = jnp.roll(ct[:, :, ::t], -1, axis=-1)
    per_tile = jnp.stack([k_norm2[..., 0].transpose(0, 2, 1), c_next], axis=1)
    per_tile = jnp.pad(per_tile, ((0, 0),) * 3 + ((0, LANES - n_tiles),))
    stats = (per_tile.reshape(B, 2, N_PAIRS, 2, LANES).transpose(0, 2, 1, 3, 4)
             .reshape(B, N_PAIRS, 4, LANES))
    pair_block = lambda *shape: pl.BlockSpec((None, 2) + shape, lambda b, p, i: (b, p, 0, 0))
    return pl.pallas_call(
        _attn_kernel,
        grid=(B, N_PAIRS, L // t),
        in_specs=[pl.BlockSpec((None, t, HEAD_PAIR), lambda b, p, i: (b, i, p)),
                  pair_block(HEAD_PAIR, L), pair_block(L, HEAD_PAIR),
                  pl.BlockSpec((None, None, 2, L), lambda b, p, i: (b, p, 0, 0)),
                  pl.BlockSpec((None, None, 4, LANES), lambda b, p, i: (b, p, 0, 0))],
        out_specs=pl.BlockSpec((None, t, HEAD_PAIR), lambda b, p, i: (b, i, p)),
        out_shape=jax.ShapeDtypeStruct((B, L, ATTN_WIDTH), F32),
        scratch_shapes=[pltpu.VMEM((t, t), F32)] * 4
                       + [pltpu.VMEM((2, t, 1), F32), pltpu.VMEM((2, t, HEAD_PAIR), F32)],
        compiler_params=pltpu.CompilerParams(
            dimension_semantics=("arbitrary", "arbitrary", "arbitrary"),
            vmem_limit_bytes=VMEM_LIMIT),
        name="attn",
    )(q, kt, v, c4, stats)


def _gelu_tanh(x):
    return 0.5 * x * (1.0 + jnp.tanh(math.sqrt(2.0 / math.pi) * (x + 0.044715 * (x * x * x))))


def _cis(mag_arg, ang):
    mag = jnp.exp(mag_arg)
    return mag * jnp.cos(ang), mag * jnp.sin(ang)


def _cmul(ar, ai, br, bi):
    return ar * br - ai * bi, ar * bi + ai * br


def _cpow2(zr, zi, n):
    assert n & (n - 1) == 0
    while n > 1:
        zr, zi = zr * zr - zi * zi, 2.0 * zr * zi
        n //= 2
    return zr, zi


def _ssm_kernel(arow_ref, acol_ref, ldt_ref, bt_ref, cab_ref, dcol_ref, e_ref, f_ref, z_ref,
                *, chunks_per_seq):
    T, P, H = SSM_CHUNK, SSM_STATE, SSM_GROUP_CH
    TH = T * H
    hi = lax.Precision.HIGHEST
    dt = jnp.exp(ldt_ref[...])

    lam_r, lam_i = dt * arow_ref[0:1, :], dt * arow_ref[1:2, :]
    j0 = lax.broadcasted_iota(jnp.int32, (T, 2 * P), 0).astype(F32)
    pa0, pb0 = _cis(j0 * lam_r, j0 * lam_i)
    pa1, pb1 = _cmul(pa0, pb0, *_cis(lam_r, lam_i))
    over_h = lambda a: jnp.concatenate(
        [jnp.broadcast_to(a[j:j + 1, :], (H, 2 * P)) for j in range(T)], axis=0)
    ca, cb = jnp.tile(cab_ref[0], (T, 1)), jnp.tile(cab_ref[1], (T, 1))
    c_pow0 = over_h(pa0) * ca + over_h(pb0) * cb
    c_pow1 = over_h(pa1) * ca + over_h(pb1) * cb

    a_r, a_i = acol_ref[:, 0:1], acol_ref[:, 1:2]
    lr, li = dt * a_r, dt * a_i
    abar_r, abar_i = _cis(lr, li)
    nr, ni = abar_r - 1.0, abar_i
    den = a_r * a_r + a_i * a_i
    fr, fi = (nr * a_r + ni * a_i) / den, (ni * a_r - nr * a_i) / den
    b_r, b_i = bt_ref[0], bt_ref[1]
    bb_r, bb_i = fr * b_r - fi * b_i, fr * b_i + fi * b_r

    kcol = jnp.dot(c_pow0, jnp.concatenate([bb_r, bb_i], axis=0), precision=hi,
                   preferred_element_type=F32)
    lane_h = lax.broadcasted_iota(jnp.int32, (H, LANES), 1) % H
    skip = jnp.where(lane_h == lax.broadcasted_iota(jnp.int32, (H, LANES), 0), dcol_ref[...], 0.0)

    @pl.when(pl.program_id(0) == 0)
    def _():
        z_ref[0:TH, :] = jnp.zeros((TH, LANES), F32)

    z_ref[TH:2 * TH, :] = kcol
    z_ref[TH:TH + H, :] = kcol[:H] + skip
    lane_group = lax.broadcasted_iota(jnp.int32, (1, LANES), 1) // H
    groups_per_block = LANES // H
    blocks = []
    for v in range(TH // LANES):
        blk = None
        for u in range(groups_per_block):
            s = v * groups_per_block + u
            piece = z_ref[TH - H * s:2 * TH - H * s, :]
            blk = piece if blk is None else jnp.where(lane_group == u, piece, blk)
        blocks.append(blk.astype(BF16))
    mt = jnp.concatenate(blocks, axis=1)

    expo = (groups_per_block - 1 - lane_group).astype(F32)
    wr, wi = _cis(lr * expo, li * expo)
    hop_r, hop_i = _cpow2(abar_r, abar_i, groups_per_block)
    w1_r, w1_i = [], []
    for v in range(TH // LANES):
        w1_r.insert(0, wr * bb_r - wi * bb_i)
        w1_i.insert(0, wr * bb_i + wi * bb_r)
        wr, wi = _cmul(wr, wi, hop_r, hop_i)
    w1t = jnp.concatenate([jnp.concatenate(w1_r, axis=1),
                           jnp.concatenate(w1_i, axis=1)], axis=0).astype(BF16)

    e = e_ref[...].reshape(TH, e_ref.shape[-1])
    y = jnp.dot(mt, e, preferred_element_type=F32)
    st = jnp.dot(w1t, e, preferred_element_type=F32)
    sr, si = st[:P], st[P:]
    pos = lax.broadcasted_iota(jnp.int32, sr.shape, 1) % chunks_per_seq

    def shifted(a, shift):
        return jnp.where(pos >= shift, pltpu.roll(a, shift, 1), 0.0)

    qr, qi = _cpow2(abar_r, abar_i, T)
    shift = 1
    while shift < chunks_per_seq:
        srs, sis = shifted(sr, shift), shifted(si, shift)
        sr, si = sr + qr * srs - qi * sis, si + qr * sis + qi * srs
        qr, qi = qr * qr - qi * qi, 2.0 * qr * qi
        shift *= 2
    x_prev = jnp.concatenate([shifted(sr, 1), shifted(si, 1)], axis=0).astype(BF16)
    y = y + jnp.dot(c_pow1.astype(BF16), x_prev, preferred_element_type=F32)
    f_ref[...] = _gelu_tanh(y).reshape(f_ref.shape)


def _ssm_call(arow, acol, ldt, bt, cab, dcol, e, chunks_per_seq):
    T, _, NC = e.shape
    G = arow.shape[0]
    grp = lambda a: pl.BlockSpec((None,) + a.shape[1:], lambda g: (g,) + (0,) * (a.ndim - 1))
    return pl.pallas_call(
        functools.partial(_ssm_kernel, chunks_per_seq=chunks_per_seq),
        grid=(G,),
        in_specs=[grp(a) for a in (arow, acol, ldt, bt, cab, dcol)]
                 + [pl.BlockSpec((T, SSM_GROUP_CH, NC), lambda g: (0, g, 0))],
        out_specs=pl.BlockSpec((T, SSM_GROUP_CH, NC), lambda g: (0, g, 0)),
        out_shape=jax.ShapeDtypeStruct((T, SSM_WIDTH, NC), F32),
        scratch_shapes=[pltpu.VMEM((2 * T * SSM_GROUP_CH, LANES), F32)],
        compiler_params=pltpu.CompilerParams(
            dimension_semantics=("arbitrary",), vmem_limit_bytes=VMEM_LIMIT),
        name="ssm",
    )(arow, acol, ldt, bt, cab, dcol, e)


def _ssm_param_layouts(a_re, a_im, log_dt, b_re, b_im, c_re, c_im, d_skip):
    G = a_re.shape[0]
    arow = jnp.stack([jnp.concatenate([a_re, a_re], -1), jnp.concatenate([a_im, a_im], -1)], 1)
    acol = jnp.stack([a_re, a_im], -1)
    reps = LANES // SSM_GROUP_CH
    bt = jnp.stack([jnp.tile(b_re, (1, 1, reps)), jnp.tile(b_im, (1, 1, reps))], 1)
    cab = jnp.stack([jnp.concatenate([c_re, -c_im], -1), jnp.concatenate([-c_im, -c_re], -1)], 1)
    return (arow.astype(F32), acol.astype(F32), log_dt.reshape(G, 1, 1).astype(F32),
            bt.astype(F32), cab.astype(F32), d_skip.reshape(G, SSM_GROUP_CH, 1).astype(F32))


def _tail_kernel(h1_ref, attn_ref, y_ref, p_ref, wglu_ref, bglu_ref, ga_ref, gs_ref,
                 woa_ref, wos_ref, g2_ref, w1_ref, w3_ref, w2_ref, gp_ref, wpg_ref,
                 wpp_ref, gf_ref, o_ref):
    y = y_ref[...]
    glu = y * _sigmoid(jnp.dot(y.astype(BF16), wglu_ref[...], preferred_element_type=F32)
                       + bglu_ref[...])
    an = _rms(attn_ref[...], ga_ref[...]).astype(BF16)
    sn = _rms(glu, gs_ref[...]).astype(BF16)
    h = (h1_ref[...] + jnp.dot(an, woa_ref[...], preferred_element_type=F32)
         + jnp.dot(sn, wos_ref[...], preferred_element_type=F32))
    h = h + 0.5 * _swiglu(_rms(h, g2_ref[...]).astype(BF16), w1_ref, w3_ref, w2_ref)
    gate = _sigmoid(jnp.dot(_rms(h, gp_ref[...]).astype(BF16), wpg_ref[...],
                            preferred_element_type=F32))
    h = h + gate * jnp.dot(p_ref[...].astype(BF16), wpp_ref[...], preferred_element_type=F32)
    o_ref[...] = _rms(h, gf_ref[...])


def _tail_call(h1, attn, y, p, *consts):
    B, L, D = h1.shape
    tm = TOKEN_TILE
    tile = lambda w: pl.BlockSpec((None, tm, w), lambda b, i: (b, i, 0))
    return pl.pallas_call(
        _tail_kernel,
        grid=(B, L // tm),
        in_specs=[tile(D), tile(ATTN_WIDTH), tile(SSM_WIDTH), tile(PLE_DIM)]
                 + [_const_spec(c.shape) for c in consts],
        out_specs=tile(D),
        out_shape=jax.ShapeDtypeStruct((B, L, D), F32),
        compiler_params=pltpu.CompilerParams(
            dimension_semantics=("arbitrary", "arbitrary"),
            vmem_limit_bytes=VMEM_LIMIT),
        name="tail",
    )(h1, attn, y, p, *consts)


def kernel(x, p, g_ffn1, w1_a, w3_a, w2_a, g_mix, w_in, b_f, a_re, a_im, log_dt, b_re, b_im, c_re, c_im, d_skip, w_glu, b_glu, g_attn_out, g_ssm_out, w_out, g_ffn2, w1_b, w3_b, w2_b, g_ple, w_ple_gate, w_ple_proj, g_final):
    B, L, D = x.shape
    assert D == D_MODEL and L % ATTN_TILE == 0 and L % TOKEN_TILE == 0 and L % SSM_CHUNK == 0
    assert g_ffn1.shape[0] == 1, "single layer"
    assert TOKEN_TILE == ATTN_TILE, "decay bias rows are relative to the kv tile start"
    row = lambda g: g.reshape(1, -1).astype(F32)
    bf = lambda w: w.astype(BF16)
    s_v, s_f = 3 * ATTN_WIDTH, 3 * ATTN_WIDTH + ATTN_HEADS
    w_in0 = w_in[0]
    scale = LOG2E / math.sqrt(HEAD_DIM)
    wqkv = jnp.concatenate([w_in0[:, :ATTN_WIDTH] * scale, w_in0[:, ATTN_WIDTH:s_v]], axis=1)
    wf = jnp.pad(w_in0[:, s_v:s_f], ((0, 0), (0, LANES - ATTN_HEADS)))

    h1, q, kt, v, s_in, ct, k_norm2 = _head_call(
        x, row(g_ffn1[0]), bf(w1_a[0]), bf(w3_a[0]), bf(w2_a[0]), row(g_mix[0]),
        bf(wqkv), bf(w_in0[:, s_f:]), bf(wf), b_f[0].reshape(ATTN_HEADS, 1).astype(F32))

    attn = _attn_call(q, kt, v, ct, k_norm2)

    T = SSM_CHUNK
    chunks_per_seq = L // T
    n_chunks = B * chunks_per_seq
    e = s_in.reshape(n_chunks, T, SSM_WIDTH).transpose(1, 2, 0)
    f = _ssm_call(*_ssm_param_layouts(a_re[0], a_im[0], log_dt[0], b_re[0], b_im[0],
                                      c_re[0], c_im[0], d_skip[0]),
                  e, chunks_per_seq)
    y = f.transpose(2, 0, 1).reshape(B, L, SSM_WIDTH)

    w_out0 = w_out[0]
    return _tail_call(
        h1, attn, y, p[0],
        bf(w_glu[0]), row(b_glu[0]), row(g_attn_out[0]), row(g_ssm_out[0]),
        bf(w_out0[:ATTN_WIDTH]), bf(w_out0[ATTN_WIDTH:]), row(g_ffn2[0]),
        bf(w1_b[0]), bf(w3_b[0]), bf(w2_b[0]), row(g_ple[0]), bf(w_ple_gate[0]),
        bf(w_ple_proj[0]), row(g_final))
```

```python
import functools
import math

import jax
import jax.numpy as jnp
from jax import lax
from jax.experimental import pallas as pl
from jax.experimental.pallas import tpu as pltpu

D_MODEL = 1024
ATTN_HEADS = 8
HEAD_DIM = 64
ATTN_WIDTH = ATTN_HEADS * HEAD_DIM
SSM_WIDTH = D_MODEL - ATTN_WIDTH
SSM_GROUP_CH = 16
SSM_GROUPS = SSM_WIDTH // SSM_GROUP_CH
SSM_STATE = 64
D_FF = 2816
PLE_DIM = 256
EPS = 1e-6

LANES = 128
HEAD_PAIR = 2 * HEAD_DIM
N_PAIRS = ATTN_HEADS // 2
FF_CHUNK = 256
TOKEN_TILE = 512
ATTN_TILE = 512
SSM_CHUNK = 32
NEG_BIG = -1e30
SKIP_LOG2 = 140.0
BOUND_SLACK_MUL = 1.001
BOUND_SLACK_ADD = 1.0
LOG2E = math.log2(math.e)
N_BIAS = 3
BIAS_ROWS = 8
VMEM_LIMIT = 56 * 1024 * 1024

BF16 = jnp.bfloat16
F32 = jnp.float32


def _rms(x, g):
    ms = jnp.mean(x * x, axis=-1, keepdims=True)
    return x * lax.rsqrt(ms + EPS) * g


def _sigmoid(x):
    return 1.0 / (1.0 + jnp.exp(-x))


def _swiglu(xn, w1_ref, w3_ref, w2_ref):
    acc = None
    for c in range(D_FF // FF_CHUNK):
        sl = slice(c * FF_CHUNK, (c + 1) * FF_CHUNK)
        a = jnp.dot(xn, w1_ref[:, sl], preferred_element_type=F32)
        b = jnp.dot(xn, w3_ref[:, sl], preferred_element_type=F32)
        gated = (a * _sigmoid(a) * b).astype(BF16)
        part = jnp.dot(gated, w2_ref[sl, :], preferred_element_type=F32)
        acc = part if acc is None else acc + part
    return acc


def _const_spec(shape):
    nd = len(shape)
    return pl.BlockSpec(shape, lambda *_: (0,) * nd, pipeline_mode=pl.Buffered(1))


def _head_kernel(x_ref, g1_ref, w1_ref, w3_ref, w2_ref, gm_ref, wkvq_ref, ws_ref,
                 wf_ref, bf_ref, h1_ref, q_ref, kt_ref, v_ref, s_ref, ct_ref, kn_ref,
                 carry_ref):
    tm = x_ref.shape[0]
    x = x_ref[...]
    h1 = x + 0.5 * _swiglu(_rms(x, g1_ref[...]).astype(BF16), w1_ref, w3_ref, w2_ref)
    h1_ref[...] = h1
    un = _rms(h1, gm_ref[...]).astype(BF16)
    zf = jnp.dot(un, wf_ref[...], preferred_element_type=F32)
    kv = jnp.dot(un, wkvq_ref[:, :2 * ATTN_WIDTH], preferred_element_type=F32)
    zft = zf.T[:ATTN_HEADS, :] + bf_ref[...]
    logf = jnp.minimum(zft, 0.0) - jnp.log1p(jnp.exp(-jnp.abs(zft)))
    lane = lax.broadcasted_iota(jnp.int32, logf.shape, 1)
    c = logf
    shift = 1
    while shift < tm:
        c = c + jnp.where(lane >= shift, pltpu.roll(c, shift, 1), 0.0)
        shift *= 2

    @pl.when(pl.program_id(1) == 0)
    def _():
        carry_ref[...] = jnp.zeros_like(carry_ref)

    c_abs = c + carry_ref[:, 0:1]
    ct_ref[...] = c_abs * LOG2E
    carry_ref[...] = jnp.broadcast_to(c_abs[:, tm - 1:tm], carry_ref.shape)

    rel = (c - c[:, 0:1]) * LOG2E
    hi = rel.astype(BF16).astype(F32)
    mid = (rel - hi).astype(BF16).astype(F32)
    lo = (rel - hi - mid).astype(BF16).astype(F32)
    kt = kv[:, :ATTN_WIDTH].astype(BF16).astype(F32).T
    vv = kv[:, ATTN_WIDTH:]
    k_sq = (kt * kt).reshape(ATTN_HEADS, HEAD_DIM, tm)
    kn_ref[...] = jnp.broadcast_to(
        jnp.max(jnp.sum(k_sq, axis=1), axis=-1, keepdims=True), kn_ref.shape)
    sub = lax.broadcasted_iota(jnp.int32, (BIAS_ROWS, tm), 0)
    zeros = jnp.zeros((HEAD_DIM - BIAS_ROWS, tm), F32)
    vlane = lax.broadcasted_iota(jnp.int32, (tm, HEAD_PAIR), 1)
    for h in range(ATTN_HEADS):
        bias = jnp.where(sub == 0, -hi[h:h + 1],
                         jnp.where(sub == 1, -mid[h:h + 1],
                                   jnp.where(sub == 2, -lo[h:h + 1], 0.0)))
        k_h = kt[h * HEAD_DIM:(h + 1) * HEAD_DIM]
        vp = vv[:, (h // 2) * HEAD_PAIR:(h // 2 + 1) * HEAD_PAIR]
        if h % 2 == 0:
            kt_ref[h] = jnp.concatenate([k_h, bias, zeros], axis=0).astype(BF16)
            v_ref[h] = jnp.where(vlane < HEAD_DIM, vp,
                                 jnp.where(vlane == HEAD_DIM, 1.0, 0.0)).astype(BF16)
        else:
            kt_ref[h] = jnp.concatenate([bias, zeros, k_h], axis=0).astype(BF16)
            v_ref[h] = jnp.where(vlane >= HEAD_DIM, vp,
                                 jnp.where(vlane == 0, 1.0, 0.0)).astype(BF16)

    q_ref[...] = jnp.dot(un, wkvq_ref[:, 2 * ATTN_WIDTH:], preferred_element_type=F32).astype(BF16)
    s_ref[...] = jnp.dot(un, ws_ref[...], preferred_element_type=F32).astype(BF16)


def _head_call(x, g1, w1, w3, w2, gm, wqkv, ws, wf, bf):
    B, L, D = x.shape
    tm = TOKEN_TILE
    tile = lambda w: pl.BlockSpec((None, tm, w), lambda b, i: (b, i, 0))
    out_shape = (
        jax.ShapeDtypeStruct((B, L, D), F32),
        jax.ShapeDtypeStruct((B, L, ATTN_WIDTH), BF16),
        jax.ShapeDtypeStruct((B, ATTN_HEADS, HEAD_PAIR, L), BF16),
        jax.ShapeDtypeStruct((B, ATTN_HEADS, L, HEAD_PAIR), BF16),
        jax.ShapeDtypeStruct((B, L, SSM_WIDTH), BF16),
        jax.ShapeDtypeStruct((B, ATTN_HEADS, L), F32),
        jax.ShapeDtypeStruct((B, L // tm, ATTN_HEADS, LANES), F32),
    )
    return pl.pallas_call(
        _head_kernel,
        grid=(B, L // tm),
        in_specs=[tile(D), _const_spec(g1.shape), _const_spec(w1.shape),
                  _const_spec(w3.shape), _const_spec(w2.shape), _const_spec(gm.shape),
                  _const_spec(wqkv.shape), _const_spec(ws.shape), _const_spec(wf.shape),
                  _const_spec(bf.shape)],
        out_specs=(tile(D), tile(ATTN_WIDTH),
                   pl.BlockSpec((None, ATTN_HEADS, HEAD_PAIR, tm), lambda b, i: (b, 0, 0, i)),
                   pl.BlockSpec((None, ATTN_HEADS, tm, HEAD_PAIR), lambda b, i: (b, 0, i, 0)),
                   tile(SSM_WIDTH),
                   pl.BlockSpec((None, ATTN_HEADS, tm), lambda b, i: (b, 0, i)),
                   pl.BlockSpec((None, None, ATTN_HEADS, LANES), lambda b, i: (b, i, 0, 0))),
        out_shape=out_shape,
        scratch_shapes=[pltpu.VMEM((ATTN_HEADS, LANES), F32)],
        compiler_params=pltpu.CompilerParams(
            dimension_semantics=("arbitrary", "arbitrary"),
            vmem_limit_bytes=VMEM_LIMIT),
        name="head",
    )(x, g1, w1, w3, w2, gm, wqkv, ws, wf, bf)


def _attn_kernel(q_ref, kt_ref, v_ref, c_ref, st_ref, o_ref, s_e0, s_e1, s_o0, s_o1, m_ref, acc_ref):
    t = q_ref.shape[0]
    n = pl.program_id(2)
    q = q_ref[...]
    lane = lax.broadcasted_iota(jnp.int32, (1, HEAD_PAIR), 1)
    first = lane < HEAD_DIM
    ones_even = jnp.where((lane >= HEAD_DIM) & (lane < HEAD_DIM + N_BIAS), 1.0, 0.0).astype(BF16)
    ones_odd = jnp.where(lane < N_BIAS, 1.0, 0.0).astype(BF16)
    q_heads = (jnp.where(first, q, ones_even), jnp.where(first, ones_odd, q))
    s_buf = ((s_e0, s_e1), (s_o0, s_o1))

    def start(kv):
        return pl.multiple_of(kv * t, t)

    c_q = [c_ref[h:h + 1, pl.ds(start(n), LANES)][:, 0:1] for h in range(2)]

    def qk(h, kv):
        return jnp.dot(q_heads[h], kt_ref[h, :, pl.ds(start(kv), t)], preferred_element_type=F32)

    def consume(h, s, kv):
        d = c_ref[h:h + 1, pl.ds(start(kv), LANES)][:, 0:1] - c_q[h]
        m_old = m_ref[h]
        m_new = jnp.maximum(m_old, jnp.max(s, axis=-1, keepdims=True) - d)
        p = jnp.exp2((s - (m_new + d)).astype(BF16))
        m_ref[h] = m_new
        acc_ref[h] = jnp.exp2(m_old - m_new) * acc_ref[h] + jnp.dot(
            p, v_ref[h, pl.ds(start(kv), t), :], preferred_element_type=F32)

    def step(kv, par):
        for h in range(2):
            s_buf[1 - par][h][...] = qk(h, kv - 1)
        for h in range(2):
            consume(h, s_buf[par][h][...], kv)

    m_ref[...] = jnp.full(m_ref.shape, NEG_BIG, F32)
    acc_ref[...] = jnp.zeros(acc_ref.shape, F32)
    for h in range(2):
        s_buf[1][h][...] = qk(h, n)

    q_sq = q.astype(F32) * q.astype(F32)
    reach = []
    for h in range(2):
        q_norm2 = jnp.max(jnp.sum(jnp.where(first == (h == 0), q_sq, 0.0), axis=-1, keepdims=True),
                          axis=0, keepdims=True)
        k_norm2, c_next = st_ref[h:h + 1, :], st_ref[2 + h:3 + h, :]
        reach.append(jnp.sqrt(q_norm2 * k_norm2) * BOUND_SLACK_MUL + BOUND_SLACK_ADD
                     + c_q[h] - c_next)

    @pl.when(n >= 0)
    def _():
        row = lax.broadcasted_iota(jnp.int32, (t, t), 0)
        col = lax.broadcasted_iota(jnp.int32, (t, t), 1)
        for h in range(2):
            s_buf[0][h][...] = qk(h, jnp.maximum(n - 1, 0))
        for h in range(2):
            consume(h, jnp.where(col <= row, s_buf[1][h][...], NEG_BIG), n)

    tile_id = lax.broadcasted_iota(jnp.int32, (1, LANES), 1)
    needed = tile_id < 0
    for h in range(2):
        m_low = jnp.min(m_ref[h], axis=0, keepdims=True)
        needed = needed | (reach[h] - m_low >= -SKIP_LOG2)
    n_f = n.astype(F32)
    first_needed = jnp.min(jnp.where(needed & (tile_id < n), tile_id.astype(F32), n_f))
    count = n - first_needed.astype(jnp.int32)

    def pair(i, carry):
        step(n - 1 - 2 * i, 0)
        step(n - 2 - 2 * i, 1)
        return carry

    n_pairs = jnp.maximum(count - 1, 0) // 2
    lax.fori_loop(0, n_pairs, pair, 0)
    left = count - 2 * n_pairs
    last = n - count

    @pl.when(left == 2)
    def _():
        step(last + 1, 0)
        for h in range(2):
            consume(h, s_buf[1][h][...], last)

    @pl.when(left == 1)
    def _():
        for h in range(2):
            consume(h, s_buf[0][h][...], last)

    acc0, acc1 = acc_ref[0], acc_ref[1]
    o_ref[...] = jnp.where(first, acc0 / acc0[:, HEAD_DIM:HEAD_DIM + 1], acc1 / acc1[:, 0:1])


def _attn_call(q, kt, v, ct, k_norm2):
    B, L, _ = q.shape
    t = ATTN_TILE
    n_tiles = L // t
    assert n_tiles <= LANES
    c4 = ct.reshape(B, N_PAIRS, 2, L)
    c_next = jnp.roll(ct[:, :, ::t], -1, axis=-1)
    per_tile = jnp.stack([k_norm2[..., 0].transpose(0, 2, 1), c_next], axis=1)
    per_tile = jnp.pad(per_tile, ((0, 0),) * 3 + ((0, LANES - n_tiles),))
    stats = (per_tile.reshape(B, 2, N_PAIRS, 2, LANES).transpose(0, 2, 1, 3, 4)
             .reshape(B, N_PAIRS, 4, LANES))
    pair_block = lambda *shape: pl.BlockSpec((None, 2) + shape, lambda b, p, i: (b, p, 0, 0))
    return pl.pallas_call(
        _attn_kernel,
        grid=(B, N_PAIRS, L // t),
        in_specs=[pl.BlockSpec((None, t, HEAD_PAIR), lambda b, p, i: (b, i, p)),
                  pair_block(HEAD_PAIR, L), pair_block(L, HEAD_PAIR),
                  pl.BlockSpec((None, None, 2, L), lambda b, p, i: (b, p, 0, 0)),
                  pl.BlockSpec((None, None, 4, LANES), lambda b, p, i: (b, p, 0, 0))],
        out_specs=pl.BlockSpec((None, t, HEAD_PAIR), lambda b, p, i: (b, i, p)),
        out_shape=jax.ShapeDtypeStruct((B, L, ATTN_WIDTH), F32),
        scratch_shapes=[pltpu.VMEM((t, t), F32)] * 4
                       + [pltpu.VMEM((2, t, 1), F32), pltpu.VMEM((2, t, HEAD_PAIR), F32)],
        compiler_params=pltpu.CompilerParams(
            dimension_semantics=("arbitrary", "arbitrary", "arbitrary"),
            vmem_limit_bytes=VMEM_LIMIT),
        name="attn",
    )(q, kt, v, c4, stats)


def _gelu_tanh(x):
    return 0.5 * x * (1.0 + jnp.tanh(math.sqrt(2.0 / math.pi) * (x + 0.044715 * (x * x * x))))


def _cis(mag_arg, ang):
    mag = jnp.exp(mag_arg)
    return mag * jnp.cos(ang), mag * jnp.sin(ang)


def _cmul(ar, ai, br, bi):
    return ar * br - ai * bi, ar * bi + ai * br


def _cpow2(zr, zi, n):
    assert n & (n - 1) == 0
    while n > 1:
        zr, zi = zr * zr - zi * zi, 2.0 * zr * zi
        n //= 2
    return zr, zi


def _ssm_kernel(arow_ref, acol_ref, ldt_ref, bt_ref, cab_ref, dcol_ref, e_ref, f_ref, z_ref,
                *, chunks_per_seq):
    T, P, H = SSM_CHUNK, SSM_STATE, SSM_GROUP_CH
    TH = T * H
    hi = lax.Precision.HIGHEST
    dt = jnp.exp(ldt_ref[...])

    lam_r, lam_i = dt * arow_ref[0:1, :], dt * arow_ref[1:2, :]
    j0 = lax.broadcasted_iota(jnp.int32, (T, 2 * P), 0).astype(F32)
    pa0, pb0 = _cis(j0 * lam_r, j0 * lam_i)
    pa1, pb1 = _cmul(pa0, pb0, *_cis(lam_r, lam_i))
    over_h = lambda a: jnp.concatenate(
        [jnp.broadcast_to(a[j:j + 1, :], (H, 2 * P)) for j in range(T)], axis=0)
    ca, cb = jnp.tile(cab_ref[0], (T, 1)), jnp.tile(cab_ref[1], (T, 1))
    c_pow0 = over_h(pa0) * ca + over_h(pb0) * cb
    c_pow1 = over_h(pa1) * ca + over_h(pb1) * cb

    a_r, a_i = acol_ref[:, 0:1], acol_ref[:, 1:2]
    lr, li = dt * a_r, dt * a_i
    abar_r, abar_i = _cis(lr, li)
    nr, ni = abar_r - 1.0, abar_i
    den = a_r * a_r + a_i * a_i
    fr, fi = (nr * a_r + ni * a_i) / den, (ni * a_r - nr * a_i) / den
    b_r, b_i = bt_ref[0], bt_ref[1]
    bb_r, bb_i = fr * b_r - fi * b_i, fr * b_i + fi * b_r

    kcol = jnp.dot(c_pow0, jnp.concatenate([bb_r, bb_i], axis=0), precision=hi,
                   preferred_element_type=F32)
    lane_h = lax.broadcasted_iota(jnp.int32, (H, LANES), 1) % H
    skip = jnp.where(lane_h == lax.broadcasted_iota(jnp.int32, (H, LANES), 0), dcol_ref[...], 0.0)

    @pl.when(pl.program_id(0) == 0)
    def _():
        z_ref[0:TH, :] = jnp.zeros((TH, LANES), F32)

    z_ref[TH:2 * TH, :] = kcol
    z_ref[TH:TH + H, :] = kcol[:H] + skip
    lane_group = lax.broadcasted_iota(jnp.int32, (1, LANES), 1) // H
    groups_per_block = LANES // H
    blocks = []
    for v in range(TH // LANES):
        blk = None
        for u in range(groups_per_block):
            s = v * groups_per_block + u
            piece = z_ref[TH - H * s:2 * TH - H * s, :]
            blk = piece if blk is None else jnp.where(lane_group == u, piece, blk)
        blocks.append(blk.astype(BF16))
    mt = jnp.concatenate(blocks, axis=1)

    expo = (groups_per_block - 1 - lane_group).astype(F32)
    wr, wi = _cis(lr * expo, li * expo)
    hop_r, hop_i = _cpow2(abar_r, abar_i, groups_per_block)
    w1_r, w1_i = [], []
    for v in range(TH // LANES):
        w1_r.insert(0, wr * bb_r - wi * bb_i)
        w1_i.insert(0, wr * bb_i + wi * bb_r)
        wr, wi = _cmul(wr, wi, hop_r, hop_i)
    w1t = jnp.concatenate([jnp.concatenate(w1_r, axis=1),
                           jnp.concatenate(w1_i, axis=1)], axis=0).astype(BF16)

    e = e_ref[...].reshape(TH, e_ref.shape[-1])
    y = jnp.dot(mt, e, preferred_element_type=F32)
    st = jnp.dot(w1t, e, preferred_element_type=F32)
    sr, si = st[:P], st[P:]
    pos = lax.broadcasted_iota(jnp.int32, sr.shape, 1) % chunks_per_seq

    def shifted(a, shift):
        return jnp.where(pos >= shift, pltpu.roll(a, shift, 1), 0.0)

    qr, qi = _cpow2(abar_r, abar_i, T)
    shift = 1
    while shift < chunks_per_seq:
        srs, sis = shifted(sr, shift), shifted(si, shift)
        sr, si = sr + qr * srs - qi * sis, si + qr * sis + qi * srs
        qr, qi = qr * qr - qi * qi, 2.0 * qr * qi
        shift *= 2
    x_prev = jnp.concatenate([shifted(sr, 1), shifted(si, 1)], axis=0).astype(BF16)
    y = y + jnp.dot(c_pow1.astype(BF16), x_prev, preferred_element_type=F32)
    f_ref[...] = _gelu_tanh(y).reshape(f_ref.shape)


def _ssm_call(arow, acol, ldt, bt, cab, dcol, e, chunks_per_seq):
    T, _, NC = e.shape
    G = arow.shape[0]
    grp = lambda a: pl.BlockSpec((None,) + a.shape[1:], lambda g: (g,) + (0,) * (a.ndim - 1))
    return pl.pallas_call(
        functools.partial(_ssm_kernel, chunks_per_seq=chunks_per_seq),
        grid=(G,),
        in_specs=[grp(a) for a in (arow, acol, ldt, bt, cab, dcol)]
                 + [pl.BlockSpec((T, SSM_GROUP_CH, NC), lambda g: (0, g, 0))],
        out_specs=pl.BlockSpec((T, SSM_GROUP_CH, NC), lambda g: (0, g, 0)),
        out_shape=jax.ShapeDtypeStruct((T, SSM_WIDTH, NC), F32),
        scratch_shapes=[pltpu.VMEM((2 * T * SSM_GROUP_CH, LANES), F32)],
        compiler_params=pltpu.CompilerParams(
            dimension_semantics=("arbitrary",), vmem_limit_bytes=VMEM_LIMIT),
        name="ssm",
    )(arow, acol, ldt, bt, cab, dcol, e)


def _ssm_param_layouts(a_re, a_im, log_dt, b_re, b_im, c_re, c_im, d_skip):
    G = a_re.shape[0]
    arow = jnp.stack([jnp.concatenate([a_re, a_re], -1), jnp.concatenate([a_im, a_im], -1)], 1)
    acol = jnp.stack([a_re, a_im], -1)
    reps = LANES // SSM_GROUP_CH
    bt = jnp.stack([jnp.tile(b_re, (1, 1, reps)), jnp.tile(b_im, (1, 1, reps))], 1)
    cab = jnp.stack([jnp.concatenate([c_re, -c_im], -1), jnp.concatenate([-c_im, -c_re], -1)], 1)
    return (arow.astype(F32), acol.astype(F32), log_dt.reshape(G, 1, 1).astype(F32),
            bt.astype(F32), cab.astype(F32), d_skip.reshape(G, SSM_GROUP_CH, 1).astype(F32))


def _tail_kernel(h1_ref, attn_ref, y_ref, p_ref, wglu_ref, bglu_ref, ga_ref, gs_ref,
                 woa_ref, wos_ref, g2_ref, w1_ref, w3_ref, w2_ref, gp_ref, wpg_ref,
                 wpp_ref, gf_ref, o_ref):
    y = y_ref[...]
    glu = y * _sigmoid(jnp.dot(y.astype(BF16), wglu_ref[...], preferred_element_type=F32)
                       + bglu_ref[...])
    an = _rms(attn_ref[...], ga_ref[...]).astype(BF16)
    sn = _rms(glu, gs_ref[...]).astype(BF16)
    h = (h1_ref[...] + jnp.dot(an, woa_ref[...], preferred_element_type=F32)
         + jnp.dot(sn, wos_ref[...], preferred_element_type=F32))
    h = h + 0.5 * _swiglu(_rms(h, g2_ref[...]).astype(BF16), w1_ref, w3_ref, w2_ref)
    gate = _sigmoid(jnp.dot(_rms(h, gp_ref[...]).astype(BF16), wpg_ref[...],
                            preferred_element_type=F32))
    h = h + gate * jnp.dot(p_ref[...].astype(BF16), wpp_ref[...], preferred_element_type=F32)
    o_ref[...] = _rms(h, gf_ref[...])


def _tail_call(h1, attn, y, p, *consts):
    B, L, D = h1.shape
    tm = TOKEN_TILE
    tile = lambda w: pl.BlockSpec((None, tm, w), lambda b, i: (b, i, 0))
    return pl.pallas_call(
        _tail_kernel,
        grid=(B, L // tm),
        in_specs=[tile(D), tile(ATTN_WIDTH), tile(SSM_WIDTH), tile(PLE_DIM)]
                 + [_const_spec(c.shape) for c in consts],
        out_specs=tile(D),
        out_shape=jax.ShapeDtypeStruct((B, L, D), F32),
        compiler_params=pltpu.CompilerParams(
            dimension_semantics=("arbitrary", "arbitrary"),
            vmem_limit_bytes=VMEM_LIMIT),
        name="tail",
    )(h1, attn, y, p, *consts)


def kernel(x, p, g_ffn1, w1_a, w3_a, w2_a, g_mix, w_in, b_f, a_re, a_im, log_dt, b_re, b_im, c_re, c_im, d_skip, w_glu, b_glu, g_attn_out, g_ssm_out, w_out, g_ffn2, w1_b, w3_b, w2_b, g_ple, w_ple_gate, w_ple_proj, g_final):
    B, L, D = x.shape
    assert D == D_MODEL and L % ATTN_TILE == 0 and L % TOKEN_TILE == 0 and L % SSM_CHUNK == 0
    assert g_ffn1.shape[0] == 1, "single layer"
    assert TOKEN_TILE == ATTN_TILE, "decay bias rows are relative to the kv tile start"
    row = lambda g: g.reshape(1, -1).astype(F32)
    bf = lambda w: w.astype(BF16)
    s_v, s_f = 3 * ATTN_WIDTH, 3 * ATTN_WIDTH + ATTN_HEADS
    w_in0 = w_in[0]
    scale = LOG2E / math.sqrt(HEAD_DIM)
    wqkv = jnp.concatenate([w_in0[:, ATTN_WIDTH:s_v], w_in0[:, :ATTN_WIDTH] * scale], axis=1)
    wf = jnp.pad(w_in0[:, s_v:s_f], ((0, 0), (0, LANES - ATTN_HEADS)))

    h1, q, kt, v, s_in, ct, k_norm2 = _head_call(
        x, row(g_ffn1[0]), bf(w1_a[0]), bf(w3_a[0]), bf(w2_a[0]), row(g_mix[0]),
        bf(wqkv), bf(w_in0[:, s_f:]), bf(wf), b_f[0].reshape(ATTN_HEADS, 1).astype(F32))

    attn = _attn_call(q, kt, v, ct, k_norm2)

    T = SSM_CHUNK
    chunks_per_seq = L // T
    n_chunks = B * chunks_per_seq
    e = s_in.reshape(n_chunks, T, SSM_WIDTH).transpose(1, 2, 0)
    f = _ssm_call(*_ssm_param_layouts(a_re[0], a_im[0], log_dt[0], b_re[0], b_im[0],
                                      c_re[0], c_im[0], d_skip[0]),
                  e, chunks_per_seq)
    y = f.transpose(2, 0, 1).reshape(B, L, SSM_WIDTH)

    w_out0 = w_out[0]
    return _tail_call(
        h1, attn, y, p[0],
        bf(w_glu[0]), row(b_glu[0]), row(g_attn_out[0]), row(g_ssm_out[0]),
        bf(w_out0[:ATTN_WIDTH]), bf(w_out0[ATTN_WIDTH:]), row(g_ffn2[0]),
        bf(w1_b[0]), bf(w3_b[0]), bf(w2_b[0]), row(g_ple[0]), bf(w_ple_gate[0]),
        bf(w_ple_proj[0]), row(g_final))
```

```python
import functools
import math

import jax
import jax.numpy as jnp
from jax import lax
from jax.experimental import pallas as pl
from jax.experimental.pallas import tpu as pltpu

D_MODEL = 1024
ATTN_HEADS = 8
HEAD_DIM = 64
ATTN_WIDTH = ATTN_HEADS * HEAD_DIM
SSM_WIDTH = D_MODEL - ATTN_WIDTH
SSM_GROUP_CH = 16
SSM_GROUPS = SSM_WIDTH // SSM_GROUP_CH
SSM_STATE = 64
D_FF = 2816
PLE_DIM = 256
EPS = 1e-6

LANES = 128
HEAD_PAIR = 2 * HEAD_DIM
N_PAIRS = ATTN_HEADS // 2
FF_CHUNK = 256
TOKEN_TILE = 512
ATTN_TILE = 512
SSM_CHUNK = 32
NEG_BIG = -1e30
SKIP_LOG2 = 140.0
BOUND_SLACK_MUL = 1.001
BOUND_SLACK_ADD = 1.0
LOG2E = math.log2(math.e)
N_BIAS = 3
BIAS_ROWS = 8
VMEM_LIMIT = 56 * 1024 * 1024

BF16 = jnp.bfloat16
F32 = jnp.float32


def _rms(x, g):
    ms = jnp.mean(x * x, axis=-1, keepdims=True)
    return x * lax.rsqrt(ms + EPS) * g


def _sigmoid(x):
    return 1.0 / (1.0 + jnp.exp(-x))


def _swiglu(xn, w1_ref, w3_ref, w2_ref):
    acc = None
    for c in range(D_FF // FF_CHUNK):
        sl = slice(c * FF_CHUNK, (c + 1) * FF_CHUNK)
        a = jnp.dot(xn, w1_ref[:, sl], preferred_element_type=F32)
        b = jnp.dot(xn, w3_ref[:, sl], preferred_element_type=F32)
        gated = (a * _sigmoid(a) * b).astype(BF16)
        part = jnp.dot(gated, w2_ref[sl, :], preferred_element_type=F32)
        acc = part if acc is None else acc + part
    return acc


def _const_spec(shape):
    nd = len(shape)
    return pl.BlockSpec(shape, lambda *_: (0,) * nd, pipeline_mode=pl.Buffered(1))


def _head_kernel(x_ref, g1_ref, w1_ref, w3_ref, w2_ref, gm_ref, wkvq_ref, ws_ref,
                 wf_ref, bf_ref, h1_ref, q_ref, kt_ref, v_ref, s_ref, ct_ref, kn_ref,
                 carry_ref):
    tm = x_ref.shape[0]
    x = x_ref[...]
    h1 = x + 0.5 * _swiglu(_rms(x, g1_ref[...]).astype(BF16), w1_ref, w3_ref, w2_ref)
    h1_ref[...] = h1
    un = _rms(h1, gm_ref[...]).astype(BF16)
    project = lambda w: jnp.dot(un, w, preferred_element_type=F32)
    zf = project(wf_ref[...])
    kv = project(wkvq_ref[:, :2 * ATTN_WIDTH])
    zft = zf.T[:ATTN_HEADS, :] + bf_ref[...]
    logf = jnp.minimum(zft, 0.0) - jnp.log1p(jnp.exp(-jnp.abs(zft)))
    lane = lax.broadcasted_iota(jnp.int32, logf.shape, 1)
    c = logf
    shift = 1
    while shift < tm:
        c = c + jnp.where(lane >= shift, pltpu.roll(c, shift, 1), 0.0)
        shift *= 2

    @pl.when(pl.program_id(1) == 0)
    def _():
        carry_ref[...] = jnp.zeros_like(carry_ref)

    c_abs = c + carry_ref[:, 0:1]
    ct_ref[...] = c_abs * LOG2E
    carry_ref[...] = jnp.broadcast_to(c_abs[:, tm - 1:tm], carry_ref.shape)

    rel = (c - c[:, 0:1]) * LOG2E
    hi = rel.astype(BF16).astype(F32)
    mid = (rel - hi).astype(BF16).astype(F32)
    lo = (rel - hi - mid).astype(BF16).astype(F32)
    kt = kv[:, :ATTN_WIDTH].astype(BF16).astype(F32).T
    vv = kv[:, ATTN_WIDTH:]
    k_sq = (kt * kt).reshape(ATTN_HEADS, HEAD_DIM, tm)
    kn_ref[...] = jnp.broadcast_to(
        jnp.max(jnp.sum(k_sq, axis=1), axis=-1, keepdims=True), kn_ref.shape)
    sub = lax.broadcasted_iota(jnp.int32, (BIAS_ROWS, tm), 0)
    zeros = jnp.zeros((HEAD_DIM - BIAS_ROWS, tm), F32)
    vlane = lax.broadcasted_iota(jnp.int32, (tm, HEAD_PAIR), 1)
    for h in range(ATTN_HEADS):
        bias = jnp.where(sub == 0, -hi[h:h + 1],
                         jnp.where(sub == 1, -mid[h:h + 1],
                                   jnp.where(sub == 2, -lo[h:h + 1], 0.0)))
        k_h = kt[h * HEAD_DIM:(h + 1) * HEAD_DIM]
        vp = vv[:, (h // 2) * HEAD_PAIR:(h // 2 + 1) * HEAD_PAIR]
        if h % 2 == 0:
            kt_ref[h] = jnp.concatenate([k_h, bias, zeros], axis=0).astype(BF16)
            v_ref[h] = jnp.where(vlane < HEAD_DIM, vp,
                                 jnp.where(vlane == HEAD_DIM, 1.0, 0.0)).astype(BF16)
        else:
            kt_ref[h] = jnp.concatenate([bias, zeros, k_h], axis=0).astype(BF16)
            v_ref[h] = jnp.where(vlane >= HEAD_DIM, vp,
                                 jnp.where(vlane == 0, 1.0, 0.0)).astype(BF16)

    q_ref[...] = project(wkvq_ref[:, 2 * ATTN_WIDTH:]).astype(BF16)
    s_ref[...] = project(ws_ref[...]).astype(BF16)


def _head_call(x, g1, w1, w3, w2, gm, wqkv, ws, wf, bf):
    B, L, D = x.shape
    tm = TOKEN_TILE
    tile = lambda w: pl.BlockSpec((None, tm, w), lambda b, i: (b, i, 0))
    out_shape = (
        jax.ShapeDtypeStruct((B, L, D), F32),
        jax.ShapeDtypeStruct((B, L, ATTN_WIDTH), BF16),
        jax.ShapeDtypeStruct((B, ATTN_HEADS, HEAD_PAIR, L), BF16),
        jax.ShapeDtypeStruct((B, ATTN_HEADS, L, HEAD_PAIR), BF16),
        jax.ShapeDtypeStruct((B, L, SSM_WIDTH), BF16),
        jax.ShapeDtypeStruct((B, ATTN_HEADS, L), F32),
        jax.ShapeDtypeStruct((B, L // tm, ATTN_HEADS, LANES), F32),
    )
    return pl.pallas_call(
        _head_kernel,
        grid=(B, L // tm),
        in_specs=[tile(D), _const_spec(g1.shape), _const_spec(w1.shape),
                  _const_spec(w3.shape), _const_spec(w2.shape), _const_spec(gm.shape),
                  _const_spec(wqkv.shape), _const_spec(ws.shape), _const_spec(wf.shape),
                  _const_spec(bf.shape)],
        out_specs=(tile(D), tile(ATTN_WIDTH),
                   pl.BlockSpec((None, ATTN_HEADS, HEAD_PAIR, tm), lambda b, i: (b, 0, 0, i)),
                   pl.BlockSpec((None, ATTN_HEADS, tm, HEAD_PAIR), lambda b, i: (b, 0, i, 0)),
                   tile(SSM_WIDTH),
                   pl.BlockSpec((None, ATTN_HEADS, tm), lambda b, i: (b, 0, i)),
                   pl.BlockSpec((None, None, ATTN_HEADS, LANES), lambda b, i: (b, i, 0, 0))),
        out_shape=out_shape,
        scratch_shapes=[pltpu.VMEM((ATTN_HEADS, LANES), F32)],
        compiler_params=pltpu.CompilerParams(
            dimension_semantics=("arbitrary", "arbitrary"),
            vmem_limit_bytes=VMEM_LIMIT),
        name="head",
    )(x, g1, w1, w3, w2, gm, wqkv, ws, wf, bf)


def _attn_kernel(q_ref, kt_ref, v_ref, c_ref, st_ref, o_ref, s_e0, s_e1, s_o0, s_o1, m_ref, acc_ref):
    t = q_ref.shape[0]
    n = pl.program_id(2)
    q = q_ref[...]
    lane = lax.broadcasted_iota(jnp.int32, (1, HEAD_PAIR), 1)
    first = lane < HEAD_DIM
    ones_even = jnp.where((lane >= HEAD_DIM) & (lane < HEAD_DIM + N_BIAS), 1.0, 0.0).astype(BF16)
    ones_odd = jnp.where(lane < N_BIAS, 1.0, 0.0).astype(BF16)
    q_heads = (jnp.where(first, q, ones_even), jnp.where(first, ones_odd, q))
    s_buf = ((s_e0, s_e1), (s_o0, s_o1))

    def start(kv):
        return pl.multiple_of(kv * t, t)

    c_q = [c_ref[h:h + 1, pl.ds(start(n), LANES)][:, 0:1] for h in range(2)]

    def qk(h, kv):
        return jnp.dot(q_heads[h], kt_ref[h, :, pl.ds(start(kv), t)], preferred_element_type=F32)

    def consume(h, s, kv):
        d = c_ref[h:h + 1, pl.ds(start(kv), LANES)][:, 0:1] - c_q[h]
        m_old = m_ref[h]
        m_new = jnp.maximum(m_old, jnp.max(s, axis=-1, keepdims=True) - d)
        p = jnp.exp2(s - (m_new + d)).astype(BF16)
        m_ref[h] = m_new
        acc_ref[h] = jnp.exp2(m_old - m_new) * acc_ref[h] + jnp.dot(
            p, v_ref[h, pl.ds(start(kv), t), :], preferred_element_type=F32)

    def step(kv, par):
        for h in range(2):
            s_buf[1 - par][h][...] = qk(h, kv - 1)
        for h in range(2):
            consume(h, s_buf[par][h][...], kv)

    m_ref[...] = jnp.full(m_ref.shape, NEG_BIG, F32)
    acc_ref[...] = jnp.zeros(acc_ref.shape, F32)
    for h in range(2):
        s_buf[1][h][...] = qk(h, n)

    q_sq = q.astype(F32) * q.astype(F32)
    reach = []
    for h in range(2):
        q_norm2 = jnp.max(jnp.sum(jnp.where(first == (h == 0), q_sq, 0.0), axis=-1, keepdims=True),
                          axis=0, keepdims=True)
        k_norm2, c_next = st_ref[h:h + 1, :], st_ref[2 + h:3 + h, :]
        reach.append(jnp.sqrt(q_norm2 * k_norm2) * BOUND_SLACK_MUL + BOUND_SLACK_ADD
                     + c_q[h] - c_next)

    @pl.when(n >= 0)
    def _():
        row = lax.broadcasted_iota(jnp.int32, (t, t), 0)
        col = lax.broadcasted_iota(jnp.int32, (t, t), 1)
        for h in range(2):
            s_buf[0][h][...] = qk(h, jnp.maximum(n - 1, 0))
        for h in range(2):
            consume(h, jnp.where(col <= row, s_buf[1][h][...], NEG_BIG), n)

    tile_id = lax.broadcasted_iota(jnp.int32, (1, LANES), 1)
    needed = tile_id < 0
    for h in range(2):
        m_low = jnp.min(m_ref[h], axis=0, keepdims=True)
        needed = needed | (reach[h] - m_low >= -SKIP_LOG2)
    n_f = n.astype(F32)
    first_needed = jnp.min(jnp.where(needed & (tile_id < n), tile_id.astype(F32), n_f))
    count = n - first_needed.astype(jnp.int32)

    def pair(i, carry):
        step(n - 1 - 2 * i, 0)
        step(n - 2 - 2 * i, 1)
        return carry

    n_pairs = jnp.maximum(count - 1, 0) // 2
    lax.fori_loop(0, n_pairs, pair, 0)
    left = count - 2 * n_pairs
    last = n - count

    @pl.when(left == 2)
    def _():
        step(last + 1, 0)
        for h in range(2):
            consume(h, s_buf[1][h][...], last)

    @pl.when(left == 1)
    def _():
        for h in range(2):
            consume(h, s_buf[0][h][...], last)

    acc0, acc1 = acc_ref[0], acc_ref[1]
    o_ref[...] = jnp.where(first, acc0 / acc0[:, HEAD_DIM:HEAD_DIM + 1], acc1 / acc1[:, 0:1])


def _attn_call(q, kt, v, ct, k_norm2):
    B, L, _ = q.shape
    t = ATTN_TILE
    n_tiles = L // t
    assert n_tiles <= LANES
    c4 = ct.reshape(B, N_PAIRS, 2, L)
    c_next = jnp.roll(ct[:, :, ::t], -1, axis=-1)
    per_tile = jnp.stack([k_norm2[..., 0].transpose(0, 2, 1), c_next], axis=1)
    per_tile = jnp.pad(per_tile, ((0, 0),) * 3 + ((0, LANES - n_tiles),))
    stats = (per_tile.reshape(B, 2, N_PAIRS, 2, LANES).transpose(0, 2, 1, 3, 4)
             .reshape(B, N_PAIRS, 4, LANES))
    pair_block = lambda *shape: pl.BlockSpec((None, 2) + shape, lambda b, p, i: (b, p, 0, 0))
    return pl.pallas_call(
        _attn_kernel,
        grid=(B, N_PAIRS, L // t),
        in_specs=[pl.BlockSpec((None, t, HEAD_PAIR), lambda b, p, i: (b, i, p)),
                  pair_block(HEAD_PAIR, L), pair_block(L, HEAD_PAIR),
                  pl.BlockSpec((None, None, 2, L), lambda b, p, i: (b, p, 0, 0)),
                  pl.BlockSpec((None, None, 4, LANES), lambda b, p, i: (b, p, 0, 0))],
        out_specs=pl.BlockSpec((None, t, HEAD_PAIR), lambda b, p, i: (b, i, p)),
        out_shape=jax.ShapeDtypeStruct((B, L, ATTN_WIDTH), F32),
        scratch_shapes=[pltpu.VMEM((t, t), F32)] * 4
                       + [pltpu.VMEM((2, t, 1), F32), pltpu.VMEM((2, t, HEAD_PAIR), F32)],
        compiler_params=pltpu.CompilerParams(
            dimension_semantics=("arbitrary", "arbitrary", "arbitrary"),
            vmem_limit_bytes=VMEM_LIMIT),
        name="attn",
    )(q, kt, v, c4, stats)


def _gelu_tanh(x):
    return 0.5 * x * (1.0 + jnp.tanh(math.sqrt(2.0 / math.pi) * (x + 0.044715 * (x * x * x))))


def _cis(mag_arg, ang):
    mag = jnp.exp(mag_arg)
    return mag * jnp.cos(ang), mag * jnp.sin(ang)


def _cmul(ar, ai, br, bi):
    return ar * br - ai * bi, ar * bi + ai * br


def _cpow2(zr, zi, n):
    assert n & (n - 1) == 0
    while n > 1:
        zr, zi = zr * zr - zi * zi, 2.0 * zr * zi
        n //= 2
    return zr, zi


def _ssm_kernel(arow_ref, acol_ref, ldt_ref, bt_ref, cab_ref, dcol_ref, e_ref, f_ref, z_ref,
                *, chunks_per_seq):
    T, P, H = SSM_CHUNK, SSM_STATE, SSM_GROUP_CH
    TH = T * H
    hi = lax.Precision.HIGHEST
    dt = jnp.exp(ldt_ref[...])

    lam_r, lam_i = dt * arow_ref[0:1, :], dt * arow_ref[1:2, :]
    j0 = lax.broadcasted_iota(jnp.int32, (T, 2 * P), 0).astype(F32)
    pa0, pb0 = _cis(j0 * lam_r, j0 * lam_i)
    pa1, pb1 = _cmul(pa0, pb0, *_cis(lam_r, lam_i))
    over_h = lambda a: jnp.concatenate(
        [jnp.broadcast_to(a[j:j + 1, :], (H, 2 * P)) for j in range(T)], axis=0)
    ca, cb = jnp.tile(cab_ref[0], (T, 1)), jnp.tile(cab_ref[1], (T, 1))
    c_pow0 = over_h(pa0) * ca + over_h(pb0) * cb
    c_pow1 = over_h(pa1) * ca + over_h(pb1) * cb

    a_r, a_i = acol_ref[:, 0:1], acol_ref[:, 1:2]
    lr, li = dt * a_r, dt * a_i
    abar_r, abar_i = _cis(lr, li)
    nr, ni = abar_r - 1.0, abar_i
    den = a_r * a_r + a_i * a_i
    fr, fi = (nr * a_r + ni * a_i) / den, (ni * a_r - nr * a_i) / den
    b_r, b_i = bt_ref[0], bt_ref[1]
    bb_r, bb_i = fr * b_r - fi * b_i, fr * b_i + fi * b_r

    kcol = jnp.dot(c_pow0, jnp.concatenate([bb_r, bb_i], axis=0), precision=hi,
                   preferred_element_type=F32)
    lane_h = lax.broadcasted_iota(jnp.int32, (H, LANES), 1) % H
    skip = jnp.where(lane_h == lax.broadcasted_iota(jnp.int32, (H, LANES), 0), dcol_ref[...], 0.0)

    @pl.when(pl.program_id(0) == 0)
    def _():
        z_ref[0:TH, :] = jnp.zeros((TH, LANES), F32)

    z_ref[TH:2 * TH, :] = kcol
    z_ref[TH:TH + H, :] = kcol[:H] + skip
    lane_group = lax.broadcasted_iota(jnp.int32, (1, LANES), 1) // H
    groups_per_block = LANES // H
    blocks = []
    for v in range(TH // LANES):
        blk = None
        for u in range(groups_per_block):
            s = v * groups_per_block + u
            piece = z_ref[TH - H * s:2 * TH - H * s, :]
            blk = piece if blk is None else jnp.where(lane_group == u, piece, blk)
        blocks.append(blk.astype(BF16))
    mt = jnp.concatenate(blocks, axis=1)

    expo = (groups_per_block - 1 - lane_group).astype(F32)
    wr, wi = _cis(lr * expo, li * expo)
    hop_r, hop_i = _cpow2(abar_r, abar_i, groups_per_block)
    w1_r, w1_i = [], []
    for v in range(TH // LANES):
        w1_r.insert(0, wr * bb_r - wi * bb_i)
        w1_i.insert(0, wr * bb_i + wi * bb_r)
        wr, wi = _cmul(wr, wi, hop_r, hop_i)
    w1t = jnp.concatenate([jnp.concatenate(w1_r, axis=1),
                           jnp.concatenate(w1_i, axis=1)], axis=0).astype(BF16)

    e = e_ref[...].reshape(TH, e_ref.shape[-1])
    y = jnp.dot(mt, e, preferred_element_type=F32)
    st = jnp.dot(w1t, e, preferred_element_type=F32)
    sr, si = st[:P], st[P:]
    pos = lax.broadcasted_iota(jnp.int32, sr.shape, 1) % chunks_per_seq

    def shifted(a, shift):
        return jnp.where(pos >= shift, pltpu.roll(a, shift, 1), 0.0)

    qr, qi = _cpow2(abar_r, abar_i, T)
    shift = 1
    while shift < chunks_per_seq:
        srs, sis = shifted(sr, shift), shifted(si, shift)
        sr, si = sr + qr * srs - qi * sis, si + qr * sis + qi * srs
        qr, qi = qr * qr - qi * qi, 2.0 * qr * qi
        shift *= 2
    x_prev = jnp.concatenate([shifted(sr, 1), shifted(si, 1)], axis=0).astype(BF16)
    y = y + jnp.dot(c_pow1.astype(BF16), x_prev, preferred_element_type=F32)
    f_ref[...] = _gelu_tanh(y).reshape(f_ref.shape)


def _ssm_call(arow, acol, ldt, bt, cab, dcol, e, chunks_per_seq):
    T, _, NC = e.shape
    G = arow.shape[0]
    grp = lambda a: pl.BlockSpec((None,) + a.shape[1:], lambda g: (g,) + (0,) * (a.ndim - 1))
    return pl.pallas_call(
        functools.partial(_ssm_kernel, chunks_per_seq=chunks_per_seq),
        grid=(G,),
        in_specs=[grp(a) for a in (arow, acol, ldt, bt, cab, dcol)]
                 + [pl.BlockSpec((T, SSM_GROUP_CH, NC), lambda g: (0, g, 0))],
        out_specs=pl.BlockSpec((T, SSM_GROUP_CH, NC), lambda g: (0, g, 0)),
        out_shape=jax.ShapeDtypeStruct((T, SSM_WIDTH, NC), F32),
        scratch_shapes=[pltpu.VMEM((2 * T * SSM_GROUP_CH, LANES), F32)],
        compiler_params=pltpu.CompilerParams(
            dimension_semantics=("arbitrary",), vmem_limit_bytes=VMEM_LIMIT),
        name="ssm",
    )(arow, acol, ldt, bt, cab, dcol, e)


def _ssm_param_layouts(a_re, a_im, log_dt, b_re, b_im, c_re, c_im, d_skip):
    G = a_re.shape[0]
    arow = jnp.stack([jnp.concatenate([a_re, a_re], -1), jnp.concatenate([a_im, a_im], -1)], 1)
    acol = jnp.stack([a_re, a_im], -1)
    reps = LANES // SSM_GROUP_CH
    bt = jnp.stack([jnp.tile(b_re, (1, 1, reps)), jnp.tile(b_im, (1, 1, reps))], 1)
    cab = jnp.stack([jnp.concatenate([c_re, -c_im], -1), jnp.concatenate([-c_im, -c_re], -1)], 1)
    return (arow.astype(F32), acol.astype(F32), log_dt.reshape(G, 1, 1).astype(F32),
            bt.astype(F32), cab.astype(F32), d_skip.reshape(G, SSM_GROUP_CH, 1).astype(F32))


def _tail_kernel(h1_ref, attn_ref, y_ref, p_ref, wglu_ref, bglu_ref, ga_ref, gs_ref,
                 woa_ref, wos_ref, g2_ref, w1_ref, w3_ref, w2_ref, gp_ref, wpg_ref,
                 wpp_ref, gf_ref, o_ref):
    y = y_ref[...]
    glu = y * _sigmoid(jnp.dot(y.astype(BF16), wglu_ref[...], preferred_element_type=F32)
                       + bglu_ref[...])
    an = _rms(attn_ref[...], ga_ref[...]).astype(BF16)
    sn = _rms(glu, gs_ref[...]).astype(BF16)
    h = (h1_ref[...] + jnp.dot(an, woa_ref[...], preferred_element_type=F32)
         + jnp.dot(sn, wos_ref[...], preferred_element_type=F32))
    h = h + 0.5 * _swiglu(_rms(h, g2_ref[...]).astype(BF16), w1_ref, w3_ref, w2_ref)
    gate = _sigmoid(jnp.dot(_rms(h, gp_ref[...]).astype(BF16), wpg_ref[...],
                            preferred_element_type=F32))
    h = h + gate * jnp.dot(p_ref[...].astype(BF16), wpp_ref[...], preferred_element_type=F32)
    o_ref[...] = _rms(h, gf_ref[...])


def _tail_call(h1, attn, y, p, *consts):
    B, L, D = h1.shape
    tm = TOKEN_TILE
    tile = lambda w: pl.BlockSpec((None, tm, w), lambda b, i: (b, i, 0))
    return pl.pallas_call(
        _tail_kernel,
        grid=(B, L // tm),
        in_specs=[tile(D), tile(ATTN_WIDTH), tile(SSM_WIDTH), tile(PLE_DIM)]
                 + [_const_spec(c.shape) for c in consts],
        out_specs=tile(D),
        out_shape=jax.ShapeDtypeStruct((B, L, D), F32),
        compiler_params=pltpu.CompilerParams(
            dimension_semantics=("arbitrary", "arbitrary"),
            vmem_limit_bytes=VMEM_LIMIT),
        name="tail",
    )(h1, attn, y, p, *consts)


def kernel(x, p, g_ffn1, w1_a, w3_a, w2_a, g_mix, w_in, b_f, a_re, a_im, log_dt, b_re, b_im, c_re, c_im, d_skip, w_glu, b_glu, g_attn_out, g_ssm_out, w_out, g_ffn2, w1_b, w3_b, w2_b, g_ple, w_ple_gate, w_ple_proj, g_final):
    B, L, D = x.shape
    assert D == D_MODEL and L % ATTN_TILE == 0 and L % TOKEN_TILE == 0 and L % SSM_CHUNK == 0
    assert g_ffn1.shape[0] == 1, "single layer"
    assert TOKEN_TILE == ATTN_TILE, "decay bias rows are relative to the kv tile start"
    row = lambda g: g.reshape(1, -1).astype(F32)
    bf = lambda w: w.astype(BF16)
    s_v, s_f = 3 * ATTN_WIDTH, 3 * ATTN_WIDTH + ATTN_HEADS
    w_in0 = w_in[0]
    scale = LOG2E / math.sqrt(HEAD_DIM)
    wqkv = jnp.concatenate([w_in0[:, ATTN_WIDTH:s_v], w_in0[:, :ATTN_WIDTH] * scale], axis=1)
    wf = jnp.pad(w_in0[:, s_v:s_f], ((0, 0), (0, LANES - ATTN_HEADS)))

    h1, q, kt, v, s_in, ct, k_norm2 = _head_call(
        x, row(g_ffn1[0]), bf(w1_a[0]), bf(w3_a[0]), bf(w2_a[0]), row(g_mix[0]),
        bf(wqkv), bf(w_in0[:, s_f:]), bf(wf), b_f[0].reshape(ATTN_HEADS, 1).astype(F32))

    attn = _attn_call(q, kt, v, ct, k_norm2)

    T = SSM_CHUNK
    chunks_per_seq = L // T
    n_chunks = B * chunks_per_seq
    e = s_in.reshape(n_chunks, T, SSM_WIDTH).transpose(1, 2, 0)
    f = _ssm_call(*_ssm_param_layouts(a_re[0], a_im[0], log_dt[0], b_re[0], b_im[0],
                                      c_re[0], c_im[0], d_skip[0]),
                  e, chunks_per_seq)
    y = f.transpose(2, 0, 1).reshape(B, L, SSM_WIDTH)

    w_out0 = w_out[0]
    return _tail_call(
        h1, attn, y, p[0],
        bf(w_glu[0]), row(b_glu[0]), row(g_attn_out[0]), row(g_ssm_out[0]),
        bf(w_out0[:ATTN_WIDTH]), bf(w_out0[ATTN_WIDTH:]), row(g_ffn2[0]),
        bf(w1_b[0]), bf(w3_b[0]), bf(w2_b[0]), row(g_ple[0]), bf(w_ple_gate[0]),
        bf(w_ple_proj[0]), row(g_final))
```

```python
import functools
import math

import jax
import jax.numpy as jnp
from jax import lax
from jax.experimental import pallas as pl
from jax.experimental.pallas import tpu as pltpu

D_MODEL = 1024
ATTN_HEADS = 8
HEAD_DIM = 64
ATTN_WIDTH = ATTN_HEADS * HEAD_DIM
SSM_WIDTH = D_MODEL - ATTN_WIDTH
SSM_GROUP_CH = 16
SSM_GROUPS = SSM_WIDTH // SSM_GROUP_CH
SSM_STATE = 64
D_FF = 2816
PLE_DIM = 256
EPS = 1e-6

LANES = 128
HEAD_PAIR = 2 * HEAD_DIM
N_PAIRS = ATTN_HEADS // 2
FF_CHUNK = 256
TOKEN_TILE = 512
ATTN_TILE = 512
SSM_CHUNK = 32
NEG_BIG = -1e30
SKIP_LOG2 = 140.0
BOUND_SLACK_MUL = 1.001
BOUND_SLACK_ADD = 1.0
LOG2E = math.log2(math.e)
N_BIAS = 3
BIAS_ROWS = 8
VMEM_LIMIT = 56 * 1024 * 1024

BF16 = jnp.bfloat16
F32 = jnp.float32


def _rms(x, g):
    ms = jnp.mean(x * x, axis=-1, keepdims=True)
    return x * lax.rsqrt(ms + EPS) * g


def _sigmoid(x):
    return 1.0 / (1.0 + jnp.exp(-x))


def _swiglu(xn, w1_ref, w3_ref, w2_ref):
    acc = None
    for c in range(D_FF // FF_CHUNK):
        sl = slice(c * FF_CHUNK, (c + 1) * FF_CHUNK)
        a = jnp.dot(xn, w1_ref[:, sl], preferred_element_type=F32)
        b = jnp.dot(xn, w3_ref[:, sl], preferred_element_type=F32)
        gated = (a * _sigmoid(a) * b).astype(BF16)
        part = jnp.dot(gated, w2_ref[sl, :], preferred_element_type=F32)
        acc = part if acc is None else acc + part
    return acc


def _const_spec(shape):
    nd = len(shape)
    return pl.BlockSpec(shape, lambda *_: (0,) * nd, pipeline_mode=pl.Buffered(1))


def _head_kernel(x_ref, g1_ref, w1_ref, w3_ref, w2_ref, gm_ref, wkvq_ref, ws_ref,
                 wf_ref, bf_ref, h1_ref, q_ref, kt_ref, v_ref, s_ref, ct_ref, kn_ref,
                 carry_ref):
    tm = x_ref.shape[0]
    x = x_ref[...]
    h1 = x + 0.5 * _swiglu(_rms(x, g1_ref[...]).astype(BF16), w1_ref, w3_ref, w2_ref)
    h1_ref[...] = h1
    un = _rms(h1, gm_ref[...]).astype(BF16)
    project = lambda w: jnp.dot(un, w, preferred_element_type=F32)
    zf = project(wf_ref[...])
    kv = project(wkvq_ref[:, :2 * ATTN_WIDTH])
    zft = zf.T[:ATTN_HEADS, :] + bf_ref[...]
    logf = jnp.minimum(zft, 0.0) - jnp.log1p(jnp.exp(-jnp.abs(zft)))
    lane = lax.broadcasted_iota(jnp.int32, logf.shape, 1)
    c = logf
    shift = 1
    while shift < tm:
        c = c + jnp.where(lane >= shift, pltpu.roll(c, shift, 1), 0.0)
        shift *= 2

    @pl.when(pl.program_id(1) == 0)
    def _():
        carry_ref[...] = jnp.zeros_like(carry_ref)

    c_abs = c + carry_ref[:, 0:1]
    ct_ref[...] = c_abs * LOG2E
    carry_ref[...] = jnp.broadcast_to(c_abs[:, tm - 1:tm], carry_ref.shape)

    rel = (c - c[:, 0:1]) * LOG2E
    hi = rel.astype(BF16).astype(F32)
    mid = (rel - hi).astype(BF16).astype(F32)
    lo = (rel - hi - mid).astype(BF16).astype(F32)
    kt = kv[:, :ATTN_WIDTH].astype(BF16).astype(F32).T
    vv = kv[:, ATTN_WIDTH:]
    k_sq = (kt * kt).reshape(ATTN_HEADS, HEAD_DIM, tm)
    kn_ref[...] = jnp.broadcast_to(
        jnp.max(jnp.sum(k_sq, axis=1), axis=-1, keepdims=True), kn_ref.shape)
    sub = lax.broadcasted_iota(jnp.int32, (BIAS_ROWS, tm), 0)
    zeros = jnp.zeros((HEAD_DIM - BIAS_ROWS, tm), F32)
    vlane = lax.broadcasted_iota(jnp.int32, (tm, HEAD_PAIR), 1)
    for h in range(ATTN_HEADS):
        bias = jnp.where(sub == 0, -hi[h:h + 1],
                         jnp.where(sub == 1, -mid[h:h + 1],
                                   jnp.where(sub == 2, -lo[h:h + 1], 0.0)))
        k_h = kt[h * HEAD_DIM:(h + 1) * HEAD_DIM]
        vp = vv[:, (h // 2) * HEAD_PAIR:(h // 2 + 1) * HEAD_PAIR]
        if h % 2 == 0:
            kt_ref[h] = jnp.concatenate([k_h, bias, zeros], axis=0).astype(BF16)
            v_ref[h] = jnp.where(vlane < HEAD_DIM, vp,
                                 jnp.where(vlane == HEAD_DIM, 1.0, 0.0)).astype(BF16)
        else:
            kt_ref[h] = jnp.concatenate([bias, zeros, k_h], axis=0).astype(BF16)
            v_ref[h] = jnp.where(vlane >= HEAD_DIM, vp,
                                 jnp.where(vlane == 0, 1.0, 0.0)).astype(BF16)

    q_ref[...] = project(wkvq_ref[:, 2 * ATTN_WIDTH:]).astype(BF16)
    s_ref[...] = project(ws_ref[...]).astype(BF16)


def _head_call(x, g1, w1, w3, w2, gm, wqkv, ws, wf, bf):
    B, L, D = x.shape
    tm = TOKEN_TILE
    tile = lambda w: pl.BlockSpec((None, tm, w), lambda b, i: (b, i, 0))
    out_shape = (
        jax.ShapeDtypeStruct((B, L, D), F32),
        jax.ShapeDtypeStruct((B, L, ATTN_WIDTH), BF16),
        jax.ShapeDtypeStruct((B, ATTN_HEADS, HEAD_PAIR, L), BF16),
        jax.ShapeDtypeStruct((B, ATTN_HEADS, L, HEAD_PAIR), BF16),
        jax.ShapeDtypeStruct((B, L, SSM_WIDTH), BF16),
        jax.ShapeDtypeStruct((B, ATTN_HEADS, L), F32),
        jax.ShapeDtypeStruct((B, L // tm, ATTN_HEADS, LANES), F32),
    )
    return pl.pallas_call(
        _head_kernel,
        grid=(B, L // tm),
        in_specs=[tile(D), _const_spec(g1.shape), _const_spec(w1.shape),
                  _const_spec(w3.shape), _const_spec(w2.shape), _const_spec(gm.shape),
                  _const_spec(wqkv.shape), _const_spec(ws.shape), _const_spec(wf.shape),
                  _const_spec(bf.shape)],
        out_specs=(tile(D), tile(ATTN_WIDTH),
                   pl.BlockSpec((None, ATTN_HEADS, HEAD_PAIR, tm), lambda b, i: (b, 0, 0, i)),
                   pl.BlockSpec((None, ATTN_HEADS, tm, HEAD_PAIR), lambda b, i: (b, 0, i, 0)),
                   tile(SSM_WIDTH),
                   pl.BlockSpec((None, ATTN_HEADS, tm), lambda b, i: (b, 0, i)),
                   pl.BlockSpec((None, None, ATTN_HEADS, LANES), lambda b, i: (b, i, 0, 0))),
        out_shape=out_shape,
        scratch_shapes=[pltpu.VMEM((ATTN_HEADS, LANES), F32)],
        compiler_params=pltpu.CompilerParams(
            dimension_semantics=("arbitrary", "arbitrary"),
            vmem_limit_bytes=VMEM_LIMIT),
        name="head",
    )(x, g1, w1, w3, w2, gm, wqkv, ws, wf, bf)


def _attn_kernel(q_ref, *refs):
    def q_tile(n, carry):
        _attn_q_tile(n, q_ref, *refs)
        return carry

    lax.fori_loop(0, q_ref.shape[0] // ATTN_TILE, q_tile, 0)


def _attn_q_tile(n, q_ref, kt_ref, v_ref, c_ref, st_ref, o_ref, s_e0, s_e1, s_o0, s_o1, m_ref,
                 acc_ref):
    t = ATTN_TILE
    q = q_ref[pl.ds(pl.multiple_of(n * t, t), t), :]
    lane = lax.broadcasted_iota(jnp.int32, (1, HEAD_PAIR), 1)
    first = lane < HEAD_DIM
    ones_even = jnp.where((lane >= HEAD_DIM) & (lane < HEAD_DIM + N_BIAS), 1.0, 0.0).astype(BF16)
    ones_odd = jnp.where(lane < N_BIAS, 1.0, 0.0).astype(BF16)
    q_heads = (jnp.where(first, q, ones_even), jnp.where(first, ones_odd, q))
    s_buf = ((s_e0, s_e1), (s_o0, s_o1))

    def start(kv):
        return pl.multiple_of(kv * t, t)

    c_q = [c_ref[h:h + 1, pl.ds(start(n), LANES)][:, 0:1] for h in range(2)]

    def qk(h, kv):
        return jnp.dot(q_heads[h], kt_ref[h, :, pl.ds(start(kv), t)], preferred_element_type=F32)

    def consume(h, s, kv):
        d = c_ref[h:h + 1, pl.ds(start(kv), LANES)][:, 0:1] - c_q[h]
        m_old = m_ref[h]
        m_new = jnp.maximum(m_old, jnp.max(s, axis=-1, keepdims=True) - d)
        p = jnp.exp2(s - (m_new + d)).astype(BF16)
        m_ref[h] = m_new
        acc_ref[h] = jnp.exp2(m_old - m_new) * acc_ref[h] + jnp.dot(
            p, v_ref[h, pl.ds(start(kv), t), :], preferred_element_type=F32)

    def step(kv, par):
        for h in range(2):
            s_buf[1 - par][h][...] = qk(h, kv - 1)
        for h in range(2):
            consume(h, s_buf[par][h][...], kv)

    m_ref[...] = jnp.full(m_ref.shape, NEG_BIG, F32)
    acc_ref[...] = jnp.zeros(acc_ref.shape, F32)
    for h in range(2):
        s_buf[1][h][...] = qk(h, n)

    q_sq = q.astype(F32) * q.astype(F32)
    reach = []
    for h in range(2):
        q_norm2 = jnp.max(jnp.sum(jnp.where(first == (h == 0), q_sq, 0.0), axis=-1, keepdims=True),
                          axis=0, keepdims=True)
        k_norm2, c_next = st_ref[h:h + 1, :], st_ref[2 + h:3 + h, :]
        reach.append(jnp.sqrt(q_norm2 * k_norm2) * BOUND_SLACK_MUL + BOUND_SLACK_ADD
                     + c_q[h] - c_next)

    @pl.when(n >= 0)
    def _():
        row = lax.broadcasted_iota(jnp.int32, (t, t), 0)
        col = lax.broadcasted_iota(jnp.int32, (t, t), 1)
        for h in range(2):
            s_buf[0][h][...] = qk(h, jnp.maximum(n - 1, 0))
        for h in range(2):
            consume(h, jnp.where(col <= row, s_buf[1][h][...], NEG_BIG), n)

    tile_id = lax.broadcasted_iota(jnp.int32, (1, LANES), 1)
    needed = tile_id < 0
    for h in range(2):
        m_low = jnp.min(m_ref[h], axis=0, keepdims=True)
        needed = needed | (reach[h] - m_low >= -SKIP_LOG2)
    n_f = n.astype(F32)
    first_needed = jnp.min(jnp.where(needed & (tile_id < n), tile_id.astype(F32), n_f))
    count = n - first_needed.astype(jnp.int32)

    def pair(i, carry):
        step(n - 1 - 2 * i, 0)
        step(n - 2 - 2 * i, 1)
        return carry

    n_pairs = jnp.maximum(count - 1, 0) // 2
    lax.fori_loop(0, n_pairs, pair, 0)
    left = count - 2 * n_pairs
    last = n - count

    @pl.when(left == 2)
    def _():
        step(last + 1, 0)
        for h in range(2):
            consume(h, s_buf[1][h][...], last)

    @pl.when(left == 1)
    def _():
        for h in range(2):
            consume(h, s_buf[0][h][...], last)

    acc0, acc1 = acc_ref[0], acc_ref[1]
    o_ref[pl.ds(start(n), t), :] = jnp.where(first, acc0 / acc0[:, HEAD_DIM:HEAD_DIM + 1],
                                             acc1 / acc1[:, 0:1])


def _attn_call(q, kt, v, ct, k_norm2):
    B, L, _ = q.shape
    t = ATTN_TILE
    n_tiles = L // t
    assert n_tiles <= LANES
    c4 = ct.reshape(B, N_PAIRS, 2, L)
    c_next = jnp.roll(ct[:, :, ::t], -1, axis=-1)
    per_tile = jnp.stack([k_norm2[..., 0].transpose(0, 2, 1), c_next], axis=1)
    per_tile = jnp.pad(per_tile, ((0, 0),) * 3 + ((0, LANES - n_tiles),))
    stats = (per_tile.reshape(B, 2, N_PAIRS, 2, LANES).transpose(0, 2, 1, 3, 4)
             .reshape(B, N_PAIRS, 4, LANES))
    pair_block = lambda *shape: pl.BlockSpec((None, 2) + shape, lambda b, p: (b, p, 0, 0))
    lanes_of_pair = pl.BlockSpec((None, L, HEAD_PAIR), lambda b, p: (b, 0, p))
    return pl.pallas_call(
        _attn_kernel,
        grid=(B, N_PAIRS),
        in_specs=[lanes_of_pair, pair_block(HEAD_PAIR, L), pair_block(L, HEAD_PAIR),
                  pl.BlockSpec((None, None, 2, L), lambda b, p: (b, p, 0, 0)),
                  pl.BlockSpec((None, None, 4, LANES), lambda b, p: (b, p, 0, 0))],
        out_specs=lanes_of_pair,
        out_shape=jax.ShapeDtypeStruct((B, L, ATTN_WIDTH), F32),
        scratch_shapes=[pltpu.VMEM((t, t), F32)] * 4
                       + [pltpu.VMEM((2, t, 1), F32), pltpu.VMEM((2, t, HEAD_PAIR), F32)],
        compiler_params=pltpu.CompilerParams(
            dimension_semantics=("arbitrary", "arbitrary"),
            vmem_limit_bytes=VMEM_LIMIT),
        name="attn",
    )(q, kt, v, c4, stats)


def _gelu_tanh(x):
    return 0.5 * x * (1.0 + jnp.tanh(math.sqrt(2.0 / math.pi) * (x + 0.044715 * (x * x * x))))


def _cis(mag_arg, ang):
    mag = jnp.exp(mag_arg)
    return mag * jnp.cos(ang), mag * jnp.sin(ang)


def _cmul(ar, ai, br, bi):
    return ar * br - ai * bi, ar * bi + ai * br


def _cpow2(zr, zi, n):
    assert n & (n - 1) == 0
    while n > 1:
        zr, zi = zr * zr - zi * zi, 2.0 * zr * zi
        n //= 2
    return zr, zi


def _ssm_kernel(arow_ref, acol_ref, ldt_ref, bt_ref, cab_ref, dcol_ref, e_ref, f_ref, z_ref,
                *, chunks_per_seq):
    T, P, H = SSM_CHUNK, SSM_STATE, SSM_GROUP_CH
    TH = T * H
    hi = lax.Precision.HIGHEST
    dt = jnp.exp(ldt_ref[...])

    lam_r, lam_i = dt * arow_ref[0:1, :], dt * arow_ref[1:2, :]
    j0 = lax.broadcasted_iota(jnp.int32, (T, 2 * P), 0).astype(F32)
    pa0, pb0 = _cis(j0 * lam_r, j0 * lam_i)
    pa1, pb1 = _cmul(pa0, pb0, *_cis(lam_r, lam_i))
    over_h = lambda a: jnp.concatenate(
        [jnp.broadcast_to(a[j:j + 1, :], (H, 2 * P)) for j in range(T)], axis=0)
    ca, cb = jnp.tile(cab_ref[0], (T, 1)), jnp.tile(cab_ref[1], (T, 1))
    c_pow0 = over_h(pa0) * ca + over_h(pb0) * cb
    c_pow1 = over_h(pa1) * ca + over_h(pb1) * cb

    a_r, a_i = acol_ref[:, 0:1], acol_ref[:, 1:2]
    lr, li = dt * a_r, dt * a_i
    abar_r, abar_i = _cis(lr, li)
    nr, ni = abar_r - 1.0, abar_i
    den = a_r * a_r + a_i * a_i
    fr, fi = (nr * a_r + ni * a_i) / den, (ni * a_r - nr * a_i) / den
    b_r, b_i = bt_ref[0], bt_ref[1]
    bb_r, bb_i = fr * b_r - fi * b_i, fr * b_i + fi * b_r

    kcol = jnp.dot(c_pow0, jnp.concatenate([bb_r, bb_i], axis=0), precision=hi,
                   preferred_element_type=F32)
    lane_h = lax.broadcasted_iota(jnp.int32, (H, LANES), 1) % H
    skip = jnp.where(lane_h == lax.broadcasted_iota(jnp.int32, (H, LANES), 0), dcol_ref[...], 0.0)

    @pl.when(pl.program_id(0) == 0)
    def _():
        z_ref[0:TH, :] = jnp.zeros((TH, LANES), F32)

    z_ref[TH:2 * TH, :] = kcol
    z_ref[TH:TH + H, :] = kcol[:H] + skip
    lane_group = lax.broadcasted_iota(jnp.int32, (1, LANES), 1) // H
    groups_per_block = LANES // H
    blocks = []
    for v in range(TH // LANES):
        blk = None
        for u in range(groups_per_block):
            s = v * groups_per_block + u
            piece = z_ref[TH - H * s:2 * TH - H * s, :]
            blk = piece if blk is None else jnp.where(lane_group == u, piece, blk)
        blocks.append(blk.astype(BF16))
    mt = jnp.concatenate(blocks, axis=1)

    expo = (groups_per_block - 1 - lane_group).astype(F32)
    wr, wi = _cis(lr * expo, li * expo)
    hop_r, hop_i = _cpow2(abar_r, abar_i, groups_per_block)
    w1_r, w1_i = [], []
    for v in range(TH // LANES):
        w1_r.insert(0, wr * bb_r - wi * bb_i)
        w1_i.insert(0, wr * bb_i + wi * bb_r)
        wr, wi = _cmul(wr, wi, hop_r, hop_i)
    w1t = jnp.concatenate([jnp.concatenate(w1_r, axis=1),
                           jnp.concatenate(w1_i, axis=1)], axis=0).astype(BF16)

    e = e_ref[...].reshape(TH, e_ref.shape[-1])
    y = jnp.dot(mt, e, preferred_element_type=F32)
    st = jnp.dot(w1t, e, preferred_element_type=F32)
    sr, si = st[:P], st[P:]
    pos = lax.broadcasted_iota(jnp.int32, sr.shape, 1) % chunks_per_seq

    def shifted(a, shift):
        return jnp.where(pos >= shift, pltpu.roll(a, shift, 1), 0.0)

    qr, qi = _cpow2(abar_r, abar_i, T)
    shift = 1
    while shift < chunks_per_seq:
        srs, sis = shifted(sr, shift), shifted(si, shift)
        sr, si = sr + qr * srs - qi * sis, si + qr * sis + qi * srs
        qr, qi = qr * qr - qi * qi, 2.0 * qr * qi
        shift *= 2
    x_prev = jnp.concatenate([shifted(sr, 1), shifted(si, 1)], axis=0).astype(BF16)
    y = y + jnp.dot(c_pow1.astype(BF16), x_prev, preferred_element_type=F32)
    f_ref[...] = _gelu_tanh(y).reshape(f_ref.shape)


def _ssm_call(arow, acol, ldt, bt, cab, dcol, e, chunks_per_seq):
    T, _, NC = e.shape
    G = arow.shape[0]
    grp = lambda a: pl.BlockSpec((None,) + a.shape[1:], lambda g: (g,) + (0,) * (a.ndim - 1))
    return pl.pallas_call(
        functools.partial(_ssm_kernel, chunks_per_seq=chunks_per_seq),
        grid=(G,),
        in_specs=[grp(a) for a in (arow, acol, ldt, bt, cab, dcol)]
                 + [pl.BlockSpec((T, SSM_GROUP_CH, NC), lambda g: (0, g, 0))],
        out_specs=pl.BlockSpec((T, SSM_GROUP_CH, NC), lambda g: (0, g, 0)),
        out_shape=jax.ShapeDtypeStruct((T, SSM_WIDTH, NC), F32),
        scratch_shapes=[pltpu.VMEM((2 * T * SSM_GROUP_CH, LANES), F32)],
        compiler_params=pltpu.CompilerParams(
            dimension_semantics=("arbitrary",), vmem_limit_bytes=VMEM_LIMIT),
        name="ssm",
    )(arow, acol, ldt, bt, cab, dcol, e)


def _ssm_param_layouts(a_re, a_im, log_dt, b_re, b_im, c_re, c_im, d_skip):
    G = a_re.shape[0]
    arow = jnp.stack([jnp.concatenate([a_re, a_re], -1), jnp.concatenate([a_im, a_im], -1)], 1)
    acol = jnp.stack([a_re, a_im], -1)
    reps = LANES // SSM_GROUP_CH
    bt = jnp.stack([jnp.tile(b_re, (1, 1, reps)), jnp.tile(b_im, (1, 1, reps))], 1)
    cab = jnp.stack([jnp.concatenate([c_re, -c_im], -1), jnp.concatenate([-c_im, -c_re], -1)], 1)
    return (arow.astype(F32), acol.astype(F32), log_dt.reshape(G, 1, 1).astype(F32),
            bt.astype(F32), cab.astype(F32), d_skip.reshape(G, SSM_GROUP_CH, 1).astype(F32))


def _tail_kernel(h1_ref, attn_ref, y_ref, p_ref, wglu_ref, bglu_ref, ga_ref, gs_ref,
                 woa_ref, wos_ref, g2_ref, w1_ref, w3_ref, w2_ref, gp_ref, wpg_ref,
                 wpp_ref, gf_ref, o_ref):
    y = y_ref[...]
    glu = y * _sigmoid(jnp.dot(y.astype(BF16), wglu_ref[...], preferred_element_type=F32)
                       + bglu_ref[...])
    an = _rms(attn_ref[...], ga_ref[...]).astype(BF16)
    sn = _rms(glu, gs_ref[...]).astype(BF16)
    h = (h1_ref[...] + jnp.dot(an, woa_ref[...], preferred_element_type=F32)
         + jnp.dot(sn, wos_ref[...], preferred_element_type=F32))
    h = h + 0.5 * _swiglu(_rms(h, g2_ref[...]).astype(BF16), w1_ref, w3_ref, w2_ref)
    gate = _sigmoid(jnp.dot(_rms(h, gp_ref[...]).astype(BF16), wpg_ref[...],
                            preferred_element_type=F32))
    h = h + gate * jnp.dot(p_ref[...].astype(BF16), wpp_ref[...], preferred_element_type=F32)
    o_ref[...] = _rms(h, gf_ref[...])


def _tail_call(h1, attn, y, p, *consts):
    B, L, D = h1.shape
    tm = TOKEN_TILE
    tile = lambda w: pl.BlockSpec((None, tm, w), lambda b, i: (b, i, 0))
    return pl.pallas_call(
        _tail_kernel,
        grid=(B, L // tm),
        in_specs=[tile(D), tile(ATTN_WIDTH), tile(SSM_WIDTH), tile(PLE_DIM)]
                 + [_const_spec(c.shape) for c in consts],
        out_specs=tile(D),
        out_shape=jax.ShapeDtypeStruct((B, L, D), F32),
        compiler_params=pltpu.CompilerParams(
            dimension_semantics=("arbitrary", "arbitrary"),
            vmem_limit_bytes=VMEM_LIMIT),
        name="tail",
    )(h1, attn, y, p, *consts)


def kernel(x, p, g_ffn1, w1_a, w3_a, w2_a, g_mix, w_in, b_f, a_re, a_im, log_dt, b_re, b_im, c_re, c_im, d_skip, w_glu, b_glu, g_attn_out, g_ssm_out, w_out, g_ffn2, w1_b, w3_b, w2_b, g_ple, w_ple_gate, w_ple_proj, g_final):
    B, L, D = x.shape
    assert D == D_MODEL and L % ATTN_TILE == 0 and L % TOKEN_TILE == 0 and L % SSM_CHUNK == 0
    assert g_ffn1.shape[0] == 1, "single layer"
    assert TOKEN_TILE == ATTN_TILE, "decay bias rows are relative to the kv tile start"
    row = lambda g: g.reshape(1, -1).astype(F32)
    bf = lambda w: w.astype(BF16)
    s_v, s_f = 3 * ATTN_WIDTH, 3 * ATTN_WIDTH + ATTN_HEADS
    w_in0 = w_in[0]
    scale = LOG2E / math.sqrt(HEAD_DIM)
    wqkv = jnp.concatenate([w_in0[:, ATTN_WIDTH:s_v], w_in0[:, :ATTN_WIDTH] * scale], axis=1)
    wf = jnp.pad(w_in0[:, s_v:s_f], ((0, 0), (0, LANES - ATTN_HEADS)))

    h1, q, kt, v, s_in, ct, k_norm2 = _head_call(
        x, row(g_ffn1[0]), bf(w1_a[0]), bf(w3_a[0]), bf(w2_a[0]), row(g_mix[0]),
        bf(wqkv), bf(w_in0[:, s_f:]), bf(wf), b_f[0].reshape(ATTN_HEADS, 1).astype(F32))

    attn = _attn_call(q, kt, v, ct, k_norm2)

    T = SSM_CHUNK
    chunks_per_seq = L // T
    n_chunks = B * chunks_per_seq
    e = s_in.reshape(n_chunks, T, SSM_WIDTH).transpose(1, 2, 0)
    f = _ssm_call(*_ssm_param_layouts(a_re[0], a_im[0], log_dt[0], b_re[0], b_im[0],
                                      c_re[0], c_im[0], d_skip[0]),
                  e, chunks_per_seq)
    y = f.transpose(2, 0, 1).reshape(B, L, SSM_WIDTH)

    w_out0 = w_out[0]
    return _tail_call(
        h1, attn, y, p[0],
        bf(w_glu[0]), row(b_glu[0]), row(g_attn_out[0]), row(g_ssm_out[0]),
        bf(w_out0[:ATTN_WIDTH]), bf(w_out0[ATTN_WIDTH:]), row(g_ffn2[0]),
        bf(w1_b[0]), bf(w3_b[0]), bf(w2_b[0]), row(g_ple[0]), bf(w_ple_gate[0]),
        bf(w_ple_proj[0]), row(g_final))
```

```python
import functools
import math

import jax
import jax.numpy as jnp
from jax import lax
from jax.experimental import pallas as pl
from jax.experimental.pallas import tpu as pltpu

D_MODEL = 1024
ATTN_HEADS = 8
HEAD_DIM = 64
ATTN_WIDTH = ATTN_HEADS * HEAD_DIM
SSM_WIDTH = D_MODEL - ATTN_WIDTH
SSM_GROUP_CH = 16
SSM_GROUPS = SSM_WIDTH // SSM_GROUP_CH
SSM_STATE = 64
D_FF = 2816
PLE_DIM = 256
EPS = 1e-6

LANES = 128
HEAD_PAIR = 2 * HEAD_DIM
N_PAIRS = ATTN_HEADS // 2
FF_CHUNK = 256
TOKEN_TILE = 512
ATTN_TILE = 512
SSM_CHUNK = 32
NEG_BIG = -1e30
SKIP_LOG2 = 140.0
BOUND_SLACK_MUL = 1.001
BOUND_SLACK_ADD = 1.0
LOG2E = math.log2(math.e)
N_BIAS = 3
BIAS_ROWS = 8
VMEM_LIMIT = 56 * 1024 * 1024

BF16 = jnp.bfloat16
F32 = jnp.float32


def _rms(x, g):
    ms = jnp.mean(x * x, axis=-1, keepdims=True)
    return x * lax.rsqrt(ms + EPS) * g


def _sigmoid(x):
    return 1.0 / (1.0 + jnp.exp(-x))


def _swiglu(xn, w1_ref, w3_ref, w2_ref):
    acc = None
    for c in range(D_FF // FF_CHUNK):
        sl = slice(c * FF_CHUNK, (c + 1) * FF_CHUNK)
        a = jnp.dot(xn, w1_ref[:, sl], preferred_element_type=F32)
        b = jnp.dot(xn, w3_ref[:, sl], preferred_element_type=F32)
        gated = (a * _sigmoid(a) * b).astype(BF16)
        part = jnp.dot(gated, w2_ref[sl, :], preferred_element_type=F32)
        acc = part if acc is None else acc + part
    return acc


def _const_spec(shape):
    nd = len(shape)
    return pl.BlockSpec(shape, lambda *_: (0,) * nd, pipeline_mode=pl.Buffered(1))


def _head_kernel(x_ref, g1_ref, w1_ref, w3_ref, w2_ref, gm_ref, wkvq_ref, ws_ref,
                 wf_ref, bf_ref, h1_ref, q_ref, kt_ref, v_ref, s_ref, ct_ref, kn_ref,
                 carry_ref):
    tm = x_ref.shape[0]
    x = x_ref[...]
    h1 = x + 0.5 * _swiglu(_rms(x, g1_ref[...]).astype(BF16), w1_ref, w3_ref, w2_ref)
    h1_ref[...] = h1
    un = _rms(h1, gm_ref[...]).astype(BF16)
    project = lambda w: jnp.dot(un, w, preferred_element_type=F32)
    zf = project(wf_ref[...])
    kv = project(wkvq_ref[:, :2 * ATTN_WIDTH])
    zft = zf.T[:ATTN_HEADS, :] + bf_ref[...]
    logf = jnp.minimum(zft, 0.0) - jnp.log1p(jnp.exp(-jnp.abs(zft)))
    lane = lax.broadcasted_iota(jnp.int32, logf.shape, 1)
    c = logf
    shift = 1
    while shift < tm:
        c = c + jnp.where(lane >= shift, pltpu.roll(c, shift, 1), 0.0)
        shift *= 2

    @pl.when(pl.program_id(1) == 0)
    def _():
        carry_ref[...] = jnp.zeros_like(carry_ref)

    c_abs = c + carry_ref[:, 0:1]
    ct_ref[...] = c_abs * LOG2E
    carry_ref[...] = jnp.broadcast_to(c_abs[:, tm - 1:tm], carry_ref.shape)

    rel = (c - c[:, 0:1]) * LOG2E
    hi = rel.astype(BF16).astype(F32)
    mid = (rel - hi).astype(BF16).astype(F32)
    lo = (rel - hi - mid).astype(BF16).astype(F32)
    kt = kv[:, :ATTN_WIDTH].astype(BF16).astype(F32).T
    vv = kv[:, ATTN_WIDTH:]
    k_sq = (kt * kt).reshape(ATTN_HEADS, HEAD_DIM, tm)
    kn_ref[...] = jnp.broadcast_to(
        jnp.max(jnp.sum(k_sq, axis=1), axis=-1, keepdims=True), kn_ref.shape)
    sub = lax.broadcasted_iota(jnp.int32, (BIAS_ROWS, tm), 0)
    zeros = jnp.zeros((HEAD_DIM - BIAS_ROWS, tm), F32)
    vlane = lax.broadcasted_iota(jnp.int32, (tm, HEAD_PAIR), 1)
    for h in range(ATTN_HEADS):
        bias = jnp.where(sub == 0, -hi[h:h + 1],
                         jnp.where(sub == 1, -mid[h:h + 1],
                                   jnp.where(sub == 2, -lo[h:h + 1], 0.0)))
        k_h = kt[h * HEAD_DIM:(h + 1) * HEAD_DIM]
        vp = vv[:, (h // 2) * HEAD_PAIR:(h // 2 + 1) * HEAD_PAIR]
        if h % 2 == 0:
            kt_ref[h] = jnp.concatenate([k_h, bias, zeros], axis=0).astype(BF16)
            v_ref[h] = jnp.where(vlane < HEAD_DIM, vp,
                                 jnp.where(vlane == HEAD_DIM, 1.0, 0.0)).astype(BF16)
        else:
            kt_ref[h] = jnp.concatenate([bias, zeros, k_h], axis=0).astype(BF16)
            v_ref[h] = jnp.where(vlane >= HEAD_DIM, vp,
                                 jnp.where(vlane == 0, 1.0, 0.0)).astype(BF16)

    q_ref[...] = project(wkvq_ref[:, 2 * ATTN_WIDTH:]).astype(BF16)
    s_ref[...] = project(ws_ref[...]).astype(BF16)


def _head_call(x, g1, w1, w3, w2, gm, wqkv, ws, wf, bf):
    B, L, D = x.shape
    tm = TOKEN_TILE
    tile = lambda w: pl.BlockSpec((None, tm, w), lambda b, i: (b, i, 0))
    out_shape = (
        jax.ShapeDtypeStruct((B, L, D), F32),
        jax.ShapeDtypeStruct((B, L, ATTN_WIDTH), BF16),
        jax.ShapeDtypeStruct((B, ATTN_HEADS, HEAD_PAIR, L), BF16),
        jax.ShapeDtypeStruct((B, ATTN_HEADS, L, HEAD_PAIR), BF16),
        jax.ShapeDtypeStruct((B, L, SSM_WIDTH), BF16),
        jax.ShapeDtypeStruct((B, ATTN_HEADS, L), F32),
        jax.ShapeDtypeStruct((B, L // tm, ATTN_HEADS, LANES), F32),
    )
    return pl.pallas_call(
        _head_kernel,
        grid=(B, L // tm),
        in_specs=[tile(D), _const_spec(g1.shape), _const_spec(w1.shape),
                  _const_spec(w3.shape), _const_spec(w2.shape), _const_spec(gm.shape),
                  _const_spec(wqkv.shape), _const_spec(ws.shape), _const_spec(wf.shape),
                  _const_spec(bf.shape)],
        out_specs=(tile(D), tile(ATTN_WIDTH),
                   pl.BlockSpec((None, ATTN_HEADS, HEAD_PAIR, tm), lambda b, i: (b, 0, 0, i)),
                   pl.BlockSpec((None, ATTN_HEADS, tm, HEAD_PAIR), lambda b, i: (b, 0, i, 0)),
                   tile(SSM_WIDTH),
                   pl.BlockSpec((None, ATTN_HEADS, tm), lambda b, i: (b, 0, i)),
                   pl.BlockSpec((None, None, ATTN_HEADS, LANES), lambda b, i: (b, i, 0, 0))),
        out_shape=out_shape,
        scratch_shapes=[pltpu.VMEM((ATTN_HEADS, LANES), F32)],
        compiler_params=pltpu.CompilerParams(
            dimension_semantics=("arbitrary", "arbitrary"),
            vmem_limit_bytes=VMEM_LIMIT),
        name="head",
    )(x, g1, w1, w3, w2, gm, wqkv, ws, wf, bf)


def _attn_kernel(q_ref, *refs):
    _attn_prepare(0, q_ref, *refs)

    def q_tile(n, carry):
        _attn_q_tile(n, q_ref, *refs)
        return carry

    lax.fori_loop(0, q_ref.shape[0] // ATTN_TILE, q_tile, 0)


def _tile_start(i):
    return pl.multiple_of(i * ATTN_TILE, ATTN_TILE)


def _q_heads(q):
    lane = lax.broadcasted_iota(jnp.int32, (1, HEAD_PAIR), 1)
    first = lane < HEAD_DIM
    ones_even = jnp.where((lane >= HEAD_DIM) & (lane < HEAD_DIM + N_BIAS), 1.0, 0.0).astype(BF16)
    ones_odd = jnp.where(lane < N_BIAS, 1.0, 0.0).astype(BF16)
    return first, (jnp.where(first, q, ones_even), jnp.where(first, ones_odd, q))


def _attn_prepare(n, q_ref, kt_ref, v_ref, c_ref, st_ref, o_ref, s_e0, s_e1, s_o0, s_o1, m_ref,
                  acc_ref, reach_ref):
    t = ATTN_TILE
    q = q_ref[pl.ds(_tile_start(n), t), :]
    first, q_heads = _q_heads(q)
    q_sq = q.astype(F32) * q.astype(F32)
    for h, s_ref in enumerate((s_o0, s_o1)):
        s_ref[...] = jnp.dot(q_heads[h], kt_ref[h, :, pl.ds(_tile_start(n), t)],
                             preferred_element_type=F32)
        q_norm2 = jnp.max(jnp.sum(jnp.where(first == (h == 0), q_sq, 0.0), axis=-1, keepdims=True),
                          axis=0, keepdims=True)
        k_norm2, c_next = st_ref[h:h + 1, :], st_ref[2 + h:3 + h, :]
        c_q = c_ref[h:h + 1, pl.ds(_tile_start(n), LANES)][:, 0:1]
        reach_ref[h:h + 1, :] = (jnp.sqrt(q_norm2 * k_norm2) * BOUND_SLACK_MUL + BOUND_SLACK_ADD
                                 + c_q - c_next)


def _attn_q_tile(n, q_ref, kt_ref, v_ref, c_ref, st_ref, o_ref, s_e0, s_e1, s_o0, s_o1, m_ref,
                 acc_ref, reach_ref):
    t = ATTN_TILE
    start = _tile_start
    first, q_heads = _q_heads(q_ref[pl.ds(start(n), t), :])
    s_buf = ((s_e0, s_e1), (s_o0, s_o1))
    c_q = [c_ref[h:h + 1, pl.ds(start(n), LANES)][:, 0:1] for h in range(2)]

    def qk(h, kv):
        return jnp.dot(q_heads[h], kt_ref[h, :, pl.ds(start(kv), t)], preferred_element_type=F32)

    def consume(h, s, kv):
        d = c_ref[h:h + 1, pl.ds(start(kv), LANES)][:, 0:1] - c_q[h]
        m_old = m_ref[h]
        m_new = jnp.maximum(m_old, jnp.max(s, axis=-1, keepdims=True) - d)
        p = jnp.exp2(s - (m_new + d)).astype(BF16)
        m_ref[h] = m_new
        acc_ref[h] = jnp.exp2(m_old - m_new) * acc_ref[h] + jnp.dot(
            p, v_ref[h, pl.ds(start(kv), t), :], preferred_element_type=F32)

    def step(kv, par):
        for h in range(2):
            s_buf[1 - par][h][...] = qk(h, kv - 1)
        for h in range(2):
            consume(h, s_buf[par][h][...], kv)

    m_ref[...] = jnp.full(m_ref.shape, NEG_BIG, F32)
    acc_ref[...] = jnp.zeros(acc_ref.shape, F32)

    @pl.when(n >= 0)
    def _():
        row = lax.broadcasted_iota(jnp.int32, (t, t), 0)
        col = lax.broadcasted_iota(jnp.int32, (t, t), 1)
        for h in range(2):
            s_buf[0][h][...] = qk(h, jnp.maximum(n - 1, 0))
        for h in range(2):
            consume(h, jnp.where(col <= row, s_buf[1][h][...], NEG_BIG), n)

    tile_id = lax.broadcasted_iota(jnp.int32, (1, LANES), 1)
    needed = tile_id < 0
    for h in range(2):
        m_low = jnp.min(m_ref[h], axis=0, keepdims=True)
        needed = needed | (reach_ref[h:h + 1, :] - m_low >= -SKIP_LOG2)
    first_needed = jnp.min(jnp.where(needed & (tile_id < n), tile_id, n).astype(F32))
    count = n - first_needed.astype(jnp.int32)

    def pair(i, carry):
        step(n - 1 - 2 * i, 0)
        step(n - 2 - 2 * i, 1)
        return carry

    n_pairs = jnp.maximum(count - 1, 0) // 2
    lax.fori_loop(0, n_pairs, pair, 0)
    left = count - 2 * n_pairs
    last = n - count

    @pl.when(left == 2)
    def _():
        step(last + 1, 0)
        for h in range(2):
            consume(h, s_buf[1][h][...], last)

    @pl.when(left == 1)
    def _():
        for h in range(2):
            consume(h, s_buf[0][h][...], last)

    acc0, acc1 = acc_ref[0], acc_ref[1]
    o_ref[pl.ds(start(n), t), :] = jnp.where(first, acc0 / acc0[:, HEAD_DIM:HEAD_DIM + 1],
                                             acc1 / acc1[:, 0:1])
    n_tiles = q_ref.shape[0] // t
    _attn_prepare(jnp.minimum(n + 1, n_tiles - 1), q_ref, kt_ref, v_ref, c_ref, st_ref, o_ref,
                  s_e0, s_e1, s_o0, s_o1, m_ref, acc_ref, reach_ref)


def _attn_call(q, kt, v, ct, k_norm2):
    B, L, _ = q.shape
    t = ATTN_TILE
    n_tiles = L // t
    assert n_tiles <= LANES
    c4 = ct.reshape(B, N_PAIRS, 2, L)
    c_next = jnp.roll(ct[:, :, ::t], -1, axis=-1)
    per_tile = jnp.stack([k_norm2[..., 0].transpose(0, 2, 1), c_next], axis=1)
    per_tile = jnp.pad(per_tile, ((0, 0),) * 3 + ((0, LANES - n_tiles),))
    stats = (per_tile.reshape(B, 2, N_PAIRS, 2, LANES).transpose(0, 2, 1, 3, 4)
             .reshape(B, N_PAIRS, 4, LANES))
    pair_block = lambda *shape: pl.BlockSpec((None, 2) + shape, lambda b, p: (b, p, 0, 0))
    lanes_of_pair = pl.BlockSpec((None, L, HEAD_PAIR), lambda b, p: (b, 0, p))
    return pl.pallas_call(
        _attn_kernel,
        grid=(B, N_PAIRS),
        in_specs=[lanes_of_pair, pair_block(HEAD_PAIR, L), pair_block(L, HEAD_PAIR),
                  pl.BlockSpec((None, None, 2, L), lambda b, p: (b, p, 0, 0)),
                  pl.BlockSpec((None, None, 4, LANES), lambda b, p: (b, p, 0, 0))],
        out_specs=lanes_of_pair,
        out_shape=jax.ShapeDtypeStruct((B, L, ATTN_WIDTH), F32),
        scratch_shapes=[pltpu.VMEM((t, t), F32)] * 4
                       + [pltpu.VMEM((2, t, 1), F32), pltpu.VMEM((2, t, HEAD_PAIR), F32),
                          pltpu.VMEM((2, LANES), F32)],
        compiler_params=pltpu.CompilerParams(
            dimension_semantics=("arbitrary", "arbitrary"),
            vmem_limit_bytes=VMEM_LIMIT),
        name="attn",
    )(q, kt, v, c4, stats)


def _gelu_tanh(x):
    return 0.5 * x * (1.0 + jnp.tanh(math.sqrt(2.0 / math.pi) * (x + 0.044715 * (x * x * x))))


def _cis(mag_arg, ang):
    mag = jnp.exp(mag_arg)
    return mag * jnp.cos(ang), mag * jnp.sin(ang)


def _cmul(ar, ai, br, bi):
    return ar * br - ai * bi, ar * bi + ai * br


def _cpow2(zr, zi, n):
    assert n & (n - 1) == 0
    while n > 1:
        zr, zi = zr * zr - zi * zi, 2.0 * zr * zi
        n //= 2
    return zr, zi


def _ssm_kernel(arow_ref, acol_ref, ldt_ref, bt_ref, cab_ref, dcol_ref, e_ref, f_ref, z_ref,
                *, chunks_per_seq):
    T, P, H = SSM_CHUNK, SSM_STATE, SSM_GROUP_CH
    TH = T * H
    hi = lax.Precision.HIGHEST
    dt = jnp.exp(ldt_ref[...])

    lam_r, lam_i = dt * arow_ref[0:1, :], dt * arow_ref[1:2, :]
    j0 = lax.broadcasted_iota(jnp.int32, (T, 2 * P), 0).astype(F32)
    pa0, pb0 = _cis(j0 * lam_r, j0 * lam_i)
    pa1, pb1 = _cmul(pa0, pb0, *_cis(lam_r, lam_i))
    over_h = lambda a: jnp.concatenate(
        [jnp.broadcast_to(a[j:j + 1, :], (H, 2 * P)) for j in range(T)], axis=0)
    ca, cb = jnp.tile(cab_ref[0], (T, 1)), jnp.tile(cab_ref[1], (T, 1))
    c_pow0 = over_h(pa0) * ca + over_h(pb0) * cb
    c_pow1 = over_h(pa1) * ca + over_h(pb1) * cb

    a_r, a_i = acol_ref[:, 0:1], acol_ref[:, 1:2]
    lr, li = dt * a_r, dt * a_i
    abar_r, abar_i = _cis(lr, li)
    nr, ni = abar_r - 1.0, abar_i
    den = a_r * a_r + a_i * a_i
    fr, fi = (nr * a_r + ni * a_i) / den, (ni * a_r - nr * a_i) / den
    b_r, b_i = bt_ref[0], bt_ref[1]
    bb_r, bb_i = fr * b_r - fi * b_i, fr * b_i + fi * b_r

    kcol = jnp.dot(c_pow0, jnp.concatenate([bb_r, bb_i], axis=0), precision=hi,
                   preferred_element_type=F32)
    lane_h = lax.broadcasted_iota(jnp.int32, (H, LANES), 1) % H
    skip = jnp.where(lane_h == lax.broadcasted_iota(jnp.int32, (H, LANES), 0), dcol_ref[...], 0.0)

    @pl.when(pl.program_id(0) == 0)
    def _():
        z_ref[0:TH, :] = jnp.zeros((TH, LANES), F32)

    z_ref[TH:2 * TH, :] = kcol
    z_ref[TH:TH + H, :] = kcol[:H] + skip
    lane_group = lax.broadcasted_iota(jnp.int32, (1, LANES), 1) // H
    groups_per_block = LANES // H
    blocks = []
    for v in range(TH // LANES):
        blk = None
        for u in range(groups_per_block):
            s = v * groups_per_block + u
            piece = z_ref[TH - H * s:2 * TH - H * s, :]
            blk = piece if blk is None else jnp.where(lane_group == u, piece, blk)
        blocks.append(blk.astype(BF16))
    mt = jnp.concatenate(blocks, axis=1)

    expo = (groups_per_block - 1 - lane_group).astype(F32)
    wr, wi = _cis(lr * expo, li * expo)
    hop_r, hop_i = _cpow2(abar_r, abar_i, groups_per_block)
    w1_r, w1_i = [], []
    for v in range(TH // LANES):
        w1_r.insert(0, wr * bb_r - wi * bb_i)
        w1_i.insert(0, wr * bb_i + wi * bb_r)
        wr, wi = _cmul(wr, wi, hop_r, hop_i)
    w1t = jnp.concatenate([jnp.concatenate(w1_r, axis=1),
                           jnp.concatenate(w1_i, axis=1)], axis=0).astype(BF16)

    e = e_ref[...].reshape(TH, e_ref.shape[-1])
    y = jnp.dot(mt, e, preferred_element_type=F32)
    st = jnp.dot(w1t, e, preferred_element_type=F32)
    sr, si = st[:P], st[P:]
    pos = lax.broadcasted_iota(jnp.int32, sr.shape, 1) % chunks_per_seq

    def shifted(a, shift):
        return jnp.where(pos >= shift, pltpu.roll(a, shift, 1), 0.0)

    qr, qi = _cpow2(abar_r, abar_i, T)
    shift = 1
    while shift < chunks_per_seq:
        srs, sis = shifted(sr, shift), shifted(si, shift)
        sr, si = sr + qr * srs - qi * sis, si + qr * sis + qi * srs
        qr, qi = qr * qr - qi * qi, 2.0 * qr * qi
        shift *= 2
    x_prev = jnp.concatenate([shifted(sr, 1), shifted(si, 1)], axis=0).astype(BF16)
    y = y + jnp.dot(c_pow1.astype(BF16), x_prev, preferred_element_type=F32)
    f_ref[...] = _gelu_tanh(y).reshape(f_ref.shape)


def _ssm_call(arow, acol, ldt, bt, cab, dcol, e, chunks_per_seq):
    T, _, NC = e.shape
    G = arow.shape[0]
    grp = lambda a: pl.BlockSpec((None,) + a.shape[1:], lambda g: (g,) + (0,) * (a.ndim - 1))
    return pl.pallas_call(
        functools.partial(_ssm_kernel, chunks_per_seq=chunks_per_seq),
        grid=(G,),
        in_specs=[grp(a) for a in (arow, acol, ldt, bt, cab, dcol)]
                 + [pl.BlockSpec((T, SSM_GROUP_CH, NC), lambda g: (0, g, 0))],
        out_specs=pl.BlockSpec((T, SSM_GROUP_CH, NC), lambda g: (0, g, 0)),
        out_shape=jax.ShapeDtypeStruct((T, SSM_WIDTH, NC), F32),
        scratch_shapes=[pltpu.VMEM((2 * T * SSM_GROUP_CH, LANES), F32)],
        compiler_params=pltpu.CompilerParams(
            dimension_semantics=("arbitrary",), vmem_limit_bytes=VMEM_LIMIT),
        name="ssm",
    )(arow, acol, ldt, bt, cab, dcol, e)


def _ssm_param_layouts(a_re, a_im, log_dt, b_re, b_im, c_re, c_im, d_skip):
    G = a_re.shape[0]
    arow = jnp.stack([jnp.concatenate([a_re, a_re], -1), jnp.concatenate([a_im, a_im], -1)], 1)
    acol = jnp.stack([a_re, a_im], -1)
    reps = LANES // SSM_GROUP_CH
    bt = jnp.stack([jnp.tile(b_re, (1, 1, reps)), jnp.tile(b_im, (1, 1, reps))], 1)
    cab = jnp.stack([jnp.concatenate([c_re, -c_im], -1), jnp.concatenate([-c_im, -c_re], -1)], 1)
    return (arow.astype(F32), acol.astype(F32), log_dt.reshape(G, 1, 1).astype(F32),
            bt.astype(F32), cab.astype(F32), d_skip.reshape(G, SSM_GROUP_CH, 1).astype(F32))


def _tail_kernel(h1_ref, attn_ref, y_ref, p_ref, wglu_ref, bglu_ref, ga_ref, gs_ref,
                 woa_ref, wos_ref, g2_ref, w1_ref, w3_ref, w2_ref, gp_ref, wpg_ref,
                 wpp_ref, gf_ref, o_ref):
    y = y_ref[...]
    glu = y * _sigmoid(jnp.dot(y.astype(BF16), wglu_ref[...], preferred_element_type=F32)
                       + bglu_ref[...])
    an = _rms(attn_ref[...], ga_ref[...]).astype(BF16)
    sn = _rms(glu, gs_ref[...]).astype(BF16)
    h = (h1_ref[...] + jnp.dot(an, woa_ref[...], preferred_element_type=F32)
         + jnp.dot(sn, wos_ref[...], preferred_element_type=F32))
    h = h + 0.5 * _swiglu(_rms(h, g2_ref[...]).astype(BF16), w1_ref, w3_ref, w2_ref)
    gate = _sigmoid(jnp.dot(_rms(h, gp_ref[...]).astype(BF16), wpg_ref[...],
                            preferred_element_type=F32))
    h = h + gate * jnp.dot(p_ref[...].astype(BF16), wpp_ref[...], preferred_element_type=F32)
    o_ref[...] = _rms(h, gf_ref[...])


def _tail_call(h1, attn, y, p, *consts):
    B, L, D = h1.shape
    tm = TOKEN_TILE
    tile = lambda w: pl.BlockSpec((None, tm, w), lambda b, i: (b, i, 0))
    return pl.pallas_call(
        _tail_kernel,
        grid=(B, L // tm),
        in_specs=[tile(D), tile(ATTN_WIDTH), tile(SSM_WIDTH), tile(PLE_DIM)]
                 + [_const_spec(c.shape) for c in consts],
        out_specs=tile(D),
        out_shape=jax.ShapeDtypeStruct((B, L, D), F32),
        compiler_params=pltpu.CompilerParams(
            dimension_semantics=("arbitrary", "arbitrary"),
            vmem_limit_bytes=VMEM_LIMIT),
        name="tail",
    )(h1, attn, y, p, *consts)


def kernel(x, p, g_ffn1, w1_a, w3_a, w2_a, g_mix, w_in, b_f, a_re, a_im, log_dt, b_re, b_im, c_re, c_im, d_skip, w_glu, b_glu, g_attn_out, g_ssm_out, w_out, g_ffn2, w1_b, w3_b, w2_b, g_ple, w_ple_gate, w_ple_proj, g_final):
    B, L, D = x.shape
    assert D == D_MODEL and L % ATTN_TILE == 0 and L % TOKEN_TILE == 0 and L % SSM_CHUNK == 0
    assert g_ffn1.shape[0] == 1, "single layer"
    assert TOKEN_TILE == ATTN_TILE, "decay bias rows are relative to the kv tile start"
    row = lambda g: g.reshape(1, -1).astype(F32)
    bf = lambda w: w.astype(BF16)
    s_v, s_f = 3 * ATTN_WIDTH, 3 * ATTN_WIDTH + ATTN_HEADS
    w_in0 = w_in[0]
    scale = LOG2E / math.sqrt(HEAD_DIM)
    wqkv = jnp.concatenate([w_in0[:, ATTN_WIDTH:s_v], w_in0[:, :ATTN_WIDTH] * scale], axis=1)
    wf = jnp.pad(w_in0[:, s_v:s_f], ((0, 0), (0, LANES - ATTN_HEADS)))

    h1, q, kt, v, s_in, ct, k_norm2 = _head_call(
        x, row(g_ffn1[0]), bf(w1_a[0]), bf(w3_a[0]), bf(w2_a[0]), row(g_mix[0]),
        bf(wqkv), bf(w_in0[:, s_f:]), bf(wf), b_f[0].reshape(ATTN_HEADS, 1).astype(F32))

    attn = _attn_call(q, kt, v, ct, k_norm2)

    T = SSM_CHUNK
    chunks_per_seq = L // T
    n_chunks = B * chunks_per_seq
    e = s_in.reshape(n_chunks, T, SSM_WIDTH).transpose(1, 2, 0)
    f = _ssm_call(*_ssm_param_layouts(a_re[0], a_im[0], log_dt[0], b_re[0], b_im[0],
                                      c_re[0], c_im[0], d_skip[0]),
                  e, chunks_per_seq)
    y = f.transpose(2, 0, 1).reshape(B, L, SSM_WIDTH)

    w_out0 = w_out[0]
    return _tail_call(
        h1, attn, y, p[0],
        bf(w_glu[0]), row(b_glu[0]), row(g_attn_out[0]), row(g_ssm_out[0]),
        bf(w_out0[:ATTN_WIDTH]), bf(w_out0[ATTN_WIDTH:]), row(g_ffn2[0]),
        bf(w1_b[0]), bf(w3_b[0]), bf(w2_b[0]), row(g_ple[0]), bf(w_ple_gate[0]),
        bf(w_ple_proj[0]), row(g_final))
```

```python
import functools
import math

import jax
import jax.numpy as jnp
from jax import lax
from jax.experimental import pallas as pl
from jax.experimental.pallas import tpu as pltpu

D_MODEL = 1024
ATTN_HEADS = 8
HEAD_DIM = 64
ATTN_WIDTH = ATTN_HEADS * HEAD_DIM
SSM_WIDTH = D_MODEL - ATTN_WIDTH
SSM_GROUP_CH = 16
SSM_GROUPS = SSM_WIDTH // SSM_GROUP_CH
SSM_STATE = 64
D_FF = 2816
PLE_DIM = 256
EPS = 1e-6

LANES = 128
HEAD_PAIR = 2 * HEAD_DIM
N_PAIRS = ATTN_HEADS // 2
FF_CHUNK = 256
TOKEN_TILE = 512
ATTN_TILE = 512
SSM_CHUNK = 32
SSM_GROUPS_PER_STEP = 2
NEG_BIG = -1e30
SKIP_LOG2 = 140.0
BOUND_SLACK_MUL = 1.001
BOUND_SLACK_ADD = 1.0
LOG2E = math.log2(math.e)
N_BIAS = 3
BIAS_ROWS = 8
VMEM_LIMIT = 56 * 1024 * 1024

BF16 = jnp.bfloat16
F32 = jnp.float32


def _rms(x, g):
    ms = jnp.mean(x * x, axis=-1, keepdims=True)
    return x * lax.rsqrt(ms + EPS) * g


def _sigmoid(x):
    return 1.0 / (1.0 + jnp.exp(-x))


def _swiglu(xn, w1_ref, w3_ref, w2_ref):
    acc = None
    for c in range(D_FF // FF_CHUNK):
        sl = slice(c * FF_CHUNK, (c + 1) * FF_CHUNK)
        a = jnp.dot(xn, w1_ref[:, sl], preferred_element_type=F32)
        b = jnp.dot(xn, w3_ref[:, sl], preferred_element_type=F32)
        gated = (a * _sigmoid(a) * b).astype(BF16)
        part = jnp.dot(gated, w2_ref[sl, :], preferred_element_type=F32)
        acc = part if acc is None else acc + part
    return acc


def _const_spec(shape):
    nd = len(shape)
    return pl.BlockSpec(shape, lambda *_: (0,) * nd, pipeline_mode=pl.Buffered(1))


def _head_kernel(x_ref, g1_ref, w1_ref, w3_ref, w2_ref, gm_ref, wkvq_ref, ws_ref,
                 wf_ref, bf_ref, h1_ref, q_ref, kt_ref, v_ref, s_ref, ct_ref, kn_ref,
                 carry_ref):
    tm = x_ref.shape[0]
    x = x_ref[...]
    h1 = x + 0.5 * _swiglu(_rms(x, g1_ref[...]).astype(BF16), w1_ref, w3_ref, w2_ref)
    h1_ref[...] = h1
    un = _rms(h1, gm_ref[...]).astype(BF16)
    project = lambda w: jnp.dot(un, w, preferred_element_type=F32)
    zf = project(wf_ref[...])
    kv = project(wkvq_ref[:, :2 * ATTN_WIDTH])
    zft = zf.T[:ATTN_HEADS, :] + bf_ref[...]
    logf = jnp.minimum(zft, 0.0) - jnp.log1p(jnp.exp(-jnp.abs(zft)))
    lane = lax.broadcasted_iota(jnp.int32, logf.shape, 1)
    c = logf
    shift = 1
    while shift < tm:
        c = c + jnp.where(lane >= shift, pltpu.roll(c, shift, 1), 0.0)
        shift *= 2

    @pl.when(pl.program_id(1) == 0)
    def _():
        carry_ref[...] = jnp.zeros_like(carry_ref)

    c_abs = c + carry_ref[:, 0:1]
    ct_ref[...] = c_abs * LOG2E
    carry_ref[...] = jnp.broadcast_to(c_abs[:, tm - 1:tm], carry_ref.shape)

    rel = (c - c[:, 0:1]) * LOG2E
    hi = rel.astype(BF16).astype(F32)
    mid = (rel - hi).astype(BF16).astype(F32)
    lo = (rel - hi - mid).astype(BF16).astype(F32)
    kt = kv[:, :ATTN_WIDTH].astype(BF16).astype(F32).T
    vv = kv[:, ATTN_WIDTH:]
    k_sq = (kt * kt).reshape(ATTN_HEADS, HEAD_DIM, tm)
    kn_ref[...] = jnp.broadcast_to(
        jnp.max(jnp.sum(k_sq, axis=1), axis=-1, keepdims=True), kn_ref.shape)
    sub = lax.broadcasted_iota(jnp.int32, (BIAS_ROWS, tm), 0)
    zeros = jnp.zeros((HEAD_DIM - BIAS_ROWS, tm), F32)
    vlane = lax.broadcasted_iota(jnp.int32, (tm, HEAD_PAIR), 1)
    for h in range(ATTN_HEADS):
        bias = jnp.where(sub == 0, -hi[h:h + 1],
                         jnp.where(sub == 1, -mid[h:h + 1],
                                   jnp.where(sub == 2, -lo[h:h + 1], 0.0)))
        k_h = kt[h * HEAD_DIM:(h + 1) * HEAD_DIM]
        vp = vv[:, (h // 2) * HEAD_PAIR:(h // 2 + 1) * HEAD_PAIR]
        if h % 2 == 0:
            kt_ref[h] = jnp.concatenate([k_h, bias, zeros], axis=0).astype(BF16)
            v_ref[h] = jnp.where(vlane < HEAD_DIM, vp,
                                 jnp.where(vlane == HEAD_DIM, 1.0, 0.0)).astype(BF16)
        else:
            kt_ref[h] = jnp.concatenate([bias, zeros, k_h], axis=0).astype(BF16)
            v_ref[h] = jnp.where(vlane >= HEAD_DIM, vp,
                                 jnp.where(vlane == 0, 1.0, 0.0)).astype(BF16)

    q_ref[...] = project(wkvq_ref[:, 2 * ATTN_WIDTH:]).astype(BF16)
    s_ref[...] = project(ws_ref[...]).astype(BF16)


def _head_call(x, g1, w1, w3, w2, gm, wqkv, ws, wf, bf):
    B, L, D = x.shape
    tm = TOKEN_TILE
    tile = lambda w: pl.BlockSpec((None, tm, w), lambda b, i: (b, i, 0))
    out_shape = (
        jax.ShapeDtypeStruct((B, L, D), F32),
        jax.ShapeDtypeStruct((B, L, ATTN_WIDTH), BF16),
        jax.ShapeDtypeStruct((B, ATTN_HEADS, HEAD_PAIR, L), BF16),
        jax.ShapeDtypeStruct((B, ATTN_HEADS, L, HEAD_PAIR), BF16),
        jax.ShapeDtypeStruct((B, L, SSM_WIDTH), BF16),
        jax.ShapeDtypeStruct((B, ATTN_HEADS, L), F32),
        jax.ShapeDtypeStruct((B, L // tm, ATTN_HEADS, LANES), F32),
    )
    return pl.pallas_call(
        _head_kernel,
        grid=(B, L // tm),
        in_specs=[tile(D), _const_spec(g1.shape), _const_spec(w1.shape),
                  _const_spec(w3.shape), _const_spec(w2.shape), _const_spec(gm.shape),
                  _const_spec(wqkv.shape), _const_spec(ws.shape), _const_spec(wf.shape),
                  _const_spec(bf.shape)],
        out_specs=(tile(D), tile(ATTN_WIDTH),
                   pl.BlockSpec((None, ATTN_HEADS, HEAD_PAIR, tm), lambda b, i: (b, 0, 0, i)),
                   pl.BlockSpec((None, ATTN_HEADS, tm, HEAD_PAIR), lambda b, i: (b, 0, i, 0)),
                   tile(SSM_WIDTH),
                   pl.BlockSpec((None, ATTN_HEADS, tm), lambda b, i: (b, 0, i)),
                   pl.BlockSpec((None, None, ATTN_HEADS, LANES), lambda b, i: (b, i, 0, 0))),
        out_shape=out_shape,
        scratch_shapes=[pltpu.VMEM((ATTN_HEADS, LANES), F32)],
        compiler_params=pltpu.CompilerParams(
            dimension_semantics=("arbitrary", "arbitrary"),
            vmem_limit_bytes=VMEM_LIMIT),
        name="head",
    )(x, g1, w1, w3, w2, gm, wqkv, ws, wf, bf)


def _attn_kernel(q_ref, *refs):
    _attn_prepare(0, q_ref, *refs)

    def q_tile(n, carry):
        _attn_q_tile(n, q_ref, *refs)
        return carry

    lax.fori_loop(0, q_ref.shape[0] // ATTN_TILE, q_tile, 0)


def _tile_start(i):
    return pl.multiple_of(i * ATTN_TILE, ATTN_TILE)


def _q_heads(q):
    lane = lax.broadcasted_iota(jnp.int32, (1, HEAD_PAIR), 1)
    first = lane < HEAD_DIM
    ones_even = jnp.where((lane >= HEAD_DIM) & (lane < HEAD_DIM + N_BIAS), 1.0, 0.0).astype(BF16)
    ones_odd = jnp.where(lane < N_BIAS, 1.0, 0.0).astype(BF16)
    return first, (jnp.where(first, q, ones_even), jnp.where(first, ones_odd, q))


def _attn_prepare(n, q_ref, kt_ref, v_ref, c_ref, st_ref, o_ref, s_e0, s_e1, s_o0, s_o1, m_ref,
                  acc_ref, reach_ref):
    t = ATTN_TILE
    q = q_ref[pl.ds(_tile_start(n), t), :]
    first, q_heads = _q_heads(q)
    q_sq = q.astype(F32) * q.astype(F32)
    for h, s_ref in enumerate((s_o0, s_o1)):
        s_ref[...] = jnp.dot(q_heads[h], kt_ref[h, :, pl.ds(_tile_start(n), t)],
                             preferred_element_type=F32)
        q_norm2 = jnp.max(jnp.sum(jnp.where(first == (h == 0), q_sq, 0.0), axis=-1, keepdims=True),
                          axis=0, keepdims=True)
        k_norm2, c_next = st_ref[h:h + 1, :], st_ref[2 + h:3 + h, :]
        c_q = c_ref[h:h + 1, pl.ds(_tile_start(n), LANES)][:, 0:1]
        reach_ref[h:h + 1, :] = (jnp.sqrt(q_norm2 * k_norm2) * BOUND_SLACK_MUL + BOUND_SLACK_ADD
                                 + c_q - c_next)


def _attn_q_tile(n, q_ref, kt_ref, v_ref, c_ref, st_ref, o_ref, s_e0, s_e1, s_o0, s_o1, m_ref,
                 acc_ref, reach_ref):
    t = ATTN_TILE
    start = _tile_start
    first, q_heads = _q_heads(q_ref[pl.ds(start(n), t), :])
    s_buf = ((s_e0, s_e1), (s_o0, s_o1))
    c_q = [c_ref[h:h + 1, pl.ds(start(n), LANES)][:, 0:1] for h in range(2)]

    def qk(h, kv):
        return jnp.dot(q_heads[h], kt_ref[h, :, pl.ds(start(kv), t)], preferred_element_type=F32)

    def consume(h, s, kv):
        d = c_ref[h:h + 1, pl.ds(start(kv), LANES)][:, 0:1] - c_q[h]
        m_old = m_ref[h]
        m_new = jnp.maximum(m_old, jnp.max(s, axis=-1, keepdims=True) - d)
        p = jnp.exp2(s - (m_new + d)).astype(BF16)
        m_ref[h] = m_new
        acc_ref[h] = jnp.exp2(m_old - m_new) * acc_ref[h] + jnp.dot(
            p, v_ref[h, pl.ds(start(kv), t), :], preferred_element_type=F32)

    def step(kv, par):
        for h in range(2):
            s_buf[1 - par][h][...] = qk(h, kv - 1)
        for h in range(2):
            consume(h, s_buf[par][h][...], kv)

    m_ref[...] = jnp.full(m_ref.shape, NEG_BIG, F32)
    acc_ref[...] = jnp.zeros(acc_ref.shape, F32)

    @pl.when(n >= 0)
    def _():
        row = lax.broadcasted_iota(jnp.int32, (t, t), 0)
        col = lax.broadcasted_iota(jnp.int32, (t, t), 1)
        for h in range(2):
            s_buf[0][h][...] = qk(h, jnp.maximum(n - 1, 0))
        for h in range(2):
            consume(h, jnp.where(col <= row, s_buf[1][h][...], NEG_BIG), n)

    tile_id = lax.broadcasted_iota(jnp.int32, (1, LANES), 1)
    needed = tile_id < 0
    for h in range(2):
        m_low = jnp.min(m_ref[h], axis=0, keepdims=True)
        needed = needed | (reach_ref[h:h + 1, :] - m_low >= -SKIP_LOG2)
    first_needed = jnp.min(jnp.where(needed & (tile_id < n), tile_id, n).astype(F32))
    count = n - first_needed.astype(jnp.int32)

    def pair(i, carry):
        step(n - 1 - 2 * i, 0)
        step(n - 2 - 2 * i, 1)
        return carry

    n_pairs = jnp.maximum(count - 1, 0) // 2
    lax.fori_loop(0, n_pairs, pair, 0)
    left = count - 2 * n_pairs
    last = n - count

    @pl.when(left == 2)
    def _():
        step(last + 1, 0)
        for h in range(2):
            consume(h, s_buf[1][h][...], last)

    @pl.when(left == 1)
    def _():
        for h in range(2):
            consume(h, s_buf[0][h][...], last)

    acc0, acc1 = acc_ref[0], acc_ref[1]
    o_ref[pl.ds(start(n), t), :] = jnp.where(first, acc0 / acc0[:, HEAD_DIM:HEAD_DIM + 1],
                                             acc1 / acc1[:, 0:1])
    n_tiles = q_ref.shape[0] // t
    _attn_prepare(jnp.minimum(n + 1, n_tiles - 1), q_ref, kt_ref, v_ref, c_ref, st_ref, o_ref,
                  s_e0, s_e1, s_o0, s_o1, m_ref, acc_ref, reach_ref)


def _attn_call(q, kt, v, ct, k_norm2):
    B, L, _ = q.shape
    t = ATTN_TILE
    n_tiles = L // t
    assert n_tiles <= LANES
    c4 = ct.reshape(B, N_PAIRS, 2, L)
    c_next = jnp.roll(ct[:, :, ::t], -1, axis=-1)
    per_tile = jnp.stack([k_norm2[..., 0].transpose(0, 2, 1), c_next], axis=1)
    per_tile = jnp.pad(per_tile, ((0, 0),) * 3 + ((0, LANES - n_tiles),))
    stats = (per_tile.reshape(B, 2, N_PAIRS, 2, LANES).transpose(0, 2, 1, 3, 4)
             .reshape(B, N_PAIRS, 4, LANES))
    pair_block = lambda *shape: pl.BlockSpec((None, 2) + shape, lambda b, p: (b, p, 0, 0))
    lanes_of_pair = pl.BlockSpec((None, L, HEAD_PAIR), lambda b, p: (b, 0, p))
    return pl.pallas_call(
        _attn_kernel,
        grid=(B, N_PAIRS),
        in_specs=[lanes_of_pair, pair_block(HEAD_PAIR, L), pair_block(L, HEAD_PAIR),
                  pl.BlockSpec((None, None, 2, L), lambda b, p: (b, p, 0, 0)),
                  pl.BlockSpec((None, None, 4, LANES), lambda b, p: (b, p, 0, 0))],
        out_specs=lanes_of_pair,
        out_shape=jax.ShapeDtypeStruct((B, L, ATTN_WIDTH), F32),
        scratch_shapes=[pltpu.VMEM((t, t), F32)] * 4
                       + [pltpu.VMEM((2, t, 1), F32), pltpu.VMEM((2, t, HEAD_PAIR), F32),
                          pltpu.VMEM((2, LANES), F32)],
        compiler_params=pltpu.CompilerParams(
            dimension_semantics=("arbitrary", "arbitrary"),
            vmem_limit_bytes=VMEM_LIMIT),
        name="attn",
    )(q, kt, v, c4, stats)


def _gelu_tanh(x):
    return 0.5 * x * (1.0 + jnp.tanh(math.sqrt(2.0 / math.pi) * (x + 0.044715 * (x * x * x))))


def _cis(mag_arg, ang):
    mag = jnp.exp(mag_arg)
    return mag * jnp.cos(ang), mag * jnp.sin(ang)


def _cmul(ar, ai, br, bi):
    return ar * br - ai * bi, ar * bi + ai * br


def _cpow2(zr, zi, n):
    assert n & (n - 1) == 0
    while n > 1:
        zr, zi = zr * zr - zi * zi, 2.0 * zr * zi
        n //= 2
    return zr, zi


def _ssm_kernel(arow_ref, acol_ref, ldt_ref, bt_ref, cab_ref, dcol_ref, e_ref, f_ref, z_ref,
                *, chunks_per_seq):
    H = SSM_GROUP_CH

    @pl.when(pl.program_id(0) == 0)
    def _():
        z_ref[:, 0:SSM_CHUNK * H, :] = jnp.zeros((z_ref.shape[0], SSM_CHUNK * H, LANES), F32)

    for gi in range(SSM_GROUPS_PER_STEP):
        y = _ssm_group(arow_ref.at[gi], acol_ref.at[gi], ldt_ref.at[gi], bt_ref.at[gi], cab_ref.at[gi],
                       dcol_ref.at[gi], e_ref[:, gi * H:(gi + 1) * H, :], z_ref.at[gi], chunks_per_seq)
        f_ref[:, gi * H:(gi + 1) * H, :] = y.reshape(SSM_CHUNK, H, y.shape[-1])


def _ssm_group(arow_ref, acol_ref, ldt_ref, bt_ref, cab_ref, dcol_ref, e, z_ref, chunks_per_seq):
    T, P, H = SSM_CHUNK, SSM_STATE, SSM_GROUP_CH
    TH = T * H
    hi = lax.Precision.HIGHEST
    dt = jnp.exp(ldt_ref[...])

    lam_r, lam_i = dt * arow_ref[0:1, :], dt * arow_ref[1:2, :]
    j0 = lax.broadcasted_iota(jnp.int32, (T, 2 * P), 0).astype(F32)
    pa0, pb0 = _cis(j0 * lam_r, j0 * lam_i)
    pa1, pb1 = _cmul(pa0, pb0, *_cis(lam_r, lam_i))
    over_h = lambda a: jnp.concatenate(
        [jnp.broadcast_to(a[j:j + 1, :], (H, 2 * P)) for j in range(T)], axis=0)
    ca, cb = jnp.tile(cab_ref[0], (T, 1)), jnp.tile(cab_ref[1], (T, 1))
    c_pow0 = over_h(pa0) * ca + over_h(pb0) * cb
    c_pow1 = over_h(pa1) * ca + over_h(pb1) * cb

    a_r, a_i = acol_ref[:, 0:1], acol_ref[:, 1:2]
    lr, li = dt * a_r, dt * a_i
    abar_r, abar_i = _cis(lr, li)
    nr, ni = abar_r - 1.0, abar_i
    den = a_r * a_r + a_i * a_i
    fr, fi = (nr * a_r + ni * a_i) / den, (ni * a_r - nr * a_i) / den
    b_r, b_i = bt_ref[0], bt_ref[1]
    bb_r, bb_i = fr * b_r - fi * b_i, fr * b_i + fi * b_r

    kcol = jnp.dot(c_pow0, jnp.concatenate([bb_r, bb_i], axis=0), precision=hi,
                   preferred_element_type=F32)
    lane_h = lax.broadcasted_iota(jnp.int32, (H, LANES), 1) % H
    skip = jnp.where(lane_h == lax.broadcasted_iota(jnp.int32, (H, LANES), 0), dcol_ref[...], 0.0)

    z_ref[TH:2 * TH, :] = kcol
    z_ref[TH:TH + H, :] = kcol[:H] + skip
    lane_group = lax.broadcasted_iota(jnp.int32, (1, LANES), 1) // H
    groups_per_block = LANES // H
    blocks = []
    for v in range(TH // LANES):
        blk = None
        for u in range(groups_per_block):
            s = v * groups_per_block + u
            piece = z_ref[TH - H * s:2 * TH - H * s, :]
            blk = piece if blk is None else jnp.where(lane_group == u, piece, blk)
        blocks.append(blk.astype(BF16))
    mt = jnp.concatenate(blocks, axis=1)

    expo = (groups_per_block - 1 - lane_group).astype(F32)
    wr, wi = _cis(lr * expo, li * expo)
    hop_r, hop_i = _cpow2(abar_r, abar_i, groups_per_block)
    w1_r, w1_i = [], []
    for v in range(TH // LANES):
        w1_r.insert(0, wr * bb_r - wi * bb_i)
        w1_i.insert(0, wr * bb_i + wi * bb_r)
        wr, wi = _cmul(wr, wi, hop_r, hop_i)
    w1t = jnp.concatenate([jnp.concatenate(w1_r, axis=1),
                           jnp.concatenate(w1_i, axis=1)], axis=0).astype(BF16)

    e = e.reshape(TH, e.shape[-1])
    y = jnp.dot(mt, e, preferred_element_type=F32)
    st = jnp.dot(w1t, e, preferred_element_type=F32)
    sr, si = st[:P], st[P:]
    pos = lax.broadcasted_iota(jnp.int32, sr.shape, 1) % chunks_per_seq

    def shifted(a, shift):
        return jnp.where(pos >= shift, pltpu.roll(a, shift, 1), 0.0)

    qr, qi = _cpow2(abar_r, abar_i, T)
    shift = 1
    while shift < chunks_per_seq:
        srs, sis = shifted(sr, shift), shifted(si, shift)
        sr, si = sr + qr * srs - qi * sis, si + qr * sis + qi * srs
        qr, qi = qr * qr - qi * qi, 2.0 * qr * qi
        shift *= 2
    x_prev = jnp.concatenate([shifted(sr, 1), shifted(si, 1)], axis=0).astype(BF16)
    y = y + jnp.dot(c_pow1.astype(BF16), x_prev, preferred_element_type=F32)
    return _gelu_tanh(y)


def _ssm_call(arow, acol, ldt, bt, cab, dcol, e, chunks_per_seq):
    T, _, NC = e.shape
    G = arow.shape[0]
    gps = SSM_GROUPS_PER_STEP
    assert G % gps == 0
    grp = lambda a: pl.BlockSpec((gps,) + a.shape[1:], lambda g: (g,) + (0,) * (a.ndim - 1))
    channels = pl.BlockSpec((T, gps * SSM_GROUP_CH, NC), lambda g: (0, g, 0))
    return pl.pallas_call(
        functools.partial(_ssm_kernel, chunks_per_seq=chunks_per_seq),
        grid=(G // gps,),
        in_specs=[grp(a) for a in (arow, acol, ldt, bt, cab, dcol)] + [channels],
        out_specs=channels,
        out_shape=jax.ShapeDtypeStruct((T, SSM_WIDTH, NC), F32),
        scratch_shapes=[pltpu.VMEM((gps, 2 * T * SSM_GROUP_CH, LANES), F32)],
        compiler_params=pltpu.CompilerParams(
            dimension_semantics=("arbitrary",), vmem_limit_bytes=VMEM_LIMIT),
        name="ssm",
    )(arow, acol, ldt, bt, cab, dcol, e)


def _ssm_param_layouts(a_re, a_im, log_dt, b_re, b_im, c_re, c_im, d_skip):
    G = a_re.shape[0]
    arow = jnp.stack([jnp.concatenate([a_re, a_re], -1), jnp.concatenate([a_im, a_im], -1)], 1)
    acol = jnp.stack([a_re, a_im], -1)
    reps = LANES // SSM_GROUP_CH
    bt = jnp.stack([jnp.tile(b_re, (1, 1, reps)), jnp.tile(b_im, (1, 1, reps))], 1)
    cab = jnp.stack([jnp.concatenate([c_re, -c_im], -1), jnp.concatenate([-c_im, -c_re], -1)], 1)
    return (arow.astype(F32), acol.astype(F32), log_dt.reshape(G, 1, 1).astype(F32),
            bt.astype(F32), cab.astype(F32), d_skip.reshape(G, SSM_GROUP_CH, 1).astype(F32))


def _tail_kernel(h1_ref, attn_ref, y_ref, p_ref, wglu_ref, bglu_ref, ga_ref, gs_ref,
                 woa_ref, wos_ref, g2_ref, w1_ref, w3_ref, w2_ref, gp_ref, wpg_ref,
                 wpp_ref, gf_ref, o_ref):
    y = y_ref[...]
    glu = y * _sigmoid(jnp.dot(y.astype(BF16), wglu_ref[...], preferred_element_type=F32)
                       + bglu_ref[...])
    an = _rms(attn_ref[...], ga_ref[...]).astype(BF16)
    sn = _rms(glu, gs_ref[...]).astype(BF16)
    h = (h1_ref[...] + jnp.dot(an, woa_ref[...], preferred_element_type=F32)
         + jnp.dot(sn, wos_ref[...], preferred_element_type=F32))
    h = h + 0.5 * _swiglu(_rms(h, g2_ref[...]).astype(BF16), w1_ref, w3_ref, w2_ref)
    gate = _sigmoid(jnp.dot(_rms(h, gp_ref[...]).astype(BF16), wpg_ref[...],
                            preferred_element_type=F32))
    h = h + gate * jnp.dot(p_ref[...].astype(BF16), wpp_ref[...], preferred_element_type=F32)
    o_ref[...] = _rms(h, gf_ref[...])


def _tail_call(h1, attn, y, p, *consts):
    B, L, D = h1.shape
    tm = TOKEN_TILE
    tile = lambda w: pl.BlockSpec((None, tm, w), lambda b, i: (b, i, 0))
    return pl.pallas_call(
        _tail_kernel,
        grid=(B, L // tm),
        in_specs=[tile(D), tile(ATTN_WIDTH), tile(SSM_WIDTH), tile(PLE_DIM)]
                 + [_const_spec(c.shape) for c in consts],
        out_specs=tile(D),
        out_shape=jax.ShapeDtypeStruct((B, L, D), F32),
        compiler_params=pltpu.CompilerParams(
            dimension_semantics=("arbitrary", "arbitrary"),
            vmem_limit_bytes=VMEM_LIMIT),
        name="tail",
    )(h1, attn, y, p, *consts)


def kernel(x, p, g_ffn1, w1_a, w3_a, w2_a, g_mix, w_in, b_f, a_re, a_im, log_dt, b_re, b_im, c_re, c_im, d_skip, w_glu, b_glu, g_attn_out, g_ssm_out, w_out, g_ffn2, w1_b, w3_b, w2_b, g_ple, w_ple_gate, w_ple_proj, g_final):
    B, L, D = x.shape
    assert D == D_MODEL and L % ATTN_TILE == 0 and L % TOKEN_TILE == 0 and L % SSM_CHUNK == 0
    assert g_ffn1.shape[0] == 1, "single layer"
    assert TOKEN_TILE == ATTN_TILE, "decay bias rows are relative to the kv tile start"
    row = lambda g: g.reshape(1, -1).astype(F32)
    bf = lambda w: w.astype(BF16)
    s_v, s_f = 3 * ATTN_WIDTH, 3 * ATTN_WIDTH + ATTN_HEADS
    w_in0 = w_in[0]
    scale = LOG2E / math.sqrt(HEAD_DIM)
    wqkv = jnp.concatenate([w_in0[:, ATTN_WIDTH:s_v], w_in0[:, :ATTN_WIDTH] * scale], axis=1)
    wf = jnp.pad(w_in0[:, s_v:s_f], ((0, 0), (0, LANES - ATTN_HEADS)))

    h1, q, kt, v, s_in, ct, k_norm2 = _head_call(
        x, row(g_ffn1[0]), bf(w1_a[0]), bf(w3_a[0]), bf(w2_a[0]), row(g_mix[0]),
        bf(wqkv), bf(w_in0[:, s_f:]), bf(wf), b_f[0].reshape(ATTN_HEADS, 1).astype(F32))

    attn = _attn_call(q, kt, v, ct, k_norm2)

    T = SSM_CHUNK
    chunks_per_seq = L // T
    n_chunks = B * chunks_per_seq
    e = s_in.reshape(n_chunks, T, SSM_WIDTH).transpose(1, 2, 0)
    f = _ssm_call(*_ssm_param_layouts(a_re[0], a_im[0], log_dt[0], b_re[0], b_im[0],
                                      c_re[0], c_im[0], d_skip[0]),
                  e, chunks_per_seq)
    y = f.transpose(2, 0, 1).reshape(B, L, SSM_WIDTH)

    w_out0 = w_out[0]
    return _tail_call(
        h1, attn, y, p[0],
        bf(w_glu[0]), row(b_glu[0]), row(g_attn_out[0]), row(g_ssm_out[0]),
        bf(w_out0[:ATTN_WIDTH]), bf(w_out0[ATTN_WIDTH:]), row(g_ffn2[0]),
        bf(w1_b[0]), bf(w3_b[0]), bf(w2_b[0]), row(g_ple[0]), bf(w_ple_gate[0]),
        bf(w_ple_proj[0]), row(g_final))
```

```python
import functools
import math

import jax
import jax.numpy as jnp
from jax import lax
from jax.experimental import pallas as pl
from jax.experimental.pallas import tpu as pltpu

D_MODEL = 1024
ATTN_HEADS = 8
HEAD_DIM = 64
ATTN_WIDTH = ATTN_HEADS * HEAD_DIM
SSM_WIDTH = D_MODEL - ATTN_WIDTH
SSM_GROUP_CH = 16
SSM_GROUPS = SSM_WIDTH // SSM_GROUP_CH
SSM_STATE = 64
D_FF = 2816
PLE_DIM = 256
EPS = 1e-6

LANES = 128
HEAD_PAIR = 2 * HEAD_DIM
N_PAIRS = ATTN_HEADS // 2
FF_CHUNK = 256
TOKEN_TILE = 512
ATTN_TILE = 512
SSM_CHUNK = 32
SSM_GROUPS_PER_STEP = 2
NEG_BIG = -1e30
SKIP_LOG2 = 140.0
BOUND_SLACK_MUL = 1.001
BOUND_SLACK_ADD = 1.0
LOG2E = math.log2(math.e)
N_BIAS = 3
BIAS_ROWS = 8
VMEM_LIMIT = 56 * 1024 * 1024

BF16 = jnp.bfloat16
F32 = jnp.float32


def _rms(x, g):
    ms = jnp.mean(x * x, axis=-1, keepdims=True)
    return x * lax.rsqrt(ms + EPS) * g


def _sigmoid(x):
    return 1.0 / (1.0 + jnp.exp(-x))


def _swiglu(xn, w1_ref, w3_ref, w2_ref):
    acc = None
    for c in range(D_FF // FF_CHUNK):
        sl = slice(c * FF_CHUNK, (c + 1) * FF_CHUNK)
        a = jnp.dot(xn, w1_ref[:, sl], preferred_element_type=F32)
        b = jnp.dot(xn, w3_ref[:, sl], preferred_element_type=F32)
        gated = (a * _sigmoid(a) * b).astype(BF16)
        part = jnp.dot(gated, w2_ref[sl, :], preferred_element_type=F32)
        acc = part if acc is None else acc + part
    return acc


def _const_spec(shape):
    nd = len(shape)
    return pl.BlockSpec(shape, lambda *_: (0,) * nd, pipeline_mode=pl.Buffered(1))


def _head_kernel(x_ref, g1_ref, w1_ref, w3_ref, w2_ref, gm_ref, wkvq_ref, ws_ref,
                 wf_ref, bf_ref, h1_ref, q_ref, kt_ref, v_ref, s_ref, ct_ref, kn_ref,
                 carry_ref):
    tm = x_ref.shape[0]
    x = x_ref[...]
    h1 = x + 0.5 * _swiglu(_rms(x, g1_ref[...]).astype(BF16), w1_ref, w3_ref, w2_ref)
    h1_ref[...] = h1
    un = _rms(h1, gm_ref[...]).astype(BF16)
    project = lambda w: jnp.dot(un, w, preferred_element_type=F32)
    zf = project(wf_ref[...])
    kv = project(wkvq_ref[:, :2 * ATTN_WIDTH])
    zft = zf.T[:ATTN_HEADS, :] + bf_ref[...]
    logf = jnp.minimum(zft, 0.0) - jnp.log1p(jnp.exp(-jnp.abs(zft)))
    lane = lax.broadcasted_iota(jnp.int32, logf.shape, 1)
    c = logf
    shift = 1
    while shift < tm:
        c = c + jnp.where(lane >= shift, pltpu.roll(c, shift, 1), 0.0)
        shift *= 2

    @pl.when(pl.program_id(1) == 0)
    def _():
        carry_ref[...] = jnp.zeros_like(carry_ref)

    c_abs = c + carry_ref[:, 0:1]
    ct_ref[...] = c_abs * LOG2E
    carry_ref[...] = jnp.broadcast_to(c_abs[:, tm - 1:tm], carry_ref.shape)

    rel = (c - c[:, 0:1]) * LOG2E
    hi = rel.astype(BF16).astype(F32)
    mid = (rel - hi).astype(BF16).astype(F32)
    lo = (rel - hi - mid).astype(BF16).astype(F32)
    kt = kv[:, :ATTN_WIDTH].astype(BF16).astype(F32).T
    vv = kv[:, ATTN_WIDTH:]
    k_sq = (kt * kt).reshape(ATTN_HEADS, HEAD_DIM, tm)
    kn_ref[...] = jnp.broadcast_to(
        jnp.max(jnp.sum(k_sq, axis=1), axis=-1, keepdims=True), kn_ref.shape)
    sub = lax.broadcasted_iota(jnp.int32, (BIAS_ROWS, tm), 0)
    zeros = jnp.zeros((HEAD_DIM - BIAS_ROWS, tm), F32)
    vlane = lax.broadcasted_iota(jnp.int32, (tm, HEAD_PAIR), 1)
    for h in range(ATTN_HEADS):
        bias = jnp.where(sub == 0, -hi[h:h + 1],
                         jnp.where(sub == 1, -mid[h:h + 1],
                                   jnp.where(sub == 2, -lo[h:h + 1], 0.0)))
        k_h = kt[h * HEAD_DIM:(h + 1) * HEAD_DIM]
        vp = vv[:, (h // 2) * HEAD_PAIR:(h // 2 + 1) * HEAD_PAIR]
        if h % 2 == 0:
            kt_ref[h] = jnp.concatenate([k_h, bias, zeros], axis=0).astype(BF16)
            v_ref[h] = jnp.where(vlane < HEAD_DIM, vp,
                                 jnp.where(vlane == HEAD_DIM, 1.0, 0.0)).astype(BF16)
        else:
            kt_ref[h] = jnp.concatenate([bias, zeros, k_h], axis=0).astype(BF16)
            v_ref[h] = jnp.where(vlane >= HEAD_DIM, vp,
                                 jnp.where(vlane == 0, 1.0, 0.0)).astype(BF16)

    q_ref[...] = project(wkvq_ref[:, 2 * ATTN_WIDTH:]).astype(BF16)
    s_ref[...] = project(ws_ref[...]).astype(BF16)


def _head_call(x, g1, w1, w3, w2, gm, wqkv, ws, wf, bf):
    B, L, D = x.shape
    tm = TOKEN_TILE
    tile = lambda w: pl.BlockSpec((None, tm, w), lambda b, i: (b, i, 0))
    out_shape = (
        jax.ShapeDtypeStruct((B, L, D), F32),
        jax.ShapeDtypeStruct((B, L, ATTN_WIDTH), BF16),
        jax.ShapeDtypeStruct((B, ATTN_HEADS, HEAD_PAIR, L), BF16),
        jax.ShapeDtypeStruct((B, ATTN_HEADS, L, HEAD_PAIR), BF16),
        jax.ShapeDtypeStruct((B, L, SSM_WIDTH), BF16),
        jax.ShapeDtypeStruct((B, ATTN_HEADS, L), F32),
        jax.ShapeDtypeStruct((B, L // tm, ATTN_HEADS, LANES), F32),
    )
    return pl.pallas_call(
        _head_kernel,
        grid=(B, L // tm),
        in_specs=[tile(D), _const_spec(g1.shape), _const_spec(w1.shape),
                  _const_spec(w3.shape), _const_spec(w2.shape), _const_spec(gm.shape),
                  _const_spec(wqkv.shape), _const_spec(ws.shape), _const_spec(wf.shape),
                  _const_spec(bf.shape)],
        out_specs=(tile(D), tile(ATTN_WIDTH),
                   pl.BlockSpec((None, ATTN_HEADS, HEAD_PAIR, tm), lambda b, i: (b, 0, 0, i)),
                   pl.BlockSpec((None, ATTN_HEADS, tm, HEAD_PAIR), lambda b, i: (b, 0, i, 0)),
                   tile(SSM_WIDTH),
                   pl.BlockSpec((None, ATTN_HEADS, tm), lambda b, i: (b, 0, i)),
                   pl.BlockSpec((None, None, ATTN_HEADS, LANES), lambda b, i: (b, i, 0, 0))),
        out_shape=out_shape,
        scratch_shapes=[pltpu.VMEM((ATTN_HEADS, LANES), F32)],
        compiler_params=pltpu.CompilerParams(
            dimension_semantics=("arbitrary", "arbitrary"),
            vmem_limit_bytes=VMEM_LIMIT),
        name="head",
    )(x, g1, w1, w3, w2, gm, wqkv, ws, wf, bf)


def _attn_kernel(q_ref, *refs):
    _attn_prepare(0, q_ref, *refs)

    def q_tile(n, carry):
        _attn_q_tile(n, q_ref, *refs)
        return carry

    lax.fori_loop(0, q_ref.shape[0] // ATTN_TILE, q_tile, 0)


def _tile_start(i):
    return pl.multiple_of(i * ATTN_TILE, ATTN_TILE)


def _q_heads(q):
    lane = lax.broadcasted_iota(jnp.int32, (1, HEAD_PAIR), 1)
    first = lane < HEAD_DIM
    ones_even = jnp.where((lane >= HEAD_DIM) & (lane < HEAD_DIM + N_BIAS), 1.0, 0.0).astype(BF16)
    ones_odd = jnp.where(lane < N_BIAS, 1.0, 0.0).astype(BF16)
    return first, (jnp.where(first, q, ones_even), jnp.where(first, ones_odd, q))


def _attn_prepare(n, q_ref, kt_ref, v_ref, c_ref, st_ref, o_ref, s_e0, s_e1, s_o0, s_o1, m_ref,
                  acc_ref, reach_ref):
    t = ATTN_TILE
    q = q_ref[pl.ds(_tile_start(n), t), :]
    first, q_heads = _q_heads(q)
    q_sq = q.astype(F32) * q.astype(F32)
    for h, s_ref in enumerate((s_o0, s_o1)):
        s_ref[...] = jnp.dot(q_heads[h], kt_ref[h, :, pl.ds(_tile_start(n), t)],
                             preferred_element_type=F32)
        q_norm2 = jnp.max(jnp.sum(jnp.where(first == (h == 0), q_sq, 0.0), axis=-1, keepdims=True),
                          axis=0, keepdims=True)
        k_norm2, c_next = st_ref[h:h + 1, :], st_ref[2 + h:3 + h, :]
        c_q = c_ref[h:h + 1, pl.ds(_tile_start(n), LANES)][:, 0:1]
        reach_ref[h:h + 1, :] = (jnp.sqrt(q_norm2 * k_norm2) * BOUND_SLACK_MUL + BOUND_SLACK_ADD
                                 + c_q - c_next)


def _attn_q_tile(n, q_ref, kt_ref, v_ref, c_ref, st_ref, o_ref, s_e0, s_e1, s_o0, s_o1, m_ref,
                 acc_ref, reach_ref):
    t = ATTN_TILE
    start = _tile_start
    first, q_heads = _q_heads(q_ref[pl.ds(start(n), t), :])
    s_buf = ((s_e0, s_e1), (s_o0, s_o1))
    c_q = [c_ref[h:h + 1, pl.ds(start(n), LANES)][:, 0:1] for h in range(2)]

    def qk(h, kv):
        return jnp.dot(q_heads[h], kt_ref[h, :, pl.ds(start(kv), t)], preferred_element_type=F32)

    def consume(h, s, kv):
        d = c_ref[h:h + 1, pl.ds(start(kv), LANES)][:, 0:1] - c_q[h]
        m_old = m_ref[h]
        m_new = jnp.maximum(m_old, jnp.max(s, axis=-1, keepdims=True) - d)
        p = jnp.exp2(s - (m_new + d)).astype(BF16)
        m_ref[h] = m_new
        acc_ref[h] = jnp.exp2(m_old - m_new) * acc_ref[h] + jnp.dot(
            p, v_ref[h, pl.ds(start(kv), t), :], preferred_element_type=F32)

    def step(kv, par):
        for h in range(2):
            s_buf[1 - par][h][...] = qk(h, kv - 1)
        for h in range(2):
            consume(h, s_buf[par][h][...], kv)

    m_ref[...] = jnp.full(m_ref.shape, NEG_BIG, F32)
    acc_ref[...] = jnp.zeros(acc_ref.shape, F32)

    @pl.when(n >= 0)
    def _():
        row = lax.broadcasted_iota(jnp.int32, (t, t), 0)
        col = lax.broadcasted_iota(jnp.int32, (t, t), 1)
        for h in range(2):
            s_buf[0][h][...] = qk(h, jnp.maximum(n - 1, 0))
        for h in range(2):
            consume(h, jnp.where(col <= row, s_buf[1][h][...], NEG_BIG), n)

    tile_id = lax.broadcasted_iota(jnp.int32, (1, LANES), 1)
    needed = tile_id < 0
    for h in range(2):
        m_low = jnp.min(m_ref[h], axis=0, keepdims=True)
        needed = needed | (reach_ref[h:h + 1, :] - m_low >= -SKIP_LOG2)
    first_needed = jnp.min(jnp.where(needed & (tile_id < n), tile_id, n).astype(F32))
    count = n - first_needed.astype(jnp.int32)

    def pair(i, carry):
        step(n - 1 - 2 * i, 0)
        step(n - 2 - 2 * i, 1)
        return carry

    n_pairs = jnp.maximum(count - 1, 0) // 2
    lax.fori_loop(0, n_pairs, pair, 0)
    left = count - 2 * n_pairs
    last = n - count

    @pl.when(left == 2)
    def _():
        step(last + 1, 0)
        for h in range(2):
            consume(h, s_buf[1][h][...], last)

    @pl.when(left == 1)
    def _():
        for h in range(2):
            consume(h, s_buf[0][h][...], last)

    acc0, acc1 = acc_ref[0], acc_ref[1]
    o_ref[pl.ds(start(n), t), :] = jnp.where(first, acc0 / acc0[:, HEAD_DIM:HEAD_DIM + 1],
                                             acc1 / acc1[:, 0:1])
    n_tiles = q_ref.shape[0] // t
    _attn_prepare(jnp.minimum(n + 1, n_tiles - 1), q_ref, kt_ref, v_ref, c_ref, st_ref, o_ref,
                  s_e0, s_e1, s_o0, s_o1, m_ref, acc_ref, reach_ref)


def _attn_call(q, kt, v, ct, k_norm2):
    B, L, _ = q.shape
    t = ATTN_TILE
    n_tiles = L // t
    assert n_tiles <= LANES
    c4 = ct.reshape(B, N_PAIRS, 2, L)
    c_next = jnp.roll(ct[:, :, ::t], -1, axis=-1)
    per_tile = jnp.stack([k_norm2[..., 0].transpose(0, 2, 1), c_next], axis=1)
    per_tile = jnp.pad(per_tile, ((0, 0),) * 3 + ((0, LANES - n_tiles),))
    stats = (per_tile.reshape(B, 2, N_PAIRS, 2, LANES).transpose(0, 2, 1, 3, 4)
             .reshape(B, N_PAIRS, 4, LANES))
    pair_block = lambda *shape: pl.BlockSpec((None, 2) + shape, lambda b, p: (b, p, 0, 0))
    lanes_of_pair = pl.BlockSpec((None, L, HEAD_PAIR), lambda b, p: (b, 0, p))
    return pl.pallas_call(
        _attn_kernel,
        grid=(B, N_PAIRS),
        in_specs=[lanes_of_pair, pair_block(HEAD_PAIR, L), pair_block(L, HEAD_PAIR),
                  pl.BlockSpec((None, None, 2, L), lambda b, p: (b, p, 0, 0)),
                  pl.BlockSpec((None, None, 4, LANES), lambda b, p: (b, p, 0, 0))],
        out_specs=lanes_of_pair,
        out_shape=jax.ShapeDtypeStruct((B, L, ATTN_WIDTH), F32),
        scratch_shapes=[pltpu.VMEM((t, t), F32)] * 4
                       + [pltpu.VMEM((2, t, 1), F32), pltpu.VMEM((2, t, HEAD_PAIR), F32),
                          pltpu.VMEM((2, LANES), F32)],
        compiler_params=pltpu.CompilerParams(
            dimension_semantics=("arbitrary", "arbitrary"),
            vmem_limit_bytes=VMEM_LIMIT),
        name="attn",
    )(q, kt, v, c4, stats)


def _gelu_tanh(x):
    return 0.5 * x * (1.0 + jnp.tanh(math.sqrt(2.0 / math.pi) * (x + 0.044715 * (x * x * x))))


def _cis(mag_arg, ang):
    mag = jnp.exp(mag_arg)
    return mag * jnp.cos(ang), mag * jnp.sin(ang)


def _cmul(ar, ai, br, bi):
    return ar * br - ai * bi, ar * bi + ai * br


def _cpow2(zr, zi, n):
    assert n & (n - 1) == 0
    while n > 1:
        zr, zi = zr * zr - zi * zi, 2.0 * zr * zi
        n //= 2
    return zr, zi


def _ssm_kernel(arow_ref, acol_ref, ldt_ref, bt_ref, cab_ref, dcol_ref, e_ref, f_ref, z_ref,
                *, chunks_per_seq):
    H = SSM_GROUP_CH

    @pl.when(pl.program_id(0) == 0)
    def _():
        z_ref[:, 0:SSM_CHUNK * H, :] = jnp.zeros((z_ref.shape[0], SSM_CHUNK * H, LANES), F32)

    for gi in range(SSM_GROUPS_PER_STEP):
        y = _ssm_group(arow_ref.at[gi], acol_ref.at[gi], ldt_ref.at[gi], bt_ref.at[gi], cab_ref.at[gi],
                       dcol_ref.at[gi], e_ref[:, gi * H:(gi + 1) * H, :], z_ref.at[gi], chunks_per_seq)
        f_ref[:, gi * H:(gi + 1) * H, :] = y.reshape(SSM_CHUNK, H, y.shape[-1])


def _ssm_group(arow_ref, acol_ref, ldt_ref, bt_ref, cab_ref, dcol_ref, e, z_ref, chunks_per_seq):
    T, P, H = SSM_CHUNK, SSM_STATE, SSM_GROUP_CH
    TH = T * H
    hi = lax.Precision.HIGHEST
    dt = jnp.exp(ldt_ref[...])

    lam_r, lam_i = dt * arow_ref[0:1, :], dt * arow_ref[1:2, :]
    j0 = lax.broadcasted_iota(jnp.int32, (T, 2 * P), 0).astype(F32)
    pa0, pb0 = _cis(j0 * lam_r, j0 * lam_i)
    pa1, pb1 = _cmul(pa0, pb0, *_cis(lam_r, lam_i))
    over_h = lambda a: jnp.concatenate(
        [jnp.broadcast_to(a[j:j + 1, :], (H, 2 * P)) for j in range(T)], axis=0)
    ca, cb = jnp.tile(cab_ref[0], (T, 1)), jnp.tile(cab_ref[1], (T, 1))
    c_pow0 = over_h(pa0) * ca + over_h(pb0) * cb
    c_pow1 = over_h(pa1) * ca + over_h(pb1) * cb

    a_r, a_i = acol_ref[:, 0:1], acol_ref[:, 1:2]
    lr, li = dt * a_r, dt * a_i
    abar_r, abar_i = _cis(lr, li)
    nr, ni = abar_r - 1.0, abar_i
    den = a_r * a_r + a_i * a_i
    fr, fi = (nr * a_r + ni * a_i) / den, (ni * a_r - nr * a_i) / den
    b_r, b_i = bt_ref[0], bt_ref[1]
    bb_r, bb_i = fr * b_r - fi * b_i, fr * b_i + fi * b_r

    kcol = jnp.dot(c_pow0, jnp.concatenate([bb_r, bb_i], axis=0), precision=hi,
                   preferred_element_type=F32)
    lane_h = lax.broadcasted_iota(jnp.int32, (H, LANES), 1) % H
    skip = jnp.where(lane_h == lax.broadcasted_iota(jnp.int32, (H, LANES), 0), dcol_ref[...], 0.0)

    z_ref[TH:2 * TH, :] = kcol
    z_ref[TH:TH + H, :] = kcol[:H] + skip
    lane_group = lax.broadcasted_iota(jnp.int32, (1, LANES), 1) // H
    groups_per_block = LANES // H
    blocks = []
    for v in range(TH // LANES):
        blk = None
        for u in range(groups_per_block):
            s = v * groups_per_block + u
            piece = z_ref[TH - H * s:2 * TH - H * s, :]
            blk = piece if blk is None else jnp.where(lane_group == u, piece, blk)
        blocks.append(blk.astype(BF16))
    mt = jnp.concatenate(blocks, axis=1)

    expo = (groups_per_block - 1 - lane_group).astype(F32)
    wr, wi = _cis(lr * expo, li * expo)
    hop_r, hop_i = _cpow2(abar_r, abar_i, groups_per_block)
    w1_r, w1_i = [], []
    for v in range(TH // LANES):
        w1_r.insert(0, wr * bb_r - wi * bb_i)
        w1_i.insert(0, wr * bb_i + wi * bb_r)
        wr, wi = _cmul(wr, wi, hop_r, hop_i)
    w1t = jnp.concatenate([jnp.concatenate(w1_r, axis=1),
                           jnp.concatenate(w1_i, axis=1)], axis=0).astype(BF16)

    e = e.reshape(TH, e.shape[-1])
    y = jnp.dot(mt, e, preferred_element_type=F32)
    st = jnp.dot(w1t, e, preferred_element_type=F32)
    sr, si = st[:P], st[P:]
    pos = lax.broadcasted_iota(jnp.int32, sr.shape, 1) % chunks_per_seq

    def shifted(a, shift):
        return jnp.where(pos >= shift, pltpu.roll(a, shift, 1), 0.0)

    qr, qi = _cpow2(abar_r, abar_i, T)
    shift = 1
    while shift < chunks_per_seq:
        srs, sis = shifted(sr, shift), shifted(si, shift)
        sr, si = sr + qr * srs - qi * sis, si + qr * sis + qi * srs
        qr, qi = qr * qr - qi * qi, 2.0 * qr * qi
        shift *= 2
    x_prev = jnp.concatenate([shifted(sr, 1), shifted(si, 1)], axis=0).astype(BF16)
    y = y + jnp.dot(c_pow1.astype(BF16), x_prev, preferred_element_type=F32)
    return _gelu_tanh(y)


def _ssm_call(arow, acol, ldt, bt, cab, dcol, e, chunks_per_seq):
    T, _, NC = e.shape
    G = arow.shape[0]
    gps = SSM_GROUPS_PER_STEP
    assert G % gps == 0
    grp = lambda a: pl.BlockSpec((gps,) + a.shape[1:], lambda g: (g,) + (0,) * (a.ndim - 1))
    channels = pl.BlockSpec((T, gps * SSM_GROUP_CH, NC), lambda g: (0, g, 0))
    return pl.pallas_call(
        functools.partial(_ssm_kernel, chunks_per_seq=chunks_per_seq),
        grid=(G // gps,),
        in_specs=[grp(a) for a in (arow, acol, ldt, bt, cab, dcol)] + [channels],
        out_specs=channels,
        out_shape=jax.ShapeDtypeStruct((T, SSM_WIDTH, NC), F32),
        scratch_shapes=[pltpu.VMEM((gps, 2 * T * SSM_GROUP_CH, LANES), F32)],
        compiler_params=pltpu.CompilerParams(
            dimension_semantics=("arbitrary",), vmem_limit_bytes=VMEM_LIMIT),
        name="ssm",
    )(arow, acol, ldt, bt, cab, dcol, e)


def _ssm_param_layouts(a_re, a_im, log_dt, b_re, b_im, c_re, c_im, d_skip):
    G = a_re.shape[0]
    arow = jnp.stack([jnp.concatenate([a_re, a_re], -1), jnp.concatenate([a_im, a_im], -1)], 1)
    acol = jnp.stack([a_re, a_im], -1)
    reps = LANES // SSM_GROUP_CH
    bt = jnp.stack([jnp.tile(b_re, (1, 1, reps)), jnp.tile(b_im, (1, 1, reps))], 1)
    cab = jnp.stack([jnp.concatenate([c_re, -c_im], -1), jnp.concatenate([-c_im, -c_re], -1)], 1)
    return (arow.astype(F32), acol.astype(F32), log_dt.reshape(G, 1, 1).astype(F32),
            bt.astype(F32), cab.astype(F32), d_skip.reshape(G, SSM_GROUP_CH, 1).astype(F32))


def _tail_kernel(h1_ref, attn_ref, y_ref, p_ref, wglu_ref, bglu_ref, ga_ref, gs_ref,
                 woa_ref, wos_ref, g2_ref, w1_ref, w3_ref, w2_ref, gp_ref, wpg_ref,
                 wpp_ref, gf_ref, o_ref):
    y = y_ref[...]
    glu = y * _sigmoid(jnp.dot(y.astype(BF16), wglu_ref[...], preferred_element_type=F32)
                       + bglu_ref[...])
    an = _rms(attn_ref[...], ga_ref[...]).astype(BF16)
    sn = _rms(glu, gs_ref[...]).astype(BF16)
    h = (h1_ref[...] + jnp.dot(an, woa_ref[...], preferred_element_type=F32)
         + jnp.dot(sn, wos_ref[...], preferred_element_type=F32))
    h = h + 0.5 * _swiglu(_rms(h, g2_ref[...]).astype(BF16), w1_ref, w3_ref, w2_ref)
    gate = _sigmoid(jnp.dot(_rms(h, gp_ref[...]).astype(BF16), wpg_ref[...],
                            preferred_element_type=F32))
    h = h + gate * jnp.dot(p_ref[...].astype(BF16), wpp_ref[...], preferred_element_type=F32)
    o_ref[...] = _rms(h, gf_ref[...])


def _tail_call(h1, attn, y, p, *consts):
    B, L, D = h1.shape
    tm = TOKEN_TILE
    tile = lambda w: pl.BlockSpec((None, tm, w), lambda b, i: (b, i, 0))
    return pl.pallas_call(
        _tail_kernel,
        grid=(B, L // tm),
        in_specs=[tile(D), tile(ATTN_WIDTH), tile(SSM_WIDTH), tile(PLE_DIM)]
                 + [_const_spec(c.shape) for c in consts],
        out_specs=tile(D),
        out_shape=jax.ShapeDtypeStruct((B, L, D), F32),
        compiler_params=pltpu.CompilerParams(
            dimension_semantics=("arbitrary", "arbitrary"),
            vmem_limit_bytes=VMEM_LIMIT,
            allow_input_fusion=[False] * 4 + [c.dtype == BF16 for c in consts]),
        name="tail",
    )(h1, attn, y, p, *consts)


def kernel(x, p, g_ffn1, w1_a, w3_a, w2_a, g_mix, w_in, b_f, a_re, a_im, log_dt, b_re, b_im, c_re, c_im, d_skip, w_glu, b_glu, g_attn_out, g_ssm_out, w_out, g_ffn2, w1_b, w3_b, w2_b, g_ple, w_ple_gate, w_ple_proj, g_final):
    B, L, D = x.shape
    assert D == D_MODEL and L % ATTN_TILE == 0 and L % TOKEN_TILE == 0 and L % SSM_CHUNK == 0
    assert g_ffn1.shape[0] == 1, "single layer"
    assert TOKEN_TILE == ATTN_TILE, "decay bias rows are relative to the kv tile start"
    row = lambda g: g.reshape(1, -1).astype(F32)
    bf = lambda w: w.astype(BF16)
    s_v, s_f = 3 * ATTN_WIDTH, 3 * ATTN_WIDTH + ATTN_HEADS
    w_in0 = w_in[0]
    scale = LOG2E / math.sqrt(HEAD_DIM)
    wqkv = jnp.concatenate([w_in0[:, ATTN_WIDTH:s_v], w_in0[:, :ATTN_WIDTH] * scale], axis=1)
    wf = jnp.pad(w_in0[:, s_v:s_f], ((0, 0), (0, LANES - ATTN_HEADS)))

    h1, q, kt, v, s_in, ct, k_norm2 = _head_call(
        x, row(g_ffn1[0]), bf(w1_a[0]), bf(w3_a[0]), bf(w2_a[0]), row(g_mix[0]),
        bf(wqkv), bf(w_in0[:, s_f:]), bf(wf), b_f[0].reshape(ATTN_HEADS, 1).astype(F32))

    attn = _attn_call(q, kt, v, ct, k_norm2)

    T = SSM_CHUNK
    chunks_per_seq = L // T
    n_chunks = B * chunks_per_seq
    e = s_in.reshape(n_chunks, T, SSM_WIDTH).transpose(1, 2, 0)
    f = _ssm_call(*_ssm_param_layouts(a_re[0], a_im[0], log_dt[0], b_re[0], b_im[0],
                                      c_re[0], c_im[0], d_skip[0]),
                  e, chunks_per_seq)
    y = f.transpose(2, 0, 1).reshape(B, L, SSM_WIDTH)

    w_out0 = w_out[0]
    return _tail_call(
        h1, attn, y, p[0],
        bf(w_glu[0]), row(b_glu[0]), row(g_attn_out[0]), row(g_ssm_out[0]),
        bf(w_out0[:ATTN_WIDTH]), bf(w_out0[ATTN_WIDTH:]), row(g_ffn2[0]),
        bf(w1_b[0]), bf(w3_b[0]), bf(w2_b[0]), row(g_ple[0]), bf(w_ple_gate[0]),
        bf(w_ple_proj[0]), row(g_final))
```

```python
import functools
import math

import jax
import jax.numpy as jnp
from jax import lax
from jax.experimental import pallas as pl
from jax.experimental.pallas import tpu as pltpu

D_MODEL = 1024
ATTN_HEADS = 8
HEAD_DIM = 64
ATTN_WIDTH = ATTN_HEADS * HEAD_DIM
SSM_WIDTH = D_MODEL - ATTN_WIDTH
SSM_GROUP_CH = 16
SSM_GROUPS = SSM_WIDTH // SSM_GROUP_CH
SSM_STATE = 64
D_FF = 2816
PLE_DIM = 256
EPS = 1e-6

LANES = 128
BF16_SUBLANES = 16
HEAD_PAIR = 2 * HEAD_DIM
N_PAIRS = ATTN_HEADS // 2
FF_CHUNK = 256
TOKEN_TILE = 512
ATTN_TILE = 512
SSM_CHUNK = 32
SSM_GROUPS_PER_STEP = 2
NEG_BIG = -1e30
SKIP_LOG2 = 140.0
BOUND_SLACK_MUL = 1.001
BOUND_SLACK_ADD = 1.0
LOG2E = math.log2(math.e)
N_BIAS = 3
BIAS_ROWS = 8
VMEM_LIMIT = 56 * 1024 * 1024

BF16 = jnp.bfloat16
F32 = jnp.float32


def _rms(x, g):
    ms = jnp.mean(x * x, axis=-1, keepdims=True)
    return x * lax.rsqrt(ms + EPS) * g


def _sigmoid(x):
    return 1.0 / (1.0 + jnp.exp(-x))


def _swiglu(xn, w1_ref, w3_ref, w2_ref):
    acc = None
    for c in range(D_FF // FF_CHUNK):
        sl = slice(c * FF_CHUNK, (c + 1) * FF_CHUNK)
        a = jnp.dot(xn, w1_ref[:, sl], preferred_element_type=F32)
        b = jnp.dot(xn, w3_ref[:, sl], preferred_element_type=F32)
        gated = (a * _sigmoid(a) * b).astype(BF16)
        part = jnp.dot(gated, w2_ref[sl, :], preferred_element_type=F32)
        acc = part if acc is None else acc + part
    return acc


def _const_spec(shape):
    nd = len(shape)
    return pl.BlockSpec(shape, lambda *_: (0,) * nd, pipeline_mode=pl.Buffered(1))


def _head_kernel(*refs, n_later):
    (x_ref, g1_ref, w1_ref, w3_ref, w2_ref, gm_ref, wkvq_ref, ws_ref, wf_ref, bf_ref) = refs[:10]
    later_in = refs[10:10 + n_later]
    h1_ref, q_ref, kt_ref, v_ref, s_ref, ct_ref, kn_ref = refs[10 + n_later:17 + n_later]
    later_out = refs[17 + n_later:17 + 2 * n_later]
    carry_ref = refs[-1]
    for src, dst in zip(later_in, later_out):
        dst[...] = src[...].astype(BF16)
    tm = x_ref.shape[0]
    x = x_ref[...]
    h1 = x + 0.5 * _swiglu(_rms(x, g1_ref[...]).astype(BF16), w1_ref, w3_ref, w2_ref)
    h1_ref[...] = h1
    un = _rms(h1, gm_ref[...]).astype(BF16)
    project = lambda w: jnp.dot(un, w, preferred_element_type=F32)
    zf = project(wf_ref[...])
    kv = project(wkvq_ref[:, :2 * ATTN_WIDTH])
    zft = zf.T[:ATTN_HEADS, :] + bf_ref[...]
    logf = jnp.minimum(zft, 0.0) - jnp.log1p(jnp.exp(-jnp.abs(zft)))
    lane = lax.broadcasted_iota(jnp.int32, logf.shape, 1)
    c = logf
    shift = 1
    while shift < tm:
        c = c + jnp.where(lane >= shift, pltpu.roll(c, shift, 1), 0.0)
        shift *= 2

    @pl.when(pl.program_id(1) == 0)
    def _():
        carry_ref[...] = jnp.zeros_like(carry_ref)

    c_abs = c + carry_ref[:, 0:1]
    ct_ref[...] = c_abs * LOG2E
    carry_ref[...] = jnp.broadcast_to(c_abs[:, tm - 1:tm], carry_ref.shape)

    rel = (c - c[:, 0:1]) * LOG2E
    hi = rel.astype(BF16).astype(F32)
    mid = (rel - hi).astype(BF16).astype(F32)
    lo = (rel - hi - mid).astype(BF16).astype(F32)
    kt = kv[:, :ATTN_WIDTH].astype(BF16).astype(F32).T
    vv = kv[:, ATTN_WIDTH:]
    k_sq = (kt * kt).reshape(ATTN_HEADS, HEAD_DIM, tm)
    kn_ref[...] = jnp.broadcast_to(
        jnp.max(jnp.sum(k_sq, axis=1), axis=-1, keepdims=True), kn_ref.shape)
    sub = lax.broadcasted_iota(jnp.int32, (BIAS_ROWS, tm), 0)
    zeros = jnp.zeros((HEAD_DIM - BIAS_ROWS, tm), F32)
    vlane = lax.broadcasted_iota(jnp.int32, (tm, HEAD_PAIR), 1)
    for h in range(ATTN_HEADS):
        bias = jnp.where(sub == 0, -hi[h:h + 1],
                         jnp.where(sub == 1, -mid[h:h + 1],
                                   jnp.where(sub == 2, -lo[h:h + 1], 0.0)))
        k_h = kt[h * HEAD_DIM:(h + 1) * HEAD_DIM]
        vp = vv[:, (h // 2) * HEAD_PAIR:(h // 2 + 1) * HEAD_PAIR]
        if h % 2 == 0:
            kt_ref[h] = jnp.concatenate([k_h, bias, zeros], axis=0).astype(BF16)
            v_ref[h] = jnp.where(vlane < HEAD_DIM, vp,
                                 jnp.where(vlane == HEAD_DIM, 1.0, 0.0)).astype(BF16)
        else:
            kt_ref[h] = jnp.concatenate([bias, zeros, k_h], axis=0).astype(BF16)
            v_ref[h] = jnp.where(vlane >= HEAD_DIM, vp,
                                 jnp.where(vlane == 0, 1.0, 0.0)).astype(BF16)

    q_ref[...] = project(wkvq_ref[:, 2 * ATTN_WIDTH:]).astype(BF16)
    s_ref[...] = project(ws_ref[...]).astype(BF16)


def _slab_spec(w, n_steps, steps_per_batch):
    rows = next(r for r in range(BF16_SUBLANES, w.shape[0] + 1, BF16_SUBLANES)
                if w.shape[0] % r == 0 and w.shape[0] // r <= n_steps)
    last = w.shape[0] // rows - 1
    return pl.BlockSpec((rows, w.shape[1]),
                        lambda b, i: (jnp.minimum(b * steps_per_batch + i, last), 0))


def _head_call(x, g1, w1, w3, w2, gm, wqkv, ws, wf, bf, later):
    B, L, D = x.shape
    tm = TOKEN_TILE
    tile = lambda w: pl.BlockSpec((None, tm, w), lambda b, i: (b, i, 0))
    slabs = [_slab_spec(w, B * (L // tm), L // tm) for w in later]
    out_shape = (
        jax.ShapeDtypeStruct((B, L, D), F32),
        jax.ShapeDtypeStruct((B, L, ATTN_WIDTH), BF16),
        jax.ShapeDtypeStruct((B, ATTN_HEADS, HEAD_PAIR, L), BF16),
        jax.ShapeDtypeStruct((B, ATTN_HEADS, L, HEAD_PAIR), BF16),
        jax.ShapeDtypeStruct((B, L, SSM_WIDTH), BF16),
        jax.ShapeDtypeStruct((B, ATTN_HEADS, L), F32),
        jax.ShapeDtypeStruct((B, L // tm, ATTN_HEADS, LANES), F32),
    ) + tuple(jax.ShapeDtypeStruct(w.shape, BF16) for w in later)
    outs = pl.pallas_call(
        functools.partial(_head_kernel, n_later=len(later)),
        grid=(B, L // tm),
        in_specs=[tile(D), _const_spec(g1.shape), _const_spec(w1.shape),
                  _const_spec(w3.shape), _const_spec(w2.shape), _const_spec(gm.shape),
                  _const_spec(wqkv.shape), _const_spec(ws.shape), _const_spec(wf.shape),
                  _const_spec(bf.shape)] + slabs,
        out_specs=(tile(D), tile(ATTN_WIDTH),
                   pl.BlockSpec((None, ATTN_HEADS, HEAD_PAIR, tm), lambda b, i: (b, 0, 0, i)),
                   pl.BlockSpec((None, ATTN_HEADS, tm, HEAD_PAIR), lambda b, i: (b, 0, i, 0)),
                   tile(SSM_WIDTH),
                   pl.BlockSpec((None, ATTN_HEADS, tm), lambda b, i: (b, 0, i)),
                   pl.BlockSpec((None, None, ATTN_HEADS, LANES), lambda b, i: (b, i, 0, 0)))
                  + tuple(slabs),
        out_shape=out_shape,
        scratch_shapes=[pltpu.VMEM((ATTN_HEADS, LANES), F32)],
        compiler_params=pltpu.CompilerParams(
            dimension_semantics=("arbitrary", "arbitrary"),
            vmem_limit_bytes=VMEM_LIMIT),
        name="head",
    )(x, g1, w1, w3, w2, gm, wqkv, ws, wf, bf, *later)
    return outs[:7], outs[7:]


def _attn_kernel(q_ref, *refs):
    _attn_prepare(0, q_ref, *refs)

    def q_tile(n, carry):
        _attn_q_tile(n, q_ref, *refs)
        return carry

    lax.fori_loop(0, q_ref.shape[0] // ATTN_TILE, q_tile, 0)


def _tile_start(i):
    return pl.multiple_of(i * ATTN_TILE, ATTN_TILE)


def _q_heads(q):
    lane = lax.broadcasted_iota(jnp.int32, (1, HEAD_PAIR), 1)
    first = lane < HEAD_DIM
    ones_even = jnp.where((lane >= HEAD_DIM) & (lane < HEAD_DIM + N_BIAS), 1.0, 0.0).astype(BF16)
    ones_odd = jnp.where(lane < N_BIAS, 1.0, 0.0).astype(BF16)
    return first, (jnp.where(first, q, ones_even), jnp.where(first, ones_odd, q))


def _attn_prepare(n, q_ref, kt_ref, v_ref, c_ref, st_ref, o_ref, s_e0, s_e1, s_o0, s_o1, m_ref,
                  acc_ref, reach_ref):
    t = ATTN_TILE
    q = q_ref[pl.ds(_tile_start(n), t), :]
    first, q_heads = _q_heads(q)
    q_sq = q.astype(F32) * q.astype(F32)
    for h, s_ref in enumerate((s_o0, s_o1)):
        s_ref[...] = jnp.dot(q_heads[h], kt_ref[h, :, pl.ds(_tile_start(n), t)],
                             preferred_element_type=F32)
        q_norm2 = jnp.max(jnp.sum(jnp.where(first == (h == 0), q_sq, 0.0), axis=-1, keepdims=True),
                          axis=0, keepdims=True)
        k_norm2, c_next = st_ref[h:h + 1, :], st_ref[2 + h:3 + h, :]
        c_q = c_ref[h:h + 1, pl.ds(_tile_start(n), LANES)][:, 0:1]
        reach_ref[h:h + 1, :] = (jnp.sqrt(q_norm2 * k_norm2) * BOUND_SLACK_MUL + BOUND_SLACK_ADD
                                 + c_q - c_next)


def _attn_q_tile(n, q_ref, kt_ref, v_ref, c_ref, st_ref, o_ref, s_e0, s_e1, s_o0, s_o1, m_ref,
                 acc_ref, reach_ref):
    t = ATTN_TILE
    start = _tile_start
    first, q_heads = _q_heads(q_ref[pl.ds(start(n), t), :])
    s_buf = ((s_e0, s_e1), (s_o0, s_o1))
    c_q = [c_ref[h:h + 1, pl.ds(start(n), LANES)][:, 0:1] for h in range(2)]

    def qk(h, kv):
        return jnp.dot(q_heads[h], kt_ref[h, :, pl.ds(start(kv), t)], preferred_element_type=F32)

    def consume(h, s, kv):
        d = c_ref[h:h + 1, pl.ds(start(kv), LANES)][:, 0:1] - c_q[h]
        m_old = m_ref[h]
        m_new = jnp.maximum(m_old, jnp.max(s, axis=-1, keepdims=True) - d)
        p = jnp.exp2(s - (m_new + d)).astype(BF16)
        m_ref[h] = m_new
        acc_ref[h] = jnp.exp2(m_old - m_new) * acc_ref[h] + jnp.dot(
            p, v_ref[h, pl.ds(start(kv), t), :], preferred_element_type=F32)

    def step(kv, par):
        for h in range(2):
            s_buf[1 - par][h][...] = qk(h, kv - 1)
        for h in range(2):
            consume(h, s_buf[par][h][...], kv)

    m_ref[...] = jnp.full(m_ref.shape, NEG_BIG, F32)
    acc_ref[...] = jnp.zeros(acc_ref.shape, F32)

    @pl.when(n >= 0)
    def _():
        row = lax.broadcasted_iota(jnp.int32, (t, t), 0)
        col = lax.broadcasted_iota(jnp.int32, (t, t), 1)
        for h in range(2):
            s_buf[0][h][...] = qk(h, jnp.maximum(n - 1, 0))
        for h in range(2):
            consume(h, jnp.where(col <= row, s_buf[1][h][...], NEG_BIG), n)

    tile_id = lax.broadcasted_iota(jnp.int32, (1, LANES), 1)
    needed = tile_id < 0
    for h in range(2):
        m_low = jnp.min(m_ref[h], axis=0, keepdims=True)
        needed = needed | (reach_ref[h:h + 1, :] - m_low >= -SKIP_LOG2)
    first_needed = jnp.min(jnp.where(needed & (tile_id < n), tile_id, n).astype(F32))
    count = n - first_needed.astype(jnp.int32)

    def pair(i, carry):
        step(n - 1 - 2 * i, 0)
        step(n - 2 - 2 * i, 1)
        return carry

    n_pairs = jnp.maximum(count - 1, 0) // 2
    lax.fori_loop(0, n_pairs, pair, 0)
    left = count - 2 * n_pairs
    last = n - count

    @pl.when(left == 2)
    def _():
        step(last + 1, 0)
        for h in range(2):
            consume(h, s_buf[1][h][...], last)

    @pl.when(left == 1)
    def _():
        for h in range(2):
            consume(h, s_buf[0][h][...], last)

    acc0, acc1 = acc_ref[0], acc_ref[1]
    o_ref[pl.ds(start(n), t), :] = jnp.where(first, acc0 / acc0[:, HEAD_DIM:HEAD_DIM + 1],
                                             acc1 / acc1[:, 0:1])
    n_tiles = q_ref.shape[0] // t
    _attn_prepare(jnp.minimum(n + 1, n_tiles - 1), q_ref, kt_ref, v_ref, c_ref, st_ref, o_ref,
                  s_e0, s_e1, s_o0, s_o1, m_ref, acc_ref, reach_ref)


def _attn_call(q, kt, v, ct, k_norm2):
    B, L, _ = q.shape
    t = ATTN_TILE
    n_tiles = L // t
    assert n_tiles <= LANES
    c4 = ct.reshape(B, N_PAIRS, 2, L)
    c_next = jnp.roll(ct[:, :, ::t], -1, axis=-1)
    per_tile = jnp.stack([k_norm2[..., 0].transpose(0, 2, 1), c_next], axis=1)
    per_tile = jnp.pad(per_tile, ((0, 0),) * 3 + ((0, LANES - n_tiles),))
    stats = (per_tile.reshape(B, 2, N_PAIRS, 2, LANES).transpose(0, 2, 1, 3, 4)
             .reshape(B, N_PAIRS, 4, LANES))
    pair_block = lambda *shape: pl.BlockSpec((None, 2) + shape, lambda b, p: (b, p, 0, 0))
    lanes_of_pair = pl.BlockSpec((None, L, HEAD_PAIR), lambda b, p: (b, 0, p))
    return pl.pallas_call(
        _attn_kernel,
        grid=(B, N_PAIRS),
        in_specs=[lanes_of_pair, pair_block(HEAD_PAIR, L), pair_block(L, HEAD_PAIR),
                  pl.BlockSpec((None, None, 2, L), lambda b, p: (b, p, 0, 0)),
                  pl.BlockSpec((None, None, 4, LANES), lambda b, p: (b, p, 0, 0))],
        out_specs=lanes_of_pair,
        out_shape=jax.ShapeDtypeStruct((B, L, ATTN_WIDTH), F32),
        scratch_shapes=[pltpu.VMEM((t, t), F32)] * 4
                       + [pltpu.VMEM((2, t, 1), F32), pltpu.VMEM((2, t, HEAD_PAIR), F32),
                          pltpu.VMEM((2, LANES), F32)],
        compiler_params=pltpu.CompilerParams(
            dimension_semantics=("arbitrary", "arbitrary"),
            vmem_limit_bytes=VMEM_LIMIT),
        name="attn",
    )(q, kt, v, c4, stats)


def _gelu_tanh(x):
    return 0.5 * x * (1.0 + jnp.tanh(math.sqrt(2.0 / math.pi) * (x + 0.044715 * (x * x * x))))


def _cis(mag_arg, ang):
    mag = jnp.exp(mag_arg)
    return mag * jnp.cos(ang), mag * jnp.sin(ang)


def _cmul(ar, ai, br, bi):
    return ar * br - ai * bi, ar * bi + ai * br


def _cpow2(zr, zi, n):
    assert n & (n - 1) == 0
    while n > 1:
        zr, zi = zr * zr - zi * zi, 2.0 * zr * zi
        n //= 2
    return zr, zi


def _ssm_kernel(arow_ref, acol_ref, ldt_ref, bt_ref, cab_ref, dcol_ref, e_ref, f_ref, z_ref,
                *, chunks_per_seq):
    H = SSM_GROUP_CH

    @pl.when(pl.program_id(0) == 0)
    def _():
        z_ref[:, 0:SSM_CHUNK * H, :] = jnp.zeros((z_ref.shape[0], SSM_CHUNK * H, LANES), F32)

    for gi in range(SSM_GROUPS_PER_STEP):
        y = _ssm_group(arow_ref.at[gi], acol_ref.at[gi], ldt_ref.at[gi], bt_ref.at[gi], cab_ref.at[gi],
                       dcol_ref.at[gi], e_ref[:, gi * H:(gi + 1) * H, :], z_ref.at[gi], chunks_per_seq)
        f_ref[:, gi * H:(gi + 1) * H, :] = y.reshape(SSM_CHUNK, H, y.shape[-1])


def _ssm_group(arow_ref, acol_ref, ldt_ref, bt_ref, cab_ref, dcol_ref, e, z_ref, chunks_per_seq):
    T, P, H = SSM_CHUNK, SSM_STATE, SSM_GROUP_CH
    TH = T * H
    hi = lax.Precision.HIGHEST
    dt = jnp.exp(ldt_ref[...])

    lam_r, lam_i = dt * arow_ref[0:1, :], dt * arow_ref[1:2, :]
    j0 = lax.broadcasted_iota(jnp.int32, (T, 2 * P), 0).astype(F32)
    pa0, pb0 = _cis(j0 * lam_r, j0 * lam_i)
    pa1, pb1 = _cmul(pa0, pb0, *_cis(lam_r, lam_i))
    over_h = lambda a: jnp.concatenate(
        [jnp.broadcast_to(a[j:j + 1, :], (H, 2 * P)) for j in range(T)], axis=0)
    ca, cb = jnp.tile(cab_ref[0], (T, 1)), jnp.tile(cab_ref[1], (T, 1))
    c_pow0 = over_h(pa0) * ca + over_h(pb0) * cb
    c_pow1 = over_h(pa1) * ca + over_h(pb1) * cb

    a_r, a_i = acol_ref[:, 0:1], acol_ref[:, 1:2]
    lr, li = dt * a_r, dt * a_i
    abar_r, abar_i = _cis(lr, li)
    nr, ni = abar_r - 1.0, abar_i
    den = a_r * a_r + a_i * a_i
    fr, fi = (nr * a_r + ni * a_i) / den, (ni * a_r - nr * a_i) / den
    b_r, b_i = bt_ref[0], bt_ref[1]
    bb_r, bb_i = fr * b_r - fi * b_i, fr * b_i + fi * b_r

    kcol = jnp.dot(c_pow0, jnp.concatenate([bb_r, bb_i], axis=0), precision=hi,
                   preferred_element_type=F32)
    lane_h = lax.broadcasted_iota(jnp.int32, (H, LANES), 1) % H
    skip = jnp.where(lane_h == lax.broadcasted_iota(jnp.int32, (H, LANES), 0), dcol_ref[...], 0.0)

    z_ref[TH:2 * TH, :] = kcol
    z_ref[TH:TH + H, :] = kcol[:H] + skip
    lane_group = lax.broadcasted_iota(jnp.int32, (1, LANES), 1) // H
    groups_per_block = LANES // H
    blocks = []
    for v in range(TH // LANES):
        blk = None
        for u in range(groups_per_block):
            s = v * groups_per_block + u
            piece = z_ref[TH - H * s:2 * TH - H * s, :]
            blk = piece if blk is None else jnp.where(lane_group == u, piece, blk)
        blocks.append(blk.astype(BF16))
    mt = jnp.concatenate(blocks, axis=1)

    expo = (groups_per_block - 1 - lane_group).astype(F32)
    wr, wi = _cis(lr * expo, li * expo)
    hop_r, hop_i = _cpow2(abar_r, abar_i, groups_per_block)
    w1_r, w1_i = [], []
    for v in range(TH // LANES):
        w1_r.insert(0, wr * bb_r - wi * bb_i)
        w1_i.insert(0, wr * bb_i + wi * bb_r)
        wr, wi = _cmul(wr, wi, hop_r, hop_i)
    w1t = jnp.concatenate([jnp.concatenate(w1_r, axis=1),
                           jnp.concatenate(w1_i, axis=1)], axis=0).astype(BF16)

    e = e.reshape(TH, e.shape[-1])
    y = jnp.dot(mt, e, preferred_element_type=F32)
    st = jnp.dot(w1t, e, preferred_element_type=F32)
    sr, si = st[:P], st[P:]
    pos = lax.broadcasted_iota(jnp.int32, sr.shape, 1) % chunks_per_seq

    def shifted(a, shift):
        return jnp.where(pos >= shift, pltpu.roll(a, shift, 1), 0.0)

    qr, qi = _cpow2(abar_r, abar_i, T)
    shift = 1
    while shift < chunks_per_seq:
        srs, sis = shifted(sr, shift), shifted(si, shift)
        sr, si = sr + qr * srs - qi * sis, si + qr * sis + qi * srs
        qr, qi = qr * qr - qi * qi, 2.0 * qr * qi
        shift *= 2
    x_prev = jnp.concatenate([shifted(sr, 1), shifted(si, 1)], axis=0).astype(BF16)
    y = y + jnp.dot(c_pow1.astype(BF16), x_prev, preferred_element_type=F32)
    return _gelu_tanh(y)


def _ssm_call(arow, acol, ldt, bt, cab, dcol, e, chunks_per_seq):
    T, _, NC = e.shape
    G = arow.shape[0]
    gps = SSM_GROUPS_PER_STEP
    assert G % gps == 0
    grp = lambda a: pl.BlockSpec((gps,) + a.shape[1:], lambda g: (g,) + (0,) * (a.ndim - 1))
    channels = pl.BlockSpec((T, gps * SSM_GROUP_CH, NC), lambda g: (0, g, 0))
    return pl.pallas_call(
        functools.partial(_ssm_kernel, chunks_per_seq=chunks_per_seq),
        grid=(G // gps,),
        in_specs=[grp(a) for a in (arow, acol, ldt, bt, cab, dcol)] + [channels],
        out_specs=channels,
        out_shape=jax.ShapeDtypeStruct((T, SSM_WIDTH, NC), F32),
        scratch_shapes=[pltpu.VMEM((gps, 2 * T * SSM_GROUP_CH, LANES), F32)],
        compiler_params=pltpu.CompilerParams(
            dimension_semantics=("arbitrary",), vmem_limit_bytes=VMEM_LIMIT),
        name="ssm",
    )(arow, acol, ldt, bt, cab, dcol, e)


def _ssm_param_layouts(a_re, a_im, log_dt, b_re, b_im, c_re, c_im, d_skip):
    G = a_re.shape[0]
    arow = jnp.stack([jnp.concatenate([a_re, a_re], -1), jnp.concatenate([a_im, a_im], -1)], 1)
    acol = jnp.stack([a_re, a_im], -1)
    reps = LANES // SSM_GROUP_CH
    bt = jnp.stack([jnp.tile(b_re, (1, 1, reps)), jnp.tile(b_im, (1, 1, reps))], 1)
    cab = jnp.stack([jnp.concatenate([c_re, -c_im], -1), jnp.concatenate([-c_im, -c_re], -1)], 1)
    return (arow.astype(F32), acol.astype(F32), log_dt.reshape(G, 1, 1).astype(F32),
            bt.astype(F32), cab.astype(F32), d_skip.reshape(G, SSM_GROUP_CH, 1).astype(F32))


def _tail_kernel(h1_ref, attn_ref, y_ref, p_ref, wglu_ref, bglu_ref, ga_ref, gs_ref,
                 wo_ref, g2_ref, w1_ref, w3_ref, w2_ref, gp_ref, wpg_ref,
                 wpp_ref, gf_ref, o_ref):
    y = y_ref[...]
    glu = y * _sigmoid(jnp.dot(y.astype(BF16), wglu_ref[...], preferred_element_type=F32)
                       + bglu_ref[...])
    an = _rms(attn_ref[...], ga_ref[...]).astype(BF16)
    sn = _rms(glu, gs_ref[...]).astype(BF16)
    h = (h1_ref[...] + jnp.dot(an, wo_ref[:ATTN_WIDTH, :], preferred_element_type=F32)
         + jnp.dot(sn, wo_ref[ATTN_WIDTH:, :], preferred_element_type=F32))
    h = h + 0.5 * _swiglu(_rms(h, g2_ref[...]).astype(BF16), w1_ref, w3_ref, w2_ref)
    gate = _sigmoid(jnp.dot(_rms(h, gp_ref[...]).astype(BF16), wpg_ref[...],
                            preferred_element_type=F32))
    h = h + gate * jnp.dot(p_ref[...].astype(BF16), wpp_ref[...], preferred_element_type=F32)
    o_ref[...] = _rms(h, gf_ref[...])


def _tail_call(h1, attn, y, p, *consts):
    B, L, D = h1.shape
    tm = TOKEN_TILE
    tile = lambda w: pl.BlockSpec((None, tm, w), lambda b, i: (b, i, 0))
    return pl.pallas_call(
        _tail_kernel,
        grid=(B, L // tm),
        in_specs=[tile(D), tile(ATTN_WIDTH), tile(SSM_WIDTH), tile(PLE_DIM)]
                 + [_const_spec(c.shape) for c in consts],
        out_specs=tile(D),
        out_shape=jax.ShapeDtypeStruct((B, L, D), F32),
        compiler_params=pltpu.CompilerParams(
            dimension_semantics=("arbitrary", "arbitrary"),
            vmem_limit_bytes=VMEM_LIMIT),
        name="tail",
    )(h1, attn, y, p, *consts)


def kernel(x, p, g_ffn1, w1_a, w3_a, w2_a, g_mix, w_in, b_f, a_re, a_im, log_dt, b_re, b_im, c_re, c_im, d_skip, w_glu, b_glu, g_attn_out, g_ssm_out, w_out, g_ffn2, w1_b, w3_b, w2_b, g_ple, w_ple_gate, w_ple_proj, g_final):
    B, L, D = x.shape
    assert D == D_MODEL and L % ATTN_TILE == 0 and L % TOKEN_TILE == 0 and L % SSM_CHUNK == 0
    assert g_ffn1.shape[0] == 1, "single layer"
    assert TOKEN_TILE == ATTN_TILE, "decay bias rows are relative to the kv tile start"
    row = lambda g: g.reshape(1, -1).astype(F32)
    bf = lambda w: w.astype(BF16)
    s_v, s_f = 3 * ATTN_WIDTH, 3 * ATTN_WIDTH + ATTN_HEADS
    w_in0 = w_in[0]
    scale = LOG2E / math.sqrt(HEAD_DIM)
    wqkv = jnp.concatenate([w_in0[:, ATTN_WIDTH:s_v], w_in0[:, :ATTN_WIDTH] * scale], axis=1)
    wf = jnp.pad(w_in0[:, s_v:s_f], ((0, 0), (0, LANES - ATTN_HEADS)))

    (h1, q, kt, v, s_in, ct, k_norm2), (w1_b16, w3_b16, w2_b16, w_out16, w_gate16) = _head_call(
        x, row(g_ffn1[0]), bf(w1_a[0]), bf(w3_a[0]), bf(w2_a[0]), row(g_mix[0]),
        bf(wqkv), bf(w_in0[:, s_f:]), bf(wf), b_f[0].reshape(ATTN_HEADS, 1).astype(F32),
        later=tuple(w.astype(F32) for w in (w1_b[0], w3_b[0], w2_b[0], w_out[0], w_ple_gate[0])))

    attn = _attn_call(q, kt, v, ct, k_norm2)

    T = SSM_CHUNK
    chunks_per_seq = L // T
    n_chunks = B * chunks_per_seq
    e = s_in.reshape(n_chunks, T, SSM_WIDTH).transpose(1, 2, 0)
    f = _ssm_call(*_ssm_param_layouts(a_re[0], a_im[0], log_dt[0], b_re[0], b_im[0],
                                      c_re[0], c_im[0], d_skip[0]),
                  e, chunks_per_seq)
    y = f.transpose(2, 0, 1).reshape(B, L, SSM_WIDTH)

    return _tail_call(
        h1, attn, y, p[0],
        bf(w_glu[0]), row(b_glu[0]), row(g_attn_out[0]), row(g_ssm_out[0]),
        w_out16, row(g_ffn2[0]),
        w1_b16, w3_b16, w2_b16, row(g_ple[0]), w_gate16,
        bf(w_ple_proj[0]), row(g_final))
```

```python
import functools
import math

import jax
import jax.numpy as jnp
from jax import lax
from jax.experimental import pallas as pl
from jax.experimental.pallas import tpu as pltpu

D_MODEL = 1024
ATTN_HEADS = 8
HEAD_DIM = 64
ATTN_WIDTH = ATTN_HEADS * HEAD_DIM
SSM_WIDTH = D_MODEL - ATTN_WIDTH
SSM_GROUP_CH = 16
SSM_GROUPS = SSM_WIDTH // SSM_GROUP_CH
SSM_STATE = 64
D_FF = 2816
PLE_DIM = 256
EPS = 1e-6

LANES = 128
BF16_SUBLANES = 16
HEAD_PAIR = 2 * HEAD_DIM
N_PAIRS = ATTN_HEADS // 2
FF_CHUNK = 256
TOKEN_TILE = 512
ATTN_TILE = 512
SSM_CHUNK = 32
SSM_GROUPS_PER_STEP = 2
NEG_BIG = -1e30
SKIP_LOG2 = 140.0
BOUND_SLACK_MUL = 1.001
BOUND_SLACK_ADD = 1.0
LOG2E = math.log2(math.e)
N_BIAS = 3
BIAS_ROWS = 8
VMEM_LIMIT = 56 * 1024 * 1024

BF16 = jnp.bfloat16
F32 = jnp.float32


def _rms(x, g):
    ms = jnp.mean(x * x, axis=-1, keepdims=True)
    return x * lax.rsqrt(ms + EPS) * g


def _sigmoid(x):
    return 1.0 / (1.0 + jnp.exp(-x))


def _swiglu(xn, w1_ref, w3_ref, w2_ref):
    acc = None
    for c in range(D_FF // FF_CHUNK):
        sl = slice(c * FF_CHUNK, (c + 1) * FF_CHUNK)
        a = jnp.dot(xn, w1_ref[:, sl], preferred_element_type=F32)
        b = jnp.dot(xn, w3_ref[:, sl], preferred_element_type=F32)
        gated = (a * _sigmoid(a) * b).astype(BF16)
        part = jnp.dot(gated, w2_ref[sl, :], preferred_element_type=F32)
        acc = part if acc is None else acc + part
    return acc


def _const_spec(shape):
    nd = len(shape)
    return pl.BlockSpec(shape, lambda *_: (0,) * nd, pipeline_mode=pl.Buffered(1))


def _head_kernel(*refs, n_later):
    (x_ref, g1_ref, w1_ref, w3_ref, w2_ref, gm_ref, wkvq_ref, ws_ref, wf_ref, bf_ref) = refs[:10]
    later_in = refs[10:10 + n_later]
    h1_ref, q_ref, kt_ref, v_ref, s_ref, ct_ref, kn_ref = refs[10 + n_later:17 + n_later]
    later_out = refs[17 + n_later:17 + 2 * n_later]
    carry_ref = refs[-1]
    for src, dst in zip(later_in, later_out):
        dst[...] = src[...].astype(BF16)
    tm = x_ref.shape[0]
    x = x_ref[...]
    h1 = x + 0.5 * _swiglu(_rms(x, g1_ref[...]).astype(BF16), w1_ref, w3_ref, w2_ref)
    h1_ref[...] = h1
    un = _rms(h1, gm_ref[...]).astype(BF16)
    project = lambda w: jnp.dot(un, w, preferred_element_type=F32)
    zf = project(wf_ref[...])
    kv = project(wkvq_ref[:, :2 * ATTN_WIDTH])
    zft = zf.T[:ATTN_HEADS, :] + bf_ref[...]
    logf = jnp.minimum(zft, 0.0) - jnp.log1p(jnp.exp(-jnp.abs(zft)))
    lane = lax.broadcasted_iota(jnp.int32, logf.shape, 1)
    c = logf
    shift = 1
    while shift < tm:
        c = c + jnp.where(lane >= shift, pltpu.roll(c, shift, 1), 0.0)
        shift *= 2

    @pl.when(pl.program_id(1) == 0)
    def _():
        carry_ref[...] = jnp.zeros_like(carry_ref)

    c_abs = c + carry_ref[:, 0:1]
    ct_ref[...] = c_abs * LOG2E
    carry_ref[...] = jnp.broadcast_to(c_abs[:, tm - 1:tm], carry_ref.shape)

    rel = (c - c[:, 0:1]) * LOG2E
    hi = rel.astype(BF16).astype(F32)
    mid = (rel - hi).astype(BF16).astype(F32)
    lo = (rel - hi - mid).astype(BF16).astype(F32)
    kt = kv[:, :ATTN_WIDTH].astype(BF16).astype(F32).T
    vv = kv[:, ATTN_WIDTH:]
    k_sq = (kt * kt).reshape(ATTN_HEADS, HEAD_DIM, tm)
    kn_ref[...] = jnp.broadcast_to(
        jnp.max(jnp.sum(k_sq, axis=1), axis=-1, keepdims=True), kn_ref.shape)
    sub = lax.broadcasted_iota(jnp.int32, (BIAS_ROWS, tm), 0)
    zeros = jnp.zeros((HEAD_DIM - BIAS_ROWS, tm), F32)
    vlane = lax.broadcasted_iota(jnp.int32, (tm, HEAD_PAIR), 1)
    for h in range(ATTN_HEADS):
        bias = jnp.where(sub == 0, -hi[h:h + 1],
                         jnp.where(sub == 1, -mid[h:h + 1],
                                   jnp.where(sub == 2, -lo[h:h + 1], 0.0)))
        k_h = kt[h * HEAD_DIM:(h + 1) * HEAD_DIM]
        vp = vv[:, (h // 2) * HEAD_PAIR:(h // 2 + 1) * HEAD_PAIR]
        if h % 2 == 0:
            kt_ref[h] = jnp.concatenate([k_h, bias, zeros], axis=0).astype(BF16)
            v_ref[h] = jnp.where(vlane < HEAD_DIM, vp,
                                 jnp.where(vlane == HEAD_DIM, 1.0, 0.0)).astype(BF16)
        else:
            kt_ref[h] = jnp.concatenate([bias, zeros, k_h], axis=0).astype(BF16)
            v_ref[h] = jnp.where(vlane >= HEAD_DIM, vp,
                                 jnp.where(vlane == 0, 1.0, 0.0)).astype(BF16)

    q_ref[...] = project(wkvq_ref[:, 2 * ATTN_WIDTH:]).astype(BF16)
    s_ref[...] = project(ws_ref[...]).astype(BF16)


def _slab_spec(w, n_steps, steps_per_batch):
    rows = next(r for r in range(BF16_SUBLANES, w.shape[0] + 1, BF16_SUBLANES)
                if w.shape[0] % r == 0 and w.shape[0] // r <= n_steps)
    last = w.shape[0] // rows - 1
    return pl.BlockSpec((rows, w.shape[1]),
                        lambda b, i: (jnp.minimum(b * steps_per_batch + i, last), 0))


def _head_call(x, g1, w1, w3, w2, gm, wqkv, ws, wf, bf, later):
    B, L, D = x.shape
    tm = TOKEN_TILE
    tile = lambda w: pl.BlockSpec((None, tm, w), lambda b, i: (b, i, 0))
    slabs = [_slab_spec(w, B * (L // tm), L // tm) for w in later]
    out_shape = (
        jax.ShapeDtypeStruct((B, L, D), F32),
        jax.ShapeDtypeStruct((B, L, ATTN_WIDTH), BF16),
        jax.ShapeDtypeStruct((B, ATTN_HEADS, HEAD_PAIR, L), BF16),
        jax.ShapeDtypeStruct((B, ATTN_HEADS, L, HEAD_PAIR), BF16),
        jax.ShapeDtypeStruct((B, L, SSM_WIDTH), BF16),
        jax.ShapeDtypeStruct((B, ATTN_HEADS, L), F32),
        jax.ShapeDtypeStruct((B, L // tm, ATTN_HEADS, LANES), F32),
    ) + tuple(jax.ShapeDtypeStruct(w.shape, BF16) for w in later)
    outs = pl.pallas_call(
        functools.partial(_head_kernel, n_later=len(later)),
        grid=(B, L // tm),
        in_specs=[tile(D), _const_spec(g1.shape), _const_spec(w1.shape),
                  _const_spec(w3.shape), _const_spec(w2.shape), _const_spec(gm.shape),
                  _const_spec(wqkv.shape), _const_spec(ws.shape), _const_spec(wf.shape),
                  _const_spec(bf.shape)] + slabs,
        out_specs=(tile(D), tile(ATTN_WIDTH),
                   pl.BlockSpec((None, ATTN_HEADS, HEAD_PAIR, tm), lambda b, i: (b, 0, 0, i)),
                   pl.BlockSpec((None, ATTN_HEADS, tm, HEAD_PAIR), lambda b, i: (b, 0, i, 0)),
                   tile(SSM_WIDTH),
                   pl.BlockSpec((None, ATTN_HEADS, tm), lambda b, i: (b, 0, i)),
                   pl.BlockSpec((None, None, ATTN_HEADS, LANES), lambda b, i: (b, i, 0, 0)))
                  + tuple(slabs),
        out_shape=out_shape,
        scratch_shapes=[pltpu.VMEM((ATTN_HEADS, LANES), F32)],
        compiler_params=pltpu.CompilerParams(
            dimension_semantics=("arbitrary", "arbitrary"),
            vmem_limit_bytes=VMEM_LIMIT),
        name="head",
    )(x, g1, w1, w3, w2, gm, wqkv, ws, wf, bf, *later)
    return outs[:7], outs[7:]


def _attn_kernel(q_ref, *refs):
    _attn_prepare(0, q_ref, *refs)

    def q_tile(n, carry):
        _attn_q_tile(n, q_ref, *refs)
        return carry

    lax.fori_loop(0, q_ref.shape[0] // ATTN_TILE, q_tile, 0)


def _tile_start(i):
    return pl.multiple_of(i * ATTN_TILE, ATTN_TILE)


def _q_heads(q):
    lane = lax.broadcasted_iota(jnp.int32, (1, HEAD_PAIR), 1)
    first = lane < HEAD_DIM
    ones_even = jnp.where((lane >= HEAD_DIM) & (lane < HEAD_DIM + N_BIAS), 1.0, 0.0).astype(BF16)
    ones_odd = jnp.where(lane < N_BIAS, 1.0, 0.0).astype(BF16)
    return first, (jnp.where(first, q, ones_even), jnp.where(first, ones_odd, q))


def _attn_prepare(n, q_ref, kt_ref, v_ref, c_ref, st_ref, o_ref, s_e0, s_e1, s_o0, s_o1, m_ref,
                  acc_ref, reach_ref):
    t = ATTN_TILE
    q = q_ref[pl.ds(_tile_start(n), t), :]
    first, q_heads = _q_heads(q)
    q_sq = q.astype(F32) * q.astype(F32)
    for h, s_ref in enumerate((s_o0, s_o1)):
        s_ref[...] = jnp.dot(q_heads[h], kt_ref[h, :, pl.ds(_tile_start(n), t)],
                             preferred_element_type=F32)
        q_norm2 = jnp.max(jnp.sum(jnp.where(first == (h == 0), q_sq, 0.0), axis=-1, keepdims=True),
                          axis=0, keepdims=True)
        k_norm2, c_next = st_ref[h:h + 1, :], st_ref[2 + h:3 + h, :]
        c_q = c_ref[h:h + 1, pl.ds(_tile_start(n), LANES)][:, 0:1]
        reach_ref[h:h + 1, :] = (jnp.sqrt(q_norm2 * k_norm2) * BOUND_SLACK_MUL + BOUND_SLACK_ADD
                                 + c_q - c_next)


def _attn_q_tile(n, q_ref, kt_ref, v_ref, c_ref, st_ref, o_ref, s_e0, s_e1, s_o0, s_o1, m_ref,
                 acc_ref, reach_ref):
    t = ATTN_TILE
    start = _tile_start
    first, q_heads = _q_heads(q_ref[pl.ds(start(n), t), :])
    s_buf = ((s_e0, s_e1), (s_o0, s_o1))
    c_q = [c_ref[h:h + 1, pl.ds(start(n), LANES)][:, 0:1] for h in range(2)]

    def qk(h, kv):
        return jnp.dot(q_heads[h], kt_ref[h, :, pl.ds(start(kv), t)], preferred_element_type=F32)

    def consume(h, s, kv):
        d = c_ref[h:h + 1, pl.ds(start(kv), LANES)][:, 0:1] - c_q[h]
        m_old = m_ref[h]
        m_new = jnp.maximum(m_old, jnp.max(s, axis=-1, keepdims=True) - d)
        p = jnp.exp2(s - (m_new + d)).astype(BF16)
        m_ref[h] = m_new
        acc_ref[h] = jnp.exp2(m_old - m_new) * acc_ref[h] + jnp.dot(
            p, v_ref[h, pl.ds(start(kv), t), :], preferred_element_type=F32)

    def step(kv, par):
        for h in range(2):
            s_buf[1 - par][h][...] = qk(h, kv - 1)
        for h in range(2):
            consume(h, s_buf[par][h][...], kv)

    @pl.when(n >= 0)
    def _():
        row = lax.broadcasted_iota(jnp.int32, (t, t), 0)
        col = lax.broadcasted_iota(jnp.int32, (t, t), 1)
        for h in range(2):
            s_buf[0][h][...] = qk(h, jnp.maximum(n - 1, 0))
        for h in range(2):
            s = jnp.where(col <= row, s_buf[1][h][...], NEG_BIG)
            m = jnp.max(s, axis=-1, keepdims=True)
            m_ref[h] = m
            acc_ref[h] = jnp.dot(jnp.exp2(s - m).astype(BF16), v_ref[h, pl.ds(start(n), t), :],
                                 preferred_element_type=F32)

    tile_id = lax.broadcasted_iota(jnp.int32, (1, LANES), 1)
    needed = tile_id < 0
    for h in range(2):
        m_low = jnp.min(m_ref[h], axis=0, keepdims=True)
        needed = needed | (reach_ref[h:h + 1, :] - m_low >= -SKIP_LOG2)
    first_needed = jnp.min(jnp.where(needed & (tile_id < n), tile_id, n).astype(F32))
    count = n - first_needed.astype(jnp.int32)

    def pair(i, carry):
        step(n - 1 - 2 * i, 0)
        step(n - 2 - 2 * i, 1)
        return carry

    n_pairs = jnp.maximum(count - 1, 0) // 2
    lax.fori_loop(0, n_pairs, pair, 0)
    left = count - 2 * n_pairs
    last = n - count

    @pl.when(left == 2)
    def _():
        step(last + 1, 0)
        for h in range(2):
            consume(h, s_buf[1][h][...], last)

    @pl.when(left == 1)
    def _():
        for h in range(2):
            consume(h, s_buf[0][h][...], last)

    acc0, acc1 = acc_ref[0], acc_ref[1]
    o_ref[pl.ds(start(n), t), :] = jnp.where(first, acc0 / acc0[:, HEAD_DIM:HEAD_DIM + 1],
                                             acc1 / acc1[:, 0:1])
    n_tiles = q_ref.shape[0] // t
    _attn_prepare(jnp.minimum(n + 1, n_tiles - 1), q_ref, kt_ref, v_ref, c_ref, st_ref, o_ref,
                  s_e0, s_e1, s_o0, s_o1, m_ref, acc_ref, reach_ref)


def _attn_call(q, kt, v, ct, k_norm2):
    B, L, _ = q.shape
    t = ATTN_TILE
    n_tiles = L // t
    assert n_tiles <= LANES
    c4 = ct.reshape(B, N_PAIRS, 2, L)
    c_next = jnp.roll(ct[:, :, ::t], -1, axis=-1)
    per_tile = jnp.stack([k_norm2[..., 0].transpose(0, 2, 1), c_next], axis=1)
    per_tile = jnp.pad(per_tile, ((0, 0),) * 3 + ((0, LANES - n_tiles),))
    stats = (per_tile.reshape(B, 2, N_PAIRS, 2, LANES).transpose(0, 2, 1, 3, 4)
             .reshape(B, N_PAIRS, 4, LANES))
    pair_block = lambda *shape: pl.BlockSpec((None, 2) + shape, lambda b, p: (b, p, 0, 0))
    lanes_of_pair = pl.BlockSpec((None, L, HEAD_PAIR), lambda b, p: (b, 0, p))
    return pl.pallas_call(
        _attn_kernel,
        grid=(B, N_PAIRS),
        in_specs=[lanes_of_pair, pair_block(HEAD_PAIR, L), pair_block(L, HEAD_PAIR),
                  pl.BlockSpec((None, None, 2, L), lambda b, p: (b, p, 0, 0)),
                  pl.BlockSpec((None, None, 4, LANES), lambda b, p: (b, p, 0, 0))],
        out_specs=lanes_of_pair,
        out_shape=jax.ShapeDtypeStruct((B, L, ATTN_WIDTH), F32),
        scratch_shapes=[pltpu.VMEM((t, t), F32)] * 4
                       + [pltpu.VMEM((2, t, 1), F32), pltpu.VMEM((2, t, HEAD_PAIR), F32),
                          pltpu.VMEM((2, LANES), F32)],
        compiler_params=pltpu.CompilerParams(
            dimension_semantics=("arbitrary", "arbitrary"),
            vmem_limit_bytes=VMEM_LIMIT),
        name="attn",
    )(q, kt, v, c4, stats)


def _gelu_tanh(x):
    return 0.5 * x * (1.0 + jnp.tanh(math.sqrt(2.0 / math.pi) * (x + 0.044715 * (x * x * x))))


def _cis(mag_arg, ang):
    mag = jnp.exp(mag_arg)
    return mag * jnp.cos(ang), mag * jnp.sin(ang)


def _cmul(ar, ai, br, bi):
    return ar * br - ai * bi, ar * bi + ai * br


def _cpow2(zr, zi, n):
    assert n & (n - 1) == 0
    while n > 1:
        zr, zi = zr * zr - zi * zi, 2.0 * zr * zi
        n //= 2
    return zr, zi


def _ssm_kernel(arow_ref, acol_ref, ldt_ref, bt_ref, cab_ref, dcol_ref, e_ref, f_ref, z_ref,
                *, chunks_per_seq):
    H = SSM_GROUP_CH

    @pl.when(pl.program_id(0) == 0)
    def _():
        z_ref[:, 0:SSM_CHUNK * H, :] = jnp.zeros((z_ref.shape[0], SSM_CHUNK * H, LANES), F32)

    for gi in range(SSM_GROUPS_PER_STEP):
        y = _ssm_group(arow_ref.at[gi], acol_ref.at[gi], ldt_ref.at[gi], bt_ref.at[gi], cab_ref.at[gi],
                       dcol_ref.at[gi], e_ref[:, gi * H:(gi + 1) * H, :], z_ref.at[gi], chunks_per_seq)
        f_ref[:, gi * H:(gi + 1) * H, :] = y.reshape(SSM_CHUNK, H, y.shape[-1])


def _ssm_group(arow_ref, acol_ref, ldt_ref, bt_ref, cab_ref, dcol_ref, e, z_ref, chunks_per_seq):
    T, P, H = SSM_CHUNK, SSM_STATE, SSM_GROUP_CH
    TH = T * H
    hi = lax.Precision.HIGHEST
    dt = jnp.exp(ldt_ref[...])

    lam_r, lam_i = dt * arow_ref[0:1, :], dt * arow_ref[1:2, :]
    j0 = lax.broadcasted_iota(jnp.int32, (T, 2 * P), 0).astype(F32)
    pa0, pb0 = _cis(j0 * lam_r, j0 * lam_i)
    pa1, pb1 = _cmul(pa0, pb0, *_cis(lam_r, lam_i))
    over_h = lambda a: jnp.concatenate(
        [jnp.broadcast_to(a[j:j + 1, :], (H, 2 * P)) for j in range(T)], axis=0)
    ca, cb = jnp.tile(cab_ref[0], (T, 1)), jnp.tile(cab_ref[1], (T, 1))
    c_pow0 = over_h(pa0) * ca + over_h(pb0) * cb
    c_pow1 = over_h(pa1) * ca + over_h(pb1) * cb

    a_r, a_i = acol_ref[:, 0:1], acol_ref[:, 1:2]
    lr, li = dt * a_r, dt * a_i
    abar_r, abar_i = _cis(lr, li)
    nr, ni = abar_r - 1.0, abar_i
    den = a_r * a_r + a_i * a_i
    fr, fi = (nr * a_r + ni * a_i) / den, (ni * a_r - nr * a_i) / den
    b_r, b_i = bt_ref[0], bt_ref[1]
    bb_r, bb_i = fr * b_r - fi * b_i, fr * b_i + fi * b_r

    kcol = jnp.dot(c_pow0, jnp.concatenate([bb_r, bb_i], axis=0), precision=hi,
                   preferred_element_type=F32)
    lane_h = lax.broadcasted_iota(jnp.int32, (H, LANES), 1) % H
    skip = jnp.where(lane_h == lax.broadcasted_iota(jnp.int32, (H, LANES), 0), dcol_ref[...], 0.0)

    z_ref[TH:2 * TH, :] = kcol
    z_ref[TH:TH + H, :] = kcol[:H] + skip
    lane_group = lax.broadcasted_iota(jnp.int32, (1, LANES), 1) // H
    groups_per_block = LANES // H
    blocks = []
    for v in range(TH // LANES):
        blk = None
        for u in range(groups_per_block):
            s = v * groups_per_block + u
            piece = z_ref[TH - H * s:2 * TH - H * s, :]
            blk = piece if blk is None else jnp.where(lane_group == u, piece, blk)
        blocks.append(blk.astype(BF16))
    mt = jnp.concatenate(blocks, axis=1)

    expo = (groups_per_block - 1 - lane_group).astype(F32)
    wr, wi = _cis(lr * expo, li * expo)
    hop_r, hop_i = _cpow2(abar_r, abar_i, groups_per_block)
    w1_r, w1_i = [], []
    for v in range(TH // LANES):
        w1_r.insert(0, wr * bb_r - wi * bb_i)
        w1_i.insert(0, wr * bb_i + wi * bb_r)
        wr, wi = _cmul(wr, wi, hop_r, hop_i)
    w1t = jnp.concatenate([jnp.concatenate(w1_r, axis=1),
                           jnp.concatenate(w1_i, axis=1)], axis=0).astype(BF16)

    e = e.reshape(TH, e.shape[-1])
    y = jnp.dot(mt, e, preferred_element_type=F32)
    st = jnp.dot(w1t, e, preferred_element_type=F32)
    sr, si = st[:P], st[P:]
    pos = lax.broadcasted_iota(jnp.int32, sr.shape, 1) % chunks_per_seq

    def shifted(a, shift):
        return jnp.where(pos >= shift, pltpu.roll(a, shift, 1), 0.0)

    qr, qi = _cpow2(abar_r, abar_i, T)
    shift = 1
    while shift < chunks_per_seq:
        srs, sis = shifted(sr, shift), shifted(si, shift)
        sr, si = sr + qr * srs - qi * sis, si + qr * sis + qi * srs
        qr, qi = qr * qr - qi * qi, 2.0 * qr * qi
        shift *= 2
    x_prev = jnp.concatenate([shifted(sr, 1), shifted(si, 1)], axis=0).astype(BF16)
    y = y + jnp.dot(c_pow1.astype(BF16), x_prev, preferred_element_type=F32)
    return _gelu_tanh(y)


def _ssm_call(arow, acol, ldt, bt, cab, dcol, e, chunks_per_seq):
    T, _, NC = e.shape
    G = arow.shape[0]
    gps = SSM_GROUPS_PER_STEP
    assert G % gps == 0
    grp = lambda a: pl.BlockSpec((gps,) + a.shape[1:], lambda g: (g,) + (0,) * (a.ndim - 1))
    channels = pl.BlockSpec((T, gps * SSM_GROUP_CH, NC), lambda g: (0, g, 0))
    return pl.pallas_call(
        functools.partial(_ssm_kernel, chunks_per_seq=chunks_per_seq),
        grid=(G // gps,),
        in_specs=[grp(a) for a in (arow, acol, ldt, bt, cab, dcol)] + [channels],
        out_specs=channels,
        out_shape=jax.ShapeDtypeStruct((T, SSM_WIDTH, NC), F32),
        scratch_shapes=[pltpu.VMEM((gps, 2 * T * SSM_GROUP_CH, LANES), F32)],
        compiler_params=pltpu.CompilerParams(
            dimension_semantics=("arbitrary",), vmem_limit_bytes=VMEM_LIMIT),
        name="ssm",
    )(arow, acol, ldt, bt, cab, dcol, e)


def _ssm_param_layouts(a_re, a_im, log_dt, b_re, b_im, c_re, c_im, d_skip):
    G = a_re.shape[0]
    arow = jnp.stack([jnp.concatenate([a_re, a_re], -1), jnp.concatenate([a_im, a_im], -1)], 1)
    acol = jnp.stack([a_re, a_im], -1)
    reps = LANES // SSM_GROUP_CH
    bt = jnp.stack([jnp.tile(b_re, (1, 1, reps)), jnp.tile(b_im, (1, 1, reps))], 1)
    cab = jnp.stack([jnp.concatenate([c_re, -c_im], -1), jnp.concatenate([-c_im, -c_re], -1)], 1)
    return (arow.astype(F32), acol.astype(F32), log_dt.reshape(G, 1, 1).astype(F32),
            bt.astype(F32), cab.astype(F32), d_skip.reshape(G, SSM_GROUP_CH, 1).astype(F32))


def _tail_kernel(h1_ref, attn_ref, y_ref, p_ref, wglu_ref, bglu_ref, ga_ref, gs_ref,
                 wo_ref, g2_ref, w1_ref, w3_ref, w2_ref, gp_ref, wpg_ref,
                 wpp_ref, gf_ref, o_ref):
    y = y_ref[...]
    glu = y * _sigmoid(jnp.dot(y.astype(BF16), wglu_ref[...], preferred_element_type=F32)
                       + bglu_ref[...])
    an = _rms(attn_ref[...], ga_ref[...]).astype(BF16)
    sn = _rms(glu, gs_ref[...]).astype(BF16)
    h = (h1_ref[...] + jnp.dot(an, wo_ref[:ATTN_WIDTH, :], preferred_element_type=F32)
         + jnp.dot(sn, wo_ref[ATTN_WIDTH:, :], preferred_element_type=F32))
    h = h + 0.5 * _swiglu(_rms(h, g2_ref[...]).astype(BF16), w1_ref, w3_ref, w2_ref)
    gate = _sigmoid(jnp.dot(_rms(h, gp_ref[...]).astype(BF16), wpg_ref[...],
                            preferred_element_type=F32))
    h = h + gate * jnp.dot(p_ref[...].astype(BF16), wpp_ref[...], preferred_element_type=F32)
    o_ref[...] = _rms(h, gf_ref[...])


def _tail_call(h1, attn, y, p, *consts):
    B, L, D = h1.shape
    tm = TOKEN_TILE
    tile = lambda w: pl.BlockSpec((None, tm, w), lambda b, i: (b, i, 0))
    return pl.pallas_call(
        _tail_kernel,
        grid=(B, L // tm),
        in_specs=[tile(D), tile(ATTN_WIDTH), tile(SSM_WIDTH), tile(PLE_DIM)]
                 + [_const_spec(c.shape) for c in consts],
        out_specs=tile(D),
        out_shape=jax.ShapeDtypeStruct((B, L, D), F32),
        compiler_params=pltpu.CompilerParams(
            dimension_semantics=("arbitrary", "arbitrary"),
            vmem_limit_bytes=VMEM_LIMIT),
        name="tail",
    )(h1, attn, y, p, *consts)


def kernel(x, p, g_ffn1, w1_a, w3_a, w2_a, g_mix, w_in, b_f, a_re, a_im, log_dt, b_re, b_im, c_re, c_im, d_skip, w_glu, b_glu, g_attn_out, g_ssm_out, w_out, g_ffn2, w1_b, w3_b, w2_b, g_ple, w_ple_gate, w_ple_proj, g_final):
    B, L, D = x.shape
    assert D == D_MODEL and L % ATTN_TILE == 0 and L % TOKEN_TILE == 0 and L % SSM_CHUNK == 0
    assert g_ffn1.shape[0] == 1, "single layer"
    assert TOKEN_TILE == ATTN_TILE, "decay bias rows are relative to the kv tile start"
    row = lambda g: g.reshape(1, -1).astype(F32)
    bf = lambda w: w.astype(BF16)
    s_v, s_f = 3 * ATTN_WIDTH, 3 * ATTN_WIDTH + ATTN_HEADS
    w_in0 = w_in[0]
    scale = LOG2E / math.sqrt(HEAD_DIM)
    wqkv = jnp.concatenate([w_in0[:, ATTN_WIDTH:s_v], w_in0[:, :ATTN_WIDTH] * scale], axis=1)
    wf = jnp.pad(w_in0[:, s_v:s_f], ((0, 0), (0, LANES - ATTN_HEADS)))

    (h1, q, kt, v, s_in, ct, k_norm2), (w1_b16, w3_b16, w2_b16, w_out16, w_gate16) = _head_call(
        x, row(g_ffn1[0]), bf(w1_a[0]), bf(w3_a[0]), bf(w2_a[0]), row(g_mix[0]),
        bf(wqkv), bf(w_in0[:, s_f:]), bf(wf), b_f[0].reshape(ATTN_HEADS, 1).astype(F32),
        later=tuple(w.astype(F32) for w in (w1_b[0], w3_b[0], w2_b[0], w_out[0], w_ple_gate[0])))

    attn = _attn_call(q, kt, v, ct, k_norm2)

    T = SSM_CHUNK
    chunks_per_seq = L // T
    n_chunks = B * chunks_per_seq
    e = s_in.reshape(n_chunks, T, SSM_WIDTH).transpose(1, 2, 0)
    f = _ssm_call(*_ssm_param_layouts(a_re[0], a_im[0], log_dt[0], b_re[0], b_im[0],
                                      c_re[0], c_im[0], d_skip[0]),
                  e, chunks_per_seq)
    y = f.transpose(2, 0, 1).reshape(B, L, SSM_WIDTH)

    return _tail_call(
        h1, attn, y, p[0],
        bf(w_glu[0]), row(b_glu[0]), row(g_attn_out[0]), row(g_ssm_out[0]),
        w_out16, row(g_ffn2[0]),
        w1_b16, w3_b16, w2_b16, row(g_ple[0]), w_gate16,
        bf(w_ple_proj[0]), row(g_final))
```

```python
import functools
import math

import jax
import jax.numpy as jnp
from jax import lax
from jax.experimental import pallas as pl
from jax.experimental.pallas import tpu as pltpu

D_MODEL = 1024
ATTN_HEADS = 8
HEAD_DIM = 64
ATTN_WIDTH = ATTN_HEADS * HEAD_DIM
SSM_WIDTH = D_MODEL - ATTN_WIDTH
SSM_GROUP_CH = 16
SSM_GROUPS = SSM_WIDTH // SSM_GROUP_CH
SSM_STATE = 64
D_FF = 2816
PLE_DIM = 256
EPS = 1e-6

LANES = 128
BF16_SUBLANES = 16
HEAD_PAIR = 2 * HEAD_DIM
N_PAIRS = ATTN_HEADS // 2
FF_CHUNK = 256
TOKEN_TILE = 512
ATTN_TILE = 512
SSM_CHUNK = 32
SSM_GROUPS_PER_STEP = 2
NEG_BIG = -1e30
SKIP_LOG2 = 140.0
BOUND_SLACK_MUL = 1.001
BOUND_SLACK_ADD = 1.0
LOG2E = math.log2(math.e)
N_BIAS = 3
BIAS_ROWS = 8
VMEM_LIMIT = 56 * 1024 * 1024

BF16 = jnp.bfloat16
F32 = jnp.float32


def _rms(x, g):
    ms = jnp.mean(x * x, axis=-1, keepdims=True)
    return x * lax.rsqrt(ms + EPS) * g


def _sigmoid(x):
    return 1.0 / (1.0 + jnp.exp(-x))


def _swiglu(xn, w1_ref, w3_ref, w2_ref):
    acc = None
    for c in range(D_FF // FF_CHUNK):
        sl = slice(c * FF_CHUNK, (c + 1) * FF_CHUNK)
        a = jnp.dot(xn, w1_ref[:, sl], preferred_element_type=F32)
        b = jnp.dot(xn, w3_ref[:, sl], preferred_element_type=F32)
        gated = (a * _sigmoid(a) * b).astype(BF16)
        part = jnp.dot(gated, w2_ref[sl, :], preferred_element_type=F32)
        acc = part if acc is None else acc + part
    return acc


def _const_spec(shape):
    nd = len(shape)
    return pl.BlockSpec(shape, lambda *_: (0,) * nd, pipeline_mode=pl.Buffered(1))


def _head_kernel(*refs, n_later):
    (x_ref, g1_ref, w1_ref, w3_ref, w2_ref, gm_ref, wkvq_ref, ws_ref, wf_ref, bf_ref) = refs[:10]
    later_in = refs[10:10 + n_later]
    h1_ref, q_ref, kt_ref, v_ref, s_ref, ct_ref, kn_ref = refs[10 + n_later:17 + n_later]
    later_out = refs[17 + n_later:17 + 2 * n_later]
    carry_ref = refs[-1]
    for src, dst in zip(later_in, later_out):
        dst[...] = src[...].astype(BF16)
    tm = x_ref.shape[0]
    x = x_ref[...]
    h1 = x + 0.5 * _swiglu(_rms(x, g1_ref[...]).astype(BF16), w1_ref, w3_ref, w2_ref)
    h1_ref[...] = h1
    un = _rms(h1, gm_ref[...]).astype(BF16)
    project = lambda w: jnp.dot(un, w, preferred_element_type=F32)
    zf = project(wf_ref[...])
    kv = project(wkvq_ref[:, :2 * ATTN_WIDTH])
    zft = zf.T[:ATTN_HEADS, :] + bf_ref[...]
    logf = jnp.minimum(zft, 0.0) - jnp.log1p(jnp.exp(-jnp.abs(zft)))
    lane = lax.broadcasted_iota(jnp.int32, logf.shape, 1)
    c = logf
    shift = 1
    while shift < tm:
        c = c + jnp.where(lane >= shift, pltpu.roll(c, shift, 1), 0.0)
        shift *= 2

    @pl.when(pl.program_id(1) == 0)
    def _():
        carry_ref[...] = jnp.zeros_like(carry_ref)

    c_abs = c + carry_ref[:, 0:1]
    ct_ref[...] = c_abs * LOG2E
    carry_ref[...] = jnp.broadcast_to(c_abs[:, tm - 1:tm], carry_ref.shape)

    rel = (c - c[:, 0:1]) * LOG2E
    hi = rel.astype(BF16).astype(F32)
    mid = (rel - hi).astype(BF16).astype(F32)
    lo = (rel - hi - mid).astype(BF16).astype(F32)
    kt = kv[:, :ATTN_WIDTH].astype(BF16).astype(F32).T
    vv = kv[:, ATTN_WIDTH:]
    k_sq = (kt * kt).reshape(ATTN_HEADS, HEAD_DIM, tm)
    kn_ref[...] = jnp.broadcast_to(
        jnp.max(jnp.sum(k_sq, axis=1), axis=-1, keepdims=True), kn_ref.shape)
    sub = lax.broadcasted_iota(jnp.int32, (BIAS_ROWS, tm), 0)
    zeros = jnp.zeros((HEAD_DIM - BIAS_ROWS, tm), F32)
    vlane = lax.broadcasted_iota(jnp.int32, (tm, HEAD_PAIR), 1)
    for h in range(ATTN_HEADS):
        bias = jnp.where(sub == 0, -hi[h:h + 1],
                         jnp.where(sub == 1, -mid[h:h + 1],
                                   jnp.where(sub == 2, -lo[h:h + 1], 0.0)))
        k_h = kt[h * HEAD_DIM:(h + 1) * HEAD_DIM]
        vp = vv[:, (h // 2) * HEAD_PAIR:(h // 2 + 1) * HEAD_PAIR]
        if h % 2 == 0:
            kt_ref[h] = jnp.concatenate([k_h, bias, zeros], axis=0).astype(BF16)
            v_ref[h] = jnp.where(vlane < HEAD_DIM, vp,
                                 jnp.where(vlane == HEAD_DIM, 1.0, 0.0)).astype(BF16)
        else:
            kt_ref[h] = jnp.concatenate([bias, zeros, k_h], axis=0).astype(BF16)
            v_ref[h] = jnp.where(vlane >= HEAD_DIM, vp,
                                 jnp.where(vlane == 0, 1.0, 0.0)).astype(BF16)

    q_ref[...] = project(wkvq_ref[:, 2 * ATTN_WIDTH:]).astype(BF16)
    s_ref[...] = project(ws_ref[...]).astype(BF16)


def _slab_spec(w, n_steps, steps_per_batch):
    rows = next(r for r in range(BF16_SUBLANES, w.shape[0] + 1, BF16_SUBLANES)
                if w.shape[0] % r == 0 and w.shape[0] // r <= n_steps)
    last = w.shape[0] // rows - 1
    return pl.BlockSpec((rows, w.shape[1]),
                        lambda b, i: (jnp.minimum(b * steps_per_batch + i, last), 0))


def _head_call(x, g1, w1, w3, w2, gm, wqkv, ws, wf, bf, later):
    B, L, D = x.shape
    tm = TOKEN_TILE
    tile = lambda w: pl.BlockSpec((None, tm, w), lambda b, i: (b, i, 0))
    slabs = [_slab_spec(w, B * (L // tm), L // tm) for w in later]
    out_shape = (
        jax.ShapeDtypeStruct((B, L, D), F32),
        jax.ShapeDtypeStruct((B, L, ATTN_WIDTH), BF16),
        jax.ShapeDtypeStruct((B, ATTN_HEADS, HEAD_PAIR, L), BF16),
        jax.ShapeDtypeStruct((B, ATTN_HEADS, L, HEAD_PAIR), BF16),
        jax.ShapeDtypeStruct((B, L, SSM_WIDTH), BF16),
        jax.ShapeDtypeStruct((B, ATTN_HEADS, L), F32),
        jax.ShapeDtypeStruct((B, L // tm, ATTN_HEADS, LANES), F32),
    ) + tuple(jax.ShapeDtypeStruct(w.shape, BF16) for w in later)
    outs = pl.pallas_call(
        functools.partial(_head_kernel, n_later=len(later)),
        grid=(B, L // tm),
        in_specs=[tile(D), _const_spec(g1.shape), _const_spec(w1.shape),
                  _const_spec(w3.shape), _const_spec(w2.shape), _const_spec(gm.shape),
                  _const_spec(wqkv.shape), _const_spec(ws.shape), _const_spec(wf.shape),
                  _const_spec(bf.shape)] + slabs,
        out_specs=(tile(D), tile(ATTN_WIDTH),
                   pl.BlockSpec((None, ATTN_HEADS, HEAD_PAIR, tm), lambda b, i: (b, 0, 0, i)),
                   pl.BlockSpec((None, ATTN_HEADS, tm, HEAD_PAIR), lambda b, i: (b, 0, i, 0)),
                   tile(SSM_WIDTH),
                   pl.BlockSpec((None, ATTN_HEADS, tm), lambda b, i: (b, 0, i)),
                   pl.BlockSpec((None, None, ATTN_HEADS, LANES), lambda b, i: (b, i, 0, 0)))
                  + tuple(slabs),
        out_shape=out_shape,
        scratch_shapes=[pltpu.VMEM((ATTN_HEADS, LANES), F32)],
        compiler_params=pltpu.CompilerParams(
            dimension_semantics=("arbitrary", "arbitrary"),
            vmem_limit_bytes=VMEM_LIMIT),
        name="head",
    )(x, g1, w1, w3, w2, gm, wqkv, ws, wf, bf, *later)
    return outs[:7], outs[7:]


def _attn_kernel(q_ref, *refs):
    _attn_prepare(0, q_ref, *refs)

    def q_tile(n, carry):
        _attn_q_tile(n, q_ref, *refs)
        return carry

    lax.fori_loop(0, q_ref.shape[0] // ATTN_TILE, q_tile, 0)


def _tile_start(i):
    return pl.multiple_of(i * ATTN_TILE, ATTN_TILE)


def _q_heads(q):
    lane = lax.broadcasted_iota(jnp.int32, (1, HEAD_PAIR), 1)
    first = lane < HEAD_DIM
    ones_even = jnp.where((lane >= HEAD_DIM) & (lane < HEAD_DIM + N_BIAS), 1.0, 0.0).astype(BF16)
    ones_odd = jnp.where(lane < N_BIAS, 1.0, 0.0).astype(BF16)
    return first, (jnp.where(first, q, ones_even), jnp.where(first, ones_odd, q))


def _attn_prepare(n, q_ref, kt_ref, v_ref, c_ref, st_ref, o_ref, s_e0, s_e1, s_o0, s_o1, m_ref,
                  acc_ref, count_ref):
    t = ATTN_TILE
    q = q_ref[pl.ds(_tile_start(n), t), :]
    first, q_heads = _q_heads(q)
    q_sq = q.astype(F32) * q.astype(F32)
    row = lax.broadcasted_iota(jnp.int32, (t, t), 0)
    col = lax.broadcasted_iota(jnp.int32, (t, t), 1)
    tile_id = lax.broadcasted_iota(jnp.int32, (1, LANES), 1)
    needed = tile_id < 0
    for h, s_ref in enumerate((s_o0, s_o1)):
        s = jnp.where(col <= row, jnp.dot(q_heads[h], kt_ref[h, :, pl.ds(_tile_start(n), t)],
                                          preferred_element_type=F32), NEG_BIG)
        s_ref[...] = s
        m = jnp.max(s, axis=-1, keepdims=True)
        m_ref[h] = m
        q_norm2 = jnp.max(jnp.sum(jnp.where(first == (h == 0), q_sq, 0.0), axis=-1, keepdims=True),
                          axis=0, keepdims=True)
        k_norm2, c_next = st_ref[h:h + 1, :], st_ref[2 + h:3 + h, :]
        c_q = c_ref[h:h + 1, pl.ds(_tile_start(n), LANES)][:, 0:1]
        reach = jnp.sqrt(q_norm2 * k_norm2) * BOUND_SLACK_MUL + BOUND_SLACK_ADD + c_q - c_next
        needed = needed | (reach - jnp.min(m, axis=0, keepdims=True) >= -SKIP_LOG2)
    first_needed = jnp.min(jnp.where(needed & (tile_id < n), tile_id, n).astype(F32))
    count_ref[0] = n - first_needed.astype(jnp.int32)


def _attn_q_tile(n, q_ref, kt_ref, v_ref, c_ref, st_ref, o_ref, s_e0, s_e1, s_o0, s_o1, m_ref,
                 acc_ref, count_ref):
    t = ATTN_TILE
    count = count_ref[0]
    start = _tile_start
    first, q_heads = _q_heads(q_ref[pl.ds(start(n), t), :])
    s_buf = ((s_e0, s_e1), (s_o0, s_o1))
    c_q = [c_ref[h:h + 1, pl.ds(start(n), LANES)][:, 0:1] for h in range(2)]

    def qk(h, kv):
        return jnp.dot(q_heads[h], kt_ref[h, :, pl.ds(start(kv), t)], preferred_element_type=F32)

    def consume(h, s, kv):
        d = c_ref[h:h + 1, pl.ds(start(kv), LANES)][:, 0:1] - c_q[h]
        m_old = m_ref[h]
        m_new = jnp.maximum(m_old, jnp.max(s, axis=-1, keepdims=True) - d)
        p = jnp.exp2(s - (m_new + d)).astype(BF16)
        m_ref[h] = m_new
        acc_ref[h] = jnp.exp2(m_old - m_new) * acc_ref[h] + jnp.dot(
            p, v_ref[h, pl.ds(start(kv), t), :], preferred_element_type=F32)

    def step(kv, par):
        for h in range(2):
            s_buf[1 - par][h][...] = qk(h, kv - 1)
        for h in range(2):
            consume(h, s_buf[par][h][...], kv)

    for h in range(2):
        s_buf[0][h][...] = qk(h, jnp.maximum(n - 1, 0))
    for h in range(2):
        acc_ref[h] = jnp.dot(jnp.exp2(s_buf[1][h][...] - m_ref[h]).astype(BF16),
                             v_ref[h, pl.ds(start(n), t), :], preferred_element_type=F32)

    def pair(i, carry):
        step(n - 1 - 2 * i, 0)
        step(n - 2 - 2 * i, 1)
        return carry

    n_pairs = jnp.maximum(count - 1, 0) // 2
    lax.fori_loop(0, n_pairs, pair, 0)
    left = count - 2 * n_pairs
    last = n - count

    @pl.when(left == 2)
    def _():
        step(last + 1, 0)
        for h in range(2):
            consume(h, s_buf[1][h][...], last)

    @pl.when(left == 1)
    def _():
        for h in range(2):
            consume(h, s_buf[0][h][...], last)

    acc0, acc1 = acc_ref[0], acc_ref[1]
    o_ref[pl.ds(start(n), t), :] = jnp.where(first, acc0 / acc0[:, HEAD_DIM:HEAD_DIM + 1],
                                             acc1 / acc1[:, 0:1])
    n_tiles = q_ref.shape[0] // t
    _attn_prepare(jnp.minimum(n + 1, n_tiles - 1), q_ref, kt_ref, v_ref, c_ref, st_ref, o_ref,
                  s_e0, s_e1, s_o0, s_o1, m_ref, acc_ref, count_ref)


def _attn_call(q, kt, v, ct, k_norm2):
    B, L, _ = q.shape
    t = ATTN_TILE
    n_tiles = L // t
    assert n_tiles <= LANES
    c4 = ct.reshape(B, N_PAIRS, 2, L)
    c_next = jnp.roll(ct[:, :, ::t], -1, axis=-1)
    per_tile = jnp.stack([k_norm2[..., 0].transpose(0, 2, 1), c_next], axis=1)
    per_tile = jnp.pad(per_tile, ((0, 0),) * 3 + ((0, LANES - n_tiles),))
    stats = (per_tile.reshape(B, 2, N_PAIRS, 2, LANES).transpose(0, 2, 1, 3, 4)
             .reshape(B, N_PAIRS, 4, LANES))
    pair_block = lambda *shape: pl.BlockSpec((None, 2) + shape, lambda b, p: (b, p, 0, 0))
    lanes_of_pair = pl.BlockSpec((None, L, HEAD_PAIR), lambda b, p: (b, 0, p))
    return pl.pallas_call(
        _attn_kernel,
        grid=(B, N_PAIRS),
        in_specs=[lanes_of_pair, pair_block(HEAD_PAIR, L), pair_block(L, HEAD_PAIR),
                  pl.BlockSpec((None, None, 2, L), lambda b, p: (b, p, 0, 0)),
                  pl.BlockSpec((None, None, 4, LANES), lambda b, p: (b, p, 0, 0))],
        out_specs=lanes_of_pair,
        out_shape=jax.ShapeDtypeStruct((B, L, ATTN_WIDTH), F32),
        scratch_shapes=[pltpu.VMEM((t, t), F32)] * 4
                       + [pltpu.VMEM((2, t, 1), F32), pltpu.VMEM((2, t, HEAD_PAIR), F32),
                          pltpu.SMEM((1,), jnp.int32)],
        compiler_params=pltpu.CompilerParams(
            dimension_semantics=("arbitrary", "arbitrary"),
            vmem_limit_bytes=VMEM_LIMIT),
        name="attn",
    )(q, kt, v, c4, stats)


def _gelu_tanh(x):
    return 0.5 * x * (1.0 + jnp.tanh(math.sqrt(2.0 / math.pi) * (x + 0.044715 * (x * x * x))))


def _cis(mag_arg, ang):
    mag = jnp.exp(mag_arg)
    return mag * jnp.cos(ang), mag * jnp.sin(ang)


def _cmul(ar, ai, br, bi):
    return ar * br - ai * bi, ar * bi + ai * br


def _cpow2(zr, zi, n):
    assert n & (n - 1) == 0
    while n > 1:
        zr, zi = zr * zr - zi * zi, 2.0 * zr * zi
        n //= 2
    return zr, zi


def _ssm_kernel(arow_ref, acol_ref, ldt_ref, bt_ref, cab_ref, dcol_ref, e_ref, f_ref, z_ref,
                *, chunks_per_seq):
    H = SSM_GROUP_CH

    @pl.when(pl.program_id(0) == 0)
    def _():
        z_ref[:, 0:SSM_CHUNK * H, :] = jnp.zeros((z_ref.shape[0], SSM_CHUNK * H, LANES), F32)

    for gi in range(SSM_GROUPS_PER_STEP):
        y = _ssm_group(arow_ref.at[gi], acol_ref.at[gi], ldt_ref.at[gi], bt_ref.at[gi], cab_ref.at[gi],
                       dcol_ref.at[gi], e_ref[:, gi * H:(gi + 1) * H, :], z_ref.at[gi], chunks_per_seq)
        f_ref[:, gi * H:(gi + 1) * H, :] = y.reshape(SSM_CHUNK, H, y.shape[-1])


def _ssm_group(arow_ref, acol_ref, ldt_ref, bt_ref, cab_ref, dcol_ref, e, z_ref, chunks_per_seq):
    T, P, H = SSM_CHUNK, SSM_STATE, SSM_GROUP_CH
    TH = T * H
    hi = lax.Precision.HIGHEST
    dt = jnp.exp(ldt_ref[...])

    lam_r, lam_i = dt * arow_ref[0:1, :], dt * arow_ref[1:2, :]
    j0 = lax.broadcasted_iota(jnp.int32, (T, 2 * P), 0).astype(F32)
    pa0, pb0 = _cis(j0 * lam_r, j0 * lam_i)
    pa1, pb1 = _cmul(pa0, pb0, *_cis(lam_r, lam_i))
    over_h = lambda a: jnp.concatenate(
        [jnp.broadcast_to(a[j:j + 1, :], (H, 2 * P)) for j in range(T)], axis=0)
    ca, cb = jnp.tile(cab_ref[0], (T, 1)), jnp.tile(cab_ref[1], (T, 1))
    c_pow0 = over_h(pa0) * ca + over_h(pb0) * cb
    c_pow1 = over_h(pa1) * ca + over_h(pb1) * cb

    a_r, a_i = acol_ref[:, 0:1], acol_ref[:, 1:2]
    lr, li = dt * a_r, dt * a_i
    abar_r, abar_i = _cis(lr, li)
    nr, ni = abar_r - 1.0, abar_i
    den = a_r * a_r + a_i * a_i
    fr, fi = (nr * a_r + ni * a_i) / den, (ni * a_r - nr * a_i) / den
    b_r, b_i = bt_ref[0], bt_ref[1]
    bb_r, bb_i = fr * b_r - fi * b_i, fr * b_i + fi * b_r

    kcol = jnp.dot(c_pow0, jnp.concatenate([bb_r, bb_i], axis=0), precision=hi,
                   preferred_element_type=F32)
    lane_h = lax.broadcasted_iota(jnp.int32, (H, LANES), 1) % H
    skip = jnp.where(lane_h == lax.broadcasted_iota(jnp.int32, (H, LANES), 0), dcol_ref[...], 0.0)

    z_ref[TH:2 * TH, :] = kcol
    z_ref[TH:TH + H, :] = kcol[:H] + skip
    lane_group = lax.broadcasted_iota(jnp.int32, (1, LANES), 1) // H
    groups_per_block = LANES // H
    blocks = []
    for v in range(TH // LANES):
        blk = None
        for u in range(groups_per_block):
            s = v * groups_per_block + u
            piece = z_ref[TH - H * s:2 * TH - H * s, :]
            blk = piece if blk is None else jnp.where(lane_group == u, piece, blk)
        blocks.append(blk.astype(BF16))
    mt = jnp.concatenate(blocks, axis=1)

    expo = (groups_per_block - 1 - lane_group).astype(F32)
    wr, wi = _cis(lr * expo, li * expo)
    hop_r, hop_i = _cpow2(abar_r, abar_i, groups_per_block)
    w1_r, w1_i = [], []
    for v in range(TH // LANES):
        w1_r.insert(0, wr * bb_r - wi * bb_i)
        w1_i.insert(0, wr * bb_i + wi * bb_r)
        wr, wi = _cmul(wr, wi, hop_r, hop_i)
    w1t = jnp.concatenate([jnp.concatenate(w1_r, axis=1),
                           jnp.concatenate(w1_i, axis=1)], axis=0).astype(BF16)

    e = e.reshape(TH, e.shape[-1])
    y = jnp.dot(mt, e, preferred_element_type=F32)
    st = jnp.dot(w1t, e, preferred_element_type=F32)
    sr, si = st[:P], st[P:]
    pos = lax.broadcasted_iota(jnp.int32, sr.shape, 1) % chunks_per_seq

    def shifted(a, shift):
        return jnp.where(pos >= shift, pltpu.roll(a, shift, 1), 0.0)

    qr, qi = _cpow2(abar_r, abar_i, T)
    shift = 1
    while shift < chunks_per_seq:
        srs, sis = shifted(sr, shift), shifted(si, shift)
        sr, si = sr + qr * srs - qi * sis, si + qr * sis + qi * srs
        qr, qi = qr * qr - qi * qi, 2.0 * qr * qi
        shift *= 2
    x_prev = jnp.concatenate([shifted(sr, 1), shifted(si, 1)], axis=0).astype(BF16)
    y = y + jnp.dot(c_pow1.astype(BF16), x_prev, preferred_element_type=F32)
    return _gelu_tanh(y)


def _ssm_call(arow, acol, ldt, bt, cab, dcol, e, chunks_per_seq):
    T, _, NC = e.shape
    G = arow.shape[0]
    gps = SSM_GROUPS_PER_STEP
    assert G % gps == 0
    grp = lambda a: pl.BlockSpec((gps,) + a.shape[1:], lambda g: (g,) + (0,) * (a.ndim - 1))
    channels = pl.BlockSpec((T, gps * SSM_GROUP_CH, NC), lambda g: (0, g, 0))
    return pl.pallas_call(
        functools.partial(_ssm_kernel, chunks_per_seq=chunks_per_seq),
        grid=(G // gps,),
        in_specs=[grp(a) for a in (arow, acol, ldt, bt, cab, dcol)] + [channels],
        out_specs=channels,
        out_shape=jax.ShapeDtypeStruct((T, SSM_WIDTH, NC), F32),
        scratch_shapes=[pltpu.VMEM((gps, 2 * T * SSM_GROUP_CH, LANES), F32)],
        compiler_params=pltpu.CompilerParams(
            dimension_semantics=("arbitrary",), vmem_limit_bytes=VMEM_LIMIT),
        name="ssm",
    )(arow, acol, ldt, bt, cab, dcol, e)


def _ssm_param_layouts(a_re, a_im, log_dt, b_re, b_im, c_re, c_im, d_skip):
    G = a_re.shape[0]
    arow = jnp.stack([jnp.concatenate([a_re, a_re], -1), jnp.concatenate([a_im, a_im], -1)], 1)
    acol = jnp.stack([a_re, a_im], -1)
    reps = LANES // SSM_GROUP_CH
    bt = jnp.stack([jnp.tile(b_re, (1, 1, reps)), jnp.tile(b_im, (1, 1, reps))], 1)
    cab = jnp.stack([jnp.concatenate([c_re, -c_im], -1), jnp.concatenate([-c_im, -c_re], -1)], 1)
    return (arow.astype(F32), acol.astype(F32), log_dt.reshape(G, 1, 1).astype(F32),
            bt.astype(F32), cab.astype(F32), d_skip.reshape(G, SSM_GROUP_CH, 1).astype(F32))


def _tail_kernel(h1_ref, attn_ref, y_ref, p_ref, wglu_ref, bglu_ref, ga_ref, gs_ref,
                 wo_ref, g2_ref, w1_ref, w3_ref, w2_ref, gp_ref, wpg_ref,
                 wpp_ref, gf_ref, o_ref):
    y = y_ref[...]
    glu = y * _sigmoid(jnp.dot(y.astype(BF16), wglu_ref[...], preferred_element_type=F32)
                       + bglu_ref[...])
    an = _rms(attn_ref[...], ga_ref[...]).astype(BF16)
    sn = _rms(glu, gs_ref[...]).astype(BF16)
    h = (h1_ref[...] + jnp.dot(an, wo_ref[:ATTN_WIDTH, :], preferred_element_type=F32)
         + jnp.dot(sn, wo_ref[ATTN_WIDTH:, :], preferred_element_type=F32))
    h = h + 0.5 * _swiglu(_rms(h, g2_ref[...]).astype(BF16), w1_ref, w3_ref, w2_ref)
    gate = _sigmoid(jnp.dot(_rms(h, gp_ref[...]).astype(BF16), wpg_ref[...],
                            preferred_element_type=F32))
    h = h + gate * jnp.dot(p_ref[...].astype(BF16), wpp_ref[...], preferred_element_type=F32)
    o_ref[...] = _rms(h, gf_ref[...])


def _tail_call(h1, attn, y, p, *consts):
    B, L, D = h1.shape
    tm = TOKEN_TILE
    tile = lambda w: pl.BlockSpec((None, tm, w), lambda b, i: (b, i, 0))
    return pl.pallas_call(
        _tail_kernel,
        grid=(B, L // tm),
        in_specs=[tile(D), tile(ATTN_WIDTH), tile(SSM_WIDTH), tile(PLE_DIM)]
                 + [_const_spec(c.shape) for c in consts],
        out_specs=tile(D),
        out_shape=jax.ShapeDtypeStruct((B, L, D), F32),
        compiler_params=pltpu.CompilerParams(
            dimension_semantics=("arbitrary", "arbitrary"),
            vmem_limit_bytes=VMEM_LIMIT),
        name="tail",
    )(h1, attn, y, p, *consts)


def kernel(x, p, g_ffn1, w1_a, w3_a, w2_a, g_mix, w_in, b_f, a_re, a_im, log_dt, b_re, b_im, c_re, c_im, d_skip, w_glu, b_glu, g_attn_out, g_ssm_out, w_out, g_ffn2, w1_b, w3_b, w2_b, g_ple, w_ple_gate, w_ple_proj, g_final):
    B, L, D = x.shape
    assert D == D_MODEL and L % ATTN_TILE == 0 and L % TOKEN_TILE == 0 and L % SSM_CHUNK == 0
    assert g_ffn1.shape[0] == 1, "single layer"
    assert TOKEN_TILE == ATTN_TILE, "decay bias rows are relative to the kv tile start"
    row = lambda g: g.reshape(1, -1).astype(F32)
    bf = lambda w: w.astype(BF16)
    s_v, s_f = 3 * ATTN_WIDTH, 3 * ATTN_WIDTH + ATTN_HEADS
    w_in0 = w_in[0]
    scale = LOG2E / math.sqrt(HEAD_DIM)
    wqkv = jnp.concatenate([w_in0[:, ATTN_WIDTH:s_v], w_in0[:, :ATTN_WIDTH] * scale], axis=1)
    wf = jnp.pad(w_in0[:, s_v:s_f], ((0, 0), (0, LANES - ATTN_HEADS)))

    (h1, q, kt, v, s_in, ct, k_norm2), (w1_b16, w3_b16, w2_b16, w_out16, w_gate16) = _head_call(
        x, row(g_ffn1[0]), bf(w1_a[0]), bf(w3_a[0]), bf(w2_a[0]), row(g_mix[0]),
        bf(wqkv), bf(w_in0[:, s_f:]), bf(wf), b_f[0].reshape(ATTN_HEADS, 1).astype(F32),
        later=tuple(w.astype(F32) for w in (w1_b[0], w3_b[0], w2_b[0], w_out[0], w_ple_gate[0])))

    attn = _attn_call(q, kt, v, ct, k_norm2)

    T = SSM_CHUNK
    chunks_per_seq = L // T
    n_chunks = B * chunks_per_seq
    e = s_in.reshape(n_chunks, T, SSM_WIDTH).transpose(1, 2, 0)
    f = _ssm_call(*_ssm_param_layouts(a_re[0], a_im[0], log_dt[0], b_re[0], b_im[0],
                                      c_re[0], c_im[0], d_skip[0]),
                  e, chunks_per_seq)
    y = f.transpose(2, 0, 1).reshape(B, L, SSM_WIDTH)

    return _tail_call(
        h1, attn, y, p[0],
        bf(w_glu[0]), row(b_glu[0]), row(g_attn_out[0]), row(g_ssm_out[0]),
        w_out16, row(g_ffn2[0]),
        w1_b16, w3_b16, w2_b16, row(g_ple[0]), w_gate16,
        bf(w_ple_proj[0]), row(g_final))
```

```python
import functools
import math

import jax
import jax.numpy as jnp
from jax import lax
from jax.experimental import pallas as pl
from jax.experimental.pallas import tpu as pltpu

D_MODEL = 1024
ATTN_HEADS = 8
HEAD_DIM = 64
ATTN_WIDTH = ATTN_HEADS * HEAD_DIM
SSM_WIDTH = D_MODEL - ATTN_WIDTH
SSM_GROUP_CH = 16
SSM_GROUPS = SSM_WIDTH // SSM_GROUP_CH
SSM_STATE = 64
D_FF = 2816
PLE_DIM = 256
EPS = 1e-6

LANES = 128
BF16_SUBLANES = 16
HEAD_PAIR = 2 * HEAD_DIM
N_PAIRS = ATTN_HEADS // 2
FF_CHUNK = 256
TOKEN_TILE = 512
ATTN_TILE = 512
SSM_CHUNK = 32
SSM_GROUPS_PER_STEP = 2
NEG_BIG = -1e30
SKIP_LOG2 = 140.0
BOUND_SLACK_MUL = 1.001
BOUND_SLACK_ADD = 1.0
LOG2E = math.log2(math.e)
N_BIAS = 3
BIAS_ROWS = 8
VMEM_LIMIT = 56 * 1024 * 1024

BF16 = jnp.bfloat16
F32 = jnp.float32


def _rms(x, g):
    ms = jnp.mean(x * x, axis=-1, keepdims=True)
    return x * lax.rsqrt(ms + EPS) * g


def _sigmoid(x):
    return 1.0 / (1.0 + jnp.exp(-x))


def _swiglu(xn, w1_ref, w3_ref, w2_ref):
    acc = None
    for c in range(D_FF // FF_CHUNK):
        sl = slice(c * FF_CHUNK, (c + 1) * FF_CHUNK)
        a = jnp.dot(xn, w1_ref[:, sl], preferred_element_type=F32)
        b = jnp.dot(xn, w3_ref[:, sl], preferred_element_type=F32)
        gated = (a * _sigmoid(a) * b).astype(BF16)
        part = jnp.dot(gated, w2_ref[sl, :], preferred_element_type=F32)
        acc = part if acc is None else acc + part
    return acc


def _const_spec(shape):
    nd = len(shape)
    return pl.BlockSpec(shape, lambda *_: (0,) * nd, pipeline_mode=pl.Buffered(1))


def _head_kernel(*refs, n_later):
    (x_ref, g1_ref, w1_ref, w3_ref, w2_ref, gm_ref, wkvq_ref, ws_ref, wf_ref, bf_ref) = refs[:10]
    later_in = refs[10:10 + n_later]
    h1_ref, q_ref, kt_ref, v_ref, s_ref, ct_ref, kn_ref = refs[10 + n_later:17 + n_later]
    later_out = refs[17 + n_later:17 + 2 * n_later]
    carry_ref = refs[-1]
    for src, dst in zip(later_in, later_out):
        dst[...] = src[...].astype(BF16)
    tm = x_ref.shape[0]
    x = x_ref[...]
    h1 = x + 0.5 * _swiglu(_rms(x, g1_ref[...]).astype(BF16), w1_ref, w3_ref, w2_ref)
    h1_ref[...] = h1
    un = _rms(h1, gm_ref[...]).astype(BF16)
    project = lambda w: jnp.dot(un, w, preferred_element_type=F32)
    zf = project(wf_ref[...])
    kv = project(wkvq_ref[:, :2 * ATTN_WIDTH])
    zft = zf.T[:ATTN_HEADS, :] + bf_ref[...]
    logf = jnp.minimum(zft, 0.0) - jnp.log1p(jnp.exp(-jnp.abs(zft)))
    lane = lax.broadcasted_iota(jnp.int32, logf.shape, 1)
    c = logf
    shift = 1
    while shift < tm:
        c = c + jnp.where(lane >= shift, pltpu.roll(c, shift, 1), 0.0)
        shift *= 2

    @pl.when(pl.program_id(1) == 0)
    def _():
        carry_ref[...] = jnp.zeros_like(carry_ref)

    c_abs = c + carry_ref[:, 0:1]
    ct_ref[...] = c_abs * LOG2E
    carry_ref[...] = jnp.broadcast_to(c_abs[:, tm - 1:tm], carry_ref.shape)

    rel = (c - c[:, 0:1]) * LOG2E
    hi = rel.astype(BF16).astype(F32)
    mid = (rel - hi).astype(BF16).astype(F32)
    lo = (rel - hi - mid).astype(BF16).astype(F32)
    kt = kv[:, :ATTN_WIDTH].astype(BF16).astype(F32).T
    vv = kv[:, ATTN_WIDTH:]
    k_sq = (kt * kt).reshape(ATTN_HEADS, HEAD_DIM, tm)
    kn_ref[...] = jnp.broadcast_to(
        jnp.max(jnp.sum(k_sq, axis=1), axis=-1, keepdims=True), kn_ref.shape)
    sub = lax.broadcasted_iota(jnp.int32, (BIAS_ROWS, tm), 0)
    zeros = jnp.zeros((HEAD_DIM - BIAS_ROWS, tm), F32)
    vlane = lax.broadcasted_iota(jnp.int32, (tm, HEAD_PAIR), 1)
    for h in range(ATTN_HEADS):
        bias = jnp.where(sub == 0, -hi[h:h + 1],
                         jnp.where(sub == 1, -mid[h:h + 1],
                                   jnp.where(sub == 2, -lo[h:h + 1], 0.0)))
        k_h = kt[h * HEAD_DIM:(h + 1) * HEAD_DIM]
        vp = vv[:, (h // 2) * HEAD_PAIR:(h // 2 + 1) * HEAD_PAIR]
        if h % 2 == 0:
            kt_ref[h] = jnp.concatenate([k_h, bias, zeros], axis=0).astype(BF16)
            v_ref[h] = jnp.where(vlane < HEAD_DIM, vp,
                                 jnp.where(vlane == HEAD_DIM, 1.0, 0.0)).astype(BF16)
        else:
            kt_ref[h] = jnp.concatenate([bias, zeros, k_h], axis=0).astype(BF16)
            v_ref[h] = jnp.where(vlane >= HEAD_DIM, vp,
                                 jnp.where(vlane == 0, 1.0, 0.0)).astype(BF16)

    q_ref[...] = project(wkvq_ref[:, 2 * ATTN_WIDTH:]).astype(BF16)
    s_ref[...] = project(ws_ref[...]).astype(BF16)


def _slab_spec(w, n_steps, steps_per_batch):
    rows = next(r for r in range(BF16_SUBLANES, w.shape[0] + 1, BF16_SUBLANES)
                if w.shape[0] % r == 0 and w.shape[0] // r <= n_steps)
    last = w.shape[0] // rows - 1
    return pl.BlockSpec((rows, w.shape[1]),
                        lambda b, i: (jnp.minimum(b * steps_per_batch + i, last), 0))


def _head_call(x, g1, w1, w3, w2, gm, wqkv, ws, wf, bf, later):
    B, L, D = x.shape
    tm = TOKEN_TILE
    tile = lambda w: pl.BlockSpec((None, tm, w), lambda b, i: (b, i, 0))
    slabs = [_slab_spec(w, B * (L // tm), L // tm) for w in later]
    out_shape = (
        jax.ShapeDtypeStruct((B, L, D), F32),
        jax.ShapeDtypeStruct((B, L, ATTN_WIDTH), BF16),
        jax.ShapeDtypeStruct((B, ATTN_HEADS, HEAD_PAIR, L), BF16),
        jax.ShapeDtypeStruct((B, ATTN_HEADS, L, HEAD_PAIR), BF16),
        jax.ShapeDtypeStruct((B, L, SSM_WIDTH), BF16),
        jax.ShapeDtypeStruct((B, ATTN_HEADS, L), F32),
        jax.ShapeDtypeStruct((B, L // tm, ATTN_HEADS, LANES), F32),
    ) + tuple(jax.ShapeDtypeStruct(w.shape, BF16) for w in later)
    outs = pl.pallas_call(
        functools.partial(_head_kernel, n_later=len(later)),
        grid=(B, L // tm),
        in_specs=[tile(D), _const_spec(g1.shape), _const_spec(w1.shape),
                  _const_spec(w3.shape), _const_spec(w2.shape), _const_spec(gm.shape),
                  _const_spec(wqkv.shape), _const_spec(ws.shape), _const_spec(wf.shape),
                  _const_spec(bf.shape)] + slabs,
        out_specs=(tile(D), tile(ATTN_WIDTH),
                   pl.BlockSpec((None, ATTN_HEADS, HEAD_PAIR, tm), lambda b, i: (b, 0, 0, i)),
                   pl.BlockSpec((None, ATTN_HEADS, tm, HEAD_PAIR), lambda b, i: (b, 0, i, 0)),
                   tile(SSM_WIDTH),
                   pl.BlockSpec((None, ATTN_HEADS, tm), lambda b, i: (b, 0, i)),
                   pl.BlockSpec((None, None, ATTN_HEADS, LANES), lambda b, i: (b, i, 0, 0)))
                  + tuple(slabs),
        out_shape=out_shape,
        scratch_shapes=[pltpu.VMEM((ATTN_HEADS, LANES), F32)],
        compiler_params=pltpu.CompilerParams(
            dimension_semantics=("arbitrary", "arbitrary"),
            vmem_limit_bytes=VMEM_LIMIT),
        name="head",
    )(x, g1, w1, w3, w2, gm, wqkv, ws, wf, bf, *later)
    return outs[:7], outs[7:]


def _attn_kernel(q_ref, *refs):
    _attn_prepare(0, q_ref, *refs)

    def q_tile(n, carry):
        _attn_q_tile(n, q_ref, *refs)
        return carry

    lax.fori_loop(0, q_ref.shape[0] // ATTN_TILE, q_tile, 0)


def _tile_start(i):
    return pl.multiple_of(i * ATTN_TILE, ATTN_TILE)


def _q_heads(q):
    lane = lax.broadcasted_iota(jnp.int32, (1, HEAD_PAIR), 1)
    first = lane < HEAD_DIM
    ones_even = jnp.where((lane >= HEAD_DIM) & (lane < HEAD_DIM + N_BIAS), 1.0, 0.0).astype(BF16)
    ones_odd = jnp.where(lane < N_BIAS, 1.0, 0.0).astype(BF16)
    return first, (jnp.where(first, q, ones_even), jnp.where(first, ones_odd, q))


def _attn_prepare(n, q_ref, kt_ref, v_ref, c_ref, st_ref, o_ref, s_e0, s_e1, s_o0, s_o1, m_ref,
                  acc_ref, count_ref):
    t = ATTN_TILE
    q = q_ref[pl.ds(_tile_start(n), t), :]
    first, q_heads = _q_heads(q)
    q_sq = q.astype(F32) * q.astype(F32)
    row = lax.broadcasted_iota(jnp.int32, (t, t), 0)
    col = lax.broadcasted_iota(jnp.int32, (t, t), 1)
    tile_id = lax.broadcasted_iota(jnp.int32, (1, LANES), 1)
    needed = tile_id < 0
    for h, s_ref in enumerate((s_o0, s_o1)):
        s = jnp.where(col <= row, jnp.dot(q_heads[h], kt_ref[h, :, pl.ds(_tile_start(n), t)],
                                          preferred_element_type=F32), NEG_BIG)
        s_ref[...] = s
        m = jnp.max(s, axis=-1, keepdims=True)
        m_ref[h] = m
        q_norm2 = jnp.max(jnp.sum(jnp.where(first == (h == 0), q_sq, 0.0), axis=-1, keepdims=True),
                          axis=0, keepdims=True)
        k_norm2, c_next = st_ref[h:h + 1, :], st_ref[2 + h:3 + h, :]
        c_q = c_ref[h:h + 1, pl.ds(_tile_start(n), LANES)][:, 0:1]
        reach = jnp.sqrt(q_norm2 * k_norm2) * BOUND_SLACK_MUL + BOUND_SLACK_ADD + c_q - c_next
        needed = needed | (reach - jnp.min(m, axis=0, keepdims=True) >= -SKIP_LOG2)
    first_needed = jnp.min(jnp.where(needed & (tile_id < n), tile_id, n).astype(F32))
    count_ref[0] = n - first_needed.astype(jnp.int32)


def _attn_q_tile(n, q_ref, kt_ref, v_ref, c_ref, st_ref, o_ref, s_e0, s_e1, s_o0, s_o1, m_ref,
                 acc_ref, count_ref):
    t = ATTN_TILE
    count = count_ref[0]
    start = _tile_start
    first, q_heads = _q_heads(q_ref[pl.ds(start(n), t), :])
    s_buf = ((s_e0, s_e1), (s_o0, s_o1))
    c_q = [c_ref[h:h + 1, pl.ds(start(n), LANES)][:, 0:1] for h in range(2)]

    def qk(h, kv):
        return jnp.dot(q_heads[h], kt_ref[h, :, pl.ds(start(kv), t)], preferred_element_type=F32)

    def consume(h, s, kv):
        d = c_ref[h:h + 1, pl.ds(start(kv), LANES)][:, 0:1] - c_q[h]
        m_old = m_ref[h]
        m_new = jnp.maximum(m_old, jnp.max(s, axis=-1, keepdims=True) - d)
        p = jnp.exp2(s - (m_new + d)).astype(BF16)
        m_ref[h] = m_new
        acc_ref[h] = jnp.exp2(m_old - m_new) * acc_ref[h] + jnp.dot(
            p, v_ref[h, pl.ds(start(kv), t), :], preferred_element_type=F32)

    def step(kv, par):
        for h in range(2):
            s_buf[1 - par][h][...] = qk(h, kv - 1)
        for h in range(2):
            consume(h, s_buf[par][h][...], kv)

    for h in range(2):
        s_buf[0][h][...] = qk(h, jnp.maximum(n - 1, 0))
    for h in range(2):
        acc_ref[h] = jnp.dot(jnp.exp2(s_buf[1][h][...] - m_ref[h]).astype(BF16),
                             v_ref[h, pl.ds(start(n), t), :], preferred_element_type=F32)

    def pair(i, carry):
        step(n - 1 - 2 * i, 0)
        step(n - 2 - 2 * i, 1)
        return carry

    n_pairs = jnp.maximum(count - 1, 0) // 2
    lax.fori_loop(0, n_pairs, pair, 0)
    left = count - 2 * n_pairs
    last = n - count

    @pl.when(left == 2)
    def _():
        step(last + 1, 0)
        for h in range(2):
            consume(h, s_buf[1][h][...], last)

    @pl.when(left == 1)
    def _():
        for h in range(2):
            consume(h, s_buf[0][h][...], last)

    acc0, acc1 = acc_ref[0], acc_ref[1]
    o_ref[pl.ds(start(n), t), :] = jnp.where(first, acc0 / acc0[:, HEAD_DIM:HEAD_DIM + 1],
                                             acc1 / acc1[:, 0:1])
    n_tiles = q_ref.shape[0] // t
    _attn_prepare(jnp.minimum(n + 1, n_tiles - 1), q_ref, kt_ref, v_ref, c_ref, st_ref, o_ref,
                  s_e0, s_e1, s_o0, s_o1, m_ref, acc_ref, count_ref)


def _attn_call(q, kt, v, ct, k_norm2):
    B, L, _ = q.shape
    t = ATTN_TILE
    n_tiles = L // t
    assert n_tiles <= LANES
    c4 = ct.reshape(B, N_PAIRS, 2, L)
    c_next = jnp.roll(ct[:, :, ::t], -1, axis=-1)
    per_tile = jnp.stack([k_norm2[..., 0].transpose(0, 2, 1), c_next], axis=1)
    per_tile = jnp.pad(per_tile, ((0, 0),) * 3 + ((0, LANES - n_tiles),))
    stats = (per_tile.reshape(B, 2, N_PAIRS, 2, LANES).transpose(0, 2, 1, 3, 4)
             .reshape(B, N_PAIRS, 4, LANES))
    pair_block = lambda *shape: pl.BlockSpec((None, 2) + shape, lambda b, p: (b, p, 0, 0))
    lanes_of_pair = pl.BlockSpec((None, L, HEAD_PAIR), lambda b, p: (b, 0, p))
    return pl.pallas_call(
        _attn_kernel,
        grid=(B, N_PAIRS),
        in_specs=[lanes_of_pair, pair_block(HEAD_PAIR, L), pair_block(L, HEAD_PAIR),
                  pl.BlockSpec((None, None, 2, L), lambda b, p: (b, p, 0, 0)),
                  pl.BlockSpec((None, None, 4, LANES), lambda b, p: (b, p, 0, 0))],
        out_specs=lanes_of_pair,
        out_shape=jax.ShapeDtypeStruct((B, L, ATTN_WIDTH), F32),
        scratch_shapes=[pltpu.VMEM((t, t), F32)] * 4
                       + [pltpu.VMEM((2, t, 1), F32), pltpu.VMEM((2, t, HEAD_PAIR), F32),
                          pltpu.SMEM((1,), jnp.int32)],
        compiler_params=pltpu.CompilerParams(
            dimension_semantics=("arbitrary", "arbitrary"),
            vmem_limit_bytes=VMEM_LIMIT),
        name="attn",
    )(q, kt, v, c4, stats)


def _gelu_tanh(x):
    return 0.5 * x * (1.0 + jnp.tanh(math.sqrt(2.0 / math.pi) * (x + 0.044715 * (x * x * x))))


def _cis(mag_arg, ang):
    mag = jnp.exp(mag_arg)
    return mag * jnp.cos(ang), mag * jnp.sin(ang)


def _cmul(ar, ai, br, bi):
    return ar * br - ai * bi, ar * bi + ai * br


def _cpow2(zr, zi, n):
    assert n & (n - 1) == 0
    while n > 1:
        zr, zi = zr * zr - zi * zi, 2.0 * zr * zi
        n //= 2
    return zr, zi


def _ssm_kernel(arow_ref, acol_ref, ldt_ref, bt_ref, cab_ref, dcol_ref, e_ref, f_ref, z_ref,
                *, chunks_per_seq):
    H = SSM_GROUP_CH

    @pl.when(pl.program_id(0) == 0)
    def _():
        z_ref[:, 0:SSM_CHUNK * H, :] = jnp.zeros((z_ref.shape[0], SSM_CHUNK * H, LANES), F32)

    for gi in range(SSM_GROUPS_PER_STEP):
        y = _ssm_group(arow_ref.at[gi], acol_ref.at[gi], ldt_ref.at[gi], bt_ref.at[gi], cab_ref.at[gi],
                       dcol_ref.at[gi], e_ref[:, gi * H:(gi + 1) * H, :], z_ref.at[gi], chunks_per_seq)
        f_ref[:, gi * H:(gi + 1) * H, :] = y.reshape(SSM_CHUNK, H, y.shape[-1])


def _ssm_group(arow_ref, acol_ref, ldt_ref, bt_ref, cab_ref, dcol_ref, e, z_ref, chunks_per_seq):
    T, P, H = SSM_CHUNK, SSM_STATE, SSM_GROUP_CH
    TH = T * H
    hi = lax.Precision.HIGHEST
    dt = jnp.exp(ldt_ref[...])

    lam_r, lam_i = dt * arow_ref[0:1, :], dt * arow_ref[1:2, :]
    j0 = lax.broadcasted_iota(jnp.int32, (T, 2 * P), 0).astype(F32)
    pa0, pb0 = _cis(j0 * lam_r, j0 * lam_i)
    pa1, pb1 = _cmul(pa0, pb0, *_cis(lam_r, lam_i))
    over_h = lambda a: jnp.concatenate(
        [jnp.broadcast_to(a[j:j + 1, :], (H, 2 * P)) for j in range(T)], axis=0)
    ca, cb = jnp.tile(cab_ref[0], (T, 1)), jnp.tile(cab_ref[1], (T, 1))
    c_pow0 = over_h(pa0) * ca + over_h(pb0) * cb
    c_pow1 = over_h(pa1) * ca + over_h(pb1) * cb

    a_r, a_i = acol_ref[:, 0:1], acol_ref[:, 1:2]
    lr, li = dt * a_r, dt * a_i
    abar_r, abar_i = _cis(lr, li)
    nr, ni = abar_r - 1.0, abar_i
    den = a_r * a_r + a_i * a_i
    fr, fi = (nr * a_r + ni * a_i) / den, (ni * a_r - nr * a_i) / den
    b_r, b_i = bt_ref[0], bt_ref[1]
    bb_r, bb_i = fr * b_r - fi * b_i, fr * b_i + fi * b_r

    kcol = jnp.dot(c_pow0, jnp.concatenate([bb_r, bb_i], axis=0), precision=hi,
                   preferred_element_type=F32)
    lane_h = lax.broadcasted_iota(jnp.int32, (H, LANES), 1) % H
    skip = jnp.where(lane_h == lax.broadcasted_iota(jnp.int32, (H, LANES), 0), dcol_ref[...], 0.0)

    z_ref[TH:2 * TH, :] = kcol
    z_ref[TH:TH + H, :] = kcol[:H] + skip
    lane_group = lax.broadcasted_iota(jnp.int32, (1, LANES), 1) // H
    groups_per_block = LANES // H
    blocks = []
    for v in range(TH // LANES):
        blk = None
        for u in range(groups_per_block):
            s = v * groups_per_block + u
            piece = z_ref[TH - H * s:2 * TH - H * s, :]
            blk = piece if blk is None else jnp.where(lane_group == u, piece, blk)
        blocks.append(blk.astype(BF16))
    mt = jnp.concatenate(blocks, axis=1)

    expo = (groups_per_block - 1 - lane_group).astype(F32)
    wr, wi = _cis(lr * expo, li * expo)
    hop_r, hop_i = _cpow2(abar_r, abar_i, groups_per_block)
    w1_r, w1_i = [], []
    for v in range(TH // LANES):
        w1_r.insert(0, wr * bb_r - wi * bb_i)
        w1_i.insert(0, wr * bb_i + wi * bb_r)
        wr, wi = _cmul(wr, wi, hop_r, hop_i)
    w1t = jnp.concatenate([jnp.concatenate(w1_r, axis=1),
                           jnp.concatenate(w1_i, axis=1)], axis=0).astype(BF16)

    e = e.reshape(TH, e.shape[-1])
    y = jnp.dot(mt, e, preferred_element_type=F32)
    st = jnp.dot(w1t, e, preferred_element_type=F32)
    sr, si = st[:P], st[P:]
    pos = lax.broadcasted_iota(jnp.int32, sr.shape, 1) % chunks_per_seq

    def shifted(a, shift):
        return jnp.where(pos >= shift, pltpu.roll(a, shift, 1), 0.0)

    qr, qi = _cpow2(abar_r, abar_i, T)
    shift = 1
    while shift < chunks_per_seq:
        srs, sis = shifted(sr, shift), shifted(si, shift)
        sr, si = sr + qr * srs - qi * sis, si + qr * sis + qi * srs
        qr, qi = qr * qr - qi * qi, 2.0 * qr * qi
        shift *= 2
    x_prev = jnp.concatenate([shifted(sr, 1), shifted(si, 1)], axis=0).astype(BF16)
    y = y + jnp.dot(c_pow1.astype(BF16), x_prev, preferred_element_type=F32)
    return _gelu_tanh(y)


def _ssm_call(arow, acol, ldt, bt, cab, dcol, e, chunks_per_seq):
    T, _, NC = e.shape
    G = arow.shape[0]
    gps = SSM_GROUPS_PER_STEP
    assert G % gps == 0
    grp = lambda a: pl.BlockSpec((gps,) + a.shape[1:], lambda g: (g,) + (0,) * (a.ndim - 1))
    channels = pl.BlockSpec((T, gps * SSM_GROUP_CH, NC), lambda g: (0, g, 0))
    return pl.pallas_call(
        functools.partial(_ssm_kernel, chunks_per_seq=chunks_per_seq),
        grid=(G // gps,),
        in_specs=[grp(a) for a in (arow, acol, ldt, bt, cab, dcol)] + [channels],
        out_specs=channels,
        out_shape=jax.ShapeDtypeStruct((T, SSM_WIDTH, NC), F32),
        scratch_shapes=[pltpu.VMEM((gps, 2 * T * SSM_GROUP_CH, LANES), F32)],
        compiler_params=pltpu.CompilerParams(
            dimension_semantics=("arbitrary",), vmem_limit_bytes=VMEM_LIMIT),
        name="ssm",
    )(arow, acol, ldt, bt, cab, dcol, e)


def _ssm_param_layouts(a_re, a_im, log_dt, b_re, b_im, c_re, c_im, d_skip):
    G = a_re.shape[0]
    arow = jnp.stack([jnp.concatenate([a_re, a_re], -1), jnp.concatenate([a_im, a_im], -1)], 1)
    acol = jnp.stack([a_re, a_im], -1)
    reps = LANES // SSM_GROUP_CH
    bt = jnp.stack([jnp.tile(b_re, (1, 1, reps)), jnp.tile(b_im, (1, 1, reps))], 1)
    cab = jnp.stack([jnp.concatenate([c_re, -c_im], -1), jnp.concatenate([-c_im, -c_re], -1)], 1)
    return (arow.astype(F32), acol.astype(F32), log_dt.reshape(G, 1, 1).astype(F32),
            bt.astype(F32), cab.astype(F32), d_skip.reshape(G, SSM_GROUP_CH, 1).astype(F32))


def _tail_kernel(h1_ref, attn_ref, y_ref, p_ref, wglu_ref, bglu_ref, ga_ref, gs_ref,
                 wo_ref, g2_ref, w1_ref, w3_ref, w2_ref, gp_ref, wpg_ref,
                 wpp_ref, gf_ref, o_ref):
    y = y_ref[...]
    glu = y * _sigmoid(jnp.dot(y.astype(BF16), wglu_ref[...], preferred_element_type=F32)
                       + bglu_ref[...])
    an = _rms(attn_ref[...], ga_ref[...]).astype(BF16)
    sn = _rms(glu, gs_ref[...]).astype(BF16)
    h = (h1_ref[...] + jnp.dot(an, wo_ref[:ATTN_WIDTH, :], preferred_element_type=F32)
         + jnp.dot(sn, wo_ref[ATTN_WIDTH:, :], preferred_element_type=F32))
    h = h + 0.5 * _swiglu(_rms(h, g2_ref[...]).astype(BF16), w1_ref, w3_ref, w2_ref)
    gate = _sigmoid(jnp.dot(_rms(h, gp_ref[...]).astype(BF16), wpg_ref[...],
                            preferred_element_type=F32))
    h = h + gate * jnp.dot(p_ref[...].astype(BF16), wpp_ref[...], preferred_element_type=F32)
    o_ref[...] = _rms(h, gf_ref[...])


def _tail_call(h1, attn, y, p, *consts):
    B, L, D = h1.shape
    tm = TOKEN_TILE
    tile = lambda w: pl.BlockSpec((None, tm, w), lambda b, i: (b, i, 0))
    return pl.pallas_call(
        _tail_kernel,
        grid=(B, L // tm),
        in_specs=[tile(D), tile(ATTN_WIDTH), tile(SSM_WIDTH), tile(PLE_DIM)]
                 + [_const_spec(c.shape) for c in consts],
        out_specs=tile(D),
        out_shape=jax.ShapeDtypeStruct((B, L, D), F32),
        compiler_params=pltpu.CompilerParams(
            dimension_semantics=("arbitrary", "arbitrary"),
            vmem_limit_bytes=VMEM_LIMIT),
        name="tail",
    )(h1, attn, y, p, *consts)


def kernel(x, p, g_ffn1, w1_a, w3_a, w2_a, g_mix, w_in, b_f, a_re, a_im, log_dt, b_re, b_im, c_re, c_im, d_skip, w_glu, b_glu, g_attn_out, g_ssm_out, w_out, g_ffn2, w1_b, w3_b, w2_b, g_ple, w_ple_gate, w_ple_proj, g_final):
    B, L, D = x.shape
    assert D == D_MODEL and L % ATTN_TILE == 0 and L % TOKEN_TILE == 0 and L % SSM_CHUNK == 0
    assert g_ffn1.shape[0] == 1, "single layer"
    assert TOKEN_TILE == ATTN_TILE, "decay bias rows are relative to the kv tile start"
    row = lambda g: g.reshape(1, -1).astype(F32)
    bf = lambda w: w.astype(BF16)
    s_v, s_f = 3 * ATTN_WIDTH, 3 * ATTN_WIDTH + ATTN_HEADS
    w_in0 = w_in[0]
    scale = LOG2E / math.sqrt(HEAD_DIM)
    wqkv = jnp.concatenate([w_in0[:, ATTN_WIDTH:s_v], w_in0[:, :ATTN_WIDTH] * scale], axis=1)
    wf = jnp.pad(w_in0[:, s_v:s_f], ((0, 0), (0, LANES - ATTN_HEADS)))

    (h1, q, kt, v, s_in, ct, k_norm2), (w1_b16, w3_b16, w2_b16, w_out16, w_gate16) = _head_call(
        x, row(g_ffn1[0]), bf(w1_a[0]), bf(w3_a[0]), bf(w2_a[0]), row(g_mix[0]),
        bf(wqkv), bf(w_in0[:, s_f:]), bf(wf), b_f[0].reshape(ATTN_HEADS, 1).astype(F32),
        later=tuple(w.astype(F32) for w in (w1_b[0], w3_b[0], w2_b[0], w_out[0], w_ple_gate[0])))

    T = SSM_CHUNK
    chunks_per_seq = L // T
    n_chunks = B * chunks_per_seq
    e = s_in.reshape(n_chunks, T, SSM_WIDTH).transpose(1, 2, 0)
    f = _ssm_call(*_ssm_param_layouts(a_re[0], a_im[0], log_dt[0], b_re[0], b_im[0],
                                      c_re[0], c_im[0], d_skip[0]),
                  e, chunks_per_seq)
    y = f.transpose(2, 0, 1).reshape(B, L, SSM_WIDTH)

    attn = _attn_call(q, kt, v, ct, k_norm2)

    return _tail_call(
        h1, attn, y, p[0],
        bf(w_glu[0]), row(b_glu[0]), row(g_attn_out[0]), row(g_ssm_out[0]),
        w_out16, row(g_ffn2[0]),
        w1_b16, w3_b16, w2_b16, row(g_ple[0]), w_gate16,
        bf(w_ple_proj[0]), row(g_final))
```

```python
import functools
import math

import jax
import jax.numpy as jnp
from jax import lax
from jax.experimental import pallas as pl
from jax.experimental.pallas import tpu as pltpu

D_MODEL = 1024
ATTN_HEADS = 8
HEAD_DIM = 64
ATTN_WIDTH = ATTN_HEADS * HEAD_DIM
SSM_WIDTH = D_MODEL - ATTN_WIDTH
SSM_GROUP_CH = 16
SSM_GROUPS = SSM_WIDTH // SSM_GROUP_CH
SSM_STATE = 64
D_FF = 2816
PLE_DIM = 256
EPS = 1e-6

LANES = 128
BF16_SUBLANES = 16
HEAD_PAIR = 2 * HEAD_DIM
N_PAIRS = ATTN_HEADS // 2
FF_CHUNK = 256
TOKEN_TILE = 512
ATTN_TILE = 512
SSM_CHUNK = 32
SSM_GROUPS_PER_STEP = 2
NEG_BIG = -1e30
SKIP_LOG2 = 140.0
BOUND_SLACK_MUL = 1.001
BOUND_SLACK_ADD = 1.0
LOG2E = math.log2(math.e)
N_BIAS = 3
BIAS_ROWS = 8
VMEM_LIMIT = 56 * 1024 * 1024

BF16 = jnp.bfloat16
F32 = jnp.float32


def _rms(x, g):
    ms = jnp.mean(x * x, axis=-1, keepdims=True)
    return x * lax.rsqrt(ms + EPS) * g


def _sigmoid(x):
    return 1.0 / (1.0 + jnp.exp(-x))


def _swiglu(xn, w1_ref, w3_ref, w2_ref):
    acc = None
    for c in range(D_FF // FF_CHUNK):
        sl = slice(c * FF_CHUNK, (c + 1) * FF_CHUNK)
        a = jnp.dot(xn, w1_ref[:, sl], preferred_element_type=F32)
        b = jnp.dot(xn, w3_ref[:, sl], preferred_element_type=F32)
        gated = (a * _sigmoid(a) * b).astype(BF16)
        part = jnp.dot(gated, w2_ref[sl, :], preferred_element_type=F32)
        acc = part if acc is None else acc + part
    return acc


def _const_spec(shape):
    nd = len(shape)
    return pl.BlockSpec(shape, lambda *_: (0,) * nd, pipeline_mode=pl.Buffered(1))


def _head_kernel(*refs, n_later):
    (x_ref, g1_ref, w1_ref, w3_ref, w2_ref, gm_ref, win_ref, ws_ref, bf_ref) = refs[:9]
    later_in = refs[9:9 + n_later]
    h1_ref, q_ref, kt_ref, v_ref, s_ref, ct_ref, kn_ref = refs[9 + n_later:16 + n_later]
    later_out = refs[16 + n_later:16 + 2 * n_later]
    carry_ref = refs[-1]
    for src, dst in zip(later_in, later_out):
        dst[...] = src[...].astype(BF16)
    tm = x_ref.shape[0]
    x = x_ref[...]
    h1 = x + 0.5 * _swiglu(_rms(x, g1_ref[...]).astype(BF16), w1_ref, w3_ref, w2_ref)
    h1_ref[...] = h1
    un = _rms(h1, gm_ref[...]).astype(BF16)
    project = lambda w: jnp.dot(un, w, preferred_element_type=F32)
    zf = project(win_ref[:, 3 * ATTN_WIDTH:3 * ATTN_WIDTH + LANES])
    kv = project(win_ref[:, ATTN_WIDTH:3 * ATTN_WIDTH])
    zft = zf.T[:ATTN_HEADS, :] + bf_ref[...]
    logf = jnp.minimum(zft, 0.0) - jnp.log1p(jnp.exp(-jnp.abs(zft)))
    lane = lax.broadcasted_iota(jnp.int32, logf.shape, 1)
    c = logf
    shift = 1
    while shift < tm:
        c = c + jnp.where(lane >= shift, pltpu.roll(c, shift, 1), 0.0)
        shift *= 2

    @pl.when(pl.program_id(1) == 0)
    def _():
        carry_ref[...] = jnp.zeros_like(carry_ref)

    c_abs = c + carry_ref[:, 0:1]
    ct_ref[...] = c_abs * LOG2E
    carry_ref[...] = jnp.broadcast_to(c_abs[:, tm - 1:tm], carry_ref.shape)

    rel = (c - c[:, 0:1]) * LOG2E
    hi = rel.astype(BF16).astype(F32)
    mid = (rel - hi).astype(BF16).astype(F32)
    lo = (rel - hi - mid).astype(BF16).astype(F32)
    kt = kv[:, :ATTN_WIDTH].astype(BF16).astype(F32).T
    vv = kv[:, ATTN_WIDTH:]
    k_sq = (kt * kt).reshape(ATTN_HEADS, HEAD_DIM, tm)
    kn_ref[...] = jnp.broadcast_to(
        jnp.max(jnp.sum(k_sq, axis=1), axis=-1, keepdims=True), kn_ref.shape)
    sub = lax.broadcasted_iota(jnp.int32, (BIAS_ROWS, tm), 0)
    zeros = jnp.zeros((HEAD_DIM - BIAS_ROWS, tm), F32)
    vlane = lax.broadcasted_iota(jnp.int32, (tm, HEAD_PAIR), 1)
    for h in range(ATTN_HEADS):
        bias = jnp.where(sub == 0, -hi[h:h + 1],
                         jnp.where(sub == 1, -mid[h:h + 1],
                                   jnp.where(sub == 2, -lo[h:h + 1], 0.0)))
        k_h = kt[h * HEAD_DIM:(h + 1) * HEAD_DIM]
        vp = vv[:, (h // 2) * HEAD_PAIR:(h // 2 + 1) * HEAD_PAIR]
        if h % 2 == 0:
            kt_ref[h] = jnp.concatenate([k_h, bias, zeros], axis=0).astype(BF16)
            v_ref[h] = jnp.where(vlane < HEAD_DIM, vp,
                                 jnp.where(vlane == HEAD_DIM, 1.0, 0.0)).astype(BF16)
        else:
            kt_ref[h] = jnp.concatenate([bias, zeros, k_h], axis=0).astype(BF16)
            v_ref[h] = jnp.where(vlane >= HEAD_DIM, vp,
                                 jnp.where(vlane == 0, 1.0, 0.0)).astype(BF16)

    q_ref[...] = (project(win_ref[:, :ATTN_WIDTH]) * (LOG2E / math.sqrt(HEAD_DIM))).astype(BF16)
    s_ref[...] = project(ws_ref[...]).astype(BF16)


def _slab_spec(w, n_steps, steps_per_batch):
    rows = next(r for r in range(BF16_SUBLANES, w.shape[0] + 1, BF16_SUBLANES)
                if w.shape[0] % r == 0 and w.shape[0] // r <= n_steps)
    last = w.shape[0] // rows - 1
    return pl.BlockSpec((rows, w.shape[1]),
                        lambda b, i: (jnp.minimum(b * steps_per_batch + i, last), 0))


def _head_call(x, g1, w1, w3, w2, gm, w_in, ws, bf, later):
    B, L, D = x.shape
    tm = TOKEN_TILE
    tile = lambda w: pl.BlockSpec((None, tm, w), lambda b, i: (b, i, 0))
    slabs = [_slab_spec(w, B * (L // tm), L // tm) for w in later]
    out_shape = (
        jax.ShapeDtypeStruct((B, L, D), F32),
        jax.ShapeDtypeStruct((B, L, ATTN_WIDTH), BF16),
        jax.ShapeDtypeStruct((B, ATTN_HEADS, HEAD_PAIR, L), BF16),
        jax.ShapeDtypeStruct((B, ATTN_HEADS, L, HEAD_PAIR), BF16),
        jax.ShapeDtypeStruct((B, L, SSM_WIDTH), BF16),
        jax.ShapeDtypeStruct((B, ATTN_HEADS, L), F32),
        jax.ShapeDtypeStruct((B, L // tm, ATTN_HEADS, LANES), F32),
    ) + tuple(jax.ShapeDtypeStruct(w.shape, BF16) for w in later)
    outs = pl.pallas_call(
        functools.partial(_head_kernel, n_later=len(later)),
        grid=(B, L // tm),
        in_specs=[tile(D), _const_spec(g1.shape), _const_spec(w1.shape),
                  _const_spec(w3.shape), _const_spec(w2.shape), _const_spec(gm.shape),
                  _const_spec(w_in.shape), _const_spec(ws.shape), _const_spec(bf.shape)] + slabs,
        out_specs=(tile(D), tile(ATTN_WIDTH),
                   pl.BlockSpec((None, ATTN_HEADS, HEAD_PAIR, tm), lambda b, i: (b, 0, 0, i)),
                   pl.BlockSpec((None, ATTN_HEADS, tm, HEAD_PAIR), lambda b, i: (b, 0, i, 0)),
                   tile(SSM_WIDTH),
                   pl.BlockSpec((None, ATTN_HEADS, tm), lambda b, i: (b, 0, i)),
                   pl.BlockSpec((None, None, ATTN_HEADS, LANES), lambda b, i: (b, i, 0, 0)))
                  + tuple(slabs),
        out_shape=out_shape,
        scratch_shapes=[pltpu.VMEM((ATTN_HEADS, LANES), F32)],
        compiler_params=pltpu.CompilerParams(
            dimension_semantics=("arbitrary", "arbitrary"),
            vmem_limit_bytes=VMEM_LIMIT),
        name="head",
    )(x, g1, w1, w3, w2, gm, w_in, ws, bf, *later)
    return outs[:7], outs[7:]


def _attn_kernel(q_ref, *refs):
    _attn_prepare(0, q_ref, *refs)

    def q_tile(n, carry):
        _attn_q_tile(n, q_ref, *refs)
        return carry

    lax.fori_loop(0, q_ref.shape[0] // ATTN_TILE, q_tile, 0)


def _tile_start(i):
    return pl.multiple_of(i * ATTN_TILE, ATTN_TILE)


def _q_heads(q):
    lane = lax.broadcasted_iota(jnp.int32, (1, HEAD_PAIR), 1)
    first = lane < HEAD_DIM
    ones_even = jnp.where((lane >= HEAD_DIM) & (lane < HEAD_DIM + N_BIAS), 1.0, 0.0).astype(BF16)
    ones_odd = jnp.where(lane < N_BIAS, 1.0, 0.0).astype(BF16)
    return first, (jnp.where(first, q, ones_even), jnp.where(first, ones_odd, q))


def _attn_prepare(n, q_ref, kt_ref, v_ref, c_ref, st_ref, o_ref, s_e0, s_e1, s_o0, s_o1, m_ref,
                  acc_ref, count_ref):
    t = ATTN_TILE
    q = q_ref[pl.ds(_tile_start(n), t), :]
    first, q_heads = _q_heads(q)
    q_sq = q.astype(F32) * q.astype(F32)
    row = lax.broadcasted_iota(jnp.int32, (t, t), 0)
    col = lax.broadcasted_iota(jnp.int32, (t, t), 1)
    tile_id = lax.broadcasted_iota(jnp.int32, (1, LANES), 1)
    needed = tile_id < 0
    for h, s_ref in enumerate((s_o0, s_o1)):
        s = jnp.where(col <= row, jnp.dot(q_heads[h], kt_ref[h, :, pl.ds(_tile_start(n), t)],
                                          preferred_element_type=F32), NEG_BIG)
        s_ref[...] = s
        m = jnp.max(s, axis=-1, keepdims=True)
        m_ref[h] = m
        q_norm2 = jnp.max(jnp.sum(jnp.where(first == (h == 0), q_sq, 0.0), axis=-1, keepdims=True),
                          axis=0, keepdims=True)
        k_norm2, c_next = st_ref[h:h + 1, :], st_ref[2 + h:3 + h, :]
        c_q = c_ref[h:h + 1, pl.ds(_tile_start(n), LANES)][:, 0:1]
        reach = jnp.sqrt(q_norm2 * k_norm2) * BOUND_SLACK_MUL + BOUND_SLACK_ADD + c_q - c_next
        needed = needed | (reach - jnp.min(m, axis=0, keepdims=True) >= -SKIP_LOG2)
    first_needed = jnp.min(jnp.where(needed & (tile_id < n), tile_id, n).astype(F32))
    count_ref[0] = n - first_needed.astype(jnp.int32)


def _attn_q_tile(n, q_ref, kt_ref, v_ref, c_ref, st_ref, o_ref, s_e0, s_e1, s_o0, s_o1, m_ref,
                 acc_ref, count_ref):
    t = ATTN_TILE
    count = count_ref[0]
    start = _tile_start
    first, q_heads = _q_heads(q_ref[pl.ds(start(n), t), :])
    s_buf = ((s_e0, s_e1), (s_o0, s_o1))
    c_q = [c_ref[h:h + 1, pl.ds(start(n), LANES)][:, 0:1] for h in range(2)]

    def qk(h, kv):
        return jnp.dot(q_heads[h], kt_ref[h, :, pl.ds(start(kv), t)], preferred_element_type=F32)

    def consume(h, s, kv):
        d = c_ref[h:h + 1, pl.ds(start(kv), LANES)][:, 0:1] - c_q[h]
        m_old = m_ref[h]
        m_new = jnp.maximum(m_old, jnp.max(s, axis=-1, keepdims=True) - d)
        p = jnp.exp2(s - (m_new + d)).astype(BF16)
        m_ref[h] = m_new
        acc_ref[h] = jnp.exp2(m_old - m_new) * acc_ref[h] + jnp.dot(
            p, v_ref[h, pl.ds(start(kv), t), :], preferred_element_type=F32)

    def step(kv, par):
        for h in range(2):
            s_buf[1 - par][h][...] = qk(h, kv - 1)
        for h in range(2):
            consume(h, s_buf[par][h][...], kv)

    for h in range(2):
        s_buf[0][h][...] = qk(h, jnp.maximum(n - 1, 0))
    for h in range(2):
        acc_ref[h] = jnp.dot(jnp.exp2(s_buf[1][h][...] - m_ref[h]).astype(BF16),
                             v_ref[h, pl.ds(start(n), t), :], preferred_element_type=F32)

    def pair(i, carry):
        step(n - 1 - 2 * i, 0)
        step(n - 2 - 2 * i, 1)
        return carry

    n_pairs = jnp.maximum(count - 1, 0) // 2
    lax.fori_loop(0, n_pairs, pair, 0)
    left = count - 2 * n_pairs
    last = n - count

    @pl.when(left == 2)
    def _():
        step(last + 1, 0)
        for h in range(2):
            consume(h, s_buf[1][h][...], last)

    @pl.when(left == 1)
    def _():
        for h in range(2):
            consume(h, s_buf[0][h][...], last)

    acc0, acc1 = acc_ref[0], acc_ref[1]
    o_ref[pl.ds(start(n), t), :] = jnp.where(first, acc0 / acc0[:, HEAD_DIM:HEAD_DIM + 1],
                                             acc1 / acc1[:, 0:1])
    n_tiles = q_ref.shape[0] // t
    _attn_prepare(jnp.minimum(n + 1, n_tiles - 1), q_ref, kt_ref, v_ref, c_ref, st_ref, o_ref,
                  s_e0, s_e1, s_o0, s_o1, m_ref, acc_ref, count_ref)


def _attn_call(q, kt, v, ct, k_norm2):
    B, L, _ = q.shape
    t = ATTN_TILE
    n_tiles = L // t
    assert n_tiles <= LANES
    c4 = ct.reshape(B, N_PAIRS, 2, L)
    c_next = jnp.roll(ct[:, :, ::t], -1, axis=-1)
    per_tile = jnp.stack([k_norm2[..., 0].transpose(0, 2, 1), c_next], axis=1)
    per_tile = jnp.pad(per_tile, ((0, 0),) * 3 + ((0, LANES - n_tiles),))
    stats = (per_tile.reshape(B, 2, N_PAIRS, 2, LANES).transpose(0, 2, 1, 3, 4)
             .reshape(B, N_PAIRS, 4, LANES))
    pair_block = lambda *shape: pl.BlockSpec((None, 2) + shape, lambda b, p: (b, p, 0, 0))
    lanes_of_pair = pl.BlockSpec((None, L, HEAD_PAIR), lambda b, p: (b, 0, p))
    return pl.pallas_call(
        _attn_kernel,
        grid=(B, N_PAIRS),
        in_specs=[lanes_of_pair, pair_block(HEAD_PAIR, L), pair_block(L, HEAD_PAIR),
                  pl.BlockSpec((None, None, 2, L), lambda b, p: (b, p, 0, 0)),
                  pl.BlockSpec((None, None, 4, LANES), lambda b, p: (b, p, 0, 0))],
        out_specs=lanes_of_pair,
        out_shape=jax.ShapeDtypeStruct((B, L, ATTN_WIDTH), F32),
        scratch_shapes=[pltpu.VMEM((t, t), F32)] * 4
                       + [pltpu.VMEM((2, t, 1), F32), pltpu.VMEM((2, t, HEAD_PAIR), F32),
                          pltpu.SMEM((1,), jnp.int32)],
        compiler_params=pltpu.CompilerParams(
            dimension_semantics=("arbitrary", "arbitrary"),
            vmem_limit_bytes=VMEM_LIMIT),
        name="attn",
    )(q, kt, v, c4, stats)


def _gelu_tanh(x):
    return 0.5 * x * (1.0 + jnp.tanh(math.sqrt(2.0 / math.pi) * (x + 0.044715 * (x * x * x))))


def _cis(mag_arg, ang):
    mag = jnp.exp(mag_arg)
    return mag * jnp.cos(ang), mag * jnp.sin(ang)


def _cmul(ar, ai, br, bi):
    return ar * br - ai * bi, ar * bi + ai * br


def _cpow2(zr, zi, n):
    assert n & (n - 1) == 0
    while n > 1:
        zr, zi = zr * zr - zi * zi, 2.0 * zr * zi
        n //= 2
    return zr, zi


def _ssm_kernel(arow_ref, acol_ref, ldt_ref, bt_ref, cab_ref, dcol_ref, e_ref, f_ref, z_ref,
                *, chunks_per_seq):
    H = SSM_GROUP_CH

    @pl.when(pl.program_id(0) == 0)
    def _():
        z_ref[:, 0:SSM_CHUNK * H, :] = jnp.zeros((z_ref.shape[0], SSM_CHUNK * H, LANES), F32)

    for gi in range(SSM_GROUPS_PER_STEP):
        y = _ssm_group(arow_ref.at[gi], acol_ref.at[gi], ldt_ref.at[gi], bt_ref.at[gi], cab_ref.at[gi],
                       dcol_ref.at[gi], e_ref[:, gi * H:(gi + 1) * H, :], z_ref.at[gi], chunks_per_seq)
        f_ref[:, gi * H:(gi + 1) * H, :] = y.reshape(SSM_CHUNK, H, y.shape[-1])


def _ssm_group(arow_ref, acol_ref, ldt_ref, bt_ref, cab_ref, dcol_ref, e, z_ref, chunks_per_seq):
    T, P, H = SSM_CHUNK, SSM_STATE, SSM_GROUP_CH
    TH = T * H
    hi = lax.Precision.HIGHEST
    dt = jnp.exp(ldt_ref[...])

    lam_r, lam_i = dt * arow_ref[0:1, :], dt * arow_ref[1:2, :]
    j0 = lax.broadcasted_iota(jnp.int32, (T, 2 * P), 0).astype(F32)
    pa0, pb0 = _cis(j0 * lam_r, j0 * lam_i)
    pa1, pb1 = _cmul(pa0, pb0, *_cis(lam_r, lam_i))
    over_h = lambda a: jnp.concatenate(
        [jnp.broadcast_to(a[j:j + 1, :], (H, 2 * P)) for j in range(T)], axis=0)
    ca, cb = jnp.tile(cab_ref[0], (T, 1)), jnp.tile(cab_ref[1], (T, 1))
    c_pow0 = over_h(pa0) * ca + over_h(pb0) * cb
    c_pow1 = over_h(pa1) * ca + over_h(pb1) * cb

    a_r, a_i = acol_ref[:, 0:1], acol_ref[:, 1:2]
    lr, li = dt * a_r, dt * a_i
    abar_r, abar_i = _cis(lr, li)
    nr, ni = abar_r - 1.0, abar_i
    den = a_r * a_r + a_i * a_i
    fr, fi = (nr * a_r + ni * a_i) / den, (ni * a_r - nr * a_i) / den
    b_r, b_i = bt_ref[0], bt_ref[1]
    bb_r, bb_i = fr * b_r - fi * b_i, fr * b_i + fi * b_r

    kcol = jnp.dot(c_pow0, jnp.concatenate([bb_r, bb_i], axis=0), precision=hi,
                   preferred_element_type=F32)
    lane_h = lax.broadcasted_iota(jnp.int32, (H, LANES), 1) % H
    skip = jnp.where(lane_h == lax.broadcasted_iota(jnp.int32, (H, LANES), 0), dcol_ref[...], 0.0)

    z_ref[TH:2 * TH, :] = kcol
    z_ref[TH:TH + H, :] = kcol[:H] + skip
    lane_group = lax.broadcasted_iota(jnp.int32, (1, LANES), 1) // H
    groups_per_block = LANES // H
    blocks = []
    for v in range(TH // LANES):
        blk = None
        for u in range(groups_per_block):
            s = v * groups_per_block + u
            piece = z_ref[TH - H * s:2 * TH - H * s, :]
            blk = piece if blk is None else jnp.where(lane_group == u, piece, blk)
        blocks.append(blk.astype(BF16))
    mt = jnp.concatenate(blocks, axis=1)

    expo = (groups_per_block - 1 - lane_group).astype(F32)
    wr, wi = _cis(lr * expo, li * expo)
    hop_r, hop_i = _cpow2(abar_r, abar_i, groups_per_block)
    w1_r, w1_i = [], []
    for v in range(TH // LANES):
        w1_r.insert(0, wr * bb_r - wi * bb_i)
        w1_i.insert(0, wr * bb_i + wi * bb_r)
        wr, wi = _cmul(wr, wi, hop_r, hop_i)
    w1t = jnp.concatenate([jnp.concatenate(w1_r, axis=1),
                           jnp.concatenate(w1_i, axis=1)], axis=0).astype(BF16)

    e = e.reshape(TH, e.shape[-1])
    y = jnp.dot(mt, e, preferred_element_type=F32)
    st = jnp.dot(w1t, e, preferred_element_type=F32)
    sr, si = st[:P], st[P:]
    pos = lax.broadcasted_iota(jnp.int32, sr.shape, 1) % chunks_per_seq

    def shifted(a, shift):
        return jnp.where(pos >= shift, pltpu.roll(a, shift, 1), 0.0)

    qr, qi = _cpow2(abar_r, abar_i, T)
    shift = 1
    while shift < chunks_per_seq:
        srs, sis = shifted(sr, shift), shifted(si, shift)
        sr, si = sr + qr * srs - qi * sis, si + qr * sis + qi * srs
        qr, qi = qr * qr - qi * qi, 2.0 * qr * qi
        shift *= 2
    x_prev = jnp.concatenate([shifted(sr, 1), shifted(si, 1)], axis=0).astype(BF16)
    y = y + jnp.dot(c_pow1.astype(BF16), x_prev, preferred_element_type=F32)
    return _gelu_tanh(y)


def _ssm_call(arow, acol, ldt, bt, cab, dcol, e, chunks_per_seq):
    T, _, NC = e.shape
    G = arow.shape[0]
    gps = SSM_GROUPS_PER_STEP
    assert G % gps == 0
    grp = lambda a: pl.BlockSpec((gps,) + a.shape[1:], lambda g: (g,) + (0,) * (a.ndim - 1))
    channels = pl.BlockSpec((T, gps * SSM_GROUP_CH, NC), lambda g: (0, g, 0))
    return pl.pallas_call(
        functools.partial(_ssm_kernel, chunks_per_seq=chunks_per_seq),
        grid=(G // gps,),
        in_specs=[grp(a) for a in (arow, acol, ldt, bt, cab, dcol)] + [channels],
        out_specs=channels,
        out_shape=jax.ShapeDtypeStruct((T, SSM_WIDTH, NC), F32),
        scratch_shapes=[pltpu.VMEM((gps, 2 * T * SSM_GROUP_CH, LANES), F32)],
        compiler_params=pltpu.CompilerParams(
            dimension_semantics=("arbitrary",), vmem_limit_bytes=VMEM_LIMIT),
        name="ssm",
    )(arow, acol, ldt, bt, cab, dcol, e)


def _ssm_param_layouts(a_re, a_im, log_dt, b_re, b_im, c_re, c_im, d_skip):
    G = a_re.shape[0]
    arow = jnp.stack([jnp.concatenate([a_re, a_re], -1), jnp.concatenate([a_im, a_im], -1)], 1)
    acol = jnp.stack([a_re, a_im], -1)
    reps = LANES // SSM_GROUP_CH
    bt = jnp.stack([jnp.tile(b_re, (1, 1, reps)), jnp.tile(b_im, (1, 1, reps))], 1)
    cab = jnp.stack([jnp.concatenate([c_re, -c_im], -1), jnp.concatenate([-c_im, -c_re], -1)], 1)
    return (arow.astype(F32), acol.astype(F32), log_dt.reshape(G, 1, 1).astype(F32),
            bt.astype(F32), cab.astype(F32), d_skip.reshape(G, SSM_GROUP_CH, 1).astype(F32))


def _tail_kernel(h1_ref, attn_ref, y_ref, p_ref, wglu_ref, bglu_ref, ga_ref, gs_ref,
                 wo_ref, g2_ref, w1_ref, w3_ref, w2_ref, gp_ref, wpg_ref,
                 wpp_ref, gf_ref, o_ref):
    y = y_ref[...]
    glu = y * _sigmoid(jnp.dot(y.astype(BF16), wglu_ref[...], preferred_element_type=F32)
                       + bglu_ref[...])
    an = _rms(attn_ref[...], ga_ref[...]).astype(BF16)
    sn = _rms(glu, gs_ref[...]).astype(BF16)
    h = (h1_ref[...] + jnp.dot(an, wo_ref[:ATTN_WIDTH, :], preferred_element_type=F32)
         + jnp.dot(sn, wo_ref[ATTN_WIDTH:, :], preferred_element_type=F32))
    h = h + 0.5 * _swiglu(_rms(h, g2_ref[...]).astype(BF16), w1_ref, w3_ref, w2_ref)
    gate = _sigmoid(jnp.dot(_rms(h, gp_ref[...]).astype(BF16), wpg_ref[...],
                            preferred_element_type=F32))
    h = h + gate * jnp.dot(p_ref[...].astype(BF16), wpp_ref[...], preferred_element_type=F32)
    o_ref[...] = _rms(h, gf_ref[...])


def _tail_call(h1, attn, y, p, *consts):
    B, L, D = h1.shape
    tm = TOKEN_TILE
    tile = lambda w: pl.BlockSpec((None, tm, w), lambda b, i: (b, i, 0))
    return pl.pallas_call(
        _tail_kernel,
        grid=(B, L // tm),
        in_specs=[tile(D), tile(ATTN_WIDTH), tile(SSM_WIDTH), tile(PLE_DIM)]
                 + [_const_spec(c.shape) for c in consts],
        out_specs=tile(D),
        out_shape=jax.ShapeDtypeStruct((B, L, D), F32),
        compiler_params=pltpu.CompilerParams(
            dimension_semantics=("arbitrary", "arbitrary"),
            vmem_limit_bytes=VMEM_LIMIT),
        name="tail",
    )(h1, attn, y, p, *consts)


def kernel(x, p, g_ffn1, w1_a, w3_a, w2_a, g_mix, w_in, b_f, a_re, a_im, log_dt, b_re, b_im, c_re, c_im, d_skip, w_glu, b_glu, g_attn_out, g_ssm_out, w_out, g_ffn2, w1_b, w3_b, w2_b, g_ple, w_ple_gate, w_ple_proj, g_final):
    B, L, D = x.shape
    assert D == D_MODEL and L % ATTN_TILE == 0 and L % TOKEN_TILE == 0 and L % SSM_CHUNK == 0
    assert g_ffn1.shape[0] == 1, "single layer"
    assert TOKEN_TILE == ATTN_TILE, "decay bias rows are relative to the kv tile start"
    row = lambda g: g.reshape(1, -1).astype(F32)
    bf = lambda w: w.astype(BF16)
    w_in0 = w_in[0]
    assert w_in0.shape[1] == 3 * ATTN_WIDTH + ATTN_HEADS + SSM_WIDTH

    (h1, q, kt, v, s_in, ct, k_norm2), (w1_b16, w3_b16, w2_b16, w_out16, w_gate16) = _head_call(
        x, row(g_ffn1[0]), bf(w1_a[0]), bf(w3_a[0]), bf(w2_a[0]), row(g_mix[0]),
        bf(w_in0), bf(w_in0[:, 3 * ATTN_WIDTH + ATTN_HEADS:]),
        b_f[0].reshape(ATTN_HEADS, 1).astype(F32),
        later=tuple(w.astype(F32) for w in (w1_b[0], w3_b[0], w2_b[0], w_out[0], w_ple_gate[0])))

    T = SSM_CHUNK
    chunks_per_seq = L // T
    n_chunks = B * chunks_per_seq
    e = s_in.reshape(n_chunks, T, SSM_WIDTH).transpose(1, 2, 0)
    f = _ssm_call(*_ssm_param_layouts(a_re[0], a_im[0], log_dt[0], b_re[0], b_im[0],
                                      c_re[0], c_im[0], d_skip[0]),
                  e, chunks_per_seq)
    y = f.transpose(2, 0, 1).reshape(B, L, SSM_WIDTH)

    attn = _attn_call(q, kt, v, ct, k_norm2)

    return _tail_call(
        h1, attn, y, p[0],
        bf(w_glu[0]), row(b_glu[0]), row(g_attn_out[0]), row(g_ssm_out[0]),
        w_out16, row(g_ffn2[0]),
        w1_b16, w3_b16, w2_b16, row(g_ple[0]), w_gate16,
        bf(w_ple_proj[0]), row(g_final))
```

```python
import functools
import math

import jax
import jax.numpy as jnp
from jax import lax
from jax.experimental import pallas as pl
from jax.experimental.pallas import tpu as pltpu

D_MODEL = 1024
ATTN_HEADS = 8
HEAD_DIM = 64
ATTN_WIDTH = ATTN_HEADS * HEAD_DIM
SSM_WIDTH = D_MODEL - ATTN_WIDTH
SSM_GROUP_CH = 16
SSM_GROUPS = SSM_WIDTH // SSM_GROUP_CH
SSM_STATE = 64
D_FF = 2816
PLE_DIM = 256
EPS = 1e-6

LANES = 128
BF16_SUBLANES = 16
HEAD_PAIR = 2 * HEAD_DIM
N_PAIRS = ATTN_HEADS // 2
FF_CHUNK = 256
TOKEN_TILE = 512
ATTN_TILE = 512
SSM_CHUNK = 32
SSM_GROUPS_PER_STEP = 2
NEG_BIG = -1e30
SKIP_LOG2 = 140.0
BOUND_SLACK_MUL = 1.001
BOUND_SLACK_ADD = 1.0
LOG2E = math.log2(math.e)
N_BIAS = 3
BIAS_ROWS = 8
VMEM_LIMIT = 56 * 1024 * 1024

BF16 = jnp.bfloat16
F32 = jnp.float32


def _rms(x, g):
    ms = jnp.mean(x * x, axis=-1, keepdims=True)
    return x * lax.rsqrt(ms + EPS) * g


def _sigmoid(x):
    return 1.0 / (1.0 + jnp.exp(-x))


def _swiglu(xn, w1_ref, w3_ref, w2_ref):
    acc = None
    for c in range(D_FF // FF_CHUNK):
        sl = slice(c * FF_CHUNK, (c + 1) * FF_CHUNK)
        a = jnp.dot(xn, w1_ref[:, sl], preferred_element_type=F32)
        b = jnp.dot(xn, w3_ref[:, sl], preferred_element_type=F32)
        gated = (a * _sigmoid(a) * b).astype(BF16)
        part = jnp.dot(gated, w2_ref[sl, :], preferred_element_type=F32)
        acc = part if acc is None else acc + part
    return acc


def _const_spec(shape):
    nd = len(shape)
    return pl.BlockSpec(shape, lambda *_: (0,) * nd, pipeline_mode=pl.Buffered(1))


def _head_kernel(*refs, n_later):
    (x_ref, g1_ref, w1_ref, w3_ref, w2_ref, gm_ref, win_ref, ws_ref, bf_ref) = refs[:9]
    later_in = refs[9:9 + n_later]
    h1_ref, q_ref, kt_ref, v_ref, s_ref, ct_ref, kn_ref, cn_ref = refs[9 + n_later:17 + n_later]
    later_out = refs[17 + n_later:17 + 2 * n_later]
    carry_ref = refs[-1]
    for src, dst in zip(later_in, later_out):
        dst[...] = src[...].astype(BF16)
    tm = x_ref.shape[0]
    x = x_ref[...]
    h1 = x + 0.5 * _swiglu(_rms(x, g1_ref[...]).astype(BF16), w1_ref, w3_ref, w2_ref)
    h1_ref[...] = h1
    un = _rms(h1, gm_ref[...]).astype(BF16)
    project = lambda w: jnp.dot(un, w, preferred_element_type=F32)
    zf = project(win_ref[:, 3 * ATTN_WIDTH:3 * ATTN_WIDTH + LANES])
    kv = project(win_ref[:, ATTN_WIDTH:3 * ATTN_WIDTH])
    zft = zf.T[:ATTN_HEADS, :] + bf_ref[...]
    logf = jnp.minimum(zft, 0.0) - jnp.log1p(jnp.exp(-jnp.abs(zft)))
    lane = lax.broadcasted_iota(jnp.int32, logf.shape, 1)
    c = logf
    shift = 1
    while shift < tm:
        c = c + jnp.where(lane >= shift, pltpu.roll(c, shift, 1), 0.0)
        shift *= 2

    @pl.when(pl.program_id(1) == 0)
    def _():
        carry_ref[...] = jnp.zeros_like(carry_ref)
        kn_ref[...] = jnp.zeros_like(kn_ref)
        cn_ref[...] = jnp.zeros_like(cn_ref)

    c_abs = c + carry_ref[:, 0:1]
    ct_ref[...] = c_abs * LOG2E
    carry_ref[...] = jnp.broadcast_to(c_abs[:, tm - 1:tm], carry_ref.shape)

    rel = (c - c[:, 0:1]) * LOG2E
    hi = rel.astype(BF16).astype(F32)
    mid = (rel - hi).astype(BF16).astype(F32)
    lo = (rel - hi - mid).astype(BF16).astype(F32)
    kt = kv[:, :ATTN_WIDTH].astype(BF16).astype(F32).T
    vv = kv[:, ATTN_WIDTH:]
    k_sq = (kt * kt).reshape(ATTN_HEADS, HEAD_DIM, tm)
    tile = pl.program_id(1)
    stat_lane = lax.broadcasted_iota(jnp.int32, kn_ref.shape, 1)

    kn_ref[...] = jnp.where(stat_lane == tile,
                            jnp.max(jnp.sum(k_sq, axis=1), axis=-1, keepdims=True), kn_ref[...])
    cn_ref[...] = jnp.where(stat_lane == tile - 1, c_abs[:, 0:1] * LOG2E, cn_ref[...])
    sub = lax.broadcasted_iota(jnp.int32, (BIAS_ROWS, tm), 0)
    zeros = jnp.zeros((HEAD_DIM - BIAS_ROWS, tm), F32)
    vlane = lax.broadcasted_iota(jnp.int32, (tm, HEAD_PAIR), 1)
    for h in range(ATTN_HEADS):
        bias = jnp.where(sub == 0, -hi[h:h + 1],
                         jnp.where(sub == 1, -mid[h:h + 1],
                                   jnp.where(sub == 2, -lo[h:h + 1], 0.0)))
        k_h = kt[h * HEAD_DIM:(h + 1) * HEAD_DIM]
        vp = vv[:, (h // 2) * HEAD_PAIR:(h // 2 + 1) * HEAD_PAIR]
        if h % 2 == 0:
            kt_ref[h] = jnp.concatenate([k_h, bias, zeros], axis=0).astype(BF16)
            v_ref[h] = jnp.where(vlane < HEAD_DIM, vp,
                                 jnp.where(vlane == HEAD_DIM, 1.0, 0.0)).astype(BF16)
        else:
            kt_ref[h] = jnp.concatenate([bias, zeros, k_h], axis=0).astype(BF16)
            v_ref[h] = jnp.where(vlane >= HEAD_DIM, vp,
                                 jnp.where(vlane == 0, 1.0, 0.0)).astype(BF16)

    q_ref[...] = (project(win_ref[:, :ATTN_WIDTH]) * (LOG2E / math.sqrt(HEAD_DIM))).astype(BF16)
    s_ref[...] = project(ws_ref[...]).astype(BF16)


def _slab_spec(w, n_steps, steps_per_batch):
    rows = next(r for r in range(BF16_SUBLANES, w.shape[0] + 1, BF16_SUBLANES)
                if w.shape[0] % r == 0 and w.shape[0] // r <= n_steps)
    last = w.shape[0] // rows - 1
    return pl.BlockSpec((rows, w.shape[1]),
                        lambda b, i: (jnp.minimum(b * steps_per_batch + i, last), 0))


def _head_call(x, g1, w1, w3, w2, gm, w_in, ws, bf, later):
    B, L, D = x.shape
    tm = TOKEN_TILE
    tile = lambda w: pl.BlockSpec((None, tm, w), lambda b, i: (b, i, 0))
    slabs = [_slab_spec(w, B * (L // tm), L // tm) for w in later]
    out_shape = (
        jax.ShapeDtypeStruct((B, L, D), F32),
        jax.ShapeDtypeStruct((B, L, ATTN_WIDTH), BF16),
        jax.ShapeDtypeStruct((B, ATTN_HEADS, HEAD_PAIR, L), BF16),
        jax.ShapeDtypeStruct((B, ATTN_HEADS, L, HEAD_PAIR), BF16),
        jax.ShapeDtypeStruct((B, L, SSM_WIDTH), BF16),
        jax.ShapeDtypeStruct((B, ATTN_HEADS, L), F32),
        jax.ShapeDtypeStruct((B, ATTN_HEADS, LANES), F32),
        jax.ShapeDtypeStruct((B, ATTN_HEADS, LANES), F32),
    ) + tuple(jax.ShapeDtypeStruct(w.shape, BF16) for w in later)
    outs = pl.pallas_call(
        functools.partial(_head_kernel, n_later=len(later)),
        grid=(B, L // tm),
        in_specs=[tile(D), _const_spec(g1.shape), _const_spec(w1.shape),
                  _const_spec(w3.shape), _const_spec(w2.shape), _const_spec(gm.shape),
                  _const_spec(w_in.shape), _const_spec(ws.shape), _const_spec(bf.shape)] + slabs,
        out_specs=(tile(D), tile(ATTN_WIDTH),
                   pl.BlockSpec((None, ATTN_HEADS, HEAD_PAIR, tm), lambda b, i: (b, 0, 0, i)),
                   pl.BlockSpec((None, ATTN_HEADS, tm, HEAD_PAIR), lambda b, i: (b, 0, i, 0)),
                   tile(SSM_WIDTH),
                   pl.BlockSpec((None, ATTN_HEADS, tm), lambda b, i: (b, 0, i)),
                   pl.BlockSpec((None, ATTN_HEADS, LANES), lambda b, i: (b, 0, 0)),
                   pl.BlockSpec((None, ATTN_HEADS, LANES), lambda b, i: (b, 0, 0)))
                  + tuple(slabs),
        out_shape=out_shape,
        scratch_shapes=[pltpu.VMEM((ATTN_HEADS, LANES), F32)],
        compiler_params=pltpu.CompilerParams(
            dimension_semantics=("arbitrary", "arbitrary"),
            vmem_limit_bytes=VMEM_LIMIT),
        name="head",
    )(x, g1, w1, w3, w2, gm, w_in, ws, bf, *later)
    return outs[:8], outs[8:]


def _attn_kernel(q_ref, *refs):
    _attn_prepare(0, q_ref, *refs)

    def q_tile(n, carry):
        _attn_q_tile(n, q_ref, *refs)
        return carry

    lax.fori_loop(0, q_ref.shape[0] // ATTN_TILE, q_tile, 0)


def _tile_start(i):
    return pl.multiple_of(i * ATTN_TILE, ATTN_TILE)


def _q_heads(q):
    lane = lax.broadcasted_iota(jnp.int32, (1, HEAD_PAIR), 1)
    first = lane < HEAD_DIM
    ones_even = jnp.where((lane >= HEAD_DIM) & (lane < HEAD_DIM + N_BIAS), 1.0, 0.0).astype(BF16)
    ones_odd = jnp.where(lane < N_BIAS, 1.0, 0.0).astype(BF16)
    return first, (jnp.where(first, q, ones_even), jnp.where(first, ones_odd, q))


def _attn_prepare(n, q_ref, kt_ref, v_ref, c_ref, kn_ref, cn_ref, o_ref, s_e0, s_e1, s_o0, s_o1, m_ref,
                  acc_ref, count_ref):
    t = ATTN_TILE
    q = q_ref[pl.ds(_tile_start(n), t), :]
    first, q_heads = _q_heads(q)
    q_sq = q.astype(F32) * q.astype(F32)
    row = lax.broadcasted_iota(jnp.int32, (t, t), 0)
    col = lax.broadcasted_iota(jnp.int32, (t, t), 1)
    tile_id = lax.broadcasted_iota(jnp.int32, (1, LANES), 1)
    needed = tile_id < 0
    for h, s_ref in enumerate((s_o0, s_o1)):
        s = jnp.where(col <= row, jnp.dot(q_heads[h], kt_ref[h, :, pl.ds(_tile_start(n), t)],
                                          preferred_element_type=F32), NEG_BIG)
        s_ref[...] = s
        m = jnp.max(s, axis=-1, keepdims=True)
        m_ref[h] = m
        q_norm2 = jnp.max(jnp.sum(jnp.where(first == (h == 0), q_sq, 0.0), axis=-1, keepdims=True),
                          axis=0, keepdims=True)
        k_norm2, c_next = kn_ref[h:h + 1, :], cn_ref[h:h + 1, :]
        c_q = c_ref[h:h + 1, pl.ds(_tile_start(n), LANES)][:, 0:1]
        reach = jnp.sqrt(q_norm2 * k_norm2) * BOUND_SLACK_MUL + BOUND_SLACK_ADD + c_q - c_next
        needed = needed | (reach - jnp.min(m, axis=0, keepdims=True) >= -SKIP_LOG2)
    first_needed = jnp.min(jnp.where(needed & (tile_id < n), tile_id, n).astype(F32))
    count_ref[0] = n - first_needed.astype(jnp.int32)


def _attn_q_tile(n, q_ref, kt_ref, v_ref, c_ref, kn_ref, cn_ref, o_ref, s_e0, s_e1, s_o0, s_o1, m_ref,
                 acc_ref, count_ref):
    t = ATTN_TILE
    count = count_ref[0]
    start = _tile_start
    first, q_heads = _q_heads(q_ref[pl.ds(start(n), t), :])
    s_buf = ((s_e0, s_e1), (s_o0, s_o1))
    c_q = [c_ref[h:h + 1, pl.ds(start(n), LANES)][:, 0:1] for h in range(2)]

    def qk(h, kv):
        return jnp.dot(q_heads[h], kt_ref[h, :, pl.ds(start(kv), t)], preferred_element_type=F32)

    def consume(h, s, kv):
        d = c_ref[h:h + 1, pl.ds(start(kv), LANES)][:, 0:1] - c_q[h]
        m_old = m_ref[h]
        m_new = jnp.maximum(m_old, jnp.max(s, axis=-1, keepdims=True) - d)
        p = jnp.exp2(s - (m_new + d)).astype(BF16)
        m_ref[h] = m_new
        acc_ref[h] = jnp.exp2(m_old - m_new) * acc_ref[h] + jnp.dot(
            p, v_ref[h, pl.ds(start(kv), t), :], preferred_element_type=F32)

    def step(kv, par):
        for h in range(2):
            s_buf[1 - par][h][...] = qk(h, kv - 1)
        for h in range(2):
            consume(h, s_buf[par][h][...], kv)

    for h in range(2):
        s_buf[0][h][...] = qk(h, jnp.maximum(n - 1, 0))
    for h in range(2):
        acc_ref[h] = jnp.dot(jnp.exp2(s_buf[1][h][...] - m_ref[h]).astype(BF16),
                             v_ref[h, pl.ds(start(n), t), :], preferred_element_type=F32)

    def pair(i, carry):
        step(n - 1 - 2 * i, 0)
        step(n - 2 - 2 * i, 1)
        return carry

    n_pairs = jnp.maximum(count - 1, 0) // 2
    lax.fori_loop(0, n_pairs, pair, 0)
    left = count - 2 * n_pairs
    last = n - count

    @pl.when(left == 2)
    def _():
        step(last + 1, 0)
        for h in range(2):
            consume(h, s_buf[1][h][...], last)

    @pl.when(left == 1)
    def _():
        for h in range(2):
            consume(h, s_buf[0][h][...], last)

    acc0, acc1 = acc_ref[0], acc_ref[1]
    o_ref[pl.ds(start(n), t), :] = jnp.where(first, acc0 / acc0[:, HEAD_DIM:HEAD_DIM + 1],
                                             acc1 / acc1[:, 0:1])
    n_tiles = q_ref.shape[0] // t
    _attn_prepare(jnp.minimum(n + 1, n_tiles - 1), q_ref, kt_ref, v_ref, c_ref, kn_ref, cn_ref, o_ref,
                  s_e0, s_e1, s_o0, s_o1, m_ref, acc_ref, count_ref)


def _attn_call(q, kt, v, ct, k_norm2, c_next):
    B, L, _ = q.shape
    t = ATTN_TILE
    assert L // t <= LANES
    by_pair = lambda a: a.reshape(B, N_PAIRS, 2, a.shape[-1])
    pair_block = lambda *shape: pl.BlockSpec((None, 2) + shape, lambda b, p: (b, p, 0, 0))
    pair_rows = lambda w: pl.BlockSpec((None, None, 2, w), lambda b, p: (b, p, 0, 0))
    lanes_of_pair = pl.BlockSpec((None, L, HEAD_PAIR), lambda b, p: (b, 0, p))
    return pl.pallas_call(
        _attn_kernel,
        grid=(B, N_PAIRS),
        in_specs=[lanes_of_pair, pair_block(HEAD_PAIR, L), pair_block(L, HEAD_PAIR),
                  pair_rows(L), pair_rows(LANES), pair_rows(LANES)],
        out_specs=lanes_of_pair,
        out_shape=jax.ShapeDtypeStruct((B, L, ATTN_WIDTH), F32),
        scratch_shapes=[pltpu.VMEM((t, t), F32)] * 4
                       + [pltpu.VMEM((2, t, 1), F32), pltpu.VMEM((2, t, HEAD_PAIR), F32),
                          pltpu.SMEM((1,), jnp.int32)],
        compiler_params=pltpu.CompilerParams(
            dimension_semantics=("arbitrary", "arbitrary"),
            vmem_limit_bytes=VMEM_LIMIT),
        name="attn",
    )(q, kt, v, by_pair(ct), by_pair(k_norm2), by_pair(c_next))


def _gelu_tanh(x):
    return 0.5 * x * (1.0 + jnp.tanh(math.sqrt(2.0 / math.pi) * (x + 0.044715 * (x * x * x))))


def _cis(mag_arg, ang):
    mag = jnp.exp(mag_arg)
    return mag * jnp.cos(ang), mag * jnp.sin(ang)


def _cmul(ar, ai, br, bi):
    return ar * br - ai * bi, ar * bi + ai * br


def _cpow2(zr, zi, n):
    assert n & (n - 1) == 0
    while n > 1:
        zr, zi = zr * zr - zi * zi, 2.0 * zr * zi
        n //= 2
    return zr, zi


def _ssm_kernel(arow_ref, acol_ref, ldt_ref, bt_ref, cab_ref, dcol_ref, e_ref, f_ref, z_ref,
                *, chunks_per_seq):
    H = SSM_GROUP_CH

    @pl.when(pl.program_id(0) == 0)
    def _():
        z_ref[:, 0:SSM_CHUNK * H, :] = jnp.zeros((z_ref.shape[0], SSM_CHUNK * H, LANES), F32)

    for gi in range(SSM_GROUPS_PER_STEP):
        y = _ssm_group(arow_ref.at[gi], acol_ref.at[gi], ldt_ref.at[gi], bt_ref.at[gi], cab_ref.at[gi],
                       dcol_ref.at[gi], e_ref[:, gi * H:(gi + 1) * H, :], z_ref.at[gi], chunks_per_seq)
        f_ref[:, gi * H:(gi + 1) * H, :] = y.reshape(SSM_CHUNK, H, y.shape[-1])


def _ssm_group(arow_ref, acol_ref, ldt_ref, bt_ref, cab_ref, dcol_ref, e, z_ref, chunks_per_seq):
    T, P, H = SSM_CHUNK, SSM_STATE, SSM_GROUP_CH
    TH = T * H
    hi = lax.Precision.HIGHEST
    dt = jnp.exp(ldt_ref[...])

    lam_r, lam_i = dt * arow_ref[0:1, :], dt * arow_ref[1:2, :]
    j0 = lax.broadcasted_iota(jnp.int32, (T, 2 * P), 0).astype(F32)
    pa0, pb0 = _cis(j0 * lam_r, j0 * lam_i)
    pa1, pb1 = _cmul(pa0, pb0, *_cis(lam_r, lam_i))
    over_h = lambda a: jnp.concatenate(
        [jnp.broadcast_to(a[j:j + 1, :], (H, 2 * P)) for j in range(T)], axis=0)
    ca, cb = jnp.tile(cab_ref[0], (T, 1)), jnp.tile(cab_ref[1], (T, 1))
    c_pow0 = over_h(pa0) * ca + over_h(pb0) * cb
    c_pow1 = over_h(pa1) * ca + over_h(pb1) * cb

    a_r, a_i = acol_ref[:, 0:1], acol_ref[:, 1:2]
    lr, li = dt * a_r, dt * a_i
    abar_r, abar_i = _cis(lr, li)
    nr, ni = abar_r - 1.0, abar_i
    den = a_r * a_r + a_i * a_i
    fr, fi = (nr * a_r + ni * a_i) / den, (ni * a_r - nr * a_i) / den
    b_r, b_i = bt_ref[0], bt_ref[1]
    bb_r, bb_i = fr * b_r - fi * b_i, fr * b_i + fi * b_r

    kcol = jnp.dot(c_pow0, jnp.concatenate([bb_r, bb_i], axis=0), precision=hi,
                   preferred_element_type=F32)
    lane_h = lax.broadcasted_iota(jnp.int32, (H, LANES), 1) % H
    skip = jnp.where(lane_h == lax.broadcasted_iota(jnp.int32, (H, LANES), 0), dcol_ref[...], 0.0)

    z_ref[TH:2 * TH, :] = kcol
    z_ref[TH:TH + H, :] = kcol[:H] + skip
    lane_group = lax.broadcasted_iota(jnp.int32, (1, LANES), 1) // H
    groups_per_block = LANES // H
    blocks = []
    for v in range(TH // LANES):
        blk = None
        for u in range(groups_per_block):
            s = v * groups_per_block + u
            piece = z_ref[TH - H * s:2 * TH - H * s, :]
            blk = piece if blk is None else jnp.where(lane_group == u, piece, blk)
        blocks.append(blk.astype(BF16))
    mt = jnp.concatenate(blocks, axis=1)

    expo = (groups_per_block - 1 - lane_group).astype(F32)
    wr, wi = _cis(lr * expo, li * expo)
    hop_r, hop_i = _cpow2(abar_r, abar_i, groups_per_block)
    w1_r, w1_i = [], []
    for v in range(TH // LANES):
        w1_r.insert(0, wr * bb_r - wi * bb_i)
        w1_i.insert(0, wr * bb_i + wi * bb_r)
        wr, wi = _cmul(wr, wi, hop_r, hop_i)
    w1t = jnp.concatenate([jnp.concatenate(w1_r, axis=1),
                           jnp.concatenate(w1_i, axis=1)], axis=0).astype(BF16)

    e = e.reshape(TH, e.shape[-1])
    y = jnp.dot(mt, e, preferred_element_type=F32)
    st = jnp.dot(w1t, e, preferred_element_type=F32)
    sr, si = st[:P], st[P:]
    pos = lax.broadcasted_iota(jnp.int32, sr.shape, 1) % chunks_per_seq

    def shifted(a, shift):
        return jnp.where(pos >= shift, pltpu.roll(a, shift, 1), 0.0)

    qr, qi = _cpow2(abar_r, abar_i, T)
    shift = 1
    while shift < chunks_per_seq:
        srs, sis = shifted(sr, shift), shifted(si, shift)
        sr, si = sr + qr * srs - qi * sis, si + qr * sis + qi * srs
        qr, qi = qr * qr - qi * qi, 2.0 * qr * qi
        shift *= 2
    x_prev = jnp.concatenate([shifted(sr, 1), shifted(si, 1)], axis=0).astype(BF16)
    y = y + jnp.dot(c_pow1.astype(BF16), x_prev, preferred_element_type=F32)
    return _gelu_tanh(y)


def _ssm_call(arow, acol, ldt, bt, cab, dcol, e, chunks_per_seq):
    T, _, NC = e.shape
    G = arow.shape[0]
    gps = SSM_GROUPS_PER_STEP
    assert G % gps == 0
    grp = lambda a: pl.BlockSpec((gps,) + a.shape[1:], lambda g: (g,) + (0,) * (a.ndim - 1))
    channels = pl.BlockSpec((T, gps * SSM_GROUP_CH, NC), lambda g: (0, g, 0))
    return pl.pallas_call(
        functools.partial(_ssm_kernel, chunks_per_seq=chunks_per_seq),
        grid=(G // gps,),
        in_specs=[grp(a) for a in (arow, acol, ldt, bt, cab, dcol)] + [channels],
        out_specs=channels,
        out_shape=jax.ShapeDtypeStruct((T, SSM_WIDTH, NC), F32),
        scratch_shapes=[pltpu.VMEM((gps, 2 * T * SSM_GROUP_CH, LANES), F32)],
        compiler_params=pltpu.CompilerParams(
            dimension_semantics=("arbitrary",), vmem_limit_bytes=VMEM_LIMIT),
        name="ssm",
    )(arow, acol, ldt, bt, cab, dcol, e)


def _ssm_param_layouts(a_re, a_im, log_dt, b_re, b_im, c_re, c_im, d_skip):
    G = a_re.shape[0]
    arow = jnp.stack([jnp.concatenate([a_re, a_re], -1), jnp.concatenate([a_im, a_im], -1)], 1)
    acol = jnp.stack([a_re, a_im], -1)
    reps = LANES // SSM_GROUP_CH
    bt = jnp.stack([jnp.tile(b_re, (1, 1, reps)), jnp.tile(b_im, (1, 1, reps))], 1)
    cab = jnp.stack([jnp.concatenate([c_re, -c_im], -1), jnp.concatenate([-c_im, -c_re], -1)], 1)
    return (arow.astype(F32), acol.astype(F32), log_dt.reshape(G, 1, 1).astype(F32),
            bt.astype(F32), cab.astype(F32), d_skip.reshape(G, SSM_GROUP_CH, 1).astype(F32))


def _tail_kernel(h1_ref, attn_ref, y_ref, p_ref, wglu_ref, bglu_ref, ga_ref, gs_ref,
                 wo_ref, g2_ref, w1_ref, w3_ref, w2_ref, gp_ref, wpg_ref,
                 wpp_ref, gf_ref, o_ref):
    y = y_ref[...]
    glu = y * _sigmoid(jnp.dot(y.astype(BF16), wglu_ref[...], preferred_element_type=F32)
                       + bglu_ref[...])
    an = _rms(attn_ref[...], ga_ref[...]).astype(BF16)
    sn = _rms(glu, gs_ref[...]).astype(BF16)
    h = (h1_ref[...] + jnp.dot(an, wo_ref[:ATTN_WIDTH, :], preferred_element_type=F32)
         + jnp.dot(sn, wo_ref[ATTN_WIDTH:, :], preferred_element_type=F32))
    h = h + 0.5 * _swiglu(_rms(h, g2_ref[...]).astype(BF16), w1_ref, w3_ref, w2_ref)
    gate = _sigmoid(jnp.dot(_rms(h, gp_ref[...]).astype(BF16), wpg_ref[...],
                            preferred_element_type=F32))
    h = h + gate * jnp.dot(p_ref[...].astype(BF16), wpp_ref[...], preferred_element_type=F32)
    o_ref[...] = _rms(h, gf_ref[...])


def _tail_call(h1, attn, y, p, *consts):
    B, L, D = h1.shape
    tm = TOKEN_TILE
    tile = lambda w: pl.BlockSpec((None, tm, w), lambda b, i: (b, i, 0))
    return pl.pallas_call(
        _tail_kernel,
        grid=(B, L // tm),
        in_specs=[tile(D), tile(ATTN_WIDTH), tile(SSM_WIDTH), tile(PLE_DIM)]
                 + [_const_spec(c.shape) for c in consts],
        out_specs=tile(D),
        out_shape=jax.ShapeDtypeStruct((B, L, D), F32),
        compiler_params=pltpu.CompilerParams(
            dimension_semantics=("arbitrary", "arbitrary"),
            vmem_limit_bytes=VMEM_LIMIT),
        name="tail",
    )(h1, attn, y, p, *consts)


def kernel(x, p, g_ffn1, w1_a, w3_a, w2_a, g_mix, w_in, b_f, a_re, a_im, log_dt, b_re, b_im, c_re, c_im, d_skip, w_glu, b_glu, g_attn_out, g_ssm_out, w_out, g_ffn2, w1_b, w3_b, w2_b, g_ple, w_ple_gate, w_ple_proj, g_final):
    B, L, D = x.shape
    assert D == D_MODEL and L % ATTN_TILE == 0 and L % TOKEN_TILE == 0 and L % SSM_CHUNK == 0
    assert g_ffn1.shape[0] == 1, "single layer"
    assert TOKEN_TILE == ATTN_TILE, "decay bias rows are relative to the kv tile start"
    row = lambda g: g.reshape(1, -1).astype(F32)
    bf = lambda w: w.astype(BF16)
    w_in0 = w_in[0]
    assert w_in0.shape[1] == 3 * ATTN_WIDTH + ATTN_HEADS + SSM_WIDTH

    (h1, q, kt, v, s_in, ct, k_norm2, c_next), (w1_b16, w3_b16, w2_b16, w_out16, w_gate16) = _head_call(
        x, row(g_ffn1[0]), bf(w1_a[0]), bf(w3_a[0]), bf(w2_a[0]), row(g_mix[0]),
        bf(w_in0), bf(w_in0[:, 3 * ATTN_WIDTH + ATTN_HEADS:]),
        b_f[0].reshape(ATTN_HEADS, 1).astype(F32),
        later=tuple(w.astype(F32) for w in (w1_b[0], w3_b[0], w2_b[0], w_out[0], w_ple_gate[0])))

    T = SSM_CHUNK
    chunks_per_seq = L // T
    n_chunks = B * chunks_per_seq
    e = s_in.reshape(n_chunks, T, SSM_WIDTH).transpose(1, 2, 0)
    f = _ssm_call(*_ssm_param_layouts(a_re[0], a_im[0], log_dt[0], b_re[0], b_im[0],
                                      c_re[0], c_im[0], d_skip[0]),
                  e, chunks_per_seq)
    y = f.transpose(2, 0, 1).reshape(B, L, SSM_WIDTH)

    attn = _attn_call(q, kt, v, ct, k_norm2, c_next)

    return _tail_call(
        h1, attn, y, p[0],
        bf(w_glu[0]), row(b_glu[0]), row(g_attn_out[0]), row(g_ssm_out[0]),
        w_out16, row(g_ffn2[0]),
        w1_b16, w3_b16, w2_b16, row(g_ple[0]), w_gate16,
        bf(w_ple_proj[0]), row(g_final))
```

```python
import functools
import math

import jax
import jax.numpy as jnp
from jax import lax
from jax.experimental import pallas as pl
from jax.experimental.pallas import tpu as pltpu

D_MODEL = 1024
ATTN_HEADS = 8
HEAD_DIM = 64
ATTN_WIDTH = ATTN_HEADS * HEAD_DIM
SSM_WIDTH = D_MODEL - ATTN_WIDTH
SSM_GROUP_CH = 16
SSM_GROUPS = SSM_WIDTH // SSM_GROUP_CH
SSM_STATE = 64
D_FF = 2816
PLE_DIM = 256
EPS = 1e-6

LANES = 128
BF16_SUBLANES = 16
HEAD_PAIR = 2 * HEAD_DIM
N_PAIRS = ATTN_HEADS // 2
FF_CHUNK = 256
TOKEN_TILE = 512
ATTN_TILE = 512
SSM_CHUNK = 32
SSM_GROUPS_PER_STEP = 2
NEG_BIG = -1e30
SKIP_LOG2 = 140.0
BOUND_SLACK_MUL = 1.001
BOUND_SLACK_ADD = 1.0
LOG2E = math.log2(math.e)
N_BIAS = 3
BIAS_ROWS = 8
VMEM_LIMIT = 56 * 1024 * 1024

BF16 = jnp.bfloat16
F32 = jnp.float32


def _rms(x, g):
    ms = jnp.mean(x * x, axis=-1, keepdims=True)
    return x * lax.rsqrt(ms + EPS) * g


def _sigmoid(x):
    return 1.0 / (1.0 + jnp.exp(-x))


def _swiglu(xn, w1_ref, w3_ref, w2_ref):
    acc = None
    for c in range(D_FF // FF_CHUNK):
        sl = slice(c * FF_CHUNK, (c + 1) * FF_CHUNK)
        a = jnp.dot(xn, w1_ref[:, sl], preferred_element_type=F32)
        b = jnp.dot(xn, w3_ref[:, sl], preferred_element_type=F32)
        gated = (a * _sigmoid(a) * b).astype(BF16)
        part = jnp.dot(gated, w2_ref[sl, :], preferred_element_type=F32)
        acc = part if acc is None else acc + part
    return acc


def _const_spec(shape):
    nd = len(shape)
    return pl.BlockSpec(shape, lambda *_: (0,) * nd, pipeline_mode=pl.Buffered(1))


def _head_kernel(*refs, n_later):
    (x_ref, g1_ref, w1_ref, w3_ref, w2_ref, gm_ref, win_ref, ws_ref, bf_ref) = refs[:9]
    later_in = refs[9:9 + n_later]
    h1_ref, q_ref, kt_ref, v_ref, s_ref, ct_ref, kn_ref, cn_ref = refs[9 + n_later:17 + n_later]
    later_out = refs[17 + n_later:17 + 2 * n_later]
    carry_ref = refs[-1]
    @pl.when(pl.program_id(1) == 0)
    def _():
        carry_ref[...] = jnp.zeros_like(carry_ref)
        kn_ref[...] = jnp.zeros_like(kn_ref)
        cn_ref[...] = jnp.zeros_like(cn_ref)

    for src, dst in zip(later_in, later_out):
        dst[...] = src[...].astype(BF16)
    tm = x_ref.shape[0]
    x = x_ref[...]
    h1 = x + 0.5 * _swiglu(_rms(x, g1_ref[...]).astype(BF16), w1_ref, w3_ref, w2_ref)
    h1_ref[...] = h1
    un = _rms(h1, gm_ref[...]).astype(BF16)
    project = lambda w: jnp.dot(un, w, preferred_element_type=F32)
    zf = project(win_ref[:, 3 * ATTN_WIDTH:3 * ATTN_WIDTH + LANES])
    kv = project(win_ref[:, ATTN_WIDTH:3 * ATTN_WIDTH])
    zft = zf.T[:ATTN_HEADS, :] + bf_ref[...]
    logf = jnp.minimum(zft, 0.0) - jnp.log1p(jnp.exp(-jnp.abs(zft)))
    lane = lax.broadcasted_iota(jnp.int32, logf.shape, 1)
    c = logf
    shift = 1
    while shift < tm:
        c = c + jnp.where(lane >= shift, pltpu.roll(c, shift, 1), 0.0)
        shift *= 2

    c_abs = c + carry_ref[:, 0:1]
    ct_ref[...] = c_abs * LOG2E
    carry_ref[...] = jnp.broadcast_to(c_abs[:, tm - 1:tm], carry_ref.shape)

    rel = (c - c[:, 0:1]) * LOG2E
    hi = rel.astype(BF16).astype(F32)
    mid = (rel - hi).astype(BF16).astype(F32)
    lo = (rel - hi - mid).astype(BF16).astype(F32)
    kt = kv[:, :ATTN_WIDTH].astype(BF16).astype(F32).T
    vv = kv[:, ATTN_WIDTH:]
    k_sq = (kt * kt).reshape(ATTN_HEADS, HEAD_DIM, tm)
    tile = pl.program_id(1)
    stat_lane = lax.broadcasted_iota(jnp.int32, kn_ref.shape, 1)

    kn_ref[...] = jnp.where(stat_lane == tile,
                            jnp.max(jnp.sum(k_sq, axis=1), axis=-1, keepdims=True), kn_ref[...])
    cn_ref[...] = jnp.where(stat_lane == tile - 1, c_abs[:, 0:1] * LOG2E, cn_ref[...])
    sub = lax.broadcasted_iota(jnp.int32, (BIAS_ROWS, tm), 0)
    zeros = jnp.zeros((HEAD_DIM - BIAS_ROWS, tm), F32)
    vlane = lax.broadcasted_iota(jnp.int32, (tm, HEAD_PAIR), 1)
    for h in range(ATTN_HEADS):
        bias = jnp.where(sub == 0, -hi[h:h + 1],
                         jnp.where(sub == 1, -mid[h:h + 1],
                                   jnp.where(sub == 2, -lo[h:h + 1], 0.0)))
        k_h = kt[h * HEAD_DIM:(h + 1) * HEAD_DIM]
        vp = vv[:, (h // 2) * HEAD_PAIR:(h // 2 + 1) * HEAD_PAIR]
        if h % 2 == 0:
            kt_ref[h] = jnp.concatenate([k_h, bias, zeros], axis=0).astype(BF16)
            v_ref[h] = jnp.where(vlane < HEAD_DIM, vp,
                                 jnp.where(vlane == HEAD_DIM, 1.0, 0.0)).astype(BF16)
        else:
            kt_ref[h] = jnp.concatenate([bias, zeros, k_h], axis=0).astype(BF16)
            v_ref[h] = jnp.where(vlane >= HEAD_DIM, vp,
                                 jnp.where(vlane == 0, 1.0, 0.0)).astype(BF16)

    q_ref[...] = (project(win_ref[:, :ATTN_WIDTH]) * (LOG2E / math.sqrt(HEAD_DIM))).astype(BF16)
    s_ref[...] = project(ws_ref[...]).astype(BF16)


def _slab_spec(w, n_steps, steps_per_batch):
    rows = next(r for r in range(BF16_SUBLANES, w.shape[0] + 1, BF16_SUBLANES)
                if w.shape[0] % r == 0 and w.shape[0] // r <= n_steps)
    last = w.shape[0] // rows - 1
    return pl.BlockSpec((rows, w.shape[1]),
                        lambda b, i: (jnp.minimum(b * steps_per_batch + i, last), 0))


def _head_call(x, g1, w1, w3, w2, gm, w_in, ws, bf, later):
    B, L, D = x.shape
    tm = TOKEN_TILE
    tile = lambda w: pl.BlockSpec((None, tm, w), lambda b, i: (b, i, 0))
    slabs = [_slab_spec(w, B * (L // tm), L // tm) for w in later]
    out_shape = (
        jax.ShapeDtypeStruct((B, L, D), F32),
        jax.ShapeDtypeStruct((B, L, ATTN_WIDTH), BF16),
        jax.ShapeDtypeStruct((B, ATTN_HEADS, HEAD_PAIR, L), BF16),
        jax.ShapeDtypeStruct((B, ATTN_HEADS, L, HEAD_PAIR), BF16),
        jax.ShapeDtypeStruct((B, L, SSM_WIDTH), BF16),
        jax.ShapeDtypeStruct((B, ATTN_HEADS, L), F32),
        jax.ShapeDtypeStruct((B, ATTN_HEADS, LANES), F32),
        jax.ShapeDtypeStruct((B, ATTN_HEADS, LANES), F32),
    ) + tuple(jax.ShapeDtypeStruct(w.shape, BF16) for w in later)
    outs = pl.pallas_call(
        functools.partial(_head_kernel, n_later=len(later)),
        grid=(B, L // tm),
        in_specs=[tile(D), _const_spec(g1.shape), _const_spec(w1.shape),
                  _const_spec(w3.shape), _const_spec(w2.shape), _const_spec(gm.shape),
                  _const_spec(w_in.shape), _const_spec(ws.shape), _const_spec(bf.shape)] + slabs,
        out_specs=(tile(D), tile(ATTN_WIDTH),
                   pl.BlockSpec((None, ATTN_HEADS, HEAD_PAIR, tm), lambda b, i: (b, 0, 0, i)),
                   pl.BlockSpec((None, ATTN_HEADS, tm, HEAD_PAIR), lambda b, i: (b, 0, i, 0)),
                   tile(SSM_WIDTH),
                   pl.BlockSpec((None, ATTN_HEADS, tm), lambda b, i: (b, 0, i)),
                   pl.BlockSpec((None, ATTN_HEADS, LANES), lambda b, i: (b, 0, 0)),
                   pl.BlockSpec((None, ATTN_HEADS, LANES), lambda b, i: (b, 0, 0)))
                  + tuple(slabs),
        out_shape=out_shape,
        scratch_shapes=[pltpu.VMEM((ATTN_HEADS, LANES), F32)],
        compiler_params=pltpu.CompilerParams(
            dimension_semantics=("arbitrary", "arbitrary"),
            vmem_limit_bytes=VMEM_LIMIT),
        name="head",
    )(x, g1, w1, w3, w2, gm, w_in, ws, bf, *later)
    return outs[:8], outs[8:]


def _attn_kernel(q_ref, *refs):
    _attn_prepare(0, q_ref, *refs)

    def q_tile(n, carry):
        _attn_q_tile(n, q_ref, *refs)
        return carry

    lax.fori_loop(0, q_ref.shape[0] // ATTN_TILE, q_tile, 0)


def _tile_start(i):
    return pl.multiple_of(i * ATTN_TILE, ATTN_TILE)


def _q_heads(q):
    lane = lax.broadcasted_iota(jnp.int32, (1, HEAD_PAIR), 1)
    first = lane < HEAD_DIM
    ones_even = jnp.where((lane >= HEAD_DIM) & (lane < HEAD_DIM + N_BIAS), 1.0, 0.0).astype(BF16)
    ones_odd = jnp.where(lane < N_BIAS, 1.0, 0.0).astype(BF16)
    return first, (jnp.where(first, q, ones_even), jnp.where(first, ones_odd, q))


def _attn_prepare(n, q_ref, kt_ref, v_ref, c_ref, kn_ref, cn_ref, o_ref, s_e0, s_e1, s_o0, s_o1, m_ref,
                  acc_ref, count_ref):
    t = ATTN_TILE
    q = q_ref[pl.ds(_tile_start(n), t), :]
    first, q_heads = _q_heads(q)
    q_sq = q.astype(F32) * q.astype(F32)
    row = lax.broadcasted_iota(jnp.int32, (t, t), 0)
    col = lax.broadcasted_iota(jnp.int32, (t, t), 1)
    tile_id = lax.broadcasted_iota(jnp.int32, (1, LANES), 1)
    needed = tile_id < 0
    for h, s_ref in enumerate((s_o0, s_o1)):
        s = jnp.where(col <= row, jnp.dot(q_heads[h], kt_ref[h, :, pl.ds(_tile_start(n), t)],
                                          preferred_element_type=F32), NEG_BIG)
        s_ref[...] = s
        m = jnp.max(s, axis=-1, keepdims=True)
        m_ref[h] = m
        q_norm2 = jnp.max(jnp.sum(jnp.where(first == (h == 0), q_sq, 0.0), axis=-1, keepdims=True),
                          axis=0, keepdims=True)
        k_norm2, c_next = kn_ref[h:h + 1, :], cn_ref[h:h + 1, :]
        c_q = c_ref[h:h + 1, pl.ds(_tile_start(n), LANES)][:, 0:1]
        reach = jnp.sqrt(q_norm2 * k_norm2) * BOUND_SLACK_MUL + BOUND_SLACK_ADD + c_q - c_next
        needed = needed | (reach - jnp.min(m, axis=0, keepdims=True) >= -SKIP_LOG2)
    first_needed = jnp.min(jnp.where(needed & (tile_id < n), tile_id, n).astype(F32))
    count_ref[0] = n - first_needed.astype(jnp.int32)


def _attn_q_tile(n, q_ref, kt_ref, v_ref, c_ref, kn_ref, cn_ref, o_ref, s_e0, s_e1, s_o0, s_o1, m_ref,
                 acc_ref, count_ref):
    t = ATTN_TILE
    count = count_ref[0]
    start = _tile_start
    first, q_heads = _q_heads(q_ref[pl.ds(start(n), t), :])
    s_buf = ((s_e0, s_e1), (s_o0, s_o1))
    c_q = [c_ref[h:h + 1, pl.ds(start(n), LANES)][:, 0:1] for h in range(2)]

    def qk(h, kv):
        return jnp.dot(q_heads[h], kt_ref[h, :, pl.ds(start(kv), t)], preferred_element_type=F32)

    def consume(h, s, kv):
        d = c_ref[h:h + 1, pl.ds(start(kv), LANES)][:, 0:1] - c_q[h]
        m_old = m_ref[h]
        m_new = jnp.maximum(m_old, jnp.max(s, axis=-1, keepdims=True) - d)
        p = jnp.exp2(s - (m_new + d)).astype(BF16)
        m_ref[h] = m_new
        acc_ref[h] = jnp.exp2(m_old - m_new) * acc_ref[h] + jnp.dot(
            p, v_ref[h, pl.ds(start(kv), t), :], preferred_element_type=F32)

    def step(kv, par):
        for h in range(2):
            s_buf[1 - par][h][...] = qk(h, kv - 1)
        for h in range(2):
            consume(h, s_buf[par][h][...], kv)

    for h in range(2):
        s_buf[0][h][...] = qk(h, jnp.maximum(n - 1, 0))
    for h in range(2):
        acc_ref[h] = jnp.dot(jnp.exp2(s_buf[1][h][...] - m_ref[h]).astype(BF16),
                             v_ref[h, pl.ds(start(n), t), :], preferred_element_type=F32)

    def pair(i, carry):
        step(n - 1 - 2 * i, 0)
        step(n - 2 - 2 * i, 1)
        return carry

    n_pairs = jnp.maximum(count - 1, 0) // 2
    lax.fori_loop(0, n_pairs, pair, 0)
    left = count - 2 * n_pairs
    last = n - count

    @pl.when(left == 2)
    def _():
        step(last + 1, 0)
        for h in range(2):
            consume(h, s_buf[1][h][...], last)

    @pl.when(left == 1)
    def _():
        for h in range(2):
            consume(h, s_buf[0][h][...], last)

    acc0, acc1 = acc_ref[0], acc_ref[1]
    o_ref[pl.ds(start(n), t), :] = jnp.where(first, acc0 / acc0[:, HEAD_DIM:HEAD_DIM + 1],
                                             acc1 / acc1[:, 0:1])
    n_tiles = q_ref.shape[0] // t
    _attn_prepare(jnp.minimum(n + 1, n_tiles - 1), q_ref, kt_ref, v_ref, c_ref, kn_ref, cn_ref, o_ref,
                  s_e0, s_e1, s_o0, s_o1, m_ref, acc_ref, count_ref)


def _attn_call(q, kt, v, ct, k_norm2, c_next):
    B, L, _ = q.shape
    t = ATTN_TILE
    assert L // t <= LANES
    by_pair = lambda a: a.reshape(B, N_PAIRS, 2, a.shape[-1])
    pair_block = lambda *shape: pl.BlockSpec((None, 2) + shape, lambda b, p: (b, p, 0, 0))
    pair_rows = lambda w: pl.BlockSpec((None, None, 2, w), lambda b, p: (b, p, 0, 0))
    lanes_of_pair = pl.BlockSpec((None, L, HEAD_PAIR), lambda b, p: (b, 0, p))
    return pl.pallas_call(
        _attn_kernel,
        grid=(B, N_PAIRS),
        in_specs=[lanes_of_pair, pair_block(HEAD_PAIR, L), pair_block(L, HEAD_PAIR),
                  pair_rows(L), pair_rows(LANES), pair_rows(LANES)],
        out_specs=lanes_of_pair,
        out_shape=jax.ShapeDtypeStruct((B, L, ATTN_WIDTH), F32),
        scratch_shapes=[pltpu.VMEM((t, t), F32)] * 4
                       + [pltpu.VMEM((2, t, 1), F32), pltpu.VMEM((2, t, HEAD_PAIR), F32),
                          pltpu.SMEM((1,), jnp.int32)],
        compiler_params=pltpu.CompilerParams(
            dimension_semantics=("arbitrary", "arbitrary"),
            vmem_limit_bytes=VMEM_LIMIT),
        name="attn",
    )(q, kt, v, by_pair(ct), by_pair(k_norm2), by_pair(c_next))


def _gelu_tanh(x):
    return 0.5 * x * (1.0 + jnp.tanh(math.sqrt(2.0 / math.pi) * (x + 0.044715 * (x * x * x))))


def _cis(mag_arg, ang):
    mag = jnp.exp(mag_arg)
    return mag * jnp.cos(ang), mag * jnp.sin(ang)


def _cmul(ar, ai, br, bi):
    return ar * br - ai * bi, ar * bi + ai * br


def _cpow2(zr, zi, n):
    assert n & (n - 1) == 0
    while n > 1:
        zr, zi = zr * zr - zi * zi, 2.0 * zr * zi
        n //= 2
    return zr, zi


def _ssm_kernel(arow_ref, acol_ref, ldt_ref, bt_ref, cab_ref, dcol_ref, e_ref, f_ref, z_ref,
                *, chunks_per_seq):
    H = SSM_GROUP_CH

    @pl.when(pl.program_id(0) == 0)
    def _():
        z_ref[:, 0:SSM_CHUNK * H, :] = jnp.zeros((z_ref.shape[0], SSM_CHUNK * H, LANES), F32)

    for gi in range(SSM_GROUPS_PER_STEP):
        y = _ssm_group(arow_ref.at[gi], acol_ref.at[gi], ldt_ref.at[gi], bt_ref.at[gi], cab_ref.at[gi],
                       dcol_ref.at[gi], e_ref[:, gi * H:(gi + 1) * H, :], z_ref.at[gi], chunks_per_seq)
        f_ref[:, gi * H:(gi + 1) * H, :] = y.reshape(SSM_CHUNK, H, y.shape[-1])


def _ssm_group(arow_ref, acol_ref, ldt_ref, bt_ref, cab_ref, dcol_ref, e, z_ref, chunks_per_seq):
    T, P, H = SSM_CHUNK, SSM_STATE, SSM_GROUP_CH
    TH = T * H
    hi = lax.Precision.HIGHEST
    dt = jnp.exp(ldt_ref[...])

    lam_r, lam_i = dt * arow_ref[0:1, :], dt * arow_ref[1:2, :]
    j0 = lax.broadcasted_iota(jnp.int32, (T, 2 * P), 0).astype(F32)
    pa0, pb0 = _cis(j0 * lam_r, j0 * lam_i)
    pa1, pb1 = _cmul(pa0, pb0, *_cis(lam_r, lam_i))
    over_h = lambda a: jnp.concatenate(
        [jnp.broadcast_to(a[j:j + 1, :], (H, 2 * P)) for j in range(T)], axis=0)
    ca, cb = jnp.tile(cab_ref[0], (T, 1)), jnp.tile(cab_ref[1], (T, 1))
    c_pow0 = over_h(pa0) * ca + over_h(pb0) * cb
    c_pow1 = over_h(pa1) * ca + over_h(pb1) * cb

    a_r, a_i = acol_ref[:, 0:1], acol_ref[:, 1:2]
    lr, li = dt * a_r, dt * a_i
    abar_r, abar_i = _cis(lr, li)
    nr, ni = abar_r - 1.0, abar_i
    den = a_r * a_r + a_i * a_i
    fr, fi = (nr * a_r + ni * a_i) / den, (ni * a_r - nr * a_i) / den
    b_r, b_i = bt_ref[0], bt_ref[1]
    bb_r, bb_i = fr * b_r - fi * b_i, fr * b_i + fi * b_r

    kcol = jnp.dot(c_pow0, jnp.concatenate([bb_r, bb_i], axis=0), precision=hi,
                   preferred_element_type=F32)
    lane_h = lax.broadcasted_iota(jnp.int32, (H, LANES), 1) % H
    skip = jnp.where(lane_h == lax.broadcasted_iota(jnp.int32, (H, LANES), 0), dcol_ref[...], 0.0)

    z_ref[TH:2 * TH, :] = kcol
    z_ref[TH:TH + H, :] = kcol[:H] + skip
    lane_group = lax.broadcasted_iota(jnp.int32, (1, LANES), 1) // H
    groups_per_block = LANES // H
    blocks = []
    for v in range(TH // LANES):
        blk = None
        for u in range(groups_per_block):
            s = v * groups_per_block + u
            piece = z_ref[TH - H * s:2 * TH - H * s, :]
            blk = piece if blk is None else jnp.where(lane_group == u, piece, blk)
        blocks.append(blk.astype(BF16))
    mt = jnp.concatenate(blocks, axis=1)

    expo = (groups_per_block - 1 - lane_group).astype(F32)
    wr, wi = _cis(lr * expo, li * expo)
    hop_r, hop_i = _cpow2(abar_r, abar_i, groups_per_block)
    w1_r, w1_i = [], []
    for v in range(TH // LANES):
        w1_r.insert(0, wr * bb_r - wi * bb_i)
        w1_i.insert(0, wr * bb_i + wi * bb_r)
        wr, wi = _cmul(wr, wi, hop_r, hop_i)
    w1t = jnp.concatenate([jnp.concatenate(w1_r, axis=1),
                           jnp.concatenate(w1_i, axis=1)], axis=0).astype(BF16)

    e = e.reshape(TH, e.shape[-1])
    y = jnp.dot(mt, e, preferred_element_type=F32)
    st = jnp.dot(w1t, e, preferred_element_type=F32)
    sr, si = st[:P], st[P:]
    pos = lax.broadcasted_iota(jnp.int32, sr.shape, 1) % chunks_per_seq

    def shifted(a, shift):
        return jnp.where(pos >= shift, pltpu.roll(a, shift, 1), 0.0)

    qr, qi = _cpow2(abar_r, abar_i, T)
    shift = 1
    while shift < chunks_per_seq:
        srs, sis = shifted(sr, shift), shifted(si, shift)
        sr, si = sr + qr * srs - qi * sis, si + qr * sis + qi * srs
        qr, qi = qr * qr - qi * qi, 2.0 * qr * qi
        shift *= 2
    x_prev = jnp.concatenate([shifted(sr, 1), shifted(si, 1)], axis=0).astype(BF16)
    y = y + jnp.dot(c_pow1.astype(BF16), x_prev, preferred_element_type=F32)
    return _gelu_tanh(y)


def _ssm_call(arow, acol, ldt, bt, cab, dcol, e, chunks_per_seq):
    T, _, NC = e.shape
    G = arow.shape[0]
    gps = SSM_GROUPS_PER_STEP
    assert G % gps == 0
    grp = lambda a: pl.BlockSpec((gps,) + a.shape[1:], lambda g: (g,) + (0,) * (a.ndim - 1))
    channels = pl.BlockSpec((T, gps * SSM_GROUP_CH, NC), lambda g: (0, g, 0))
    return pl.pallas_call(
        functools.partial(_ssm_kernel, chunks_per_seq=chunks_per_seq),
        grid=(G // gps,),
        in_specs=[grp(a) for a in (arow, acol, ldt, bt, cab, dcol)] + [channels],
        out_specs=channels,
        out_shape=jax.ShapeDtypeStruct((T, SSM_WIDTH, NC), F32),
        scratch_shapes=[pltpu.VMEM((gps, 2 * T * SSM_GROUP_CH, LANES), F32)],
        compiler_params=pltpu.CompilerParams(
            dimension_semantics=("arbitrary",), vmem_limit_bytes=VMEM_LIMIT),
        name="ssm",
    )(arow, acol, ldt, bt, cab, dcol, e)


def _ssm_param_layouts(a_re, a_im, log_dt, b_re, b_im, c_re, c_im, d_skip):
    G = a_re.shape[0]
    arow = jnp.stack([jnp.concatenate([a_re, a_re], -1), jnp.concatenate([a_im, a_im], -1)], 1)
    acol = jnp.stack([a_re, a_im], -1)
    reps = LANES // SSM_GROUP_CH
    bt = jnp.stack([jnp.tile(b_re, (1, 1, reps)), jnp.tile(b_im, (1, 1, reps))], 1)
    cab = jnp.stack([jnp.concatenate([c_re, -c_im], -1), jnp.concatenate([-c_im, -c_re], -1)], 1)
    return (arow.astype(F32), acol.astype(F32), log_dt.reshape(G, 1, 1).astype(F32),
            bt.astype(F32), cab.astype(F32), d_skip.reshape(G, SSM_GROUP_CH, 1).astype(F32))


def _tail_kernel(h1_ref, attn_ref, y_ref, p_ref, wglu_ref, bglu_ref, ga_ref, gs_ref,
                 wo_ref, g2_ref, w1_ref, w3_ref, w2_ref, gp_ref, wpg_ref,
                 wpp_ref, gf_ref, o_ref):
    y = y_ref[...]
    glu = y * _sigmoid(jnp.dot(y.astype(BF16), wglu_ref[...], preferred_element_type=F32)
                       + bglu_ref[...])
    an = _rms(attn_ref[...], ga_ref[...]).astype(BF16)
    sn = _rms(glu, gs_ref[...]).astype(BF16)
    h = (h1_ref[...] + jnp.dot(an, wo_ref[:ATTN_WIDTH, :], preferred_element_type=F32)
         + jnp.dot(sn, wo_ref[ATTN_WIDTH:, :], preferred_element_type=F32))
    h = h + 0.5 * _swiglu(_rms(h, g2_ref[...]).astype(BF16), w1_ref, w3_ref, w2_ref)
    gate = _sigmoid(jnp.dot(_rms(h, gp_ref[...]).astype(BF16), wpg_ref[...],
                            preferred_element_type=F32))
    h = h + gate * jnp.dot(p_ref[...].astype(BF16), wpp_ref[...], preferred_element_type=F32)
    o_ref[...] = _rms(h, gf_ref[...])


def _tail_call(h1, attn, y, p, *consts):
    B, L, D = h1.shape
    tm = TOKEN_TILE
    tile = lambda w: pl.BlockSpec((None, tm, w), lambda b, i: (b, i, 0))
    return pl.pallas_call(
        _tail_kernel,
        grid=(B, L // tm),
        in_specs=[tile(D), tile(ATTN_WIDTH), tile(SSM_WIDTH), tile(PLE_DIM)]
                 + [_const_spec(c.shape) for c in consts],
        out_specs=tile(D),
        out_shape=jax.ShapeDtypeStruct((B, L, D), F32),
        compiler_params=pltpu.CompilerParams(
            dimension_semantics=("arbitrary", "arbitrary"),
            vmem_limit_bytes=VMEM_LIMIT),
        name="tail",
    )(h1, attn, y, p, *consts)


def kernel(x, p, g_ffn1, w1_a, w3_a, w2_a, g_mix, w_in, b_f, a_re, a_im, log_dt, b_re, b_im, c_re, c_im, d_skip, w_glu, b_glu, g_attn_out, g_ssm_out, w_out, g_ffn2, w1_b, w3_b, w2_b, g_ple, w_ple_gate, w_ple_proj, g_final):
    B, L, D = x.shape
    assert D == D_MODEL and L % ATTN_TILE == 0 and L % TOKEN_TILE == 0 and L % SSM_CHUNK == 0
    assert g_ffn1.shape[0] == 1, "single layer"
    assert TOKEN_TILE == ATTN_TILE, "decay bias rows are relative to the kv tile start"
    row = lambda g: g.reshape(1, -1).astype(F32)
    bf = lambda w: w.astype(BF16)
    w_in0 = w_in[0]
    assert w_in0.shape[1] == 3 * ATTN_WIDTH + ATTN_HEADS + SSM_WIDTH

    (h1, q, kt, v, s_in, ct, k_norm2, c_next), (w1_b16, w3_b16, w2_b16, w_out16, w_gate16) = _head_call(
        x, row(g_ffn1[0]), bf(w1_a[0]), bf(w3_a[0]), bf(w2_a[0]), row(g_mix[0]),
        bf(w_in0), bf(w_in0[:, 3 * ATTN_WIDTH + ATTN_HEADS:]),
        b_f[0].reshape(ATTN_HEADS, 1).astype(F32),
        later=tuple(w.astype(F32) for w in (w1_b[0], w3_b[0], w2_b[0], w_out[0], w_ple_gate[0])))

    T = SSM_CHUNK
    chunks_per_seq = L // T
    n_chunks = B * chunks_per_seq
    e = s_in.reshape(n_chunks, T, SSM_WIDTH).transpose(1, 2, 0)
    f = _ssm_call(*_ssm_param_layouts(a_re[0], a_im[0], log_dt[0], b_re[0], b_im[0],
                                      c_re[0], c_im[0], d_skip[0]),
                  e, chunks_per_seq)
    y = f.transpose(2, 0, 1).reshape(B, L, SSM_WIDTH)

    attn = _attn_call(q, kt, v, ct, k_norm2, c_next)

    return _tail_call(
        h1, attn, y, p[0],
        bf(w_glu[0]), row(b_glu[0]), row(g_attn_out[0]), row(g_ssm_out[0]),
        w_out16, row(g_ffn2[0]),
        w1_b16, w3_b16, w2_b16, row(g_ple[0]), w_gate16,
        bf(w_ple_proj[0]), row(g_final))
```

```python
import functools
import math

import jax
import jax.numpy as jnp
from jax import lax
from jax.experimental import pallas as pl
from jax.experimental.pallas import tpu as pltpu

D_MODEL = 1024
ATTN_HEADS = 8
HEAD_DIM = 64
ATTN_WIDTH = ATTN_HEADS * HEAD_DIM
SSM_WIDTH = D_MODEL - ATTN_WIDTH
SSM_GROUP_CH = 16
SSM_GROUPS = SSM_WIDTH // SSM_GROUP_CH
SSM_STATE = 64
D_FF = 2816
PLE_DIM = 256
EPS = 1e-6

LANES = 128
BF16_SUBLANES = 16
HEAD_PAIR = 2 * HEAD_DIM
N_PAIRS = ATTN_HEADS // 2
FF_CHUNK = 256
TOKEN_TILE = 512
ATTN_TILE = 512
SSM_CHUNK = 32
SSM_GROUPS_PER_STEP = 2
NEG_BIG = -1e30
SKIP_LOG2 = 140.0
BOUND_SLACK_MUL = 1.001
BOUND_SLACK_ADD = 1.0
LOG2E = math.log2(math.e)
N_BIAS = 3
BIAS_ROWS = 8
VMEM_LIMIT = 56 * 1024 * 1024

BF16 = jnp.bfloat16
F32 = jnp.float32


def _rms(x, g):
    ms = jnp.mean(x * x, axis=-1, keepdims=True)
    return x * lax.rsqrt(ms + EPS) * g


def _sigmoid(x):
    return 1.0 / (1.0 + jnp.exp(-x))


def _swiglu(xn, w1_ref, w3_ref, w2_ref):
    acc = None
    for c in range(D_FF // FF_CHUNK):
        sl = slice(c * FF_CHUNK, (c + 1) * FF_CHUNK)
        a = jnp.dot(xn, w1_ref[:, sl], preferred_element_type=F32)
        b = jnp.dot(xn, w3_ref[:, sl], preferred_element_type=F32)
        gated = (a * _sigmoid(a) * b).astype(BF16)
        part = jnp.dot(gated, w2_ref[sl, :], preferred_element_type=F32)
        acc = part if acc is None else acc + part
    return acc


def _const_spec(shape):
    nd = len(shape)
    return pl.BlockSpec(shape, lambda *_: (0,) * nd, pipeline_mode=pl.Buffered(1))


def _head_kernel(*refs, n_later):
    (x_ref, g1_ref, w1_ref, w3_ref, w2_ref, gm_ref, win_ref, ws_ref, bf_ref) = refs[:9]
    later_in = refs[9:9 + n_later]
    h1_ref, q_ref, kt_ref, v_ref, s_ref, ct_ref, kn_ref, cn_ref = refs[9 + n_later:17 + n_later]
    later_out = refs[17 + n_later:17 + 2 * n_later]
    carry_ref = refs[-1]
    @pl.when(pl.program_id(1) == 0)
    def _():
        carry_ref[...] = jnp.zeros_like(carry_ref)
        kn_ref[...] = jnp.zeros_like(kn_ref)
        cn_ref[...] = jnp.zeros_like(cn_ref)

    for src, dst in zip(later_in, later_out):
        dst[...] = src[...].astype(BF16)
    tm = x_ref.shape[0]
    x = x_ref[...]
    h1 = x + 0.5 * _swiglu(_rms(x, g1_ref[...]).astype(BF16), w1_ref, w3_ref, w2_ref)
    h1_ref[...] = h1
    un = _rms(h1, gm_ref[...]).astype(BF16)
    project = lambda w: jnp.dot(un, w, preferred_element_type=F32)
    zf = project(win_ref[:, 3 * ATTN_WIDTH:3 * ATTN_WIDTH + LANES])
    kv = project(win_ref[:, ATTN_WIDTH:3 * ATTN_WIDTH])
    zft = zf.T[:ATTN_HEADS, :] + bf_ref[...]
    logf = jnp.minimum(zft, 0.0) - jnp.log1p(jnp.exp(-jnp.abs(zft)))
    lane = lax.broadcasted_iota(jnp.int32, logf.shape, 1)
    c = logf
    shift = 1
    while shift < tm:
        c = c + jnp.where(lane >= shift, pltpu.roll(c, shift, 1), 0.0)
        shift *= 2

    c_abs = c + carry_ref[:, 0:1]
    ct_ref[...] = c_abs * LOG2E
    carry_ref[...] = jnp.broadcast_to(c_abs[:, tm - 1:tm], carry_ref.shape)

    rel = (c - c[:, 0:1]) * LOG2E
    hi = rel.astype(BF16).astype(F32)
    mid = (rel - hi).astype(BF16).astype(F32)
    lo = (rel - hi - mid).astype(BF16).astype(F32)
    kt = kv[:, :ATTN_WIDTH].astype(BF16).astype(F32).T
    vv = kv[:, ATTN_WIDTH:]
    k_sq = (kt * kt).reshape(ATTN_HEADS, HEAD_DIM, tm)
    tile = pl.program_id(1)
    stat_lane = lax.broadcasted_iota(jnp.int32, kn_ref.shape, 1)

    kn_ref[...] = jnp.where(stat_lane == tile,
                            jnp.max(jnp.sum(k_sq, axis=1), axis=-1, keepdims=True), kn_ref[...])
    cn_ref[...] = jnp.where(stat_lane == tile - 1, c_abs[:, 0:1] * LOG2E, cn_ref[...])
    sub = lax.broadcasted_iota(jnp.int32, (BIAS_ROWS, tm), 0)
    zeros = jnp.zeros((HEAD_DIM - BIAS_ROWS, tm), F32)
    vlane = lax.broadcasted_iota(jnp.int32, (tm, HEAD_PAIR), 1)
    for h in range(ATTN_HEADS):
        bias = jnp.where(sub == 0, -hi[h:h + 1],
                         jnp.where(sub == 1, -mid[h:h + 1],
                                   jnp.where(sub == 2, -lo[h:h + 1], 0.0)))
        k_h = kt[h * HEAD_DIM:(h + 1) * HEAD_DIM]
        vp = vv[:, (h // 2) * HEAD_PAIR:(h // 2 + 1) * HEAD_PAIR]
        if h % 2 == 0:
            kt_ref[h] = jnp.concatenate([k_h, bias, zeros], axis=0).astype(BF16)
            v_ref[h] = jnp.where(vlane < HEAD_DIM, vp,
                                 jnp.where(vlane == HEAD_DIM, 1.0, 0.0)).astype(BF16)
        else:
            kt_ref[h] = jnp.concatenate([bias, zeros, k_h], axis=0).astype(BF16)
            v_ref[h] = jnp.where(vlane >= HEAD_DIM, vp,
                                 jnp.where(vlane == 0, 1.0, 0.0)).astype(BF16)

    q_ref[...] = (project(win_ref[:, :ATTN_WIDTH]) * (LOG2E / math.sqrt(HEAD_DIM))).astype(BF16)
    s_ref[...] = project(ws_ref[...]).astype(BF16)


def _slab_spec(w, n_steps, steps_per_batch):
    rows = next(r for r in range(BF16_SUBLANES, w.shape[0] + 1, BF16_SUBLANES)
                if w.shape[0] % r == 0 and w.shape[0] // r <= n_steps)
    last = w.shape[0] // rows - 1
    return pl.BlockSpec((rows, w.shape[1]),
                        lambda b, i: (jnp.minimum(b * steps_per_batch + i, last), 0))


def _head_call(x, g1, w1, w3, w2, gm, w_in, ws, bf, later):
    B, L, D = x.shape
    tm = TOKEN_TILE
    tile = lambda w: pl.BlockSpec((None, tm, w), lambda b, i: (b, i, 0))
    slabs = [_slab_spec(w, B * (L // tm), L // tm) for w in later]
    out_shape = (
        jax.ShapeDtypeStruct((B, L, D), F32),
        jax.ShapeDtypeStruct((B, L, ATTN_WIDTH), BF16),
        jax.ShapeDtypeStruct((B, ATTN_HEADS, HEAD_PAIR, L), BF16),
        jax.ShapeDtypeStruct((B, ATTN_HEADS, L, HEAD_PAIR), BF16),
        jax.ShapeDtypeStruct((B, L, SSM_WIDTH), BF16),
        jax.ShapeDtypeStruct((B, ATTN_HEADS, L), F32),
        jax.ShapeDtypeStruct((B, ATTN_HEADS, LANES), F32),
        jax.ShapeDtypeStruct((B, ATTN_HEADS, LANES), F32),
    ) + tuple(jax.ShapeDtypeStruct(w.shape, BF16) for w in later)
    outs = pl.pallas_call(
        functools.partial(_head_kernel, n_later=len(later)),
        grid=(B, L // tm),
        in_specs=[tile(D), _const_spec(g1.shape), _const_spec(w1.shape),
                  _const_spec(w3.shape), _const_spec(w2.shape), _const_spec(gm.shape),
                  _const_spec(w_in.shape), _const_spec(ws.shape), _const_spec(bf.shape)] + slabs,
        out_specs=(tile(D), tile(ATTN_WIDTH),
                   pl.BlockSpec((None, ATTN_HEADS, HEAD_PAIR, tm), lambda b, i: (b, 0, 0, i)),
                   pl.BlockSpec((None, ATTN_HEADS, tm, HEAD_PAIR), lambda b, i: (b, 0, i, 0)),
                   tile(SSM_WIDTH),
                   pl.BlockSpec((None, ATTN_HEADS, tm), lambda b, i: (b, 0, i)),
                   pl.BlockSpec((None, ATTN_HEADS, LANES), lambda b, i: (b, 0, 0)),
                   pl.BlockSpec((None, ATTN_HEADS, LANES), lambda b, i: (b, 0, 0)))
                  + tuple(slabs),
        out_shape=out_shape,
        scratch_shapes=[pltpu.VMEM((ATTN_HEADS, LANES), F32)],
        compiler_params=pltpu.CompilerParams(
            dimension_semantics=("arbitrary", "arbitrary"),
            vmem_limit_bytes=VMEM_LIMIT),
        name="head",
    )(x, g1, w1, w3, w2, gm, w_in, ws, bf, *later)
    return outs[:8], outs[8:]


def _attn_kernel(q_ref, *refs):
    _attn_prepare(0, q_ref, *refs)
    _attn_first_stage(0, q_ref, *refs)

    def q_tile(n, carry):
        _attn_q_tile(n, q_ref, *refs)
        return carry

    lax.fori_loop(0, q_ref.shape[0] // ATTN_TILE, q_tile, 0)


def _tile_start(i):
    return pl.multiple_of(i * ATTN_TILE, ATTN_TILE)


def _q_heads(q):
    lane = lax.broadcasted_iota(jnp.int32, (1, HEAD_PAIR), 1)
    first = lane < HEAD_DIM
    ones_even = jnp.where((lane >= HEAD_DIM) & (lane < HEAD_DIM + N_BIAS), 1.0, 0.0).astype(BF16)
    ones_odd = jnp.where(lane < N_BIAS, 1.0, 0.0).astype(BF16)
    return first, (jnp.where(first, q, ones_even), jnp.where(first, ones_odd, q))


def _attn_prepare(n, q_ref, kt_ref, v_ref, c_ref, kn_ref, cn_ref, o_ref, s_e0, s_e1, s_o0, s_o1, m_ref,
                  acc_ref, count_ref):
    t = ATTN_TILE
    q = q_ref[pl.ds(_tile_start(n), t), :]
    first, q_heads = _q_heads(q)
    q_sq = q.astype(F32) * q.astype(F32)
    row = lax.broadcasted_iota(jnp.int32, (t, t), 0)
    col = lax.broadcasted_iota(jnp.int32, (t, t), 1)
    tile_id = lax.broadcasted_iota(jnp.int32, (1, LANES), 1)
    needed = tile_id < 0
    for h, s_ref in enumerate((s_o0, s_o1)):
        s = jnp.where(col <= row, jnp.dot(q_heads[h], kt_ref[h, :, pl.ds(_tile_start(n), t)],
                                          preferred_element_type=F32), NEG_BIG)
        s_ref[...] = s
        m = jnp.max(s, axis=-1, keepdims=True)
        m_ref[h] = m
        q_norm2 = jnp.max(jnp.sum(jnp.where(first == (h == 0), q_sq, 0.0), axis=-1, keepdims=True),
                          axis=0, keepdims=True)
        k_norm2, c_next = kn_ref[h:h + 1, :], cn_ref[h:h + 1, :]
        c_q = c_ref[h:h + 1, pl.ds(_tile_start(n), LANES)][:, 0:1]
        reach = jnp.sqrt(q_norm2 * k_norm2) * BOUND_SLACK_MUL + BOUND_SLACK_ADD + c_q - c_next
        needed = needed | (reach - jnp.min(m, axis=0, keepdims=True) >= -SKIP_LOG2)
    first_needed = jnp.min(jnp.where(needed & (tile_id < n), tile_id, n).astype(F32))
    count_ref[0] = n - first_needed.astype(jnp.int32)


def _attn_first_stage(n, q_ref, kt_ref, v_ref, c_ref, kn_ref, cn_ref, o_ref, s_e0, s_e1, s_o0, s_o1,
                      m_ref, acc_ref, count_ref):
    t = ATTN_TILE
    _, q_heads = _q_heads(q_ref[pl.ds(_tile_start(n), t), :])
    older = _tile_start(jnp.maximum(n - 1, 0))
    for h, s_ref in enumerate((s_e0, s_e1)):
        s_ref[...] = jnp.dot(q_heads[h], kt_ref[h, :, pl.ds(older, t)], preferred_element_type=F32)
    for h, s_ref in enumerate((s_o0, s_o1)):
        acc_ref[h] = jnp.dot(jnp.exp2(s_ref[...] - m_ref[h]).astype(BF16),
                             v_ref[h, pl.ds(_tile_start(n), t), :], preferred_element_type=F32)


def _attn_q_tile(n, q_ref, kt_ref, v_ref, c_ref, kn_ref, cn_ref, o_ref, s_e0, s_e1, s_o0, s_o1, m_ref,
                 acc_ref, count_ref):
    t = ATTN_TILE
    count = count_ref[0]
    start = _tile_start
    first, q_heads = _q_heads(q_ref[pl.ds(start(n), t), :])
    s_buf = ((s_e0, s_e1), (s_o0, s_o1))
    c_q = [c_ref[h:h + 1, pl.ds(start(n), LANES)][:, 0:1] for h in range(2)]

    def qk(h, kv):
        return jnp.dot(q_heads[h], kt_ref[h, :, pl.ds(start(kv), t)], preferred_element_type=F32)

    def consume(h, s, kv):
        d = c_ref[h:h + 1, pl.ds(start(kv), LANES)][:, 0:1] - c_q[h]
        m_old = m_ref[h]
        m_new = jnp.maximum(m_old, jnp.max(s, axis=-1, keepdims=True) - d)
        p = jnp.exp2(s - (m_new + d)).astype(BF16)
        m_ref[h] = m_new
        acc_ref[h] = jnp.exp2(m_old - m_new) * acc_ref[h] + jnp.dot(
            p, v_ref[h, pl.ds(start(kv), t), :], preferred_element_type=F32)

    def step(kv, par):
        for h in range(2):
            s_buf[1 - par][h][...] = qk(h, kv - 1)
        for h in range(2):
            consume(h, s_buf[par][h][...], kv)

    def pair(i, carry):
        step(n - 1 - 2 * i, 0)
        step(n - 2 - 2 * i, 1)
        return carry

    n_pairs = jnp.maximum(count - 1, 0) // 2
    lax.fori_loop(0, n_pairs, pair, 0)
    left = count - 2 * n_pairs
    last = n - count

    @pl.when(left == 2)
    def _():
        step(last + 1, 0)
        for h in range(2):
            consume(h, s_buf[1][h][...], last)

    @pl.when(left == 1)
    def _():
        for h in range(2):
            consume(h, s_buf[0][h][...], last)

    acc0, acc1 = acc_ref[0], acc_ref[1]
    o_ref[pl.ds(start(n), t), :] = jnp.where(first, acc0 / acc0[:, HEAD_DIM:HEAD_DIM + 1],
                                             acc1 / acc1[:, 0:1])
    n_tiles = q_ref.shape[0] // t
    nxt = jnp.minimum(n + 1, n_tiles - 1)
    rest = (q_ref, kt_ref, v_ref, c_ref, kn_ref, cn_ref, o_ref, s_e0, s_e1, s_o0, s_o1, m_ref,
            acc_ref, count_ref)
    _attn_prepare(nxt, *rest)
    _attn_first_stage(nxt, *rest)


def _attn_call(q, kt, v, ct, k_norm2, c_next):
    B, L, _ = q.shape
    t = ATTN_TILE
    assert L // t <= LANES
    by_pair = lambda a: a.reshape(B, N_PAIRS, 2, a.shape[-1])
    pair_block = lambda *shape: pl.BlockSpec((None, 2) + shape, lambda b, p: (b, p, 0, 0))
    pair_rows = lambda w: pl.BlockSpec((None, None, 2, w), lambda b, p: (b, p, 0, 0))
    lanes_of_pair = pl.BlockSpec((None, L, HEAD_PAIR), lambda b, p: (b, 0, p))
    return pl.pallas_call(
        _attn_kernel,
        grid=(B, N_PAIRS),
        in_specs=[lanes_of_pair, pair_block(HEAD_PAIR, L), pair_block(L, HEAD_PAIR),
                  pair_rows(L), pair_rows(LANES), pair_rows(LANES)],
        out_specs=lanes_of_pair,
        out_shape=jax.ShapeDtypeStruct((B, L, ATTN_WIDTH), F32),
        scratch_shapes=[pltpu.VMEM((t, t), F32)] * 4
                       + [pltpu.VMEM((2, t, 1), F32), pltpu.VMEM((2, t, HEAD_PAIR), F32),
                          pltpu.SMEM((1,), jnp.int32)],
        compiler_params=pltpu.CompilerParams(
            dimension_semantics=("arbitrary", "arbitrary"),
            vmem_limit_bytes=VMEM_LIMIT),
        name="attn",
    )(q, kt, v, by_pair(ct), by_pair(k_norm2), by_pair(c_next))


def _gelu_tanh(x):
    return 0.5 * x * (1.0 + jnp.tanh(math.sqrt(2.0 / math.pi) * (x + 0.044715 * (x * x * x))))


def _cis(mag_arg, ang):
    mag = jnp.exp(mag_arg)
    return mag * jnp.cos(ang), mag * jnp.sin(ang)


def _cmul(ar, ai, br, bi):
    return ar * br - ai * bi, ar * bi + ai * br


def _cpow2(zr, zi, n):
    assert n & (n - 1) == 0
    while n > 1:
        zr, zi = zr * zr - zi * zi, 2.0 * zr * zi
        n //= 2
    return zr, zi


def _ssm_kernel(arow_ref, acol_ref, ldt_ref, bt_ref, cab_ref, dcol_ref, e_ref, f_ref, z_ref,
                *, chunks_per_seq):
    H = SSM_GROUP_CH

    @pl.when(pl.program_id(0) == 0)
    def _():
        z_ref[:, 0:SSM_CHUNK * H, :] = jnp.zeros((z_ref.shape[0], SSM_CHUNK * H, LANES), F32)

    for gi in range(SSM_GROUPS_PER_STEP):
        y = _ssm_group(arow_ref.at[gi], acol_ref.at[gi], ldt_ref.at[gi], bt_ref.at[gi], cab_ref.at[gi],
                       dcol_ref.at[gi], e_ref[:, gi * H:(gi + 1) * H, :], z_ref.at[gi], chunks_per_seq)
        f_ref[:, gi * H:(gi + 1) * H, :] = y.reshape(SSM_CHUNK, H, y.shape[-1])


def _ssm_group(arow_ref, acol_ref, ldt_ref, bt_ref, cab_ref, dcol_ref, e, z_ref, chunks_per_seq):
    T, P, H = SSM_CHUNK, SSM_STATE, SSM_GROUP_CH
    TH = T * H
    hi = lax.Precision.HIGHEST
    dt = jnp.exp(ldt_ref[...])

    lam_r, lam_i = dt * arow_ref[0:1, :], dt * arow_ref[1:2, :]
    j0 = lax.broadcasted_iota(jnp.int32, (T, 2 * P), 0).astype(F32)
    pa0, pb0 = _cis(j0 * lam_r, j0 * lam_i)
    pa1, pb1 = _cmul(pa0, pb0, *_cis(lam_r, lam_i))
    over_h = lambda a: jnp.concatenate(
        [jnp.broadcast_to(a[j:j + 1, :], (H, 2 * P)) for j in range(T)], axis=0)
    ca, cb = jnp.tile(cab_ref[0], (T, 1)), jnp.tile(cab_ref[1], (T, 1))
    c_pow0 = over_h(pa0) * ca + over_h(pb0) * cb
    c_pow1 = over_h(pa1) * ca + over_h(pb1) * cb

    a_r, a_i = acol_ref[:, 0:1], acol_ref[:, 1:2]
    lr, li = dt * a_r, dt * a_i
    abar_r, abar_i = _cis(lr, li)
    nr, ni = abar_r - 1.0, abar_i
    den = a_r * a_r + a_i * a_i
    fr, fi = (nr * a_r + ni * a_i) / den, (ni * a_r - nr * a_i) / den
    b_r, b_i = bt_ref[0], bt_ref[1]
    bb_r, bb_i = fr * b_r - fi * b_i, fr * b_i + fi * b_r

    kcol = jnp.dot(c_pow0, jnp.concatenate([bb_r, bb_i], axis=0), precision=hi,
                   preferred_element_type=F32)
    lane_h = lax.broadcasted_iota(jnp.int32, (H, LANES), 1) % H
    skip = jnp.where(lane_h == lax.broadcasted_iota(jnp.int32, (H, LANES), 0), dcol_ref[...], 0.0)

    z_ref[TH:2 * TH, :] = kcol
    z_ref[TH:TH + H, :] = kcol[:H] + skip
    lane_group = lax.broadcasted_iota(jnp.int32, (1, LANES), 1) // H
    groups_per_block = LANES // H
    blocks = []
    for v in range(TH // LANES):
        blk = None
        for u in range(groups_per_block):
            s = v * groups_per_block + u
            piece = z_ref[TH - H * s:2 * TH - H * s, :]
            blk = piece if blk is None else jnp.where(lane_group == u, piece, blk)
        blocks.append(blk.astype(BF16))
    mt = jnp.concatenate(blocks, axis=1)

    expo = (groups_per_block - 1 - lane_group).astype(F32)
    wr, wi = _cis(lr * expo, li * expo)
    hop_r, hop_i = _cpow2(abar_r, abar_i, groups_per_block)
    w1_r, w1_i = [], []
    for v in range(TH // LANES):
        w1_r.insert(0, wr * bb_r - wi * bb_i)
        w1_i.insert(0, wr * bb_i + wi * bb_r)
        wr, wi = _cmul(wr, wi, hop_r, hop_i)
    w1t = jnp.concatenate([jnp.concatenate(w1_r, axis=1),
                           jnp.concatenate(w1_i, axis=1)], axis=0).astype(BF16)

    e = e.reshape(TH, e.shape[-1])
    y = jnp.dot(mt, e, preferred_element_type=F32)
    st = jnp.dot(w1t, e, preferred_element_type=F32)
    sr, si = st[:P], st[P:]
    pos = lax.broadcasted_iota(jnp.int32, sr.shape, 1) % chunks_per_seq

    def shifted(a, shift):
        return jnp.where(pos >= shift, pltpu.roll(a, shift, 1), 0.0)

    qr, qi = _cpow2(abar_r, abar_i, T)
    shift = 1
    while shift < chunks_per_seq:
        srs, sis = shifted(sr, shift), shifted(si, shift)
        sr, si = sr + qr * srs - qi * sis, si + qr * sis + qi * srs
        qr, qi = qr * qr - qi * qi, 2.0 * qr * qi
        shift *= 2
    x_prev = jnp.concatenate([shifted(sr, 1), shifted(si, 1)], axis=0).astype(BF16)
    y = y + jnp.dot(c_pow1.astype(BF16), x_prev, preferred_element_type=F32)
    return _gelu_tanh(y)


def _ssm_call(arow, acol, ldt, bt, cab, dcol, e, chunks_per_seq):
    T, _, NC = e.shape
    G = arow.shape[0]
    gps = SSM_GROUPS_PER_STEP
    assert G % gps == 0
    grp = lambda a: pl.BlockSpec((gps,) + a.shape[1:], lambda g: (g,) + (0,) * (a.ndim - 1))
    channels = pl.BlockSpec((T, gps * SSM_GROUP_CH, NC), lambda g: (0, g, 0))
    return pl.pallas_call(
        functools.partial(_ssm_kernel, chunks_per_seq=chunks_per_seq),
        grid=(G // gps,),
        in_specs=[grp(a) for a in (arow, acol, ldt, bt, cab, dcol)] + [channels],
        out_specs=channels,
        out_shape=jax.ShapeDtypeStruct((T, SSM_WIDTH, NC), F32),
        scratch_shapes=[pltpu.VMEM((gps, 2 * T * SSM_GROUP_CH, LANES), F32)],
        compiler_params=pltpu.CompilerParams(
            dimension_semantics=("arbitrary",), vmem_limit_bytes=VMEM_LIMIT),
        name="ssm",
    )(arow, acol, ldt, bt, cab, dcol, e)


def _ssm_param_layouts(a_re, a_im, log_dt, b_re, b_im, c_re, c_im, d_skip):
    G = a_re.shape[0]
    arow = jnp.stack([jnp.concatenate([a_re, a_re], -1), jnp.concatenate([a_im, a_im], -1)], 1)
    acol = jnp.stack([a_re, a_im], -1)
    reps = LANES // SSM_GROUP_CH
    bt = jnp.stack([jnp.tile(b_re, (1, 1, reps)), jnp.tile(b_im, (1, 1, reps))], 1)
    cab = jnp.stack([jnp.concatenate([c_re, -c_im], -1), jnp.concatenate([-c_im, -c_re], -1)], 1)
    return (arow.astype(F32), acol.astype(F32), log_dt.reshape(G, 1, 1).astype(F32),
            bt.astype(F32), cab.astype(F32), d_skip.reshape(G, SSM_GROUP_CH, 1).astype(F32))


def _tail_kernel(h1_ref, attn_ref, y_ref, p_ref, wglu_ref, bglu_ref, ga_ref, gs_ref,
                 wo_ref, g2_ref, w1_ref, w3_ref, w2_ref, gp_ref, wpg_ref,
                 wpp_ref, gf_ref, o_ref):
    y = y_ref[...]
    glu = y * _sigmoid(jnp.dot(y.astype(BF16), wglu_ref[...], preferred_element_type=F32)
                       + bglu_ref[...])
    an = _rms(attn_ref[...], ga_ref[...]).astype(BF16)
    sn = _rms(glu, gs_ref[...]).astype(BF16)
    h = (h1_ref[...] + jnp.dot(an, wo_ref[:ATTN_WIDTH, :], preferred_element_type=F32)
         + jnp.dot(sn, wo_ref[ATTN_WIDTH:, :], preferred_element_type=F32))
    h = h + 0.5 * _swiglu(_rms(h, g2_ref[...]).astype(BF16), w1_ref, w3_ref, w2_ref)
    gate = _sigmoid(jnp.dot(_rms(h, gp_ref[...]).astype(BF16), wpg_ref[...],
                            preferred_element_type=F32))
    h = h + gate * jnp.dot(p_ref[...].astype(BF16), wpp_ref[...], preferred_element_type=F32)
    o_ref[...] = _rms(h, gf_ref[...])


def _tail_call(h1, attn, y, p, *consts):
    B, L, D = h1.shape
    tm = TOKEN_TILE
    tile = lambda w: pl.BlockSpec((None, tm, w), lambda b, i: (b, i, 0))
    return pl.pallas_call(
        _tail_kernel,
        grid=(B, L // tm),
        in_specs=[tile(D), tile(ATTN_WIDTH), tile(SSM_WIDTH), tile(PLE_DIM)]
                 + [_const_spec(c.shape) for c in consts],
        out_specs=tile(D),
        out_shape=jax.ShapeDtypeStruct((B, L, D), F32),
        compiler_params=pltpu.CompilerParams(
            dimension_semantics=("arbitrary", "arbitrary"),
            vmem_limit_bytes=VMEM_LIMIT),
        name="tail",
    )(h1, attn, y, p, *consts)


def kernel(x, p, g_ffn1, w1_a, w3_a, w2_a, g_mix, w_in, b_f, a_re, a_im, log_dt, b_re, b_im, c_re, c_im, d_skip, w_glu, b_glu, g_attn_out, g_ssm_out, w_out, g_ffn2, w1_b, w3_b, w2_b, g_ple, w_ple_gate, w_ple_proj, g_final):
    B, L, D = x.shape
    assert D == D_MODEL and L % ATTN_TILE == 0 and L % TOKEN_TILE == 0 and L % SSM_CHUNK == 0
    assert g_ffn1.shape[0] == 1, "single layer"
    assert TOKEN_TILE == ATTN_TILE, "decay bias rows are relative to the kv tile start"
    row = lambda g: g.reshape(1, -1).astype(F32)
    bf = lambda w: w.astype(BF16)
    w_in0 = w_in[0]
    assert w_in0.shape[1] == 3 * ATTN_WIDTH + ATTN_HEADS + SSM_WIDTH

    (h1, q, kt, v, s_in, ct, k_norm2, c_next), (w1_b16, w3_b16, w2_b16, w_out16, w_gate16) = _head_call(
        x, row(g_ffn1[0]), bf(w1_a[0]), bf(w3_a[0]), bf(w2_a[0]), row(g_mix[0]),
        bf(w_in0), bf(w_in0[:, 3 * ATTN_WIDTH + ATTN_HEADS:]),
        b_f[0].reshape(ATTN_HEADS, 1).astype(F32),
        later=tuple(w.astype(F32) for w in (w1_b[0], w3_b[0], w2_b[0], w_out[0], w_ple_gate[0])))

    T = SSM_CHUNK
    chunks_per_seq = L // T
    n_chunks = B * chunks_per_seq
    e = s_in.reshape(n_chunks, T, SSM_WIDTH).transpose(1, 2, 0)
    f = _ssm_call(*_ssm_param_layouts(a_re[0], a_im[0], log_dt[0], b_re[0], b_im[0],
                                      c_re[0], c_im[0], d_skip[0]),
                  e, chunks_per_seq)
    y = f.transpose(2, 0, 1).reshape(B, L, SSM_WIDTH)

    attn = _attn_call(q, kt, v, ct, k_norm2, c_next)

    return _tail_call(
        h1, attn, y, p[0],
        bf(w_glu[0]), row(b_glu[0]), row(g_attn_out[0]), row(g_ssm_out[0]),
        w_out16, row(g_ffn2[0]),
        w1_b16, w3_b16, w2_b16, row(g_ple[0]), w_gate16,
        bf(w_ple_proj[0]), row(g_final))
```

```python
import functools
import math

import jax
import jax.numpy as jnp
from jax import lax
from jax.experimental import pallas as pl
from jax.experimental.pallas import tpu as pltpu

D_MODEL = 1024
ATTN_HEADS = 8
HEAD_DIM = 64
ATTN_WIDTH = ATTN_HEADS * HEAD_DIM
SSM_WIDTH = D_MODEL - ATTN_WIDTH
SSM_GROUP_CH = 16
SSM_GROUPS = SSM_WIDTH // SSM_GROUP_CH
SSM_STATE = 64
D_FF = 2816
PLE_DIM = 256
EPS = 1e-6

LANES = 128
BF16_SUBLANES = 16
HEAD_PAIR = 2 * HEAD_DIM
N_PAIRS = ATTN_HEADS // 2
FF_CHUNK = 256
TOKEN_TILE = 512
ATTN_TILE = 512
SSM_CHUNK = 32
SSM_GROUPS_PER_STEP = 2
NEG_BIG = -1e30
SKIP_LOG2 = 140.0
BOUND_SLACK_MUL = 1.001
BOUND_SLACK_ADD = 1.0
LOG2E = math.log2(math.e)
N_BIAS = 3
BIAS_ROWS = 8
VMEM_LIMIT = 56 * 1024 * 1024

BF16 = jnp.bfloat16
F32 = jnp.float32


def _rms(x, g):
    ms = jnp.mean(x * x, axis=-1, keepdims=True)
    return x * lax.rsqrt(ms + EPS) * g


def _sigmoid(x):
    return 1.0 / (1.0 + jnp.exp(-x))


def _swiglu(xn, w1_ref, w3_ref, w2_ref):
    acc = None
    for c in range(D_FF // FF_CHUNK):
        sl = slice(c * FF_CHUNK, (c + 1) * FF_CHUNK)
        a = jnp.dot(xn, w1_ref[:, sl], preferred_element_type=F32)
        b = jnp.dot(xn, w3_ref[:, sl], preferred_element_type=F32)
        gated = (a * _sigmoid(a) * b).astype(BF16)
        part = jnp.dot(gated, w2_ref[sl, :], preferred_element_type=F32)
        acc = part if acc is None else acc + part
    return acc


def _const_spec(shape):
    nd = len(shape)
    return pl.BlockSpec(shape, lambda *_: (0,) * nd, pipeline_mode=pl.Buffered(1))


def _head_kernel(*refs, n_later):
    (x_ref, g1_ref, w1_ref, w3_ref, w2_ref, gm_ref, win_ref, ws_ref, bf_ref) = refs[:9]
    later_in = refs[9:9 + n_later]
    h1_ref, q_ref, kt_ref, v_ref, s_ref, ct_ref, kn_ref, cn_ref = refs[9 + n_later:17 + n_later]
    later_out = refs[17 + n_later:17 + 2 * n_later]
    carry_ref = refs[-1]
    @pl.when(pl.program_id(1) == 0)
    def _():
        carry_ref[...] = jnp.zeros_like(carry_ref)
        kn_ref[...] = jnp.zeros_like(kn_ref)
        cn_ref[...] = jnp.zeros_like(cn_ref)

    for src, dst in zip(later_in, later_out):
        dst[...] = src[...].astype(BF16)
    tm = x_ref.shape[0]
    x = x_ref[...]
    h1 = x + 0.5 * _swiglu(_rms(x, g1_ref[...]).astype(BF16), w1_ref, w3_ref, w2_ref)
    h1_ref[...] = h1
    un = _rms(h1, gm_ref[...]).astype(BF16)
    project = lambda w: jnp.dot(un, w, preferred_element_type=F32)
    zf = project(win_ref[:, 3 * ATTN_WIDTH:3 * ATTN_WIDTH + LANES])
    kv = project(win_ref[:, ATTN_WIDTH:3 * ATTN_WIDTH])
    zft = zf.T[:ATTN_HEADS, :] + bf_ref[...]
    logf = jnp.minimum(zft, 0.0) - jnp.log1p(jnp.exp(-jnp.abs(zft)))
    lane = lax.broadcasted_iota(jnp.int32, logf.shape, 1)
    c = logf
    shift = 1
    while shift < tm:
        c = c + jnp.where(lane >= shift, pltpu.roll(c, shift, 1), 0.0)
        shift *= 2

    c_abs = c + carry_ref[:, 0:1]
    ct_ref[...] = c_abs * LOG2E
    carry_ref[...] = jnp.broadcast_to(c_abs[:, tm - 1:tm], carry_ref.shape)

    rel = (c - c[:, 0:1]) * LOG2E
    hi = rel.astype(BF16).astype(F32)
    mid = (rel - hi).astype(BF16).astype(F32)
    lo = (rel - hi - mid).astype(BF16).astype(F32)
    kt = kv[:, :ATTN_WIDTH].astype(BF16).astype(F32).T
    vv = kv[:, ATTN_WIDTH:]
    k_sq = (kt * kt).reshape(ATTN_HEADS, HEAD_DIM, tm)
    tile = pl.program_id(1)
    stat_lane = lax.broadcasted_iota(jnp.int32, kn_ref.shape, 1)

    kn_ref[...] = jnp.where(stat_lane == tile,
                            jnp.max(jnp.sum(k_sq, axis=1), axis=-1, keepdims=True), kn_ref[...])
    cn_ref[...] = jnp.where(stat_lane == tile - 1, c_abs[:, 0:1] * LOG2E, cn_ref[...])
    sub = lax.broadcasted_iota(jnp.int32, (BIAS_ROWS, tm), 0)
    zeros = jnp.zeros((HEAD_DIM - BIAS_ROWS, tm), F32)
    vlane = lax.broadcasted_iota(jnp.int32, (tm, HEAD_PAIR), 1)
    for h in range(ATTN_HEADS):
        bias = jnp.where(sub == 0, -hi[h:h + 1],
                         jnp.where(sub == 1, -mid[h:h + 1],
                                   jnp.where(sub == 2, -lo[h:h + 1], 0.0)))
        k_h = kt[h * HEAD_DIM:(h + 1) * HEAD_DIM]
        vp = vv[:, (h // 2) * HEAD_PAIR:(h // 2 + 1) * HEAD_PAIR]
        if h % 2 == 0:
            kt_ref[h] = jnp.concatenate([k_h, bias, zeros], axis=0).astype(BF16)
            v_ref[h] = jnp.where(vlane < HEAD_DIM, vp,
                                 jnp.where(vlane == HEAD_DIM, 1.0, 0.0)).astype(BF16)
        else:
            kt_ref[h] = jnp.concatenate([bias, zeros, k_h], axis=0).astype(BF16)
            v_ref[h] = jnp.where(vlane >= HEAD_DIM, vp,
                                 jnp.where(vlane == 0, 1.0, 0.0)).astype(BF16)

    q_ref[...] = (project(win_ref[:, :ATTN_WIDTH]) * (LOG2E / math.sqrt(HEAD_DIM))).astype(BF16)
    s_ref[...] = project(ws_ref[...]).astype(BF16)


def _slab_spec(w, n_steps, steps_per_batch):
    rows = next(r for r in range(BF16_SUBLANES, w.shape[0] + 1, BF16_SUBLANES)
                if w.shape[0] % r == 0 and w.shape[0] // r <= n_steps)
    last = w.shape[0] // rows - 1
    return pl.BlockSpec((rows, w.shape[1]),
                        lambda b, i: (jnp.minimum(b * steps_per_batch + i, last), 0))


def _head_call(x, g1, w1, w3, w2, gm, w_in, ws, bf, later):
    B, L, D = x.shape
    tm = TOKEN_TILE
    tile = lambda w: pl.BlockSpec((None, tm, w), lambda b, i: (b, i, 0))
    slabs = [_slab_spec(w, B * (L // tm), L // tm) for w in later]
    out_shape = (
        jax.ShapeDtypeStruct((B, L, D), F32),
        jax.ShapeDtypeStruct((B, L, ATTN_WIDTH), BF16),
        jax.ShapeDtypeStruct((B, ATTN_HEADS, HEAD_PAIR, L), BF16),
        jax.ShapeDtypeStruct((B, ATTN_HEADS, L, HEAD_PAIR), BF16),
        jax.ShapeDtypeStruct((B, L, SSM_WIDTH), BF16),
        jax.ShapeDtypeStruct((B, ATTN_HEADS, L), F32),
        jax.ShapeDtypeStruct((B, ATTN_HEADS, LANES), F32),
        jax.ShapeDtypeStruct((B, ATTN_HEADS, LANES), F32),
    ) + tuple(jax.ShapeDtypeStruct(w.shape, BF16) for w in later)
    outs = pl.pallas_call(
        functools.partial(_head_kernel, n_later=len(later)),
        grid=(B, L // tm),
        in_specs=[tile(D), _const_spec(g1.shape), _const_spec(w1.shape),
                  _const_spec(w3.shape), _const_spec(w2.shape), _const_spec(gm.shape),
                  _const_spec(w_in.shape), _const_spec(ws.shape), _const_spec(bf.shape)] + slabs,
        out_specs=(tile(D), tile(ATTN_WIDTH),
                   pl.BlockSpec((None, ATTN_HEADS, HEAD_PAIR, tm), lambda b, i: (b, 0, 0, i)),
                   pl.BlockSpec((None, ATTN_HEADS, tm, HEAD_PAIR), lambda b, i: (b, 0, i, 0)),
                   tile(SSM_WIDTH),
                   pl.BlockSpec((None, ATTN_HEADS, tm), lambda b, i: (b, 0, i)),
                   pl.BlockSpec((None, ATTN_HEADS, LANES), lambda b, i: (b, 0, 0)),
                   pl.BlockSpec((None, ATTN_HEADS, LANES), lambda b, i: (b, 0, 0)))
                  + tuple(slabs),
        out_shape=out_shape,
        scratch_shapes=[pltpu.VMEM((ATTN_HEADS, LANES), F32)],
        compiler_params=pltpu.CompilerParams(
            dimension_semantics=("arbitrary", "arbitrary"),
            vmem_limit_bytes=VMEM_LIMIT),
        name="head",
    )(x, g1, w1, w3, w2, gm, w_in, ws, bf, *later)
    return outs[:8], outs[8:]


def _attn_kernel(q_ref, *refs):
    _attn_prepare(0, q_ref, *refs)
    _attn_first_stage(0, q_ref, *refs)

    def q_tile(n, carry):
        _attn_q_tile(n, q_ref, *refs)
        return carry

    lax.fori_loop(0, q_ref.shape[0] // ATTN_TILE, q_tile, 0)


def _tile_start(i):
    return pl.multiple_of(i * ATTN_TILE, ATTN_TILE)


def _q_heads(q):
    lane = lax.broadcasted_iota(jnp.int32, (1, HEAD_PAIR), 1)
    first = lane < HEAD_DIM
    ones_even = jnp.where((lane >= HEAD_DIM) & (lane < HEAD_DIM + N_BIAS), 1.0, 0.0).astype(BF16)
    ones_odd = jnp.where(lane < N_BIAS, 1.0, 0.0).astype(BF16)
    return first, (jnp.where(first, q, ones_even), jnp.where(first, ones_odd, q))


def _attn_prepare(n, q_ref, kt_ref, v_ref, c_ref, kn_ref, cn_ref, o_ref, s_e0, s_e1, s_o0, s_o1, m_ref,
                  acc_ref, count_ref):
    t = ATTN_TILE
    q = q_ref[pl.ds(_tile_start(n), t), :]
    first, q_heads = _q_heads(q)
    q_sq = q.astype(F32) * q.astype(F32)
    row = lax.broadcasted_iota(jnp.int32, (t, t), 0)
    col = lax.broadcasted_iota(jnp.int32, (t, t), 1)
    tile_id = lax.broadcasted_iota(jnp.int32, (1, LANES), 1)
    needed = tile_id < 0
    for h, s_ref in enumerate((s_o0, s_o1)):
        s = jnp.where(col <= row, jnp.dot(q_heads[h], kt_ref[h, :, pl.ds(_tile_start(n), t)],
                                          preferred_element_type=F32), NEG_BIG)
        s_ref[...] = s
        m = jnp.max(s, axis=-1, keepdims=True)
        m_ref[h] = m
        q_norm2 = jnp.max(jnp.sum(jnp.where(first == (h == 0), q_sq, 0.0), axis=-1, keepdims=True),
                          axis=0, keepdims=True)
        k_norm2, c_next = kn_ref[h:h + 1, :], cn_ref[h:h + 1, :]
        c_q = c_ref[h:h + 1, pl.ds(_tile_start(n), LANES)][:, 0:1]
        reach = jnp.sqrt(q_norm2 * k_norm2) * BOUND_SLACK_MUL + BOUND_SLACK_ADD + c_q - c_next
        needed = needed | (reach - jnp.min(m, axis=0, keepdims=True) >= -SKIP_LOG2)
    first_needed = jnp.min(jnp.where(needed & (tile_id < n), tile_id, n).astype(F32))
    count_ref[0] = n - first_needed.astype(jnp.int32)


def _attn_first_stage(n, q_ref, kt_ref, v_ref, c_ref, kn_ref, cn_ref, o_ref, s_e0, s_e1, s_o0, s_o1,
                      m_ref, acc_ref, count_ref):
    t = ATTN_TILE
    _, q_heads = _q_heads(q_ref[pl.ds(_tile_start(n), t), :])
    older = _tile_start(jnp.maximum(n - 1, 0))
    for h, s_ref in enumerate((s_e0, s_e1)):
        s_ref[...] = jnp.dot(q_heads[h], kt_ref[h, :, pl.ds(older, t)], preferred_element_type=F32)
    for h, s_ref in enumerate((s_o0, s_o1)):
        acc_ref[h] = jnp.dot(jnp.exp2(s_ref[...] - m_ref[h]).astype(BF16),
                             v_ref[h, pl.ds(_tile_start(n), t), :], preferred_element_type=F32)


def _attn_q_tile(n, q_ref, kt_ref, v_ref, c_ref, kn_ref, cn_ref, o_ref, s_e0, s_e1, s_o0, s_o1, m_ref,
                 acc_ref, count_ref):
    t = ATTN_TILE
    count = count_ref[0]
    start = _tile_start
    first, q_heads = _q_heads(q_ref[pl.ds(start(n), t), :])
    s_buf = ((s_e0, s_e1), (s_o0, s_o1))
    c_q = [c_ref[h:h + 1, pl.ds(start(n), LANES)][:, 0:1] for h in range(2)]

    def qk(h, kv):
        return jnp.dot(q_heads[h], kt_ref[h, :, pl.ds(start(kv), t)], preferred_element_type=F32)

    def consume(h, s, kv):
        d = c_ref[h:h + 1, pl.ds(start(kv), LANES)][:, 0:1] - c_q[h]
        m_old = m_ref[h]
        m_new = jnp.maximum(m_old, jnp.max(s, axis=-1, keepdims=True) - d)
        p = jnp.exp2(s - (m_new + d)).astype(BF16)
        m_ref[h] = m_new
        acc_ref[h] = jnp.exp2(m_old - m_new) * acc_ref[h] + jnp.dot(
            p, v_ref[h, pl.ds(start(kv), t), :], preferred_element_type=F32)

    def step(kv, par):
        for h in range(2):
            s_buf[1 - par][h][...] = qk(h, kv - 1)
        for h in range(2):
            consume(h, s_buf[par][h][...], kv)

    def pair(i, carry):
        step(n - 1 - 2 * i, 0)
        step(n - 2 - 2 * i, 1)
        return carry

    n_pairs = jnp.maximum(count - 1, 0) // 2
    lax.fori_loop(0, n_pairs, pair, 0)
    left = count - 2 * n_pairs
    last = n - count

    def finish():
        acc0, acc1 = acc_ref[0], acc_ref[1]
        o_ref[pl.ds(start(n), t), :] = jnp.where(first, acc0 / acc0[:, HEAD_DIM:HEAD_DIM + 1],
                                                 acc1 / acc1[:, 0:1])
        nxt = jnp.minimum(n + 1, q_ref.shape[0] // t - 1)
        rest = (q_ref, kt_ref, v_ref, c_ref, kn_ref, cn_ref, o_ref, s_e0, s_e1, s_o0, s_o1, m_ref,
                acc_ref, count_ref)
        _attn_prepare(nxt, *rest)
        _attn_first_stage(nxt, *rest)

    @pl.when(left == 2)
    def _():
        step(last + 1, 0)
        for h in range(2):
            consume(h, s_buf[1][h][...], last)
        finish()

    @pl.when(left == 1)
    def _():
        for h in range(2):
            consume(h, s_buf[0][h][...], last)
        finish()

    @pl.when(left == 0)
    def _():
        finish()


def _attn_call(q, kt, v, ct, k_norm2, c_next):
    B, L, _ = q.shape
    t = ATTN_TILE
    assert L // t <= LANES
    by_pair = lambda a: a.reshape(B, N_PAIRS, 2, a.shape[-1])
    pair_block = lambda *shape: pl.BlockSpec((None, 2) + shape, lambda b, p: (b, p, 0, 0))
    pair_rows = lambda w: pl.BlockSpec((None, None, 2, w), lambda b, p: (b, p, 0, 0))
    lanes_of_pair = pl.BlockSpec((None, L, HEAD_PAIR), lambda b, p: (b, 0, p))
    return pl.pallas_call(
        _attn_kernel,
        grid=(B, N_PAIRS),
        in_specs=[lanes_of_pair, pair_block(HEAD_PAIR, L), pair_block(L, HEAD_PAIR),
                  pair_rows(L), pair_rows(LANES), pair_rows(LANES)],
        out_specs=lanes_of_pair,
        out_shape=jax.ShapeDtypeStruct((B, L, ATTN_WIDTH), F32),
        scratch_shapes=[pltpu.VMEM((t, t), F32)] * 4
                       + [pltpu.VMEM((2, t, 1), F32), pltpu.VMEM((2, t, HEAD_PAIR), F32),
                          pltpu.SMEM((1,), jnp.int32)],
        compiler_params=pltpu.CompilerParams(
            dimension_semantics=("arbitrary", "arbitrary"),
            vmem_limit_bytes=VMEM_LIMIT),
        name="attn",
    )(q, kt, v, by_pair(ct), by_pair(k_norm2), by_pair(c_next))


def _gelu_tanh(x):
    return 0.5 * x * (1.0 + jnp.tanh(math.sqrt(2.0 / math.pi) * (x + 0.044715 * (x * x * x))))


def _cis(mag_arg, ang):
    mag = jnp.exp(mag_arg)
    return mag * jnp.cos(ang), mag * jnp.sin(ang)


def _cmul(ar, ai, br, bi):
    return ar * br - ai * bi, ar * bi + ai * br


def _cpow2(zr, zi, n):
    assert n & (n - 1) == 0
    while n > 1:
        zr, zi = zr * zr - zi * zi, 2.0 * zr * zi
        n //= 2
    return zr, zi


def _ssm_kernel(arow_ref, acol_ref, ldt_ref, bt_ref, cab_ref, dcol_ref, e_ref, f_ref, z_ref,
                *, chunks_per_seq):
    H = SSM_GROUP_CH

    @pl.when(pl.program_id(0) == 0)
    def _():
        z_ref[:, 0:SSM_CHUNK * H, :] = jnp.zeros((z_ref.shape[0], SSM_CHUNK * H, LANES), F32)

    for gi in range(SSM_GROUPS_PER_STEP):
        y = _ssm_group(arow_ref.at[gi], acol_ref.at[gi], ldt_ref.at[gi], bt_ref.at[gi], cab_ref.at[gi],
                       dcol_ref.at[gi], e_ref[:, gi * H:(gi + 1) * H, :], z_ref.at[gi], chunks_per_seq)
        f_ref[:, gi * H:(gi + 1) * H, :] = y.reshape(SSM_CHUNK, H, y.shape[-1])


def _ssm_group(arow_ref, acol_ref, ldt_ref, bt_ref, cab_ref, dcol_ref, e, z_ref, chunks_per_seq):
    T, P, H = SSM_CHUNK, SSM_STATE, SSM_GROUP_CH
    TH = T * H
    hi = lax.Precision.HIGHEST
    dt = jnp.exp(ldt_ref[...])

    lam_r, lam_i = dt * arow_ref[0:1, :], dt * arow_ref[1:2, :]
    j0 = lax.broadcasted_iota(jnp.int32, (T, 2 * P), 0).astype(F32)
    pa0, pb0 = _cis(j0 * lam_r, j0 * lam_i)
    pa1, pb1 = _cmul(pa0, pb0, *_cis(lam_r, lam_i))
    over_h = lambda a: jnp.concatenate(
        [jnp.broadcast_to(a[j:j + 1, :], (H, 2 * P)) for j in range(T)], axis=0)
    ca, cb = jnp.tile(cab_ref[0], (T, 1)), jnp.tile(cab_ref[1], (T, 1))
    c_pow0 = over_h(pa0) * ca + over_h(pb0) * cb
    c_pow1 = over_h(pa1) * ca + over_h(pb1) * cb

    a_r, a_i = acol_ref[:, 0:1], acol_ref[:, 1:2]
    lr, li = dt * a_r, dt * a_i
    abar_r, abar_i = _cis(lr, li)
    nr, ni = abar_r - 1.0, abar_i
    den = a_r * a_r + a_i * a_i
    fr, fi = (nr * a_r + ni * a_i) / den, (ni * a_r - nr * a_i) / den
    b_r, b_i = bt_ref[0], bt_ref[1]
    bb_r, bb_i = fr * b_r - fi * b_i, fr * b_i + fi * b_r

    kcol = jnp.dot(c_pow0, jnp.concatenate([bb_r, bb_i], axis=0), precision=hi,
                   preferred_element_type=F32)
    lane_h = lax.broadcasted_iota(jnp.int32, (H, LANES), 1) % H
    skip = jnp.where(lane_h == lax.broadcasted_iota(jnp.int32, (H, LANES), 0), dcol_ref[...], 0.0)

    z_ref[TH:2 * TH, :] = kcol
    z_ref[TH:TH + H, :] = kcol[:H] + skip
    lane_group = lax.broadcasted_iota(jnp.int32, (1, LANES), 1) // H
    groups_per_block = LANES // H
    blocks = []
    for v in range(TH // LANES):
        blk = None
        for u in range(groups_per_block):
            s = v * groups_per_block + u
            piece = z_ref[TH - H * s:2 * TH - H * s, :]
            blk = piece if blk is None else jnp.where(lane_group == u, piece, blk)
        blocks.append(blk.astype(BF16))
    mt = jnp.concatenate(blocks, axis=1)

    expo = (groups_per_block - 1 - lane_group).astype(F32)
    wr, wi = _cis(lr * expo, li * expo)
    hop_r, hop_i = _cpow2(abar_r, abar_i, groups_per_block)
    w1_r, w1_i = [], []
    for v in range(TH // LANES):
        w1_r.insert(0, wr * bb_r - wi * bb_i)
        w1_i.insert(0, wr * bb_i + wi * bb_r)
        wr, wi = _cmul(wr, wi, hop_r, hop_i)
    w1t = jnp.concatenate([jnp.concatenate(w1_r, axis=1),
                           jnp.concatenate(w1_i, axis=1)], axis=0).astype(BF16)

    e = e.reshape(TH, e.shape[-1])
    y = jnp.dot(mt, e, preferred_element_type=F32)
    st = jnp.dot(w1t, e, preferred_element_type=F32)
    sr, si = st[:P], st[P:]
    pos = lax.broadcasted_iota(jnp.int32, sr.shape, 1) % chunks_per_seq

    def shifted(a, shift):
        return jnp.where(pos >= shift, pltpu.roll(a, shift, 1), 0.0)

    qr, qi = _cpow2(abar_r, abar_i, T)
    shift = 1
    while shift < chunks_per_seq:
        srs, sis = shifted(sr, shift), shifted(si, shift)
        sr, si = sr + qr * srs - qi * sis, si + qr * sis + qi * srs
        qr, qi = qr * qr - qi * qi, 2.0 * qr * qi
        shift *= 2
    x_prev = jnp.concatenate([shifted(sr, 1), shifted(si, 1)], axis=0).astype(BF16)
    y = y + jnp.dot(c_pow1.astype(BF16), x_prev, preferred_element_type=F32)
    return _gelu_tanh(y)


def _ssm_call(arow, acol, ldt, bt, cab, dcol, e, chunks_per_seq):
    T, _, NC = e.shape
    G = arow.shape[0]
    gps = SSM_GROUPS_PER_STEP
    assert G % gps == 0
    grp = lambda a: pl.BlockSpec((gps,) + a.shape[1:], lambda g: (g,) + (0,) * (a.ndim - 1))
    channels = pl.BlockSpec((T, gps * SSM_GROUP_CH, NC), lambda g: (0, g, 0))
    return pl.pallas_call(
        functools.partial(_ssm_kernel, chunks_per_seq=chunks_per_seq),
        grid=(G // gps,),
        in_specs=[grp(a) for a in (arow, acol, ldt, bt, cab, dcol)] + [channels],
        out_specs=channels,
        out_shape=jax.ShapeDtypeStruct((T, SSM_WIDTH, NC), F32),
        scratch_shapes=[pltpu.VMEM((gps, 2 * T * SSM_GROUP_CH, LANES), F32)],
        compiler_params=pltpu.CompilerParams(
            dimension_semantics=("arbitrary",), vmem_limit_bytes=VMEM_LIMIT),
        name="ssm",
    )(arow, acol, ldt, bt, cab, dcol, e)


def _ssm_param_layouts(a_re, a_im, log_dt, b_re, b_im, c_re, c_im, d_skip):
    G = a_re.shape[0]
    arow = jnp.stack([jnp.concatenate([a_re, a_re], -1), jnp.concatenate([a_im, a_im], -1)], 1)
    acol = jnp.stack([a_re, a_im], -1)
    reps = LANES // SSM_GROUP_CH
    bt = jnp.stack([jnp.tile(b_re, (1, 1, reps)), jnp.tile(b_im, (1, 1, reps))], 1)
    cab = jnp.stack([jnp.concatenate([c_re, -c_im], -1), jnp.concatenate([-c_im, -c_re], -1)], 1)
    return (arow.astype(F32), acol.astype(F32), log_dt.reshape(G, 1, 1).astype(F32),
            bt.astype(F32), cab.astype(F32), d_skip.reshape(G, SSM_GROUP_CH, 1).astype(F32))


def _tail_kernel(h1_ref, attn_ref, y_ref, p_ref, wglu_ref, bglu_ref, ga_ref, gs_ref,
                 wo_ref, g2_ref, w1_ref, w3_ref, w2_ref, gp_ref, wpg_ref,
                 wpp_ref, gf_ref, o_ref):
    y = y_ref[...]
    glu = y * _sigmoid(jnp.dot(y.astype(BF16), wglu_ref[...], preferred_element_type=F32)
                       + bglu_ref[...])
    an = _rms(attn_ref[...], ga_ref[...]).astype(BF16)
    sn = _rms(glu, gs_ref[...]).astype(BF16)
    h = (h1_ref[...] + jnp.dot(an, wo_ref[:ATTN_WIDTH, :], preferred_element_type=F32)
         + jnp.dot(sn, wo_ref[ATTN_WIDTH:, :], preferred_element_type=F32))
    h = h + 0.5 * _swiglu(_rms(h, g2_ref[...]).astype(BF16), w1_ref, w3_ref, w2_ref)
    gate = _sigmoid(jnp.dot(_rms(h, gp_ref[...]).astype(BF16), wpg_ref[...],
                            preferred_element_type=F32))
    h = h + gate * jnp.dot(p_ref[...].astype(BF16), wpp_ref[...], preferred_element_type=F32)
    o_ref[...] = _rms(h, gf_ref[...])


def _tail_call(h1, attn, y, p, *consts):
    B, L, D = h1.shape
    tm = TOKEN_TILE
    tile = lambda w: pl.BlockSpec((None, tm, w), lambda b, i: (b, i, 0))
    return pl.pallas_call(
        _tail_kernel,
        grid=(B, L // tm),
        in_specs=[tile(D), tile(ATTN_WIDTH), tile(SSM_WIDTH), tile(PLE_DIM)]
                 + [_const_spec(c.shape) for c in consts],
        out_specs=tile(D),
        out_shape=jax.ShapeDtypeStruct((B, L, D), F32),
        compiler_params=pltpu.CompilerParams(
            dimension_semantics=("arbitrary", "arbitrary"),
            vmem_limit_bytes=VMEM_LIMIT),
        name="tail",
    )(h1, attn, y, p, *consts)


def kernel(x, p, g_ffn1, w1_a, w3_a, w2_a, g_mix, w_in, b_f, a_re, a_im, log_dt, b_re, b_im, c_re, c_im, d_skip, w_glu, b_glu, g_attn_out, g_ssm_out, w_out, g_ffn2, w1_b, w3_b, w2_b, g_ple, w_ple_gate, w_ple_proj, g_final):
    B, L, D = x.shape
    assert D == D_MODEL and L % ATTN_TILE == 0 and L % TOKEN_TILE == 0 and L % SSM_CHUNK == 0
    assert g_ffn1.shape[0] == 1, "single layer"
    assert TOKEN_TILE == ATTN_TILE, "decay bias rows are relative to the kv tile start"
    row = lambda g: g.reshape(1, -1).astype(F32)
    bf = lambda w: w.astype(BF16)
    w_in0 = w_in[0]
    assert w_in0.shape[1] == 3 * ATTN_WIDTH + ATTN_HEADS + SSM_WIDTH

    (h1, q, kt, v, s_in, ct, k_norm2, c_next), (w1_b16, w3_b16, w2_b16, w_out16, w_gate16) = _head_call(
        x, row(g_ffn1[0]), bf(w1_a[0]), bf(w3_a[0]), bf(w2_a[0]), row(g_mix[0]),
        bf(w_in0), bf(w_in0[:, 3 * ATTN_WIDTH + ATTN_HEADS:]),
        b_f[0].reshape(ATTN_HEADS, 1).astype(F32),
        later=tuple(w.astype(F32) for w in (w1_b[0], w3_b[0], w2_b[0], w_out[0], w_ple_gate[0])))

    T = SSM_CHUNK
    chunks_per_seq = L // T
    n_chunks = B * chunks_per_seq
    e = s_in.reshape(n_chunks, T, SSM_WIDTH).transpose(1, 2, 0)
    f = _ssm_call(*_ssm_param_layouts(a_re[0], a_im[0], log_dt[0], b_re[0], b_im[0],
                                      c_re[0], c_im[0], d_skip[0]),
                  e, chunks_per_seq)
    y = f.transpose(2, 0, 1).reshape(B, L, SSM_WIDTH)

    attn = _attn_call(q, kt, v, ct, k_norm2, c_next)

    return _tail_call(
        h1, attn, y, p[0],
        bf(w_glu[0]), row(b_glu[0]), row(g_attn_out[0]), row(g_ssm_out[0]),
        w_out16, row(g_ffn2[0]),
        w1_b16, w3_b16, w2_b16, row(g_ple[0]), w_gate16,
        bf(w_ple_proj[0]), row(g_final))
```

```python
import functools
import math

import jax
import jax.numpy as jnp
from jax import lax
from jax.experimental import pallas as pl
from jax.experimental.pallas import tpu as pltpu

D_MODEL = 1024
ATTN_HEADS = 8
HEAD_DIM = 64
ATTN_WIDTH = ATTN_HEADS * HEAD_DIM
SSM_WIDTH = D_MODEL - ATTN_WIDTH
SSM_GROUP_CH = 16
SSM_GROUPS = SSM_WIDTH // SSM_GROUP_CH
SSM_STATE = 64
D_FF = 2816
PLE_DIM = 256
EPS = 1e-6

LANES = 128
BF16_SUBLANES = 16
HEAD_PAIR = 2 * HEAD_DIM
N_PAIRS = ATTN_HEADS // 2
FF_CHUNK = 256
TOKEN_TILE = 512
ATTN_TILE = 512
SSM_CHUNK = 32
SSM_GROUPS_PER_STEP = 2
NEG_BIG = -1e30
SKIP_LOG2 = 140.0
BOUND_SLACK_MUL = 1.001
BOUND_SLACK_ADD = 1.0
LOG2E = math.log2(math.e)
N_BIAS = 3
BIAS_ROWS = 8
VMEM_LIMIT = 56 * 1024 * 1024

BF16 = jnp.bfloat16
F32 = jnp.float32


def _rms(x, g):
    ms = jnp.mean(x * x, axis=-1, keepdims=True)
    return x * lax.rsqrt(ms + EPS) * g


def _sigmoid(x):
    return 1.0 / (1.0 + jnp.exp(-x))


def _swiglu(xn, w1_ref, w3_ref, w2_ref):
    acc = None
    for c in range(D_FF // FF_CHUNK):
        sl = slice(c * FF_CHUNK, (c + 1) * FF_CHUNK)
        a = jnp.dot(xn, w1_ref[:, sl], preferred_element_type=F32)
        b = jnp.dot(xn, w3_ref[:, sl], preferred_element_type=F32)
        gated = (a * _sigmoid(a) * b).astype(BF16)
        part = jnp.dot(gated, w2_ref[sl, :], preferred_element_type=F32)
        acc = part if acc is None else acc + part
    return acc


def _const_spec(shape):
    nd = len(shape)
    return pl.BlockSpec(shape, lambda *_: (0,) * nd, pipeline_mode=pl.Buffered(1))


def _head_kernel(*refs, n_later):
    (x_ref, g1_ref, w1_ref, w3_ref, w2_ref, gm_ref, win_ref, ws_ref, bf_ref) = refs[:9]
    later_in = refs[9:9 + n_later]
    h1_ref, q_ref, kt_ref, v_ref, s_ref, ct_ref, kn_ref, cn_ref = refs[9 + n_later:17 + n_later]
    later_out = refs[17 + n_later:17 + 2 * n_later]
    carry_ref = refs[-1]
    @pl.when(pl.program_id(1) == 0)
    def _():
        carry_ref[...] = jnp.zeros_like(carry_ref)
        kn_ref[...] = jnp.zeros_like(kn_ref)
        cn_ref[...] = jnp.zeros_like(cn_ref)

    for src, dst in zip(later_in, later_out):
        dst[...] = src[...].astype(BF16)
    tm = x_ref.shape[0]
    x = x_ref[...]
    h1 = x + 0.5 * _swiglu(_rms(x, g1_ref[...]).astype(BF16), w1_ref, w3_ref, w2_ref)
    h1_ref[...] = h1
    un = _rms(h1, gm_ref[...]).astype(BF16)
    project = lambda w: jnp.dot(un, w, preferred_element_type=F32)
    zf = project(win_ref[:, 3 * ATTN_WIDTH:3 * ATTN_WIDTH + LANES])
    kv = project(win_ref[:, ATTN_WIDTH:3 * ATTN_WIDTH])
    zft = zf.T[:ATTN_HEADS, :] + bf_ref[...]
    logf = jnp.minimum(zft, 0.0) - jnp.log1p(jnp.exp(-jnp.abs(zft)))
    lane = lax.broadcasted_iota(jnp.int32, logf.shape, 1)
    c = logf
    shift = 1
    while shift < tm:
        c = c + jnp.where(lane >= shift, pltpu.roll(c, shift, 1), 0.0)
        shift *= 2

    c_abs = c + carry_ref[:, 0:1]
    ct_ref[...] = c_abs * LOG2E
    carry_ref[...] = jnp.broadcast_to(c_abs[:, tm - 1:tm], carry_ref.shape)

    rel = (c - c[:, 0:1]) * LOG2E
    hi = rel.astype(BF16).astype(F32)
    mid = (rel - hi).astype(BF16).astype(F32)
    lo = (rel - hi - mid).astype(BF16).astype(F32)
    kt = kv[:, :ATTN_WIDTH].astype(BF16).astype(F32).T
    vv = kv[:, ATTN_WIDTH:]
    k_sq = (kt * kt).reshape(ATTN_HEADS, HEAD_DIM, tm)
    tile = pl.program_id(1)
    stat_lane = lax.broadcasted_iota(jnp.int32, kn_ref.shape, 1)

    kn_ref[...] = jnp.where(stat_lane == tile,
                            jnp.max(jnp.sum(k_sq, axis=1), axis=-1, keepdims=True), kn_ref[...])
    cn_ref[...] = jnp.where(stat_lane == tile - 1, c_abs[:, 0:1] * LOG2E, cn_ref[...])
    sub = lax.broadcasted_iota(jnp.int32, (BIAS_ROWS, tm), 0)
    zeros = jnp.zeros((HEAD_DIM - BIAS_ROWS, tm), F32)
    vlane = lax.broadcasted_iota(jnp.int32, (tm, HEAD_PAIR), 1)
    for h in range(ATTN_HEADS):
        bias = jnp.where(sub == 0, -hi[h:h + 1],
                         jnp.where(sub == 1, -mid[h:h + 1],
                                   jnp.where(sub == 2, -lo[h:h + 1], 0.0)))
        k_h = kt[h * HEAD_DIM:(h + 1) * HEAD_DIM]
        vp = vv[:, (h // 2) * HEAD_PAIR:(h // 2 + 1) * HEAD_PAIR]
        if h % 2 == 0:
            kt_ref[h] = jnp.concatenate([k_h, bias, zeros], axis=0).astype(BF16)
            v_ref[h] = jnp.where(vlane < HEAD_DIM, vp,
                                 jnp.where(vlane == HEAD_DIM, 1.0, 0.0)).astype(BF16)
        else:
            kt_ref[h] = jnp.concatenate([bias, zeros, k_h], axis=0).astype(BF16)
            v_ref[h] = jnp.where(vlane >= HEAD_DIM, vp,
                                 jnp.where(vlane == 0, 1.0, 0.0)).astype(BF16)

    q_ref[...] = (project(win_ref[:, :ATTN_WIDTH]) * (LOG2E / math.sqrt(HEAD_DIM))).astype(BF16)
    s_ref[...] = project(ws_ref[...]).astype(BF16)


def _slab_spec(w, n_steps, steps_per_batch):
    rows = next(r for r in range(BF16_SUBLANES, w.shape[0] + 1, BF16_SUBLANES)
                if w.shape[0] % r == 0 and w.shape[0] // r <= n_steps)
    last = w.shape[0] // rows - 1
    return pl.BlockSpec((rows, w.shape[1]),
                        lambda b, i: (jnp.minimum(b * steps_per_batch + i, last), 0))


def _head_call(x, g1, w1, w3, w2, gm, w_in, ws, bf, later):
    B, L, D = x.shape
    tm = TOKEN_TILE
    tile = lambda w: pl.BlockSpec((None, tm, w), lambda b, i: (b, i, 0))
    slabs = [_slab_spec(w, B * (L // tm), L // tm) for w in later]
    out_shape = (
        jax.ShapeDtypeStruct((B, L, D), F32),
        jax.ShapeDtypeStruct((B, L, ATTN_WIDTH), BF16),
        jax.ShapeDtypeStruct((B, ATTN_HEADS, HEAD_PAIR, L), BF16),
        jax.ShapeDtypeStruct((B, ATTN_HEADS, L, HEAD_PAIR), BF16),
        jax.ShapeDtypeStruct((B, L, SSM_WIDTH), BF16),
        jax.ShapeDtypeStruct((B, ATTN_HEADS, L), F32),
        jax.ShapeDtypeStruct((B, ATTN_HEADS, LANES), F32),
        jax.ShapeDtypeStruct((B, ATTN_HEADS, LANES), F32),
    ) + tuple(jax.ShapeDtypeStruct(w.shape, BF16) for w in later)
    outs = pl.pallas_call(
        functools.partial(_head_kernel, n_later=len(later)),
        grid=(B, L // tm),
        in_specs=[tile(D), _const_spec(g1.shape), _const_spec(w1.shape),
                  _const_spec(w3.shape), _const_spec(w2.shape), _const_spec(gm.shape),
                  _const_spec(w_in.shape), _const_spec(ws.shape), _const_spec(bf.shape)] + slabs,
        out_specs=(tile(D), tile(ATTN_WIDTH),
                   pl.BlockSpec((None, ATTN_HEADS, HEAD_PAIR, tm), lambda b, i: (b, 0, 0, i)),
                   pl.BlockSpec((None, ATTN_HEADS, tm, HEAD_PAIR), lambda b, i: (b, 0, i, 0)),
                   tile(SSM_WIDTH),
                   pl.BlockSpec((None, ATTN_HEADS, tm), lambda b, i: (b, 0, i)),
                   pl.BlockSpec((None, ATTN_HEADS, LANES), lambda b, i: (b, 0, 0)),
                   pl.BlockSpec((None, ATTN_HEADS, LANES), lambda b, i: (b, 0, 0)))
                  + tuple(slabs),
        out_shape=out_shape,
        scratch_shapes=[pltpu.VMEM((ATTN_HEADS, LANES), F32)],
        compiler_params=pltpu.CompilerParams(
            dimension_semantics=("arbitrary", "arbitrary"),
            vmem_limit_bytes=VMEM_LIMIT),
        name="head",
    )(x, g1, w1, w3, w2, gm, w_in, ws, bf, *later)
    return outs[:8], outs[8:]


def _attn_kernel(q_ref, *refs):
    _attn_prepare(0, q_ref, *refs)
    _attn_first_stage(0, q_ref, *refs)

    def q_tile(n, carry):
        _attn_q_tile(n, q_ref, *refs)
        return carry

    lax.fori_loop(0, q_ref.shape[0] // ATTN_TILE, q_tile, 0)


def _tile_start(i):
    return pl.multiple_of(i * ATTN_TILE, ATTN_TILE)


def _q_heads(q):
    lane = lax.broadcasted_iota(jnp.int32, (1, HEAD_PAIR), 1)
    first = lane < HEAD_DIM
    ones_even = jnp.where((lane >= HEAD_DIM) & (lane < HEAD_DIM + N_BIAS), 1.0, 0.0).astype(BF16)
    ones_odd = jnp.where(lane < N_BIAS, 1.0, 0.0).astype(BF16)
    return first, (jnp.where(first, q, ones_even), jnp.where(first, ones_odd, q))


def _attn_prepare(n, q_ref, kt_ref, v_ref, c_ref, kn_ref, cn_ref, o_ref, s_e0, s_e1, s_o0, s_o1, m_ref,
                  acc_ref, count_ref):
    t = ATTN_TILE
    q = q_ref[pl.ds(_tile_start(n), t), :]
    first, q_heads = _q_heads(q)
    q_sq = q.astype(F32) * q.astype(F32)
    row = lax.broadcasted_iota(jnp.int32, (t, t), 0)
    col = lax.broadcasted_iota(jnp.int32, (t, t), 1)
    tile_id = lax.broadcasted_iota(jnp.int32, (1, LANES), 1)
    needed = tile_id < 0
    for h, s_ref in enumerate((s_o0, s_o1)):
        s = jnp.where(col <= row, jnp.dot(q_heads[h], kt_ref[h, :, pl.ds(_tile_start(n), t)],
                                          preferred_element_type=F32), NEG_BIG)
        s_ref[...] = s
        m = jnp.max(s, axis=-1, keepdims=True)
        m_ref[h] = m
        q_norm2 = jnp.max(jnp.sum(jnp.where(first == (h == 0), q_sq, 0.0), axis=-1, keepdims=True),
                          axis=0, keepdims=True)
        k_norm2, c_next = kn_ref[h:h + 1, :], cn_ref[h:h + 1, :]
        c_q = c_ref[h:h + 1, pl.ds(_tile_start(n), LANES)][:, 0:1]
        reach = jnp.sqrt(q_norm2 * k_norm2) * BOUND_SLACK_MUL + BOUND_SLACK_ADD + c_q - c_next
        needed = needed | (reach - jnp.min(m, axis=0, keepdims=True) >= -SKIP_LOG2)
    first_needed = jnp.min(jnp.where(needed & (tile_id < n), tile_id, n).astype(F32))
    count_ref[0] = n - first_needed.astype(jnp.int32)


def _attn_first_stage(n, q_ref, kt_ref, v_ref, c_ref, kn_ref, cn_ref, o_ref, s_e0, s_e1, s_o0, s_o1,
                      m_ref, acc_ref, count_ref):
    t = ATTN_TILE
    _, q_heads = _q_heads(q_ref[pl.ds(_tile_start(n), t), :])
    older = _tile_start(jnp.maximum(n - 1, 0))
    for h, s_ref in enumerate((s_e0, s_e1)):
        s_ref[...] = jnp.dot(q_heads[h], kt_ref[h, :, pl.ds(older, t)], preferred_element_type=F32)
    for h, s_ref in enumerate((s_o0, s_o1)):
        acc_ref[h] = jnp.dot(jnp.exp2(s_ref[...] - m_ref[h]).astype(BF16),
                             v_ref[h, pl.ds(_tile_start(n), t), :], preferred_element_type=F32)


def _attn_q_tile(n, q_ref, kt_ref, v_ref, c_ref, kn_ref, cn_ref, o_ref, s_e0, s_e1, s_o0, s_o1, m_ref,
                 acc_ref, count_ref):
    t = ATTN_TILE
    count = count_ref[0]
    start = _tile_start
    first, q_heads = _q_heads(q_ref[pl.ds(start(n), t), :])
    s_buf = ((s_e0, s_e1), (s_o0, s_o1))
    c_q = [c_ref[h:h + 1, pl.ds(start(n), LANES)][:, 0:1] for h in range(2)]

    def qk(h, kv):
        return jnp.dot(q_heads[h], kt_ref[h, :, pl.ds(start(kv), t)], preferred_element_type=F32)

    def consume(h, s, kv):
        d = c_ref[h:h + 1, pl.ds(start(kv), LANES)][:, 0:1] - c_q[h]
        m_old = m_ref[h]
        m_new = jnp.maximum(m_old, jnp.max(s, axis=-1, keepdims=True) - d)
        p = jnp.exp2(s - (m_new + d)).astype(BF16)
        m_ref[h] = m_new
        acc_ref[h] = jnp.exp2(m_old - m_new) * acc_ref[h] + jnp.dot(
            p, v_ref[h, pl.ds(start(kv), t), :], preferred_element_type=F32)

    def step(kv, par):
        for h in range(2):
            s_buf[1 - par][h][...] = qk(h, kv - 1)
        for h in range(2):
            consume(h, s_buf[par][h][...], kv)

    def pair(i, carry):
        step(n - 1 - 2 * i, 0)
        step(n - 2 - 2 * i, 1)
        return carry

    n_pairs = jnp.maximum(count - 1, 0) // 2
    lax.fori_loop(0, n_pairs, pair, 0)
    left = count - 2 * n_pairs
    last = n - count

    def finish():
        acc0, acc1 = acc_ref[0], acc_ref[1]
        o_ref[pl.ds(start(n), t), :] = jnp.where(first, acc0 / acc0[:, HEAD_DIM:HEAD_DIM + 1],
                                                 acc1 / acc1[:, 0:1])
        nxt = jnp.minimum(n + 1, q_ref.shape[0] // t - 1)
        rest = (q_ref, kt_ref, v_ref, c_ref, kn_ref, cn_ref, o_ref, s_e0, s_e1, s_o0, s_o1, m_ref,
                acc_ref, count_ref)
        _attn_prepare(nxt, *rest)
        _attn_first_stage(nxt, *rest)

    @pl.when(left == 2)
    def _():
        step(last + 1, 0)
        for h in range(2):
            consume(h, s_buf[1][h][...], last)
        finish()

    @pl.when(left == 1)
    def _():
        for h in range(2):
            consume(h, s_buf[0][h][...], last)
        finish()

    @pl.when(left == 0)
    def _():
        finish()


def _attn_call(q, kt, v, ct, k_norm2, c_next):
    B, L, _ = q.shape
    t = ATTN_TILE
    assert L // t <= LANES
    by_pair = lambda a: a.reshape(B, N_PAIRS, 2, a.shape[-1])
    pair_block = lambda *shape: pl.BlockSpec((None, 2) + shape, lambda b, p: (b, p, 0, 0))
    pair_rows = lambda w: pl.BlockSpec((None, None, 2, w), lambda b, p: (b, p, 0, 0))
    lanes_of_pair = pl.BlockSpec((None, L, HEAD_PAIR), lambda b, p: (b, 0, p))
    return pl.pallas_call(
        _attn_kernel,
        grid=(B, N_PAIRS),
        in_specs=[lanes_of_pair, pair_block(HEAD_PAIR, L), pair_block(L, HEAD_PAIR),
                  pair_rows(L), pair_rows(LANES), pair_rows(LANES)],
        out_specs=lanes_of_pair,
        out_shape=jax.ShapeDtypeStruct((B, L, ATTN_WIDTH), F32),
        scratch_shapes=[pltpu.VMEM((t, t), F32)] * 4
                       + [pltpu.VMEM((2, t, 1), F32), pltpu.VMEM((2, t, HEAD_PAIR), F32),
                          pltpu.SMEM((1,), jnp.int32)],
        compiler_params=pltpu.CompilerParams(
            dimension_semantics=("arbitrary", "arbitrary"),
            vmem_limit_bytes=VMEM_LIMIT),
        name="attn",
    )(q, kt, v, by_pair(ct), by_pair(k_norm2), by_pair(c_next))


def _gelu_tanh(x):
    return 0.5 * x * (1.0 + jnp.tanh(math.sqrt(2.0 / math.pi) * (x + 0.044715 * (x * x * x))))


def _cis(mag_arg, ang):
    mag = jnp.exp(mag_arg)
    return mag * jnp.cos(ang), mag * jnp.sin(ang)


def _cmul(ar, ai, br, bi):
    return ar * br - ai * bi, ar * bi + ai * br


def _cpow2(zr, zi, n):
    assert n & (n - 1) == 0
    while n > 1:
        zr, zi = zr * zr - zi * zi, 2.0 * zr * zi
        n //= 2
    return zr, zi


def _ssm_kernel(arow_ref, acol_ref, ldt_ref, bt_ref, cab_ref, dcol_ref, e_ref, f_ref, z_ref,
                *, chunks_per_seq):
    H = SSM_GROUP_CH

    @pl.when(pl.program_id(0) == 0)
    def _():
        z_ref[:, 0:SSM_CHUNK * H, :] = jnp.zeros((z_ref.shape[0], SSM_CHUNK * H, LANES), F32)

    for gi in range(SSM_GROUPS_PER_STEP):
        y = _ssm_group(arow_ref.at[gi], acol_ref.at[gi], ldt_ref.at[gi], bt_ref.at[gi], cab_ref.at[gi],
                       dcol_ref.at[gi], e_ref[:, gi * H:(gi + 1) * H, :], z_ref.at[gi], chunks_per_seq)
        f_ref[:, gi * H:(gi + 1) * H, :] = y.reshape(SSM_CHUNK, H, y.shape[-1])


def _ssm_group(arow_ref, acol_ref, ldt_ref, bt_ref, cab_ref, dcol_ref, e, z_ref, chunks_per_seq):
    T, P, H = SSM_CHUNK, SSM_STATE, SSM_GROUP_CH
    TH = T * H
    hi = lax.Precision.HIGHEST
    dt = jnp.exp(ldt_ref[...])

    lam_r, lam_i = dt * arow_ref[0:1, :], dt * arow_ref[1:2, :]
    j0 = lax.broadcasted_iota(jnp.int32, (T, 2 * P), 0).astype(F32)
    pa0, pb0 = _cis(j0 * lam_r, j0 * lam_i)
    pa1, pb1 = _cmul(pa0, pb0, *_cis(lam_r, lam_i))
    over_h = lambda a: jnp.concatenate(
        [jnp.broadcast_to(a[j:j + 1, :], (H, 2 * P)) for j in range(T)], axis=0)
    ca, cb = jnp.tile(cab_ref[0], (T, 1)), jnp.tile(cab_ref[1], (T, 1))
    c_pow0 = over_h(pa0) * ca + over_h(pb0) * cb
    c_pow1 = over_h(pa1) * ca + over_h(pb1) * cb

    a_r, a_i = acol_ref[:, 0:1], acol_ref[:, 1:2]
    lr, li = dt * a_r, dt * a_i
    abar_r, abar_i = _cis(lr, li)
    nr, ni = abar_r - 1.0, abar_i
    den = a_r * a_r + a_i * a_i
    fr, fi = (nr * a_r + ni * a_i) / den, (ni * a_r - nr * a_i) / den
    b_r, b_i = bt_ref[0], bt_ref[1]
    bb_r, bb_i = fr * b_r - fi * b_i, fr * b_i + fi * b_r

    kcol = jnp.dot(c_pow0, jnp.concatenate([bb_r, bb_i], axis=0), precision=hi,
                   preferred_element_type=F32)
    lane_h = lax.broadcasted_iota(jnp.int32, (H, LANES), 1) % H
    skip = jnp.where(lane_h == lax.broadcasted_iota(jnp.int32, (H, LANES), 0), dcol_ref[...], 0.0)

    z_ref[TH:2 * TH, :] = kcol
    z_ref[TH:TH + H, :] = kcol[:H] + skip
    lane_group = lax.broadcasted_iota(jnp.int32, (1, LANES), 1) // H
    groups_per_block = LANES // H
    blocks = []
    for v in range(TH // LANES):
        blk = None
        for u in range(groups_per_block):
            s = v * groups_per_block + u
            piece = z_ref[TH - H * s:2 * TH - H * s, :]
            blk = piece if blk is None else jnp.where(lane_group == u, piece, blk)
        blocks.append(blk.astype(BF16))
    mt = jnp.concatenate(blocks, axis=1)

    expo = (groups_per_block - 1 - lane_group).astype(F32)
    wr, wi = _cis(lr * expo, li * expo)
    hop_r, hop_i = _cpow2(abar_r, abar_i, groups_per_block)
    w1_r, w1_i = [], []
    for v in range(TH // LANES):
        w1_r.insert(0, wr * bb_r - wi * bb_i)
        w1_i.insert(0, wr * bb_i + wi * bb_r)
        wr, wi = _cmul(wr, wi, hop_r, hop_i)
    w1t = jnp.concatenate([jnp.concatenate(w1_r, axis=1),
                           jnp.concatenate(w1_i, axis=1)], axis=0).astype(BF16)

    e = e.reshape(TH, e.shape[-1])
    y = jnp.dot(mt, e, preferred_element_type=F32)
    st = jnp.dot(w1t, e, preferred_element_type=F32)
    sr, si = st[:P], st[P:]
    pos = lax.broadcasted_iota(jnp.int32, sr.shape, 1) % chunks_per_seq

    def shifted(a, shift):
        return jnp.where(pos >= shift, pltpu.roll(a, shift, 1), 0.0)

    qr, qi = _cpow2(abar_r, abar_i, T)
    shift = 1
    while shift < chunks_per_seq:
        srs, sis = shifted(sr, shift), shifted(si, shift)
        sr, si = sr + qr * srs - qi * sis, si + qr * sis + qi * srs
        qr, qi = qr * qr - qi * qi, 2.0 * qr * qi
        shift *= 2
    x_prev = jnp.concatenate([shifted(sr, 1), shifted(si, 1)], axis=0).astype(BF16)
    y = y + jnp.dot(c_pow1.astype(BF16), x_prev, preferred_element_type=F32)
    return _gelu_tanh(y)


def _ssm_call(arow, acol, ldt, bt, cab, dcol, e, chunks_per_seq):
    T, _, NC = e.shape
    G = arow.shape[0]
    gps = SSM_GROUPS_PER_STEP
    assert G % gps == 0
    grp = lambda a: pl.BlockSpec((gps,) + a.shape[1:], lambda g: (g,) + (0,) * (a.ndim - 1))
    channels = pl.BlockSpec((T, gps * SSM_GROUP_CH, NC), lambda g: (0, g, 0))
    return pl.pallas_call(
        functools.partial(_ssm_kernel, chunks_per_seq=chunks_per_seq),
        grid=(G // gps,),
        in_specs=[grp(a) for a in (arow, acol, ldt, bt, cab, dcol)] + [channels],
        out_specs=channels,
        out_shape=jax.ShapeDtypeStruct((T, SSM_WIDTH, NC), F32),
        scratch_shapes=[pltpu.VMEM((gps, 2 * T * SSM_GROUP_CH, LANES), F32)],
        compiler_params=pltpu.CompilerParams(
            dimension_semantics=("arbitrary",), vmem_limit_bytes=VMEM_LIMIT),
        name="ssm",
    )(arow, acol, ldt, bt, cab, dcol, e)


def _ssm_param_layouts(a_re, a_im, log_dt, b_re, b_im, c_re, c_im, d_skip):
    G = a_re.shape[0]
    arow = jnp.stack([jnp.concatenate([a_re, a_re], -1), jnp.concatenate([a_im, a_im], -1)], 1)
    acol = jnp.stack([a_re, a_im], -1)
    reps = LANES // SSM_GROUP_CH
    bt = jnp.stack([jnp.tile(b_re, (1, 1, reps)), jnp.tile(b_im, (1, 1, reps))], 1)
    cab = jnp.stack([jnp.concatenate([c_re, -c_im], -1), jnp.concatenate([-c_im, -c_re], -1)], 1)
    return (arow.astype(F32), acol.astype(F32), log_dt.reshape(G, 1, 1).astype(F32),
            bt.astype(F32), cab.astype(F32), d_skip.reshape(G, SSM_GROUP_CH, 1).astype(F32))


def _tail_kernel(h1_ref, attn_ref, y_ref, p_ref, wglu_ref, bglu_ref, ga_ref, gs_ref,
                 wo_ref, g2_ref, w1_ref, w3_ref, w2_ref, gp_ref, wpg_ref,
                 wpp_ref, gf_ref, o_ref):
    y = y_ref[...]
    glu = y * _sigmoid(jnp.dot(y.astype(BF16), wglu_ref[...], preferred_element_type=F32)
                       + bglu_ref[...])
    an = _rms(attn_ref[...], ga_ref[...]).astype(BF16)
    sn = _rms(glu, gs_ref[...]).astype(BF16)
    h = (h1_ref[...] + jnp.dot(an, wo_ref[:ATTN_WIDTH, :], preferred_element_type=F32)
         + jnp.dot(sn, wo_ref[ATTN_WIDTH:, :], preferred_element_type=F32))
    h = h + 0.5 * _swiglu(_rms(h, g2_ref[...]).astype(BF16), w1_ref, w3_ref, w2_ref)
    gate = _sigmoid(jnp.dot(_rms(h, gp_ref[...]).astype(BF16), wpg_ref[...],
                            preferred_element_type=F32))
    h = h + gate * jnp.dot(p_ref[...].astype(BF16), wpp_ref[...], preferred_element_type=F32)
    o_ref[...] = _rms(h, gf_ref[...])


def _tail_call(h1, attn, y, p, *consts):
    B, L, D = h1.shape
    tm = TOKEN_TILE
    tile = lambda w: pl.BlockSpec((None, tm, w), lambda b, i: (b, i, 0))
    return pl.pallas_call(
        _tail_kernel,
        grid=(B, L // tm),
        in_specs=[tile(D), tile(ATTN_WIDTH), tile(SSM_WIDTH), tile(PLE_DIM)]
                 + [_const_spec(c.shape) for c in consts],
        out_specs=tile(D),
        out_shape=jax.ShapeDtypeStruct((B, L, D), F32),
        compiler_params=pltpu.CompilerParams(
            dimension_semantics=("arbitrary", "arbitrary"),
            vmem_limit_bytes=VMEM_LIMIT),
        name="tail",
    )(h1, attn, y, p, *consts)


def kernel(x, p, g_ffn1, w1_a, w3_a, w2_a, g_mix, w_in, b_f, a_re, a_im, log_dt, b_re, b_im, c_re, c_im, d_skip, w_glu, b_glu, g_attn_out, g_ssm_out, w_out, g_ffn2, w1_b, w3_b, w2_b, g_ple, w_ple_gate, w_ple_proj, g_final):
    B, L, D = x.shape
    assert D == D_MODEL and L % ATTN_TILE == 0 and L % TOKEN_TILE == 0 and L % SSM_CHUNK == 0
    assert g_ffn1.shape[0] == 1, "single layer"
    assert TOKEN_TILE == ATTN_TILE, "decay bias rows are relative to the kv tile start"
    row = lambda g: g.reshape(1, -1).astype(F32)
    bf = lambda w: w.astype(BF16)
    w_in16 = bf(w_in[0])
    assert w_in16.shape[1] == 3 * ATTN_WIDTH + ATTN_HEADS + SSM_WIDTH

    (h1, q, kt, v, s_in, ct, k_norm2, c_next), (w1_b16, w3_b16, w2_b16, w_out16, w_gate16) = _head_call(
        x, row(g_ffn1[0]), bf(w1_a[0]), bf(w3_a[0]), bf(w2_a[0]), row(g_mix[0]),
        w_in16, w_in16[:, 3 * ATTN_WIDTH + ATTN_HEADS:],
        b_f[0].reshape(ATTN_HEADS, 1).astype(F32),
        later=tuple(w.astype(F32) for w in (w1_b[0], w3_b[0], w2_b[0], w_out[0], w_ple_gate[0])))

    T = SSM_CHUNK
    chunks_per_seq = L // T
    n_chunks = B * chunks_per_seq
    e = s_in.reshape(n_chunks, T, SSM_WIDTH).transpose(1, 2, 0)
    f = _ssm_call(*_ssm_param_layouts(a_re[0], a_im[0], log_dt[0], b_re[0], b_im[0],
                                      c_re[0], c_im[0], d_skip[0]),
                  e, chunks_per_seq)
    y = f.transpose(2, 0, 1).reshape(B, L, SSM_WIDTH)

    attn = _attn_call(q, kt, v, ct, k_norm2, c_next)

    return _tail_call(
        h1, attn, y, p[0],
        bf(w_glu[0]), row(b_glu[0]), row(g_attn_out[0]), row(g_ssm_out[0]),
        w_out16, row(g_ffn2[0]),
        w1_b16, w3_b16, w2_b16, row(g_ple[0]), w_gate16,
        bf(w_ple_proj[0]), row(g_final))
```

```python
import functools
import math

import jax
import jax.numpy as jnp
from jax import lax
from jax.experimental import pallas as pl
from jax.experimental.pallas import tpu as pltpu

D_MODEL = 1024
ATTN_HEADS = 8
HEAD_DIM = 64
ATTN_WIDTH = ATTN_HEADS * HEAD_DIM
SSM_WIDTH = D_MODEL - ATTN_WIDTH
SSM_GROUP_CH = 16
SSM_GROUPS = SSM_WIDTH // SSM_GROUP_CH
SSM_STATE = 64
D_FF = 2816
PLE_DIM = 256
EPS = 1e-6

LANES = 128
BF16_SUBLANES = 16
HEAD_PAIR = 2 * HEAD_DIM
N_PAIRS = ATTN_HEADS // 2
FF_CHUNK = 256
TOKEN_TILE = 512
ATTN_TILE = 512
SSM_CHUNK = 32
SSM_GROUPS_PER_STEP = 2
NEG_BIG = -1e30
SKIP_LOG2 = 140.0
BOUND_SLACK_MUL = 1.001
BOUND_SLACK_ADD = 1.0
LOG2E = math.log2(math.e)
N_BIAS = 3
BIAS_ROWS = 8
VMEM_LIMIT = 56 * 1024 * 1024

BF16 = jnp.bfloat16
F32 = jnp.float32


def _rms(x, g):
    ms = jnp.mean(x * x, axis=-1, keepdims=True)
    return x * lax.rsqrt(ms + EPS) * g


def _sigmoid(x):
    return 1.0 / (1.0 + jnp.exp(-x))


def _swiglu(xn, w1_ref, w3_ref, w2_ref):
    acc = None
    for c in range(D_FF // FF_CHUNK):
        sl = slice(c * FF_CHUNK, (c + 1) * FF_CHUNK)
        a = jnp.dot(xn, w1_ref[:, sl], preferred_element_type=F32)
        b = jnp.dot(xn, w3_ref[:, sl], preferred_element_type=F32)
        gated = (a * _sigmoid(a) * b).astype(BF16)
        part = jnp.dot(gated, w2_ref[sl, :], preferred_element_type=F32)
        acc = part if acc is None else acc + part
    return acc


def _const_spec(shape):
    nd = len(shape)
    return pl.BlockSpec(shape, lambda *_: (0,) * nd, pipeline_mode=pl.Buffered(1))


def _head_kernel(*refs, n_later):
    (x_ref, g1_ref, w1_ref, w3_ref, w2_ref, gm_ref, win_ref, ws_ref, bf_ref) = refs[:9]
    later_in = refs[9:9 + n_later]
    h1_ref, q_ref, kt_ref, v_ref, s_ref, ct_ref, kn_ref, cn_ref = refs[9 + n_later:17 + n_later]
    later_out = refs[17 + n_later:17 + 2 * n_later]
    carry_ref = refs[-1]
    @pl.when(pl.program_id(1) == 0)
    def _():
        carry_ref[...] = jnp.zeros_like(carry_ref)
        kn_ref[...] = jnp.zeros_like(kn_ref)
        cn_ref[...] = jnp.zeros_like(cn_ref)

    for src, dst in zip(later_in, later_out):
        dst[...] = src[...].astype(BF16)
    tm = x_ref.shape[0]
    x = x_ref[...]
    h1 = x + 0.5 * _swiglu(_rms(x, g1_ref[...]).astype(BF16), w1_ref, w3_ref, w2_ref)
    h1_ref[...] = h1
    un = _rms(h1, gm_ref[...]).astype(BF16)
    project = lambda w: jnp.dot(un, w, preferred_element_type=F32)
    zf = project(win_ref[:, 3 * ATTN_WIDTH:3 * ATTN_WIDTH + LANES])
    kv = project(win_ref[:, ATTN_WIDTH:3 * ATTN_WIDTH])
    zft = zf.T[:ATTN_HEADS, :] + bf_ref[...]
    logf = jnp.minimum(zft, 0.0) - jnp.log1p(jnp.exp(-jnp.abs(zft)))
    lane = lax.broadcasted_iota(jnp.int32, logf.shape, 1)
    c = logf
    shift = 1
    while shift < tm:
        c = c + jnp.where(lane >= shift, pltpu.roll(c, shift, 1), 0.0)
        shift *= 2

    c_abs = c + carry_ref[:, 0:1]
    ct_ref[...] = c_abs * LOG2E
    carry_ref[...] = jnp.broadcast_to(c_abs[:, tm - 1:tm], carry_ref.shape)

    rel = (c - c[:, 0:1]) * LOG2E
    hi = rel.astype(BF16).astype(F32)
    mid = (rel - hi).astype(BF16).astype(F32)
    lo = (rel - hi - mid).astype(BF16).astype(F32)
    kt = kv[:, :ATTN_WIDTH].astype(BF16).astype(F32).T
    vv = kv[:, ATTN_WIDTH:]
    k_sq = (kt * kt).reshape(ATTN_HEADS, HEAD_DIM, tm)
    tile = pl.program_id(1)
    stat_lane = lax.broadcasted_iota(jnp.int32, kn_ref.shape, 1)

    kn_ref[...] = jnp.where(stat_lane == tile,
                            jnp.max(jnp.sum(k_sq, axis=1), axis=-1, keepdims=True), kn_ref[...])
    cn_ref[...] = jnp.where(stat_lane == tile - 1, c_abs[:, 0:1] * LOG2E, cn_ref[...])
    sub = lax.broadcasted_iota(jnp.int32, (BIAS_ROWS, tm), 0)
    zeros = jnp.zeros((HEAD_DIM - BIAS_ROWS, tm), F32)
    vlane = lax.broadcasted_iota(jnp.int32, (tm, HEAD_PAIR), 1)
    for h in range(ATTN_HEADS):
        bias = jnp.where(sub == 0, -hi[h:h + 1],
                         jnp.where(sub == 1, -mid[h:h + 1],
                                   jnp.where(sub == 2, -lo[h:h + 1], 0.0)))
        k_h = kt[h * HEAD_DIM:(h + 1) * HEAD_DIM]
        vp = vv[:, (h // 2) * HEAD_PAIR:(h // 2 + 1) * HEAD_PAIR]
        if h % 2 == 0:
            kt_ref[h] = jnp.concatenate([k_h, bias, zeros], axis=0).astype(BF16)
            v_ref[h] = jnp.where(vlane < HEAD_DIM, vp,
                                 jnp.where(vlane == HEAD_DIM, 1.0, 0.0)).astype(BF16)
        else:
            kt_ref[h] = jnp.concatenate([bias, zeros, k_h], axis=0).astype(BF16)
            v_ref[h] = jnp.where(vlane >= HEAD_DIM, vp,
                                 jnp.where(vlane == 0, 1.0, 0.0)).astype(BF16)

    q_ref[...] = (project(win_ref[:, :ATTN_WIDTH]) * (LOG2E / math.sqrt(HEAD_DIM))).astype(BF16)
    s_ref[...] = project(ws_ref[...]).astype(BF16)


def _slab_spec(w, n_steps, steps_per_batch):
    rows = next(r for r in range(BF16_SUBLANES, w.shape[0] + 1, BF16_SUBLANES)
                if w.shape[0] % r == 0 and w.shape[0] // r <= n_steps)
    last = w.shape[0] // rows - 1
    return pl.BlockSpec((rows, w.shape[1]),
                        lambda b, i: (jnp.minimum(b * steps_per_batch + i, last), 0))


def _head_call(x, g1, w1, w3, w2, gm, w_in, ws, bf, later):
    B, L, D = x.shape
    tm = TOKEN_TILE
    tile = lambda w: pl.BlockSpec((None, tm, w), lambda b, i: (b, i, 0))
    slabs = [_slab_spec(w, B * (L // tm), L // tm) for w in later]
    out_shape = (
        jax.ShapeDtypeStruct((B, L, D), F32),
        jax.ShapeDtypeStruct((B, L, ATTN_WIDTH), BF16),
        jax.ShapeDtypeStruct((B, ATTN_HEADS, HEAD_PAIR, L), BF16),
        jax.ShapeDtypeStruct((B, ATTN_HEADS, L, HEAD_PAIR), BF16),
        jax.ShapeDtypeStruct((B, L, SSM_WIDTH), BF16),
        jax.ShapeDtypeStruct((B, ATTN_HEADS, L), F32),
        jax.ShapeDtypeStruct((B, ATTN_HEADS, LANES), F32),
        jax.ShapeDtypeStruct((B, ATTN_HEADS, LANES), F32),
    ) + tuple(jax.ShapeDtypeStruct(w.shape, BF16) for w in later)
    outs = pl.pallas_call(
        functools.partial(_head_kernel, n_later=len(later)),
        grid=(B, L // tm),
        in_specs=[tile(D), _const_spec(g1.shape), _const_spec(w1.shape),
                  _const_spec(w3.shape), _const_spec(w2.shape), _const_spec(gm.shape),
                  _const_spec(w_in.shape), _const_spec(ws.shape), _const_spec(bf.shape)] + slabs,
        out_specs=(tile(D), tile(ATTN_WIDTH),
                   pl.BlockSpec((None, ATTN_HEADS, HEAD_PAIR, tm), lambda b, i: (b, 0, 0, i)),
                   pl.BlockSpec((None, ATTN_HEADS, tm, HEAD_PAIR), lambda b, i: (b, 0, i, 0)),
                   tile(SSM_WIDTH),
                   pl.BlockSpec((None, ATTN_HEADS, tm), lambda b, i: (b, 0, i)),
                   pl.BlockSpec((None, ATTN_HEADS, LANES), lambda b, i: (b, 0, 0)),
                   pl.BlockSpec((None, ATTN_HEADS, LANES), lambda b, i: (b, 0, 0)))
                  + tuple(slabs),
        out_shape=out_shape,
        scratch_shapes=[pltpu.VMEM((ATTN_HEADS, LANES), F32)],
        compiler_params=pltpu.CompilerParams(
            dimension_semantics=("arbitrary", "arbitrary"),
            vmem_limit_bytes=VMEM_LIMIT),
        name="head",
    )(x, g1, w1, w3, w2, gm, w_in, ws, bf, *later)
    return outs[:8], outs[8:]


def _attn_kernel(q_ref, *refs):
    _attn_prepare(0, q_ref, *refs)
    _attn_first_stage(0, q_ref, *refs)

    def q_tile(n, carry):
        _attn_q_tile(n, q_ref, *refs)
        return carry

    lax.fori_loop(0, q_ref.shape[0] // ATTN_TILE, q_tile, 0)


def _tile_start(i):
    return pl.multiple_of(i * ATTN_TILE, ATTN_TILE)


def _q_heads(q):
    lane = lax.broadcasted_iota(jnp.int32, (1, HEAD_PAIR), 1)
    first = lane < HEAD_DIM
    ones_even = jnp.where((lane >= HEAD_DIM) & (lane < HEAD_DIM + N_BIAS), 1.0, 0.0).astype(BF16)
    ones_odd = jnp.where(lane < N_BIAS, 1.0, 0.0).astype(BF16)
    return first, (jnp.where(first, q, ones_even), jnp.where(first, ones_odd, q))


def _attn_prepare(n, q_ref, kt_ref, v_ref, c_ref, kn_ref, cn_ref, o_ref, s_e0, s_e1, s_o0, s_o1, m_ref,
                  acc_ref, count_ref):
    t = ATTN_TILE
    q = q_ref[pl.ds(_tile_start(n), t), :]
    first, q_heads = _q_heads(q)
    q_sq = q.astype(F32) * q.astype(F32)
    row = lax.broadcasted_iota(jnp.int32, (t, t), 0)
    col = lax.broadcasted_iota(jnp.int32, (t, t), 1)
    tile_id = lax.broadcasted_iota(jnp.int32, (1, LANES), 1)
    needed = tile_id < 0
    for h, s_ref in enumerate((s_o0, s_o1)):
        s = jnp.where(col <= row, jnp.dot(q_heads[h], kt_ref[h, :, pl.ds(_tile_start(n), t)],
                                          preferred_element_type=F32), NEG_BIG)
        s_ref[...] = s
        m = jnp.max(s, axis=-1, keepdims=True)
        m_ref[h] = jnp.broadcast_to(m, (t, LANES))
        q_norm2 = jnp.max(jnp.sum(jnp.where(first == (h == 0), q_sq, 0.0), axis=-1, keepdims=True),
                          axis=0, keepdims=True)
        k_norm2, c_next = kn_ref[h:h + 1, :], cn_ref[h:h + 1, :]
        c_q = c_ref[h:h + 1, pl.ds(_tile_start(n), LANES)][:, 0:1]
        reach = jnp.sqrt(q_norm2 * k_norm2) * BOUND_SLACK_MUL + BOUND_SLACK_ADD + c_q - c_next
        needed = needed | (reach - jnp.min(m, axis=0, keepdims=True) >= -SKIP_LOG2)
    first_needed = jnp.min(jnp.where(needed & (tile_id < n), tile_id, n).astype(F32))
    count_ref[0] = n - first_needed.astype(jnp.int32)


def _attn_first_stage(n, q_ref, kt_ref, v_ref, c_ref, kn_ref, cn_ref, o_ref, s_e0, s_e1, s_o0, s_o1,
                      m_ref, acc_ref, count_ref):
    t = ATTN_TILE
    _, q_heads = _q_heads(q_ref[pl.ds(_tile_start(n), t), :])
    older = _tile_start(jnp.maximum(n - 1, 0))
    for h, s_ref in enumerate((s_e0, s_e1)):
        s_ref[...] = jnp.dot(q_heads[h], kt_ref[h, :, pl.ds(older, t)], preferred_element_type=F32)
    for h, s_ref in enumerate((s_o0, s_o1)):
        acc_ref[h] = jnp.dot(jnp.exp2(s_ref[...] - jnp.tile(m_ref[h], (1, t // LANES))).astype(BF16),
                             v_ref[h, pl.ds(_tile_start(n), t), :], preferred_element_type=F32)


def _attn_q_tile(n, q_ref, kt_ref, v_ref, c_ref, kn_ref, cn_ref, o_ref, s_e0, s_e1, s_o0, s_o1, m_ref,
                 acc_ref, count_ref):
    t = ATTN_TILE
    count = count_ref[0]
    start = _tile_start
    first, q_heads = _q_heads(q_ref[pl.ds(start(n), t), :])
    s_buf = ((s_e0, s_e1), (s_o0, s_o1))
    c_q = [c_ref[h:h + 1, pl.ds(start(n), LANES)][:, 0:1] for h in range(2)]

    def qk(h, kv):
        return jnp.dot(q_heads[h], kt_ref[h, :, pl.ds(start(kv), t)], preferred_element_type=F32)

    def consume(h, s, kv):
        d = c_ref[h:h + 1, pl.ds(start(kv), LANES)][:, 0:1] - c_q[h]
        m_old = m_ref[h]
        m_new = jnp.maximum(m_old, jnp.max(s, axis=-1, keepdims=True) - d)
        p = jnp.exp2(s - jnp.tile(m_new + d, (1, t // LANES))).astype(BF16)
        m_ref[h] = m_new
        acc_ref[h] = jnp.exp2(m_old - m_new) * acc_ref[h] + jnp.dot(
            p, v_ref[h, pl.ds(start(kv), t), :], preferred_element_type=F32)

    def step(kv, par):
        for h in range(2):
            s_buf[1 - par][h][...] = qk(h, kv - 1)
        for h in range(2):
            consume(h, s_buf[par][h][...], kv)

    def pair(i, carry):
        step(n - 1 - 2 * i, 0)
        step(n - 2 - 2 * i, 1)
        return carry

    n_pairs = jnp.maximum(count - 1, 0) // 2
    lax.fori_loop(0, n_pairs, pair, 0)
    left = count - 2 * n_pairs
    last = n - count

    def finish():
        acc0, acc1 = acc_ref[0], acc_ref[1]
        o_ref[pl.ds(start(n), t), :] = jnp.where(first, acc0 / acc0[:, HEAD_DIM:HEAD_DIM + 1],
                                                 acc1 / acc1[:, 0:1])
        nxt = jnp.minimum(n + 1, q_ref.shape[0] // t - 1)
        rest = (q_ref, kt_ref, v_ref, c_ref, kn_ref, cn_ref, o_ref, s_e0, s_e1, s_o0, s_o1, m_ref,
                acc_ref, count_ref)
        _attn_prepare(nxt, *rest)
        _attn_first_stage(nxt, *rest)

    @pl.when(left == 2)
    def _():
        step(last + 1, 0)
        for h in range(2):
            consume(h, s_buf[1][h][...], last)
        finish()

    @pl.when(left == 1)
    def _():
        for h in range(2):
            consume(h, s_buf[0][h][...], last)
        finish()

    @pl.when(left == 0)
    def _():
        finish()


def _attn_call(q, kt, v, ct, k_norm2, c_next):
    B, L, _ = q.shape
    t = ATTN_TILE
    assert L // t <= LANES
    by_pair = lambda a: a.reshape(B, N_PAIRS, 2, a.shape[-1])
    pair_block = lambda *shape: pl.BlockSpec((None, 2) + shape, lambda b, p: (b, p, 0, 0))
    pair_rows = lambda w: pl.BlockSpec((None, None, 2, w), lambda b, p: (b, p, 0, 0))
    lanes_of_pair = pl.BlockSpec((None, L, HEAD_PAIR), lambda b, p: (b, 0, p))
    return pl.pallas_call(
        _attn_kernel,
        grid=(B, N_PAIRS),
        in_specs=[lanes_of_pair, pair_block(HEAD_PAIR, L), pair_block(L, HEAD_PAIR),
                  pair_rows(L), pair_rows(LANES), pair_rows(LANES)],
        out_specs=lanes_of_pair,
        out_shape=jax.ShapeDtypeStruct((B, L, ATTN_WIDTH), F32),
        scratch_shapes=[pltpu.VMEM((t, t), F32)] * 4
                       + [pltpu.VMEM((2, t, LANES), F32), pltpu.VMEM((2, t, HEAD_PAIR), F32),
                          pltpu.SMEM((1,), jnp.int32)],
        compiler_params=pltpu.CompilerParams(
            dimension_semantics=("arbitrary", "arbitrary"),
            vmem_limit_bytes=VMEM_LIMIT),
        name="attn",
    )(q, kt, v, by_pair(ct), by_pair(k_norm2), by_pair(c_next))


def _gelu_tanh(x):
    return 0.5 * x * (1.0 + jnp.tanh(math.sqrt(2.0 / math.pi) * (x + 0.044715 * (x * x * x))))


def _cis(mag_arg, ang):
    mag = jnp.exp(mag_arg)
    return mag * jnp.cos(ang), mag * jnp.sin(ang)


def _cmul(ar, ai, br, bi):
    return ar * br - ai * bi, ar * bi + ai * br


def _cpow2(zr, zi, n):
    assert n & (n - 1) == 0
    while n > 1:
        zr, zi = zr * zr - zi * zi, 2.0 * zr * zi
        n //= 2
    return zr, zi


def _ssm_kernel(arow_ref, acol_ref, ldt_ref, bt_ref, cab_ref, dcol_ref, e_ref, f_ref, z_ref,
                *, chunks_per_seq):
    H = SSM_GROUP_CH

    @pl.when(pl.program_id(0) == 0)
    def _():
        z_ref[:, 0:SSM_CHUNK * H, :] = jnp.zeros((z_ref.shape[0], SSM_CHUNK * H, LANES), F32)

    for gi in range(SSM_GROUPS_PER_STEP):
        y = _ssm_group(arow_ref.at[gi], acol_ref.at[gi], ldt_ref.at[gi], bt_ref.at[gi], cab_ref.at[gi],
                       dcol_ref.at[gi], e_ref[:, gi * H:(gi + 1) * H, :], z_ref.at[gi], chunks_per_seq)
        f_ref[:, gi * H:(gi + 1) * H, :] = y.reshape(SSM_CHUNK, H, y.shape[-1])


def _ssm_group(arow_ref, acol_ref, ldt_ref, bt_ref, cab_ref, dcol_ref, e, z_ref, chunks_per_seq):
    T, P, H = SSM_CHUNK, SSM_STATE, SSM_GROUP_CH
    TH = T * H
    hi = lax.Precision.HIGHEST
    dt = jnp.exp(ldt_ref[...])

    lam_r, lam_i = dt * arow_ref[0:1, :], dt * arow_ref[1:2, :]
    j0 = lax.broadcasted_iota(jnp.int32, (T, 2 * P), 0).astype(F32)
    pa0, pb0 = _cis(j0 * lam_r, j0 * lam_i)
    pa1, pb1 = _cmul(pa0, pb0, *_cis(lam_r, lam_i))
    over_h = lambda a: jnp.concatenate(
        [jnp.broadcast_to(a[j:j + 1, :], (H, 2 * P)) for j in range(T)], axis=0)
    ca, cb = jnp.tile(cab_ref[0], (T, 1)), jnp.tile(cab_ref[1], (T, 1))
    c_pow0 = over_h(pa0) * ca + over_h(pb0) * cb
    c_pow1 = over_h(pa1) * ca + over_h(pb1) * cb

    a_r, a_i = acol_ref[:, 0:1], acol_ref[:, 1:2]
    lr, li = dt * a_r, dt * a_i
    abar_r, abar_i = _cis(lr, li)
    nr, ni = abar_r - 1.0, abar_i
    den = a_r * a_r + a_i * a_i
    fr, fi = (nr * a_r + ni * a_i) / den, (ni * a_r - nr * a_i) / den
    b_r, b_i = bt_ref[0], bt_ref[1]
    bb_r, bb_i = fr * b_r - fi * b_i, fr * b_i + fi * b_r

    kcol = jnp.dot(c_pow0, jnp.concatenate([bb_r, bb_i], axis=0), precision=hi,
                   preferred_element_type=F32)
    lane_h = lax.broadcasted_iota(jnp.int32, (H, LANES), 1) % H
    skip = jnp.where(lane_h == lax.broadcasted_iota(jnp.int32, (H, LANES), 0), dcol_ref[...], 0.0)

    z_ref[TH:2 * TH, :] = kcol
    z_ref[TH:TH + H, :] = kcol[:H] + skip
    lane_group = lax.broadcasted_iota(jnp.int32, (1, LANES), 1) // H
    groups_per_block = LANES // H
    blocks = []
    for v in range(TH // LANES):
        blk = None
        for u in range(groups_per_block):
            s = v * groups_per_block + u
            piece = z_ref[TH - H * s:2 * TH - H * s, :]
            blk = piece if blk is None else jnp.where(lane_group == u, piece, blk)
        blocks.append(blk.astype(BF16))
    mt = jnp.concatenate(blocks, axis=1)

    expo = (groups_per_block - 1 - lane_group).astype(F32)
    wr, wi = _cis(lr * expo, li * expo)
    hop_r, hop_i = _cpow2(abar_r, abar_i, groups_per_block)
    w1_r, w1_i = [], []
    for v in range(TH // LANES):
        w1_r.insert(0, wr * bb_r - wi * bb_i)
        w1_i.insert(0, wr * bb_i + wi * bb_r)
        wr, wi = _cmul(wr, wi, hop_r, hop_i)
    w1t = jnp.concatenate([jnp.concatenate(w1_r, axis=1),
                           jnp.concatenate(w1_i, axis=1)], axis=0).astype(BF16)

    e = e.reshape(TH, e.shape[-1])
    y = jnp.dot(mt, e, preferred_element_type=F32)
    st = jnp.dot(w1t, e, preferred_element_type=F32)
    sr, si = st[:P], st[P:]
    pos = lax.broadcasted_iota(jnp.int32, sr.shape, 1) % chunks_per_seq

    def shifted(a, shift):
        return jnp.where(pos >= shift, pltpu.roll(a, shift, 1), 0.0)

    qr, qi = _cpow2(abar_r, abar_i, T)
    shift = 1
    while shift < chunks_per_seq:
        srs, sis = shifted(sr, shift), shifted(si, shift)
        sr, si = sr + qr * srs - qi * sis, si + qr * sis + qi * srs
        qr, qi = qr * qr - qi * qi, 2.0 * qr * qi
        shift *= 2
    x_prev = jnp.concatenate([shifted(sr, 1), shifted(si, 1)], axis=0).astype(BF16)
    y = y + jnp.dot(c_pow1.astype(BF16), x_prev, preferred_element_type=F32)
    return _gelu_tanh(y)


def _ssm_call(arow, acol, ldt, bt, cab, dcol, e, chunks_per_seq):
    T, _, NC = e.shape
    G = arow.shape[0]
    gps = SSM_GROUPS_PER_STEP
    assert G % gps == 0
    grp = lambda a: pl.BlockSpec((gps,) + a.shape[1:], lambda g: (g,) + (0,) * (a.ndim - 1))
    channels = pl.BlockSpec((T, gps * SSM_GROUP_CH, NC), lambda g: (0, g, 0))
    return pl.pallas_call(
        functools.partial(_ssm_kernel, chunks_per_seq=chunks_per_seq),
        grid=(G // gps,),
        in_specs=[grp(a) for a in (arow, acol, ldt, bt, cab, dcol)] + [channels],
        out_specs=channels,
        out_shape=jax.ShapeDtypeStruct((T, SSM_WIDTH, NC), F32),
        scratch_shapes=[pltpu.VMEM((gps, 2 * T * SSM_GROUP_CH, LANES), F32)],
        compiler_params=pltpu.CompilerParams(
            dimension_semantics=("arbitrary",), vmem_limit_bytes=VMEM_LIMIT),
        name="ssm",
    )(arow, acol, ldt, bt, cab, dcol, e)


def _ssm_param_layouts(a_re, a_im, log_dt, b_re, b_im, c_re, c_im, d_skip):
    G = a_re.shape[0]
    arow = jnp.stack([jnp.concatenate([a_re, a_re], -1), jnp.concatenate([a_im, a_im], -1)], 1)
    acol = jnp.stack([a_re, a_im], -1)
    reps = LANES // SSM_GROUP_CH
    bt = jnp.stack([jnp.tile(b_re, (1, 1, reps)), jnp.tile(b_im, (1, 1, reps))], 1)
    cab = jnp.stack([jnp.concatenate([c_re, -c_im], -1), jnp.concatenate([-c_im, -c_re], -1)], 1)
    return (arow.astype(F32), acol.astype(F32), log_dt.reshape(G, 1, 1).astype(F32),
            bt.astype(F32), cab.astype(F32), d_skip.reshape(G, SSM_GROUP_CH, 1).astype(F32))


def _tail_kernel(h1_ref, attn_ref, y_ref, p_ref, wglu_ref, bglu_ref, ga_ref, gs_ref,
                 wo_ref, g2_ref, w1_ref, w3_ref, w2_ref, gp_ref, wpg_ref,
                 wpp_ref, gf_ref, o_ref):
    y = y_ref[...]
    glu = y * _sigmoid(jnp.dot(y.astype(BF16), wglu_ref[...], preferred_element_type=F32)
                       + bglu_ref[...])
    an = _rms(attn_ref[...], ga_ref[...]).astype(BF16)
    sn = _rms(glu, gs_ref[...]).astype(BF16)
    h = (h1_ref[...] + jnp.dot(an, wo_ref[:ATTN_WIDTH, :], preferred_element_type=F32)
         + jnp.dot(sn, wo_ref[ATTN_WIDTH:, :], preferred_element_type=F32))
    h = h + 0.5 * _swiglu(_rms(h, g2_ref[...]).astype(BF16), w1_ref, w3_ref, w2_ref)
    gate = _sigmoid(jnp.dot(_rms(h, gp_ref[...]).astype(BF16), wpg_ref[...],
                            preferred_element_type=F32))
    h = h + gate * jnp.dot(p_ref[...].astype(BF16), wpp_ref[...], preferred_element_type=F32)
    o_ref[...] = _rms(h, gf_ref[...])


def _tail_call(h1, attn, y, p, *consts):
    B, L, D = h1.shape
    tm = TOKEN_TILE
    tile = lambda w: pl.BlockSpec((None, tm, w), lambda b, i: (b, i, 0))
    return pl.pallas_call(
        _tail_kernel,
        grid=(B, L // tm),
        in_specs=[tile(D), tile(ATTN_WIDTH), tile(SSM_WIDTH), tile(PLE_DIM)]
                 + [_const_spec(c.shape) for c in consts],
        out_specs=tile(D),
        out_shape=jax.ShapeDtypeStruct((B, L, D), F32),
        compiler_params=pltpu.CompilerParams(
            dimension_semantics=("arbitrary", "arbitrary"),
            vmem_limit_bytes=VMEM_LIMIT),
        name="tail",
    )(h1, attn, y, p, *consts)


def kernel(x, p, g_ffn1, w1_a, w3_a, w2_a, g_mix, w_in, b_f, a_re, a_im, log_dt, b_re, b_im, c_re, c_im, d_skip, w_glu, b_glu, g_attn_out, g_ssm_out, w_out, g_ffn2, w1_b, w3_b, w2_b, g_ple, w_ple_gate, w_ple_proj, g_final):
    B, L, D = x.shape
    assert D == D_MODEL and L % ATTN_TILE == 0 and L % TOKEN_TILE == 0 and L % SSM_CHUNK == 0
    assert g_ffn1.shape[0] == 1, "single layer"
    assert TOKEN_TILE == ATTN_TILE, "decay bias rows are relative to the kv tile start"
    row = lambda g: g.reshape(1, -1).astype(F32)
    bf = lambda w: w.astype(BF16)
    w_in16 = bf(w_in[0])
    assert w_in16.shape[1] == 3 * ATTN_WIDTH + ATTN_HEADS + SSM_WIDTH

    (h1, q, kt, v, s_in, ct, k_norm2, c_next), (w1_b16, w3_b16, w2_b16, w_out16, w_gate16) = _head_call(
        x, row(g_ffn1[0]), bf(w1_a[0]), bf(w3_a[0]), bf(w2_a[0]), row(g_mix[0]),
        w_in16, w_in16[:, 3 * ATTN_WIDTH + ATTN_HEADS:],
        b_f[0].reshape(ATTN_HEADS, 1).astype(F32),
        later=tuple(w.astype(F32) for w in (w1_b[0], w3_b[0], w2_b[0], w_out[0], w_ple_gate[0])))

    T = SSM_CHUNK
    chunks_per_seq = L // T
    n_chunks = B * chunks_per_seq
    e = s_in.reshape(n_chunks, T, SSM_WIDTH).transpose(1, 2, 0)
    f = _ssm_call(*_ssm_param_layouts(a_re[0], a_im[0], log_dt[0], b_re[0], b_im[0],
                                      c_re[0], c_im[0], d_skip[0]),
                  e, chunks_per_seq)
    y = f.transpose(2, 0, 1).reshape(B, L, SSM_WIDTH)

    attn = _attn_call(q, kt, v, ct, k_norm2, c_next)

    return _tail_call(
        h1, attn, y, p[0],
        bf(w_glu[0]), row(b_glu[0]), row(g_attn_out[0]), row(g_ssm_out[0]),
        w_out16, row(g_ffn2[0]),
        w1_b16, w3_b16, w2_b16, row(g_ple[0]), w_gate16,
        bf(w_ple_proj[0]), row(g_final))
```

```python
import functools
import math

import jax
import jax.numpy as jnp
from jax import lax
from jax.experimental import pallas as pl
from jax.experimental.pallas import tpu as pltpu

D_MODEL = 1024
ATTN_HEADS = 8
HEAD_DIM = 64
ATTN_WIDTH = ATTN_HEADS * HEAD_DIM
SSM_WIDTH = D_MODEL - ATTN_WIDTH
SSM_GROUP_CH = 16
SSM_GROUPS = SSM_WIDTH // SSM_GROUP_CH
SSM_STATE = 64
D_FF = 2816
PLE_DIM = 256
EPS = 1e-6

LANES = 128
BF16_SUBLANES = 16
HEAD_PAIR = 2 * HEAD_DIM
N_PAIRS = ATTN_HEADS // 2
FF_CHUNK = 256
TOKEN_TILE = 512
ATTN_TILE = 512
SSM_CHUNK = 32
SSM_GROUPS_PER_STEP = 2
NEG_BIG = -1e30
SKIP_LOG2 = 140.0
BOUND_SLACK_MUL = 1.001
BOUND_SLACK_ADD = 1.0
LOG2E = math.log2(math.e)
N_BIAS = 3
BIAS_ROWS = 8
VMEM_LIMIT = 56 * 1024 * 1024

BF16 = jnp.bfloat16
F32 = jnp.float32


def _rms(x, g):
    ms = jnp.mean(x * x, axis=-1, keepdims=True)
    return x * lax.rsqrt(ms + EPS) * g


def _rms_unit(x):
    return x * lax.rsqrt(jnp.mean(x * x, axis=-1, keepdims=True) + EPS)


def _sigmoid(x):
    return 1.0 / (1.0 + jnp.exp(-x))


def _swiglu(xn, w1_ref, w3_ref, w2_ref):
    acc = None
    for c in range(D_FF // FF_CHUNK):
        sl = slice(c * FF_CHUNK, (c + 1) * FF_CHUNK)
        a = jnp.dot(xn, w1_ref[:, sl], preferred_element_type=F32)
        b = jnp.dot(xn, w3_ref[:, sl], preferred_element_type=F32)
        gated = (a * _sigmoid(a) * b).astype(BF16)
        part = jnp.dot(gated, w2_ref[sl, :], preferred_element_type=F32)
        acc = part if acc is None else acc + part
    return acc


def _const_spec(shape):
    nd = len(shape)
    return pl.BlockSpec(shape, lambda *_: (0,) * nd, pipeline_mode=pl.Buffered(1))


def _head_kernel(*refs, n_later):
    (x_ref, w1_ref, w3_ref, w2_ref, win_ref, ws_ref, bf_ref) = refs[:7]
    later_in = refs[7:7 + n_later]
    later_gain = refs[7 + n_later:7 + 2 * n_later]
    h1_ref, q_ref, kt_ref, v_ref, s_ref, ct_ref, kn_ref, cn_ref = (
        refs[7 + 2 * n_later:15 + 2 * n_later])
    later_out = refs[15 + 2 * n_later:15 + 3 * n_later]
    carry_ref = refs[-1]
    @pl.when(pl.program_id(1) == 0)
    def _():
        carry_ref[...] = jnp.zeros_like(carry_ref)
        kn_ref[...] = jnp.zeros_like(kn_ref)
        cn_ref[...] = jnp.zeros_like(cn_ref)

    for src, gain, dst in zip(later_in, later_gain, later_out):
        dst[...] = (src[...] * gain[...]).astype(BF16)
    tm = x_ref.shape[0]
    x = x_ref[...]
    h1 = x + 0.5 * _swiglu(_rms_unit(x).astype(BF16), w1_ref, w3_ref, w2_ref)
    h1_ref[...] = h1
    un = _rms_unit(h1).astype(BF16)
    project = lambda w: jnp.dot(un, w, preferred_element_type=F32)
    zf = project(win_ref[:, 3 * ATTN_WIDTH:3 * ATTN_WIDTH + LANES])
    kv = project(win_ref[:, ATTN_WIDTH:3 * ATTN_WIDTH])
    zft = zf.T[:ATTN_HEADS, :] + bf_ref[...]
    logf = jnp.minimum(zft, 0.0) - jnp.log1p(jnp.exp(-jnp.abs(zft)))
    lane = lax.broadcasted_iota(jnp.int32, logf.shape, 1)
    c = logf
    shift = 1
    while shift < tm:
        c = c + jnp.where(lane >= shift, pltpu.roll(c, shift, 1), 0.0)
        shift *= 2

    c_abs = c + carry_ref[:, 0:1]
    ct_ref[...] = c_abs * LOG2E
    carry_ref[...] = jnp.broadcast_to(c_abs[:, tm - 1:tm], carry_ref.shape)

    rel = (c - c[:, 0:1]) * LOG2E
    hi = rel.astype(BF16).astype(F32)
    mid = (rel - hi).astype(BF16).astype(F32)
    lo = (rel - hi - mid).astype(BF16).astype(F32)
    kt = kv[:, :ATTN_WIDTH].astype(BF16).astype(F32).T
    vv = kv[:, ATTN_WIDTH:]
    k_sq = (kt * kt).reshape(ATTN_HEADS, HEAD_DIM, tm)
    tile = pl.program_id(1)
    stat_lane = lax.broadcasted_iota(jnp.int32, kn_ref.shape, 1)
    kn_ref[...] = jnp.where(stat_lane == tile,
                            jnp.max(jnp.sum(k_sq, axis=1), axis=-1, keepdims=True), kn_ref[...])
    cn_ref[...] = jnp.where(stat_lane == tile - 1, c_abs[:, 0:1] * LOG2E, cn_ref[...])
    sub = lax.broadcasted_iota(jnp.int32, (BIAS_ROWS, tm), 0)
    zeros = jnp.zeros((HEAD_DIM - BIAS_ROWS, tm), F32)
    vlane = lax.broadcasted_iota(jnp.int32, (tm, HEAD_PAIR), 1)
    for h in range(ATTN_HEADS):
        bias = jnp.where(sub == 0, -hi[h:h + 1],
                         jnp.where(sub == 1, -mid[h:h + 1],
                                   jnp.where(sub == 2, -lo[h:h + 1], 0.0)))
        k_h = kt[h * HEAD_DIM:(h + 1) * HEAD_DIM]
        vp = vv[:, (h // 2) * HEAD_PAIR:(h // 2 + 1) * HEAD_PAIR]
        if h % 2 == 0:
            kt_ref[h] = jnp.concatenate([k_h, bias, zeros], axis=0).astype(BF16)
            v_ref[h] = jnp.where(vlane < HEAD_DIM, vp,
                                 jnp.where(vlane == HEAD_DIM, 1.0, 0.0)).astype(BF16)
        else:
            kt_ref[h] = jnp.concatenate([bias, zeros, k_h], axis=0).astype(BF16)
            v_ref[h] = jnp.where(vlane >= HEAD_DIM, vp,
                                 jnp.where(vlane == 0, 1.0, 0.0)).astype(BF16)

    q_ref[...] = (project(win_ref[:, :ATTN_WIDTH]) * (LOG2E / math.sqrt(HEAD_DIM))).astype(BF16)
    s_ref[...] = project(ws_ref[...]).astype(BF16)


def _slab_spec(w, n_steps, steps_per_batch, rows_like=None):
    n_rows = (w if rows_like is None else rows_like).shape[0]
    rows = next(r for r in range(BF16_SUBLANES, n_rows + 1, BF16_SUBLANES)
                if n_rows % r == 0 and n_rows // r <= n_steps)
    last = n_rows // rows - 1
    return pl.BlockSpec((rows, w.shape[1]),
                        lambda b, i: (jnp.minimum(b * steps_per_batch + i, last), 0))


def _head_call(x, w1, w3, w2, w_in, ws, bf, later, later_gain):
    B, L, D = x.shape
    tm = TOKEN_TILE
    tile = lambda w: pl.BlockSpec((None, tm, w), lambda b, i: (b, i, 0))
    slabs = [_slab_spec(w, B * (L // tm), L // tm) for w in later]
    gain_slabs = [_slab_spec(g, B * (L // tm), L // tm, rows_like=w) for w, g in zip(later, later_gain)]
    out_shape = (
        jax.ShapeDtypeStruct((B, L, D), F32),
        jax.ShapeDtypeStruct((B, L, ATTN_WIDTH), BF16),
        jax.ShapeDtypeStruct((B, ATTN_HEADS, HEAD_PAIR, L), BF16),
        jax.ShapeDtypeStruct((B, ATTN_HEADS, L, HEAD_PAIR), BF16),
        jax.ShapeDtypeStruct((B, L, SSM_WIDTH), BF16),
        jax.ShapeDtypeStruct((B, ATTN_HEADS, L), F32),
        jax.ShapeDtypeStruct((B, ATTN_HEADS, LANES), F32),
        jax.ShapeDtypeStruct((B, ATTN_HEADS, LANES), F32),
    ) + tuple(jax.ShapeDtypeStruct(w.shape, BF16) for w in later)
    outs = pl.pallas_call(
        functools.partial(_head_kernel, n_later=len(later)),
        grid=(B, L // tm),
        in_specs=[tile(D), _const_spec(w1.shape), _const_spec(w3.shape), _const_spec(w2.shape),
                  _const_spec(w_in.shape), _const_spec(ws.shape), _const_spec(bf.shape)]
                 + slabs + gain_slabs,
        out_specs=(tile(D), tile(ATTN_WIDTH),
                   pl.BlockSpec((None, ATTN_HEADS, HEAD_PAIR, tm), lambda b, i: (b, 0, 0, i)),
                   pl.BlockSpec((None, ATTN_HEADS, tm, HEAD_PAIR), lambda b, i: (b, 0, i, 0)),
                   tile(SSM_WIDTH),
                   pl.BlockSpec((None, ATTN_HEADS, tm), lambda b, i: (b, 0, i)),
                   pl.BlockSpec((None, ATTN_HEADS, LANES), lambda b, i: (b, 0, 0)),
                   pl.BlockSpec((None, ATTN_HEADS, LANES), lambda b, i: (b, 0, 0)))
                  + tuple(slabs),
        out_shape=out_shape,
        scratch_shapes=[pltpu.VMEM((ATTN_HEADS, LANES), F32)],
        compiler_params=pltpu.CompilerParams(
            dimension_semantics=("arbitrary", "arbitrary"),
            vmem_limit_bytes=VMEM_LIMIT),
        name="head",
    )(x, w1, w3, w2, w_in, ws, bf, *later, *later_gain)
    return outs[:8], outs[8:]


def _attn_kernel(q_ref, *refs):
    _attn_prepare(0, q_ref, *refs)
    _attn_first_stage(0, q_ref, *refs)

    def q_tile(n, carry):
        _attn_q_tile(n, q_ref, *refs)
        return carry

    lax.fori_loop(0, q_ref.shape[0] // ATTN_TILE, q_tile, 0)


def _tile_start(i):
    return pl.multiple_of(i * ATTN_TILE, ATTN_TILE)


def _q_heads(q):
    lane = lax.broadcasted_iota(jnp.int32, (1, HEAD_PAIR), 1)
    first = lane < HEAD_DIM
    ones_even = jnp.where((lane >= HEAD_DIM) & (lane < HEAD_DIM + N_BIAS), 1.0, 0.0).astype(BF16)
    ones_odd = jnp.where(lane < N_BIAS, 1.0, 0.0).astype(BF16)
    return first, (jnp.where(first, q, ones_even), jnp.where(first, ones_odd, q))


def _attn_prepare(n, q_ref, kt_ref, v_ref, c_ref, kn_ref, cn_ref, o_ref, s_e0, s_e1, s_o0, s_o1, m_ref,
                  acc_ref, count_ref):
    t = ATTN_TILE
    q = q_ref[pl.ds(_tile_start(n), t), :]
    first, q_heads = _q_heads(q)
    q_sq = q.astype(F32) * q.astype(F32)
    row = lax.broadcasted_iota(jnp.int32, (t, t), 0)
    col = lax.broadcasted_iota(jnp.int32, (t, t), 1)
    tile_id = lax.broadcasted_iota(jnp.int32, (1, LANES), 1)
    needed = tile_id < 0
    for h, s_ref in enumerate((s_o0, s_o1)):
        s = jnp.where(col <= row, jnp.dot(q_heads[h], kt_ref[h, :, pl.ds(_tile_start(n), t)],
                                          preferred_element_type=F32), NEG_BIG)
        s_ref[...] = s
        m = jnp.max(s, axis=-1, keepdims=True)
        m_ref[h] = jnp.broadcast_to(m, (t, LANES))
        q_norm2 = jnp.max(jnp.sum(jnp.where(first == (h == 0), q_sq, 0.0), axis=-1, keepdims=True),
                          axis=0, keepdims=True)
        k_norm2, c_next = kn_ref[h:h + 1, :], cn_ref[h:h + 1, :]
        c_q = c_ref[h:h + 1, pl.ds(_tile_start(n), LANES)][:, 0:1]
        reach = jnp.sqrt(q_norm2 * k_norm2) * BOUND_SLACK_MUL + BOUND_SLACK_ADD + c_q - c_next
        needed = needed | (reach - jnp.min(m, axis=0, keepdims=True) >= -SKIP_LOG2)
    first_needed = jnp.min(jnp.where(needed & (tile_id < n), tile_id, n).astype(F32))
    count_ref[0] = n - first_needed.astype(jnp.int32)


def _attn_first_stage(n, q_ref, kt_ref, v_ref, c_ref, kn_ref, cn_ref, o_ref, s_e0, s_e1, s_o0, s_o1,
                      m_ref, acc_ref, count_ref):
    t = ATTN_TILE
    _, q_heads = _q_heads(q_ref[pl.ds(_tile_start(n), t), :])
    older = _tile_start(jnp.maximum(n - 1, 0))
    for h, s_ref in enumerate((s_e0, s_e1)):
        s_ref[...] = jnp.dot(q_heads[h], kt_ref[h, :, pl.ds(older, t)], preferred_element_type=F32)
    for h, s_ref in enumerate((s_o0, s_o1)):
        acc_ref[h] = jnp.dot(jnp.exp2(s_ref[...] - jnp.tile(m_ref[h], (1, t // LANES))).astype(BF16),
                             v_ref[h, pl.ds(_tile_start(n), t), :], preferred_element_type=F32)


def _attn_q_tile(n, q_ref, kt_ref, v_ref, c_ref, kn_ref, cn_ref, o_ref, s_e0, s_e1, s_o0, s_o1, m_ref,
                 acc_ref, count_ref):
    t = ATTN_TILE
    count = count_ref[0]
    start = _tile_start
    first, q_heads = _q_heads(q_ref[pl.ds(start(n), t), :])
    s_buf = ((s_e0, s_e1), (s_o0, s_o1))
    c_q = [c_ref[h:h + 1, pl.ds(start(n), LANES)][:, 0:1] for h in range(2)]

    def qk(h, kv):
        return jnp.dot(q_heads[h], kt_ref[h, :, pl.ds(start(kv), t)], preferred_element_type=F32)

    def consume(h, s, kv):
        d = c_ref[h:h + 1, pl.ds(start(kv), LANES)][:, 0:1] - c_q[h]
        m_old = m_ref[h]
        m_new = jnp.maximum(m_old, jnp.max(s, axis=-1, keepdims=True) - d)
        p = jnp.exp2(s - jnp.tile(m_new + d, (1, t // LANES))).astype(BF16)
        m_ref[h] = m_new
        acc_ref[h] = jnp.exp2(m_old - m_new) * acc_ref[h] + jnp.dot(
            p, v_ref[h, pl.ds(start(kv), t), :], preferred_element_type=F32)

    def step(kv, par):
        for h in range(2):
            s_buf[1 - par][h][...] = qk(h, kv - 1)
        for h in range(2):
            consume(h, s_buf[par][h][...], kv)

    def pair(i, carry):
        step(n - 1 - 2 * i, 0)
        step(n - 2 - 2 * i, 1)
        return carry

    n_pairs = jnp.maximum(count - 1, 0) // 2
    lax.fori_loop(0, n_pairs, pair, 0)
    left = count - 2 * n_pairs
    last = n - count

    def finish():
        acc0, acc1 = acc_ref[0], acc_ref[1]
        o_ref[pl.ds(start(n), t), :] = jnp.where(first, acc0 / acc0[:, HEAD_DIM:HEAD_DIM + 1],
                                                 acc1 / acc1[:, 0:1])
        nxt = jnp.minimum(n + 1, q_ref.shape[0] // t - 1)
        rest = (q_ref, kt_ref, v_ref, c_ref, kn_ref, cn_ref, o_ref, s_e0, s_e1, s_o0, s_o1, m_ref,
                acc_ref, count_ref)
        _attn_prepare(nxt, *rest)
        _attn_first_stage(nxt, *rest)

    @pl.when(left == 2)
    def _():
        step(last + 1, 0)
        for h in range(2):
            consume(h, s_buf[1][h][...], last)
        finish()

    @pl.when(left == 1)
    def _():
        for h in range(2):
            consume(h, s_buf[0][h][...], last)
        finish()

    @pl.when(left == 0)
    def _():
        finish()


def _attn_call(q, kt, v, ct, k_norm2, c_next):
    B, L, _ = q.shape
    t = ATTN_TILE
    assert L // t <= LANES
    by_pair = lambda a: a.reshape(B, N_PAIRS, 2, a.shape[-1])
    pair_block = lambda *shape: pl.BlockSpec((None, 2) + shape, lambda b, p: (b, p, 0, 0))
    pair_rows = lambda w: pl.BlockSpec((None, None, 2, w), lambda b, p: (b, p, 0, 0))
    lanes_of_pair = pl.BlockSpec((None, L, HEAD_PAIR), lambda b, p: (b, 0, p))
    return pl.pallas_call(
        _attn_kernel,
        grid=(B, N_PAIRS),
        in_specs=[lanes_of_pair, pair_block(HEAD_PAIR, L), pair_block(L, HEAD_PAIR),
                  pair_rows(L), pair_rows(LANES), pair_rows(LANES)],
        out_specs=lanes_of_pair,
        out_shape=jax.ShapeDtypeStruct((B, L, ATTN_WIDTH), F32),
        scratch_shapes=[pltpu.VMEM((t, t), F32)] * 4
                       + [pltpu.VMEM((2, t, LANES), F32), pltpu.VMEM((2, t, HEAD_PAIR), F32),
                          pltpu.SMEM((1,), jnp.int32)],
        compiler_params=pltpu.CompilerParams(
            dimension_semantics=("arbitrary", "arbitrary"),
            vmem_limit_bytes=VMEM_LIMIT),
        name="attn",
    )(q, kt, v, by_pair(ct), by_pair(k_norm2), by_pair(c_next))


def _gelu_tanh(x):
    return 0.5 * x * (1.0 + jnp.tanh(math.sqrt(2.0 / math.pi) * (x + 0.044715 * (x * x * x))))


def _cis(mag_arg, ang):
    mag = jnp.exp(mag_arg)
    return mag * jnp.cos(ang), mag * jnp.sin(ang)


def _cmul(ar, ai, br, bi):
    return ar * br - ai * bi, ar * bi + ai * br


def _cpow2(zr, zi, n):
    assert n & (n - 1) == 0
    while n > 1:
        zr, zi = zr * zr - zi * zi, 2.0 * zr * zi
        n //= 2
    return zr, zi


def _ssm_kernel(arow_ref, acol_ref, ldt_ref, bt_ref, cab_ref, dcol_ref, e_ref, f_ref, z_ref,
                *, chunks_per_seq):
    H = SSM_GROUP_CH

    @pl.when(pl.program_id(0) == 0)
    def _():
        z_ref[:, 0:SSM_CHUNK * H, :] = jnp.zeros((z_ref.shape[0], SSM_CHUNK * H, LANES), F32)

    for gi in range(SSM_GROUPS_PER_STEP):
        y = _ssm_group(arow_ref.at[gi], acol_ref.at[gi], ldt_ref.at[gi], bt_ref.at[gi], cab_ref.at[gi],
                       dcol_ref.at[gi], e_ref[:, gi * H:(gi + 1) * H, :], z_ref.at[gi], chunks_per_seq)
        f_ref[:, gi * H:(gi + 1) * H, :] = y.reshape(SSM_CHUNK, H, y.shape[-1])


def _ssm_group(arow_ref, acol_ref, ldt_ref, bt_ref, cab_ref, dcol_ref, e, z_ref, chunks_per_seq):
    T, P, H = SSM_CHUNK, SSM_STATE, SSM_GROUP_CH
    TH = T * H
    hi = lax.Precision.HIGHEST
    dt = jnp.exp(ldt_ref[...])

    lam_r, lam_i = dt * arow_ref[0:1, :], dt * arow_ref[1:2, :]
    j0 = lax.broadcasted_iota(jnp.int32, (T, 2 * P), 0).astype(F32)
    pa0, pb0 = _cis(j0 * lam_r, j0 * lam_i)
    pa1, pb1 = _cmul(pa0, pb0, *_cis(lam_r, lam_i))
    over_h = lambda a: jnp.concatenate(
        [jnp.broadcast_to(a[j:j + 1, :], (H, 2 * P)) for j in range(T)], axis=0)
    ca, cb = jnp.tile(cab_ref[0], (T, 1)), jnp.tile(cab_ref[1], (T, 1))
    c_pow0 = over_h(pa0) * ca + over_h(pb0) * cb
    c_pow1 = over_h(pa1) * ca + over_h(pb1) * cb

    a_r, a_i = acol_ref[:, 0:1], acol_ref[:, 1:2]
    lr, li = dt * a_r, dt * a_i
    abar_r, abar_i = _cis(lr, li)
    nr, ni = abar_r - 1.0, abar_i
    den = a_r * a_r + a_i * a_i
    fr, fi = (nr * a_r + ni * a_i) / den, (ni * a_r - nr * a_i) / den
    b_r, b_i = bt_ref[0], bt_ref[1]
    bb_r, bb_i = fr * b_r - fi * b_i, fr * b_i + fi * b_r

    kcol = jnp.dot(c_pow0, jnp.concatenate([bb_r, bb_i], axis=0), precision=hi,
                   preferred_element_type=F32)
    lane_h = lax.broadcasted_iota(jnp.int32, (H, LANES), 1) % H
    skip = jnp.where(lane_h == lax.broadcasted_iota(jnp.int32, (H, LANES), 0), dcol_ref[...], 0.0)

    z_ref[TH:2 * TH, :] = kcol
    z_ref[TH:TH + H, :] = kcol[:H] + skip
    lane_group = lax.broadcasted_iota(jnp.int32, (1, LANES), 1) // H
    groups_per_block = LANES // H
    blocks = []
    for v in range(TH // LANES):
        blk = None
        for u in range(groups_per_block):
            s = v * groups_per_block + u
            piece = z_ref[TH - H * s:2 * TH - H * s, :]
            blk = piece if blk is None else jnp.where(lane_group == u, piece, blk)
        blocks.append(blk.astype(BF16))
    mt = jnp.concatenate(blocks, axis=1)

    expo = (groups_per_block - 1 - lane_group).astype(F32)
    wr, wi = _cis(lr * expo, li * expo)
    hop_r, hop_i = _cpow2(abar_r, abar_i, groups_per_block)
    w1_r, w1_i = [], []
    for v in range(TH // LANES):
        w1_r.insert(0, wr * bb_r - wi * bb_i)
        w1_i.insert(0, wr * bb_i + wi * bb_r)
        wr, wi = _cmul(wr, wi, hop_r, hop_i)
    w1t = jnp.concatenate([jnp.concatenate(w1_r, axis=1),
                           jnp.concatenate(w1_i, axis=1)], axis=0).astype(BF16)

    e = e.reshape(TH, e.shape[-1])
    y = jnp.dot(mt, e, preferred_element_type=F32)
    st = jnp.dot(w1t, e, preferred_element_type=F32)
    sr, si = st[:P], st[P:]
    pos = lax.broadcasted_iota(jnp.int32, sr.shape, 1) % chunks_per_seq

    def shifted(a, shift):
        return jnp.where(pos >= shift, pltpu.roll(a, shift, 1), 0.0)

    qr, qi = _cpow2(abar_r, abar_i, T)
    shift = 1
    while shift < chunks_per_seq:
        srs, sis = shifted(sr, shift), shifted(si, shift)
        sr, si = sr + qr * srs - qi * sis, si + qr * sis + qi * srs
        qr, qi = qr * qr - qi * qi, 2.0 * qr * qi
        shift *= 2
    x_prev = jnp.concatenate([shifted(sr, 1), shifted(si, 1)], axis=0).astype(BF16)
    y = y + jnp.dot(c_pow1.astype(BF16), x_prev, preferred_element_type=F32)
    return _gelu_tanh(y)


def _ssm_call(arow, acol, ldt, bt, cab, dcol, e, chunks_per_seq):
    T, _, NC = e.shape
    G = arow.shape[0]
    gps = SSM_GROUPS_PER_STEP
    assert G % gps == 0
    grp = lambda a: pl.BlockSpec((gps,) + a.shape[1:], lambda g: (g,) + (0,) * (a.ndim - 1))
    channels = pl.BlockSpec((T, gps * SSM_GROUP_CH, NC), lambda g: (0, g, 0))
    return pl.pallas_call(
        functools.partial(_ssm_kernel, chunks_per_seq=chunks_per_seq),
        grid=(G // gps,),
        in_specs=[grp(a) for a in (arow, acol, ldt, bt, cab, dcol)] + [channels],
        out_specs=channels,
        out_shape=jax.ShapeDtypeStruct((T, SSM_WIDTH, NC), F32),
        scratch_shapes=[pltpu.VMEM((gps, 2 * T * SSM_GROUP_CH, LANES), F32)],
        compiler_params=pltpu.CompilerParams(
            dimension_semantics=("arbitrary",), vmem_limit_bytes=VMEM_LIMIT),
        name="ssm",
    )(arow, acol, ldt, bt, cab, dcol, e)


def _ssm_param_layouts(a_re, a_im, log_dt, b_re, b_im, c_re, c_im, d_skip):
    G = a_re.shape[0]
    arow = jnp.stack([jnp.concatenate([a_re, a_re], -1), jnp.concatenate([a_im, a_im], -1)], 1)
    acol = jnp.stack([a_re, a_im], -1)
    reps = LANES // SSM_GROUP_CH
    bt = jnp.stack([jnp.tile(b_re, (1, 1, reps)), jnp.tile(b_im, (1, 1, reps))], 1)
    cab = jnp.stack([jnp.concatenate([c_re, -c_im], -1), jnp.concatenate([-c_im, -c_re], -1)], 1)
    return (arow.astype(F32), acol.astype(F32), log_dt.reshape(G, 1, 1).astype(F32),
            bt.astype(F32), cab.astype(F32), d_skip.reshape(G, SSM_GROUP_CH, 1).astype(F32))


def _tail_kernel(h1_ref, attn_ref, y_ref, p_ref, wglu_ref, bglu_ref,
                 wo_ref, w1_ref, w3_ref, w2_ref, wpg_ref, wpp_ref, gf_ref, o_ref):
    y = y_ref[...]
    glu = y * _sigmoid(jnp.dot(y.astype(BF16), wglu_ref[...], preferred_element_type=F32)
                       + bglu_ref[...])
    an = _rms_unit(attn_ref[...]).astype(BF16)
    sn = _rms_unit(glu).astype(BF16)
    h = (h1_ref[...] + jnp.dot(an, wo_ref[:ATTN_WIDTH, :], preferred_element_type=F32)
         + jnp.dot(sn, wo_ref[ATTN_WIDTH:, :], preferred_element_type=F32))
    h = h + 0.5 * _swiglu(_rms_unit(h).astype(BF16), w1_ref, w3_ref, w2_ref)
    gate = _sigmoid(jnp.dot(_rms_unit(h).astype(BF16), wpg_ref[...],
                            preferred_element_type=F32))
    h = h + gate * jnp.dot(p_ref[...].astype(BF16), wpp_ref[...], preferred_element_type=F32)
    o_ref[...] = _rms(h, gf_ref[...])


def _tail_call(h1, attn, y, p, *consts):
    B, L, D = h1.shape
    tm = TOKEN_TILE
    tile = lambda w: pl.BlockSpec((None, tm, w), lambda b, i: (b, i, 0))
    return pl.pallas_call(
        _tail_kernel,
        grid=(B, L // tm),
        in_specs=[tile(D), tile(ATTN_WIDTH), tile(SSM_WIDTH), tile(PLE_DIM)]
                 + [_const_spec(c.shape) for c in consts],
        out_specs=tile(D),
        out_shape=jax.ShapeDtypeStruct((B, L, D), F32),
        compiler_params=pltpu.CompilerParams(
            dimension_semantics=("arbitrary", "arbitrary"),
            vmem_limit_bytes=VMEM_LIMIT),
        name="tail",
    )(h1, attn, y, p, *consts)


def kernel(x, p, g_ffn1, w1_a, w3_a, w2_a, g_mix, w_in, b_f, a_re, a_im, log_dt, b_re, b_im, c_re, c_im, d_skip, w_glu, b_glu, g_attn_out, g_ssm_out, w_out, g_ffn2, w1_b, w3_b, w2_b, g_ple, w_ple_gate, w_ple_proj, g_final):
    B, L, D = x.shape
    assert D == D_MODEL and L % ATTN_TILE == 0 and L % TOKEN_TILE == 0 and L % SSM_CHUNK == 0
    assert g_ffn1.shape[0] == 1, "single layer"
    assert TOKEN_TILE == ATTN_TILE, "decay bias rows are relative to the kv tile start"
    row = lambda g: g.reshape(1, -1).astype(F32)
    bf = lambda w: w.astype(BF16)
    rows = lambda g: g.reshape(-1, 1).astype(F32)
    w_in16 = bf(w_in[0] * rows(g_mix[0]))
    assert w_in16.shape[1] == 3 * ATTN_WIDTH + ATTN_HEADS + SSM_WIDTH
    (h1, q, kt, v, s_in, ct, k_norm2, c_next), (w1_b16, w3_b16, w2_b16, w_out16, w_gate16) = _head_call(
        x, bf(w1_a[0] * rows(g_ffn1[0])), bf(w3_a[0] * rows(g_ffn1[0])), bf(w2_a[0]),
        w_in16, w_in16[:, 3 * ATTN_WIDTH + ATTN_HEADS:],
        b_f[0].reshape(ATTN_HEADS, 1).astype(F32),
        later=tuple(w.astype(F32) for w in (w1_b[0], w3_b[0], w2_b[0], w_out[0], w_ple_gate[0])),
        later_gain=(rows(g_ffn2[0]), rows(g_ffn2[0]), jnp.ones((D_FF, 1), F32),
                    rows(jnp.concatenate([g_attn_out[0], g_ssm_out[0]])), rows(g_ple[0])))

    T = SSM_CHUNK
    chunks_per_seq = L // T
    n_chunks = B * chunks_per_seq
    e = s_in.reshape(n_chunks, T, SSM_WIDTH).transpose(1, 2, 0)
    f = _ssm_call(*_ssm_param_layouts(a_re[0], a_im[0], log_dt[0], b_re[0], b_im[0],
                                      c_re[0], c_im[0], d_skip[0]),
                  e, chunks_per_seq)
    y = f.transpose(2, 0, 1).reshape(B, L, SSM_WIDTH)

    attn = _attn_call(q, kt, v, ct, k_norm2, c_next)

    return _tail_call(
        h1, attn, y, p[0],
        bf(w_glu[0]), row(b_glu[0]), w_out16, w1_b16, w3_b16, w2_b16, w_gate16,
        bf(w_ple_proj[0]), row(g_final))
```

```python
import functools
import math

import jax
import jax.numpy as jnp
from jax import lax
from jax.experimental import pallas as pl
from jax.experimental.pallas import tpu as pltpu

D_MODEL = 1024
ATTN_HEADS = 8
HEAD_DIM = 64
ATTN_WIDTH = ATTN_HEADS * HEAD_DIM
SSM_WIDTH = D_MODEL - ATTN_WIDTH
SSM_GROUP_CH = 16
SSM_GROUPS = SSM_WIDTH // SSM_GROUP_CH
SSM_STATE = 64
D_FF = 2816
PLE_DIM = 256
EPS = 1e-6

LANES = 128
BF16_SUBLANES = 16
HEAD_PAIR = 2 * HEAD_DIM
N_PAIRS = ATTN_HEADS // 2
FF_CHUNK = 256
TOKEN_TILE = 512
ATTN_TILE = 512
SSM_CHUNK = 32
SSM_GROUPS_PER_STEP = 2
NEG_BIG = -1e30
SKIP_LOG2 = 140.0
BOUND_SLACK_MUL = 1.001
BOUND_SLACK_ADD = 1.0
LOG2E = math.log2(math.e)
N_BIAS = 3
BIAS_ROWS = 8
VMEM_LIMIT = 56 * 1024 * 1024

BF16 = jnp.bfloat16
F32 = jnp.float32


def _rms(x, g):
    ms = jnp.mean(x * x, axis=-1, keepdims=True)
    return x * lax.rsqrt(ms + EPS) * g


def _sigmoid(x):
    return 1.0 / (1.0 + jnp.exp(-x))


def _swiglu(xn, w1_ref, w3_ref, w2_ref):
    acc = None
    for c in range(D_FF // FF_CHUNK):
        sl = slice(c * FF_CHUNK, (c + 1) * FF_CHUNK)
        a = jnp.dot(xn, w1_ref[:, sl], preferred_element_type=F32)
        b = jnp.dot(xn, w3_ref[:, sl], preferred_element_type=F32)
        gated = (a * _sigmoid(a) * b).astype(BF16)
        part = jnp.dot(gated, w2_ref[sl, :], preferred_element_type=F32)
        acc = part if acc is None else acc + part
    return acc


def _const_spec(shape):
    nd = len(shape)
    return pl.BlockSpec(shape, lambda *_: (0,) * nd, pipeline_mode=pl.Buffered(1))


def _head_kernel(*refs, n_later):
    (x_ref, g1_ref, w1_ref, w3_ref, w2_ref, gm_ref, win_ref, ws_ref, bf_ref) = refs[:9]
    later_in = refs[9:9 + n_later]
    h1_ref, q_ref, kt_ref, v_ref, s_ref, ct_ref, kn_ref, cn_ref = refs[9 + n_later:17 + n_later]
    later_out = refs[17 + n_later:17 + 2 * n_later]
    carry_ref = refs[-1]
    @pl.when(pl.program_id(1) == 0)
    def _():
        carry_ref[...] = jnp.zeros_like(carry_ref)
        kn_ref[...] = jnp.zeros_like(kn_ref)
        cn_ref[...] = jnp.zeros_like(cn_ref)

    for src, dst in zip(later_in, later_out):
        dst[...] = src[...].astype(BF16)
    tm = x_ref.shape[0]
    x = x_ref[...]
    h1 = x + 0.5 * _swiglu(_rms(x, g1_ref[...]).astype(BF16), w1_ref, w3_ref, w2_ref)
    h1_ref[...] = h1
    un = _rms(h1, gm_ref[...]).astype(BF16)
    project = lambda w: jnp.dot(un, w, preferred_element_type=F32)
    zf = project(win_ref[:, 3 * ATTN_WIDTH:3 * ATTN_WIDTH + LANES])
    kv = project(win_ref[:, ATTN_WIDTH:3 * ATTN_WIDTH])
    zft = zf.T[:ATTN_HEADS, :] + bf_ref[...]
    logf = jnp.minimum(zft, 0.0) - jnp.log1p(jnp.exp(-jnp.abs(zft)))
    lane = lax.broadcasted_iota(jnp.int32, logf.shape, 1)
    c = logf
    shift = 1
    while shift < tm:
        c = c + jnp.where(lane >= shift, pltpu.roll(c, shift, 1), 0.0)
        shift *= 2

    c_abs = c + carry_ref[:, 0:1]
    ct_ref[...] = c_abs * LOG2E
    carry_ref[...] = jnp.broadcast_to(c_abs[:, tm - 1:tm], carry_ref.shape)

    rel = (c - c[:, 0:1]) * LOG2E
    hi = rel.astype(BF16).astype(F32)
    mid = (rel - hi).astype(BF16).astype(F32)
    lo = (rel - hi - mid).astype(BF16).astype(F32)
    kt = kv[:, :ATTN_WIDTH].astype(BF16).astype(F32).T
    vv = kv[:, ATTN_WIDTH:]
    k_sq = (kt * kt).reshape(ATTN_HEADS, HEAD_DIM, tm)
    tile = pl.program_id(1)
    stat_lane = lax.broadcasted_iota(jnp.int32, kn_ref.shape, 1)

    kn_ref[...] = jnp.where(stat_lane == tile,
                            jnp.max(jnp.sum(k_sq, axis=1), axis=-1, keepdims=True), kn_ref[...])
    cn_ref[...] = jnp.where(stat_lane == tile - 1, c_abs[:, 0:1] * LOG2E, cn_ref[...])
    sub = lax.broadcasted_iota(jnp.int32, (BIAS_ROWS, tm), 0)
    zeros = jnp.zeros((HEAD_DIM - BIAS_ROWS, tm), F32)
    vlane = lax.broadcasted_iota(jnp.int32, (tm, HEAD_PAIR), 1)
    for h in range(ATTN_HEADS):
        bias = jnp.where(sub == 0, -hi[h:h + 1],
                         jnp.where(sub == 1, -mid[h:h + 1],
                                   jnp.where(sub == 2, -lo[h:h + 1], 0.0)))
        k_h = kt[h * HEAD_DIM:(h + 1) * HEAD_DIM]
        vp = vv[:, (h // 2) * HEAD_PAIR:(h // 2 + 1) * HEAD_PAIR]
        if h % 2 == 0:
            kt_ref[h] = jnp.concatenate([k_h, bias, zeros], axis=0).astype(BF16)
            v_ref[h] = jnp.where(vlane < HEAD_DIM, vp,
                                 jnp.where(vlane == HEAD_DIM, 1.0, 0.0)).astype(BF16)
        else:
            kt_ref[h] = jnp.concatenate([bias, zeros, k_h], axis=0).astype(BF16)
            v_ref[h] = jnp.where(vlane >= HEAD_DIM, vp,
                                 jnp.where(vlane == 0, 1.0, 0.0)).astype(BF16)

    q_ref[...] = (project(win_ref[:, :ATTN_WIDTH]) * (LOG2E / math.sqrt(HEAD_DIM))).astype(BF16)
    s_ref[...] = project(ws_ref[...]).astype(BF16)


def _slab_spec(w, n_steps, steps_per_batch):
    rows = next(r for r in range(BF16_SUBLANES, w.shape[0] + 1, BF16_SUBLANES)
                if w.shape[0] % r == 0 and w.shape[0] // r <= n_steps)
    last = w.shape[0] // rows - 1
    return pl.BlockSpec((rows, w.shape[1]),
                        lambda b, i: (jnp.minimum(b * steps_per_batch + i, last), 0))


def _head_call(x, g1, w1, w3, w2, gm, w_in, ws, bf, later):
    B, L, D = x.shape
    tm = TOKEN_TILE
    tile = lambda w: pl.BlockSpec((None, tm, w), lambda b, i: (b, i, 0))
    slabs = [_slab_spec(w, B * (L // tm), L // tm) for w in later]
    out_shape = (
        jax.ShapeDtypeStruct((B, L, D), F32),
        jax.ShapeDtypeStruct((B, L, ATTN_WIDTH), BF16),
        jax.ShapeDtypeStruct((B, ATTN_HEADS, HEAD_PAIR, L), BF16),
        jax.ShapeDtypeStruct((B, ATTN_HEADS, L, HEAD_PAIR), BF16),
        jax.ShapeDtypeStruct((B, L, SSM_WIDTH), BF16),
        jax.ShapeDtypeStruct((B, ATTN_HEADS, L), F32),
        jax.ShapeDtypeStruct((B, ATTN_HEADS, LANES), F32),
        jax.ShapeDtypeStruct((B, ATTN_HEADS, LANES), F32),
    ) + tuple(jax.ShapeDtypeStruct(w.shape, BF16) for w in later)
    outs = pl.pallas_call(
        functools.partial(_head_kernel, n_later=len(later)),
        grid=(B, L // tm),
        in_specs=[tile(D), _const_spec(g1.shape), _const_spec(w1.shape),
                  _const_spec(w3.shape), _const_spec(w2.shape), _const_spec(gm.shape),
                  _const_spec(w_in.shape), _const_spec(ws.shape), _const_spec(bf.shape)] + slabs,
        out_specs=(tile(D), tile(ATTN_WIDTH),
                   pl.BlockSpec((None, ATTN_HEADS, HEAD_PAIR, tm), lambda b, i: (b, 0, 0, i)),
                   pl.BlockSpec((None, ATTN_HEADS, tm, HEAD_PAIR), lambda b, i: (b, 0, i, 0)),
                   tile(SSM_WIDTH),
                   pl.BlockSpec((None, ATTN_HEADS, tm), lambda b, i: (b, 0, i)),
                   pl.BlockSpec((None, ATTN_HEADS, LANES), lambda b, i: (b, 0, 0)),
                   pl.BlockSpec((None, ATTN_HEADS, LANES), lambda b, i: (b, 0, 0)))
                  + tuple(slabs),
        out_shape=out_shape,
        scratch_shapes=[pltpu.VMEM((ATTN_HEADS, LANES), F32)],
        compiler_params=pltpu.CompilerParams(
            dimension_semantics=("arbitrary", "arbitrary"),
            vmem_limit_bytes=VMEM_LIMIT),
        name="head",
    )(x, g1, w1, w3, w2, gm, w_in, ws, bf, *later)
    return outs[:8], outs[8:]


def _attn_kernel(q_ref, *refs):
    _attn_prepare(0, q_ref, *refs)
    _attn_first_stage(0, q_ref, *refs)

    def q_tile(n, carry):
        _attn_q_tile(n, q_ref, *refs)
        return carry

    lax.fori_loop(0, q_ref.shape[0] // ATTN_TILE, q_tile, 0)


def _tile_start(i):
    return pl.multiple_of(i * ATTN_TILE, ATTN_TILE)


def _q_heads(q):
    lane = lax.broadcasted_iota(jnp.int32, (1, HEAD_PAIR), 1)
    first = lane < HEAD_DIM
    ones_even = jnp.where((lane >= HEAD_DIM) & (lane < HEAD_DIM + N_BIAS), 1.0, 0.0).astype(BF16)
    ones_odd = jnp.where(lane < N_BIAS, 1.0, 0.0).astype(BF16)
    return first, (jnp.where(first, q, ones_even), jnp.where(first, ones_odd, q))


def _attn_prepare(n, q_ref, kt_ref, v_ref, c_ref, kn_ref, cn_ref, o_ref, s_e0, s_e1, s_o0, s_o1, m_ref,
                  acc_ref, count_ref):
    t = ATTN_TILE
    q = q_ref[pl.ds(_tile_start(n), t), :]
    first, q_heads = _q_heads(q)
    q_sq = q.astype(F32) * q.astype(F32)
    row = lax.broadcasted_iota(jnp.int32, (t, t), 0)
    col = lax.broadcasted_iota(jnp.int32, (t, t), 1)
    tile_id = lax.broadcasted_iota(jnp.int32, (1, LANES), 1)
    needed = tile_id < 0
    for h, s_ref in enumerate((s_o0, s_o1)):
        s = jnp.where(col <= row, jnp.dot(q_heads[h], kt_ref[h, :, pl.ds(_tile_start(n), t)],
                                          preferred_element_type=F32), NEG_BIG)
        s_ref[...] = s
        m = jnp.max(s, axis=-1, keepdims=True)
        m_ref[h] = jnp.broadcast_to(m, (t, LANES))
        q_norm2 = jnp.max(jnp.sum(jnp.where(first == (h == 0), q_sq, 0.0), axis=-1, keepdims=True),
                          axis=0, keepdims=True)
        k_norm2, c_next = kn_ref[h:h + 1, :], cn_ref[h:h + 1, :]
        c_q = c_ref[h:h + 1, pl.ds(_tile_start(n), LANES)][:, 0:1]
        reach = jnp.sqrt(q_norm2 * k_norm2) * BOUND_SLACK_MUL + BOUND_SLACK_ADD + c_q - c_next
        needed = needed | (reach - jnp.min(m, axis=0, keepdims=True) >= -SKIP_LOG2)
    first_needed = jnp.min(jnp.where(needed & (tile_id < n), tile_id, n).astype(F32))
    count_ref[0] = n + 0 * first_needed.astype(jnp.int32)


def _attn_first_stage(n, q_ref, kt_ref, v_ref, c_ref, kn_ref, cn_ref, o_ref, s_e0, s_e1, s_o0, s_o1,
                      m_ref, acc_ref, count_ref):
    t = ATTN_TILE
    _, q_heads = _q_heads(q_ref[pl.ds(_tile_start(n), t), :])
    older = _tile_start(jnp.maximum(n - 1, 0))
    for h, s_ref in enumerate((s_e0, s_e1)):
        s_ref[...] = jnp.dot(q_heads[h], kt_ref[h, :, pl.ds(older, t)], preferred_element_type=F32)
    for h, s_ref in enumerate((s_o0, s_o1)):
        acc_ref[h] = jnp.dot(jnp.exp2(s_ref[...] - jnp.tile(m_ref[h], (1, t // LANES))).astype(BF16),
                             v_ref[h, pl.ds(_tile_start(n), t), :], preferred_element_type=F32)


def _attn_q_tile(n, q_ref, kt_ref, v_ref, c_ref, kn_ref, cn_ref, o_ref, s_e0, s_e1, s_o0, s_o1, m_ref,
                 acc_ref, count_ref):
    t = ATTN_TILE
    count = count_ref[0]
    start = _tile_start
    first, q_heads = _q_heads(q_ref[pl.ds(start(n), t), :])
    s_buf = ((s_e0, s_e1), (s_o0, s_o1))
    c_q = [c_ref[h:h + 1, pl.ds(start(n), LANES)][:, 0:1] for h in range(2)]

    def qk(h, kv):
        return jnp.dot(q_heads[h], kt_ref[h, :, pl.ds(start(kv), t)], preferred_element_type=F32)

    def consume(h, s, kv):
        d = c_ref[h:h + 1, pl.ds(start(kv), LANES)][:, 0:1] - c_q[h]
        m_old = m_ref[h]
        m_new = jnp.maximum(m_old, jnp.max(s, axis=-1, keepdims=True) - d)
        p = jnp.exp2(s - jnp.tile(m_new + d, (1, t // LANES))).astype(BF16)
        m_ref[h] = m_new
        acc_ref[h] = jnp.exp2(m_old - m_new) * acc_ref[h] + jnp.dot(
            p, v_ref[h, pl.ds(start(kv), t), :], preferred_element_type=F32)

    def step(kv, par):
        for h in range(2):
            s_buf[1 - par][h][...] = qk(h, kv - 1)
        for h in range(2):
            consume(h, s_buf[par][h][...], kv)

    def pair(i, carry):
        step(n - 1 - 2 * i, 0)
        step(n - 2 - 2 * i, 1)
        return carry

    n_pairs = jnp.maximum(count - 1, 0) // 2
    lax.fori_loop(0, n_pairs, pair, 0)
    left = count - 2 * n_pairs
    last = n - count

    def finish():
        acc0, acc1 = acc_ref[0], acc_ref[1]
        o_ref[pl.ds(start(n), t), :] = jnp.where(first, acc0 / acc0[:, HEAD_DIM:HEAD_DIM + 1],
                                                 acc1 / acc1[:, 0:1])
        nxt = jnp.minimum(n + 1, q_ref.shape[0] // t - 1)
        rest = (q_ref, kt_ref, v_ref, c_ref, kn_ref, cn_ref, o_ref, s_e0, s_e1, s_o0, s_o1, m_ref,
                acc_ref, count_ref)
        _attn_prepare(nxt, *rest)
        _attn_first_stage(nxt, *rest)

    @pl.when(left == 2)
    def _():
        step(last + 1, 0)
        for h in range(2):
            consume(h, s_buf[1][h][...], last)
        finish()

    @pl.when(left == 1)
    def _():
        for h in range(2):
            consume(h, s_buf[0][h][...], last)
        finish()

    @pl.when(left == 0)
    def _():
        finish()


def _attn_call(q, kt, v, ct, k_norm2, c_next):
    B, L, _ = q.shape
    t = ATTN_TILE
    assert L // t <= LANES
    by_pair = lambda a: a.reshape(B, N_PAIRS, 2, a.shape[-1])
    pair_block = lambda *shape: pl.BlockSpec((None, 2) + shape, lambda b, p: (b, p, 0, 0))
    pair_rows = lambda w: pl.BlockSpec((None, None, 2, w), lambda b, p: (b, p, 0, 0))
    lanes_of_pair = pl.BlockSpec((None, L, HEAD_PAIR), lambda b, p: (b, 0, p))
    return pl.pallas_call(
        _attn_kernel,
        grid=(B, N_PAIRS),
        in_specs=[lanes_of_pair, pair_block(HEAD_PAIR, L), pair_block(L, HEAD_PAIR),
                  pair_rows(L), pair_rows(LANES), pair_rows(LANES)],
        out_specs=lanes_of_pair,
        out_shape=jax.ShapeDtypeStruct((B, L, ATTN_WIDTH), F32),
        scratch_shapes=[pltpu.VMEM((t, t), F32)] * 4
                       + [pltpu.VMEM((2, t, LANES), F32), pltpu.VMEM((2, t, HEAD_PAIR), F32),
                          pltpu.SMEM((1,), jnp.int32)],
        compiler_params=pltpu.CompilerParams(
            dimension_semantics=("arbitrary", "arbitrary"),
            vmem_limit_bytes=VMEM_LIMIT),
        name="attn",
    )(q, kt, v, by_pair(ct), by_pair(k_norm2), by_pair(c_next))


def _gelu_tanh(x):
    return 0.5 * x * (1.0 + jnp.tanh(math.sqrt(2.0 / math.pi) * (x + 0.044715 * (x * x * x))))


def _cis(mag_arg, ang):
    mag = jnp.exp(mag_arg)
    return mag * jnp.cos(ang), mag * jnp.sin(ang)


def _cmul(ar, ai, br, bi):
    return ar * br - ai * bi, ar * bi + ai * br


def _cpow2(zr, zi, n):
    assert n & (n - 1) == 0
    while n > 1:
        zr, zi = zr * zr - zi * zi, 2.0 * zr * zi
        n //= 2
    return zr, zi


def _ssm_kernel(arow_ref, acol_ref, ldt_ref, bt_ref, cab_ref, dcol_ref, e_ref, f_ref, z_ref,
                *, chunks_per_seq):
    H = SSM_GROUP_CH

    @pl.when(pl.program_id(0) == 0)
    def _():
        z_ref[:, 0:SSM_CHUNK * H, :] = jnp.zeros((z_ref.shape[0], SSM_CHUNK * H, LANES), F32)

    for gi in range(SSM_GROUPS_PER_STEP):
        y = _ssm_group(arow_ref.at[gi], acol_ref.at[gi], ldt_ref.at[gi], bt_ref.at[gi], cab_ref.at[gi],
                       dcol_ref.at[gi], e_ref[:, gi * H:(gi + 1) * H, :], z_ref.at[gi], chunks_per_seq)
        f_ref[:, gi * H:(gi + 1) * H, :] = y.reshape(SSM_CHUNK, H, y.shape[-1])


def _ssm_group(arow_ref, acol_ref, ldt_ref, bt_ref, cab_ref, dcol_ref, e, z_ref, chunks_per_seq):
    T, P, H = SSM_CHUNK, SSM_STATE, SSM_GROUP_CH
    TH = T * H
    hi = lax.Precision.HIGHEST
    dt = jnp.exp(ldt_ref[...])

    lam_r, lam_i = dt * arow_ref[0:1, :], dt * arow_ref[1:2, :]
    j0 = lax.broadcasted_iota(jnp.int32, (T, 2 * P), 0).astype(F32)
    pa0, pb0 = _cis(j0 * lam_r, j0 * lam_i)
    pa1, pb1 = _cmul(pa0, pb0, *_cis(lam_r, lam_i))
    over_h = lambda a: jnp.concatenate(
        [jnp.broadcast_to(a[j:j + 1, :], (H, 2 * P)) for j in range(T)], axis=0)
    ca, cb = jnp.tile(cab_ref[0], (T, 1)), jnp.tile(cab_ref[1], (T, 1))
    c_pow0 = over_h(pa0) * ca + over_h(pb0) * cb
    c_pow1 = over_h(pa1) * ca + over_h(pb1) * cb

    a_r, a_i = acol_ref[:, 0:1], acol_ref[:, 1:2]
    lr, li = dt * a_r, dt * a_i
    abar_r, abar_i = _cis(lr, li)
    nr, ni = abar_r - 1.0, abar_i
    den = a_r * a_r + a_i * a_i
    fr, fi = (nr * a_r + ni * a_i) / den, (ni * a_r - nr * a_i) / den
    b_r, b_i = bt_ref[0], bt_ref[1]
    bb_r, bb_i = fr * b_r - fi * b_i, fr * b_i + fi * b_r

    kcol = jnp.dot(c_pow0, jnp.concatenate([bb_r, bb_i], axis=0), precision=hi,
                   preferred_element_type=F32)
    lane_h = lax.broadcasted_iota(jnp.int32, (H, LANES), 1) % H
    skip = jnp.where(lane_h == lax.broadcasted_iota(jnp.int32, (H, LANES), 0), dcol_ref[...], 0.0)

    z_ref[TH:2 * TH, :] = kcol
    z_ref[TH:TH + H, :] = kcol[:H] + skip
    lane_group = lax.broadcasted_iota(jnp.int32, (1, LANES), 1) // H
    groups_per_block = LANES // H
    blocks = []
    for v in range(TH // LANES):
        blk = None
        for u in range(groups_per_block):
            s = v * groups_per_block + u
            piece = z_ref[TH - H * s:2 * TH - H * s, :]
            blk = piece if blk is None else jnp.where(lane_group == u, piece, blk)
        blocks.append(blk.astype(BF16))
    mt = jnp.concatenate(blocks, axis=1)

    expo = (groups_per_block - 1 - lane_group).astype(F32)
    wr, wi = _cis(lr * expo, li * expo)
    hop_r, hop_i = _cpow2(abar_r, abar_i, groups_per_block)
    w1_r, w1_i = [], []
    for v in range(TH // LANES):
        w1_r.insert(0, wr * bb_r - wi * bb_i)
        w1_i.insert(0, wr * bb_i + wi * bb_r)
        wr, wi = _cmul(wr, wi, hop_r, hop_i)
    w1t = jnp.concatenate([jnp.concatenate(w1_r, axis=1),
                           jnp.concatenate(w1_i, axis=1)], axis=0).astype(BF16)

    e = e.reshape(TH, e.shape[-1])
    y = jnp.dot(mt, e, preferred_element_type=F32)
    st = jnp.dot(w1t, e, preferred_element_type=F32)
    sr, si = st[:P], st[P:]
    pos = lax.broadcasted_iota(jnp.int32, sr.shape, 1) % chunks_per_seq

    def shifted(a, shift):
        return jnp.where(pos >= shift, pltpu.roll(a, shift, 1), 0.0)

    qr, qi = _cpow2(abar_r, abar_i, T)
    shift = 1
    while shift < chunks_per_seq:
        srs, sis = shifted(sr, shift), shifted(si, shift)
        sr, si = sr + qr * srs - qi * sis, si + qr * sis + qi * srs
        qr, qi = qr * qr - qi * qi, 2.0 * qr * qi
        shift *= 2
    x_prev = jnp.concatenate([shifted(sr, 1), shifted(si, 1)], axis=0).astype(BF16)
    y = y + jnp.dot(c_pow1.astype(BF16), x_prev, preferred_element_type=F32)
    return _gelu_tanh(y)


def _ssm_call(arow, acol, ldt, bt, cab, dcol, e, chunks_per_seq):
    T, _, NC = e.shape
    G = arow.shape[0]
    gps = SSM_GROUPS_PER_STEP
    assert G % gps == 0
    grp = lambda a: pl.BlockSpec((gps,) + a.shape[1:], lambda g: (g,) + (0,) * (a.ndim - 1))
    channels = pl.BlockSpec((T, gps * SSM_GROUP_CH, NC), lambda g: (0, g, 0))
    return pl.pallas_call(
        functools.partial(_ssm_kernel, chunks_per_seq=chunks_per_seq),
        grid=(G // gps,),
        in_specs=[grp(a) for a in (arow, acol, ldt, bt, cab, dcol)] + [channels],
        out_specs=channels,
        out_shape=jax.ShapeDtypeStruct((T, SSM_WIDTH, NC), F32),
        scratch_shapes=[pltpu.VMEM((gps, 2 * T * SSM_GROUP_CH, LANES), F32)],
        compiler_params=pltpu.CompilerParams(
            dimension_semantics=("arbitrary",), vmem_limit_bytes=VMEM_LIMIT),
        name="ssm",
    )(arow, acol, ldt, bt, cab, dcol, e)


def _ssm_param_layouts(a_re, a_im, log_dt, b_re, b_im, c_re, c_im, d_skip):
    G = a_re.shape[0]
    arow = jnp.stack([jnp.concatenate([a_re, a_re], -1), jnp.concatenate([a_im, a_im], -1)], 1)
    acol = jnp.stack([a_re, a_im], -1)
    reps = LANES // SSM_GROUP_CH
    bt = jnp.stack([jnp.tile(b_re, (1, 1, reps)), jnp.tile(b_im, (1, 1, reps))], 1)
    cab = jnp.stack([jnp.concatenate([c_re, -c_im], -1), jnp.concatenate([-c_im, -c_re], -1)], 1)
    return (arow.astype(F32), acol.astype(F32), log_dt.reshape(G, 1, 1).astype(F32),
            bt.astype(F32), cab.astype(F32), d_skip.reshape(G, SSM_GROUP_CH, 1).astype(F32))


def _tail_kernel(h1_ref, attn_ref, y_ref, p_ref, wglu_ref, bglu_ref, ga_ref, gs_ref,
                 wo_ref, g2_ref, w1_ref, w3_ref, w2_ref, gp_ref, wpg_ref,
                 wpp_ref, gf_ref, o_ref):
    y = y_ref[...]
    glu = y * _sigmoid(jnp.dot(y.astype(BF16), wglu_ref[...], preferred_element_type=F32)
                       + bglu_ref[...])
    an = _rms(attn_ref[...], ga_ref[...]).astype(BF16)
    sn = _rms(glu, gs_ref[...]).astype(BF16)
    h = (h1_ref[...] + jnp.dot(an, wo_ref[:ATTN_WIDTH, :], preferred_element_type=F32)
         + jnp.dot(sn, wo_ref[ATTN_WIDTH:, :], preferred_element_type=F32))
    h = h + 0.5 * _swiglu(_rms(h, g2_ref[...]).astype(BF16), w1_ref, w3_ref, w2_ref)
    gate = _sigmoid(jnp.dot(_rms(h, gp_ref[...]).astype(BF16), wpg_ref[...],
                            preferred_element_type=F32))
    h = h + gate * jnp.dot(p_ref[...].astype(BF16), wpp_ref[...], preferred_element_type=F32)
    o_ref[...] = _rms(h, gf_ref[...])


def _tail_call(h1, attn, y, p, *consts):
    B, L, D = h1.shape
    tm = TOKEN_TILE
    tile = lambda w: pl.BlockSpec((None, tm, w), lambda b, i: (b, i, 0))
    return pl.pallas_call(
        _tail_kernel,
        grid=(B, L // tm),
        in_specs=[tile(D), tile(ATTN_WIDTH), tile(SSM_WIDTH), tile(PLE_DIM)]
                 + [_const_spec(c.shape) for c in consts],
        out_specs=tile(D),
        out_shape=jax.ShapeDtypeStruct((B, L, D), F32),
        compiler_params=pltpu.CompilerParams(
            dimension_semantics=("arbitrary", "arbitrary"),
            vmem_limit_bytes=VMEM_LIMIT),
        name="tail",
    )(h1, attn, y, p, *consts)


def kernel(x, p, g_ffn1, w1_a, w3_a, w2_a, g_mix, w_in, b_f, a_re, a_im, log_dt, b_re, b_im, c_re, c_im, d_skip, w_glu, b_glu, g_attn_out, g_ssm_out, w_out, g_ffn2, w1_b, w3_b, w2_b, g_ple, w_ple_gate, w_ple_proj, g_final):
    B, L, D = x.shape
    assert D == D_MODEL and L % ATTN_TILE == 0 and L % TOKEN_TILE == 0 and L % SSM_CHUNK == 0
    assert g_ffn1.shape[0] == 1, "single layer"
    assert TOKEN_TILE == ATTN_TILE, "decay bias rows are relative to the kv tile start"
    row = lambda g: g.reshape(1, -1).astype(F32)
    bf = lambda w: w.astype(BF16)
    w_in16 = bf(w_in[0])
    assert w_in16.shape[1] == 3 * ATTN_WIDTH + ATTN_HEADS + SSM_WIDTH

    (h1, q, kt, v, s_in, ct, k_norm2, c_next), (w1_b16, w3_b16, w2_b16, w_out16, w_gate16) = _head_call(
        x, row(g_ffn1[0]), bf(w1_a[0]), bf(w3_a[0]), bf(w2_a[0]), row(g_mix[0]),
        w_in16, w_in16[:, 3 * ATTN_WIDTH + ATTN_HEADS:],
        b_f[0].reshape(ATTN_HEADS, 1).astype(F32),
        later=tuple(w.astype(F32) for w in (w1_b[0], w3_b[0], w2_b[0], w_out[0], w_ple_gate[0])))

    T = SSM_CHUNK
    chunks_per_seq = L // T
    n_chunks = B * chunks_per_seq
    e = s_in.reshape(n_chunks, T, SSM_WIDTH).transpose(1, 2, 0)
    f = _ssm_call(*_ssm_param_layouts(a_re[0], a_im[0], log_dt[0], b_re[0], b_im[0],
                                      c_re[0], c_im[0], d_skip[0]),
                  e, chunks_per_seq)
    y = f.transpose(2, 0, 1).reshape(B, L, SSM_WIDTH)

    attn = _attn_call(q, kt, v, ct, k_norm2, c_next)

    return _tail_call(
        h1, attn, y, p[0],
        bf(w_glu[0]), row(b_glu[0]), row(g_attn_out[0]), row(g_ssm_out[0]),
        w_out16, row(g_ffn2[0]),
        w1_b16, w3_b16, w2_b16, row(g_ple[0]), w_gate16,
        bf(w_ple_proj[0]), row(g_final))
```

```python
import functools
import math

import jax
import jax.numpy as jnp
from jax import lax
from jax.experimental import pallas as pl
from jax.experimental.pallas import tpu as pltpu

D_MODEL = 1024
ATTN_HEADS = 8
HEAD_DIM = 64
ATTN_WIDTH = ATTN_HEADS * HEAD_DIM
SSM_WIDTH = D_MODEL - ATTN_WIDTH
SSM_GROUP_CH = 16
SSM_GROUPS = SSM_WIDTH // SSM_GROUP_CH
SSM_STATE = 64
D_FF = 2816
PLE_DIM = 256
EPS = 1e-6

LANES = 128
BF16_SUBLANES = 16
HEAD_PAIR = 2 * HEAD_DIM
N_PAIRS = ATTN_HEADS // 2
FF_CHUNK = 256
TOKEN_TILE = 512
WEIGHT_STAGE_ROWS = 128
ATTN_TILE = 512
SSM_CHUNK = 32
SSM_GROUPS_PER_STEP = 2
NEG_BIG = -1e30
SKIP_LOG2 = 140.0
BOUND_SLACK_MUL = 1.001
BOUND_SLACK_ADD = 1.0
LOG2E = math.log2(math.e)
N_BIAS = 3
BIAS_ROWS = 8
VMEM_LIMIT = 56 * 1024 * 1024

BF16 = jnp.bfloat16
F32 = jnp.float32


def _rms(x, g):
    ms = jnp.mean(x * x, axis=-1, keepdims=True)
    return x * lax.rsqrt(ms + EPS) * g


def _sigmoid(x):
    return 1.0 / (1.0 + jnp.exp(-x))


def _swiglu(xn, w1_ref, w3_ref, w2_ref):
    acc = None
    for c in range(D_FF // FF_CHUNK):
        sl = slice(c * FF_CHUNK, (c + 1) * FF_CHUNK)
        a = jnp.dot(xn, w1_ref[:, sl], preferred_element_type=F32)
        b = jnp.dot(xn, w3_ref[:, sl], preferred_element_type=F32)
        gated = (a * _sigmoid(a) * b).astype(BF16)
        part = jnp.dot(gated, w2_ref[sl, :], preferred_element_type=F32)
        acc = part if acc is None else acc + part
    return acc


def _const_spec(shape):
    nd = len(shape)
    return pl.BlockSpec(shape, lambda *_: (0,) * nd, pipeline_mode=pl.Buffered(1))


def _head_kernel(*refs, n_later):
    (x_ref, g1_ref, w1_hbm, w3_hbm, w2_hbm, gm_ref, win_ref, ws_ref, bf_ref) = refs[:9]
    later_in = refs[9:9 + n_later]
    h1_ref, q_ref, kt_ref, v_ref, s_ref, ct_ref, kn_ref, cn_ref = refs[9 + n_later:17 + n_later]
    later_out = refs[17 + n_later:17 + 2 * n_later]
    carry_ref, w1_ref, w3_ref, w2_ref, stage_a, stage_b, sem_a, sem_b = refs[-8:]

    @pl.when((pl.program_id(0) == 0) & (pl.program_id(1) == 0))
    def _():
        def stream(src, dst, stage, sem):
            rows = stage.shape[1]
            n_chunks = src.shape[0] // rows
            copy = lambda c: pltpu.make_async_copy(
                src.at[pl.ds(c * rows, rows), :], stage.at[c % 2], sem.at[c % 2])
            copy(0).start()
            for c in range(n_chunks):
                if c + 1 < n_chunks:
                    copy(c + 1).start()
                copy(c).wait()
                dst[pl.ds(c * rows, rows), :] = stage[c % 2].astype(BF16)
        stream(w1_hbm, w1_ref, stage_a, sem_a)
        stream(w3_hbm, w3_ref, stage_a, sem_a)
        stream(w2_hbm, w2_ref, stage_b, sem_b)

    @pl.when(pl.program_id(1) == 0)
    def _():
        carry_ref[...] = jnp.zeros_like(carry_ref)
        kn_ref[...] = jnp.zeros_like(kn_ref)
        cn_ref[...] = jnp.zeros_like(cn_ref)

    for src, dst in zip(later_in, later_out):
        dst[...] = src[...].astype(BF16)
    tm = x_ref.shape[0]
    x = x_ref[...]
    h1 = x + 0.5 * _swiglu(_rms(x, g1_ref[...]).astype(BF16), w1_ref, w3_ref, w2_ref)
    h1_ref[...] = h1
    un = _rms(h1, gm_ref[...]).astype(BF16)
    project = lambda w: jnp.dot(un, w, preferred_element_type=F32)
    zf = project(win_ref[:, 3 * ATTN_WIDTH:3 * ATTN_WIDTH + LANES])
    kv = project(win_ref[:, ATTN_WIDTH:3 * ATTN_WIDTH])
    zft = zf.T[:ATTN_HEADS, :] + bf_ref[...]
    logf = jnp.minimum(zft, 0.0) - jnp.log1p(jnp.exp(-jnp.abs(zft)))
    lane = lax.broadcasted_iota(jnp.int32, logf.shape, 1)
    c = logf
    shift = 1
    while shift < tm:
        c = c + jnp.where(lane >= shift, pltpu.roll(c, shift, 1), 0.0)
        shift *= 2

    c_abs = c + carry_ref[:, 0:1]
    ct_ref[...] = c_abs * LOG2E
    carry_ref[...] = jnp.broadcast_to(c_abs[:, tm - 1:tm], carry_ref.shape)

    rel = (c - c[:, 0:1]) * LOG2E
    hi = rel.astype(BF16).astype(F32)
    mid = (rel - hi).astype(BF16).astype(F32)
    lo = (rel - hi - mid).astype(BF16).astype(F32)
    kt = kv[:, :ATTN_WIDTH].astype(BF16).astype(F32).T
    vv = kv[:, ATTN_WIDTH:]
    k_sq = (kt * kt).reshape(ATTN_HEADS, HEAD_DIM, tm)
    tile = pl.program_id(1)
    stat_lane = lax.broadcasted_iota(jnp.int32, kn_ref.shape, 1)

    kn_ref[...] = jnp.where(stat_lane == tile,
                            jnp.max(jnp.sum(k_sq, axis=1), axis=-1, keepdims=True), kn_ref[...])
    cn_ref[...] = jnp.where(stat_lane == tile - 1, c_abs[:, 0:1] * LOG2E, cn_ref[...])
    sub = lax.broadcasted_iota(jnp.int32, (BIAS_ROWS, tm), 0)
    zeros = jnp.zeros((HEAD_DIM - BIAS_ROWS, tm), F32)
    vlane = lax.broadcasted_iota(jnp.int32, (tm, HEAD_PAIR), 1)
    for h in range(ATTN_HEADS):
        bias = jnp.where(sub == 0, -hi[h:h + 1],
                         jnp.where(sub == 1, -mid[h:h + 1],
                                   jnp.where(sub == 2, -lo[h:h + 1], 0.0)))
        k_h = kt[h * HEAD_DIM:(h + 1) * HEAD_DIM]
        vp = vv[:, (h // 2) * HEAD_PAIR:(h // 2 + 1) * HEAD_PAIR]
        if h % 2 == 0:
            kt_ref[h] = jnp.concatenate([k_h, bias, zeros], axis=0).astype(BF16)
            v_ref[h] = jnp.where(vlane < HEAD_DIM, vp,
                                 jnp.where(vlane == HEAD_DIM, 1.0, 0.0)).astype(BF16)
        else:
            kt_ref[h] = jnp.concatenate([bias, zeros, k_h], axis=0).astype(BF16)
            v_ref[h] = jnp.where(vlane >= HEAD_DIM, vp,
                                 jnp.where(vlane == 0, 1.0, 0.0)).astype(BF16)

    q_ref[...] = (project(win_ref[:, :ATTN_WIDTH]) * (LOG2E / math.sqrt(HEAD_DIM))).astype(BF16)
    s_ref[...] = project(ws_ref[...]).astype(BF16)


def _slab_spec(w, n_steps, steps_per_batch):
    rows = next(r for r in range(BF16_SUBLANES, w.shape[0] + 1, BF16_SUBLANES)
                if w.shape[0] % r == 0 and w.shape[0] // r <= n_steps)
    last = w.shape[0] // rows - 1
    return pl.BlockSpec((rows, w.shape[1]),
                        lambda b, i: (jnp.minimum(b * steps_per_batch + i, last), 0))


def _head_call(x, g1, w1, w3, w2, gm, w_in, ws, bf, later):
    B, L, D = x.shape
    tm = TOKEN_TILE
    tile = lambda w: pl.BlockSpec((None, tm, w), lambda b, i: (b, i, 0))
    slabs = [_slab_spec(w, B * (L // tm), L // tm) for w in later]
    out_shape = (
        jax.ShapeDtypeStruct((B, L, D), F32),
        jax.ShapeDtypeStruct((B, L, ATTN_WIDTH), BF16),
        jax.ShapeDtypeStruct((B, ATTN_HEADS, HEAD_PAIR, L), BF16),
        jax.ShapeDtypeStruct((B, ATTN_HEADS, L, HEAD_PAIR), BF16),
        jax.ShapeDtypeStruct((B, L, SSM_WIDTH), BF16),
        jax.ShapeDtypeStruct((B, ATTN_HEADS, L), F32),
        jax.ShapeDtypeStruct((B, ATTN_HEADS, LANES), F32),
        jax.ShapeDtypeStruct((B, ATTN_HEADS, LANES), F32),
    ) + tuple(jax.ShapeDtypeStruct(w.shape, BF16) for w in later)
    outs = pl.pallas_call(
        functools.partial(_head_kernel, n_later=len(later)),
        grid=(B, L // tm),
        in_specs=[tile(D), _const_spec(g1.shape), pl.BlockSpec(memory_space=pl.ANY),
                  pl.BlockSpec(memory_space=pl.ANY), pl.BlockSpec(memory_space=pl.ANY),
                  _const_spec(gm.shape),
                  _const_spec(w_in.shape), _const_spec(ws.shape), _const_spec(bf.shape)] + slabs,
        out_specs=(tile(D), tile(ATTN_WIDTH),
                   pl.BlockSpec((None, ATTN_HEADS, HEAD_PAIR, tm), lambda b, i: (b, 0, 0, i)),
                   pl.BlockSpec((None, ATTN_HEADS, tm, HEAD_PAIR), lambda b, i: (b, 0, i, 0)),
                   tile(SSM_WIDTH),
                   pl.BlockSpec((None, ATTN_HEADS, tm), lambda b, i: (b, 0, i)),
                   pl.BlockSpec((None, ATTN_HEADS, LANES), lambda b, i: (b, 0, 0)),
                   pl.BlockSpec((None, ATTN_HEADS, LANES), lambda b, i: (b, 0, 0)))
                  + tuple(slabs),
        out_shape=out_shape,
        scratch_shapes=[pltpu.VMEM((ATTN_HEADS, LANES), F32),
                        pltpu.VMEM(w1.shape, BF16), pltpu.VMEM(w3.shape, BF16),
                        pltpu.VMEM(w2.shape, BF16),
                        pltpu.VMEM((2, WEIGHT_STAGE_ROWS, w1.shape[1]), F32),
                        pltpu.VMEM((2, 2 * WEIGHT_STAGE_ROWS, w2.shape[1]), F32),
                        pltpu.SemaphoreType.DMA((2,)), pltpu.SemaphoreType.DMA((2,))],
        compiler_params=pltpu.CompilerParams(
            dimension_semantics=("arbitrary", "arbitrary"),
            vmem_limit_bytes=VMEM_LIMIT),
        name="head",
    )(x, g1, w1, w3, w2, gm, w_in, ws, bf, *later)
    return outs[:8], outs[8:]


def _attn_kernel(q_ref, *refs):
    _attn_prepare(0, q_ref, *refs)
    _attn_first_stage(0, q_ref, *refs)

    def q_tile(n, carry):
        _attn_q_tile(n, q_ref, *refs)
        return carry

    lax.fori_loop(0, q_ref.shape[0] // ATTN_TILE, q_tile, 0)


def _tile_start(i):
    return pl.multiple_of(i * ATTN_TILE, ATTN_TILE)


def _q_heads(q):
    lane = lax.broadcasted_iota(jnp.int32, (1, HEAD_PAIR), 1)
    first = lane < HEAD_DIM
    ones_even = jnp.where((lane >= HEAD_DIM) & (lane < HEAD_DIM + N_BIAS), 1.0, 0.0).astype(BF16)
    ones_odd = jnp.where(lane < N_BIAS, 1.0, 0.0).astype(BF16)
    return first, (jnp.where(first, q, ones_even), jnp.where(first, ones_odd, q))


def _attn_prepare(n, q_ref, kt_ref, v_ref, c_ref, kn_ref, cn_ref, o_ref, s_e0, s_e1, s_o0, s_o1, m_ref,
                  acc_ref, count_ref):
    t = ATTN_TILE
    q = q_ref[pl.ds(_tile_start(n), t), :]
    first, q_heads = _q_heads(q)
    q_sq = q.astype(F32) * q.astype(F32)
    row = lax.broadcasted_iota(jnp.int32, (t, t), 0)
    col = lax.broadcasted_iota(jnp.int32, (t, t), 1)
    tile_id = lax.broadcasted_iota(jnp.int32, (1, LANES), 1)
    needed = tile_id < 0
    for h, s_ref in enumerate((s_o0, s_o1)):
        s = jnp.where(col <= row, jnp.dot(q_heads[h], kt_ref[h, :, pl.ds(_tile_start(n), t)],
                                          preferred_element_type=F32), NEG_BIG)
        s_ref[...] = s
        m = jnp.max(s, axis=-1, keepdims=True)
        m_ref[h] = jnp.broadcast_to(m, (t, LANES))
        q_norm2 = jnp.max(jnp.sum(jnp.where(first == (h == 0), q_sq, 0.0), axis=-1, keepdims=True),
                          axis=0, keepdims=True)
        k_norm2, c_next = kn_ref[h:h + 1, :], cn_ref[h:h + 1, :]
        c_q = c_ref[h:h + 1, pl.ds(_tile_start(n), LANES)][:, 0:1]
        reach = jnp.sqrt(q_norm2 * k_norm2) * BOUND_SLACK_MUL + BOUND_SLACK_ADD + c_q - c_next
        needed = needed | (reach - jnp.min(m, axis=0, keepdims=True) >= -SKIP_LOG2)
    first_needed = jnp.min(jnp.where(needed & (tile_id < n), tile_id, n).astype(F32))
    count_ref[0] = n - first_needed.astype(jnp.int32)


def _attn_first_stage(n, q_ref, kt_ref, v_ref, c_ref, kn_ref, cn_ref, o_ref, s_e0, s_e1, s_o0, s_o1,
                      m_ref, acc_ref, count_ref):
    t = ATTN_TILE
    _, q_heads = _q_heads(q_ref[pl.ds(_tile_start(n), t), :])
    older = _tile_start(jnp.maximum(n - 1, 0))
    for h, s_ref in enumerate((s_e0, s_e1)):
        s_ref[...] = jnp.dot(q_heads[h], kt_ref[h, :, pl.ds(older, t)], preferred_element_type=F32)
    for h, s_ref in enumerate((s_o0, s_o1)):
        acc_ref[h] = jnp.dot(jnp.exp2(s_ref[...] - jnp.tile(m_ref[h], (1, t // LANES))).astype(BF16),
                             v_ref[h, pl.ds(_tile_start(n), t), :], preferred_element_type=F32)


def _attn_q_tile(n, q_ref, kt_ref, v_ref, c_ref, kn_ref, cn_ref, o_ref, s_e0, s_e1, s_o0, s_o1, m_ref,
                 acc_ref, count_ref):
    t = ATTN_TILE
    count = count_ref[0]
    start = _tile_start
    first, q_heads = _q_heads(q_ref[pl.ds(start(n), t), :])
    s_buf = ((s_e0, s_e1), (s_o0, s_o1))
    c_q = [c_ref[h:h + 1, pl.ds(start(n), LANES)][:, 0:1] for h in range(2)]

    def qk(h, kv):
        return jnp.dot(q_heads[h], kt_ref[h, :, pl.ds(start(kv), t)], preferred_element_type=F32)

    def consume(h, s, kv):
        d = c_ref[h:h + 1, pl.ds(start(kv), LANES)][:, 0:1] - c_q[h]
        m_old = m_ref[h]
        m_new = jnp.maximum(m_old, jnp.max(s, axis=-1, keepdims=True) - d)
        p = jnp.exp2(s - jnp.tile(m_new + d, (1, t // LANES))).astype(BF16)
        m_ref[h] = m_new
        acc_ref[h] = jnp.exp2(m_old - m_new) * acc_ref[h] + jnp.dot(
            p, v_ref[h, pl.ds(start(kv), t), :], preferred_element_type=F32)

    def step(kv, par):
        for h in range(2):
            s_buf[1 - par][h][...] = qk(h, kv - 1)
        for h in range(2):
            consume(h, s_buf[par][h][...], kv)

    def pair(i, carry):
        step(n - 1 - 2 * i, 0)
        step(n - 2 - 2 * i, 1)
        return carry

    n_pairs = jnp.maximum(count - 1, 0) // 2
    lax.fori_loop(0, n_pairs, pair, 0)
    left = count - 2 * n_pairs
    last = n - count

    def finish():
        acc0, acc1 = acc_ref[0], acc_ref[1]
        o_ref[pl.ds(start(n), t), :] = jnp.where(first, acc0 / acc0[:, HEAD_DIM:HEAD_DIM + 1],
                                                 acc1 / acc1[:, 0:1])
        nxt = jnp.minimum(n + 1, q_ref.shape[0] // t - 1)
        rest = (q_ref, kt_ref, v_ref, c_ref, kn_ref, cn_ref, o_ref, s_e0, s_e1, s_o0, s_o1, m_ref,
                acc_ref, count_ref)
        _attn_prepare(nxt, *rest)
        _attn_first_stage(nxt, *rest)

    @pl.when(left == 2)
    def _():
        step(last + 1, 0)
        for h in range(2):
            consume(h, s_buf[1][h][...], last)
        finish()

    @pl.when(left == 1)
    def _():
        for h in range(2):
            consume(h, s_buf[0][h][...], last)
        finish()

    @pl.when(left == 0)
    def _():
        finish()


def _attn_call(q, kt, v, ct, k_norm2, c_next):
    B, L, _ = q.shape
    t = ATTN_TILE
    assert L // t <= LANES
    by_pair = lambda a: a.reshape(B, N_PAIRS, 2, a.shape[-1])
    pair_block = lambda *shape: pl.BlockSpec((None, 2) + shape, lambda b, p: (b, p, 0, 0))
    pair_rows = lambda w: pl.BlockSpec((None, None, 2, w), lambda b, p: (b, p, 0, 0))
    lanes_of_pair = pl.BlockSpec((None, L, HEAD_PAIR), lambda b, p: (b, 0, p))
    return pl.pallas_call(
        _attn_kernel,
        grid=(B, N_PAIRS),
        in_specs=[lanes_of_pair, pair_block(HEAD_PAIR, L), pair_block(L, HEAD_PAIR),
                  pair_rows(L), pair_rows(LANES), pair_rows(LANES)],
        out_specs=lanes_of_pair,
        out_shape=jax.ShapeDtypeStruct((B, L, ATTN_WIDTH), F32),
        scratch_shapes=[pltpu.VMEM((t, t), F32)] * 4
                       + [pltpu.VMEM((2, t, LANES), F32), pltpu.VMEM((2, t, HEAD_PAIR), F32),
                          pltpu.SMEM((1,), jnp.int32)],
        compiler_params=pltpu.CompilerParams(
            dimension_semantics=("arbitrary", "arbitrary"),
            vmem_limit_bytes=VMEM_LIMIT),
        name="attn",
    )(q, kt, v, by_pair(ct), by_pair(k_norm2), by_pair(c_next))


def _gelu_tanh(x):
    return 0.5 * x * (1.0 + jnp.tanh(math.sqrt(2.0 / math.pi) * (x + 0.044715 * (x * x * x))))


def _cis(mag_arg, ang):
    mag = jnp.exp(mag_arg)
    return mag * jnp.cos(ang), mag * jnp.sin(ang)


def _cmul(ar, ai, br, bi):
    return ar * br - ai * bi, ar * bi + ai * br


def _cpow2(zr, zi, n):
    assert n & (n - 1) == 0
    while n > 1:
        zr, zi = zr * zr - zi * zi, 2.0 * zr * zi
        n //= 2
    return zr, zi


def _ssm_kernel(arow_ref, acol_ref, ldt_ref, bt_ref, cab_ref, dcol_ref, e_ref, f_ref, z_ref,
                *, chunks_per_seq):
    H = SSM_GROUP_CH

    @pl.when(pl.program_id(0) == 0)
    def _():
        z_ref[:, 0:SSM_CHUNK * H, :] = jnp.zeros((z_ref.shape[0], SSM_CHUNK * H, LANES), F32)

    for gi in range(SSM_GROUPS_PER_STEP):
        y = _ssm_group(arow_ref.at[gi], acol_ref.at[gi], ldt_ref.at[gi], bt_ref.at[gi], cab_ref.at[gi],
                       dcol_ref.at[gi], e_ref[:, gi * H:(gi + 1) * H, :], z_ref.at[gi], chunks_per_seq)
        f_ref[:, gi * H:(gi + 1) * H, :] = y.reshape(SSM_CHUNK, H, y.shape[-1])


def _ssm_group(arow_ref, acol_ref, ldt_ref, bt_ref, cab_ref, dcol_ref, e, z_ref, chunks_per_seq):
    T, P, H = SSM_CHUNK, SSM_STATE, SSM_GROUP_CH
    TH = T * H
    hi = lax.Precision.HIGHEST
    dt = jnp.exp(ldt_ref[...])

    lam_r, lam_i = dt * arow_ref[0:1, :], dt * arow_ref[1:2, :]
    j0 = lax.broadcasted_iota(jnp.int32, (T, 2 * P), 0).astype(F32)
    pa0, pb0 = _cis(j0 * lam_r, j0 * lam_i)
    pa1, pb1 = _cmul(pa0, pb0, *_cis(lam_r, lam_i))
    over_h = lambda a: jnp.concatenate(
        [jnp.broadcast_to(a[j:j + 1, :], (H, 2 * P)) for j in range(T)], axis=0)
    ca, cb = jnp.tile(cab_ref[0], (T, 1)), jnp.tile(cab_ref[1], (T, 1))
    c_pow0 = over_h(pa0) * ca + over_h(pb0) * cb
    c_pow1 = over_h(pa1) * ca + over_h(pb1) * cb

    a_r, a_i = acol_ref[:, 0:1], acol_ref[:, 1:2]
    lr, li = dt * a_r, dt * a_i
    abar_r, abar_i = _cis(lr, li)
    nr, ni = abar_r - 1.0, abar_i
    den = a_r * a_r + a_i * a_i
    fr, fi = (nr * a_r + ni * a_i) / den, (ni * a_r - nr * a_i) / den
    b_r, b_i = bt_ref[0], bt_ref[1]
    bb_r, bb_i = fr * b_r - fi * b_i, fr * b_i + fi * b_r

    kcol = jnp.dot(c_pow0, jnp.concatenate([bb_r, bb_i], axis=0), precision=hi,
                   preferred_element_type=F32)
    lane_h = lax.broadcasted_iota(jnp.int32, (H, LANES), 1) % H
    skip = jnp.where(lane_h == lax.broadcasted_iota(jnp.int32, (H, LANES), 0), dcol_ref[...], 0.0)

    z_ref[TH:2 * TH, :] = kcol
    z_ref[TH:TH + H, :] = kcol[:H] + skip
    lane_group = lax.broadcasted_iota(jnp.int32, (1, LANES), 1) // H
    groups_per_block = LANES // H
    blocks = []
    for v in range(TH // LANES):
        blk = None
        for u in range(groups_per_block):
            s = v * groups_per_block + u
            piece = z_ref[TH - H * s:2 * TH - H * s, :]
            blk = piece if blk is None else jnp.where(lane_group == u, piece, blk)
        blocks.append(blk.astype(BF16))
    mt = jnp.concatenate(blocks, axis=1)

    expo = (groups_per_block - 1 - lane_group).astype(F32)
    wr, wi = _cis(lr * expo, li * expo)
    hop_r, hop_i = _cpow2(abar_r, abar_i, groups_per_block)
    w1_r, w1_i = [], []
    for v in range(TH // LANES):
        w1_r.insert(0, wr * bb_r - wi * bb_i)
        w1_i.insert(0, wr * bb_i + wi * bb_r)
        wr, wi = _cmul(wr, wi, hop_r, hop_i)
    w1t = jnp.concatenate([jnp.concatenate(w1_r, axis=1),
                           jnp.concatenate(w1_i, axis=1)], axis=0).astype(BF16)

    e = e.reshape(TH, e.shape[-1])
    y = jnp.dot(mt, e, preferred_element_type=F32)
    st = jnp.dot(w1t, e, preferred_element_type=F32)
    sr, si = st[:P], st[P:]
    pos = lax.broadcasted_iota(jnp.int32, sr.shape, 1) % chunks_per_seq

    def shifted(a, shift):
        return jnp.where(pos >= shift, pltpu.roll(a, shift, 1), 0.0)

    qr, qi = _cpow2(abar_r, abar_i, T)
    shift = 1
    while shift < chunks_per_seq:
        srs, sis = shifted(sr, shift), shifted(si, shift)
        sr, si = sr + qr * srs - qi * sis, si + qr * sis + qi * srs
        qr, qi = qr * qr - qi * qi, 2.0 * qr * qi
        shift *= 2
    x_prev = jnp.concatenate([shifted(sr, 1), shifted(si, 1)], axis=0).astype(BF16)
    y = y + jnp.dot(c_pow1.astype(BF16), x_prev, preferred_element_type=F32)
    return _gelu_tanh(y)


def _ssm_call(arow, acol, ldt, bt, cab, dcol, e, chunks_per_seq):
    T, _, NC = e.shape
    G = arow.shape[0]
    gps = SSM_GROUPS_PER_STEP
    assert G % gps == 0
    grp = lambda a: pl.BlockSpec((gps,) + a.shape[1:], lambda g: (g,) + (0,) * (a.ndim - 1))
    channels = pl.BlockSpec((T, gps * SSM_GROUP_CH, NC), lambda g: (0, g, 0))
    return pl.pallas_call(
        functools.partial(_ssm_kernel, chunks_per_seq=chunks_per_seq),
        grid=(G // gps,),
        in_specs=[grp(a) for a in (arow, acol, ldt, bt, cab, dcol)] + [channels],
        out_specs=channels,
        out_shape=jax.ShapeDtypeStruct((T, SSM_WIDTH, NC), F32),
        scratch_shapes=[pltpu.VMEM((gps, 2 * T * SSM_GROUP_CH, LANES), F32)],
        compiler_params=pltpu.CompilerParams(
            dimension_semantics=("arbitrary",), vmem_limit_bytes=VMEM_LIMIT),
        name="ssm",
    )(arow, acol, ldt, bt, cab, dcol, e)


def _ssm_param_layouts(a_re, a_im, log_dt, b_re, b_im, c_re, c_im, d_skip):
    G = a_re.shape[0]
    arow = jnp.stack([jnp.concatenate([a_re, a_re], -1), jnp.concatenate([a_im, a_im], -1)], 1)
    acol = jnp.stack([a_re, a_im], -1)
    reps = LANES // SSM_GROUP_CH
    bt = jnp.stack([jnp.tile(b_re, (1, 1, reps)), jnp.tile(b_im, (1, 1, reps))], 1)
    cab = jnp.stack([jnp.concatenate([c_re, -c_im], -1), jnp.concatenate([-c_im, -c_re], -1)], 1)
    return (arow.astype(F32), acol.astype(F32), log_dt.reshape(G, 1, 1).astype(F32),
            bt.astype(F32), cab.astype(F32), d_skip.reshape(G, SSM_GROUP_CH, 1).astype(F32))


def _tail_kernel(h1_ref, attn_ref, y_ref, p_ref, wglu_ref, bglu_ref, ga_ref, gs_ref,
                 wo_ref, g2_ref, w1_ref, w3_ref, w2_ref, gp_ref, wpg_ref,
                 wpp_ref, gf_ref, o_ref):
    y = y_ref[...]
    glu = y * _sigmoid(jnp.dot(y.astype(BF16), wglu_ref[...], preferred_element_type=F32)
                       + bglu_ref[...])
    an = _rms(attn_ref[...], ga_ref[...]).astype(BF16)
    sn = _rms(glu, gs_ref[...]).astype(BF16)
    h = (h1_ref[...] + jnp.dot(an, wo_ref[:ATTN_WIDTH, :], preferred_element_type=F32)
         + jnp.dot(sn, wo_ref[ATTN_WIDTH:, :], preferred_element_type=F32))
    h = h + 0.5 * _swiglu(_rms(h, g2_ref[...]).astype(BF16), w1_ref, w3_ref, w2_ref)
    gate = _sigmoid(jnp.dot(_rms(h, gp_ref[...]).astype(BF16), wpg_ref[...],
                            preferred_element_type=F32))
    h = h + gate * jnp.dot(p_ref[...].astype(BF16), wpp_ref[...], preferred_element_type=F32)
    o_ref[...] = _rms(h, gf_ref[...])


def _tail_call(h1, attn, y, p, *consts):
    B, L, D = h1.shape
    tm = TOKEN_TILE
    tile = lambda w: pl.BlockSpec((None, tm, w), lambda b, i: (b, i, 0))
    return pl.pallas_call(
        _tail_kernel,
        grid=(B, L // tm),
        in_specs=[tile(D), tile(ATTN_WIDTH), tile(SSM_WIDTH), tile(PLE_DIM)]
                 + [_const_spec(c.shape) for c in consts],
        out_specs=tile(D),
        out_shape=jax.ShapeDtypeStruct((B, L, D), F32),
        compiler_params=pltpu.CompilerParams(
            dimension_semantics=("arbitrary", "arbitrary"),
            vmem_limit_bytes=VMEM_LIMIT),
        name="tail",
    )(h1, attn, y, p, *consts)


def kernel(x, p, g_ffn1, w1_a, w3_a, w2_a, g_mix, w_in, b_f, a_re, a_im, log_dt, b_re, b_im, c_re, c_im, d_skip, w_glu, b_glu, g_attn_out, g_ssm_out, w_out, g_ffn2, w1_b, w3_b, w2_b, g_ple, w_ple_gate, w_ple_proj, g_final):
    B, L, D = x.shape
    assert D == D_MODEL and L % ATTN_TILE == 0 and L % TOKEN_TILE == 0 and L % SSM_CHUNK == 0
    assert g_ffn1.shape[0] == 1, "single layer"
    assert TOKEN_TILE == ATTN_TILE, "decay bias rows are relative to the kv tile start"
    row = lambda g: g.reshape(1, -1).astype(F32)
    bf = lambda w: w.astype(BF16)
    w_in16 = bf(w_in[0])
    assert w_in16.shape[1] == 3 * ATTN_WIDTH + ATTN_HEADS + SSM_WIDTH

    (h1, q, kt, v, s_in, ct, k_norm2, c_next), (w1_b16, w3_b16, w2_b16, w_out16, w_gate16) = _head_call(
        x, row(g_ffn1[0]), w1_a[0], w3_a[0], w2_a[0], row(g_mix[0]),
        w_in16, w_in16[:, 3 * ATTN_WIDTH + ATTN_HEADS:],
        b_f[0].reshape(ATTN_HEADS, 1).astype(F32),
        later=tuple(w.astype(F32) for w in (w1_b[0], w3_b[0], w2_b[0], w_out[0], w_ple_gate[0])))

    T = SSM_CHUNK
    chunks_per_seq = L // T
    n_chunks = B * chunks_per_seq
    e = s_in.reshape(n_chunks, T, SSM_WIDTH).transpose(1, 2, 0)
    f = _ssm_call(*_ssm_param_layouts(a_re[0], a_im[0], log_dt[0], b_re[0], b_im[0],
                                      c_re[0], c_im[0], d_skip[0]),
                  e, chunks_per_seq)
    y = f.transpose(2, 0, 1).reshape(B, L, SSM_WIDTH)

    attn = _attn_call(q, kt, v, ct, k_norm2, c_next)

    return _tail_call(
        h1, attn, y, p[0],
        bf(w_glu[0]), row(b_glu[0]), row(g_attn_out[0]), row(g_ssm_out[0]),
        w_out16, row(g_ffn2[0]),
        w1_b16, w3_b16, w2_b16, row(g_ple[0]), w_gate16,
        bf(w_ple_proj[0]), row(g_final))
```

```python
import functools
import math

import jax
import jax.numpy as jnp
from jax import lax
from jax.experimental import pallas as pl
from jax.experimental.pallas import tpu as pltpu

D_MODEL = 1024
ATTN_HEADS = 8
HEAD_DIM = 64
ATTN_WIDTH = ATTN_HEADS * HEAD_DIM
SSM_WIDTH = D_MODEL - ATTN_WIDTH
SSM_GROUP_CH = 16
SSM_GROUPS = SSM_WIDTH // SSM_GROUP_CH
SSM_STATE = 64
D_FF = 2816
PLE_DIM = 256
EPS = 1e-6

LANES = 128
BF16_SUBLANES = 16
HEAD_PAIR = 2 * HEAD_DIM
N_PAIRS = ATTN_HEADS // 2
FF_CHUNK = 256
TOKEN_TILE = 512
WEIGHT_STAGE_ROWS = 64
WEIGHT_STAGE_SLOTS = 4
ATTN_TILE = 512
SSM_CHUNK = 32
SSM_GROUPS_PER_STEP = 2
NEG_BIG = -1e30
SKIP_LOG2 = 140.0
BOUND_SLACK_MUL = 1.001
BOUND_SLACK_ADD = 1.0
LOG2E = math.log2(math.e)
N_BIAS = 3
BIAS_ROWS = 8
VMEM_LIMIT = 56 * 1024 * 1024

BF16 = jnp.bfloat16
F32 = jnp.float32


def _rms(x, g):
    ms = jnp.mean(x * x, axis=-1, keepdims=True)
    return x * lax.rsqrt(ms + EPS) * g


def _sigmoid(x):
    return 1.0 / (1.0 + jnp.exp(-x))


def _swiglu(xn, w1_ref, w3_ref, w2_ref):
    acc = None
    for c in range(D_FF // FF_CHUNK):
        sl = slice(c * FF_CHUNK, (c + 1) * FF_CHUNK)
        a = jnp.dot(xn, w1_ref[:, sl], preferred_element_type=F32)
        b = jnp.dot(xn, w3_ref[:, sl], preferred_element_type=F32)
        gated = (a * _sigmoid(a) * b).astype(BF16)
        part = jnp.dot(gated, w2_ref[sl, :], preferred_element_type=F32)
        acc = part if acc is None else acc + part
    return acc


def _const_spec(shape):
    nd = len(shape)
    return pl.BlockSpec(shape, lambda *_: (0,) * nd, pipeline_mode=pl.Buffered(1))


def _head_kernel(*refs, n_later):
    (x_ref, g1_ref, w1_hbm, w3_hbm, w2_hbm, gm_ref, win_ref, ws_ref, bf_ref) = refs[:9]
    later_in = refs[9:9 + n_later]
    h1_ref, q_ref, kt_ref, v_ref, s_ref, ct_ref, kn_ref, cn_ref = refs[9 + n_later:17 + n_later]
    later_out = refs[17 + n_later:17 + 2 * n_later]
    carry_ref, w1_ref, w3_ref, w2_ref, stage_a, stage_b, sem_a, sem_b = refs[-8:]

    @pl.when((pl.program_id(0) == 0) & (pl.program_id(1) == 0))
    def _():
        def stream(src, dst, stage, sem):
            slots, rows = stage.shape[0], stage.shape[1]
            n_chunks = src.shape[0] // rows
            copy = lambda c: pltpu.make_async_copy(
                src.at[pl.ds(c * rows, rows), :], stage.at[c % slots], sem.at[c % slots])
            for c in range(slots - 1):
                copy(c).start()
            for c in range(n_chunks):
                if c + slots - 1 < n_chunks:
                    copy(c + slots - 1).start()
                copy(c).wait()
                dst[pl.ds(c * rows, rows), :] = stage[c % slots].astype(BF16)
        stream(w1_hbm, w1_ref, stage_a, sem_a)
        stream(w3_hbm, w3_ref, stage_a, sem_a)
        stream(w2_hbm, w2_ref, stage_b, sem_b)

    @pl.when(pl.program_id(1) == 0)
    def _():
        carry_ref[...] = jnp.zeros_like(carry_ref)
        kn_ref[...] = jnp.zeros_like(kn_ref)
        cn_ref[...] = jnp.zeros_like(cn_ref)

    for src, dst in zip(later_in, later_out):
        dst[...] = src[...].astype(BF16)
    tm = x_ref.shape[0]
    x = x_ref[...]
    h1 = x + 0.5 * _swiglu(_rms(x, g1_ref[...]).astype(BF16), w1_ref, w3_ref, w2_ref)
    h1_ref[...] = h1
    un = _rms(h1, gm_ref[...]).astype(BF16)
    project = lambda w: jnp.dot(un, w, preferred_element_type=F32)
    zf = project(win_ref[:, 3 * ATTN_WIDTH:3 * ATTN_WIDTH + LANES])
    kv = project(win_ref[:, ATTN_WIDTH:3 * ATTN_WIDTH])
    zft = zf.T[:ATTN_HEADS, :] + bf_ref[...]
    logf = jnp.minimum(zft, 0.0) - jnp.log1p(jnp.exp(-jnp.abs(zft)))
    lane = lax.broadcasted_iota(jnp.int32, logf.shape, 1)
    c = logf
    shift = 1
    while shift < tm:
        c = c + jnp.where(lane >= shift, pltpu.roll(c, shift, 1), 0.0)
        shift *= 2

    c_abs = c + carry_ref[:, 0:1]
    ct_ref[...] = c_abs * LOG2E
    carry_ref[...] = jnp.broadcast_to(c_abs[:, tm - 1:tm], carry_ref.shape)

    rel = (c - c[:, 0:1]) * LOG2E
    hi = rel.astype(BF16).astype(F32)
    mid = (rel - hi).astype(BF16).astype(F32)
    lo = (rel - hi - mid).astype(BF16).astype(F32)
    kt = kv[:, :ATTN_WIDTH].astype(BF16).astype(F32).T
    vv = kv[:, ATTN_WIDTH:]
    k_sq = (kt * kt).reshape(ATTN_HEADS, HEAD_DIM, tm)
    tile = pl.program_id(1)
    stat_lane = lax.broadcasted_iota(jnp.int32, kn_ref.shape, 1)

    kn_ref[...] = jnp.where(stat_lane == tile,
                            jnp.max(jnp.sum(k_sq, axis=1), axis=-1, keepdims=True), kn_ref[...])
    cn_ref[...] = jnp.where(stat_lane == tile - 1, c_abs[:, 0:1] * LOG2E, cn_ref[...])
    sub = lax.broadcasted_iota(jnp.int32, (BIAS_ROWS, tm), 0)
    zeros = jnp.zeros((HEAD_DIM - BIAS_ROWS, tm), F32)
    vlane = lax.broadcasted_iota(jnp.int32, (tm, HEAD_PAIR), 1)
    for h in range(ATTN_HEADS):
        bias = jnp.where(sub == 0, -hi[h:h + 1],
                         jnp.where(sub == 1, -mid[h:h + 1],
                                   jnp.where(sub == 2, -lo[h:h + 1], 0.0)))
        k_h = kt[h * HEAD_DIM:(h + 1) * HEAD_DIM]
        vp = vv[:, (h // 2) * HEAD_PAIR:(h // 2 + 1) * HEAD_PAIR]
        if h % 2 == 0:
            kt_ref[h] = jnp.concatenate([k_h, bias, zeros], axis=0).astype(BF16)
            v_ref[h] = jnp.where(vlane < HEAD_DIM, vp,
                                 jnp.where(vlane == HEAD_DIM, 1.0, 0.0)).astype(BF16)
        else:
            kt_ref[h] = jnp.concatenate([bias, zeros, k_h], axis=0).astype(BF16)
            v_ref[h] = jnp.where(vlane >= HEAD_DIM, vp,
                                 jnp.where(vlane == 0, 1.0, 0.0)).astype(BF16)

    q_ref[...] = (project(win_ref[:, :ATTN_WIDTH]) * (LOG2E / math.sqrt(HEAD_DIM))).astype(BF16)
    s_ref[...] = project(ws_ref[...]).astype(BF16)


def _slab_spec(w, n_steps, steps_per_batch):
    rows = next(r for r in range(BF16_SUBLANES, w.shape[0] + 1, BF16_SUBLANES)
                if w.shape[0] % r == 0 and w.shape[0] // r <= n_steps)
    last = w.shape[0] // rows - 1
    return pl.BlockSpec((rows, w.shape[1]),
                        lambda b, i: (jnp.minimum(b * steps_per_batch + i, last), 0))


def _head_call(x, g1, w1, w3, w2, gm, w_in, ws, bf, later):
    B, L, D = x.shape
    tm = TOKEN_TILE
    tile = lambda w: pl.BlockSpec((None, tm, w), lambda b, i: (b, i, 0))
    slabs = [_slab_spec(w, B * (L // tm), L // tm) for w in later]
    out_shape = (
        jax.ShapeDtypeStruct((B, L, D), F32),
        jax.ShapeDtypeStruct((B, L, ATTN_WIDTH), BF16),
        jax.ShapeDtypeStruct((B, ATTN_HEADS, HEAD_PAIR, L), BF16),
        jax.ShapeDtypeStruct((B, ATTN_HEADS, L, HEAD_PAIR), BF16),
        jax.ShapeDtypeStruct((B, L, SSM_WIDTH), BF16),
        jax.ShapeDtypeStruct((B, ATTN_HEADS, L), F32),
        jax.ShapeDtypeStruct((B, ATTN_HEADS, LANES), F32),
        jax.ShapeDtypeStruct((B, ATTN_HEADS, LANES), F32),
    ) + tuple(jax.ShapeDtypeStruct(w.shape, BF16) for w in later)
    outs = pl.pallas_call(
        functools.partial(_head_kernel, n_later=len(later)),
        grid=(B, L // tm),
        in_specs=[tile(D), _const_spec(g1.shape), pl.BlockSpec(memory_space=pl.ANY),
                  pl.BlockSpec(memory_space=pl.ANY), pl.BlockSpec(memory_space=pl.ANY),
                  _const_spec(gm.shape),
                  _const_spec(w_in.shape), _const_spec(ws.shape), _const_spec(bf.shape)] + slabs,
        out_specs=(tile(D), tile(ATTN_WIDTH),
                   pl.BlockSpec((None, ATTN_HEADS, HEAD_PAIR, tm), lambda b, i: (b, 0, 0, i)),
                   pl.BlockSpec((None, ATTN_HEADS, tm, HEAD_PAIR), lambda b, i: (b, 0, i, 0)),
                   tile(SSM_WIDTH),
                   pl.BlockSpec((None, ATTN_HEADS, tm), lambda b, i: (b, 0, i)),
                   pl.BlockSpec((None, ATTN_HEADS, LANES), lambda b, i: (b, 0, 0)),
                   pl.BlockSpec((None, ATTN_HEADS, LANES), lambda b, i: (b, 0, 0)))
                  + tuple(slabs),
        out_shape=out_shape,
        scratch_shapes=[pltpu.VMEM((ATTN_HEADS, LANES), F32),
                        pltpu.VMEM(w1.shape, BF16), pltpu.VMEM(w3.shape, BF16),
                        pltpu.VMEM(w2.shape, BF16),
                        pltpu.VMEM((WEIGHT_STAGE_SLOTS, WEIGHT_STAGE_ROWS, w1.shape[1]), F32),
                        pltpu.VMEM((WEIGHT_STAGE_SLOTS, 2 * WEIGHT_STAGE_ROWS, w2.shape[1]), F32),
                        pltpu.SemaphoreType.DMA((WEIGHT_STAGE_SLOTS,)),
                        pltpu.SemaphoreType.DMA((WEIGHT_STAGE_SLOTS,))],
        compiler_params=pltpu.CompilerParams(
            dimension_semantics=("arbitrary", "arbitrary"),
            vmem_limit_bytes=VMEM_LIMIT),
        name="head",
    )(x, g1, w1, w3, w2, gm, w_in, ws, bf, *later)
    return outs[:8], outs[8:]


def _attn_kernel(q_ref, *refs):
    _attn_prepare(0, q_ref, *refs)
    _attn_first_stage(0, q_ref, *refs)

    def q_tile(n, carry):
        _attn_q_tile(n, q_ref, *refs)
        return carry

    lax.fori_loop(0, q_ref.shape[0] // ATTN_TILE, q_tile, 0)


def _tile_start(i):
    return pl.multiple_of(i * ATTN_TILE, ATTN_TILE)


def _q_heads(q):
    lane = lax.broadcasted_iota(jnp.int32, (1, HEAD_PAIR), 1)
    first = lane < HEAD_DIM
    ones_even = jnp.where((lane >= HEAD_DIM) & (lane < HEAD_DIM + N_BIAS), 1.0, 0.0).astype(BF16)
    ones_odd = jnp.where(lane < N_BIAS, 1.0, 0.0).astype(BF16)
    return first, (jnp.where(first, q, ones_even), jnp.where(first, ones_odd, q))


def _attn_prepare(n, q_ref, kt_ref, v_ref, c_ref, kn_ref, cn_ref, o_ref, s_e0, s_e1, s_o0, s_o1, m_ref,
                  acc_ref, count_ref):
    t = ATTN_TILE
    q = q_ref[pl.ds(_tile_start(n), t), :]
    first, q_heads = _q_heads(q)
    q_sq = q.astype(F32) * q.astype(F32)
    row = lax.broadcasted_iota(jnp.int32, (t, t), 0)
    col = lax.broadcasted_iota(jnp.int32, (t, t), 1)
    tile_id = lax.broadcasted_iota(jnp.int32, (1, LANES), 1)
    needed = tile_id < 0
    for h, s_ref in enumerate((s_o0, s_o1)):
        s = jnp.where(col <= row, jnp.dot(q_heads[h], kt_ref[h, :, pl.ds(_tile_start(n), t)],
                                          preferred_element_type=F32), NEG_BIG)
        s_ref[...] = s
        m = jnp.max(s, axis=-1, keepdims=True)
        m_ref[h] = jnp.broadcast_to(m, (t, LANES))
        q_norm2 = jnp.max(jnp.sum(jnp.where(first == (h == 0), q_sq, 0.0), axis=-1, keepdims=True),
                          axis=0, keepdims=True)
        k_norm2, c_next = kn_ref[h:h + 1, :], cn_ref[h:h + 1, :]
        c_q = c_ref[h:h + 1, pl.ds(_tile_start(n), LANES)][:, 0:1]
        reach = jnp.sqrt(q_norm2 * k_norm2) * BOUND_SLACK_MUL + BOUND_SLACK_ADD + c_q - c_next
        needed = needed | (reach - jnp.min(m, axis=0, keepdims=True) >= -SKIP_LOG2)
    first_needed = jnp.min(jnp.where(needed & (tile_id < n), tile_id, n).astype(F32))
    count_ref[0] = n - first_needed.astype(jnp.int32)


def _attn_first_stage(n, q_ref, kt_ref, v_ref, c_ref, kn_ref, cn_ref, o_ref, s_e0, s_e1, s_o0, s_o1,
                      m_ref, acc_ref, count_ref):
    t = ATTN_TILE
    _, q_heads = _q_heads(q_ref[pl.ds(_tile_start(n), t), :])
    older = _tile_start(jnp.maximum(n - 1, 0))
    for h, s_ref in enumerate((s_e0, s_e1)):
        s_ref[...] = jnp.dot(q_heads[h], kt_ref[h, :, pl.ds(older, t)], preferred_element_type=F32)
    for h, s_ref in enumerate((s_o0, s_o1)):
        acc_ref[h] = jnp.dot(jnp.exp2(s_ref[...] - jnp.tile(m_ref[h], (1, t // LANES))).astype(BF16),
                             v_ref[h, pl.ds(_tile_start(n), t), :], preferred_element_type=F32)


def _attn_q_tile(n, q_ref, kt_ref, v_ref, c_ref, kn_ref, cn_ref, o_ref, s_e0, s_e1, s_o0, s_o1, m_ref,
                 acc_ref, count_ref):
    t = ATTN_TILE
    count = count_ref[0]
    start = _tile_start
    first, q_heads = _q_heads(q_ref[pl.ds(start(n), t), :])
    s_buf = ((s_e0, s_e1), (s_o0, s_o1))
    c_q = [c_ref[h:h + 1, pl.ds(start(n), LANES)][:, 0:1] for h in range(2)]

    def qk(h, kv):
        return jnp.dot(q_heads[h], kt_ref[h, :, pl.ds(start(kv), t)], preferred_element_type=F32)

    def consume(h, s, kv):
        d = c_ref[h:h + 1, pl.ds(start(kv), LANES)][:, 0:1] - c_q[h]
        m_old = m_ref[h]
        m_new = jnp.maximum(m_old, jnp.max(s, axis=-1, keepdims=True) - d)
        p = jnp.exp2(s - jnp.tile(m_new + d, (1, t // LANES))).astype(BF16)
        m_ref[h] = m_new
        acc_ref[h] = jnp.exp2(m_old - m_new) * acc_ref[h] + jnp.dot(
            p, v_ref[h, pl.ds(start(kv), t), :], preferred_element_type=F32)

    def step(kv, par):
        for h in range(2):
            s_buf[1 - par][h][...] = qk(h, kv - 1)
        for h in range(2):
            consume(h, s_buf[par][h][...], kv)

    def pair(i, carry):
        step(n - 1 - 2 * i, 0)
        step(n - 2 - 2 * i, 1)
        return carry

    n_pairs = jnp.maximum(count - 1, 0) // 2
    lax.fori_loop(0, n_pairs, pair, 0)
    left = count - 2 * n_pairs
    last = n - count

    def finish():
        acc0, acc1 = acc_ref[0], acc_ref[1]
        o_ref[pl.ds(start(n), t), :] = jnp.where(first, acc0 / acc0[:, HEAD_DIM:HEAD_DIM + 1],
                                                 acc1 / acc1[:, 0:1])
        nxt = jnp.minimum(n + 1, q_ref.shape[0] // t - 1)
        rest = (q_ref, kt_ref, v_ref, c_ref, kn_ref, cn_ref, o_ref, s_e0, s_e1, s_o0, s_o1, m_ref,
                acc_ref, count_ref)
        _attn_prepare(nxt, *rest)
        _attn_first_stage(nxt, *rest)

    @pl.when(left == 2)
    def _():
        step(last + 1, 0)
        for h in range(2):
            consume(h, s_buf[1][h][...], last)
        finish()

    @pl.when(left == 1)
    def _():
        for h in range(2):
            consume(h, s_buf[0][h][...], last)
        finish()

    @pl.when(left == 0)
    def _():
        finish()


def _attn_call(q, kt, v, ct, k_norm2, c_next):
    B, L, _ = q.shape
    t = ATTN_TILE
    assert L // t <= LANES
    by_pair = lambda a: a.reshape(B, N_PAIRS, 2, a.shape[-1])
    pair_block = lambda *shape: pl.BlockSpec((None, 2) + shape, lambda b, p: (b, p, 0, 0))
    pair_rows = lambda w: pl.BlockSpec((None, None, 2, w), lambda b, p: (b, p, 0, 0))
    lanes_of_pair = pl.BlockSpec((None, L, HEAD_PAIR), lambda b, p: (b, 0, p))
    return pl.pallas_call(
        _attn_kernel,
        grid=(B, N_PAIRS),
        in_specs=[lanes_of_pair, pair_block(HEAD_PAIR, L), pair_block(L, HEAD_PAIR),
                  pair_rows(L), pair_rows(LANES), pair_rows(LANES)],
        out_specs=lanes_of_pair,
        out_shape=jax.ShapeDtypeStruct((B, L, ATTN_WIDTH), F32),
        scratch_shapes=[pltpu.VMEM((t, t), F32)] * 4
                       + [pltpu.VMEM((2, t, LANES), F32), pltpu.VMEM((2, t, HEAD_PAIR), F32),
                          pltpu.SMEM((1,), jnp.int32)],
        compiler_params=pltpu.CompilerParams(
            dimension_semantics=("arbitrary", "arbitrary"),
            vmem_limit_bytes=VMEM_LIMIT),
        name="attn",
    )(q, kt, v, by_pair(ct), by_pair(k_norm2), by_pair(c_next))


def _gelu_tanh(x):
    return 0.5 * x * (1.0 + jnp.tanh(math.sqrt(2.0 / math.pi) * (x + 0.044715 * (x * x * x))))


def _cis(mag_arg, ang):
    mag = jnp.exp(mag_arg)
    return mag * jnp.cos(ang), mag * jnp.sin(ang)


def _cmul(ar, ai, br, bi):
    return ar * br - ai * bi, ar * bi + ai * br


def _cpow2(zr, zi, n):
    assert n & (n - 1) == 0
    while n > 1:
        zr, zi = zr * zr - zi * zi, 2.0 * zr * zi
        n //= 2
    return zr, zi


def _ssm_kernel(arow_ref, acol_ref, ldt_ref, bt_ref, cab_ref, dcol_ref, e_ref, f_ref, z_ref,
                *, chunks_per_seq):
    H = SSM_GROUP_CH

    @pl.when(pl.program_id(0) == 0)
    def _():
        z_ref[:, 0:SSM_CHUNK * H, :] = jnp.zeros((z_ref.shape[0], SSM_CHUNK * H, LANES), F32)

    for gi in range(SSM_GROUPS_PER_STEP):
        y = _ssm_group(arow_ref.at[gi], acol_ref.at[gi], ldt_ref.at[gi], bt_ref.at[gi], cab_ref.at[gi],
                       dcol_ref.at[gi], e_ref[:, gi * H:(gi + 1) * H, :], z_ref.at[gi], chunks_per_seq)
        f_ref[:, gi * H:(gi + 1) * H, :] = y.reshape(SSM_CHUNK, H, y.shape[-1])


def _ssm_group(arow_ref, acol_ref, ldt_ref, bt_ref, cab_ref, dcol_ref, e, z_ref, chunks_per_seq):
    T, P, H = SSM_CHUNK, SSM_STATE, SSM_GROUP_CH
    TH = T * H
    hi = lax.Precision.HIGHEST
    dt = jnp.exp(ldt_ref[...])

    lam_r, lam_i = dt * arow_ref[0:1, :], dt * arow_ref[1:2, :]
    j0 = lax.broadcasted_iota(jnp.int32, (T, 2 * P), 0).astype(F32)
    pa0, pb0 = _cis(j0 * lam_r, j0 * lam_i)
    pa1, pb1 = _cmul(pa0, pb0, *_cis(lam_r, lam_i))
    over_h = lambda a: jnp.concatenate(
        [jnp.broadcast_to(a[j:j + 1, :], (H, 2 * P)) for j in range(T)], axis=0)
    ca, cb = jnp.tile(cab_ref[0], (T, 1)), jnp.tile(cab_ref[1], (T, 1))
    c_pow0 = over_h(pa0) * ca + over_h(pb0) * cb
    c_pow1 = over_h(pa1) * ca + over_h(pb1) * cb

    a_r, a_i = acol_ref[:, 0:1], acol_ref[:, 1:2]
    lr, li = dt * a_r, dt * a_i
    abar_r, abar_i = _cis(lr, li)
    nr, ni = abar_r - 1.0, abar_i
    den = a_r * a_r + a_i * a_i
    fr, fi = (nr * a_r + ni * a_i) / den, (ni * a_r - nr * a_i) / den
    b_r, b_i = bt_ref[0], bt_ref[1]
    bb_r, bb_i = fr * b_r - fi * b_i, fr * b_i + fi * b_r

    kcol = jnp.dot(c_pow0, jnp.concatenate([bb_r, bb_i], axis=0), precision=hi,
                   preferred_element_type=F32)
    lane_h = lax.broadcasted_iota(jnp.int32, (H, LANES), 1) % H
    skip = jnp.where(lane_h == lax.broadcasted_iota(jnp.int32, (H, LANES), 0), dcol_ref[...], 0.0)

    z_ref[TH:2 * TH, :] = kcol
    z_ref[TH:TH + H, :] = kcol[:H] + skip
    lane_group = lax.broadcasted_iota(jnp.int32, (1, LANES), 1) // H
    groups_per_block = LANES // H
    blocks = []
    for v in range(TH // LANES):
        blk = None
        for u in range(groups_per_block):
            s = v * groups_per_block + u
            piece = z_ref[TH - H * s:2 * TH - H * s, :]
            blk = piece if blk is None else jnp.where(lane_group == u, piece, blk)
        blocks.append(blk.astype(BF16))
    mt = jnp.concatenate(blocks, axis=1)

    expo = (groups_per_block - 1 - lane_group).astype(F32)
    wr, wi = _cis(lr * expo, li * expo)
    hop_r, hop_i = _cpow2(abar_r, abar_i, groups_per_block)
    w1_r, w1_i = [], []
    for v in range(TH // LANES):
        w1_r.insert(0, wr * bb_r - wi * bb_i)
        w1_i.insert(0, wr * bb_i + wi * bb_r)
        wr, wi = _cmul(wr, wi, hop_r, hop_i)
    w1t = jnp.concatenate([jnp.concatenate(w1_r, axis=1),
                           jnp.concatenate(w1_i, axis=1)], axis=0).astype(BF16)

    e = e.reshape(TH, e.shape[-1])
    y = jnp.dot(mt, e, preferred_element_type=F32)
    st = jnp.dot(w1t, e, preferred_element_type=F32)
    sr, si = st[:P], st[P:]
    pos = lax.broadcasted_iota(jnp.int32, sr.shape, 1) % chunks_per_seq

    def shifted(a, shift):
        return jnp.where(pos >= shift, pltpu.roll(a, shift, 1), 0.0)

    qr, qi = _cpow2(abar_r, abar_i, T)
    shift = 1
    while shift < chunks_per_seq:
        srs, sis = shifted(sr, shift), shifted(si, shift)
        sr, si = sr + qr * srs - qi * sis, si + qr * sis + qi * srs
        qr, qi = qr * qr - qi * qi, 2.0 * qr * qi
        shift *= 2
    x_prev = jnp.concatenate([shifted(sr, 1), shifted(si, 1)], axis=0).astype(BF16)
    y = y + jnp.dot(c_pow1.astype(BF16), x_prev, preferred_element_type=F32)
    return _gelu_tanh(y)


def _ssm_call(arow, acol, ldt, bt, cab, dcol, e, chunks_per_seq):
    T, _, NC = e.shape
    G = arow.shape[0]
    gps = SSM_GROUPS_PER_STEP
    assert G % gps == 0
    grp = lambda a: pl.BlockSpec((gps,) + a.shape[1:], lambda g: (g,) + (0,) * (a.ndim - 1))
    channels = pl.BlockSpec((T, gps * SSM_GROUP_CH, NC), lambda g: (0, g, 0))
    return pl.pallas_call(
        functools.partial(_ssm_kernel, chunks_per_seq=chunks_per_seq),
        grid=(G // gps,),
        in_specs=[grp(a) for a in (arow, acol, ldt, bt, cab, dcol)] + [channels],
        out_specs=channels,
        out_shape=jax.ShapeDtypeStruct((T, SSM_WIDTH, NC), F32),
        scratch_shapes=[pltpu.VMEM((gps, 2 * T * SSM_GROUP_CH, LANES), F32)],
        compiler_params=pltpu.CompilerParams(
            dimension_semantics=("arbitrary",), vmem_limit_bytes=VMEM_LIMIT),
        name="ssm",
    )(arow, acol, ldt, bt, cab, dcol, e)


def _ssm_param_layouts(a_re, a_im, log_dt, b_re, b_im, c_re, c_im, d_skip):
    G = a_re.shape[0]
    arow = jnp.stack([jnp.concatenate([a_re, a_re], -1), jnp.concatenate([a_im, a_im], -1)], 1)
    acol = jnp.stack([a_re, a_im], -1)
    reps = LANES // SSM_GROUP_CH
    bt = jnp.stack([jnp.tile(b_re, (1, 1, reps)), jnp.tile(b_im, (1, 1, reps))], 1)
    cab = jnp.stack([jnp.concatenate([c_re, -c_im], -1), jnp.concatenate([-c_im, -c_re], -1)], 1)
    return (arow.astype(F32), acol.astype(F32), log_dt.reshape(G, 1, 1).astype(F32),
            bt.astype(F32), cab.astype(F32), d_skip.reshape(G, SSM_GROUP_CH, 1).astype(F32))


def _tail_kernel(h1_ref, attn_ref, y_ref, p_ref, wglu_ref, bglu_ref, ga_ref, gs_ref,
                 wo_ref, g2_ref, w1_ref, w3_ref, w2_ref, gp_ref, wpg_ref,
                 wpp_ref, gf_ref, o_ref):
    y = y_ref[...]
    glu = y * _sigmoid(jnp.dot(y.astype(BF16), wglu_ref[...], preferred_element_type=F32)
                       + bglu_ref[...])
    an = _rms(attn_ref[...], ga_ref[...]).astype(BF16)
    sn = _rms(glu, gs_ref[...]).astype(BF16)
    h = (h1_ref[...] + jnp.dot(an, wo_ref[:ATTN_WIDTH, :], preferred_element_type=F32)
         + jnp.dot(sn, wo_ref[ATTN_WIDTH:, :], preferred_element_type=F32))
    h = h + 0.5 * _swiglu(_rms(h, g2_ref[...]).astype(BF16), w1_ref, w3_ref, w2_ref)
    gate = _sigmoid(jnp.dot(_rms(h, gp_ref[...]).astype(BF16), wpg_ref[...],
                            preferred_element_type=F32))
    h = h + gate * jnp.dot(p_ref[...].astype(BF16), wpp_ref[...], preferred_element_type=F32)
    o_ref[...] = _rms(h, gf_ref[...])


def _tail_call(h1, attn, y, p, *consts):
    B, L, D = h1.shape
    tm = TOKEN_TILE
    tile = lambda w: pl.BlockSpec((None, tm, w), lambda b, i: (b, i, 0))
    return pl.pallas_call(
        _tail_kernel,
        grid=(B, L // tm),
        in_specs=[tile(D), tile(ATTN_WIDTH), tile(SSM_WIDTH), tile(PLE_DIM)]
                 + [_const_spec(c.shape) for c in consts],
        out_specs=tile(D),
        out_shape=jax.ShapeDtypeStruct((B, L, D), F32),
        compiler_params=pltpu.CompilerParams(
            dimension_semantics=("arbitrary", "arbitrary"),
            vmem_limit_bytes=VMEM_LIMIT),
        name="tail",
    )(h1, attn, y, p, *consts)


def kernel(x, p, g_ffn1, w1_a, w3_a, w2_a, g_mix, w_in, b_f, a_re, a_im, log_dt, b_re, b_im, c_re, c_im, d_skip, w_glu, b_glu, g_attn_out, g_ssm_out, w_out, g_ffn2, w1_b, w3_b, w2_b, g_ple, w_ple_gate, w_ple_proj, g_final):
    B, L, D = x.shape
    assert D == D_MODEL and L % ATTN_TILE == 0 and L % TOKEN_TILE == 0 and L % SSM_CHUNK == 0
    assert g_ffn1.shape[0] == 1, "single layer"
    assert TOKEN_TILE == ATTN_TILE, "decay bias rows are relative to the kv tile start"
    row = lambda g: g.reshape(1, -1).astype(F32)
    bf = lambda w: w.astype(BF16)
    w_in16 = bf(w_in[0])
    assert w_in16.shape[1] == 3 * ATTN_WIDTH + ATTN_HEADS + SSM_WIDTH

    (h1, q, kt, v, s_in, ct, k_norm2, c_next), (w1_b16, w3_b16, w2_b16, w_out16, w_gate16) = _head_call(
        x, row(g_ffn1[0]), w1_a[0], w3_a[0], w2_a[0], row(g_mix[0]),
        w_in16, w_in16[:, 3 * ATTN_WIDTH + ATTN_HEADS:],
        b_f[0].reshape(ATTN_HEADS, 1).astype(F32),
        later=tuple(w.astype(F32) for w in (w1_b[0], w3_b[0], w2_b[0], w_out[0], w_ple_gate[0])))

    T = SSM_CHUNK
    chunks_per_seq = L // T
    n_chunks = B * chunks_per_seq
    e = s_in.reshape(n_chunks, T, SSM_WIDTH).transpose(1, 2, 0)
    f = _ssm_call(*_ssm_param_layouts(a_re[0], a_im[0], log_dt[0], b_re[0], b_im[0],
                                      c_re[0], c_im[0], d_skip[0]),
                  e, chunks_per_seq)
    y = f.transpose(2, 0, 1).reshape(B, L, SSM_WIDTH)

    attn = _attn_call(q, kt, v, ct, k_norm2, c_next)

    return _tail_call(
        h1, attn, y, p[0],
        bf(w_glu[0]), row(b_glu[0]), row(g_attn_out[0]), row(g_ssm_out[0]),
        w_out16, row(g_ffn2[0]),
        w1_b16, w3_b16, w2_b16, row(g_ple[0]), w_gate16,
        bf(w_ple_proj[0]), row(g_final))
```

```python
import functools
import math

import jax
import jax.numpy as jnp
from jax import lax
from jax.experimental import pallas as pl
from jax.experimental.pallas import tpu as pltpu

D_MODEL = 1024
ATTN_HEADS = 8
HEAD_DIM = 64
ATTN_WIDTH = ATTN_HEADS * HEAD_DIM
SSM_WIDTH = D_MODEL - ATTN_WIDTH
SSM_GROUP_CH = 16
SSM_GROUPS = SSM_WIDTH // SSM_GROUP_CH
SSM_STATE = 64
D_FF = 2816
PLE_DIM = 256
EPS = 1e-6

LANES = 128
BF16_SUBLANES = 16
HEAD_PAIR = 2 * HEAD_DIM
N_PAIRS = ATTN_HEADS // 2
FF_CHUNK = 256
TOKEN_TILE = 512
WEIGHT_STAGE_SLOTS = 2
ATTN_TILE = 512
SSM_CHUNK = 32
SSM_GROUPS_PER_STEP = 2
NEG_BIG = -1e30
SKIP_LOG2 = 140.0
BOUND_SLACK_MUL = 1.001
BOUND_SLACK_ADD = 1.0
LOG2E = math.log2(math.e)
N_BIAS = 3
BIAS_ROWS = 8
VMEM_LIMIT = 56 * 1024 * 1024
HEAD_VMEM_LIMIT = 60 * 1024 * 1024

BF16 = jnp.bfloat16
F32 = jnp.float32


def _rms(x, g):
    ms = jnp.mean(x * x, axis=-1, keepdims=True)
    return x * lax.rsqrt(ms + EPS) * g


def _sigmoid(x):
    return 1.0 / (1.0 + jnp.exp(-x))


def _swiglu(xn, weights):
    acc = None
    for c in range(D_FF // FF_CHUNK):
        w1c, w3c, w2c = weights(c)
        a = jnp.dot(xn, w1c, preferred_element_type=F32)
        b = jnp.dot(xn, w3c, preferred_element_type=F32)
        gated = (a * _sigmoid(a) * b).astype(BF16)
        part = jnp.dot(gated, w2c, preferred_element_type=F32)
        acc = part if acc is None else acc + part
    return acc


def _resident_weights(w1_ref, w3_ref, w2_ref):
    def weights(c):
        sl = slice(c * FF_CHUNK, (c + 1) * FF_CHUNK)
        return w1_ref[:, sl], w3_ref[:, sl], w2_ref[sl, :]
    return weights


def _const_spec(shape):
    nd = len(shape)
    return pl.BlockSpec(shape, lambda *_: (0,) * nd, pipeline_mode=pl.Buffered(1))


def _head_kernel(*refs, n_later):
    first = (pl.program_id(0) == 0) & (pl.program_id(1) == 0)

    @pl.when(first)
    def _():
        _head_body(*refs, n_later=n_later, streaming=True)

    @pl.when(jnp.logical_not(first))
    def _():
        _head_body(*refs, n_later=n_later, streaming=False)


def _head_body(*refs, n_later, streaming):
    (x_ref, g1_ref, w1_hbm, w3_hbm, w2_hbm, gm_ref, win_ref, ws_ref, bf_ref) = refs[:9]
    later_in = refs[9:9 + n_later]
    h1_ref, q_ref, kt_ref, v_ref, s_ref, ct_ref, kn_ref, cn_ref = refs[9 + n_later:17 + n_later]
    later_out = refs[17 + n_later:17 + 2 * n_later]
    carry_ref, w1_ref, w3_ref, w2_ref, stage_a, stage_b, sem = refs[-7:]
    n_chunks = D_FF // FF_CHUNK
    slots = stage_b.shape[0]

    def copies(c):
        sl, s = pl.ds(c * FF_CHUNK, FF_CHUNK), c % slots
        return (pltpu.make_async_copy(w1_hbm.at[:, sl], stage_a.at[s, 0], sem.at[s, 0]),
                pltpu.make_async_copy(w3_hbm.at[:, sl], stage_a.at[s, 1], sem.at[s, 1]),
                pltpu.make_async_copy(w2_hbm.at[sl, :], stage_b.at[s], sem.at[s, 2]))

    def streamed_weights(c):
        if c + slots - 1 < n_chunks:
            for cp in copies(c + slots - 1):
                cp.start()
        for cp in copies(c):
            cp.wait()
        sl, s = slice(c * FF_CHUNK, (c + 1) * FF_CHUNK), c % slots
        w1_ref[:, sl] = stage_a[s, 0].astype(BF16)
        w3_ref[:, sl] = stage_a[s, 1].astype(BF16)
        w2_ref[sl, :] = stage_b[s].astype(BF16)
        return w1_ref[:, sl], w3_ref[:, sl], w2_ref[sl, :]

    if streaming:
        for c in range(slots - 1):
            for cp in copies(c):
                cp.start()
        weights = streamed_weights
    else:
        weights = _resident_weights(w1_ref, w3_ref, w2_ref)

    @pl.when(pl.program_id(1) == 0)
    def _():
        carry_ref[...] = jnp.zeros_like(carry_ref)
        kn_ref[...] = jnp.zeros_like(kn_ref)
        cn_ref[...] = jnp.zeros_like(cn_ref)

    for src, dst in zip(later_in, later_out):
        dst[...] = src[...].astype(BF16)
    tm = x_ref.shape[0]
    x = x_ref[...]
    h1 = x + 0.5 * _swiglu(_rms(x, g1_ref[...]).astype(BF16), weights)
    h1_ref[...] = h1
    un = _rms(h1, gm_ref[...]).astype(BF16)
    project = lambda w: jnp.dot(un, w, preferred_element_type=F32)
    zf = project(win_ref[:, 3 * ATTN_WIDTH:3 * ATTN_WIDTH + LANES])
    kv = project(win_ref[:, ATTN_WIDTH:3 * ATTN_WIDTH])
    zft = zf.T[:ATTN_HEADS, :] + bf_ref[...]
    logf = jnp.minimum(zft, 0.0) - jnp.log1p(jnp.exp(-jnp.abs(zft)))
    lane = lax.broadcasted_iota(jnp.int32, logf.shape, 1)
    c = logf
    shift = 1
    while shift < tm:
        c = c + jnp.where(lane >= shift, pltpu.roll(c, shift, 1), 0.0)
        shift *= 2

    c_abs = c + carry_ref[:, 0:1]
    ct_ref[...] = c_abs * LOG2E
    carry_ref[...] = jnp.broadcast_to(c_abs[:, tm - 1:tm], carry_ref.shape)

    rel = (c - c[:, 0:1]) * LOG2E
    hi = rel.astype(BF16).astype(F32)
    mid = (rel - hi).astype(BF16).astype(F32)
    lo = (rel - hi - mid).astype(BF16).astype(F32)
    kt = kv[:, :ATTN_WIDTH].astype(BF16).astype(F32).T
    vv = kv[:, ATTN_WIDTH:]
    k_sq = (kt * kt).reshape(ATTN_HEADS, HEAD_DIM, tm)
    tile = pl.program_id(1)
    stat_lane = lax.broadcasted_iota(jnp.int32, kn_ref.shape, 1)

    kn_ref[...] = jnp.where(stat_lane == tile,
                            jnp.max(jnp.sum(k_sq, axis=1), axis=-1, keepdims=True), kn_ref[...])
    cn_ref[...] = jnp.where(stat_lane == tile - 1, c_abs[:, 0:1] * LOG2E, cn_ref[...])
    sub = lax.broadcasted_iota(jnp.int32, (BIAS_ROWS, tm), 0)
    zeros = jnp.zeros((HEAD_DIM - BIAS_ROWS, tm), F32)
    vlane = lax.broadcasted_iota(jnp.int32, (tm, HEAD_PAIR), 1)
    for h in range(ATTN_HEADS):
        bias = jnp.where(sub == 0, -hi[h:h + 1],
                         jnp.where(sub == 1, -mid[h:h + 1],
                                   jnp.where(sub == 2, -lo[h:h + 1], 0.0)))
        k_h = kt[h * HEAD_DIM:(h + 1) * HEAD_DIM]
        vp = vv[:, (h // 2) * HEAD_PAIR:(h // 2 + 1) * HEAD_PAIR]
        if h % 2 == 0:
            kt_ref[h] = jnp.concatenate([k_h, bias, zeros], axis=0).astype(BF16)
            v_ref[h] = jnp.where(vlane < HEAD_DIM, vp,
                                 jnp.where(vlane == HEAD_DIM, 1.0, 0.0)).astype(BF16)
        else:
            kt_ref[h] = jnp.concatenate([bias, zeros, k_h], axis=0).astype(BF16)
            v_ref[h] = jnp.where(vlane >= HEAD_DIM, vp,
                                 jnp.where(vlane == 0, 1.0, 0.0)).astype(BF16)

    q_ref[...] = (project(win_ref[:, :ATTN_WIDTH]) * (LOG2E / math.sqrt(HEAD_DIM))).astype(BF16)
    s_ref[...] = project(ws_ref[...]).astype(BF16)


def _slab_spec(w, n_steps, steps_per_batch):
    rows = next(r for r in range(BF16_SUBLANES, w.shape[0] + 1, BF16_SUBLANES)
                if w.shape[0] % r == 0 and w.shape[0] // r <= n_steps)
    last = w.shape[0] // rows - 1
    return pl.BlockSpec((rows, w.shape[1]),
                        lambda b, i: (jnp.minimum(b * steps_per_batch + i, last), 0))


def _head_call(x, g1, w1, w3, w2, gm, w_in, ws, bf, later):
    B, L, D = x.shape
    tm = TOKEN_TILE
    tile = lambda w: pl.BlockSpec((None, tm, w), lambda b, i: (b, i, 0))
    slabs = [_slab_spec(w, B * (L // tm), L // tm) for w in later]
    out_shape = (
        jax.ShapeDtypeStruct((B, L, D), F32),
        jax.ShapeDtypeStruct((B, L, ATTN_WIDTH), BF16),
        jax.ShapeDtypeStruct((B, ATTN_HEADS, HEAD_PAIR, L), BF16),
        jax.ShapeDtypeStruct((B, ATTN_HEADS, L, HEAD_PAIR), BF16),
        jax.ShapeDtypeStruct((B, L, SSM_WIDTH), BF16),
        jax.ShapeDtypeStruct((B, ATTN_HEADS, L), F32),
        jax.ShapeDtypeStruct((B, ATTN_HEADS, LANES), F32),
        jax.ShapeDtypeStruct((B, ATTN_HEADS, LANES), F32),
    ) + tuple(jax.ShapeDtypeStruct(w.shape, BF16) for w in later)
    outs = pl.pallas_call(
        functools.partial(_head_kernel, n_later=len(later)),
        grid=(B, L // tm),
        in_specs=[tile(D), _const_spec(g1.shape), pl.BlockSpec(memory_space=pl.ANY),
                  pl.BlockSpec(memory_space=pl.ANY), pl.BlockSpec(memory_space=pl.ANY),
                  _const_spec(gm.shape),
                  _const_spec(w_in.shape), _const_spec(ws.shape), _const_spec(bf.shape)] + slabs,
        out_specs=(tile(D), tile(ATTN_WIDTH),
                   pl.BlockSpec((None, ATTN_HEADS, HEAD_PAIR, tm), lambda b, i: (b, 0, 0, i)),
                   pl.BlockSpec((None, ATTN_HEADS, tm, HEAD_PAIR), lambda b, i: (b, 0, i, 0)),
                   tile(SSM_WIDTH),
                   pl.BlockSpec((None, ATTN_HEADS, tm), lambda b, i: (b, 0, i)),
                   pl.BlockSpec((None, ATTN_HEADS, LANES), lambda b, i: (b, 0, 0)),
                   pl.BlockSpec((None, ATTN_HEADS, LANES), lambda b, i: (b, 0, 0)))
                  + tuple(slabs),
        out_shape=out_shape,
        scratch_shapes=[pltpu.VMEM((ATTN_HEADS, LANES), F32),
                        pltpu.VMEM(w1.shape, BF16), pltpu.VMEM(w3.shape, BF16),
                        pltpu.VMEM(w2.shape, BF16),
                        pltpu.VMEM((WEIGHT_STAGE_SLOTS, 2, w1.shape[0], FF_CHUNK), F32),
                        pltpu.VMEM((WEIGHT_STAGE_SLOTS, FF_CHUNK, w2.shape[1]), F32),
                        pltpu.SemaphoreType.DMA((WEIGHT_STAGE_SLOTS, 3))],
        compiler_params=pltpu.CompilerParams(
            dimension_semantics=("arbitrary", "arbitrary"),
            vmem_limit_bytes=HEAD_VMEM_LIMIT),
        name="head",
    )(x, g1, w1, w3, w2, gm, w_in, ws, bf, *later)
    return outs[:8], outs[8:]


def _attn_kernel(q_ref, *refs):
    _attn_prepare(0, q_ref, *refs)
    _attn_first_stage(0, q_ref, *refs)

    def q_tile(n, carry):
        _attn_q_tile(n, q_ref, *refs)
        return carry

    lax.fori_loop(0, q_ref.shape[0] // ATTN_TILE, q_tile, 0)


def _tile_start(i):
    return pl.multiple_of(i * ATTN_TILE, ATTN_TILE)


def _q_heads(q):
    lane = lax.broadcasted_iota(jnp.int32, (1, HEAD_PAIR), 1)
    first = lane < HEAD_DIM
    ones_even = jnp.where((lane >= HEAD_DIM) & (lane < HEAD_DIM + N_BIAS), 1.0, 0.0).astype(BF16)
    ones_odd = jnp.where(lane < N_BIAS, 1.0, 0.0).astype(BF16)
    return first, (jnp.where(first, q, ones_even), jnp.where(first, ones_odd, q))


def _attn_prepare(n, q_ref, kt_ref, v_ref, c_ref, kn_ref, cn_ref, o_ref, s_e0, s_e1, s_o0, s_o1, m_ref,
                  acc_ref, count_ref):
    t = ATTN_TILE
    q = q_ref[pl.ds(_tile_start(n), t), :]
    first, q_heads = _q_heads(q)
    q_sq = q.astype(F32) * q.astype(F32)
    row = lax.broadcasted_iota(jnp.int32, (t, t), 0)
    col = lax.broadcasted_iota(jnp.int32, (t, t), 1)
    tile_id = lax.broadcasted_iota(jnp.int32, (1, LANES), 1)
    needed = tile_id < 0
    for h, s_ref in enumerate((s_o0, s_o1)):
        s = jnp.where(col <= row, jnp.dot(q_heads[h], kt_ref[h, :, pl.ds(_tile_start(n), t)],
                                          preferred_element_type=F32), NEG_BIG)
        s_ref[...] = s
        m = jnp.max(s, axis=-1, keepdims=True)
        m_ref[h] = jnp.broadcast_to(m, (t, LANES))
        q_norm2 = jnp.max(jnp.sum(jnp.where(first == (h == 0), q_sq, 0.0), axis=-1, keepdims=True),
                          axis=0, keepdims=True)
        k_norm2, c_next = kn_ref[h:h + 1, :], cn_ref[h:h + 1, :]
        c_q = c_ref[h:h + 1, pl.ds(_tile_start(n), LANES)][:, 0:1]
        reach = jnp.sqrt(q_norm2 * k_norm2) * BOUND_SLACK_MUL + BOUND_SLACK_ADD + c_q - c_next
        needed = needed | (reach - jnp.min(m, axis=0, keepdims=True) >= -SKIP_LOG2)
    first_needed = jnp.min(jnp.where(needed & (tile_id < n), tile_id, n).astype(F32))
    count_ref[0] = n - first_needed.astype(jnp.int32)


def _attn_first_stage(n, q_ref, kt_ref, v_ref, c_ref, kn_ref, cn_ref, o_ref, s_e0, s_e1, s_o0, s_o1,
                      m_ref, acc_ref, count_ref):
    t = ATTN_TILE
    _, q_heads = _q_heads(q_ref[pl.ds(_tile_start(n), t), :])
    older = _tile_start(jnp.maximum(n - 1, 0))
    for h, s_ref in enumerate((s_e0, s_e1)):
        s_ref[...] = jnp.dot(q_heads[h], kt_ref[h, :, pl.ds(older, t)], preferred_element_type=F32)
    for h, s_ref in enumerate((s_o0, s_o1)):
        acc_ref[h] = jnp.dot(jnp.exp2(s_ref[...] - jnp.tile(m_ref[h], (1, t // LANES))).astype(BF16),
                             v_ref[h, pl.ds(_tile_start(n), t), :], preferred_element_type=F32)


def _attn_q_tile(n, q_ref, kt_ref, v_ref, c_ref, kn_ref, cn_ref, o_ref, s_e0, s_e1, s_o0, s_o1, m_ref,
                 acc_ref, count_ref):
    t = ATTN_TILE
    count = count_ref[0]
    start = _tile_start
    first, q_heads = _q_heads(q_ref[pl.ds(start(n), t), :])
    s_buf = ((s_e0, s_e1), (s_o0, s_o1))
    c_q = [c_ref[h:h + 1, pl.ds(start(n), LANES)][:, 0:1] for h in range(2)]

    def qk(h, kv):
        return jnp.dot(q_heads[h], kt_ref[h, :, pl.ds(start(kv), t)], preferred_element_type=F32)

    def consume(h, s, kv):
        d = c_ref[h:h + 1, pl.ds(start(kv), LANES)][:, 0:1] - c_q[h]
        m_old = m_ref[h]
        m_new = jnp.maximum(m_old, jnp.max(s, axis=-1, keepdims=True) - d)
        p = jnp.exp2(s - jnp.tile(m_new + d, (1, t // LANES))).astype(BF16)
        m_ref[h] = m_new
        acc_ref[h] = jnp.exp2(m_old - m_new) * acc_ref[h] + jnp.dot(
            p, v_ref[h, pl.ds(start(kv), t), :], preferred_element_type=F32)

    def step(kv, par):
        for h in range(2):
            s_buf[1 - par][h][...] = qk(h, kv - 1)
        for h in range(2):
            consume(h, s_buf[par][h][...], kv)

    def pair(i, carry):
        step(n - 1 - 2 * i, 0)
        step(n - 2 - 2 * i, 1)
        return carry

    n_pairs = jnp.maximum(count - 1, 0) // 2
    lax.fori_loop(0, n_pairs, pair, 0)
    left = count - 2 * n_pairs
    last = n - count

    def finish():
        acc0, acc1 = acc_ref[0], acc_ref[1]
        o_ref[pl.ds(start(n), t), :] = jnp.where(first, acc0 / acc0[:, HEAD_DIM:HEAD_DIM + 1],
                                                 acc1 / acc1[:, 0:1])
        nxt = jnp.minimum(n + 1, q_ref.shape[0] // t - 1)
        rest = (q_ref, kt_ref, v_ref, c_ref, kn_ref, cn_ref, o_ref, s_e0, s_e1, s_o0, s_o1, m_ref,
                acc_ref, count_ref)
        _attn_prepare(nxt, *rest)
        _attn_first_stage(nxt, *rest)

    @pl.when(left == 2)
    def _():
        step(last + 1, 0)
        for h in range(2):
            consume(h, s_buf[1][h][...], last)
        finish()

    @pl.when(left == 1)
    def _():
        for h in range(2):
            consume(h, s_buf[0][h][...], last)
        finish()

    @pl.when(left == 0)
    def _():
        finish()


def _attn_call(q, kt, v, ct, k_norm2, c_next):
    B, L, _ = q.shape
    t = ATTN_TILE
    assert L // t <= LANES
    by_pair = lambda a: a.reshape(B, N_PAIRS, 2, a.shape[-1])
    pair_block = lambda *shape: pl.BlockSpec((None, 2) + shape, lambda b, p: (b, p, 0, 0))
    pair_rows = lambda w: pl.BlockSpec((None, None, 2, w), lambda b, p: (b, p, 0, 0))
    lanes_of_pair = pl.BlockSpec((None, L, HEAD_PAIR), lambda b, p: (b, 0, p))
    return pl.pallas_call(
        _attn_kernel,
        grid=(B, N_PAIRS),
        in_specs=[lanes_of_pair, pair_block(HEAD_PAIR, L), pair_block(L, HEAD_PAIR),
                  pair_rows(L), pair_rows(LANES), pair_rows(LANES)],
        out_specs=lanes_of_pair,
        out_shape=jax.ShapeDtypeStruct((B, L, ATTN_WIDTH), F32),
        scratch_shapes=[pltpu.VMEM((t, t), F32)] * 4
                       + [pltpu.VMEM((2, t, LANES), F32), pltpu.VMEM((2, t, HEAD_PAIR), F32),
                          pltpu.SMEM((1,), jnp.int32)],
        compiler_params=pltpu.CompilerParams(
            dimension_semantics=("arbitrary", "arbitrary"),
            vmem_limit_bytes=VMEM_LIMIT),
        name="attn",
    )(q, kt, v, by_pair(ct), by_pair(k_norm2), by_pair(c_next))


def _gelu_tanh(x):
    return 0.5 * x * (1.0 + jnp.tanh(math.sqrt(2.0 / math.pi) * (x + 0.044715 * (x * x * x))))


def _cis(mag_arg, ang):
    mag = jnp.exp(mag_arg)
    return mag * jnp.cos(ang), mag * jnp.sin(ang)


def _cmul(ar, ai, br, bi):
    return ar * br - ai * bi, ar * bi + ai * br


def _cpow2(zr, zi, n):
    assert n & (n - 1) == 0
    while n > 1:
        zr, zi = zr * zr - zi * zi, 2.0 * zr * zi
        n //= 2
    return zr, zi


def _ssm_kernel(arow_ref, acol_ref, ldt_ref, bt_ref, cab_ref, dcol_ref, e_ref, f_ref, z_ref,
                *, chunks_per_seq):
    H = SSM_GROUP_CH

    @pl.when(pl.program_id(0) == 0)
    def _():
        z_ref[:, 0:SSM_CHUNK * H, :] = jnp.zeros((z_ref.shape[0], SSM_CHUNK * H, LANES), F32)

    for gi in range(SSM_GROUPS_PER_STEP):
        y = _ssm_group(arow_ref.at[gi], acol_ref.at[gi], ldt_ref.at[gi], bt_ref.at[gi], cab_ref.at[gi],
                       dcol_ref.at[gi], e_ref[:, gi * H:(gi + 1) * H, :], z_ref.at[gi], chunks_per_seq)
        f_ref[:, gi * H:(gi + 1) * H, :] = y.reshape(SSM_CHUNK, H, y.shape[-1])


def _ssm_group(arow_ref, acol_ref, ldt_ref, bt_ref, cab_ref, dcol_ref, e, z_ref, chunks_per_seq):
    T, P, H = SSM_CHUNK, SSM_STATE, SSM_GROUP_CH
    TH = T * H
    hi = lax.Precision.HIGHEST
    dt = jnp.exp(ldt_ref[...])

    lam_r, lam_i = dt * arow_ref[0:1, :], dt * arow_ref[1:2, :]
    j0 = lax.broadcasted_iota(jnp.int32, (T, 2 * P), 0).astype(F32)
    pa0, pb0 = _cis(j0 * lam_r, j0 * lam_i)
    pa1, pb1 = _cmul(pa0, pb0, *_cis(lam_r, lam_i))
    over_h = lambda a: jnp.concatenate(
        [jnp.broadcast_to(a[j:j + 1, :], (H, 2 * P)) for j in range(T)], axis=0)
    ca, cb = jnp.tile(cab_ref[0], (T, 1)), jnp.tile(cab_ref[1], (T, 1))
    c_pow0 = over_h(pa0) * ca + over_h(pb0) * cb
    c_pow1 = over_h(pa1) * ca + over_h(pb1) * cb

    a_r, a_i = acol_ref[:, 0:1], acol_ref[:, 1:2]
    lr, li = dt * a_r, dt * a_i
    abar_r, abar_i = _cis(lr, li)
    nr, ni = abar_r - 1.0, abar_i
    den = a_r * a_r + a_i * a_i
    fr, fi = (nr * a_r + ni * a_i) / den, (ni * a_r - nr * a_i) / den
    b_r, b_i = bt_ref[0], bt_ref[1]
    bb_r, bb_i = fr * b_r - fi * b_i, fr * b_i + fi * b_r

    kcol = jnp.dot(c_pow0, jnp.concatenate([bb_r, bb_i], axis=0), precision=hi,
                   preferred_element_type=F32)
    lane_h = lax.broadcasted_iota(jnp.int32, (H, LANES), 1) % H
    skip = jnp.where(lane_h == lax.broadcasted_iota(jnp.int32, (H, LANES), 0), dcol_ref[...], 0.0)

    z_ref[TH:2 * TH, :] = kcol
    z_ref[TH:TH + H, :] = kcol[:H] + skip
    lane_group = lax.broadcasted_iota(jnp.int32, (1, LANES), 1) // H
    groups_per_block = LANES // H
    blocks = []
    for v in range(TH // LANES):
        blk = None
        for u in range(groups_per_block):
            s = v * groups_per_block + u
            piece = z_ref[TH - H * s:2 * TH - H * s, :]
            blk = piece if blk is None else jnp.where(lane_group == u, piece, blk)
        blocks.append(blk.astype(BF16))
    mt = jnp.concatenate(blocks, axis=1)

    expo = (groups_per_block - 1 - lane_group).astype(F32)
    wr, wi = _cis(lr * expo, li * expo)
    hop_r, hop_i = _cpow2(abar_r, abar_i, groups_per_block)
    w1_r, w1_i = [], []
    for v in range(TH // LANES):
        w1_r.insert(0, wr * bb_r - wi * bb_i)
        w1_i.insert(0, wr * bb_i + wi * bb_r)
        wr, wi = _cmul(wr, wi, hop_r, hop_i)
    w1t = jnp.concatenate([jnp.concatenate(w1_r, axis=1),
                           jnp.concatenate(w1_i, axis=1)], axis=0).astype(BF16)

    e = e.reshape(TH, e.shape[-1])
    y = jnp.dot(mt, e, preferred_element_type=F32)
    st = jnp.dot(w1t, e, preferred_element_type=F32)
    sr, si = st[:P], st[P:]
    pos = lax.broadcasted_iota(jnp.int32, sr.shape, 1) % chunks_per_seq

    def shifted(a, shift):
        return jnp.where(pos >= shift, pltpu.roll(a, shift, 1), 0.0)

    qr, qi = _cpow2(abar_r, abar_i, T)
    shift = 1
    while shift < chunks_per_seq:
        srs, sis = shifted(sr, shift), shifted(si, shift)
        sr, si = sr + qr * srs - qi * sis, si + qr * sis + qi * srs
        qr, qi = qr * qr - qi * qi, 2.0 * qr * qi
        shift *= 2
    x_prev = jnp.concatenate([shifted(sr, 1), shifted(si, 1)], axis=0).astype(BF16)
    y = y + jnp.dot(c_pow1.astype(BF16), x_prev, preferred_element_type=F32)
    return _gelu_tanh(y)


def _ssm_call(arow, acol, ldt, bt, cab, dcol, e, chunks_per_seq):
    T, _, NC = e.shape
    G = arow.shape[0]
    gps = SSM_GROUPS_PER_STEP
    assert G % gps == 0
    grp = lambda a: pl.BlockSpec((gps,) + a.shape[1:], lambda g: (g,) + (0,) * (a.ndim - 1))
    channels = pl.BlockSpec((T, gps * SSM_GROUP_CH, NC), lambda g: (0, g, 0))
    return pl.pallas_call(
        functools.partial(_ssm_kernel, chunks_per_seq=chunks_per_seq),
        grid=(G // gps,),
        in_specs=[grp(a) for a in (arow, acol, ldt, bt, cab, dcol)] + [channels],
        out_specs=channels,
        out_shape=jax.ShapeDtypeStruct((T, SSM_WIDTH, NC), F32),
        scratch_shapes=[pltpu.VMEM((gps, 2 * T * SSM_GROUP_CH, LANES), F32)],
        compiler_params=pltpu.CompilerParams(
            dimension_semantics=("arbitrary",), vmem_limit_bytes=VMEM_LIMIT),
        name="ssm",
    )(arow, acol, ldt, bt, cab, dcol, e)


def _ssm_param_layouts(a_re, a_im, log_dt, b_re, b_im, c_re, c_im, d_skip):
    G = a_re.shape[0]
    arow = jnp.stack([jnp.concatenate([a_re, a_re], -1), jnp.concatenate([a_im, a_im], -1)], 1)
    acol = jnp.stack([a_re, a_im], -1)
    reps = LANES // SSM_GROUP_CH
    bt = jnp.stack([jnp.tile(b_re, (1, 1, reps)), jnp.tile(b_im, (1, 1, reps))], 1)
    cab = jnp.stack([jnp.concatenate([c_re, -c_im], -1), jnp.concatenate([-c_im, -c_re], -1)], 1)
    return (arow.astype(F32), acol.astype(F32), log_dt.reshape(G, 1, 1).astype(F32),
            bt.astype(F32), cab.astype(F32), d_skip.reshape(G, SSM_GROUP_CH, 1).astype(F32))


def _tail_kernel(h1_ref, attn_ref, y_ref, p_ref, wglu_ref, bglu_ref, ga_ref, gs_ref,
                 wo_ref, g2_ref, w1_ref, w3_ref, w2_ref, gp_ref, wpg_ref,
                 wpp_ref, gf_ref, o_ref):
    y = y_ref[...]
    glu = y * _sigmoid(jnp.dot(y.astype(BF16), wglu_ref[...], preferred_element_type=F32)
                       + bglu_ref[...])
    an = _rms(attn_ref[...], ga_ref[...]).astype(BF16)
    sn = _rms(glu, gs_ref[...]).astype(BF16)
    h = (h1_ref[...] + jnp.dot(an, wo_ref[:ATTN_WIDTH, :], preferred_element_type=F32)
         + jnp.dot(sn, wo_ref[ATTN_WIDTH:, :], preferred_element_type=F32))
    h = h + 0.5 * _swiglu(_rms(h, g2_ref[...]).astype(BF16),
                          _resident_weights(w1_ref, w3_ref, w2_ref))
    gate = _sigmoid(jnp.dot(_rms(h, gp_ref[...]).astype(BF16), wpg_ref[...],
                            preferred_element_type=F32))
    h = h + gate * jnp.dot(p_ref[...].astype(BF16), wpp_ref[...], preferred_element_type=F32)
    o_ref[...] = _rms(h, gf_ref[...])


def _tail_call(h1, attn, y, p, *consts):
    B, L, D = h1.shape
    tm = TOKEN_TILE
    tile = lambda w: pl.BlockSpec((None, tm, w), lambda b, i: (b, i, 0))
    return pl.pallas_call(
        _tail_kernel,
        grid=(B, L // tm),
        in_specs=[tile(D), tile(ATTN_WIDTH), tile(SSM_WIDTH), tile(PLE_DIM)]
                 + [_const_spec(c.shape) for c in consts],
        out_specs=tile(D),
        out_shape=jax.ShapeDtypeStruct((B, L, D), F32),
        compiler_params=pltpu.CompilerParams(
            dimension_semantics=("arbitrary", "arbitrary"),
            vmem_limit_bytes=VMEM_LIMIT),
        name="tail",
    )(h1, attn, y, p, *consts)


def kernel(x, p, g_ffn1, w1_a, w3_a, w2_a, g_mix, w_in, b_f, a_re, a_im, log_dt, b_re, b_im, c_re, c_im, d_skip, w_glu, b_glu, g_attn_out, g_ssm_out, w_out, g_ffn2, w1_b, w3_b, w2_b, g_ple, w_ple_gate, w_ple_proj, g_final):
    B, L, D = x.shape
    assert D == D_MODEL and L % ATTN_TILE == 0 and L % TOKEN_TILE == 0 and L % SSM_CHUNK == 0
    assert g_ffn1.shape[0] == 1, "single layer"
    assert TOKEN_TILE == ATTN_TILE, "decay bias rows are relative to the kv tile start"
    row = lambda g: g.reshape(1, -1).astype(F32)
    bf = lambda w: w.astype(BF16)
    w_in16 = bf(w_in[0])
    assert w_in16.shape[1] == 3 * ATTN_WIDTH + ATTN_HEADS + SSM_WIDTH

    (h1, q, kt, v, s_in, ct, k_norm2, c_next), (w1_b16, w3_b16, w2_b16, w_out16, w_gate16) = _head_call(
        x, row(g_ffn1[0]), w1_a[0], w3_a[0], w2_a[0], row(g_mix[0]),
        w_in16, w_in16[:, 3 * ATTN_WIDTH + ATTN_HEADS:],
        b_f[0].reshape(ATTN_HEADS, 1).astype(F32),
        later=tuple(w.astype(F32) for w in (w1_b[0], w3_b[0], w2_b[0], w_out[0], w_ple_gate[0])))

    T = SSM_CHUNK
    chunks_per_seq = L // T
    n_chunks = B * chunks_per_seq
    e = s_in.reshape(n_chunks, T, SSM_WIDTH).transpose(1, 2, 0)
    f = _ssm_call(*_ssm_param_layouts(a_re[0], a_im[0], log_dt[0], b_re[0], b_im[0],
                                      c_re[0], c_im[0], d_skip[0]),
                  e, chunks_per_seq)
    y = f.transpose(2, 0, 1).reshape(B, L, SSM_WIDTH)

    attn = _attn_call(q, kt, v, ct, k_norm2, c_next)

    return _tail_call(
        h1, attn, y, p[0],
        bf(w_glu[0]), row(b_glu[0]), row(g_attn_out[0]), row(g_ssm_out[0]),
        w_out16, row(g_ffn2[0]),
        w1_b16, w3_b16, w2_b16, row(g_ple[0]), w_gate16,
        bf(w_ple_proj[0]), row(g_final))
```

```python
import functools
import math

import jax
import jax.numpy as jnp
from jax import lax
from jax.experimental import pallas as pl
from jax.experimental.pallas import tpu as pltpu

D_MODEL = 1024
ATTN_HEADS = 8
HEAD_DIM = 64
ATTN_WIDTH = ATTN_HEADS * HEAD_DIM
SSM_WIDTH = D_MODEL - ATTN_WIDTH
SSM_GROUP_CH = 16
SSM_GROUPS = SSM_WIDTH // SSM_GROUP_CH
SSM_STATE = 64
D_FF = 2816
PLE_DIM = 256
EPS = 1e-6

LANES = 128
BF16_SUBLANES = 16
HEAD_PAIR = 2 * HEAD_DIM
N_PAIRS = ATTN_HEADS // 2
FF_CHUNK = 256
TOKEN_TILE = 512
WEIGHT_STAGE_SLOTS = 2
ATTN_TILE = 512
SSM_CHUNK = 32
SSM_GROUPS_PER_STEP = 2
NEG_BIG = -1e30
SKIP_LOG2 = 140.0
BOUND_SLACK_MUL = 1.001
BOUND_SLACK_ADD = 1.0
LOG2E = math.log2(math.e)
N_BIAS = 3
BIAS_ROWS = 8
VMEM_LIMIT = 56 * 1024 * 1024
HEAD_VMEM_LIMIT = 60 * 1024 * 1024

BF16 = jnp.bfloat16
F32 = jnp.float32


def _rms(x, g):
    ms = jnp.mean(x * x, axis=-1, keepdims=True)
    return x * lax.rsqrt(ms + EPS) * g


def _sigmoid(x):
    return 1.0 / (1.0 + jnp.exp(-x))


def _swiglu(xn, weights):
    acc = None
    for c in range(D_FF // FF_CHUNK):
        w1c, w3c, w2c = weights(c)
        a = jnp.dot(xn, w1c, preferred_element_type=F32)
        b = jnp.dot(xn, w3c, preferred_element_type=F32)
        gated = (a * _sigmoid(a) * b).astype(BF16)
        part = jnp.dot(gated, w2c, preferred_element_type=F32)
        acc = part if acc is None else acc + part
    return acc


def _resident_weights(w1_ref, w3_ref, w2_ref):
    def weights(c):
        sl = slice(c * FF_CHUNK, (c + 1) * FF_CHUNK)
        return w1_ref[:, sl], w3_ref[:, sl], w2_ref[sl, :]
    return weights


def _const_spec(shape):
    nd = len(shape)
    return pl.BlockSpec(shape, lambda *_: (0,) * nd, pipeline_mode=pl.Buffered(1))


def _head_kernel(*refs, n_later):
    first = (pl.program_id(0) == 0) & (pl.program_id(1) == 0)

    @pl.when(first)
    def _():
        _head_body(*refs, n_later=n_later, streaming=True)

    @pl.when(jnp.logical_not(first))
    def _():
        _head_body(*refs, n_later=n_later, streaming=False)


def _head_body(*refs, n_later, streaming):
    (x_ref, g1_ref, w1_hbm, w3_hbm, w2_hbm, gm_ref, win_ref, ws_ref, bf_ref) = refs[:9]
    later_in = refs[9:9 + n_later]
    h1_ref, q_ref, kt_ref, v_ref, s_ref, ct_ref, kn_ref, cn_ref = refs[9 + n_later:17 + n_later]
    later_out = refs[17 + n_later:17 + 2 * n_later]
    carry_ref, w1_ref, w3_ref, w2_ref, stage_a, stage_b, sem = refs[-7:]
    n_chunks = D_FF // FF_CHUNK
    slots = stage_b.shape[0]

    def copies(c):
        sl, s = pl.ds(c * FF_CHUNK, FF_CHUNK), c % slots
        return (pltpu.make_async_copy(w1_hbm.at[:, sl], stage_a.at[s, 0], sem.at[s, 0]),
                pltpu.make_async_copy(w3_hbm.at[:, sl], stage_a.at[s, 1], sem.at[s, 1]),
                pltpu.make_async_copy(w2_hbm.at[sl, :], stage_b.at[s], sem.at[s, 2]))

    def streamed_weights(c):
        if c + slots - 1 < n_chunks:
            for cp in copies(c + slots - 1):
                cp.start()
        for cp in copies(c):
            cp.wait()
        sl, s = slice(c * FF_CHUNK, (c + 1) * FF_CHUNK), c % slots
        w1_ref[:, sl] = stage_a[s, 0].astype(BF16)
        w3_ref[:, sl] = stage_a[s, 1].astype(BF16)
        w2_ref[sl, :] = stage_b[s].astype(BF16)
        return w1_ref[:, sl], w3_ref[:, sl], w2_ref[sl, :]

    if streaming:
        for c in range(slots - 1):
            for cp in copies(c):
                cp.start()
        weights = streamed_weights
    else:
        weights = _resident_weights(w1_ref, w3_ref, w2_ref)

    @pl.when(pl.program_id(1) == 0)
    def _():
        carry_ref[...] = jnp.zeros_like(carry_ref)
        kn_ref[...] = jnp.zeros_like(kn_ref)
        cn_ref[...] = jnp.zeros_like(cn_ref)

    for src, dst in zip(later_in, later_out):
        dst[...] = src[...].astype(BF16)
    tm = x_ref.shape[0]
    x = x_ref[...]
    h1 = x + 0.5 * _swiglu(_rms(x, g1_ref[...]).astype(BF16), weights)
    h1_ref[...] = h1
    un = _rms(h1, gm_ref[...]).astype(BF16)
    project = lambda w: jnp.dot(un, w, preferred_element_type=F32)
    zf = project(win_ref[:, 3 * ATTN_WIDTH:3 * ATTN_WIDTH + LANES])
    kv = project(win_ref[:, ATTN_WIDTH:3 * ATTN_WIDTH])
    zft = zf.T[:ATTN_HEADS, :] + bf_ref[...]
    logf = jnp.minimum(zft, 0.0) - jnp.log1p(jnp.exp(-jnp.abs(zft)))
    lane = lax.broadcasted_iota(jnp.int32, logf.shape, 1)
    c = logf
    shift = 1
    while shift < tm:
        c = c + jnp.where(lane >= shift, pltpu.roll(c, shift, 1), 0.0)
        shift *= 2

    c_abs = c + carry_ref[:, 0:1]
    ct_ref[...] = c_abs * LOG2E
    carry_ref[...] = jnp.broadcast_to(c_abs[:, tm - 1:tm], carry_ref.shape)

    rel = (c - c[:, 0:1]) * LOG2E
    hi = rel.astype(BF16).astype(F32)
    mid = (rel - hi).astype(BF16).astype(F32)
    lo = (rel - hi - mid).astype(BF16).astype(F32)
    kt = kv[:, :ATTN_WIDTH].astype(BF16).astype(F32).T
    vv = kv[:, ATTN_WIDTH:]
    k_sq = (kt * kt).reshape(ATTN_HEADS, HEAD_DIM, tm)
    tile = pl.program_id(1)
    stat_lane = lax.broadcasted_iota(jnp.int32, kn_ref.shape, 1)

    kn_ref[...] = jnp.where(stat_lane == tile,
                            jnp.max(jnp.sum(k_sq, axis=1), axis=-1, keepdims=True), kn_ref[...])
    cn_ref[...] = jnp.where(stat_lane == tile - 1, c_abs[:, 0:1] * LOG2E, cn_ref[...])
    sub = lax.broadcasted_iota(jnp.int32, (BIAS_ROWS, tm), 0)
    zeros = jnp.zeros((HEAD_DIM - BIAS_ROWS, tm), F32)
    vlane = lax.broadcasted_iota(jnp.int32, (tm, HEAD_PAIR), 1)
    for h in range(ATTN_HEADS):
        bias = jnp.where(sub == 0, -hi[h:h + 1],
                         jnp.where(sub == 1, -mid[h:h + 1],
                                   jnp.where(sub == 2, -lo[h:h + 1], 0.0)))
        k_h = kt[h * HEAD_DIM:(h + 1) * HEAD_DIM]
        vp = vv[:, (h // 2) * HEAD_PAIR:(h // 2 + 1) * HEAD_PAIR]
        if h % 2 == 0:
            kt_ref[h] = jnp.concatenate([k_h, bias, zeros], axis=0).astype(BF16)
            v_ref[h] = jnp.where(vlane < HEAD_DIM, vp,
                                 jnp.where(vlane == HEAD_DIM, 1.0, 0.0)).astype(BF16)
        else:
            kt_ref[h] = jnp.concatenate([bias, zeros, k_h], axis=0).astype(BF16)
            v_ref[h] = jnp.where(vlane >= HEAD_DIM, vp,
                                 jnp.where(vlane == 0, 1.0, 0.0)).astype(BF16)

    q_ref[...] = (project(win_ref[:, :ATTN_WIDTH]) * (LOG2E / math.sqrt(HEAD_DIM))).astype(BF16)
    s_ref[...] = project(ws_ref[...]).astype(BF16)


def _slab_spec(w, n_steps, steps_per_batch):
    rows = next(r for r in range(BF16_SUBLANES, w.shape[0] + 1, BF16_SUBLANES)
                if w.shape[0] % r == 0 and w.shape[0] // r <= n_steps)
    last = w.shape[0] // rows - 1
    return pl.BlockSpec((rows, w.shape[1]),
                        lambda b, i: (jnp.minimum(b * steps_per_batch + i, last), 0))


def _head_call(x, g1, w1, w3, w2, gm, w_in, ws, bf, later):
    B, L, D = x.shape
    tm = TOKEN_TILE
    tile = lambda w: pl.BlockSpec((None, tm, w), lambda b, i: (b, i, 0))
    slabs = [_slab_spec(w, B * (L // tm), L // tm) for w in later]
    out_shape = (
        jax.ShapeDtypeStruct((B, L, D), F32),
        jax.ShapeDtypeStruct((B, L, ATTN_WIDTH), BF16),
        jax.ShapeDtypeStruct((B, ATTN_HEADS, HEAD_PAIR, L), BF16),
        jax.ShapeDtypeStruct((B, ATTN_HEADS, L, HEAD_PAIR), BF16),
        jax.ShapeDtypeStruct((B, L, SSM_WIDTH), BF16),
        jax.ShapeDtypeStruct((B, ATTN_HEADS, L), F32),
        jax.ShapeDtypeStruct((B, ATTN_HEADS, LANES), F32),
        jax.ShapeDtypeStruct((B, ATTN_HEADS, LANES), F32),
    ) + tuple(jax.ShapeDtypeStruct(w.shape, BF16) for w in later)
    outs = pl.pallas_call(
        functools.partial(_head_kernel, n_later=len(later)),
        grid=(B, L // tm),
        in_specs=[tile(D), _const_spec(g1.shape), pl.BlockSpec(memory_space=pl.ANY),
                  pl.BlockSpec(memory_space=pl.ANY), pl.BlockSpec(memory_space=pl.ANY),
                  _const_spec(gm.shape),
                  _const_spec(w_in.shape), _const_spec(ws.shape), _const_spec(bf.shape)] + slabs,
        out_specs=(tile(D), tile(ATTN_WIDTH),
                   pl.BlockSpec((None, ATTN_HEADS, HEAD_PAIR, tm), lambda b, i: (b, 0, 0, i)),
                   pl.BlockSpec((None, ATTN_HEADS, tm, HEAD_PAIR), lambda b, i: (b, 0, i, 0)),
                   tile(SSM_WIDTH),
                   pl.BlockSpec((None, ATTN_HEADS, tm), lambda b, i: (b, 0, i)),
                   pl.BlockSpec((None, ATTN_HEADS, LANES), lambda b, i: (b, 0, 0)),
                   pl.BlockSpec((None, ATTN_HEADS, LANES), lambda b, i: (b, 0, 0)))
                  + tuple(slabs),
        out_shape=out_shape,
        scratch_shapes=[pltpu.VMEM((ATTN_HEADS, LANES), F32),
                        pltpu.VMEM(w1.shape, BF16), pltpu.VMEM(w3.shape, BF16),
                        pltpu.VMEM(w2.shape, BF16),
                        pltpu.VMEM((WEIGHT_STAGE_SLOTS, 2, w1.shape[0], FF_CHUNK), F32),
                        pltpu.VMEM((WEIGHT_STAGE_SLOTS, FF_CHUNK, w2.shape[1]), F32),
                        pltpu.SemaphoreType.DMA((WEIGHT_STAGE_SLOTS, 3))],
        compiler_params=pltpu.CompilerParams(
            dimension_semantics=("arbitrary", "arbitrary"),
            vmem_limit_bytes=HEAD_VMEM_LIMIT),
        name="head",
    )(x, g1, w1, w3, w2, gm, w_in, ws, bf, *later)
    return outs[:8], outs[8:]


def _attn_kernel(q_ref, *refs):
    _attn_prepare(0, q_ref, *refs)
    _attn_first_stage(0, q_ref, *refs)

    def q_tile(n, carry):
        _attn_q_tile(n, q_ref, *refs)
        return carry

    lax.fori_loop(0, q_ref.shape[0] // ATTN_TILE, q_tile, 0)


def _tile_start(i):
    return pl.multiple_of(i * ATTN_TILE, ATTN_TILE)


def _q_heads(q):
    lane = lax.broadcasted_iota(jnp.int32, (1, HEAD_PAIR), 1)
    first = lane < HEAD_DIM
    ones_even = jnp.where((lane >= HEAD_DIM) & (lane < HEAD_DIM + N_BIAS), 1.0, 0.0).astype(BF16)
    ones_odd = jnp.where(lane < N_BIAS, 1.0, 0.0).astype(BF16)
    return first, (jnp.where(first, q, ones_even), jnp.where(first, ones_odd, q))


def _attn_prepare(n, q_ref, kt_ref, v_ref, c_ref, kn_ref, cn_ref, o_ref, s_e0, s_e1, s_o0, s_o1, m_ref,
                  acc_ref, count_ref):
    t = ATTN_TILE
    q = q_ref[pl.ds(_tile_start(n), t), :]
    first, q_heads = _q_heads(q)
    q_sq = q.astype(F32) * q.astype(F32)
    row = lax.broadcasted_iota(jnp.int32, (t, t), 0)
    col = lax.broadcasted_iota(jnp.int32, (t, t), 1)
    tile_id = lax.broadcasted_iota(jnp.int32, (1, LANES), 1)
    needed = tile_id < 0
    for h, s_ref in enumerate((s_o0, s_o1)):
        s = jnp.where(col <= row, jnp.dot(q_heads[h], kt_ref[h, :, pl.ds(_tile_start(n), t)],
                                          preferred_element_type=F32), NEG_BIG)
        s_ref[...] = s
        m = jnp.max(s, axis=-1, keepdims=True)
        m_ref[h] = jnp.broadcast_to(m, (t, LANES))
        q_norm2 = jnp.max(jnp.sum(jnp.where(first == (h == 0), q_sq, 0.0), axis=-1, keepdims=True),
                          axis=0, keepdims=True)
        k_norm2, c_next = kn_ref[h:h + 1, :], cn_ref[h:h + 1, :]
        c_q = c_ref[h:h + 1, pl.ds(_tile_start(n), LANES)][:, 0:1]
        reach = jnp.sqrt(q_norm2 * k_norm2) * BOUND_SLACK_MUL + BOUND_SLACK_ADD + c_q - c_next
        needed = needed | (reach - jnp.min(m, axis=0, keepdims=True) >= -SKIP_LOG2)
    first_needed = jnp.min(jnp.where(needed & (tile_id < n), tile_id, n).astype(F32))
    count_ref[0] = n - first_needed.astype(jnp.int32)


def _attn_first_stage(n, q_ref, kt_ref, v_ref, c_ref, kn_ref, cn_ref, o_ref, s_e0, s_e1, s_o0, s_o1,
                      m_ref, acc_ref, count_ref):
    t = ATTN_TILE
    _, q_heads = _q_heads(q_ref[pl.ds(_tile_start(n), t), :])
    older = _tile_start(jnp.maximum(n - 1, 0))
    for h, s_ref in enumerate((s_e0, s_e1)):
        s_ref[...] = jnp.dot(q_heads[h], kt_ref[h, :, pl.ds(older, t)], preferred_element_type=F32)
    for h, s_ref in enumerate((s_o0, s_o1)):
        acc_ref[h] = jnp.dot(jnp.exp2(s_ref[...] - jnp.tile(m_ref[h], (1, t // LANES))).astype(BF16),
                             v_ref[h, pl.ds(_tile_start(n), t), :], preferred_element_type=F32)


def _attn_q_tile(n, q_ref, kt_ref, v_ref, c_ref, kn_ref, cn_ref, o_ref, s_e0, s_e1, s_o0, s_o1, m_ref,
                 acc_ref, count_ref):
    t = ATTN_TILE
    count = count_ref[0]
    start = _tile_start
    first, q_heads = _q_heads(q_ref[pl.ds(start(n), t), :])
    s_buf = ((s_e0, s_e1), (s_o0, s_o1))
    c_q = [c_ref[h:h + 1, pl.ds(start(n), LANES)][:, 0:1] for h in range(2)]

    def qk(h, kv):
        return jnp.dot(q_heads[h], kt_ref[h, :, pl.ds(start(kv), t)], preferred_element_type=F32)

    def consume(h, s, kv):
        d = c_ref[h:h + 1, pl.ds(start(kv), LANES)][:, 0:1] - c_q[h]
        m_old = m_ref[h]
        m_new = jnp.maximum(m_old, jnp.max(s, axis=-1, keepdims=True) - d)
        p = jnp.exp2(s - jnp.tile(m_new + d, (1, t // LANES))).astype(BF16)
        m_ref[h] = m_new
        acc_ref[h] = jnp.exp2(m_old - m_new) * acc_ref[h] + jnp.dot(
            p, v_ref[h, pl.ds(start(kv), t), :], preferred_element_type=F32)

    def step(kv, par):
        for h in range(2):
            s_buf[1 - par][h][...] = qk(h, kv - 1)
        for h in range(2):
            consume(h, s_buf[par][h][...], kv)

    def pair(i, carry):
        step(n - 1 - 2 * i, 0)
        step(n - 2 - 2 * i, 1)
        return carry

    n_pairs = jnp.maximum(count - 1, 0) // 2
    lax.fori_loop(0, n_pairs, pair, 0)
    left = count - 2 * n_pairs
    last = n - count

    def finish():
        acc0, acc1 = acc_ref[0], acc_ref[1]
        o_ref[pl.ds(start(n), t), :] = jnp.where(first, acc0 / acc0[:, HEAD_DIM:HEAD_DIM + 1],
                                                 acc1 / acc1[:, 0:1])
        nxt = jnp.minimum(n + 1, q_ref.shape[0] // t - 1)
        rest = (q_ref, kt_ref, v_ref, c_ref, kn_ref, cn_ref, o_ref, s_e0, s_e1, s_o0, s_o1, m_ref,
                acc_ref, count_ref)
        _attn_prepare(nxt, *rest)
        _attn_first_stage(nxt, *rest)

    @pl.when(left == 2)
    def _():
        step(last + 1, 0)
        for h in range(2):
            consume(h, s_buf[1][h][...], last)
        finish()

    @pl.when(left == 1)
    def _():
        for h in range(2):
            consume(h, s_buf[0][h][...], last)
        finish()

    @pl.when(left == 0)
    def _():
        finish()


def _attn_call(q, kt, v, ct, k_norm2, c_next):
    B, L, _ = q.shape
    t = ATTN_TILE
    assert L // t <= LANES
    by_pair = lambda a: a.reshape(B, N_PAIRS, 2, a.shape[-1])
    pair_block = lambda *shape: pl.BlockSpec((None, 2) + shape, lambda b, p: (b, p, 0, 0))
    pair_rows = lambda w: pl.BlockSpec((None, None, 2, w), lambda b, p: (b, p, 0, 0))
    lanes_of_pair = pl.BlockSpec((None, L, HEAD_PAIR), lambda b, p: (b, 0, p))
    return pl.pallas_call(
        _attn_kernel,
        grid=(B, N_PAIRS),
        in_specs=[lanes_of_pair, pair_block(HEAD_PAIR, L), pair_block(L, HEAD_PAIR),
                  pair_rows(L), pair_rows(LANES), pair_rows(LANES)],
        out_specs=lanes_of_pair,
        out_shape=jax.ShapeDtypeStruct((B, L, ATTN_WIDTH), F32),
        scratch_shapes=[pltpu.VMEM((t, t), F32)] * 4
                       + [pltpu.VMEM((2, t, LANES), F32), pltpu.VMEM((2, t, HEAD_PAIR), F32),
                          pltpu.SMEM((1,), jnp.int32)],
        compiler_params=pltpu.CompilerParams(
            dimension_semantics=("arbitrary", "arbitrary"),
            vmem_limit_bytes=VMEM_LIMIT),
        name="attn",
    )(q, kt, v, by_pair(ct), by_pair(k_norm2), by_pair(c_next))


def _gelu_tanh(x):
    return 0.5 * x * (1.0 + jnp.tanh(math.sqrt(2.0 / math.pi) * (x + 0.044715 * (x * x * x))))


def _cis(mag_arg, ang):
    mag = jnp.exp(mag_arg)
    return mag * jnp.cos(ang), mag * jnp.sin(ang)


def _cmul(ar, ai, br, bi):
    return ar * br - ai * bi, ar * bi + ai * br


def _cpow2(zr, zi, n):
    assert n & (n - 1) == 0
    while n > 1:
        zr, zi = zr * zr - zi * zi, 2.0 * zr * zi
        n //= 2
    return zr, zi


def _ssm_kernel(arow_ref, acol_ref, ldt_ref, bt_ref, cab_ref, dcol_ref, e_ref, f_ref, z_ref,
                *, chunks_per_seq):
    H = SSM_GROUP_CH

    @pl.when(pl.program_id(0) == 0)
    def _():
        z_ref[:, 0:SSM_CHUNK * H, :] = jnp.zeros((z_ref.shape[0], SSM_CHUNK * H, LANES), F32)

    for gi in range(SSM_GROUPS_PER_STEP):
        y = _ssm_group(arow_ref.at[gi], acol_ref.at[gi], ldt_ref.at[gi], bt_ref.at[gi], cab_ref.at[gi],
                       dcol_ref.at[gi], e_ref[:, gi * H:(gi + 1) * H, :], z_ref.at[gi], chunks_per_seq)
        f_ref[:, gi * H:(gi + 1) * H, :] = y.reshape(SSM_CHUNK, H, y.shape[-1])


def _ssm_group(arow_ref, acol_ref, ldt_ref, bt_ref, cab_ref, dcol_ref, e, z_ref, chunks_per_seq):
    T, P, H = SSM_CHUNK, SSM_STATE, SSM_GROUP_CH
    TH = T * H
    hi = lax.Precision.HIGHEST
    dt = jnp.exp(ldt_ref[...])

    lam_r, lam_i = dt * arow_ref[0:1, :], dt * arow_ref[1:2, :]
    j0 = lax.broadcasted_iota(jnp.int32, (T, 2 * P), 0).astype(F32)
    pa0, pb0 = _cis(j0 * lam_r, j0 * lam_i)
    pa1, pb1 = _cmul(pa0, pb0, *_cis(lam_r, lam_i))
    over_h = lambda a: jnp.concatenate(
        [jnp.broadcast_to(a[j:j + 1, :], (H, 2 * P)) for j in range(T)], axis=0)
    ca, cb = jnp.tile(cab_ref[0], (T, 1)), jnp.tile(cab_ref[1], (T, 1))
    c_pow0 = over_h(pa0) * ca + over_h(pb0) * cb
    c_pow1 = over_h(pa1) * ca + over_h(pb1) * cb

    a_r, a_i = acol_ref[:, 0:1], acol_ref[:, 1:2]
    lr, li = dt * a_r, dt * a_i
    abar_r, abar_i = _cis(lr, li)
    nr, ni = abar_r - 1.0, abar_i
    den = a_r * a_r + a_i * a_i
    fr, fi = (nr * a_r + ni * a_i) / den, (ni * a_r - nr * a_i) / den
    b_r, b_i = bt_ref[0], bt_ref[1]
    bb_r, bb_i = fr * b_r - fi * b_i, fr * b_i + fi * b_r

    kcol = jnp.dot(c_pow0, jnp.concatenate([bb_r, bb_i], axis=0), precision=hi,
                   preferred_element_type=F32)
    lane_h = lax.broadcasted_iota(jnp.int32, (H, LANES), 1) % H
    skip = jnp.where(lane_h == lax.broadcasted_iota(jnp.int32, (H, LANES), 0), dcol_ref[...], 0.0)

    z_ref[TH:2 * TH, :] = kcol
    z_ref[TH:TH + H, :] = kcol[:H] + skip
    lane_group = lax.broadcasted_iota(jnp.int32, (1, LANES), 1) // H
    groups_per_block = LANES // H
    blocks = []
    for v in range(TH // LANES):
        blk = None
        for u in range(groups_per_block):
            s = v * groups_per_block + u
            piece = z_ref[TH - H * s:2 * TH - H * s, :]
            blk = piece if blk is None else jnp.where(lane_group == u, piece, blk)
        blocks.append(blk.astype(BF16))
    mt = jnp.concatenate(blocks, axis=1)

    expo = (groups_per_block - 1 - lane_group).astype(F32)
    wr, wi = _cis(lr * expo, li * expo)
    hop_r, hop_i = _cpow2(abar_r, abar_i, groups_per_block)
    w1_r, w1_i = [], []
    for v in range(TH // LANES):
        w1_r.insert(0, wr * bb_r - wi * bb_i)
        w1_i.insert(0, wr * bb_i + wi * bb_r)
        wr, wi = _cmul(wr, wi, hop_r, hop_i)
    w1t = jnp.concatenate([jnp.concatenate(w1_r, axis=1),
                           jnp.concatenate(w1_i, axis=1)], axis=0).astype(BF16)

    e = e.reshape(TH, e.shape[-1])
    y = jnp.dot(mt, e, preferred_element_type=F32)
    st = jnp.dot(w1t, e, preferred_element_type=F32)
    sr, si = st[:P], st[P:]
    pos = lax.broadcasted_iota(jnp.int32, sr.shape, 1) % chunks_per_seq

    def shifted(a, shift):
        return jnp.where(pos >= shift, pltpu.roll(a, shift, 1), 0.0)

    qr, qi = _cpow2(abar_r, abar_i, T)
    shift = 1
    while shift < chunks_per_seq:
        srs, sis = shifted(sr, shift), shifted(si, shift)
        sr, si = sr + qr * srs - qi * sis, si + qr * sis + qi * srs
        qr, qi = qr * qr - qi * qi, 2.0 * qr * qi
        shift *= 2
    x_prev = jnp.concatenate([shifted(sr, 1), shifted(si, 1)], axis=0).astype(BF16)
    y = y + jnp.dot(c_pow1.astype(BF16), x_prev, preferred_element_type=F32)
    return _gelu_tanh(y)


def _ssm_call(arow, acol, ldt, bt, cab, dcol, e, chunks_per_seq):
    T, _, NC = e.shape
    G = arow.shape[0]
    gps = SSM_GROUPS_PER_STEP
    assert G % gps == 0
    grp = lambda a: pl.BlockSpec((gps,) + a.shape[1:], lambda g: (g,) + (0,) * (a.ndim - 1))
    channels = pl.BlockSpec((T, gps * SSM_GROUP_CH, NC), lambda g: (0, g, 0))
    return pl.pallas_call(
        functools.partial(_ssm_kernel, chunks_per_seq=chunks_per_seq),
        grid=(G // gps,),
        in_specs=[grp(a) for a in (arow, acol, ldt, bt, cab, dcol)] + [channels],
        out_specs=channels,
        out_shape=jax.ShapeDtypeStruct((T, SSM_WIDTH, NC), F32),
        scratch_shapes=[pltpu.VMEM((gps, 2 * T * SSM_GROUP_CH, LANES), F32)],
        compiler_params=pltpu.CompilerParams(
            dimension_semantics=("arbitrary",), vmem_limit_bytes=VMEM_LIMIT),
        name="ssm",
    )(arow, acol, ldt, bt, cab, dcol, e)


def _ssm_param_layouts(a_re, a_im, log_dt, b_re, b_im, c_re, c_im, d_skip):
    G = a_re.shape[0]
    arow = jnp.stack([jnp.concatenate([a_re, a_re], -1), jnp.concatenate([a_im, a_im], -1)], 1)
    acol = jnp.stack([a_re, a_im], -1)
    reps = LANES // SSM_GROUP_CH
    bt = jnp.stack([jnp.tile(b_re, (1, 1, reps)), jnp.tile(b_im, (1, 1, reps))], 1)
    cab = jnp.stack([jnp.concatenate([c_re, -c_im], -1), jnp.concatenate([-c_im, -c_re], -1)], 1)
    return (arow.astype(F32), acol.astype(F32), log_dt.reshape(G, 1, 1).astype(F32),
            bt.astype(F32), cab.astype(F32), d_skip.reshape(G, SSM_GROUP_CH, 1).astype(F32))


def _tail_kernel(*refs):
    first = (pl.program_id(0) == 0) & (pl.program_id(1) == 0)

    @pl.when(first)
    def _():
        _tail_body(*refs, streaming=True)

    @pl.when(jnp.logical_not(first))
    def _():
        _tail_body(*refs, streaming=False)


def _tail_body(h1_ref, attn_ref, y_ref, p_ref, wglu_ref, bglu_ref, ga_ref, gs_ref,
               wo_ref, g2_ref, w1_hbm, w3_hbm, w2_hbm, gp_ref, wpg_ref,
               wpp_ref, gf_ref, o_ref, w1_ref, w3_ref, w2_ref, sem, *, streaming):
    def copies(c):
        sl = pl.ds(c * FF_CHUNK, FF_CHUNK)
        return (pltpu.make_async_copy(w1_hbm.at[:, sl], w1_ref.at[:, sl], sem.at[c, 0]),
                pltpu.make_async_copy(w3_hbm.at[:, sl], w3_ref.at[:, sl], sem.at[c, 1]),
                pltpu.make_async_copy(w2_hbm.at[sl, :], w2_ref.at[sl, :], sem.at[c, 2]))

    resident = _resident_weights(w1_ref, w3_ref, w2_ref)
    if streaming:
        for c in range(D_FF // FF_CHUNK):
            for cp in copies(c):
                cp.start()

        def weights(c):
            for cp in copies(c):
                cp.wait()
            return resident(c)
    else:
        weights = resident

    y = y_ref[...]
    glu = y * _sigmoid(jnp.dot(y.astype(BF16), wglu_ref[...], preferred_element_type=F32)
                       + bglu_ref[...])
    an = _rms(attn_ref[...], ga_ref[...]).astype(BF16)
    sn = _rms(glu, gs_ref[...]).astype(BF16)
    h = (h1_ref[...] + jnp.dot(an, wo_ref[:ATTN_WIDTH, :], preferred_element_type=F32)
         + jnp.dot(sn, wo_ref[ATTN_WIDTH:, :], preferred_element_type=F32))
    h = h + 0.5 * _swiglu(_rms(h, g2_ref[...]).astype(BF16), weights)
    gate = _sigmoid(jnp.dot(_rms(h, gp_ref[...]).astype(BF16), wpg_ref[...],
                            preferred_element_type=F32))
    h = h + gate * jnp.dot(p_ref[...].astype(BF16), wpp_ref[...], preferred_element_type=F32)
    o_ref[...] = _rms(h, gf_ref[...])


def _tail_call(h1, attn, y, p, *consts):
    B, L, D = h1.shape
    tm = TOKEN_TILE
    tile = lambda w: pl.BlockSpec((None, tm, w), lambda b, i: (b, i, 0))
    ffn = (6, 7, 8)
    return pl.pallas_call(
        _tail_kernel,
        grid=(B, L // tm),
        in_specs=[tile(D), tile(ATTN_WIDTH), tile(SSM_WIDTH), tile(PLE_DIM)]
                 + [pl.BlockSpec(memory_space=pl.ANY) if i in ffn else _const_spec(c.shape)
                    for i, c in enumerate(consts)],
        out_specs=tile(D),
        out_shape=jax.ShapeDtypeStruct((B, L, D), F32),
        scratch_shapes=[pltpu.VMEM(consts[i].shape, BF16) for i in ffn]
                       + [pltpu.SemaphoreType.DMA((D_FF // FF_CHUNK, 3))],
        compiler_params=pltpu.CompilerParams(
            dimension_semantics=("arbitrary", "arbitrary"),
            vmem_limit_bytes=VMEM_LIMIT),
        name="tail",
    )(h1, attn, y, p, *consts)


def kernel(x, p, g_ffn1, w1_a, w3_a, w2_a, g_mix, w_in, b_f, a_re, a_im, log_dt, b_re, b_im, c_re, c_im, d_skip, w_glu, b_glu, g_attn_out, g_ssm_out, w_out, g_ffn2, w1_b, w3_b, w2_b, g_ple, w_ple_gate, w_ple_proj, g_final):
    B, L, D = x.shape
    assert D == D_MODEL and L % ATTN_TILE == 0 and L % TOKEN_TILE == 0 and L % SSM_CHUNK == 0
    assert g_ffn1.shape[0] == 1, "single layer"
    assert TOKEN_TILE == ATTN_TILE, "decay bias rows are relative to the kv tile start"
    row = lambda g: g.reshape(1, -1).astype(F32)
    bf = lambda w: w.astype(BF16)
    w_in16 = bf(w_in[0])
    assert w_in16.shape[1] == 3 * ATTN_WIDTH + ATTN_HEADS + SSM_WIDTH

    (h1, q, kt, v, s_in, ct, k_norm2, c_next), (w1_b16, w3_b16, w2_b16, w_out16, w_gate16) = _head_call(
        x, row(g_ffn1[0]), w1_a[0], w3_a[0], w2_a[0], row(g_mix[0]),
        w_in16, w_in16[:, 3 * ATTN_WIDTH + ATTN_HEADS:],
        b_f[0].reshape(ATTN_HEADS, 1).astype(F32),
        later=tuple(w.astype(F32) for w in (w1_b[0], w3_b[0], w2_b[0], w_out[0], w_ple_gate[0])))

    T = SSM_CHUNK
    chunks_per_seq = L // T
    n_chunks = B * chunks_per_seq
    e = s_in.reshape(n_chunks, T, SSM_WIDTH).transpose(1, 2, 0)
    f = _ssm_call(*_ssm_param_layouts(a_re[0], a_im[0], log_dt[0], b_re[0], b_im[0],
                                      c_re[0], c_im[0], d_skip[0]),
                  e, chunks_per_seq)
    y = f.transpose(2, 0, 1).reshape(B, L, SSM_WIDTH)

    attn = _attn_call(q, kt, v, ct, k_norm2, c_next)

    return _tail_call(
        h1, attn, y, p[0],
        bf(w_glu[0]), row(b_glu[0]), row(g_attn_out[0]), row(g_ssm_out[0]),
        w_out16, row(g_ffn2[0]),
        w1_b16, w3_b16, w2_b16, row(g_ple[0]), w_gate16,
        bf(w_ple_proj[0]), row(g_final))
```

```python
import functools
import math

import jax
import jax.numpy as jnp
from jax import lax
from jax.experimental import pallas as pl
from jax.experimental.pallas import tpu as pltpu

D_MODEL = 1024
ATTN_HEADS = 8
HEAD_DIM = 64
ATTN_WIDTH = ATTN_HEADS * HEAD_DIM
SSM_WIDTH = D_MODEL - ATTN_WIDTH
SSM_GROUP_CH = 16
SSM_GROUPS = SSM_WIDTH // SSM_GROUP_CH
SSM_STATE = 64
D_FF = 2816
PLE_DIM = 256
EPS = 1e-6

LANES = 128
BF16_SUBLANES = 16
HEAD_PAIR = 2 * HEAD_DIM
N_PAIRS = ATTN_HEADS // 2
FF_CHUNK = 256
TOKEN_TILE = 512
WEIGHT_STAGE_SLOTS = 2
ATTN_TILE = 512
SSM_CHUNK = 32
SSM_GROUPS_PER_STEP = 2
NEG_BIG = -1e30
SKIP_LOG2 = 140.0
BOUND_SLACK_MUL = 1.001
BOUND_SLACK_ADD = 1.0
LOG2E = math.log2(math.e)
N_BIAS = 3
BIAS_ROWS = 8
VMEM_LIMIT = 56 * 1024 * 1024
HEAD_VMEM_LIMIT = 60 * 1024 * 1024

BF16 = jnp.bfloat16
F32 = jnp.float32


def _rms(x, g):
    ms = jnp.mean(x * x, axis=-1, keepdims=True)
    return x * lax.rsqrt(ms + EPS) * g


def _sigmoid(x):
    return 1.0 / (1.0 + jnp.exp(-x))


def _swiglu(xn, weights):
    acc = None
    for c in range(D_FF // FF_CHUNK):
        w1c, w3c, w2c = weights(c)
        a = jnp.dot(xn, w1c, preferred_element_type=F32)
        b = jnp.dot(xn, w3c, preferred_element_type=F32)
        gated = (a * _sigmoid(a) * b).astype(BF16)
        part = jnp.dot(gated, w2c, preferred_element_type=F32)
        acc = part if acc is None else acc + part
    return acc


def _resident_weights(w1_ref, w3_ref, w2_ref):
    def weights(c):
        sl = slice(c * FF_CHUNK, (c + 1) * FF_CHUNK)
        return w1_ref[:, sl], w3_ref[:, sl], w2_ref[sl, :]
    return weights


def _const_spec(shape):
    nd = len(shape)
    return pl.BlockSpec(shape, lambda *_: (0,) * nd, pipeline_mode=pl.Buffered(1))


def _head_kernel(*refs, n_later):
    first = (pl.program_id(0) == 0) & (pl.program_id(1) == 0)

    @pl.when(first)
    def _():
        _head_body(*refs, n_later=n_later, streaming=True)

    @pl.when(jnp.logical_not(first))
    def _():
        _head_body(*refs, n_later=n_later, streaming=False)


def _head_body(*refs, n_later, streaming):
    (x_ref, g1_ref, w1_hbm, w3_hbm, w2_hbm, gm_ref, win_hbm, ws_hbm, bf_ref) = refs[:9]
    later_in = refs[9:9 + n_later]
    h1_ref, q_ref, kt_ref, v_ref, s_ref, ct_ref, kn_ref, cn_ref = refs[9 + n_later:17 + n_later]
    later_out = refs[17 + n_later:17 + 2 * n_later]
    carry_ref, w1_ref, w3_ref, w2_ref, stage_a, stage_b, sem, win_ref, ws_ref, sem_p = refs[-10:]
    n_chunks = D_FF // FF_CHUNK
    slots = stage_b.shape[0]
    kvf = slice(ATTN_WIDTH, 3 * ATTN_WIDTH + LANES)
    proj_copies = (
        pltpu.make_async_copy(win_hbm.at[:, kvf], win_ref.at[:, kvf], sem_p.at[0]),
        pltpu.make_async_copy(win_hbm.at[:, :ATTN_WIDTH], win_ref.at[:, :ATTN_WIDTH], sem_p.at[1]),
        pltpu.make_async_copy(ws_hbm, ws_ref, sem_p.at[2]))

    def proj_wait(i):
        if streaming:
            proj_copies[i].wait()

    def copies(c):
        sl, s = pl.ds(c * FF_CHUNK, FF_CHUNK), c % slots
        return (pltpu.make_async_copy(w1_hbm.at[:, sl], stage_a.at[s, 0], sem.at[s, 0]),
                pltpu.make_async_copy(w3_hbm.at[:, sl], stage_a.at[s, 1], sem.at[s, 1]),
                pltpu.make_async_copy(w2_hbm.at[sl, :], stage_b.at[s], sem.at[s, 2]))

    def streamed_weights(c):
        if c + slots - 1 < n_chunks:
            for cp in copies(c + slots - 1):
                cp.start()
        if c + slots - 1 == n_chunks - 1:
            for cp in proj_copies:
                cp.start()
        for cp in copies(c):
            cp.wait()
        sl, s = slice(c * FF_CHUNK, (c + 1) * FF_CHUNK), c % slots
        w1_ref[:, sl] = stage_a[s, 0].astype(BF16)
        w3_ref[:, sl] = stage_a[s, 1].astype(BF16)
        w2_ref[sl, :] = stage_b[s].astype(BF16)
        return w1_ref[:, sl], w3_ref[:, sl], w2_ref[sl, :]

    if streaming:
        for c in range(slots - 1):
            for cp in copies(c):
                cp.start()
        weights = streamed_weights
    else:
        weights = _resident_weights(w1_ref, w3_ref, w2_ref)

    @pl.when(pl.program_id(1) == 0)
    def _():
        carry_ref[...] = jnp.zeros_like(carry_ref)
        kn_ref[...] = jnp.zeros_like(kn_ref)
        cn_ref[...] = jnp.zeros_like(cn_ref)

    for src, dst in zip(later_in, later_out):
        dst[...] = src[...].astype(BF16)
    tm = x_ref.shape[0]
    x = x_ref[...]
    h1 = x + 0.5 * _swiglu(_rms(x, g1_ref[...]).astype(BF16), weights)
    h1_ref[...] = h1
    un = _rms(h1, gm_ref[...]).astype(BF16)
    project = lambda w: jnp.dot(un, w, preferred_element_type=F32)
    proj_wait(0)
    zf = project(win_ref[:, 3 * ATTN_WIDTH:3 * ATTN_WIDTH + LANES])
    kv = project(win_ref[:, ATTN_WIDTH:3 * ATTN_WIDTH])
    zft = zf.T[:ATTN_HEADS, :] + bf_ref[...]
    logf = jnp.minimum(zft, 0.0) - jnp.log1p(jnp.exp(-jnp.abs(zft)))
    lane = lax.broadcasted_iota(jnp.int32, logf.shape, 1)
    c = logf
    shift = 1
    while shift < tm:
        c = c + jnp.where(lane >= shift, pltpu.roll(c, shift, 1), 0.0)
        shift *= 2

    c_abs = c + carry_ref[:, 0:1]
    ct_ref[...] = c_abs * LOG2E
    carry_ref[...] = jnp.broadcast_to(c_abs[:, tm - 1:tm], carry_ref.shape)

    rel = (c - c[:, 0:1]) * LOG2E
    hi = rel.astype(BF16).astype(F32)
    mid = (rel - hi).astype(BF16).astype(F32)
    lo = (rel - hi - mid).astype(BF16).astype(F32)
    kt = kv[:, :ATTN_WIDTH].astype(BF16).astype(F32).T
    vv = kv[:, ATTN_WIDTH:]
    k_sq = (kt * kt).reshape(ATTN_HEADS, HEAD_DIM, tm)
    tile = pl.program_id(1)
    stat_lane = lax.broadcasted_iota(jnp.int32, kn_ref.shape, 1)

    kn_ref[...] = jnp.where(stat_lane == tile,
                            jnp.max(jnp.sum(k_sq, axis=1), axis=-1, keepdims=True), kn_ref[...])
    cn_ref[...] = jnp.where(stat_lane == tile - 1, c_abs[:, 0:1] * LOG2E, cn_ref[...])
    sub = lax.broadcasted_iota(jnp.int32, (BIAS_ROWS, tm), 0)
    zeros = jnp.zeros((HEAD_DIM - BIAS_ROWS, tm), F32)
    vlane = lax.broadcasted_iota(jnp.int32, (tm, HEAD_PAIR), 1)
    for h in range(ATTN_HEADS):
        bias = jnp.where(sub == 0, -hi[h:h + 1],
                         jnp.where(sub == 1, -mid[h:h + 1],
                                   jnp.where(sub == 2, -lo[h:h + 1], 0.0)))
        k_h = kt[h * HEAD_DIM:(h + 1) * HEAD_DIM]
        vp = vv[:, (h // 2) * HEAD_PAIR:(h // 2 + 1) * HEAD_PAIR]
        if h % 2 == 0:
            kt_ref[h] = jnp.concatenate([k_h, bias, zeros], axis=0).astype(BF16)
            v_ref[h] = jnp.where(vlane < HEAD_DIM, vp,
                                 jnp.where(vlane == HEAD_DIM, 1.0, 0.0)).astype(BF16)
        else:
            kt_ref[h] = jnp.concatenate([bias, zeros, k_h], axis=0).astype(BF16)
            v_ref[h] = jnp.where(vlane >= HEAD_DIM, vp,
                                 jnp.where(vlane == 0, 1.0, 0.0)).astype(BF16)

    proj_wait(1)
    q_ref[...] = (project(win_ref[:, :ATTN_WIDTH]) * (LOG2E / math.sqrt(HEAD_DIM))).astype(BF16)
    proj_wait(2)
    s_ref[...] = project(ws_ref[...]).astype(BF16)


def _slab_spec(w, n_steps, steps_per_batch):
    rows = next(r for r in range(BF16_SUBLANES, w.shape[0] + 1, BF16_SUBLANES)
                if w.shape[0] % r == 0 and w.shape[0] // r <= n_steps)
    last = w.shape[0] // rows - 1
    return pl.BlockSpec((rows, w.shape[1]),
                        lambda b, i: (jnp.minimum(b * steps_per_batch + i, last), 0))


def _head_call(x, g1, w1, w3, w2, gm, w_in, ws, bf, later):
    B, L, D = x.shape
    tm = TOKEN_TILE
    tile = lambda w: pl.BlockSpec((None, tm, w), lambda b, i: (b, i, 0))
    slabs = [_slab_spec(w, B * (L // tm), L // tm) for w in later]
    out_shape = (
        jax.ShapeDtypeStruct((B, L, D), F32),
        jax.ShapeDtypeStruct((B, L, ATTN_WIDTH), BF16),
        jax.ShapeDtypeStruct((B, ATTN_HEADS, HEAD_PAIR, L), BF16),
        jax.ShapeDtypeStruct((B, ATTN_HEADS, L, HEAD_PAIR), BF16),
        jax.ShapeDtypeStruct((B, L, SSM_WIDTH), BF16),
        jax.ShapeDtypeStruct((B, ATTN_HEADS, L), F32),
        jax.ShapeDtypeStruct((B, ATTN_HEADS, LANES), F32),
        jax.ShapeDtypeStruct((B, ATTN_HEADS, LANES), F32),
    ) + tuple(jax.ShapeDtypeStruct(w.shape, BF16) for w in later)
    outs = pl.pallas_call(
        functools.partial(_head_kernel, n_later=len(later)),
        grid=(B, L // tm),
        in_specs=[tile(D), _const_spec(g1.shape), pl.BlockSpec(memory_space=pl.ANY),
                  pl.BlockSpec(memory_space=pl.ANY), pl.BlockSpec(memory_space=pl.ANY),
                  _const_spec(gm.shape), pl.BlockSpec(memory_space=pl.ANY),
                  pl.BlockSpec(memory_space=pl.ANY), _const_spec(bf.shape)] + slabs,
        out_specs=(tile(D), tile(ATTN_WIDTH),
                   pl.BlockSpec((None, ATTN_HEADS, HEAD_PAIR, tm), lambda b, i: (b, 0, 0, i)),
                   pl.BlockSpec((None, ATTN_HEADS, tm, HEAD_PAIR), lambda b, i: (b, 0, i, 0)),
                   tile(SSM_WIDTH),
                   pl.BlockSpec((None, ATTN_HEADS, tm), lambda b, i: (b, 0, i)),
                   pl.BlockSpec((None, ATTN_HEADS, LANES), lambda b, i: (b, 0, 0)),
                   pl.BlockSpec((None, ATTN_HEADS, LANES), lambda b, i: (b, 0, 0)))
                  + tuple(slabs),
        out_shape=out_shape,
        scratch_shapes=[pltpu.VMEM((ATTN_HEADS, LANES), F32),
                        pltpu.VMEM(w1.shape, BF16), pltpu.VMEM(w3.shape, BF16),
                        pltpu.VMEM(w2.shape, BF16),
                        pltpu.VMEM((WEIGHT_STAGE_SLOTS, 2, w1.shape[0], FF_CHUNK), F32),
                        pltpu.VMEM((WEIGHT_STAGE_SLOTS, FF_CHUNK, w2.shape[1]), F32),
                        pltpu.SemaphoreType.DMA((WEIGHT_STAGE_SLOTS, 3)),
                        pltpu.VMEM(w_in.shape, BF16), pltpu.VMEM(ws.shape, BF16),
                        pltpu.SemaphoreType.DMA((3,))],
        compiler_params=pltpu.CompilerParams(
            dimension_semantics=("arbitrary", "arbitrary"),
            vmem_limit_bytes=HEAD_VMEM_LIMIT),
        name="head",
    )(x, g1, w1, w3, w2, gm, w_in, ws, bf, *later)
    return outs[:8], outs[8:]


def _attn_kernel(q_ref, *refs):
    _attn_prepare(0, q_ref, *refs)
    _attn_first_stage(0, q_ref, *refs)

    def q_tile(n, carry):
        _attn_q_tile(n, q_ref, *refs)
        return carry

    lax.fori_loop(0, q_ref.shape[0] // ATTN_TILE, q_tile, 0)


def _tile_start(i):
    return pl.multiple_of(i * ATTN_TILE, ATTN_TILE)


def _q_heads(q):
    lane = lax.broadcasted_iota(jnp.int32, (1, HEAD_PAIR), 1)
    first = lane < HEAD_DIM
    ones_even = jnp.where((lane >= HEAD_DIM) & (lane < HEAD_DIM + N_BIAS), 1.0, 0.0).astype(BF16)
    ones_odd = jnp.where(lane < N_BIAS, 1.0, 0.0).astype(BF16)
    return first, (jnp.where(first, q, ones_even), jnp.where(first, ones_odd, q))


def _attn_prepare(n, q_ref, kt_ref, v_ref, c_ref, kn_ref, cn_ref, o_ref, s_e0, s_e1, s_o0, s_o1, m_ref,
                  acc_ref, count_ref):
    t = ATTN_TILE
    q = q_ref[pl.ds(_tile_start(n), t), :]
    first, q_heads = _q_heads(q)
    q_sq = q.astype(F32) * q.astype(F32)
    row = lax.broadcasted_iota(jnp.int32, (t, t), 0)
    col = lax.broadcasted_iota(jnp.int32, (t, t), 1)
    tile_id = lax.broadcasted_iota(jnp.int32, (1, LANES), 1)
    needed = tile_id < 0
    for h, s_ref in enumerate((s_o0, s_o1)):
        s = jnp.where(col <= row, jnp.dot(q_heads[h], kt_ref[h, :, pl.ds(_tile_start(n), t)],
                                          preferred_element_type=F32), NEG_BIG)
        s_ref[...] = s
        m = jnp.max(s, axis=-1, keepdims=True)
        m_ref[h] = jnp.broadcast_to(m, (t, LANES))
        q_norm2 = jnp.max(jnp.sum(jnp.where(first == (h == 0), q_sq, 0.0), axis=-1, keepdims=True),
                          axis=0, keepdims=True)
        k_norm2, c_next = kn_ref[h:h + 1, :], cn_ref[h:h + 1, :]
        c_q = c_ref[h:h + 1, pl.ds(_tile_start(n), LANES)][:, 0:1]
        reach = jnp.sqrt(q_norm2 * k_norm2) * BOUND_SLACK_MUL + BOUND_SLACK_ADD + c_q - c_next
        needed = needed | (reach - jnp.min(m, axis=0, keepdims=True) >= -SKIP_LOG2)
    first_needed = jnp.min(jnp.where(needed & (tile_id < n), tile_id, n).astype(F32))
    count_ref[0] = n - first_needed.astype(jnp.int32)


def _attn_first_stage(n, q_ref, kt_ref, v_ref, c_ref, kn_ref, cn_ref, o_ref, s_e0, s_e1, s_o0, s_o1,
                      m_ref, acc_ref, count_ref):
    t = ATTN_TILE
    _, q_heads = _q_heads(q_ref[pl.ds(_tile_start(n), t), :])
    older = _tile_start(jnp.maximum(n - 1, 0))
    for h, s_ref in enumerate((s_e0, s_e1)):
        s_ref[...] = jnp.dot(q_heads[h], kt_ref[h, :, pl.ds(older, t)], preferred_element_type=F32)
    for h, s_ref in enumerate((s_o0, s_o1)):
        acc_ref[h] = jnp.dot(jnp.exp2(s_ref[...] - jnp.tile(m_ref[h], (1, t // LANES))).astype(BF16),
                             v_ref[h, pl.ds(_tile_start(n), t), :], preferred_element_type=F32)


def _attn_q_tile(n, q_ref, kt_ref, v_ref, c_ref, kn_ref, cn_ref, o_ref, s_e0, s_e1, s_o0, s_o1, m_ref,
                 acc_ref, count_ref):
    t = ATTN_TILE
    count = count_ref[0]
    start = _tile_start
    first, q_heads = _q_heads(q_ref[pl.ds(start(n), t), :])
    s_buf = ((s_e0, s_e1), (s_o0, s_o1))
    c_q = [c_ref[h:h + 1, pl.ds(start(n), LANES)][:, 0:1] for h in range(2)]

    def qk(h, kv):
        return jnp.dot(q_heads[h], kt_ref[h, :, pl.ds(start(kv), t)], preferred_element_type=F32)

    def consume(h, s, kv):
        d = c_ref[h:h + 1, pl.ds(start(kv), LANES)][:, 0:1] - c_q[h]
        m_old = m_ref[h]
        m_new = jnp.maximum(m_old, jnp.max(s, axis=-1, keepdims=True) - d)
        p = jnp.exp2(s - jnp.tile(m_new + d, (1, t // LANES))).astype(BF16)
        m_ref[h] = m_new
        acc_ref[h] = jnp.exp2(m_old - m_new) * acc_ref[h] + jnp.dot(
            p, v_ref[h, pl.ds(start(kv), t), :], preferred_element_type=F32)

    def step(kv, par):
        for h in range(2):
            s_buf[1 - par][h][...] = qk(h, kv - 1)
        for h in range(2):
            consume(h, s_buf[par][h][...], kv)

    def pair(i, carry):
        step(n - 1 - 2 * i, 0)
        step(n - 2 - 2 * i, 1)
        return carry

    n_pairs = jnp.maximum(count - 1, 0) // 2
    lax.fori_loop(0, n_pairs, pair, 0)
    left = count - 2 * n_pairs
    last = n - count

    def finish():
        acc0, acc1 = acc_ref[0], acc_ref[1]
        o_ref[pl.ds(start(n), t), :] = jnp.where(first, acc0 / acc0[:, HEAD_DIM:HEAD_DIM + 1],
                                                 acc1 / acc1[:, 0:1])
        nxt = jnp.minimum(n + 1, q_ref.shape[0] // t - 1)
        rest = (q_ref, kt_ref, v_ref, c_ref, kn_ref, cn_ref, o_ref, s_e0, s_e1, s_o0, s_o1, m_ref,
                acc_ref, count_ref)
        _attn_prepare(nxt, *rest)
        _attn_first_stage(nxt, *rest)

    @pl.when(left == 2)
    def _():
        step(last + 1, 0)
        for h in range(2):
            consume(h, s_buf[1][h][...], last)
        finish()

    @pl.when(left == 1)
    def _():
        for h in range(2):
            consume(h, s_buf[0][h][...], last)
        finish()

    @pl.when(left == 0)
    def _():
        finish()


def _attn_call(q, kt, v, ct, k_norm2, c_next):
    B, L, _ = q.shape
    t = ATTN_TILE
    assert L // t <= LANES
    by_pair = lambda a: a.reshape(B, N_PAIRS, 2, a.shape[-1])
    pair_block = lambda *shape: pl.BlockSpec((None, 2) + shape, lambda b, p: (b, p, 0, 0))
    pair_rows = lambda w: pl.BlockSpec((None, None, 2, w), lambda b, p: (b, p, 0, 0))
    lanes_of_pair = pl.BlockSpec((None, L, HEAD_PAIR), lambda b, p: (b, 0, p))
    return pl.pallas_call(
        _attn_kernel,
        grid=(B, N_PAIRS),
        in_specs=[lanes_of_pair, pair_block(HEAD_PAIR, L), pair_block(L, HEAD_PAIR),
                  pair_rows(L), pair_rows(LANES), pair_rows(LANES)],
        out_specs=lanes_of_pair,
        out_shape=jax.ShapeDtypeStruct((B, L, ATTN_WIDTH), F32),
        scratch_shapes=[pltpu.VMEM((t, t), F32)] * 4
                       + [pltpu.VMEM((2, t, LANES), F32), pltpu.VMEM((2, t, HEAD_PAIR), F32),
                          pltpu.SMEM((1,), jnp.int32)],
        compiler_params=pltpu.CompilerParams(
            dimension_semantics=("arbitrary", "arbitrary"),
            vmem_limit_bytes=VMEM_LIMIT),
        name="attn",
    )(q, kt, v, by_pair(ct), by_pair(k_norm2), by_pair(c_next))


def _gelu_tanh(x):
    return 0.5 * x * (1.0 + jnp.tanh(math.sqrt(2.0 / math.pi) * (x + 0.044715 * (x * x * x))))


def _cis(mag_arg, ang):
    mag = jnp.exp(mag_arg)
    return mag * jnp.cos(ang), mag * jnp.sin(ang)


def _cmul(ar, ai, br, bi):
    return ar * br - ai * bi, ar * bi + ai * br


def _cpow2(zr, zi, n):
    assert n & (n - 1) == 0
    while n > 1:
        zr, zi = zr * zr - zi * zi, 2.0 * zr * zi
        n //= 2
    return zr, zi


def _ssm_kernel(arow_ref, acol_ref, ldt_ref, bt_ref, cab_ref, dcol_ref, e_ref, f_ref, z_ref,
                *, chunks_per_seq):
    H = SSM_GROUP_CH

    @pl.when(pl.program_id(0) == 0)
    def _():
        z_ref[:, 0:SSM_CHUNK * H, :] = jnp.zeros((z_ref.shape[0], SSM_CHUNK * H, LANES), F32)

    for gi in range(SSM_GROUPS_PER_STEP):
        y = _ssm_group(arow_ref.at[gi], acol_ref.at[gi], ldt_ref.at[gi], bt_ref.at[gi], cab_ref.at[gi],
                       dcol_ref.at[gi], e_ref[:, gi * H:(gi + 1) * H, :], z_ref.at[gi], chunks_per_seq)
        f_ref[:, gi * H:(gi + 1) * H, :] = y.reshape(SSM_CHUNK, H, y.shape[-1])


def _ssm_group(arow_ref, acol_ref, ldt_ref, bt_ref, cab_ref, dcol_ref, e, z_ref, chunks_per_seq):
    T, P, H = SSM_CHUNK, SSM_STATE, SSM_GROUP_CH
    TH = T * H
    hi = lax.Precision.HIGHEST
    dt = jnp.exp(ldt_ref[...])

    lam_r, lam_i = dt * arow_ref[0:1, :], dt * arow_ref[1:2, :]
    j0 = lax.broadcasted_iota(jnp.int32, (T, 2 * P), 0).astype(F32)
    pa0, pb0 = _cis(j0 * lam_r, j0 * lam_i)
    pa1, pb1 = _cmul(pa0, pb0, *_cis(lam_r, lam_i))
    over_h = lambda a: jnp.concatenate(
        [jnp.broadcast_to(a[j:j + 1, :], (H, 2 * P)) for j in range(T)], axis=0)
    ca, cb = jnp.tile(cab_ref[0], (T, 1)), jnp.tile(cab_ref[1], (T, 1))
    c_pow0 = over_h(pa0) * ca + over_h(pb0) * cb
    c_pow1 = over_h(pa1) * ca + over_h(pb1) * cb

    a_r, a_i = acol_ref[:, 0:1], acol_ref[:, 1:2]
    lr, li = dt * a_r, dt * a_i
    abar_r, abar_i = _cis(lr, li)
    nr, ni = abar_r - 1.0, abar_i
    den = a_r * a_r + a_i * a_i
    fr, fi = (nr * a_r + ni * a_i) / den, (ni * a_r - nr * a_i) / den
    b_r, b_i = bt_ref[0], bt_ref[1]
    bb_r, bb_i = fr * b_r - fi * b_i, fr * b_i + fi * b_r

    kcol = jnp.dot(c_pow0, jnp.concatenate([bb_r, bb_i], axis=0), precision=hi,
                   preferred_element_type=F32)
    lane_h = lax.broadcasted_iota(jnp.int32, (H, LANES), 1) % H
    skip = jnp.where(lane_h == lax.broadcasted_iota(jnp.int32, (H, LANES), 0), dcol_ref[...], 0.0)

    z_ref[TH:2 * TH, :] = kcol
    z_ref[TH:TH + H, :] = kcol[:H] + skip
    lane_group = lax.broadcasted_iota(jnp.int32, (1, LANES), 1) // H
    groups_per_block = LANES // H
    blocks = []
    for v in range(TH // LANES):
        blk = None
        for u in range(groups_per_block):
            s = v * groups_per_block + u
            piece = z_ref[TH - H * s:2 * TH - H * s, :]
            blk = piece if blk is None else jnp.where(lane_group == u, piece, blk)
        blocks.append(blk.astype(BF16))
    mt = jnp.concatenate(blocks, axis=1)

    expo = (groups_per_block - 1 - lane_group).astype(F32)
    wr, wi = _cis(lr * expo, li * expo)
    hop_r, hop_i = _cpow2(abar_r, abar_i, groups_per_block)
    w1_r, w1_i = [], []
    for v in range(TH // LANES):
        w1_r.insert(0, wr * bb_r - wi * bb_i)
        w1_i.insert(0, wr * bb_i + wi * bb_r)
        wr, wi = _cmul(wr, wi, hop_r, hop_i)
    w1t = jnp.concatenate([jnp.concatenate(w1_r, axis=1),
                           jnp.concatenate(w1_i, axis=1)], axis=0).astype(BF16)

    e = e.reshape(TH, e.shape[-1])
    y = jnp.dot(mt, e, preferred_element_type=F32)
    st = jnp.dot(w1t, e, preferred_element_type=F32)
    sr, si = st[:P], st[P:]
    pos = lax.broadcasted_iota(jnp.int32, sr.shape, 1) % chunks_per_seq

    def shifted(a, shift):
        return jnp.where(pos >= shift, pltpu.roll(a, shift, 1), 0.0)

    qr, qi = _cpow2(abar_r, abar_i, T)
    shift = 1
    while shift < chunks_per_seq:
        srs, sis = shifted(sr, shift), shifted(si, shift)
        sr, si = sr + qr * srs - qi * sis, si + qr * sis + qi * srs
        qr, qi = qr * qr - qi * qi, 2.0 * qr * qi
        shift *= 2
    x_prev = jnp.concatenate([shifted(sr, 1), shifted(si, 1)], axis=0).astype(BF16)
    y = y + jnp.dot(c_pow1.astype(BF16), x_prev, preferred_element_type=F32)
    return _gelu_tanh(y)


def _ssm_call(arow, acol, ldt, bt, cab, dcol, e, chunks_per_seq):
    T, _, NC = e.shape
    G = arow.shape[0]
    gps = SSM_GROUPS_PER_STEP
    assert G % gps == 0
    grp = lambda a: pl.BlockSpec((gps,) + a.shape[1:], lambda g: (g,) + (0,) * (a.ndim - 1))
    channels = pl.BlockSpec((T, gps * SSM_GROUP_CH, NC), lambda g: (0, g, 0))
    return pl.pallas_call(
        functools.partial(_ssm_kernel, chunks_per_seq=chunks_per_seq),
        grid=(G // gps,),
        in_specs=[grp(a) for a in (arow, acol, ldt, bt, cab, dcol)] + [channels],
        out_specs=channels,
        out_shape=jax.ShapeDtypeStruct((T, SSM_WIDTH, NC), F32),
        scratch_shapes=[pltpu.VMEM((gps, 2 * T * SSM_GROUP_CH, LANES), F32)],
        compiler_params=pltpu.CompilerParams(
            dimension_semantics=("arbitrary",), vmem_limit_bytes=VMEM_LIMIT),
        name="ssm",
    )(arow, acol, ldt, bt, cab, dcol, e)


def _ssm_param_layouts(a_re, a_im, log_dt, b_re, b_im, c_re, c_im, d_skip):
    G = a_re.shape[0]
    arow = jnp.stack([jnp.concatenate([a_re, a_re], -1), jnp.concatenate([a_im, a_im], -1)], 1)
    acol = jnp.stack([a_re, a_im], -1)
    reps = LANES // SSM_GROUP_CH
    bt = jnp.stack([jnp.tile(b_re, (1, 1, reps)), jnp.tile(b_im, (1, 1, reps))], 1)
    cab = jnp.stack([jnp.concatenate([c_re, -c_im], -1), jnp.concatenate([-c_im, -c_re], -1)], 1)
    return (arow.astype(F32), acol.astype(F32), log_dt.reshape(G, 1, 1).astype(F32),
            bt.astype(F32), cab.astype(F32), d_skip.reshape(G, SSM_GROUP_CH, 1).astype(F32))


def _tail_kernel(*refs):
    first = (pl.program_id(0) == 0) & (pl.program_id(1) == 0)

    @pl.when(first)
    def _():
        _tail_body(*refs, streaming=True)

    @pl.when(jnp.logical_not(first))
    def _():
        _tail_body(*refs, streaming=False)


def _tail_body(h1_ref, attn_ref, y_ref, p_ref, wglu_ref, bglu_ref, ga_ref, gs_ref,
               wo_ref, g2_ref, w1_hbm, w3_hbm, w2_hbm, gp_ref, wpg_ref,
               wpp_ref, gf_ref, o_ref, w1_ref, w3_ref, w2_ref, sem, *, streaming):
    def copies(c):
        sl = pl.ds(c * FF_CHUNK, FF_CHUNK)
        return (pltpu.make_async_copy(w1_hbm.at[:, sl], w1_ref.at[:, sl], sem.at[c, 0]),
                pltpu.make_async_copy(w3_hbm.at[:, sl], w3_ref.at[:, sl], sem.at[c, 1]),
                pltpu.make_async_copy(w2_hbm.at[sl, :], w2_ref.at[sl, :], sem.at[c, 2]))

    resident = _resident_weights(w1_ref, w3_ref, w2_ref)
    if streaming:
        for c in range(D_FF // FF_CHUNK):
            for cp in copies(c):
                cp.start()

        def weights(c):
            for cp in copies(c):
                cp.wait()
            return resident(c)
    else:
        weights = resident

    y = y_ref[...]
    glu = y * _sigmoid(jnp.dot(y.astype(BF16), wglu_ref[...], preferred_element_type=F32)
                       + bglu_ref[...])
    an = _rms(attn_ref[...], ga_ref[...]).astype(BF16)
    sn = _rms(glu, gs_ref[...]).astype(BF16)
    h = (h1_ref[...] + jnp.dot(an, wo_ref[:ATTN_WIDTH, :], preferred_element_type=F32)
         + jnp.dot(sn, wo_ref[ATTN_WIDTH:, :], preferred_element_type=F32))
    h = h + 0.5 * _swiglu(_rms(h, g2_ref[...]).astype(BF16), weights)
    gate = _sigmoid(jnp.dot(_rms(h, gp_ref[...]).astype(BF16), wpg_ref[...],
                            preferred_element_type=F32))
    h = h + gate * jnp.dot(p_ref[...].astype(BF16), wpp_ref[...], preferred_element_type=F32)
    o_ref[...] = _rms(h, gf_ref[...])


def _tail_call(h1, attn, y, p, *consts):
    B, L, D = h1.shape
    tm = TOKEN_TILE
    tile = lambda w: pl.BlockSpec((None, tm, w), lambda b, i: (b, i, 0))
    ffn = (6, 7, 8)
    return pl.pallas_call(
        _tail_kernel,
        grid=(B, L // tm),
        in_specs=[tile(D), tile(ATTN_WIDTH), tile(SSM_WIDTH), tile(PLE_DIM)]
                 + [pl.BlockSpec(memory_space=pl.ANY) if i in ffn else _const_spec(c.shape)
                    for i, c in enumerate(consts)],
        out_specs=tile(D),
        out_shape=jax.ShapeDtypeStruct((B, L, D), F32),
        scratch_shapes=[pltpu.VMEM(consts[i].shape, BF16) for i in ffn]
                       + [pltpu.SemaphoreType.DMA((D_FF // FF_CHUNK, 3))],
        compiler_params=pltpu.CompilerParams(
            dimension_semantics=("arbitrary", "arbitrary"),
            vmem_limit_bytes=VMEM_LIMIT),
        name="tail",
    )(h1, attn, y, p, *consts)


def kernel(x, p, g_ffn1, w1_a, w3_a, w2_a, g_mix, w_in, b_f, a_re, a_im, log_dt, b_re, b_im, c_re, c_im, d_skip, w_glu, b_glu, g_attn_out, g_ssm_out, w_out, g_ffn2, w1_b, w3_b, w2_b, g_ple, w_ple_gate, w_ple_proj, g_final):
    B, L, D = x.shape
    assert D == D_MODEL and L % ATTN_TILE == 0 and L % TOKEN_TILE == 0 and L % SSM_CHUNK == 0
    assert g_ffn1.shape[0] == 1, "single layer"
    assert TOKEN_TILE == ATTN_TILE, "decay bias rows are relative to the kv tile start"
    row = lambda g: g.reshape(1, -1).astype(F32)
    bf = lambda w: w.astype(BF16)
    w_in16 = bf(w_in[0])
    assert w_in16.shape[1] == 3 * ATTN_WIDTH + ATTN_HEADS + SSM_WIDTH

    (h1, q, kt, v, s_in, ct, k_norm2, c_next), (w1_b16, w3_b16, w2_b16, w_out16, w_gate16) = _head_call(
        x, row(g_ffn1[0]), w1_a[0], w3_a[0], w2_a[0], row(g_mix[0]),
        w_in16, w_in16[:, 3 * ATTN_WIDTH + ATTN_HEADS:],
        b_f[0].reshape(ATTN_HEADS, 1).astype(F32),
        later=tuple(w.astype(F32) for w in (w1_b[0], w3_b[0], w2_b[0], w_out[0], w_ple_gate[0])))

    T = SSM_CHUNK
    chunks_per_seq = L // T
    n_chunks = B * chunks_per_seq
    e = s_in.reshape(n_chunks, T, SSM_WIDTH).transpose(1, 2, 0)
    f = _ssm_call(*_ssm_param_layouts(a_re[0], a_im[0], log_dt[0], b_re[0], b_im[0],
                                      c_re[0], c_im[0], d_skip[0]),
                  e, chunks_per_seq)
    y = f.transpose(2, 0, 1).reshape(B, L, SSM_WIDTH)

    attn = _attn_call(q, kt, v, ct, k_norm2, c_next)

    return _tail_call(
        h1, attn, y, p[0],
        bf(w_glu[0]), row(b_glu[0]), row(g_attn_out[0]), row(g_ssm_out[0]),
        w_out16, row(g_ffn2[0]),
        w1_b16, w3_b16, w2_b16, row(g_ple[0]), w_gate16,
        bf(w_ple_proj[0]), row(g_final))
```
